```python
import math
import jax, jax.numpy as jnp
from jax import lax
import numpy as np

D_MODEL = 1024
BATCH = 32
SEQ = 256
DEPTH = 2
DEC_BATCH = 4
DEC_SEQ = 4096
PAST_LEN = 256

GRID_W = 64
N_EVEN = (DEPTH + 1) // 2
N_ODD = DEPTH // 2
ADA_CHUNKS = 6
EPS = 1e-6
HY_WIDTH = D_MODEL // 2
HY_ORDER = 2
HY_BANDS = 16
HY_EMB = 1 + 2 * HY_BANDS
HY_FFN = 64
S5_WIDTH = D_MODEL // 2
S5_CH = 16
S5_GROUPS = S5_WIDTH // S5_CH
S5_P = 64
ML_WIDTH = D_MODEL
ML_HEADS = 8
ML_DH = ML_WIDTH // ML_HEADS
ML_CHUNK = 128
PK_HEADS = 8
PK_DQ = 256
PK_HALF = PK_DQ // 2
N_KEYS = 128
N_EXPERTS = N_KEYS * N_KEYS
PK_TOPK = 16
PEER_BLOCK = 128

kernel_name = 'hyena_s5_mlstm_peer_diffusion_step'


def _f(a):
    return a.astype(jnp.float32)


def rmsnorm(x, g):
    x32 = _f(x)
    y = x32 * lax.rsqrt(jnp.mean(x32 * x32, axis=-1, keepdims=True) + EPS)
    return (y * _f(g)).astype(x.dtype)


def short_conv(x, w, b):
    xp = jnp.pad(x, ((0, 0), (1, 1), (0, 0)))
    return xp[:, :-2] * w[0] + xp[:, 1:-1] * w[1] + xp[:, 2:] * w[2] + b


def grid_pos_embed(n_tok, dtype):
    rows = n_tok // GRID_W
    quarter = D_MODEL // 4
    omega = 1.0 / (10000.0 ** (jnp.arange(quarter, dtype=jnp.float32) / quarter))
    def emb1d(pos):
        ang = pos.astype(jnp.float32)[:, None] * omega[None]
        return jnp.concatenate([jnp.sin(ang), jnp.cos(ang)], axis=-1)
    er = emb1d(jnp.arange(rows))
    ec = emb1d(jnp.arange(GRID_W))
    half = D_MODEL // 2
    pe = jnp.concatenate([jnp.broadcast_to(er[:, None], (rows, GRID_W, half)),
                          jnp.broadcast_to(ec[None], (rows, GRID_W, half))], axis=-1)
    return pe.reshape(rows * GRID_W, D_MODEL).astype(dtype)


def hyena_filters(n_tok, w1, b1, w2, b2, w3, freq, decay):
    pos = jnp.arange(n_tok, dtype=jnp.float32)
    t = pos / n_tok
    bands = jnp.linspace(1e-4, HY_BANDS - 1, HY_BANDS, dtype=jnp.float32)
    ang = 2.0 * math.pi * t[:, None] * bands[None]
    z = jnp.concatenate([t[:, None], jnp.cos(ang), jnp.sin(ang)], axis=-1)
    fr = _f(freq)
    h = jnp.sin(fr * (z @ _f(w1) + _f(b1)))
    h = jnp.sin(fr * (h @ _f(w2) + _f(b2)))
    h = (h @ _f(w3)) * jnp.exp(-t[:, None] * jnp.abs(_f(decay)))
    h = h.reshape(n_tok, HY_ORDER, 2, HY_WIDTH)
    fwd, bwd = h[:, :, 0], h[:, :, 1]
    kern = jnp.concatenate([fwd, jnp.zeros_like(fwd[:1]), bwd[1:][::-1]], axis=0)
    kern = kern * lax.rsqrt(jnp.sum(kern * kern, axis=0, keepdims=True) + EPS)
    return jnp.fft.rfft(kern, axis=0)


def hyena_mix(u, kf, bias):
    n_tok = u.shape[1]
    v, g1, g2 = jnp.split(_f(u), 3, axis=-1)
    gates = (g1, g2)
    z = v
    for o in range(HY_ORDER):
        zf = jnp.fft.rfft(z, n=2 * n_tok, axis=1)
        conv = jnp.fft.irfft(zf * kf[None, :, o], n=2 * n_tok, axis=1)[:, :n_tok]
        z = gates[o] * (conv + bias[o] * z)
    return z


def _lin_rec(e1, e2):
    a1, b1 = e1
    a2, b2 = e2
    return a1 * a2, a2 * b1 + b2


def s5_scan(u, lam_bar, b_bar, c_mat, h0):
    bu = jnp.einsum('gpj,blgj->blgp', b_bar, u.astype(jnp.complex64))
    bu = bu.at[:, 0].add(lam_bar[None] * h0)
    a = jnp.broadcast_to(lam_bar, bu.shape)
    _, hs = lax.associative_scan(_lin_rec, (a, bu), axis=1)
    y = jnp.einsum('gjp,blgp->blgj', c_mat, hs).real
    return y, hs[:, -1]


def s5_mix(u, h0_re, h0_im, a_re, a_im, b_re, b_im, c_re, c_im, log_step, d_skip, glu_w, glu_b):
    bsz, n_tok, _ = u.shape
    u = _f(u).reshape(bsz, n_tok, S5_GROUPS, S5_CH)
    lam = lax.complex(_f(a_re), _f(a_im))
    lam_bar = jnp.exp(lam * jnp.exp(_f(log_step))[..., None])
    b_bar = ((lam_bar - 1.0) / lam)[..., None] * lax.complex(_f(b_re), _f(b_im))
    c_mat = lax.complex(_f(c_re), _f(c_im))
    h0 = lax.complex(_f(h0_re), _f(h0_im))
    y_f, hf = s5_scan(u, lam_bar[0], b_bar[0], c_mat[0], h0[:, 0])
    y_b, hb = s5_scan(u[:, ::-1], lam_bar[1], b_bar[1], c_mat[1], h0[:, 1])
    u_flat = u.reshape(bsz, n_tok, S5_WIDTH)
    y = (y_f + y_b[:, ::-1]).reshape(bsz, n_tok, S5_WIDTH) + _f(d_skip) * u_flat
    y = jax.nn.gelu(y)
    y = y * jax.nn.sigmoid(y @ _f(glu_w) + _f(glu_b))
    h_fin = jnp.stack([hf, hb], axis=1)
    return y, h_fin.real, h_fin.imag


def mlstm_scan(q, k, v, ig, lf, c0, n0, m0):
    bsz, nh, n_tok, dh = q.shape
    nc = n_tok // ML_CHUNK
    def chunks(a):
        return jnp.moveaxis(a.reshape(bsz, nh, nc, ML_CHUNK, *a.shape[3:]), 2, 0)
    causal = jnp.tril(jnp.ones((ML_CHUNK, ML_CHUNK), dtype=bool))
    def step(carry, xs):
        cm, nv, m = carry
        qc, kc, vc, ic, fc = xs
        b = jnp.cumsum(fc, axis=-1)
        a = b + m[..., None]
        dmat = jnp.where(causal, b[..., :, None] - b[..., None, :] + ic[..., None, :], -jnp.inf)
        mq = jnp.maximum(a, jnp.max(dmat, axis=-1))
        w_intra = jnp.exp(dmat - mq[..., None])
        w_inter = jnp.exp(a - mq)
        s = jnp.einsum('bhtd,bhsd->bhts', qc, kc) * w_intra
        num = jnp.einsum('bhts,bhsv->bhtv', s, vc) + w_inter[..., None] * jnp.einsum('bhtk,bhkv->bhtv', qc, cm)
        den = jnp.sum(s, axis=-1) + w_inter * jnp.einsum('bhtk,bhk->bht', qc, nv)
        h = num / jnp.maximum(jnp.abs(den), jnp.exp(-mq))[..., None]
        b_last = b[..., -1]
        g = b_last[..., None] - b + ic
        m_new = jnp.maximum(b_last + m, jnp.max(g, axis=-1))
        w_s = jnp.exp(g - m_new[..., None])
        keep = jnp.exp(b_last + m - m_new)
        cm_new = keep[..., None, None] * cm + jnp.einsum('bhs,bhsk,bhsv->bhkv', w_s, kc, vc)
        nv_new = keep[..., None] * nv + jnp.einsum('bhs,bhsk->bhk', w_s, kc)
        return (cm_new, nv_new, m_new), h
    (cm, nv, m), hs = lax.scan(step, (c0, n0, m0), (chunks(q), chunks(k), chunks(v), chunks(ig), chunks(lf)))
    h = jnp.moveaxis(hs, 0, 2).reshape(bsz, nh, n_tok, dh)
    return h, cm, nv, m


def mlstm_mix(hn, c0, n0, m0, w_in, gate_b, conv_w, conv_b, norm_g, w_out):
    bsz, n_tok, _ = hn.shape
    proj = _f(hn @ w_in)
    qk_raw, v, o, gates = jnp.split(proj, [2 * ML_WIDTH, 3 * ML_WIDTH, 4 * ML_WIDTH], axis=-1)
    qk = jax.nn.silu(short_conv(qk_raw, _f(conv_w), _f(conv_b)))
    q, k = jnp.split(qk, 2, axis=-1)
    def heads(a):
        return a.reshape(bsz, n_tok, ML_HEADS, ML_DH).transpose(0, 2, 1, 3)
    q = heads(q) * (ML_DH ** -0.5)
    k = heads(k)
    vh = heads(v)
    gates = (gates.reshape(bsz, n_tok, 4, ML_HEADS) + _f(gate_b)).transpose(2, 0, 3, 1)
    ig_f, ig_b = gates[0], gates[1]
    lf_f, lf_b = jax.nn.log_sigmoid(gates[2]), jax.nn.log_sigmoid(gates[3])
    c0, n0, m0 = _f(c0), _f(n0), _f(m0)
    h_f, cf, nf, mf = mlstm_scan(q, k, vh, ig_f, lf_f, c0[:, 0], n0[:, 0], m0[:, 0])
    def rev(a):
        return jnp.flip(a, axis=2)
    h_b, cb, nb, mb = mlstm_scan(rev(q), rev(k), rev(vh), rev(ig_b), rev(lf_b), c0[:, 1], n0[:, 1], m0[:, 1])
    h = h_f + rev(h_b)
    h = h * lax.rsqrt(jnp.mean(h * h, axis=-1, keepdims=True) + EPS)
    h = h.transpose(0, 2, 1, 3).reshape(bsz, n_tok, ML_WIDTH) * _f(norm_g)
    out = (h * jax.nn.silu(o)).astype(hn.dtype) @ w_out
    return out, jnp.stack([cf, cb], 1), jnp.stack([nf, nb], 1), jnp.stack([mf, mb], 1)


def peer(x, w_q, keys, u_tab, v_tab):
    bsz, n_tok, _ = x.shape
    n = bsz * n_tok
    xt = x.reshape(n, D_MODEL)
    q = _f(xt @ w_q).reshape(n, PK_HEADS, 2, PK_HALF)
    s = jnp.einsum('thcd,hcnd->thcn', q, _f(keys))
    sv, si = lax.top_k(s, PK_TOPK)
    cand = (sv[:, :, 0, :, None] + sv[:, :, 1, None, :]).reshape(n, PK_HEADS, PK_TOPK * PK_TOPK)
    best, bi = lax.top_k(cand, PK_TOPK)
    i1 = jnp.take_along_axis(si[:, :, 0], bi // PK_TOPK, axis=-1)
    i2 = jnp.take_along_axis(si[:, :, 1], bi % PK_TOPK, axis=-1)
    nblk = n // PEER_BLOCK
    eidx = (i1 * N_KEYS + i2).reshape(nblk, PEER_BLOCK, PK_HEADS * PK_TOPK)
    gate = jax.nn.softmax(best, axis=-1).reshape(nblk, PEER_BLOCK, PK_HEADS * PK_TOPK)
    xb = xt.reshape(nblk, PEER_BLOCK, D_MODEL)
    def block(args):
        xi, ei, gi = args
        act = jax.nn.gelu(_f(jnp.einsum('td,ted->te', xi, u_tab[ei]))) * gi
        return jnp.einsum('te,ted->td', act.astype(xi.dtype), v_tab[ei])
    out = lax.map(block, (xb, eidx, gate))
    return out.reshape(bsz, n_tok, D_MODEL).astype(x.dtype)


def even_mixer(h, s5_re0, s5_im0, p, i):
    n_tok = h.shape[1]
    proj = h @ p['ev_w_in'][i]
    hy_in = short_conv(proj[..., :3 * HY_WIDTH], p['hy_conv_w'][i], p['hy_conv_b'][i])
    kf = hyena_filters(n_tok, p['hy_w1'][i], p['hy_b1'][i], p['hy_w2'][i], p['hy_b2'][i],
                       p['hy_w3'][i], p['hy_freq'][i], p['hy_decay'][i])
    hy_out = hyena_mix(hy_in, kf, _f(p['hy_bias'][i]))
    s5_out, s_re, s_im = s5_mix(proj[..., 3 * HY_WIDTH:], s5_re0, s5_im0,
                                p['s5_a_re'][i], p['s5_a_im'][i], p['s5_b_re'][i], p['s5_b_im'][i],
                                p['s5_c_re'][i], p['s5_c_im'][i], p['s5_log_step'][i], p['s5_d'][i],
                                p['s5_glu_w'][i], p['s5_glu_b'][i])
    out = jnp.concatenate([hy_out, s5_out], axis=-1).astype(h.dtype) @ p['ev_w_out'][i]
    return out, s_re, s_im


def odd_mixer(h, c0, n0, m0, p, i):
    return mlstm_mix(h, c0, n0, m0, p['od_w_in'][i], p['od_gate_b'][i], p['ml_conv_w'][i],
                     p['ml_conv_b'][i], p['ml_norm_g'][i], p['od_w_out'][i])


def trunk(x, cond, s5_re0, s5_im0, ml_c0, ml_n0, ml_m0, p):
    s5_re, s5_im, ml_c, ml_n, ml_m = [], [], [], [], []
    for l in range(DEPTH):
        mod = jax.nn.silu(_f(cond)) @ _f(p['ada_w'][l]) + _f(p['ada_b'][l])
        sh1, sc1, g1, sh2, sc2, g2 = jnp.split(mod[:, None, :].astype(x.dtype), ADA_CHUNKS, axis=-1)
        h = rmsnorm(x, p['norm_g'][l, 0]) * (1 + sc1) + sh1
        i = l // 2
        if l % 2 == 0:
            y, sr, si = even_mixer(h, s5_re0[:, i], s5_im0[:, i], p, i)
            s5_re.append(sr)
            s5_im.append(si)
        else:
            y, cm, nv, m = odd_mixer(h, ml_c0[:, i], ml_n0[:, i], ml_m0[:, i], p, i)
            ml_c.append(cm)
            ml_n.append(nv)
            ml_m.append(m)
        x = x + g1 * y
        h = rmsnorm(x, p['norm_g'][l, 1]) * (1 + sc2) + sh2
        x = x + g2 * peer(h, p['pk_w_q'][l], p['pk_keys'][l], p['pk_u'][l], p['pk_v'][l])
    y = rmsnorm(x, p['final_g'])
    return (y, jnp.stack(s5_re, 1), jnp.stack(s5_im, 1), jnp.stack(ml_c, 1), jnp.stack(ml_n, 1), jnp.stack(ml_m, 1))


def setup_inputs(seed: int = 0) -> dict:
    key = jax.random.key(seed)
    ks = iter(list(jax.random.split(key, 64)))
    f32 = jnp.float32
    def nrm(shape, scale):
        return jax.random.normal(next(ks), shape, f32) * scale
    def unif(shape, lo, hi):
        return jax.random.uniform(next(ks), shape, f32, lo, hi)
    d = D_MODEL
    ne, no = N_EVEN, N_ODD
    hy_decay0 = jnp.linspace(math.log(100.0) / 1.5, math.log(100.0) / 0.3, 4 * HY_WIDTH, dtype=f32)
    fgate0 = jnp.linspace(3.0, 6.0, ML_HEADS, dtype=f32)
    return {
        'x_prompt': nrm((BATCH, SEQ, d), 1.0),
        'x_sample': nrm((DEC_BATCH, DEC_SEQ, d), 1.0),
        'state_s5_re': nrm((DEC_BATCH, ne, 2, S5_GROUPS, S5_P), 0.1),
        'state_s5_im': nrm((DEC_BATCH, ne, 2, S5_GROUPS, S5_P), 0.1),
        'state_mlstm_C': nrm((DEC_BATCH, no, 2, ML_HEADS, ML_DH, ML_DH), 0.3),
        'state_mlstm_n': nrm((DEC_BATCH, no, 2, ML_HEADS, ML_DH), 0.3),
        'state_mlstm_m': unif((DEC_BATCH, no, 2, ML_HEADS), 0.0, 3.0),
        'c': nrm((DEC_BATCH, d), 1.0),
        'c_ctx': nrm((d,), 1.0),
        'norm_g': 1.0 + nrm((DEPTH, 2, d), 0.02),
        'ada_w': nrm((DEPTH, d, ADA_CHUNKS * d), 0.5 * d ** -0.5),
        'ada_b': nrm((DEPTH, ADA_CHUNKS * d), 0.02),
        'final_g': 1.0 + nrm((d,), 0.02),
        'ev_w_in': nrm((ne, d, 3 * HY_WIDTH + S5_WIDTH), d ** -0.5),
        'hy_conv_w': nrm((ne, 3, 3 * HY_WIDTH), 3 ** -0.5),
        'hy_conv_b': nrm((ne, 3 * HY_WIDTH), 0.02),
        'hy_w1': nrm((ne, HY_EMB, HY_FFN), HY_EMB ** -0.5),
        'hy_b1': nrm((ne, HY_FFN), 0.1),
        'hy_w2': nrm((ne, HY_FFN, HY_FFN), HY_FFN ** -0.5),
        'hy_b2': nrm((ne, HY_FFN), 0.1),
        'hy_w3': nrm((ne, HY_FFN, 4 * HY_WIDTH), HY_FFN ** -0.5),
        'hy_freq': 1.0 + nrm((ne, HY_FFN), 0.1),
        'hy_decay': hy_decay0[None] + nrm((ne, 4 * HY_WIDTH), 0.1),
        'hy_bias': nrm((ne, HY_ORDER, HY_WIDTH), 0.1),
        's5_a_re': -0.5 + nrm((ne, 2, S5_GROUPS, S5_P), 0.01),
        's5_a_im': math.pi * jnp.arange(S5_P, dtype=f32) + nrm((ne, 2, S5_GROUPS, S5_P), 0.01),
        's5_b_re': nrm((ne, 2, S5_GROUPS, S5_P, S5_CH), (2 * S5_CH) ** -0.5),
        's5_b_im': nrm((ne, 2, S5_GROUPS, S5_P, S5_CH), (2 * S5_CH) ** -0.5),
        's5_c_re': nrm((ne, 2, S5_GROUPS, S5_CH, S5_P), S5_P ** -0.5),
        's5_c_im': nrm((ne, 2, S5_GROUPS, S5_CH, S5_P), S5_P ** -0.5),
        's5_log_step': unif((ne, 2, S5_GROUPS), math.log(1e-3), math.log(1e-1)),
        's5_d': nrm((ne, S5_WIDTH), 1.0),
        's5_glu_w': nrm((ne, S5_WIDTH, S5_WIDTH), S5_WIDTH ** -0.5),
        's5_glu_b': nrm((ne, S5_WIDTH), 0.02),
        'ev_w_out': nrm((ne, HY_WIDTH + S5_WIDTH, d), (HY_WIDTH + S5_WIDTH) ** -0.5),
        'od_w_in': nrm((no, d, 4 * ML_WIDTH + 4 * ML_HEADS), d ** -0.5),
        'od_gate_b': jnp.concatenate([nrm((no, 2, ML_HEADS), 0.1),
                                      fgate0[None, None] + nrm((no, 2, ML_HEADS), 0.1)], axis=1),
        'ml_conv_w': nrm((no, 3, 2 * ML_WIDTH), 3 ** -0.5),
        'ml_conv_b': nrm((no, 2 * ML_WIDTH), 0.02),
        'ml_norm_g': 1.0 + nrm((no, ML_WIDTH), 0.02),
        'od_w_out': nrm((no, ML_WIDTH, d), ML_WIDTH ** -0.5),
        'pk_w_q': nrm((DEPTH, d, PK_HEADS * PK_DQ), d ** -0.5),
        'pk_keys': nrm((DEPTH, PK_HEADS, 2, N_KEYS, PK_HALF), PK_HALF ** -0.5),
        'pk_u': nrm((DEPTH, N_EXPERTS, d), d ** -0.5),
        'pk_v': nrm((DEPTH, N_EXPERTS, d), PK_HEADS ** -0.5),
    }


def reference(x_prompt, x_sample, state_s5_re, state_s5_im, state_mlstm_C, state_mlstm_n, state_mlstm_m,
              c, c_ctx, norm_g, ada_w, ada_b, final_g, ev_w_in, hy_conv_w, hy_conv_b, hy_w1, hy_b1, hy_w2,
              hy_b2, hy_w3, hy_freq, hy_decay, hy_bias, s5_a_re, s5_a_im, s5_b_re, s5_b_im, s5_c_re, s5_c_im,
              s5_log_step, s5_d, s5_glu_w, s5_glu_b, ev_w_out, od_w_in, od_gate_b, ml_conv_w, ml_conv_b,
              ml_norm_g, od_w_out, pk_w_q, pk_keys, pk_u, pk_v):
    p = dict(norm_g=norm_g, ada_w=ada_w, ada_b=ada_b, final_g=final_g, ev_w_in=ev_w_in,
             hy_conv_w=hy_conv_w, hy_conv_b=hy_conv_b, hy_w1=hy_w1, hy_b1=hy_b1, hy_w2=hy_w2, hy_b2=hy_b2,
             hy_w3=hy_w3, hy_freq=hy_freq, hy_decay=hy_decay, hy_bias=hy_bias, s5_a_re=s5_a_re,
             s5_a_im=s5_a_im, s5_b_re=s5_b_re, s5_b_im=s5_b_im, s5_c_re=s5_c_re, s5_c_im=s5_c_im,
             s5_log_step=s5_log_step, s5_d=s5_d, s5_glu_w=s5_glu_w, s5_glu_b=s5_glu_b, ev_w_out=ev_w_out,
             od_w_in=od_w_in, od_gate_b=od_gate_b, ml_conv_w=ml_conv_w, ml_conv_b=ml_conv_b,
             ml_norm_g=ml_norm_g, od_w_out=od_w_out, pk_w_q=pk_w_q, pk_keys=pk_keys, pk_u=pk_u, pk_v=pk_v)
    f32 = jnp.float32
    nb = x_prompt.shape[0]
    z_s5 = jnp.zeros((nb, N_EVEN, 2, S5_GROUPS, S5_P), f32)
    z_c = jnp.zeros((nb, N_ODD, 2, ML_HEADS, ML_DH, ML_DH), f32)
    z_n = jnp.zeros((nb, N_ODD, 2, ML_HEADS, ML_DH), f32)
    z_m = jnp.zeros((nb, N_ODD, 2, ML_HEADS), f32)
    y_prompt, new_s5_re, new_s5_im, new_mlstm_C, new_mlstm_n, new_mlstm_m = trunk(
        x_prompt, c_ctx[None], z_s5, z_s5, z_c, z_n, z_m, p)
    x_lat = x_sample + grid_pos_embed(x_sample.shape[1], x_sample.dtype)
    y_sample = trunk(x_lat, c, state_s5_re, state_s5_im, state_mlstm_C, state_mlstm_n, state_mlstm_m, p)[0]
    return (y_prompt, y_sample, new_s5_re, new_s5_im, new_mlstm_C, new_mlstm_n, new_mlstm_m)
```

```python
import functools
import math

import jax
import jax.numpy as jnp
from jax import lax
from jax.experimental import pallas as pl
from jax.experimental.pallas import tpu as pltpu

F32 = jnp.float32
BF16 = jnp.bfloat16
EPS = 1e-6
HIGHEST = lax.Precision.HIGHEST
V7X_VMEM_LIMIT_BYTES = 56 * 1024 * 1024
LANES = 128
ML_CHUNK = 128
PK_TOPK = 16
GRID_W = 64
NEG_INF = float("-inf")


def _cp(*sem):
    return pltpu.CompilerParams(dimension_semantics=sem, vmem_limit_bytes=V7X_VMEM_LIMIT_BYTES)


def _dot(a, b, **kw):
    return jnp.dot(a, b, preferred_element_type=F32, **kw)


def _dot_nt(a, b):
    return lax.dot_general(a, b, (((1,), (1,)), ((), ())), preferred_element_type=F32)


def _silu(x):
    return x * jax.nn.sigmoid(x)


def _ada_kernel(c_ref, w_ref, b_ref, o_ref):
    o_ref[...] = _dot(_silu(c_ref[...]), w_ref[...], precision=HIGHEST) + b_ref[...]


def ada_mod(cond8, w, b):
    d, no = w.shape
    tn = 1536 if no % 1536 == 0 else no
    return pl.pallas_call(
        _ada_kernel, grid=(no // tn,),
        in_specs=[pl.BlockSpec((8, d), lambda j: (0, 0)), pl.BlockSpec((d, tn), lambda j: (0, j)),
                  pl.BlockSpec((1, tn), lambda j: (0, j))],
        out_specs=pl.BlockSpec((8, tn), lambda j: (0, j)),
        out_shape=jax.ShapeDtypeStruct((8, no), F32), compiler_params=_cp("parallel"), name="ada_mod",
    )(cond8, w, b.reshape(1, no))


def _normmod(x, g, sc, sh):
    y = x * lax.rsqrt(jnp.mean(x * x, axis=-1, keepdims=True) + EPS)
    return (y * g) * (1.0 + sc) + sh


def _mod_spec(d, tm, rows_per_mod, nd=2):
    if nd == 2:
        return pl.BlockSpec((1, 1, d), lambda i, j: ((i * tm) // rows_per_mod, 0, 0))
    return pl.BlockSpec((1, 1, d), lambda i: ((i * tm) // rows_per_mod, 0, 0))


def _nm_matmul_kernel(x_ref, g_ref, sc_ref, sh_ref, w_ref, o_ref, h_ref):
    @pl.when(pl.program_id(1) == 0)
    def _():
        h_ref[...] = _normmod(x_ref[...], g_ref[...], sc_ref[0], sh_ref[0]).astype(BF16)
    o_ref[...] = _dot(h_ref[...], w_ref[...])


def normmod_matmul(x, g, sc, sh, w_bf16, rows_per_mod, tm, tn):
    n, d = x.shape
    no = w_bf16.shape[1]
    return pl.pallas_call(
        _nm_matmul_kernel, grid=(n // tm, no // tn),
        in_specs=[pl.BlockSpec((tm, d), lambda i, j: (i, 0)), pl.BlockSpec((1, d), lambda i, j: (0, 0)),
                  _mod_spec(d, tm, rows_per_mod), _mod_spec(d, tm, rows_per_mod),
                  pl.BlockSpec((d, tn), lambda i, j: (0, j))],
        out_specs=pl.BlockSpec((tm, tn), lambda i, j: (i, j)),
        out_shape=jax.ShapeDtypeStruct((n, no), F32),
        scratch_shapes=[pltpu.VMEM((tm, d), BF16)],
        compiler_params=_cp("parallel", "arbitrary"), name="normmod_matmul",
    )(x, g.reshape(1, d), sc, sh, w_bf16)


def _nm_kernel(x_ref, g_ref, sc_ref, sh_ref, o_ref):
    o_ref[...] = _normmod(x_ref[...], g_ref[...], sc_ref[0], sh_ref[0]).astype(o_ref.dtype)


def normmod(x, g, sc, sh, rows_per_mod, tm):
    n, d = x.shape
    return pl.pallas_call(
        _nm_kernel, grid=(n // tm,),
        in_specs=[pl.BlockSpec((tm, d), lambda i: (i, 0)), pl.BlockSpec((1, d), lambda i: (0, 0)),
                  _mod_spec(d, tm, rows_per_mod, 1), _mod_spec(d, tm, rows_per_mod, 1)],
        out_specs=pl.BlockSpec((tm, d), lambda i: (i, 0)),
        out_shape=jax.ShapeDtypeStruct((n, d), BF16), compiler_params=_cp("parallel"), name="normmod",
    )(x, g.reshape(1, d), sc, sh)


def _final_norm_kernel(x_ref, g_ref, o_ref):
    x = x_ref[...]
    o_ref[...] = (x * lax.rsqrt(jnp.mean(x * x, axis=-1, keepdims=True) + EPS)) * g_ref[...]


def final_norm(x, g, tm):
    n, d = x.shape
    return pl.pallas_call(
        _final_norm_kernel, grid=(n // tm,),
        in_specs=[pl.BlockSpec((tm, d), lambda i: (i, 0)), pl.BlockSpec((1, d), lambda i: (0, 0))],
        out_specs=pl.BlockSpec((tm, d), lambda i: (i, 0)),
        out_shape=jax.ShapeDtypeStruct((n, d), F32), compiler_params=_cp("parallel"), name="final_norm",
    )(x, g.reshape(1, d))


def _add_rows_kernel(x_ref, p_ref, o_ref):
    o_ref[...] = x_ref[...] + p_ref[...]


def add_pos(x, pe, seq_len, tm):
    n, d = x.shape
    nb = seq_len // tm
    return pl.pallas_call(
        _add_rows_kernel, grid=(n // tm,),
        in_specs=[pl.BlockSpec((tm, d), lambda i: (i, 0)), pl.BlockSpec((tm, d), lambda i: (i % nb, 0))],
        out_specs=pl.BlockSpec((tm, d), lambda i: (i, 0)),
        out_shape=jax.ShapeDtypeStruct((n, d), F32), compiler_params=_cp("parallel"), name="add_pos",
    )(x, pe)


def _sconv_kernel(x_ref, w_ref, b_ref, s_ref, o_ref, *, act):
    x = x_ref[...]
    n_tok = x.shape[0]
    row = lax.broadcasted_iota(jnp.int32, x.shape, 0)
    prev = jnp.where(row == 0, 0.0, pltpu.roll(x, 1, 0))
    nxt = jnp.where(row == n_tok - 1, 0.0, pltpu.roll(x, n_tok - 1, 0))
    y = prev * w_ref[0:1, :] + x * w_ref[1:2, :] + nxt * w_ref[2:3, :] + b_ref[...]
    if act:
        y = _silu(y) * s_ref[...]
    o_ref[...] = y


def short_conv(a, ncols, w, b, scale, seq_len, act, cb=256):
    n = a.shape[0]
    return pl.pallas_call(
        functools.partial(_sconv_kernel, act=act), grid=(n // seq_len, ncols // cb),
        in_specs=[pl.BlockSpec((seq_len, cb), lambda s, j: (s, j)), pl.BlockSpec((3, cb), lambda s, j: (0, j)),
                  pl.BlockSpec((1, cb), lambda s, j: (0, j)), pl.BlockSpec((1, cb), lambda s, j: (0, j))],
        out_specs=pl.BlockSpec((seq_len, cb), lambda s, j: (s, j)),
        out_shape=jax.ShapeDtypeStruct((n, ncols), F32), compiler_params=_cp("parallel", "parallel"),
        name="short_conv",
    )(a, w, b.reshape(1, ncols), scale.reshape(1, ncols))


def dft_tables(n_tok):
    k = jnp.arange(n_tok, dtype=jnp.int32)
    kn = (k[:, None] * k[None, :]) % (2 * n_tok)
    ang = kn.astype(F32) * (math.pi / n_tok)
    cos_t = jnp.cos(ang)
    msin = -jnp.sin(ang)
    alt = jnp.where(k % 2 == 0, 1.0, -1.0).astype(F32)
    a_t = msin.at[0, :].set(alt)
    a_tt = msin.at[:, 0].set(alt)
    return cos_t.astype(BF16), a_t.astype(BF16), a_tt.astype(BF16)


def _hyfilt_kernel(band_ref, w1_ref, b1_ref, w2_ref, b2_ref, w3_ref, fr_ref, dec_ref, h_ref, ss_ref, *,
                   n_tok, tl, hw, nbands):
    i = pl.program_id(0)
    pos = i * tl + lax.broadcasted_iota(jnp.int32, (tl, 1), 0)
    t = pos.astype(F32) / n_tok
    lane = lax.broadcasted_iota(jnp.int32, (tl, LANES), 1)
    ang = 2.0 * math.pi * t * band_ref[...]
    z = jnp.where(lane == 0, t, jnp.where(lane <= nbands, jnp.cos(ang),
                                          jnp.where(lane <= 2 * nbands, jnp.sin(ang), 0.0)))
    fr = fr_ref[...]
    h = jnp.sin(fr * (_dot(z, w1_ref[...], precision=HIGHEST) + b1_ref[...]))
    h = jnp.sin(fr * (_dot(h, w2_ref[...], precision=HIGHEST) + b2_ref[...]))
    h = _dot(h, w3_ref[...], precision=HIGHEST) * jnp.exp(-t * jnp.abs(dec_ref[...]))
    col = lax.broadcasted_iota(jnp.int32, h.shape, 1)
    is_bwd = (col // hw) % 2 == 1
    h = jnp.where(jnp.logical_and(is_bwd, pos == 0), 0.0, h)
    h_ref[...] = h.astype(BF16)

    @pl.when(i == 0)
    def _():
        ss_ref[...] = jnp.zeros_like(ss_ref)
    ss_ref[...] += jnp.sum(h * h, axis=0, keepdims=True)


def hyena_filter_taps(n_tok, w1, b1, w2, b2, w3, freq, decay, hw):
    emb, ffn = w1.shape
    nbands = (emb - 1) // 2
    tl = min(n_tok, 512)
    bands = jnp.linspace(1e-4, nbands - 1, nbands, dtype=F32)
    band_row = jnp.zeros((1, LANES), F32).at[0, 1:1 + nbands].set(bands).at[0, 1 + nbands:1 + 2 * nbands].set(bands)
    w1p = jnp.zeros((LANES, ffn), F32).at[:emb].set(w1)
    nc = w3.shape[1]
    full = lambda shp: pl.BlockSpec(shp, lambda i: (0, 0))
    return pl.pallas_call(
        functools.partial(_hyfilt_kernel, n_tok=n_tok, tl=tl, hw=hw, nbands=nbands), grid=(n_tok // tl,),
        in_specs=[full((1, LANES)), full((LANES, ffn)), full((1, ffn)), full((ffn, ffn)), full((1, ffn)),
                  full((ffn, nc)), full((1, ffn)), full((1, nc))],
        out_specs=[pl.BlockSpec((tl, nc), lambda i: (i, 0)), full((1, nc))],
        out_shape=[jax.ShapeDtypeStruct((n_tok, nc), BF16), jax.ShapeDtypeStruct((1, nc), F32)],
        compiler_params=_cp("arbitrary"), name="hyena_filter_taps",
    )(band_row, w1p, b1.reshape(1, ffn), w2, b2.reshape(1, ffn), w3, freq.reshape(1, ffn), decay.reshape(1, nc))


def _filt_dft_kernel(c_ref, a_ref, h_ref, ss_ref, kr_ref, ki_ref, *, tf, hw):
    i = pl.program_id(1)
    hf = h_ref[:, :hw]
    hb = h_ref[:, hw:]
    cc = c_ref[...]
    aa = a_ref[...]
    zrf, zif, zrb, zib = _dot(cc, hf), _dot(aa, hf), _dot(cc, hb), _dot(aa, hb)
    scale = lax.rsqrt(ss_ref[:, :hw] + ss_ref[:, hw:] + EPS)
    first = (i * tf + lax.broadcasted_iota(jnp.int32, (tf, 1), 0)) == 0
    scale = scale * jnp.where(first, 0.5, 1.0)
    kr_ref[0] = (zrf + zrb) * scale
    ki_ref[0] = jnp.where(first, zif + zib, zif - zib) * scale


def hyena_filter_spectrum(cos_t, a_t, taps, sumsq, hw, tf):
    n_tok = cos_t.shape[0]
    norder = taps.shape[1] // (2 * hw)
    out = jax.ShapeDtypeStruct((norder, n_tok, hw), F32)
    return pl.pallas_call(
        functools.partial(_filt_dft_kernel, tf=tf, hw=hw), grid=(norder, n_tok // tf),
        in_specs=[pl.BlockSpec((tf, n_tok), lambda o, i: (i, 0)), pl.BlockSpec((tf, n_tok), lambda o, i: (i, 0)),
                  pl.BlockSpec((n_tok, 2 * hw), lambda o, i: (0, o)), pl.BlockSpec((1, 2 * hw), lambda o, i: (0, o))],
        out_specs=[pl.BlockSpec((1, tf, hw), lambda o, i: (o, i, 0))] * 2,
        out_shape=[out, out], compiler_params=_cp("parallel", "parallel"), name="hyena_filter_spectrum",
    )(cos_t, a_t, taps, sumsq)


def _hy_fwd_kernel(c_ref, a_ref, z_ref, kr_ref, ki_ref, yr_ref, yi_ref, *, tf):
    i = pl.program_id(0)
    zb = z_ref[...].astype(BF16)
    zr = _dot(c_ref[...], zb)
    zi = _dot(a_ref[...], zb)
    kr = kr_ref[0]
    ki = ki_ref[0]
    first = (i * tf + lax.broadcasted_iota(jnp.int32, (tf, 1), 0)) == 0
    yr_ref[...] = jnp.where(first, zr * kr, zr * kr - zi * ki).astype(BF16)
    yi_ref[...] = jnp.where(first, zi * ki, zr * ki + zi * kr).astype(BF16)


def hyena_fwd(cos_t, a_t, z, zcol, kr, ki, order, nseq, hw, tf):
    n_tok = cos_t.shape[0]
    nf = n_tok // tf
    out = jax.ShapeDtypeStruct((nseq * n_tok, hw), BF16)
    return pl.pallas_call(
        functools.partial(_hy_fwd_kernel, tf=tf), grid=(nf, nseq),
        in_specs=[pl.BlockSpec((tf, n_tok), lambda i, b: (i, 0)), pl.BlockSpec((tf, n_tok), lambda i, b: (i, 0)),
                  pl.BlockSpec((n_tok, hw), lambda i, b: (b, zcol)),
                  pl.BlockSpec((1, tf, hw), lambda i, b: (order, i, 0)),
                  pl.BlockSpec((1, tf, hw), lambda i, b: (order, i, 0))],
        out_specs=[pl.BlockSpec((tf, hw), lambda i, b: (b * nf + i, 0))] * 2,
        out_shape=[out, out], compiler_params=_cp("parallel", "parallel"), name="hyena_fwd",
    )(cos_t, a_t, z, kr, ki)


def _hy_inv_kernel(c_ref, at_ref, yr_ref, yi_ref, zp_ref, gate_ref, bias_ref, o_ref, *, inv_len):
    conv = (_dot(c_ref[...], yr_ref[...]) + _dot(at_ref[...], yi_ref[...])) * inv_len
    o_ref[...] = gate_ref[...] * (conv + bias_ref[...] * zp_ref[...])


def hyena_inv(cos_t, a_tt, yr, yi, zprev, zcol, gates, gcol, bias_row, nseq, hw, tf):
    n_tok = cos_t.shape[0]
    nf = n_tok // tf
    return pl.pallas_call(
        functools.partial(_hy_inv_kernel, inv_len=1.0 / n_tok), grid=(nf, nseq),
        in_specs=[pl.BlockSpec((tf, n_tok), lambda i, b: (i, 0)), pl.BlockSpec((tf, n_tok), lambda i, b: (i, 0)),
                  pl.BlockSpec((n_tok, hw), lambda i, b: (b, 0)), pl.BlockSpec((n_tok, hw), lambda i, b: (b, 0)),
                  pl.BlockSpec((tf, hw), lambda i, b: (b * nf + i, zcol)),
                  pl.BlockSpec((tf, hw), lambda i, b: (b * nf + i, gcol)),
                  pl.BlockSpec((1, hw), lambda i, b: (0, 0))],
        out_specs=pl.BlockSpec((tf, hw), lambda i, b: (b * nf + i, 0)),
        out_shape=jax.ShapeDtypeStruct((nseq * n_tok, hw), F32),
        compiler_params=_cp("parallel", "parallel"), name="hyena_inv",
    )(cos_t, a_tt, yr, yi, zprev, gates, bias_row)


def _s5_kernel(u_ref, bre_ref, bim_ref, cre_ref, cim_ref, lam_ref, h0_ref, y_ref, hfin_ref, hre_s, him_s, st_s, *,
               tc, nc, ns):
    d = pl.program_id(0)
    c = pl.program_id(2)

    @pl.when(c == 0)
    def _():
        st_s[...] = h0_ref[0, 0]

    ub = u_ref[...].astype(BF16)
    hre_s[...] = _dot(ub, bre_ref[0])
    him_s[...] = _dot(ub, bim_ref[0])
    lr = lam_ref[0, 0:1, :]
    li = lam_ref[0, 1:2, :]

    def body(t, carry):
        hr, hi = carry
        r = jnp.where(d == 0, t, tc - 1 - t)
        nr = lr * hr - li * hi + hre_s[pl.ds(r, 1), :]
        ni = lr * hi + li * hr + him_s[pl.ds(r, 1), :]
        hre_s[pl.ds(r, 1), :] = nr
        him_s[pl.ds(r, 1), :] = ni
        return nr, ni

    hr, hi = lax.fori_loop(0, tc, body, (st_s[0:1, :], st_s[1:2, :]), unroll=8)
    st_s[0:1, :] = hr
    st_s[1:2, :] = hi
    y_ref[0] = _dot(hre_s[...].astype(BF16), cre_ref[0]) + _dot(him_s[...].astype(BF16), cim_ref[0])

    @pl.when(c == nc - 1)
    def _():
        hfin_ref[0, 0] = st_s[...]


def s5_scan(proj, ucol, bre, bim, cre, cim, lam, h0, nseq, seq_len, sw, tc):
    ns = bre.shape[2]
    nc = seq_len // tc

    def chunk(d, c):
        return c + d * (nc - 1 - 2 * c)

    return pl.pallas_call(
        functools.partial(_s5_kernel, tc=tc, nc=nc, ns=ns), grid=(2, nseq, nc),
        in_specs=[pl.BlockSpec((tc, sw), lambda d, b, c: (b * nc + chunk(d, c), ucol)),
                  pl.BlockSpec((1, sw, ns), lambda d, b, c: (d, 0, 0)),
                  pl.BlockSpec((1, sw, ns), lambda d, b, c: (d, 0, 0)),
                  pl.BlockSpec((1, ns, sw), lambda d, b, c: (d, 0, 0)),
                  pl.BlockSpec((1, ns, sw), lambda d, b, c: (d, 0, 0)),
                  pl.BlockSpec((1, 2, ns), lambda d, b, c: (d, 0, 0)),
                  pl.BlockSpec((1, 1, 2, ns), lambda d, b, c: (b, d, 0, 0))],
        out_specs=[pl.BlockSpec((1, tc, sw), lambda d, b, c: (d, b * nc + chunk(d, c), 0)),
                   pl.BlockSpec((1, 1, 2, ns), lambda d, b, c: (b, d, 0, 0))],
        out_shape=[jax.ShapeDtypeStruct((2, nseq * seq_len, sw), F32), jax.ShapeDtypeStruct((nseq, 2, 2, ns), F32)],
        scratch_shapes=[pltpu.VMEM((tc, ns), F32), pltpu.VMEM((tc, ns), F32), pltpu.VMEM((2, ns), F32)],
        compiler_params=_cp("parallel", "parallel", "arbitrary"), name="s5_scan",
    )(proj, bre, bim, cre, cim, lam, h0)


def _s5_glu_kernel(yf_ref, yb_ref, u_ref, d_ref, w_ref, b_ref, o_ref):
    y = jax.nn.gelu(yf_ref[0] + yb_ref[0] + d_ref[...] * u_ref[...])
    o_ref[...] = y * jax.nn.sigmoid(_dot(y.astype(BF16), w_ref[...]) + b_ref[...])


def s5_glu(y2, proj, ucol, d_skip, glu_w_bf16, glu_b, tm):
    _, n, sw = y2.shape
    return pl.pallas_call(
        _s5_glu_kernel, grid=(n // tm,),
        in_specs=[pl.BlockSpec((1, tm, sw), lambda i: (0, i, 0)), pl.BlockSpec((1, tm, sw), lambda i: (1, i, 0)),
                  pl.BlockSpec((tm, sw), lambda i: (i, ucol)), pl.BlockSpec((1, sw), lambda i: (0, 0)),
                  pl.BlockSpec((sw, sw), lambda i: (0, 0)), pl.BlockSpec((1, sw), lambda i: (0, 0))],
        out_specs=pl.BlockSpec((tm, sw), lambda i: (i, 0)),
        out_shape=jax.ShapeDtypeStruct((n, sw), F32), compiler_params=_cp("parallel"), name="s5_glu",
    )(y2, y2, proj, d_skip.reshape(1, sw), glu_w_bf16, glu_b.reshape(1, sw))


def _even_out_kernel(a_ref, b_ref, wa_ref, wb_ref, x_ref, gate_ref, o_ref):
    y = _dot(a_ref[...].astype(BF16), wa_ref[...]) + _dot(b_ref[...].astype(BF16), wb_ref[...])
    o_ref[...] = x_ref[...] + gate_ref[0] * y


def even_out(hy, s5o, w_bf16, x, gate, rows_per_mod, tm, tn):
    n, d = x.shape
    hw = hy.shape[1]
    sw = s5o.shape[1]
    return pl.pallas_call(
        _even_out_kernel, grid=(n // tm, d // tn),
        in_specs=[pl.BlockSpec((tm, hw), lambda i, j: (i, 0)), pl.BlockSpec((tm, sw), lambda i, j: (i, 0)),
                  pl.BlockSpec((hw, tn), lambda i, j: (0, j)), pl.BlockSpec((sw, tn), lambda i, j: (hw // sw, j)),
                  pl.BlockSpec((tm, tn), lambda i, j: (i, j)),
                  pl.BlockSpec((1, 1, tn), lambda i, j: ((i * tm) // rows_per_mod, 0, j))],
        out_specs=pl.BlockSpec((tm, tn), lambda i, j: (i, j)),
        out_shape=jax.ShapeDtypeStruct((n, d), F32), compiler_params=_cp("parallel", "parallel"), name="even_out",
    )(hy, s5o, w_bf16, w_bf16, x, gate)


def _log_sigmoid(x):
    return jnp.minimum(x, 0.0) - jnp.log1p(jnp.exp(-jnp.abs(x)))


def _gates_kernel(x_ref, g_ref, sc_ref, sh_ref, w_ref, b_ref, o_ref):
    h = _normmod(x_ref[...], g_ref[...], sc_ref[0], sh_ref[0]).astype(BF16)
    o_ref[0] = _dot(h, w_ref[0]) + b_ref[0]


def mlstm_gates(x, g, sc, sh, wg_bf16, bg, rows_per_mod, tm):
    n, d = x.shape
    return pl.pallas_call(
        _gates_kernel, grid=(n // tm, 2),
        in_specs=[pl.BlockSpec((tm, d), lambda i, j: (i, 0)), pl.BlockSpec((1, d), lambda i, j: (0, 0)),
                  _mod_spec(d, tm, rows_per_mod), _mod_spec(d, tm, rows_per_mod),
                  pl.BlockSpec((1, d, LANES), lambda i, j: (j, 0, 0)), pl.BlockSpec((1, 1, LANES), lambda i, j: (j, 0, 0))],
        out_specs=pl.BlockSpec((1, tm, LANES), lambda i, j: (j, i, 0)),
        out_shape=jax.ShapeDtypeStruct((2, n, LANES), F32), compiler_params=_cp("parallel", "parallel"),
        name="mlstm_gates",
    )(x, g.reshape(1, d), sc, sh, wg_bf16, bg)


def _mlstm_kernel(q_ref, k_ref, v_ref, g_ref, c0_ref, n0_ref, m0_ref, h_ref, cf_ref, nf_ref, mf_ref,
                  c_s, n_s, m_s, *, nh, dh, tc, nc):
    d = pl.program_id(0)
    c = pl.program_id(2)

    @pl.when(c == 0)
    def _():
        c_s[...] = c0_ref[0, 0]
        n_s[...] = n0_ref[0, 0]
        m_s[...] = m0_ref[0, 0]

    gates = g_ref[0]
    lane = lax.broadcasted_iota(jnp.int32, gates.shape, 1)
    logf = jnp.where(jnp.logical_and(lane >= nh, lane < 2 * nh), _log_sigmoid(gates), 0.0)
    r_i = lax.broadcasted_iota(jnp.int32, (tc, tc), 0)
    s_i = lax.broadcasted_iota(jnp.int32, (tc, tc), 1)
    causal = (r_i - s_i) * (1 - 2 * d) >= 0
    bcum = _dot(causal.astype(F32), logf, precision=HIGHEST)
    btot = jnp.sum(logf, axis=0, keepdims=True)
    gates_t = gates.T
    bcum_t = bcum.T
    for h in range(nh):
        hs = slice(h * dh, (h + 1) * dh)
        q = q_ref[:, hs]
        k = k_ref[:, hs]
        v = v_ref[:, hs]
        b_col = bcum[:, nh + h:nh + h + 1]
        b_row = bcum_t[nh + h:nh + h + 1, :]
        i_col = gates[:, h:h + 1]
        i_row = gates_t[h:h + 1, :]
        m = m_s[h:h + 1, :]
        a = b_col + m
        dmat = jnp.where(causal, b_col - b_row + i_row, NEG_INF)
        mq = jnp.maximum(a, jnp.max(dmat, axis=-1, keepdims=True))
        w_intra = jnp.exp(dmat - mq)
        w_inter = jnp.exp(a - mq)
        s = _dot_nt(q, k) * w_intra
        cm = c_s[h]
        nv = n_s[h:h + 1, :]
        num = _dot(s, v) + w_inter * _dot(q, cm)
        den = jnp.sum(s, axis=-1, keepdims=True) + w_inter * jnp.sum(q * nv, axis=-1, keepdims=True)
        h_ref[0, :, hs] = num / jnp.maximum(jnp.abs(den), jnp.exp(-mq))
        b_last = btot[:, nh + h:nh + h + 1]
        g = b_last - b_col + i_col
        m_new = jnp.maximum(b_last + m, jnp.max(g, axis=0, keepdims=True))
        kw = k * jnp.exp(g - m_new)
        keep = jnp.exp(b_last + m - m_new)
        c_s[h] = keep * cm + _dot(kw.T, v)
        n_s[h:h + 1, :] = keep * nv + jnp.sum(kw, axis=0, keepdims=True)
        m_s[h:h + 1, :] = m_new

    @pl.when(c == nc - 1)
    def _():
        cf_ref[0, 0] = c_s[...]
        nf_ref[0, 0] = n_s[...]
        mf_ref[0, 0] = m_s[...]


def mlstm_scan(qk, proj, vcol, gates, c0, n0, m0, nseq, seq_len, nh, dh):
    tc = ML_CHUNK
    nc = seq_len // tc
    w = nh * dh

    def chunk(d, c):
        return c + d * (nc - 1 - 2 * c)

    rowblk = lambda d, b, c: b * nc + chunk(d, c)
    st = lambda shp: pl.BlockSpec((1, 1) + shp, lambda d, b, c: (b, d) + (0,) * len(shp))
    return pl.pallas_call(
        functools.partial(_mlstm_kernel, nh=nh, dh=dh, tc=tc, nc=nc), grid=(2, nseq, nc),
        in_specs=[pl.BlockSpec((tc, w), lambda d, b, c: (rowblk(d, b, c), 0)),
                  pl.BlockSpec((tc, w), lambda d, b, c: (rowblk(d, b, c), 1)),
                  pl.BlockSpec((tc, w), lambda d, b, c: (rowblk(d, b, c), vcol)),
                  pl.BlockSpec((1, tc, LANES), lambda d, b, c: (d, rowblk(d, b, c), 0)),
                  st((nh, dh, dh)), st((nh, dh)), st((nh, 1))],
        out_specs=[pl.BlockSpec((1, tc, w), lambda d, b, c: (d, rowblk(d, b, c), 0)),
                   st((nh, dh, dh)), st((nh, dh)), st((nh, 1))],
        out_shape=[jax.ShapeDtypeStruct((2, nseq * seq_len, w), F32),
                   jax.ShapeDtypeStruct((nseq, 2, nh, dh, dh), F32), jax.ShapeDtypeStruct((nseq, 2, nh, dh), F32),
                   jax.ShapeDtypeStruct((nseq, 2, nh, 1), F32)],
        scratch_shapes=[pltpu.VMEM((nh, dh, dh), F32), pltpu.VMEM((nh, dh), F32), pltpu.VMEM((nh, 1), F32)],
        compiler_params=_cp("parallel", "parallel", "arbitrary"), name="mlstm_scan",
    )(qk, qk, proj, gates, c0, n0, m0)


def _odd_out_kernel(hf_ref, hb_ref, o_ref, ng_ref, w_ref, x_ref, gate_ref, out_ref, a_s, *, nh, dh):
    @pl.when(pl.program_id(1) == 0)
    def _():
        for h in range(nh):
            hs = slice(h * dh, (h + 1) * dh)
            blk = hf_ref[0, :, hs] + hb_ref[0, :, hs]
            blk = blk * lax.rsqrt(jnp.mean(blk * blk, axis=-1, keepdims=True) + EPS)
            a_s[:, hs] = ((blk * ng_ref[:, hs]) * _silu(o_ref[:, hs])).astype(BF16)
    out_ref[...] = x_ref[...] + gate_ref[0] * _dot(a_s[...], w_ref[...])


def odd_out(h2, proj, ocol, norm_g, w_bf16, x, gate, rows_per_mod, nh, dh, tm, tn):
    n, d = x.shape
    w = nh * dh
    return pl.pallas_call(
        functools.partial(_odd_out_kernel, nh=nh, dh=dh), grid=(n // tm, d // tn),
        in_specs=[pl.BlockSpec((1, tm, w), lambda i, j: (0, i, 0)), pl.BlockSpec((1, tm, w), lambda i, j: (1, i, 0)),
                  pl.BlockSpec((tm, w), lambda i, j: (i, ocol)), pl.BlockSpec((1, w), lambda i, j: (0, 0)),
                  pl.BlockSpec((w, tn), lambda i, j: (0, j)), pl.BlockSpec((tm, tn), lambda i, j: (i, j)),
                  pl.BlockSpec((1, 1, tn), lambda i, j: ((i * tm) // rows_per_mod, 0, j))],
        out_specs=pl.BlockSpec((tm, tn), lambda i, j: (i, j)),
        out_shape=jax.ShapeDtypeStruct((n, d), F32), scratch_shapes=[pltpu.VMEM((tm, w), BF16)],
        compiler_params=_cp("parallel", "arbitrary"), name="odd_out",
    )(h2, h2, proj, norm_g.reshape(1, w), w_bf16, x, gate)


def _top_values(cur, k, out_s):
    for j in range(k):
        m = jnp.max(cur, axis=0, keepdims=True)
        out_s[j:j + 1, :] = m
        cur = jnp.where(cur == m, NEG_INF, cur)


def _peer_score_kernel(h_ref, wq_ref, keys_ref, s1_ref, s2_ref, w1_ref, w2_ref, tau_ref, q_s, v1_s, v2_s, cand_s,
                       best_s, *, nh, half, topk):
    q_s[...] = _dot_nt(wq_ref[...], h_ref[...])

    def head(hd, carry):
        base = pl.multiple_of(hd * 2 * half, 2 * half)
        s1 = _dot(keys_ref[hd, 0], q_s[pl.ds(base, half), :])
        s2 = _dot(keys_ref[hd, 1], q_s[pl.ds(base + half, half), :])
        _top_values(s1, topk, v1_s)
        _top_values(s2, topk, v2_s)
        v2all = v2_s[...]
        for a in range(topk):
            cand_s[a * topk:(a + 1) * topk, :] = v1_s[a:a + 1, :] + v2all
        _top_values(cand_s[...], topk, best_s)
        best = best_s[...]
        z = jnp.sum(jnp.exp(best - best[0:1, :]), axis=0, keepdims=True)
        s1_ref[hd] = s1
        s2_ref[hd] = s2
        w1_ref[hd] = jnp.exp(s1 - v1_s[0:1, :]) / z
        w2_ref[hd] = jnp.exp(s2 - v2_s[0:1, :])
        tau_ref[hd] = best[topk - 1:topk, :]
        return carry

    lax.fori_loop(0, nh, head, 0)


def peer_scores(h_bf16, wq_t_bf16, keys, tt):
    n, d = h_bf16.shape
    nh, _, nk, half = keys.shape
    big = jax.ShapeDtypeStruct((nh, nk, n), F32)
    bspec = pl.BlockSpec((nh, nk, tt), lambda i: (0, 0, i))
    return pl.pallas_call(
        functools.partial(_peer_score_kernel, nh=nh, half=half, topk=PK_TOPK), grid=(n // tt,),
        in_specs=[pl.BlockSpec((tt, d), lambda i: (i, 0)), pl.BlockSpec((nh * 2 * half, d), lambda i: (0, 0)),
                  pl.BlockSpec((nh, 2, nk, half), lambda i: (0, 0, 0, 0))],
        out_specs=[bspec, bspec, bspec, bspec, pl.BlockSpec((nh, 1, tt), lambda i: (0, 0, i))],
        out_shape=[big, big, big, big, jax.ShapeDtypeStruct((nh, 1, n), F32)],
        scratch_shapes=[pltpu.VMEM((nh * 2 * half, tt), F32), pltpu.VMEM((PK_TOPK, tt), F32),
                        pltpu.VMEM((PK_TOPK, tt), F32), pltpu.VMEM((PK_TOPK * PK_TOPK, tt), F32),
                        pltpu.VMEM((PK_TOPK, tt), F32)],
        compiler_params=_cp("parallel"), name="peer_scores",
    )(h_bf16, wq_t_bf16, keys)


def _peer_dense_kernel(h_ref, u_ref, vt_ref, s1_ref, s2_ref, w1_ref, w2_ref, tau_ref, x_ref, gate_ref, o_ref,
                       acc_s, st_s, wt_s, *, nh, nk, ec, tt):
    e = pl.program_id(1)

    @pl.when(e == 0)
    def _():
        acc_s[...] = jnp.zeros_like(acc_s)

    st_s[...] = _dot_nt(u_ref[...], h_ref[...])
    n_i1 = ec // nk

    for li in range(n_i1):
        rows = slice(li * nk, (li + 1) * nk)
        for lt in range(tt // LANES):
            sl = slice(lt * LANES, (lt + 1) * LANES)
            g = jnp.zeros((nk, LANES), F32)
            for hd in range(nh):
                pair = s2_ref[hd, :, sl] + s1_ref[hd, li:li + 1, sl]
                g = g + jnp.where(pair >= tau_ref[hd, :, sl], w2_ref[hd, :, sl] * w1_ref[hd, li:li + 1, sl], 0.0)
            wt_s[rows, sl] = (jax.nn.gelu(st_s[rows, sl]) * g).astype(BF16)
    acc_s[...] += _dot(vt_ref[...], wt_s[...])

    @pl.when(e == pl.num_programs(1) - 1)
    def _():
        o_ref[...] = x_ref[...] + gate_ref[0] * acc_s[...].T


def peer_dense(h_bf16, u_bf16, vt_bf16, s1, s2, w1, w2, tau, x, gate, rows_per_mod, tt, ec):
    n, d = x.shape
    nh, nk, _ = s1.shape
    ne = u_bf16.shape[0]
    bspec = pl.BlockSpec((nh, nk, tt), lambda i, e: (0, 0, i))
    rspec = pl.BlockSpec((nh, ec // nk, tt), lambda i, e: (0, e, i))
    return pl.pallas_call(
        functools.partial(_peer_dense_kernel, nh=nh, nk=nk, ec=ec, tt=tt), grid=(n // tt, ne // ec),
        in_specs=[pl.BlockSpec((tt, d), lambda i, e: (i, 0)), pl.BlockSpec((ec, d), lambda i, e: (e, 0)),
                  pl.BlockSpec((d, ec), lambda i, e: (0, e)), rspec, bspec, rspec, bspec,
                  pl.BlockSpec((nh, 1, tt), lambda i, e: (0, 0, i)), pl.BlockSpec((tt, d), lambda i, e: (i, 0)),
                  pl.BlockSpec((1, 1, d), lambda i, e: ((i * tt) // rows_per_mod, 0, 0))],
        out_specs=pl.BlockSpec((tt, d), lambda i, e: (i, 0)),
        out_shape=jax.ShapeDtypeStruct((n, d), F32),
        scratch_shapes=[pltpu.VMEM((d, tt), F32), pltpu.VMEM((ec, tt), F32), pltpu.VMEM((ec, tt), BF16)],
        compiler_params=_cp("parallel", "arbitrary"), name="peer_dense",
    )(h_bf16, u_bf16, vt_bf16, s1, s2, w1, w2, tau, x, gate)


def _s5_params(a_re, a_im, b_re, b_im, c_re, c_im, log_step):
    lam = lax.complex(a_re.astype(F32), a_im.astype(F32))
    lam_bar = jnp.exp(lam * jnp.exp(log_step.astype(F32))[..., None])
    b_bar = ((lam_bar - 1.0) / lam)[..., None] * lax.complex(b_re.astype(F32), b_im.astype(F32))
    ngrp, npst, nch = b_bar.shape[1:]
    eye = jnp.eye(ngrp, dtype=F32)

    def b_mat(part):
        return jnp.einsum("dgpj,gh->dgjhp", part, eye).reshape(2, ngrp * nch, ngrp * npst)

    def c_mat(part):
        return jnp.einsum("dgjp,gh->dgphj", part, eye).reshape(2, ngrp * npst, ngrp * nch)

    bre, bim = b_mat(b_bar.real).astype(BF16), b_mat(b_bar.imag).astype(BF16)
    cre, cim = c_mat(c_re.astype(F32)).astype(BF16), c_mat(-c_im.astype(F32)).astype(BF16)
    lam2 = jnp.stack([lam_bar.real.reshape(2, -1), lam_bar.imag.reshape(2, -1)], axis=1)
    return bre, bim, cre, cim, lam2


def _pos_embed(n_tok, d, grid_w):
    rows = n_tok // grid_w
    quarter = d // 4
    omega = 1.0 / (10000.0 ** (jnp.arange(quarter, dtype=F32) / quarter))

    def emb1d(pos):
        ang = pos.astype(F32)[:, None] * omega[None]
        return jnp.concatenate([jnp.sin(ang), jnp.cos(ang)], axis=-1)

    er = emb1d(jnp.arange(rows))
    ec = emb1d(jnp.arange(grid_w))
    half = d // 2
    pe = jnp.concatenate([jnp.broadcast_to(er[:, None], (rows, grid_w, half)),
                          jnp.broadcast_to(ec[None], (rows, grid_w, half))], axis=-1)
    return pe.reshape(rows * grid_w, d)


def _tile(n, pref):
    return pref if n % pref == 0 else n


def _trunk(x, mods, s5_h0, ml_c0, ml_n0, ml_m0, p, nseq, seq_len, rows_per_mod):
    n, d = x.shape
    tm = _tile(min(rows_per_mod, n), 512)
    depth = p["norm_g"].shape[0]
    s5_fin, ml_fin = [], []
    for l in range(depth):
        sh1, sc1, g1, sh2, sc2, g2 = mods[l]
        i = l // 2
        if l % 2 == 0:
            hw = p["hy_bias"].shape[2]
            sw = p["s5_d"].shape[1]
            proj = normmod_matmul(x, p["norm_g"][l, 0], sc1, sh1, p["ev_w_in"][i].astype(BF16), rows_per_mod, tm,
                                  _tile(3 * hw + sw, 512))
            hy_in = short_conv(proj, 3 * hw, p["hy_conv_w"][i], p["hy_conv_b"][i], jnp.ones((3 * hw,), F32),
                               seq_len, act=False)
            cos_t, a_t, a_tt = dft_tables(seq_len)
            tf = _tile(seq_len, 512)
            taps, sumsq = hyena_filter_taps(seq_len, p["hy_w1"][i], p["hy_b1"][i], p["hy_w2"][i], p["hy_b2"][i],
                                            p["hy_w3"][i], p["hy_freq"][i], p["hy_decay"][i], hw)
            kr, ki = hyena_filter_spectrum(cos_t, a_t, taps, sumsq, hw, tf)
            bias = p["hy_bias"][i].astype(F32)
            z, zcol = hy_in, 0
            for o in range(bias.shape[0]):
                yr, yi = hyena_fwd(cos_t, a_t, z, zcol, kr, ki, o, nseq, hw, tf)
                z = hyena_inv(cos_t, a_tt, yr, yi, z, zcol, hy_in, 1 + o, bias[o:o + 1], nseq, hw, tf)
                zcol = 0
            bre, bim, cre, cim, lam2 = _s5_params(p["s5_a_re"][i], p["s5_a_im"][i], p["s5_b_re"][i], p["s5_b_im"][i],
                                                  p["s5_c_re"][i], p["s5_c_im"][i], p["s5_log_step"][i])
            ucol = 3 * hw // sw
            y2, hfin = s5_scan(proj, ucol, bre, bim, cre, cim, lam2, s5_h0[i], nseq, seq_len, sw, _tile(seq_len, 256))
            s5_fin.append(hfin)
            s5o = s5_glu(y2, proj, ucol, p["s5_d"][i], p["s5_glu_w"][i].astype(BF16), p["s5_glu_b"][i], tm)
            x = even_out(z, s5o, p["ev_w_out"][i].astype(BF16), x, g1, rows_per_mod, tm, _tile(d, 512))
        else:
            nh = p["od_gate_b"].shape[2]
            w = p["ml_norm_g"].shape[1]
            dh = w // nh
            w_in = p["od_w_in"][i]
            proj = normmod_matmul(x, p["norm_g"][l, 0], sc1, sh1, w_in[:, :4 * w].astype(BF16), rows_per_mod, tm,
                                  _tile(4 * w, 512))
            wg = w_in[:, 4 * w:].reshape(d, 4, nh)
            gb = p["od_gate_b"][i].astype(F32)
            wg2 = jnp.zeros((2, d, LANES), F32)
            bg2 = jnp.zeros((2, 1, LANES), F32)
            for dr in range(2):
                wg2 = wg2.at[dr, :, :nh].set(wg[:, dr]).at[dr, :, nh:2 * nh].set(wg[:, 2 + dr])
                bg2 = bg2.at[dr, 0, :nh].set(gb[dr]).at[dr, 0, nh:2 * nh].set(gb[2 + dr])
            gates = mlstm_gates(x, p["norm_g"][l, 0], sc1, sh1, wg2.astype(BF16), bg2, rows_per_mod, tm)
            qscale = jnp.concatenate([jnp.full((w,), dh ** -0.5, F32), jnp.ones((w,), F32)])
            qk = short_conv(proj, 2 * w, p["ml_conv_w"][i], p["ml_conv_b"][i], qscale, seq_len, act=True)
            h2, cf, nf, mf = mlstm_scan(qk, proj, 2, gates, ml_c0[i], ml_n0[i], ml_m0[i], nseq, seq_len, nh, dh)
            ml_fin.append((cf, nf, mf))
            x = odd_out(h2, proj, 3, p["ml_norm_g"][i], p["od_w_out"][i].astype(BF16), x, g1, rows_per_mod, nh, dh, tm,
                        _tile(d, 512))
        hn = normmod(x, p["norm_g"][l, 1], sc2, sh2, rows_per_mod, tm)
        tt = _tile(min(rows_per_mod, n), 512)
        s1, s2, w1, w2, tau = peer_scores(hn, p["pk_w_q"][l].T.astype(BF16), p["pk_keys"][l].astype(F32), _tile(tt, 256))
        x = peer_dense(hn, p["pk_u"][l].astype(BF16), p["pk_v"][l].T.astype(BF16), s1, s2, w1, w2, tau, x, g2,
                       rows_per_mod, tt, 1024)
    y = final_norm(x, p["final_g"], tm)
    return y, s5_fin, ml_fin


def kernel(x_prompt, x_sample, state_s5_re, state_s5_im, state_mlstm_C, state_mlstm_n, state_mlstm_m, c, c_ctx, norm_g, ada_w, ada_b, final_g, ev_w_in, hy_conv_w, hy_conv_b, hy_w1, hy_b1, hy_w2, hy_b2, hy_w3, hy_freq, hy_decay, hy_bias, s5_a_re, s5_a_im, s5_b_re, s5_b_im, s5_c_re, s5_c_im, s5_log_step, s5_d, s5_glu_w, s5_glu_b, ev_w_out, od_w_in, od_gate_b, ml_conv_w, ml_conv_b, ml_norm_g, od_w_out, pk_w_q, pk_keys, pk_u, pk_v):
    p = dict(norm_g=norm_g, ada_w=ada_w, ada_b=ada_b, final_g=final_g, ev_w_in=ev_w_in,
             hy_conv_w=hy_conv_w, hy_conv_b=hy_conv_b, hy_w1=hy_w1, hy_b1=hy_b1, hy_w2=hy_w2, hy_b2=hy_b2,
             hy_w3=hy_w3, hy_freq=hy_freq, hy_decay=hy_decay, hy_bias=hy_bias, s5_a_re=s5_a_re,
             s5_a_im=s5_a_im, s5_b_re=s5_b_re, s5_b_im=s5_b_im, s5_c_re=s5_c_re, s5_c_im=s5_c_im,
             s5_log_step=s5_log_step, s5_d=s5_d, s5_glu_w=s5_glu_w, s5_glu_b=s5_glu_b, ev_w_out=ev_w_out,
             od_w_in=od_w_in, od_gate_b=od_gate_b, ml_conv_w=ml_conv_w, ml_conv_b=ml_conv_b,
             ml_norm_g=ml_norm_g, od_w_out=od_w_out, pk_w_q=pk_w_q, pk_keys=pk_keys, pk_u=pk_u, pk_v=pk_v)
    nb, seq, d = x_prompt.shape
    db, dseq, _ = x_sample.shape
    depth = norm_g.shape[0]
    n_even, n_odd = (depth + 1) // 2, depth // 2
    assert db + 1 <= 8

    cond8 = jnp.zeros((8, d), F32).at[0].set(c_ctx.astype(F32)).at[1:1 + db].set(c.astype(F32))
    mods_ctx, mods_lat = [], []
    for l in range(depth):
        mod = ada_mod(cond8, ada_w[l].astype(F32), ada_b[l].astype(F32))
        chunks = [mod[:, j * d:(j + 1) * d] for j in range(6)]
        mods_ctx.append([ch[0:1].reshape(1, 1, d) for ch in chunks])
        mods_lat.append([ch[1:1 + db].reshape(db, 1, d) for ch in chunks])

    def s5_state(re, im, bsz):
        return [jnp.stack([re[:, i].reshape(bsz, 2, -1), im[:, i].reshape(bsz, 2, -1)], axis=2).astype(F32)
                for i in range(n_even)]

    ngrp, npst = s5_a_re.shape[2], s5_a_re.shape[3]
    nh, dh = state_mlstm_C.shape[3], state_mlstm_C.shape[4]
    zeros_s5 = jnp.zeros((nb, n_even, 2, ngrp, npst), F32)
    y_prompt, s5_fin, ml_fin = _trunk(
        x_prompt.reshape(nb * seq, d), mods_ctx, s5_state(zeros_s5, zeros_s5, nb),
        [jnp.zeros((nb, 2, nh, dh, dh), F32)] * n_odd, [jnp.zeros((nb, 2, nh, dh), F32)] * n_odd,
        [jnp.zeros((nb, 2, nh, 1), F32)] * n_odd, p, nb, seq, nb * seq)
    x_lat = add_pos(x_sample.reshape(db * dseq, d), _pos_embed(dseq, d, GRID_W), dseq, _tile(dseq, 512))
    y_sample, _, _ = _trunk(
        x_lat, mods_lat, s5_state(state_s5_re, state_s5_im, db),
        [state_mlstm_C[:, i].astype(F32) for i in range(n_odd)], [state_mlstm_n[:, i].astype(F32) for i in range(n_odd)],
        [state_mlstm_m[:, i].astype(F32)[..., None] for i in range(n_odd)], p, db, dseq, dseq)

    new_s5_re = jnp.stack([h[:, :, 0].reshape(nb, 2, ngrp, npst) for h in s5_fin], axis=1)
    new_s5_im = jnp.stack([h[:, :, 1].reshape(nb, 2, ngrp, npst) for h in s5_fin], axis=1)
    new_c = jnp.stack([f[0] for f in ml_fin], axis=1)
    new_n = jnp.stack([f[1] for f in ml_fin], axis=1)
    new_m = jnp.stack([f[2][..., 0] for f in ml_fin], axis=1)
    return (y_prompt.reshape(nb, seq, d), y_sample.reshape(db, dseq, d), new_s5_re, new_s5_im, new_c, new_n, new_m)
```

```python
import functools
import math

import jax
import jax.numpy as jnp
from jax import lax
from jax.experimental import pallas as pl
from jax.experimental.pallas import tpu as pltpu

F32 = jnp.float32
BF16 = jnp.bfloat16
EPS = 1e-6
HIGHEST = lax.Precision.HIGHEST
V7X_VMEM_LIMIT_BYTES = 56 * 1024 * 1024
LANES = 128
ML_CHUNK = 128
PK_TOPK = 16
GRID_W = 64
NEG_INF = float("-inf")


def _cp(*sem):
    return pltpu.CompilerParams(dimension_semantics=sem, vmem_limit_bytes=V7X_VMEM_LIMIT_BYTES)


def _dot(a, b, **kw):
    return jnp.dot(a, b, preferred_element_type=F32, **kw)


def _dot_nt(a, b):
    return lax.dot_general(a, b, (((1,), (1,)), ((), ())), preferred_element_type=F32)


def _silu(x):
    return x * jax.nn.sigmoid(x)


def _ada_kernel(c_ref, w_ref, b_ref, o_ref):
    o_ref[...] = _dot(_silu(c_ref[...]), w_ref[...], precision=HIGHEST) + b_ref[...]


def ada_mod(cond8, w, b):
    d, no = w.shape
    tn = 1536 if no % 1536 == 0 else no
    return pl.pallas_call(
        _ada_kernel, grid=(no // tn,),
        in_specs=[pl.BlockSpec((8, d), lambda j: (0, 0)), pl.BlockSpec((d, tn), lambda j: (0, j)),
                  pl.BlockSpec((1, tn), lambda j: (0, j))],
        out_specs=pl.BlockSpec((8, tn), lambda j: (0, j)),
        out_shape=jax.ShapeDtypeStruct((8, no), F32), compiler_params=_cp("parallel"), name="ada_mod",
    )(cond8, w, b.reshape(1, no))


def _normmod(x, g, sc, sh):
    y = x * lax.rsqrt(jnp.mean(x * x, axis=-1, keepdims=True) + EPS)
    return (y * g) * (1.0 + sc) + sh


def _mod_spec(d, tm, rows_per_mod, nd=2):
    if nd == 2:
        return pl.BlockSpec((1, 1, d), lambda i, j: ((i * tm) // rows_per_mod, 0, 0))
    return pl.BlockSpec((1, 1, d), lambda i: ((i * tm) // rows_per_mod, 0, 0))


def _nm_matmul_kernel(x_ref, g_ref, sc_ref, sh_ref, w_ref, o_ref, h_ref):
    @pl.when(pl.program_id(1) == 0)
    def _():
        h_ref[...] = _normmod(x_ref[...], g_ref[...], sc_ref[0], sh_ref[0]).astype(BF16)
    o_ref[...] = _dot(h_ref[...], w_ref[...])


def normmod_matmul(x, g, sc, sh, w_bf16, rows_per_mod, tm, tn):
    n, d = x.shape
    no = w_bf16.shape[1]
    return pl.pallas_call(
        _nm_matmul_kernel, grid=(n // tm, no // tn),
        in_specs=[pl.BlockSpec((tm, d), lambda i, j: (i, 0)), pl.BlockSpec((1, d), lambda i, j: (0, 0)),
                  _mod_spec(d, tm, rows_per_mod), _mod_spec(d, tm, rows_per_mod),
                  pl.BlockSpec((d, tn), lambda i, j: (0, j))],
        out_specs=pl.BlockSpec((tm, tn), lambda i, j: (i, j)),
        out_shape=jax.ShapeDtypeStruct((n, no), F32),
        scratch_shapes=[pltpu.VMEM((tm, d), BF16)],
        compiler_params=_cp("parallel", "arbitrary"), name="normmod_matmul",
    )(x, g.reshape(1, d), sc, sh, w_bf16)


def _nm_kernel(x_ref, g_ref, sc_ref, sh_ref, o_ref):
    o_ref[...] = _normmod(x_ref[...], g_ref[...], sc_ref[0], sh_ref[0]).astype(o_ref.dtype)


def normmod(x, g, sc, sh, rows_per_mod, tm):
    n, d = x.shape
    return pl.pallas_call(
        _nm_kernel, grid=(n // tm,),
        in_specs=[pl.BlockSpec((tm, d), lambda i: (i, 0)), pl.BlockSpec((1, d), lambda i: (0, 0)),
                  _mod_spec(d, tm, rows_per_mod, 1), _mod_spec(d, tm, rows_per_mod, 1)],
        out_specs=pl.BlockSpec((tm, d), lambda i: (i, 0)),
        out_shape=jax.ShapeDtypeStruct((n, d), BF16), compiler_params=_cp("parallel"), name="normmod",
    )(x, g.reshape(1, d), sc, sh)


def _final_norm_kernel(x_ref, g_ref, o_ref):
    x = x_ref[...]
    o_ref[...] = (x * lax.rsqrt(jnp.mean(x * x, axis=-1, keepdims=True) + EPS)) * g_ref[...]


def final_norm(x, g, tm):
    n, d = x.shape
    return pl.pallas_call(
        _final_norm_kernel, grid=(n // tm,),
        in_specs=[pl.BlockSpec((tm, d), lambda i: (i, 0)), pl.BlockSpec((1, d), lambda i: (0, 0))],
        out_specs=pl.BlockSpec((tm, d), lambda i: (i, 0)),
        out_shape=jax.ShapeDtypeStruct((n, d), F32), compiler_params=_cp("parallel"), name="final_norm",
    )(x, g.reshape(1, d))


def _add_rows_kernel(x_ref, p_ref, o_ref):
    o_ref[...] = x_ref[...] + p_ref[...]


def add_pos(x, pe, seq_len, tm):
    n, d = x.shape
    nb = seq_len // tm
    return pl.pallas_call(
        _add_rows_kernel, grid=(n // tm,),
        in_specs=[pl.BlockSpec((tm, d), lambda i: (i, 0)), pl.BlockSpec((tm, d), lambda i: (i % nb, 0))],
        out_specs=pl.BlockSpec((tm, d), lambda i: (i, 0)),
        out_shape=jax.ShapeDtypeStruct((n, d), F32), compiler_params=_cp("parallel"), name="add_pos",
    )(x, pe)


def _sconv_kernel(x_ref, w_ref, b_ref, s_ref, o_ref, *, act):
    x = x_ref[...]
    n_tok = x.shape[0]
    row = lax.broadcasted_iota(jnp.int32, x.shape, 0)
    prev = jnp.where(row == 0, 0.0, pltpu.roll(x, 1, 0))
    nxt = jnp.where(row == n_tok - 1, 0.0, pltpu.roll(x, n_tok - 1, 0))
    y = prev * w_ref[0:1, :] + x * w_ref[1:2, :] + nxt * w_ref[2:3, :] + b_ref[...]
    if act:
        y = _silu(y) * s_ref[...]
    o_ref[...] = y


def short_conv(a, ncols, w, b, scale, seq_len, act, cb=256):
    n = a.shape[0]
    return pl.pallas_call(
        functools.partial(_sconv_kernel, act=act), grid=(n // seq_len, ncols // cb),
        in_specs=[pl.BlockSpec((seq_len, cb), lambda s, j: (s, j)), pl.BlockSpec((3, cb), lambda s, j: (0, j)),
                  pl.BlockSpec((1, cb), lambda s, j: (0, j)), pl.BlockSpec((1, cb), lambda s, j: (0, j))],
        out_specs=pl.BlockSpec((seq_len, cb), lambda s, j: (s, j)),
        out_shape=jax.ShapeDtypeStruct((n, ncols), F32), compiler_params=_cp("parallel", "parallel"),
        name="short_conv",
    )(a, w, b.reshape(1, ncols), scale.reshape(1, ncols))


def dft_tables(n_tok):
    k = jnp.arange(n_tok, dtype=jnp.int32)
    kn = (k[:, None] * k[None, :]) % (2 * n_tok)
    ang = kn.astype(F32) * (math.pi / n_tok)
    cos_t = jnp.cos(ang)
    msin = -jnp.sin(ang)
    alt = jnp.where(k % 2 == 0, 1.0, -1.0).astype(F32)
    a_t = msin.at[0, :].set(alt)
    a_tt = msin.at[:, 0].set(alt)
    return cos_t.astype(BF16), a_t.astype(BF16), a_tt.astype(BF16)


def _hyfilt_kernel(band_ref, w1_ref, b1_ref, w2_ref, b2_ref, w3_ref, fr_ref, dec_ref, h_ref, ss_ref, *,
                   n_tok, tl, hw, nbands):
    i = pl.program_id(0)
    pos = i * tl + lax.broadcasted_iota(jnp.int32, (tl, 1), 0)
    t = pos.astype(F32) / n_tok
    lane = lax.broadcasted_iota(jnp.int32, (tl, LANES), 1)
    ang = 2.0 * math.pi * t * band_ref[...]
    z = jnp.where(lane == 0, t, jnp.where(lane <= nbands, jnp.cos(ang),
                                          jnp.where(lane <= 2 * nbands, jnp.sin(ang), 0.0)))
    fr = fr_ref[...]
    h = jnp.sin(fr * (_dot(z, w1_ref[...], precision=HIGHEST) + b1_ref[...]))
    h = jnp.sin(fr * (_dot(h, w2_ref[...], precision=HIGHEST) + b2_ref[...]))
    h = _dot(h, w3_ref[...], precision=HIGHEST) * jnp.exp(-t * jnp.abs(dec_ref[...]))
    col = lax.broadcasted_iota(jnp.int32, h.shape, 1)
    is_bwd = (col // hw) % 2 == 1
    h = jnp.where(jnp.logical_and(is_bwd, pos == 0), 0.0, h)
    h_ref[...] = h.astype(BF16)

    @pl.when(i == 0)
    def _():
        ss_ref[...] = jnp.zeros_like(ss_ref)
    ss_ref[...] += jnp.sum(h * h, axis=0, keepdims=True)


def hyena_filter_taps(n_tok, w1, b1, w2, b2, w3, freq, decay, hw):
    emb, ffn = w1.shape
    nbands = (emb - 1) // 2
    tl = min(n_tok, 512)
    bands = jnp.linspace(1e-4, nbands - 1, nbands, dtype=F32)
    band_row = jnp.zeros((1, LANES), F32).at[0, 1:1 + nbands].set(bands).at[0, 1 + nbands:1 + 2 * nbands].set(bands)
    w1p = jnp.zeros((LANES, ffn), F32).at[:emb].set(w1)
    nc = w3.shape[1]
    full = lambda shp: pl.BlockSpec(shp, lambda i: (0, 0))
    return pl.pallas_call(
        functools.partial(_hyfilt_kernel, n_tok=n_tok, tl=tl, hw=hw, nbands=nbands), grid=(n_tok // tl,),
        in_specs=[full((1, LANES)), full((LANES, ffn)), full((1, ffn)), full((ffn, ffn)), full((1, ffn)),
                  full((ffn, nc)), full((1, ffn)), full((1, nc))],
        out_specs=[pl.BlockSpec((tl, nc), lambda i: (i, 0)), full((1, nc))],
        out_shape=[jax.ShapeDtypeStruct((n_tok, nc), BF16), jax.ShapeDtypeStruct((1, nc), F32)],
        compiler_params=_cp("arbitrary"), name="hyena_filter_taps",
    )(band_row, w1p, b1.reshape(1, ffn), w2, b2.reshape(1, ffn), w3, freq.reshape(1, ffn), decay.reshape(1, nc))


def _filt_dft_kernel(c_ref, a_ref, h_ref, ss_ref, kr_ref, ki_ref, *, tf, hw):
    i = pl.program_id(1)
    hf = h_ref[:, :hw]
    hb = h_ref[:, hw:]
    cc = c_ref[...]
    aa = a_ref[...]
    zrf, zif, zrb, zib = _dot(cc, hf), _dot(aa, hf), _dot(cc, hb), _dot(aa, hb)
    scale = lax.rsqrt(ss_ref[:, :hw] + ss_ref[:, hw:] + EPS)
    first = (i * tf + lax.broadcasted_iota(jnp.int32, (tf, 1), 0)) == 0
    scale = scale * jnp.where(first, 0.5, 1.0)
    kr_ref[0] = (zrf + zrb) * scale
    ki_ref[0] = jnp.where(first, zif + zib, zif - zib) * scale


def hyena_filter_spectrum(cos_t, a_t, taps, sumsq, hw, tf):
    n_tok = cos_t.shape[0]
    norder = taps.shape[1] // (2 * hw)
    out = jax.ShapeDtypeStruct((norder, n_tok, hw), F32)
    return pl.pallas_call(
        functools.partial(_filt_dft_kernel, tf=tf, hw=hw), grid=(norder, n_tok // tf),
        in_specs=[pl.BlockSpec((tf, n_tok), lambda o, i: (i, 0)), pl.BlockSpec((tf, n_tok), lambda o, i: (i, 0)),
                  pl.BlockSpec((n_tok, 2 * hw), lambda o, i: (0, o)), pl.BlockSpec((1, 2 * hw), lambda o, i: (0, o))],
        out_specs=[pl.BlockSpec((1, tf, hw), lambda o, i: (o, i, 0))] * 2,
        out_shape=[out, out], compiler_params=_cp("parallel", "parallel"), name="hyena_filter_spectrum",
    )(cos_t, a_t, taps, sumsq)


def _hy_fwd_kernel(c_ref, a_ref, z_ref, kr_ref, ki_ref, yr_ref, yi_ref, *, tf):
    i = pl.program_id(0)
    zb = z_ref[...].astype(BF16)
    zr = _dot(c_ref[...], zb)
    zi = _dot(a_ref[...], zb)
    kr = kr_ref[0]
    ki = ki_ref[0]
    first = (i * tf + lax.broadcasted_iota(jnp.int32, (tf, 1), 0)) == 0
    yr_ref[...] = jnp.where(first, zr * kr, zr * kr - zi * ki).astype(BF16)
    yi_ref[...] = jnp.where(first, zi * ki, zr * ki + zi * kr).astype(BF16)


def hyena_fwd(cos_t, a_t, z, zcol, kr, ki, order, nseq, hw, tf):
    n_tok = cos_t.shape[0]
    nf = n_tok // tf
    out = jax.ShapeDtypeStruct((nseq * n_tok, hw), BF16)
    return pl.pallas_call(
        functools.partial(_hy_fwd_kernel, tf=tf), grid=(nf, nseq),
        in_specs=[pl.BlockSpec((tf, n_tok), lambda i, b: (i, 0)), pl.BlockSpec((tf, n_tok), lambda i, b: (i, 0)),
                  pl.BlockSpec((n_tok, hw), lambda i, b: (b, zcol)),
                  pl.BlockSpec((1, tf, hw), lambda i, b: (order, i, 0)),
                  pl.BlockSpec((1, tf, hw), lambda i, b: (order, i, 0))],
        out_specs=[pl.BlockSpec((tf, hw), lambda i, b: (b * nf + i, 0))] * 2,
        out_shape=[out, out], compiler_params=_cp("parallel", "parallel"), name="hyena_fwd",
    )(cos_t, a_t, z, kr, ki)


def _hy_inv_kernel(c_ref, at_ref, yr_ref, yi_ref, zp_ref, gate_ref, bias_ref, o_ref, *, inv_len):
    conv = (_dot(c_ref[...], yr_ref[...]) + _dot(at_ref[...], yi_ref[...])) * inv_len
    o_ref[...] = gate_ref[...] * (conv + bias_ref[...] * zp_ref[...])


def hyena_inv(cos_t, a_tt, yr, yi, zprev, zcol, gates, gcol, bias_row, nseq, hw, tf):
    n_tok = cos_t.shape[0]
    nf = n_tok // tf
    return pl.pallas_call(
        functools.partial(_hy_inv_kernel, inv_len=1.0 / n_tok), grid=(nf, nseq),
        in_specs=[pl.BlockSpec((tf, n_tok), lambda i, b: (i, 0)), pl.BlockSpec((tf, n_tok), lambda i, b: (i, 0)),
                  pl.BlockSpec((n_tok, hw), lambda i, b: (b, 0)), pl.BlockSpec((n_tok, hw), lambda i, b: (b, 0)),
                  pl.BlockSpec((tf, hw), lambda i, b: (b * nf + i, zcol)),
                  pl.BlockSpec((tf, hw), lambda i, b: (b * nf + i, gcol)),
                  pl.BlockSpec((1, hw), lambda i, b: (0, 0))],
        out_specs=pl.BlockSpec((tf, hw), lambda i, b: (b * nf + i, 0)),
        out_shape=jax.ShapeDtypeStruct((nseq * n_tok, hw), F32),
        compiler_params=_cp("parallel", "parallel"), name="hyena_inv",
    )(cos_t, a_tt, yr, yi, zprev, gates, bias_row)


def _s5_kernel(u_ref, bre_ref, bim_ref, cre_ref, cim_ref, lam_ref, h0_ref, y_ref, hfin_ref, hre_s, him_s, st_s, *,
               tc, nc, ns):
    d = pl.program_id(0)
    c = pl.program_id(2)

    @pl.when(c == 0)
    def _():
        st_s[...] = h0_ref[0, 0]

    ub = u_ref[...].astype(BF16)
    hre_s[...] = _dot(ub, bre_ref[0])
    him_s[...] = _dot(ub, bim_ref[0])
    lr = lam_ref[0, 0:1, :]
    li = lam_ref[0, 1:2, :]

    def body(t, carry):
        hr, hi = carry
        r = jnp.where(d == 0, t, tc - 1 - t)
        nr = lr * hr - li * hi + hre_s[pl.ds(r, 1), :]
        ni = lr * hi + li * hr + him_s[pl.ds(r, 1), :]
        hre_s[pl.ds(r, 1), :] = nr
        him_s[pl.ds(r, 1), :] = ni
        return nr, ni

    hr, hi = lax.fori_loop(0, tc, body, (st_s[0:1, :], st_s[1:2, :]), unroll=8)
    st_s[0:1, :] = hr
    st_s[1:2, :] = hi
    y_ref[0] = _dot(hre_s[...].astype(BF16), cre_ref[0]) + _dot(him_s[...].astype(BF16), cim_ref[0])

    @pl.when(c == nc - 1)
    def _():
        hfin_ref[0, 0] = st_s[...]


def s5_scan(proj, ucol, bre, bim, cre, cim, lam, h0, nseq, seq_len, sw, tc):
    ns = bre.shape[2]
    nc = seq_len // tc

    def chunk(d, c):
        return c + d * (nc - 1 - 2 * c)

    return pl.pallas_call(
        functools.partial(_s5_kernel, tc=tc, nc=nc, ns=ns), grid=(2, nseq, nc),
        in_specs=[pl.BlockSpec((tc, sw), lambda d, b, c: (b * nc + chunk(d, c), ucol)),
                  pl.BlockSpec((1, sw, ns), lambda d, b, c: (d, 0, 0)),
                  pl.BlockSpec((1, sw, ns), lambda d, b, c: (d, 0, 0)),
                  pl.BlockSpec((1, ns, sw), lambda d, b, c: (d, 0, 0)),
                  pl.BlockSpec((1, ns, sw), lambda d, b, c: (d, 0, 0)),
                  pl.BlockSpec((1, 2, ns), lambda d, b, c: (d, 0, 0)),
                  pl.BlockSpec((1, 1, 2, ns), lambda d, b, c: (b, d, 0, 0))],
        out_specs=[pl.BlockSpec((1, tc, sw), lambda d, b, c: (d, b * nc + chunk(d, c), 0)),
                   pl.BlockSpec((1, 1, 2, ns), lambda d, b, c: (b, d, 0, 0))],
        out_shape=[jax.ShapeDtypeStruct((2, nseq * seq_len, sw), F32), jax.ShapeDtypeStruct((nseq, 2, 2, ns), F32)],
        scratch_shapes=[pltpu.VMEM((tc, ns), F32), pltpu.VMEM((tc, ns), F32), pltpu.VMEM((2, ns), F32)],
        compiler_params=_cp("parallel", "parallel", "arbitrary"), name="s5_scan",
    )(proj, bre, bim, cre, cim, lam, h0)


def _s5_glu_kernel(yf_ref, yb_ref, u_ref, d_ref, w_ref, b_ref, o_ref):
    y = jax.nn.gelu(yf_ref[0] + yb_ref[0] + d_ref[...] * u_ref[...])
    o_ref[...] = y * jax.nn.sigmoid(_dot(y.astype(BF16), w_ref[...]) + b_ref[...])


def s5_glu(y2, proj, ucol, d_skip, glu_w_bf16, glu_b, tm):
    _, n, sw = y2.shape
    return pl.pallas_call(
        _s5_glu_kernel, grid=(n // tm,),
        in_specs=[pl.BlockSpec((1, tm, sw), lambda i: (0, i, 0)), pl.BlockSpec((1, tm, sw), lambda i: (1, i, 0)),
                  pl.BlockSpec((tm, sw), lambda i: (i, ucol)), pl.BlockSpec((1, sw), lambda i: (0, 0)),
                  pl.BlockSpec((sw, sw), lambda i: (0, 0)), pl.BlockSpec((1, sw), lambda i: (0, 0))],
        out_specs=pl.BlockSpec((tm, sw), lambda i: (i, 0)),
        out_shape=jax.ShapeDtypeStruct((n, sw), F32), compiler_params=_cp("parallel"), name="s5_glu",
    )(y2, y2, proj, d_skip.reshape(1, sw), glu_w_bf16, glu_b.reshape(1, sw))


def _even_out_kernel(a_ref, b_ref, wa_ref, wb_ref, x_ref, gate_ref, o_ref):
    y = _dot(a_ref[...].astype(BF16), wa_ref[...]) + _dot(b_ref[...].astype(BF16), wb_ref[...])
    o_ref[...] = x_ref[...] + gate_ref[0] * y


def even_out(hy, s5o, w_bf16, x, gate, rows_per_mod, tm, tn):
    n, d = x.shape
    hw = hy.shape[1]
    sw = s5o.shape[1]
    return pl.pallas_call(
        _even_out_kernel, grid=(n // tm, d // tn),
        in_specs=[pl.BlockSpec((tm, hw), lambda i, j: (i, 0)), pl.BlockSpec((tm, sw), lambda i, j: (i, 0)),
                  pl.BlockSpec((hw, tn), lambda i, j: (0, j)), pl.BlockSpec((sw, tn), lambda i, j: (hw // sw, j)),
                  pl.BlockSpec((tm, tn), lambda i, j: (i, j)),
                  pl.BlockSpec((1, 1, tn), lambda i, j: ((i * tm) // rows_per_mod, 0, j))],
        out_specs=pl.BlockSpec((tm, tn), lambda i, j: (i, j)),
        out_shape=jax.ShapeDtypeStruct((n, d), F32), compiler_params=_cp("parallel", "parallel"), name="even_out",
    )(hy, s5o, w_bf16, w_bf16, x, gate)


def _log_sigmoid(x):
    return jnp.minimum(x, 0.0) - jnp.log1p(jnp.exp(-jnp.abs(x)))


def _gates_kernel(x_ref, g_ref, sc_ref, sh_ref, w_ref, b_ref, o_ref):
    h = _normmod(x_ref[...], g_ref[...], sc_ref[0], sh_ref[0]).astype(BF16)
    o_ref[0] = _dot(h, w_ref[0]) + b_ref[0]


def mlstm_gates(x, g, sc, sh, wg_bf16, bg, rows_per_mod, tm):
    n, d = x.shape
    return pl.pallas_call(
        _gates_kernel, grid=(n // tm, 2),
        in_specs=[pl.BlockSpec((tm, d), lambda i, j: (i, 0)), pl.BlockSpec((1, d), lambda i, j: (0, 0)),
                  _mod_spec(d, tm, rows_per_mod), _mod_spec(d, tm, rows_per_mod),
                  pl.BlockSpec((1, d, LANES), lambda i, j: (j, 0, 0)), pl.BlockSpec((1, 1, LANES), lambda i, j: (j, 0, 0))],
        out_specs=pl.BlockSpec((1, tm, LANES), lambda i, j: (j, i, 0)),
        out_shape=jax.ShapeDtypeStruct((2, n, LANES), F32), compiler_params=_cp("parallel", "parallel"),
        name="mlstm_gates",
    )(x, g.reshape(1, d), sc, sh, wg_bf16, bg)


def _mlstm_kernel(q_ref, k_ref, v_ref, g_ref, c0_ref, n0_ref, m0_ref, h_ref, cf_ref, nf_ref, mf_ref,
                  c_s, n_s, m_s, *, nh, dh, tc, nc):
    d = pl.program_id(0)
    c = pl.program_id(2)

    @pl.when(c == 0)
    def _():
        c_s[...] = c0_ref[0, 0]
        n_s[...] = n0_ref[0, 0]
        m_s[...] = m0_ref[0, 0]

    gates = g_ref[0]
    lane = lax.broadcasted_iota(jnp.int32, gates.shape, 1)
    logf = jnp.where(jnp.logical_and(lane >= nh, lane < 2 * nh), _log_sigmoid(gates), 0.0)
    r_i = lax.broadcasted_iota(jnp.int32, (tc, tc), 0)
    s_i = lax.broadcasted_iota(jnp.int32, (tc, tc), 1)
    causal = (r_i - s_i) * (1 - 2 * d) >= 0
    bcum = _dot(causal.astype(F32), logf, precision=HIGHEST)
    btot = jnp.sum(logf, axis=0, keepdims=True)
    gates_t = gates.T
    bcum_t = bcum.T
    for h in range(nh):
        hs = slice(h * dh, (h + 1) * dh)
        q = q_ref[:, hs]
        k = k_ref[:, hs]
        v = v_ref[:, hs]
        b_col = bcum[:, nh + h:nh + h + 1]
        b_row = bcum_t[nh + h:nh + h + 1, :]
        i_col = gates[:, h:h + 1]
        i_row = gates_t[h:h + 1, :]
        m = m_s[h:h + 1, :]
        a = b_col + m
        dmat = jnp.where(causal, b_col - b_row + i_row, NEG_INF)
        mq = jnp.maximum(a, jnp.max(dmat, axis=-1, keepdims=True))
        w_intra = jnp.exp(dmat - mq)
        w_inter = jnp.exp(a - mq)
        s = _dot_nt(q, k) * w_intra
        cm = c_s[h]
        nv = n_s[h:h + 1, :]
        num = _dot(s, v) + w_inter * _dot(q, cm)
        den = jnp.sum(s, axis=-1, keepdims=True) + w_inter * jnp.sum(q * nv, axis=-1, keepdims=True)
        h_ref[0, :, hs] = num / jnp.maximum(jnp.abs(den), jnp.exp(-mq))
        b_last = btot[:, nh + h:nh + h + 1]
        g = b_last - b_col + i_col
        m_new = jnp.maximum(b_last + m, jnp.max(g, axis=0, keepdims=True))
        kw = k * jnp.exp(g - m_new)
        keep = jnp.exp(b_last + m - m_new)
        c_s[h] = keep * cm + _dot(kw.T, v)
        n_s[h:h + 1, :] = keep * nv + jnp.sum(kw, axis=0, keepdims=True)
        m_s[h:h + 1, :] = m_new

    @pl.when(c == nc - 1)
    def _():
        cf_ref[0, 0] = c_s[...]
        nf_ref[0, 0] = n_s[...]
        mf_ref[0, 0] = m_s[...]


def mlstm_scan(qk, proj, vcol, gates, c0, n0, m0, nseq, seq_len, nh, dh):
    tc = ML_CHUNK
    nc = seq_len // tc
    w = nh * dh

    def chunk(d, c):
        return c + d * (nc - 1 - 2 * c)

    rowblk = lambda d, b, c: b * nc + chunk(d, c)
    st = lambda shp: pl.BlockSpec((1, 1) + shp, lambda d, b, c: (b, d) + (0,) * len(shp))
    return pl.pallas_call(
        functools.partial(_mlstm_kernel, nh=nh, dh=dh, tc=tc, nc=nc), grid=(2, nseq, nc),
        in_specs=[pl.BlockSpec((tc, w), lambda d, b, c: (rowblk(d, b, c), 0)),
                  pl.BlockSpec((tc, w), lambda d, b, c: (rowblk(d, b, c), 1)),
                  pl.BlockSpec((tc, w), lambda d, b, c: (rowblk(d, b, c), vcol)),
                  pl.BlockSpec((1, tc, LANES), lambda d, b, c: (d, rowblk(d, b, c), 0)),
                  st((nh, dh, dh)), st((nh, dh)), st((nh, 1))],
        out_specs=[pl.BlockSpec((1, tc, w), lambda d, b, c: (d, rowblk(d, b, c), 0)),
                   st((nh, dh, dh)), st((nh, dh)), st((nh, 1))],
        out_shape=[jax.ShapeDtypeStruct((2, nseq * seq_len, w), F32),
                   jax.ShapeDtypeStruct((nseq, 2, nh, dh, dh), F32), jax.ShapeDtypeStruct((nseq, 2, nh, dh), F32),
                   jax.ShapeDtypeStruct((nseq, 2, nh, 1), F32)],
        scratch_shapes=[pltpu.VMEM((nh, dh, dh), F32), pltpu.VMEM((nh, dh), F32), pltpu.VMEM((nh, 1), F32)],
        compiler_params=_cp("parallel", "parallel", "arbitrary"), name="mlstm_scan",
    )(qk, qk, proj, gates, c0, n0, m0)


def _odd_out_kernel(hf_ref, hb_ref, o_ref, ng_ref, w_ref, x_ref, gate_ref, out_ref, a_s, *, nh, dh):
    @pl.when(pl.program_id(1) == 0)
    def _():
        for h in range(nh):
            hs = slice(h * dh, (h + 1) * dh)
            blk = hf_ref[0, :, hs] + hb_ref[0, :, hs]
            blk = blk * lax.rsqrt(jnp.mean(blk * blk, axis=-1, keepdims=True) + EPS)
            a_s[:, hs] = ((blk * ng_ref[:, hs]) * _silu(o_ref[:, hs])).astype(BF16)
    out_ref[...] = x_ref[...] + gate_ref[0] * _dot(a_s[...], w_ref[...])


def odd_out(h2, proj, ocol, norm_g, w_bf16, x, gate, rows_per_mod, nh, dh, tm, tn):
    n, d = x.shape
    w = nh * dh
    return pl.pallas_call(
        functools.partial(_odd_out_kernel, nh=nh, dh=dh), grid=(n // tm, d // tn),
        in_specs=[pl.BlockSpec((1, tm, w), lambda i, j: (0, i, 0)), pl.BlockSpec((1, tm, w), lambda i, j: (1, i, 0)),
                  pl.BlockSpec((tm, w), lambda i, j: (i, ocol)), pl.BlockSpec((1, w), lambda i, j: (0, 0)),
                  pl.BlockSpec((w, tn), lambda i, j: (0, j)), pl.BlockSpec((tm, tn), lambda i, j: (i, j)),
                  pl.BlockSpec((1, 1, tn), lambda i, j: ((i * tm) // rows_per_mod, 0, j))],
        out_specs=pl.BlockSpec((tm, tn), lambda i, j: (i, j)),
        out_shape=jax.ShapeDtypeStruct((n, d), F32), scratch_shapes=[pltpu.VMEM((tm, w), BF16)],
        compiler_params=_cp("parallel", "arbitrary"), name="odd_out",
    )(h2, h2, proj, norm_g.reshape(1, w), w_bf16, x, gate)


def _top_values(cur, k, out_s):
    for j in range(k):
        m = jnp.max(cur, axis=0, keepdims=True)
        out_s[j:j + 1, :] = m
        cur = jnp.where(cur == m, NEG_INF, cur)


def _pair_candidates(k):
    return [(a, k // (a + 1)) for a in range(k)]


def _peer_score_kernel(h_ref, wq_ref, keys_ref, th_ref, s2_ref, w1_ref, w2_ref, q_s, v1_s, v2_s, cand_s, best_s, *,
                       nh, half, topk):
    q_s[...] = _dot_nt(wq_ref[...], h_ref[...])
    kk = topk + 1
    cand_s[...] = jnp.full(cand_s.shape, NEG_INF, F32)

    def head(hd, carry):
        base = pl.multiple_of(hd * 2 * half, 2 * half)
        s1 = _dot(keys_ref[hd, 0], q_s[pl.ds(base, half), :])
        s2 = _dot(keys_ref[hd, 1], q_s[pl.ds(base + half, half), :])
        _top_values(s1, kk, v1_s)
        _top_values(s2, kk, v2_s)
        off = 0
        for a, cnt in _pair_candidates(kk):
            cand_s[off:off + cnt, :] = v1_s[a:a + 1, :] + v2_s[0:cnt, :]
            off += cnt
        _top_values(cand_s[...], kk, best_s)
        best = best_s[0:topk, :]
        z = jnp.sum(jnp.exp(best - best[0:1, :]), axis=0, keepdims=True)
        tmid = 0.5 * (best_s[topk - 1:topk, :] + best_s[topk:topk + 1, :])
        th = tmid - s1
        w1 = jnp.exp(s1 - v1_s[0:1, :]) / z
        w2 = jnp.exp(s2 - v2_s[0:1, :])
        for lt in range(s1.shape[1] // LANES):
            sl = slice(lt * LANES, (lt + 1) * LANES)
            th_ref[hd, lt] = th[:, sl]
            s2_ref[hd, lt] = s2[:, sl]
            w1_ref[hd, lt] = w1[:, sl]
            w2_ref[hd, lt] = w2[:, sl]
        return carry

    lax.fori_loop(0, nh, head, 0)


def peer_scores(h_bf16, wq_t_bf16, keys, tt):
    n, d = h_bf16.shape
    nh, _, nk, half = keys.shape
    kk = PK_TOPK + 1
    ncand = -(-sum(c for _, c in _pair_candidates(kk)) // 8) * 8
    big = jax.ShapeDtypeStruct((nh, n // LANES, nk, LANES), F32)
    bspec = pl.BlockSpec((nh, tt // LANES, nk, LANES), lambda i: (0, i, 0, 0))
    return pl.pallas_call(
        functools.partial(_peer_score_kernel, nh=nh, half=half, topk=PK_TOPK), grid=(n // tt,),
        in_specs=[pl.BlockSpec((tt, d), lambda i: (i, 0)), pl.BlockSpec((nh * 2 * half, d), lambda i: (0, 0)),
                  pl.BlockSpec((nh, 2, nk, half), lambda i: (0, 0, 0, 0))],
        out_specs=[bspec, bspec, bspec, bspec],
        out_shape=[big, big, big, big],
        scratch_shapes=[pltpu.VMEM((nh * 2 * half, tt), F32), pltpu.VMEM((24, tt), F32), pltpu.VMEM((24, tt), F32),
                        pltpu.VMEM((ncand, tt), F32), pltpu.VMEM((24, tt), F32)],
        compiler_params=_cp("parallel"), name="peer_scores",
    )(h_bf16, wq_t_bf16, keys)


PEER_KEY_ROWS = 32


def _peer_dense_kernel(h_ref, u_ref, vt_ref, th_ref, s2_ref, w1_ref, w2_ref, x_ref, gate_ref, o_ref,
                       acc_s, st_s, wt_s, *, nh, nk, ec, tt):
    e = pl.program_id(1)

    @pl.when(e == 0)
    def _():
        acc_s[...] = jnp.zeros_like(acc_s)

    nlt = tt // LANES
    n_i1 = ec // nk
    nkt = nk // PEER_KEY_ROWS

    def gate_tile(lt, k0):
        g = [jnp.zeros((PEER_KEY_ROWS, LANES), F32) for _ in range(n_i1)]
        for hd in range(nh):
            s2t = s2_ref[hd, lt, pl.ds(k0, PEER_KEY_ROWS), :]
            w2t = w2_ref[hd, lt, pl.ds(k0, PEER_KEY_ROWS), :]
            for li in range(n_i1):
                g[li] = g[li] + jnp.where(s2t >= th_ref[hd, lt, li:li + 1, :], w2t * w1_ref[hd, lt, li:li + 1, :], 0.0)
        return g

    def split(idx):
        return idx // nkt, pl.multiple_of((idx % nkt) * PEER_KEY_ROWS, PEER_KEY_ROWS)

    def rows_of(li, k0):
        return pl.ds(pl.multiple_of(li * nk + k0, PEER_KEY_ROWS), PEER_KEY_ROWS)

    st = jax.nn.gelu(_dot_nt(u_ref[...], h_ref[...]))
    for lt in range(nlt):
        st_s[lt] = st[:, lt * LANES:(lt + 1) * LANES]

    def tile(idx, carry):
        lt, k0 = split(idx)
        g = gate_tile(lt, k0)
        for li in range(n_i1):
            wt_s[lt, rows_of(li, k0), :] = (st_s[lt, rows_of(li, k0), :] * g[li]).astype(BF16)
        return carry

    lax.fori_loop(0, nlt * nkt, tile, 0)
    wt = jnp.concatenate([wt_s[lt] for lt in range(nlt)], axis=1)
    acc_s[...] += _dot(vt_ref[...], wt)

    @pl.when(e == pl.num_programs(1) - 1)
    def _():
        o_ref[...] = x_ref[...] + gate_ref[0] * acc_s[...].T


def peer_dense(h_bf16, u_bf16, vt_bf16, th, s2, w1, w2, x, gate, rows_per_mod, tt, ec):
    n, d = x.shape
    nh, _, nk, _ = s2.shape
    ne = u_bf16.shape[0]
    bspec = pl.BlockSpec((nh, tt // LANES, nk, LANES), lambda i, e: (0, i, 0, 0))
    rspec = pl.BlockSpec((nh, tt // LANES, ec // nk, LANES), lambda i, e: (0, i, e, 0))
    return pl.pallas_call(
        functools.partial(_peer_dense_kernel, nh=nh, nk=nk, ec=ec, tt=tt), grid=(n // tt, ne // ec),
        in_specs=[pl.BlockSpec((tt, d), lambda i, e: (i, 0)), pl.BlockSpec((ec, d), lambda i, e: (e, 0)),
                  pl.BlockSpec((d, ec), lambda i, e: (0, e)), rspec, bspec, rspec, bspec,
                  pl.BlockSpec((tt, d), lambda i, e: (i, 0)),
                  pl.BlockSpec((1, 1, d), lambda i, e: ((i * tt) // rows_per_mod, 0, 0))],
        out_specs=pl.BlockSpec((tt, d), lambda i, e: (i, 0)),
        out_shape=jax.ShapeDtypeStruct((n, d), F32),
        scratch_shapes=[pltpu.VMEM((d, tt), F32), pltpu.VMEM((tt // LANES, ec, LANES), F32),
                        pltpu.VMEM((tt // LANES, ec, LANES), BF16)],
        compiler_params=_cp("parallel", "arbitrary"), name="peer_dense",
    )(h_bf16, u_bf16, vt_bf16, th, s2, w1, w2, x, gate)


def _s5_params(a_re, a_im, b_re, b_im, c_re, c_im, log_step):
    lam = lax.complex(a_re.astype(F32), a_im.astype(F32))
    lam_bar = jnp.exp(lam * jnp.exp(log_step.astype(F32))[..., None])
    b_bar = ((lam_bar - 1.0) / lam)[..., None] * lax.complex(b_re.astype(F32), b_im.astype(F32))
    ngrp, npst, nch = b_bar.shape[1:]
    eye = jnp.eye(ngrp, dtype=F32)

    def b_mat(part):
        return jnp.einsum("dgpj,gh->dgjhp", part, eye).reshape(2, ngrp * nch, ngrp * npst)

    def c_mat(part):
        return jnp.einsum("dgjp,gh->dgphj", part, eye).reshape(2, ngrp * npst, ngrp * nch)

    bre, bim = b_mat(b_bar.real).astype(BF16), b_mat(b_bar.imag).astype(BF16)
    cre, cim = c_mat(c_re.astype(F32)).astype(BF16), c_mat(-c_im.astype(F32)).astype(BF16)
    lam2 = jnp.stack([lam_bar.real.reshape(2, -1), lam_bar.imag.reshape(2, -1)], axis=1)
    return bre, bim, cre, cim, lam2


def _pos_embed(n_tok, d, grid_w):
    rows = n_tok // grid_w
    quarter = d // 4
    omega = 1.0 / (10000.0 ** (jnp.arange(quarter, dtype=F32) / quarter))

    def emb1d(pos):
        ang = pos.astype(F32)[:, None] * omega[None]
        return jnp.concatenate([jnp.sin(ang), jnp.cos(ang)], axis=-1)

    er = emb1d(jnp.arange(rows))
    ec = emb1d(jnp.arange(grid_w))
    half = d // 2
    pe = jnp.concatenate([jnp.broadcast_to(er[:, None], (rows, grid_w, half)),
                          jnp.broadcast_to(ec[None], (rows, grid_w, half))], axis=-1)
    return pe.reshape(rows * grid_w, d)


def _tile(n, pref):
    return pref if n % pref == 0 else n


def _trunk(x, mods, s5_h0, ml_c0, ml_n0, ml_m0, p, nseq, seq_len, rows_per_mod):
    n, d = x.shape
    tm = _tile(min(rows_per_mod, n), 512)
    depth = p["norm_g"].shape[0]
    s5_fin, ml_fin = [], []
    for l in range(depth):
        sh1, sc1, g1, sh2, sc2, g2 = mods[l]
        i = l // 2
        if l % 2 == 0:
            hw = p["hy_bias"].shape[2]
            sw = p["s5_d"].shape[1]
            proj = normmod_matmul(x, p["norm_g"][l, 0], sc1, sh1, p["ev_w_in"][i].astype(BF16), rows_per_mod, tm,
                                  _tile(3 * hw + sw, 512))
            hy_in = short_conv(proj, 3 * hw, p["hy_conv_w"][i], p["hy_conv_b"][i], jnp.ones((3 * hw,), F32),
                               seq_len, act=False)
            cos_t, a_t, a_tt = dft_tables(seq_len)
            tf = _tile(seq_len, 512)
            taps, sumsq = hyena_filter_taps(seq_len, p["hy_w1"][i], p["hy_b1"][i], p["hy_w2"][i], p["hy_b2"][i],
                                            p["hy_w3"][i], p["hy_freq"][i], p["hy_decay"][i], hw)
            kr, ki = hyena_filter_spectrum(cos_t, a_t, taps, sumsq, hw, tf)
            bias = p["hy_bias"][i].astype(F32)
            z, zcol = hy_in, 0
            for o in range(bias.shape[0]):
                yr, yi = hyena_fwd(cos_t, a_t, z, zcol, kr, ki, o, nseq, hw, tf)
                z = hyena_inv(cos_t, a_tt, yr, yi, z, zcol, hy_in, 1 + o, bias[o:o + 1], nseq, hw, tf)
                zcol = 0
            bre, bim, cre, cim, lam2 = _s5_params(p["s5_a_re"][i], p["s5_a_im"][i], p["s5_b_re"][i], p["s5_b_im"][i],
                                                  p["s5_c_re"][i], p["s5_c_im"][i], p["s5_log_step"][i])
            ucol = 3 * hw // sw
            y2, hfin = s5_scan(proj, ucol, bre, bim, cre, cim, lam2, s5_h0[i], nseq, seq_len, sw, _tile(seq_len, 256))
            s5_fin.append(hfin)
            s5o = s5_glu(y2, proj, ucol, p["s5_d"][i], p["s5_glu_w"][i].astype(BF16), p["s5_glu_b"][i], tm)
            x = even_out(z, s5o, p["ev_w_out"][i].astype(BF16), x, g1, rows_per_mod, tm, _tile(d, 512))
        else:
            nh = p["od_gate_b"].shape[2]
            w = p["ml_norm_g"].shape[1]
            dh = w // nh
            w_in = p["od_w_in"][i]
            proj = normmod_matmul(x, p["norm_g"][l, 0], sc1, sh1, w_in[:, :4 * w].astype(BF16), rows_per_mod, tm,
                                  _tile(4 * w, 512))
            wg = w_in[:, 4 * w:].reshape(d, 4, nh)
            gb = p["od_gate_b"][i].astype(F32)
            wg2 = jnp.zeros((2, d, LANES), F32)
            bg2 = jnp.zeros((2, 1, LANES), F32)
            for dr in range(2):
                wg2 = wg2.at[dr, :, :nh].set(wg[:, dr]).at[dr, :, nh:2 * nh].set(wg[:, 2 + dr])
                bg2 = bg2.at[dr, 0, :nh].set(gb[dr]).at[dr, 0, nh:2 * nh].set(gb[2 + dr])
            gates = mlstm_gates(x, p["norm_g"][l, 0], sc1, sh1, wg2.astype(BF16), bg2, rows_per_mod, tm)
            qscale = jnp.concatenate([jnp.full((w,), dh ** -0.5, F32), jnp.ones((w,), F32)])
            qk = short_conv(proj, 2 * w, p["ml_conv_w"][i], p["ml_conv_b"][i], qscale, seq_len, act=True)
            h2, cf, nf, mf = mlstm_scan(qk, proj, 2, gates, ml_c0[i], ml_n0[i], ml_m0[i], nseq, seq_len, nh, dh)
            ml_fin.append((cf, nf, mf))
            x = odd_out(h2, proj, 3, p["ml_norm_g"][i], p["od_w_out"][i].astype(BF16), x, g1, rows_per_mod, nh, dh, tm,
                        _tile(d, 512))
        hn = normmod(x, p["norm_g"][l, 1], sc2, sh2, rows_per_mod, tm)
        tt = _tile(min(rows_per_mod, n), 512)
        th, s2, w1, w2 = peer_scores(hn, p["pk_w_q"][l].T.astype(BF16), p["pk_keys"][l].astype(F32), _tile(tt, 256))
        x = peer_dense(hn, p["pk_u"][l].astype(BF16), p["pk_v"][l].T.astype(BF16), th, s2, w1, w2, x, g2,
                       rows_per_mod, tt, 1024)
    y = final_norm(x, p["final_g"], tm)
    return y, s5_fin, ml_fin


def kernel(x_prompt, x_sample, state_s5_re, state_s5_im, state_mlstm_C, state_mlstm_n, state_mlstm_m, c, c_ctx, norm_g, ada_w, ada_b, final_g, ev_w_in, hy_conv_w, hy_conv_b, hy_w1, hy_b1, hy_w2, hy_b2, hy_w3, hy_freq, hy_decay, hy_bias, s5_a_re, s5_a_im, s5_b_re, s5_b_im, s5_c_re, s5_c_im, s5_log_step, s5_d, s5_glu_w, s5_glu_b, ev_w_out, od_w_in, od_gate_b, ml_conv_w, ml_conv_b, ml_norm_g, od_w_out, pk_w_q, pk_keys, pk_u, pk_v):
    p = dict(norm_g=norm_g, ada_w=ada_w, ada_b=ada_b, final_g=final_g, ev_w_in=ev_w_in,
             hy_conv_w=hy_conv_w, hy_conv_b=hy_conv_b, hy_w1=hy_w1, hy_b1=hy_b1, hy_w2=hy_w2, hy_b2=hy_b2,
             hy_w3=hy_w3, hy_freq=hy_freq, hy_decay=hy_decay, hy_bias=hy_bias, s5_a_re=s5_a_re,
             s5_a_im=s5_a_im, s5_b_re=s5_b_re, s5_b_im=s5_b_im, s5_c_re=s5_c_re, s5_c_im=s5_c_im,
             s5_log_step=s5_log_step, s5_d=s5_d, s5_glu_w=s5_glu_w, s5_glu_b=s5_glu_b, ev_w_out=ev_w_out,
             od_w_in=od_w_in, od_gate_b=od_gate_b, ml_conv_w=ml_conv_w, ml_conv_b=ml_conv_b,
             ml_norm_g=ml_norm_g, od_w_out=od_w_out, pk_w_q=pk_w_q, pk_keys=pk_keys, pk_u=pk_u, pk_v=pk_v)
    nb, seq, d = x_prompt.shape
    db, dseq, _ = x_sample.shape
    depth = norm_g.shape[0]
    n_even, n_odd = (depth + 1) // 2, depth // 2
    assert db + 1 <= 8

    cond8 = jnp.zeros((8, d), F32).at[0].set(c_ctx.astype(F32)).at[1:1 + db].set(c.astype(F32))
    mods_ctx, mods_lat = [], []
    for l in range(depth):
        mod = ada_mod(cond8, ada_w[l].astype(F32), ada_b[l].astype(F32))
        chunks = [mod[:, j * d:(j + 1) * d] for j in range(6)]
        mods_ctx.append([ch[0:1].reshape(1, 1, d) for ch in chunks])
        mods_lat.append([ch[1:1 + db].reshape(db, 1, d) for ch in chunks])

    def s5_state(re, im, bsz):
        return [jnp.stack([re[:, i].reshape(bsz, 2, -1), im[:, i].reshape(bsz, 2, -1)], axis=2).astype(F32)
                for i in range(n_even)]

    ngrp, npst = s5_a_re.shape[2], s5_a_re.shape[3]
    nh, dh = state_mlstm_C.shape[3], state_mlstm_C.shape[4]
    zeros_s5 = jnp.zeros((nb, n_even, 2, ngrp, npst), F32)
    y_prompt, s5_fin, ml_fin = _trunk(
        x_prompt.reshape(nb * seq, d), mods_ctx, s5_state(zeros_s5, zeros_s5, nb),
        [jnp.zeros((nb, 2, nh, dh, dh), F32)] * n_odd, [jnp.zeros((nb, 2, nh, dh), F32)] * n_odd,
        [jnp.zeros((nb, 2, nh, 1), F32)] * n_odd, p, nb, seq, nb * seq)
    x_lat = add_pos(x_sample.reshape(db * dseq, d), _pos_embed(dseq, d, GRID_W), dseq, _tile(dseq, 512))
    y_sample, _, _ = _trunk(
        x_lat, mods_lat, s5_state(state_s5_re, state_s5_im, db),
        [state_mlstm_C[:, i].astype(F32) for i in range(n_odd)], [state_mlstm_n[:, i].astype(F32) for i in range(n_odd)],
        [state_mlstm_m[:, i].astype(F32)[..., None] for i in range(n_odd)], p, db, dseq, dseq)

    new_s5_re = jnp.stack([h[:, :, 0].reshape(nb, 2, ngrp, npst) for h in s5_fin], axis=1)
    new_s5_im = jnp.stack([h[:, :, 1].reshape(nb, 2, ngrp, npst) for h in s5_fin], axis=1)
    new_c = jnp.stack([f[0] for f in ml_fin], axis=1)
    new_n = jnp.stack([f[1] for f in ml_fin], axis=1)
    new_m = jnp.stack([f[2][..., 0] for f in ml_fin], axis=1)
    return (y_prompt.reshape(nb, seq, d), y_sample.reshape(db, dseq, d), new_s5_re, new_s5_im, new_c, new_n, new_m)
```

```python
import functools
import math

import jax
import jax.numpy as jnp
from jax import lax
from jax.experimental import pallas as pl
from jax.experimental.pallas import tpu as pltpu

F32 = jnp.float32
BF16 = jnp.bfloat16
EPS = 1e-6
HIGHEST = lax.Precision.HIGHEST
V7X_VMEM_LIMIT_BYTES = 56 * 1024 * 1024
LANES = 128
SUBLANES = 8
ML_CHUNK = 128
PK_TOPK = 16
GRID_W = 64
NEG_INF = float("-inf")


def _cp(*sem):
    return pltpu.CompilerParams(dimension_semantics=sem, vmem_limit_bytes=V7X_VMEM_LIMIT_BYTES)


def _dot(a, b, **kw):
    return jnp.dot(a, b, preferred_element_type=F32, **kw)


def _dot_nt(a, b):
    return lax.dot_general(a, b, (((1,), (1,)), ((), ())), preferred_element_type=F32)


def _silu(x):
    return x * jax.nn.sigmoid(x)


def _ada_kernel(c_ref, w_ref, b_ref, o_ref):
    o_ref[...] = _dot(_silu(c_ref[...]), w_ref[...], precision=HIGHEST) + b_ref[...]


def ada_mod(cond8, w, b):
    d, no = w.shape
    tn = 1536 if no % 1536 == 0 else no
    return pl.pallas_call(
        _ada_kernel, grid=(no // tn,),
        in_specs=[pl.BlockSpec((8, d), lambda j: (0, 0)), pl.BlockSpec((d, tn), lambda j: (0, j)),
                  pl.BlockSpec((1, tn), lambda j: (0, j))],
        out_specs=pl.BlockSpec((8, tn), lambda j: (0, j)),
        out_shape=jax.ShapeDtypeStruct((8, no), F32), compiler_params=_cp("parallel"), name="ada_mod",
    )(cond8, w, b.reshape(1, no))


def _normmod(x, g, sc, sh):
    y = x * lax.rsqrt(jnp.mean(x * x, axis=-1, keepdims=True) + EPS)
    return (y * g) * (1.0 + sc) + sh


def _mod_spec(d, tm, rows_per_mod, nd=2):
    if nd == 2:
        return pl.BlockSpec((1, 1, d), lambda i, j: ((i * tm) // rows_per_mod, 0, 0))
    return pl.BlockSpec((1, 1, d), lambda i: ((i * tm) // rows_per_mod, 0, 0))


def _nm_matmul_kernel(x_ref, g_ref, sc_ref, sh_ref, w_ref, o_ref, h_ref):
    @pl.when(pl.program_id(1) == 0)
    def _():
        h_ref[...] = _normmod(x_ref[...], g_ref[...], sc_ref[0], sh_ref[0]).astype(BF16)
    o_ref[...] = _dot(h_ref[...], w_ref[...])


def normmod_matmul(x, g, sc, sh, w_bf16, rows_per_mod, tm, tn):
    n, d = x.shape
    no = w_bf16.shape[1]
    return pl.pallas_call(
        _nm_matmul_kernel, grid=(n // tm, no // tn),
        in_specs=[pl.BlockSpec((tm, d), lambda i, j: (i, 0)), pl.BlockSpec((1, d), lambda i, j: (0, 0)),
                  _mod_spec(d, tm, rows_per_mod), _mod_spec(d, tm, rows_per_mod),
                  pl.BlockSpec((d, tn), lambda i, j: (0, j))],
        out_specs=pl.BlockSpec((tm, tn), lambda i, j: (i, j)),
        out_shape=jax.ShapeDtypeStruct((n, no), F32),
        scratch_shapes=[pltpu.VMEM((tm, d), BF16)],
        compiler_params=_cp("parallel", "arbitrary"), name="normmod_matmul",
    )(x, g.reshape(1, d), sc, sh, w_bf16)


def _nm_kernel(x_ref, g_ref, sc_ref, sh_ref, o_ref):
    o_ref[...] = _normmod(x_ref[...], g_ref[...], sc_ref[0], sh_ref[0]).astype(o_ref.dtype)


def normmod(x, g, sc, sh, rows_per_mod, tm):
    n, d = x.shape
    return pl.pallas_call(
        _nm_kernel, grid=(n // tm,),
        in_specs=[pl.BlockSpec((tm, d), lambda i: (i, 0)), pl.BlockSpec((1, d), lambda i: (0, 0)),
                  _mod_spec(d, tm, rows_per_mod, 1), _mod_spec(d, tm, rows_per_mod, 1)],
        out_specs=pl.BlockSpec((tm, d), lambda i: (i, 0)),
        out_shape=jax.ShapeDtypeStruct((n, d), BF16), compiler_params=_cp("parallel"), name="normmod",
    )(x, g.reshape(1, d), sc, sh)


def _final_norm_kernel(x_ref, g_ref, o_ref):
    x = x_ref[...]
    o_ref[...] = (x * lax.rsqrt(jnp.mean(x * x, axis=-1, keepdims=True) + EPS)) * g_ref[...]


def final_norm(x, g, tm):
    n, d = x.shape
    return pl.pallas_call(
        _final_norm_kernel, grid=(n // tm,),
        in_specs=[pl.BlockSpec((tm, d), lambda i: (i, 0)), pl.BlockSpec((1, d), lambda i: (0, 0))],
        out_specs=pl.BlockSpec((tm, d), lambda i: (i, 0)),
        out_shape=jax.ShapeDtypeStruct((n, d), F32), compiler_params=_cp("parallel"), name="final_norm",
    )(x, g.reshape(1, d))


def _add_rows_kernel(x_ref, p_ref, o_ref):
    o_ref[...] = x_ref[...] + p_ref[...]


def add_pos(x, pe, seq_len, tm):
    n, d = x.shape
    nb = seq_len // tm
    return pl.pallas_call(
        _add_rows_kernel, grid=(n // tm,),
        in_specs=[pl.BlockSpec((tm, d), lambda i: (i, 0)), pl.BlockSpec((tm, d), lambda i: (i % nb, 0))],
        out_specs=pl.BlockSpec((tm, d), lambda i: (i, 0)),
        out_shape=jax.ShapeDtypeStruct((n, d), F32), compiler_params=_cp("parallel"), name="add_pos",
    )(x, pe)


def _sconv_kernel(x_ref, w_ref, b_ref, s_ref, o_ref, *, act):
    x = x_ref[...]
    n_tok = x.shape[0]
    row = lax.broadcasted_iota(jnp.int32, x.shape, 0)
    prev = jnp.where(row == 0, 0.0, pltpu.roll(x, 1, 0))
    nxt = jnp.where(row == n_tok - 1, 0.0, pltpu.roll(x, n_tok - 1, 0))
    y = prev * w_ref[0:1, :] + x * w_ref[1:2, :] + nxt * w_ref[2:3, :] + b_ref[...]
    if act:
        y = _silu(y) * s_ref[...]
    o_ref[...] = y


def short_conv(a, ncols, w, b, scale, seq_len, act, cb=256):
    n = a.shape[0]
    return pl.pallas_call(
        functools.partial(_sconv_kernel, act=act), grid=(n // seq_len, ncols // cb),
        in_specs=[pl.BlockSpec((seq_len, cb), lambda s, j: (s, j)), pl.BlockSpec((3, cb), lambda s, j: (0, j)),
                  pl.BlockSpec((1, cb), lambda s, j: (0, j)), pl.BlockSpec((1, cb), lambda s, j: (0, j))],
        out_specs=pl.BlockSpec((seq_len, cb), lambda s, j: (s, j)),
        out_shape=jax.ShapeDtypeStruct((n, ncols), F32), compiler_params=_cp("parallel", "parallel"),
        name="short_conv",
    )(a, w, b.reshape(1, ncols), scale.reshape(1, ncols))


def dft_tables(n_tok):
    k = jnp.arange(n_tok, dtype=jnp.int32)
    blk = 1 << ((n_tok.bit_length() - 1) // 2)
    def thin(n):
        ang = ((k[:, None] * n[None, :]) % (2 * n_tok)).astype(F32) * (math.pi / n_tok)
        return jnp.cos(ang), jnp.sin(ang)
    (c_hi, s_hi), (c_lo, s_lo) = thin(jnp.arange(0, n_tok, blk, dtype=jnp.int32)), thin(jnp.arange(blk, dtype=jnp.int32))
    cos_t = (c_hi[:, :, None] * c_lo[:, None, :] - s_hi[:, :, None] * s_lo[:, None, :]).reshape(n_tok, n_tok)
    msin = -(s_hi[:, :, None] * c_lo[:, None, :] + c_hi[:, :, None] * s_lo[:, None, :]).reshape(n_tok, n_tok)
    alt = jnp.where(k % 2 == 0, 1.0, -1.0).astype(F32)
    a_t = msin.at[0, :].set(alt)
    a_tt = msin.at[:, 0].set(alt)
    return cos_t.astype(BF16), a_t.astype(BF16), a_tt.astype(BF16)


def _hyfilt_kernel(band_ref, w1_ref, b1_ref, w2_ref, b2_ref, w3_ref, fr_ref, dec_ref, h_ref, ss_ref, *,
                   n_tok, tl, hw, nbands):
    i = pl.program_id(0)
    pos = i * tl + lax.broadcasted_iota(jnp.int32, (tl, 1), 0)
    t = pos.astype(F32) / n_tok
    lane = lax.broadcasted_iota(jnp.int32, (tl, LANES), 1)
    ang = 2.0 * math.pi * t * band_ref[...]
    z = jnp.where(lane == 0, t, jnp.where(lane <= nbands, jnp.cos(ang),
                                          jnp.where(lane <= 2 * nbands, jnp.sin(ang), 0.0)))
    fr = fr_ref[...]
    h = jnp.sin(fr * (_dot(z, w1_ref[...], precision=HIGHEST) + b1_ref[...]))
    h = jnp.sin(fr * (_dot(h, w2_ref[...], precision=HIGHEST) + b2_ref[...]))
    h = _dot(h, w3_ref[...], precision=HIGHEST) * jnp.exp(-t * jnp.abs(dec_ref[...]))
    col = lax.broadcasted_iota(jnp.int32, h.shape, 1)
    is_bwd = (col // hw) % 2 == 1
    h = jnp.where(jnp.logical_and(is_bwd, pos == 0), 0.0, h)
    h_ref[...] = h.astype(BF16)

    @pl.when(i == 0)
    def _():
        ss_ref[...] = jnp.zeros_like(ss_ref)
    ss_ref[...] += jnp.sum(h * h, axis=0, keepdims=True)


def hyena_filter_taps(n_tok, w1, b1, w2, b2, w3, freq, decay, hw):
    emb, ffn = w1.shape
    nbands = (emb - 1) // 2
    tl = min(n_tok, 512)
    bands = jnp.linspace(1e-4, nbands - 1, nbands, dtype=F32)
    band_row = jnp.zeros((1, LANES), F32).at[0, 1:1 + nbands].set(bands).at[0, 1 + nbands:1 + 2 * nbands].set(bands)
    w1p = jnp.zeros((LANES, ffn), F32).at[:emb].set(w1)
    nc = w3.shape[1]
    full = lambda shp: pl.BlockSpec(shp, lambda i: (0, 0))
    return pl.pallas_call(
        functools.partial(_hyfilt_kernel, n_tok=n_tok, tl=tl, hw=hw, nbands=nbands), grid=(n_tok // tl,),
        in_specs=[full((1, LANES)), full((LANES, ffn)), full((1, ffn)), full((ffn, ffn)), full((1, ffn)),
                  full((ffn, nc)), full((1, ffn)), full((1, nc))],
        out_specs=[pl.BlockSpec((tl, nc), lambda i: (i, 0)), full((1, nc))],
        out_shape=[jax.ShapeDtypeStruct((n_tok, nc), BF16), jax.ShapeDtypeStruct((1, nc), F32)],
        compiler_params=_cp("arbitrary"), name="hyena_filter_taps",
    )(band_row, w1p, b1.reshape(1, ffn), w2, b2.reshape(1, ffn), w3, freq.reshape(1, ffn), decay.reshape(1, nc))


def _filt_dft_kernel(c_ref, a_ref, h_ref, ss_ref, kr_ref, ki_ref, *, tf, hw):
    i = pl.program_id(1)
    hf = h_ref[:, :hw]
    hb = h_ref[:, hw:]
    cc = c_ref[...]
    aa = a_ref[...]
    zrf, zif, zrb, zib = _dot(cc, hf), _dot(aa, hf), _dot(cc, hb), _dot(aa, hb)
    scale = lax.rsqrt(ss_ref[:, :hw] + ss_ref[:, hw:] + EPS)
    first = (i * tf + lax.broadcasted_iota(jnp.int32, (tf, 1), 0)) == 0
    scale = scale * jnp.where(first, 0.5, 1.0)
    kr_ref[0] = (zrf + zrb) * scale
    ki_ref[0] = jnp.where(first, zif + zib, zif - zib) * scale


def hyena_filter_spectrum(cos_t, a_t, taps, sumsq, hw, tf):
    n_tok = cos_t.shape[0]
    norder = taps.shape[1] // (2 * hw)
    out = jax.ShapeDtypeStruct((norder, n_tok, hw), F32)
    return pl.pallas_call(
        functools.partial(_filt_dft_kernel, tf=tf, hw=hw), grid=(norder, n_tok // tf),
        in_specs=[pl.BlockSpec((tf, n_tok), lambda o, i: (i, 0)), pl.BlockSpec((tf, n_tok), lambda o, i: (i, 0)),
                  pl.BlockSpec((n_tok, 2 * hw), lambda o, i: (0, o)), pl.BlockSpec((1, 2 * hw), lambda o, i: (0, o))],
        out_specs=[pl.BlockSpec((1, tf, hw), lambda o, i: (o, i, 0))] * 2,
        out_shape=[out, out], compiler_params=_cp("parallel", "parallel"), name="hyena_filter_spectrum",
    )(cos_t, a_t, taps, sumsq)


def _hy_fwd_kernel(c_ref, a_ref, z_ref, kr_ref, ki_ref, yr_ref, yi_ref, *, tf):
    i = pl.program_id(0)
    zb = z_ref[...].astype(BF16)
    zr = _dot(c_ref[...], zb)
    zi = _dot(a_ref[...], zb)
    kr = kr_ref[0]
    ki = ki_ref[0]
    first = (i * tf + lax.broadcasted_iota(jnp.int32, (tf, 1), 0)) == 0
    yr_ref[...] = jnp.where(first, zr * kr, zr * kr - zi * ki).astype(BF16)
    yi_ref[...] = jnp.where(first, zi * ki, zr * ki + zi * kr).astype(BF16)


def hyena_fwd(cos_t, a_t, z, zcol, kr, ki, order, nseq, hw, tf):
    n_tok = cos_t.shape[0]
    nf = n_tok // tf
    out = jax.ShapeDtypeStruct((nseq * n_tok, hw), BF16)
    return pl.pallas_call(
        functools.partial(_hy_fwd_kernel, tf=tf), grid=(nf, nseq),
        in_specs=[pl.BlockSpec((tf, n_tok), lambda i, b: (i, 0)), pl.BlockSpec((tf, n_tok), lambda i, b: (i, 0)),
                  pl.BlockSpec((n_tok, hw), lambda i, b: (b, zcol)),
                  pl.BlockSpec((1, tf, hw), lambda i, b: (order, i, 0)),
                  pl.BlockSpec((1, tf, hw), lambda i, b: (order, i, 0))],
        out_specs=[pl.BlockSpec((tf, hw), lambda i, b: (b * nf + i, 0))] * 2,
        out_shape=[out, out], compiler_params=_cp("parallel", "parallel"), name="hyena_fwd",
    )(cos_t, a_t, z, kr, ki)


def _hy_inv_kernel(c_ref, at_ref, yr_ref, yi_ref, zp_ref, gate_ref, bias_ref, o_ref, *, inv_len):
    conv = (_dot(c_ref[...], yr_ref[...]) + _dot(at_ref[...], yi_ref[...])) * inv_len
    o_ref[...] = gate_ref[...] * (conv + bias_ref[...] * zp_ref[...])


def hyena_inv(cos_t, a_tt, yr, yi, zprev, zcol, gates, gcol, bias_row, nseq, hw, tf):
    n_tok = cos_t.shape[0]
    nf = n_tok // tf
    return pl.pallas_call(
        functools.partial(_hy_inv_kernel, inv_len=1.0 / n_tok), grid=(nf, nseq),
        in_specs=[pl.BlockSpec((tf, n_tok), lambda i, b: (i, 0)), pl.BlockSpec((tf, n_tok), lambda i, b: (i, 0)),
                  pl.BlockSpec((n_tok, hw), lambda i, b: (b, 0)), pl.BlockSpec((n_tok, hw), lambda i, b: (b, 0)),
                  pl.BlockSpec((tf, hw), lambda i, b: (b * nf + i, zcol)),
                  pl.BlockSpec((tf, hw), lambda i, b: (b * nf + i, gcol)),
                  pl.BlockSpec((1, hw), lambda i, b: (0, 0))],
        out_specs=pl.BlockSpec((tf, hw), lambda i, b: (b * nf + i, 0)),
        out_shape=jax.ShapeDtypeStruct((nseq * n_tok, hw), F32),
        compiler_params=_cp("parallel", "parallel"), name="hyena_inv",
    )(cos_t, a_tt, yr, yi, zprev, gates, bias_row)


def _s5_kernel(u_ref, bre_ref, bim_ref, cre_ref, cim_ref, lam_ref, h0_ref, y_ref, hfin_ref, hre_s, him_s, st_s, *,
               tc, nc, ns):
    d = pl.program_id(0)
    c = pl.program_id(2)

    @pl.when(c == 0)
    def _():
        st_s[...] = h0_ref[0, 0]

    ub = u_ref[...].astype(BF16)
    hre_s[...] = _dot(ub, bre_ref[0])
    him_s[...] = _dot(ub, bim_ref[0])
    lr = lam_ref[0, 0:1, :]
    li = lam_ref[0, 1:2, :]

    def body(t, carry):
        hr, hi = carry
        r = jnp.where(d == 0, t, tc - 1 - t)
        nr = lr * hr - li * hi + hre_s[pl.ds(r, 1), :]
        ni = lr * hi + li * hr + him_s[pl.ds(r, 1), :]
        hre_s[pl.ds(r, 1), :] = nr
        him_s[pl.ds(r, 1), :] = ni
        return nr, ni

    hr, hi = lax.fori_loop(0, tc, body, (st_s[0:1, :], st_s[1:2, :]), unroll=8)
    st_s[0:1, :] = hr
    st_s[1:2, :] = hi
    y_ref[0] = _dot(hre_s[...].astype(BF16), cre_ref[0]) + _dot(him_s[...].astype(BF16), cim_ref[0])

    @pl.when(c == nc - 1)
    def _():
        hfin_ref[0, 0] = st_s[...]


def s5_scan(proj, ucol, bre, bim, cre, cim, lam, h0, nseq, seq_len, sw, tc):
    ns = bre.shape[2]
    nc = seq_len // tc

    def chunk(d, c):
        return c + d * (nc - 1 - 2 * c)

    return pl.pallas_call(
        functools.partial(_s5_kernel, tc=tc, nc=nc, ns=ns), grid=(2, nseq, nc),
        in_specs=[pl.BlockSpec((tc, sw), lambda d, b, c: (b * nc + chunk(d, c), ucol)),
                  pl.BlockSpec((1, sw, ns), lambda d, b, c: (d, 0, 0)),
                  pl.BlockSpec((1, sw, ns), lambda d, b, c: (d, 0, 0)),
                  pl.BlockSpec((1, ns, sw), lambda d, b, c: (d, 0, 0)),
                  pl.BlockSpec((1, ns, sw), lambda d, b, c: (d, 0, 0)),
                  pl.BlockSpec((1, 2, ns), lambda d, b, c: (d, 0, 0)),
                  pl.BlockSpec((1, 1, 2, ns), lambda d, b, c: (b, d, 0, 0))],
        out_specs=[pl.BlockSpec((1, tc, sw), lambda d, b, c: (d, b * nc + chunk(d, c), 0)),
                   pl.BlockSpec((1, 1, 2, ns), lambda d, b, c: (b, d, 0, 0))],
        out_shape=[jax.ShapeDtypeStruct((2, nseq * seq_len, sw), F32), jax.ShapeDtypeStruct((nseq, 2, 2, ns), F32)],
        scratch_shapes=[pltpu.VMEM((tc, ns), F32), pltpu.VMEM((tc, ns), F32), pltpu.VMEM((2, ns), F32)],
        compiler_params=_cp("parallel", "parallel", "arbitrary"), name="s5_scan",
    )(proj, bre, bim, cre, cim, lam, h0)


def _s5_glu_kernel(yf_ref, yb_ref, u_ref, d_ref, w_ref, b_ref, o_ref):
    y = jax.nn.gelu(yf_ref[0] + yb_ref[0] + d_ref[...] * u_ref[...])
    o_ref[...] = y * jax.nn.sigmoid(_dot(y.astype(BF16), w_ref[...]) + b_ref[...])


def s5_glu(y2, proj, ucol, d_skip, glu_w_bf16, glu_b, tm):
    _, n, sw = y2.shape
    return pl.pallas_call(
        _s5_glu_kernel, grid=(n // tm,),
        in_specs=[pl.BlockSpec((1, tm, sw), lambda i: (0, i, 0)), pl.BlockSpec((1, tm, sw), lambda i: (1, i, 0)),
                  pl.BlockSpec((tm, sw), lambda i: (i, ucol)), pl.BlockSpec((1, sw), lambda i: (0, 0)),
                  pl.BlockSpec((sw, sw), lambda i: (0, 0)), pl.BlockSpec((1, sw), lambda i: (0, 0))],
        out_specs=pl.BlockSpec((tm, sw), lambda i: (i, 0)),
        out_shape=jax.ShapeDtypeStruct((n, sw), F32), compiler_params=_cp("parallel"), name="s5_glu",
    )(y2, y2, proj, d_skip.reshape(1, sw), glu_w_bf16, glu_b.reshape(1, sw))


def _even_out_kernel(a_ref, b_ref, wa_ref, wb_ref, x_ref, gate_ref, o_ref):
    y = _dot(a_ref[...].astype(BF16), wa_ref[...]) + _dot(b_ref[...].astype(BF16), wb_ref[...])
    o_ref[...] = x_ref[...] + gate_ref[0] * y


def even_out(hy, s5o, w_bf16, x, gate, rows_per_mod, tm, tn):
    n, d = x.shape
    hw = hy.shape[1]
    sw = s5o.shape[1]
    return pl.pallas_call(
        _even_out_kernel, grid=(n // tm, d // tn),
        in_specs=[pl.BlockSpec((tm, hw), lambda i, j: (i, 0)), pl.BlockSpec((tm, sw), lambda i, j: (i, 0)),
                  pl.BlockSpec((hw, tn), lambda i, j: (0, j)), pl.BlockSpec((sw, tn), lambda i, j: (hw // sw, j)),
                  pl.BlockSpec((tm, tn), lambda i, j: (i, j)),
                  pl.BlockSpec((1, 1, tn), lambda i, j: ((i * tm) // rows_per_mod, 0, j))],
        out_specs=pl.BlockSpec((tm, tn), lambda i, j: (i, j)),
        out_shape=jax.ShapeDtypeStruct((n, d), F32), compiler_params=_cp("parallel", "parallel"), name="even_out",
    )(hy, s5o, w_bf16, w_bf16, x, gate)


def _log_sigmoid(x):
    return jnp.minimum(x, 0.0) - jnp.log1p(jnp.exp(-jnp.abs(x)))


def _gates_kernel(x_ref, g_ref, sc_ref, sh_ref, w_ref, b_ref, o_ref):
    h = _normmod(x_ref[...], g_ref[...], sc_ref[0], sh_ref[0]).astype(BF16)
    o_ref[0] = _dot(h, w_ref[0]) + b_ref[0]


def mlstm_gates(x, g, sc, sh, wg_bf16, bg, rows_per_mod, tm):
    n, d = x.shape
    return pl.pallas_call(
        _gates_kernel, grid=(n // tm, 2),
        in_specs=[pl.BlockSpec((tm, d), lambda i, j: (i, 0)), pl.BlockSpec((1, d), lambda i, j: (0, 0)),
                  _mod_spec(d, tm, rows_per_mod), _mod_spec(d, tm, rows_per_mod),
                  pl.BlockSpec((1, d, LANES), lambda i, j: (j, 0, 0)), pl.BlockSpec((1, 1, LANES), lambda i, j: (j, 0, 0))],
        out_specs=pl.BlockSpec((1, tm, LANES), lambda i, j: (j, i, 0)),
        out_shape=jax.ShapeDtypeStruct((2, n, LANES), F32), compiler_params=_cp("parallel", "parallel"),
        name="mlstm_gates",
    )(x, g.reshape(1, d), sc, sh, wg_bf16, bg)


def _mlstm_kernel(q_ref, k_ref, v_ref, g_ref, c0_ref, n0_ref, m0_ref, h_ref, cf_ref, nf_ref, mf_ref,
                  c_s, n_s, m_s, *, nh, dh, tc, nc):
    d = pl.program_id(0)
    c = pl.program_id(2)

    @pl.when(c == 0)
    def _():
        c_s[...] = c0_ref[0, 0]
        n_s[...] = n0_ref[0, 0]
        m_s[...] = m0_ref[0, 0]

    gates = g_ref[0]
    lane = lax.broadcasted_iota(jnp.int32, gates.shape, 1)
    logf = jnp.where(jnp.logical_and(lane >= nh, lane < 2 * nh), _log_sigmoid(gates), 0.0)
    r_i = lax.broadcasted_iota(jnp.int32, (tc, tc), 0)
    s_i = lax.broadcasted_iota(jnp.int32, (tc, tc), 1)
    causal = (r_i - s_i) * (1 - 2 * d) >= 0
    bcum = _dot(causal.astype(F32), logf, precision=HIGHEST)
    btot = jnp.sum(logf, axis=0, keepdims=True)
    gates_t = gates.T
    bcum_t = bcum.T
    for h in range(nh):
        hs = slice(h * dh, (h + 1) * dh)
        q = q_ref[:, hs]
        k = k_ref[:, hs]
        v = v_ref[:, hs]
        b_col = bcum[:, nh + h:nh + h + 1]
        b_row = bcum_t[nh + h:nh + h + 1, :]
        i_col = gates[:, h:h + 1]
        i_row = gates_t[h:h + 1, :]
        m = m_s[h:h + 1, :]
        a = b_col + m
        dmat = jnp.where(causal, b_col - b_row + i_row, NEG_INF)
        mq = jnp.maximum(a, jnp.max(dmat, axis=-1, keepdims=True))
        w_intra = jnp.exp(dmat - mq)
        w_inter = jnp.exp(a - mq)
        s = _dot_nt(q, k) * w_intra
        cm = c_s[h]
        nv = n_s[h:h + 1, :]
        num = _dot(s, v) + w_inter * _dot(q, cm)
        den = jnp.sum(s, axis=-1, keepdims=True) + w_inter * jnp.sum(q * nv, axis=-1, keepdims=True)
        h_ref[0, :, hs] = num / jnp.maximum(jnp.abs(den), jnp.exp(-mq))
        b_last = btot[:, nh + h:nh + h + 1]
        g = b_last - b_col + i_col
        m_new = jnp.maximum(b_last + m, jnp.max(g, axis=0, keepdims=True))
        kw = k * jnp.exp(g - m_new)
        keep = jnp.exp(b_last + m - m_new)
        c_s[h] = keep * cm + _dot(kw.T, v)
        n_s[h:h + 1, :] = keep * nv + jnp.sum(kw, axis=0, keepdims=True)
        m_s[h:h + 1, :] = m_new

    @pl.when(c == nc - 1)
    def _():
        cf_ref[0, 0] = c_s[...]
        nf_ref[0, 0] = n_s[...]
        mf_ref[0, 0] = m_s[...]


def mlstm_scan(qk, proj, vcol, gates, c0, n0, m0, nseq, seq_len, nh, dh):
    tc = ML_CHUNK
    nc = seq_len // tc
    w = nh * dh

    def chunk(d, c):
        return c + d * (nc - 1 - 2 * c)

    rowblk = lambda d, b, c: b * nc + chunk(d, c)
    st = lambda shp: pl.BlockSpec((1, 1) + shp, lambda d, b, c: (b, d) + (0,) * len(shp))
    return pl.pallas_call(
        functools.partial(_mlstm_kernel, nh=nh, dh=dh, tc=tc, nc=nc), grid=(2, nseq, nc),
        in_specs=[pl.BlockSpec((tc, w), lambda d, b, c: (rowblk(d, b, c), 0)),
                  pl.BlockSpec((tc, w), lambda d, b, c: (rowblk(d, b, c), 1)),
                  pl.BlockSpec((tc, w), lambda d, b, c: (rowblk(d, b, c), vcol)),
                  pl.BlockSpec((1, tc, LANES), lambda d, b, c: (d, rowblk(d, b, c), 0)),
                  st((nh, dh, dh)), st((nh, dh)), st((nh, 1))],
        out_specs=[pl.BlockSpec((1, tc, w), lambda d, b, c: (d, rowblk(d, b, c), 0)),
                   st((nh, dh, dh)), st((nh, dh)), st((nh, 1))],
        out_shape=[jax.ShapeDtypeStruct((2, nseq * seq_len, w), F32),
                   jax.ShapeDtypeStruct((nseq, 2, nh, dh, dh), F32), jax.ShapeDtypeStruct((nseq, 2, nh, dh), F32),
                   jax.ShapeDtypeStruct((nseq, 2, nh, 1), F32)],
        scratch_shapes=[pltpu.VMEM((nh, dh, dh), F32), pltpu.VMEM((nh, dh), F32), pltpu.VMEM((nh, 1), F32)],
        compiler_params=_cp("parallel", "parallel", "arbitrary"), name="mlstm_scan",
    )(qk, qk, proj, gates, c0, n0, m0)


def _odd_out_kernel(hf_ref, hb_ref, o_ref, ng_ref, w_ref, x_ref, gate_ref, out_ref, a_s, *, nh, dh):
    @pl.when(pl.program_id(1) == 0)
    def _():
        for h in range(nh):
            hs = slice(h * dh, (h + 1) * dh)
            blk = hf_ref[0, :, hs] + hb_ref[0, :, hs]
            blk = blk * lax.rsqrt(jnp.mean(blk * blk, axis=-1, keepdims=True) + EPS)
            a_s[:, hs] = ((blk * ng_ref[:, hs]) * _silu(o_ref[:, hs])).astype(BF16)
    out_ref[...] = x_ref[...] + gate_ref[0] * _dot(a_s[...], w_ref[...])


def odd_out(h2, proj, ocol, norm_g, w_bf16, x, gate, rows_per_mod, nh, dh, tm, tn):
    n, d = x.shape
    w = nh * dh
    return pl.pallas_call(
        functools.partial(_odd_out_kernel, nh=nh, dh=dh), grid=(n // tm, d // tn),
        in_specs=[pl.BlockSpec((1, tm, w), lambda i, j: (0, i, 0)), pl.BlockSpec((1, tm, w), lambda i, j: (1, i, 0)),
                  pl.BlockSpec((tm, w), lambda i, j: (i, ocol)), pl.BlockSpec((1, w), lambda i, j: (0, 0)),
                  pl.BlockSpec((w, tn), lambda i, j: (0, j)), pl.BlockSpec((tm, tn), lambda i, j: (i, j)),
                  pl.BlockSpec((1, 1, tn), lambda i, j: ((i * tm) // rows_per_mod, 0, j))],
        out_specs=pl.BlockSpec((tm, tn), lambda i, j: (i, j)),
        out_shape=jax.ShapeDtypeStruct((n, d), F32), scratch_shapes=[pltpu.VMEM((tm, w), BF16)],
        compiler_params=_cp("parallel", "arbitrary"), name="odd_out",
    )(h2, h2, proj, norm_g.reshape(1, w), w_bf16, x, gate)


def _top_values(curs, k, outs):
    curs = list(curs)
    for j in range(k):
        for a, out_s in enumerate(outs):
            m = jnp.max(curs[a], axis=0, keepdims=True)
            out_s[j:j + 1, :] = m
            curs[a] = jnp.where(curs[a] == m, NEG_INF, curs[a])


def _pair_candidates(k):
    return [(a, k // (a + 1)) for a in range(k)]


def _peer_score_kernel(h_ref, wq_ref, keys_ref, th_ref, s2_ref, w1_ref, w2_ref, q_s, v1_s, v2_s, cand_s, best_s, *,
                       nh, half, topk):
    q_s[...] = _dot_nt(wq_ref[...], h_ref[...])
    kk = topk + 1
    cand_s[...] = jnp.full(cand_s.shape, NEG_INF, F32)

    def head(hd, carry):
        base = pl.multiple_of(hd * 2 * half, 2 * half)
        s1 = _dot(keys_ref[hd, 0], q_s[pl.ds(base, half), :])
        s2 = _dot(keys_ref[hd, 1], q_s[pl.ds(base + half, half), :])
        _top_values((s1, s2), kk, (v1_s, v2_s))
        off = 0
        for a, cnt in _pair_candidates(kk):
            cand_s[off:off + cnt, :] = v1_s[a:a + 1, :] + v2_s[0:cnt, :]
            off += cnt
        _top_values((cand_s[...],), kk, (best_s,))
        best = best_s[0:topk, :]
        z = jnp.sum(jnp.exp(best - best[0:1, :]), axis=0, keepdims=True)
        tmid = 0.5 * (best_s[topk - 1:topk, :] + best_s[topk:topk + 1, :])
        th = tmid - s1
        w1 = jnp.exp(s1 - v1_s[0:1, :]) / z
        w2 = jnp.exp(s2 - v2_s[0:1, :])
        for lt in range(s1.shape[1] // LANES):
            sl = slice(lt * LANES, (lt + 1) * LANES)
            th_ref[hd, lt] = th[:, sl]
            s2_ref[hd, lt] = s2[:, sl]
            w1_ref[hd, lt] = w1[:, sl]
            w2_ref[hd, lt] = w2[:, sl]
        return carry

    lax.fori_loop(0, nh, head, 0)


def peer_scores(h_bf16, wq_t_bf16, keys, tt):
    n, d = h_bf16.shape
    nh, _, nk, half = keys.shape
    kk = PK_TOPK + 1
    ncand = -(-sum(c for _, c in _pair_candidates(kk)) // 8) * 8
    big = jax.ShapeDtypeStruct((nh, n // LANES, nk, LANES), F32)
    bspec = pl.BlockSpec((nh, tt // LANES, nk, LANES), lambda i: (0, i, 0, 0))
    return pl.pallas_call(
        functools.partial(_peer_score_kernel, nh=nh, half=half, topk=PK_TOPK), grid=(n // tt,),
        in_specs=[pl.BlockSpec((tt, d), lambda i: (i, 0)), pl.BlockSpec((nh * 2 * half, d), lambda i: (0, 0)),
                  pl.BlockSpec((nh, 2, nk, half), lambda i: (0, 0, 0, 0))],
        out_specs=[bspec, bspec, bspec, bspec],
        out_shape=[big, big, big, big],
        scratch_shapes=[pltpu.VMEM((nh * 2 * half, tt), F32), pltpu.VMEM((24, tt), F32), pltpu.VMEM((24, tt), F32),
                        pltpu.VMEM((ncand, tt), F32), pltpu.VMEM((24, tt), F32)],
        compiler_params=_cp("parallel"), name="peer_scores",
    )(h_bf16, wq_t_bf16, keys)


PEER_KEY_ROWS = 16
PEER_MXU_ROWS = 256
PEER_TILES_WITH_SCORES = 12
GELU_C1 = math.sqrt(2.0 / math.pi)
GELU_C2 = 0.044715 * GELU_C1


def _gelu_tanh(x):
    half_x = 0.5 * x
    return half_x + half_x * jnp.tanh(x * (GELU_C1 + GELU_C2 * (x * x)))


def _peer_dense_kernel(h_ref, u_ref, vt_ref, th_ref, s2_ref, w1_ref, w2_ref, x_ref, gate_ref, o_ref,
                       acc_s, st_s, g_s, wt_s, *, nh, nk, ec, tt, nchunk):
    e = pl.program_id(1)
    nlt = tt // LANES
    n_i1 = ec // nk
    nkt = nk // PEER_KEY_ROWS
    d_model = acc_s.shape[0]
    cur = e % 2

    @pl.when(e == 0)
    def _():
        acc_s[...] = jnp.zeros_like(acc_s)
        wt_s[1] = jnp.zeros(wt_s.shape[1:], BF16)

    def gate_tile(idx):
        lt, ks = idx // nkt, slice((idx % nkt) * PEER_KEY_ROWS, (idx % nkt + 1) * PEER_KEY_ROWS)
        subs = [slice(r, r + SUBLANES) for r in range(ks.start, ks.stop, SUBLANES)]
        g = [[jnp.zeros((SUBLANES, LANES), F32) for _ in subs] for _ in range(n_i1)]
        for hd in range(nh):
            s2t = [s2_ref[hd, lt, sub, :] for sub in subs]
            w2t = [w2_ref[hd, lt, sub, :] for sub in subs]
            for li in range(n_i1):
                thb = jnp.broadcast_to(th_ref[hd, lt, li:li + 1, :], (SUBLANES, LANES))
                w1b = jnp.broadcast_to(w1_ref[hd, lt, li:li + 1, :], (SUBLANES, LANES))
                for j in range(len(subs)):
                    g[li][j] = g[li][j] + jnp.where(s2t[j] >= thb, w2t[j] * w1b, 0.0)
        for li in range(n_i1):
            for j, sub in enumerate(subs):
                g_s[lt, li * nk + sub.start:li * nk + sub.stop, :] = g[li][j]

    def spread(tiles, nslices):
        return [tiles[j::nslices] for j in range(nslices)]

    ntile = nlt * nkt
    slices1 = ec // PEER_MXU_ROWS
    slices2 = d_model // PEER_MXU_ROWS

    @pl.when(e < nchunk)
    def _():
        hb = h_ref[...]
        for j, tiles in enumerate(spread(list(range(PEER_TILES_WITH_SCORES)), slices1)):
            rs = slice(j * PEER_MXU_ROWS, (j + 1) * PEER_MXU_ROWS)
            st = _gelu_tanh(_dot_nt(u_ref[rs, :], hb))
            for lt in range(nlt):
                st_s[lt, rs, :] = st[:, lt * LANES:(lt + 1) * LANES]
            for idx in tiles:
                gate_tile(idx)

    wt_prev = jnp.concatenate([wt_s[1 - cur, lt] for lt in range(nlt)], axis=1)
    for j, tiles in enumerate(spread(list(range(PEER_TILES_WITH_SCORES, ntile)), slices2)):
        rs = slice(j * PEER_MXU_ROWS, (j + 1) * PEER_MXU_ROWS)
        acc_s[rs, :] += _dot(vt_ref[rs, :], wt_prev)
        for idx in tiles:
            gate_tile(idx)
    def finish(lt, carry):
        wt_s[cur, lt] = (st_s[lt] * g_s[lt]).astype(BF16)
        return carry

    lax.fori_loop(0, nlt, finish, 0)

    @pl.when(e == nchunk)
    def _():
        o_ref[...] = x_ref[...] + gate_ref[0] * acc_s[...].T


def peer_dense(h_bf16, u_bf16, vt_bf16, th, s2, w1, w2, x, gate, rows_per_mod, tt, ec):
    n, d = x.shape
    nh, _, nk, _ = s2.shape
    nchunk = u_bf16.shape[0] // ec
    this = lambda e: jnp.minimum(e, nchunk - 1)
    prev = lambda e: jnp.maximum(e - 1, 0)
    bspec = pl.BlockSpec((nh, tt // LANES, nk, LANES), lambda i, e: (0, i, 0, 0))
    rspec = pl.BlockSpec((nh, tt // LANES, ec // nk, LANES), lambda i, e: (0, i, this(e), 0))
    tile_buf = (tt // LANES, ec, LANES)
    return pl.pallas_call(
        functools.partial(_peer_dense_kernel, nh=nh, nk=nk, ec=ec, tt=tt, nchunk=nchunk), grid=(n // tt, nchunk + 1),
        in_specs=[pl.BlockSpec((tt, d), lambda i, e: (i, 0)), pl.BlockSpec((ec, d), lambda i, e: (this(e), 0)),
                  pl.BlockSpec((d, ec), lambda i, e: (0, prev(e))), rspec, bspec, rspec, bspec,
                  pl.BlockSpec((tt, d), lambda i, e: (i, 0)),
                  pl.BlockSpec((1, 1, d), lambda i, e: ((i * tt) // rows_per_mod, 0, 0))],
        out_specs=pl.BlockSpec((tt, d), lambda i, e: (i, 0)),
        out_shape=jax.ShapeDtypeStruct((n, d), F32),
        scratch_shapes=[pltpu.VMEM((d, tt), F32), pltpu.VMEM(tile_buf, F32), pltpu.VMEM(tile_buf, F32),
                        pltpu.VMEM((2,) + tile_buf, BF16)],
        compiler_params=_cp("parallel", "arbitrary"), name="peer_dense",
    )(h_bf16, u_bf16, vt_bf16, th, s2, w1, w2, x, gate)


def _s5_params(a_re, a_im, b_re, b_im, c_re, c_im, log_step):
    lam = lax.complex(a_re.astype(F32), a_im.astype(F32))
    lam_bar = jnp.exp(lam * jnp.exp(log_step.astype(F32))[..., None])
    b_bar = ((lam_bar - 1.0) / lam)[..., None] * lax.complex(b_re.astype(F32), b_im.astype(F32))
    ngrp, npst, nch = b_bar.shape[1:]
    eye = jnp.eye(ngrp, dtype=F32)

    def b_mat(part):
        return jnp.einsum("dgpj,gh->dgjhp", part, eye).reshape(2, ngrp * nch, ngrp * npst)

    def c_mat(part):
        return jnp.einsum("dgjp,gh->dgphj", part, eye).reshape(2, ngrp * npst, ngrp * nch)

    bre, bim = b_mat(b_bar.real).astype(BF16), b_mat(b_bar.imag).astype(BF16)
    cre, cim = c_mat(c_re.astype(F32)).astype(BF16), c_mat(-c_im.astype(F32)).astype(BF16)
    lam2 = jnp.stack([lam_bar.real.reshape(2, -1), lam_bar.imag.reshape(2, -1)], axis=1)
    return bre, bim, cre, cim, lam2


def _pos_embed(n_tok, d, grid_w):
    rows = n_tok // grid_w
    quarter = d // 4
    omega = 1.0 / (10000.0 ** (jnp.arange(quarter, dtype=F32) / quarter))

    def emb1d(pos):
        ang = pos.astype(F32)[:, None] * omega[None]
        return jnp.concatenate([jnp.sin(ang), jnp.cos(ang)], axis=-1)

    er = emb1d(jnp.arange(rows))
    ec = emb1d(jnp.arange(grid_w))
    half = d // 2
    pe = jnp.concatenate([jnp.broadcast_to(er[:, None], (rows, grid_w, half)),
                          jnp.broadcast_to(ec[None], (rows, grid_w, half))], axis=-1)
    return pe.reshape(rows * grid_w, d)


def _tile(n, pref):
    return pref if n % pref == 0 else n


def _trunk(x, mods, s5_h0, ml_c0, ml_n0, ml_m0, p, nseq, seq_len, rows_per_mod):
    n, d = x.shape
    tm = _tile(min(rows_per_mod, n), 512)
    depth = p["norm_g"].shape[0]
    s5_fin, ml_fin = [], []
    for l in range(depth):
        sh1, sc1, g1, sh2, sc2, g2 = mods[l]
        i = l // 2
        if l % 2 == 0:
            hw = p["hy_bias"].shape[2]
            sw = p["s5_d"].shape[1]
            proj = normmod_matmul(x, p["norm_g"][l, 0], sc1, sh1, p["ev_w_in"][i].astype(BF16), rows_per_mod, tm,
                                  _tile(3 * hw + sw, 512))
            hy_in = short_conv(proj, 3 * hw, p["hy_conv_w"][i], p["hy_conv_b"][i], jnp.ones((3 * hw,), F32),
                               seq_len, act=False)
            cos_t, a_t, a_tt = dft_tables(seq_len)
            tf = _tile(seq_len, 512)
            taps, sumsq = hyena_filter_taps(seq_len, p["hy_w1"][i], p["hy_b1"][i], p["hy_w2"][i], p["hy_b2"][i],
                                            p["hy_w3"][i], p["hy_freq"][i], p["hy_decay"][i], hw)
            kr, ki = hyena_filter_spectrum(cos_t, a_t, taps, sumsq, hw, tf)
            bias = p["hy_bias"][i].astype(F32)
            z, zcol = hy_in, 0
            for o in range(bias.shape[0]):
                yr, yi = hyena_fwd(cos_t, a_t, z, zcol, kr, ki, o, nseq, hw, tf)
                z = hyena_inv(cos_t, a_tt, yr, yi, z, zcol, hy_in, 1 + o, bias[o:o + 1], nseq, hw, tf)
                zcol = 0
            bre, bim, cre, cim, lam2 = _s5_params(p["s5_a_re"][i], p["s5_a_im"][i], p["s5_b_re"][i], p["s5_b_im"][i],
                                                  p["s5_c_re"][i], p["s5_c_im"][i], p["s5_log_step"][i])
            ucol = 3 * hw // sw
            y2, hfin = s5_scan(proj, ucol, bre, bim, cre, cim, lam2, s5_h0[i], nseq, seq_len, sw, _tile(seq_len, 256))
            s5_fin.append(hfin)
            s5o = s5_glu(y2, proj, ucol, p["s5_d"][i], p["s5_glu_w"][i].astype(BF16), p["s5_glu_b"][i], tm)
            x = even_out(z, s5o, p["ev_w_out"][i].astype(BF16), x, g1, rows_per_mod, tm, _tile(d, 512))
        else:
            nh = p["od_gate_b"].shape[2]
            w = p["ml_norm_g"].shape[1]
            dh = w // nh
            w_in = p["od_w_in"][i]
            proj = normmod_matmul(x, p["norm_g"][l, 0], sc1, sh1, w_in[:, :4 * w].astype(BF16), rows_per_mod, tm,
                                  _tile(4 * w, 512))
            wg = w_in[:, 4 * w:].reshape(d, 4, nh)
            gb = p["od_gate_b"][i].astype(F32)
            wg2 = jnp.zeros((2, d, LANES), F32)
            bg2 = jnp.zeros((2, 1, LANES), F32)
            for dr in range(2):
                wg2 = wg2.at[dr, :, :nh].set(wg[:, dr]).at[dr, :, nh:2 * nh].set(wg[:, 2 + dr])
                bg2 = bg2.at[dr, 0, :nh].set(gb[dr]).at[dr, 0, nh:2 * nh].set(gb[2 + dr])
            gates = mlstm_gates(x, p["norm_g"][l, 0], sc1, sh1, wg2.astype(BF16), bg2, rows_per_mod, tm)
            qscale = jnp.concatenate([jnp.full((w,), dh ** -0.5, F32), jnp.ones((w,), F32)])
            qk = short_conv(proj, 2 * w, p["ml_conv_w"][i], p["ml_conv_b"][i], qscale, seq_len, act=True)
            h2, cf, nf, mf = mlstm_scan(qk, proj, 2, gates, ml_c0[i], ml_n0[i], ml_m0[i], nseq, seq_len, nh, dh)
            ml_fin.append((cf, nf, mf))
            x = odd_out(h2, proj, 3, p["ml_norm_g"][i], p["od_w_out"][i].astype(BF16), x, g1, rows_per_mod, nh, dh, tm,
                        _tile(d, 512))
        hn = normmod(x, p["norm_g"][l, 1], sc2, sh2, rows_per_mod, tm)
        tt = _tile(min(rows_per_mod, n), 512)
        th, s2, w1, w2 = peer_scores(hn, p["pk_w_q"][l].T.astype(BF16), p["pk_keys"][l].astype(F32), _tile(tt, 256))
        x = peer_dense(hn, p["pk_u"][l].astype(BF16), p["pk_v"][l].T.astype(BF16), th, s2, w1, w2, x, g2,
                       rows_per_mod, tt, 1024)
    y = final_norm(x, p["final_g"], tm)
    return y, s5_fin, ml_fin


def kernel(x_prompt, x_sample, state_s5_re, state_s5_im, state_mlstm_C, state_mlstm_n, state_mlstm_m, c, c_ctx, norm_g, ada_w, ada_b, final_g, ev_w_in, hy_conv_w, hy_conv_b, hy_w1, hy_b1, hy_w2, hy_b2, hy_w3, hy_freq, hy_decay, hy_bias, s5_a_re, s5_a_im, s5_b_re, s5_b_im, s5_c_re, s5_c_im, s5_log_step, s5_d, s5_glu_w, s5_glu_b, ev_w_out, od_w_in, od_gate_b, ml_conv_w, ml_conv_b, ml_norm_g, od_w_out, pk_w_q, pk_keys, pk_u, pk_v):
    p = dict(norm_g=norm_g, ada_w=ada_w, ada_b=ada_b, final_g=final_g, ev_w_in=ev_w_in,
             hy_conv_w=hy_conv_w, hy_conv_b=hy_conv_b, hy_w1=hy_w1, hy_b1=hy_b1, hy_w2=hy_w2, hy_b2=hy_b2,
             hy_w3=hy_w3, hy_freq=hy_freq, hy_decay=hy_decay, hy_bias=hy_bias, s5_a_re=s5_a_re,
             s5_a_im=s5_a_im, s5_b_re=s5_b_re, s5_b_im=s5_b_im, s5_c_re=s5_c_re, s5_c_im=s5_c_im,
             s5_log_step=s5_log_step, s5_d=s5_d, s5_glu_w=s5_glu_w, s5_glu_b=s5_glu_b, ev_w_out=ev_w_out,
             od_w_in=od_w_in, od_gate_b=od_gate_b, ml_conv_w=ml_conv_w, ml_conv_b=ml_conv_b,
             ml_norm_g=ml_norm_g, od_w_out=od_w_out, pk_w_q=pk_w_q, pk_keys=pk_keys, pk_u=pk_u, pk_v=pk_v)
    nb, seq, d = x_prompt.shape
    db, dseq, _ = x_sample.shape
    depth = norm_g.shape[0]
    n_even, n_odd = (depth + 1) // 2, depth // 2
    assert db + 1 <= 8

    cond8 = jnp.zeros((8, d), F32).at[0].set(c_ctx.astype(F32)).at[1:1 + db].set(c.astype(F32))
    mods_ctx, mods_lat = [], []
    for l in range(depth):
        mod = ada_mod(cond8, ada_w[l].astype(F32), ada_b[l].astype(F32))
        chunks = [mod[:, j * d:(j + 1) * d] for j in range(6)]
        mods_ctx.append([ch[0:1].reshape(1, 1, d) for ch in chunks])
        mods_lat.append([ch[1:1 + db].reshape(db, 1, d) for ch in chunks])

    def s5_state(re, im, bsz):
        return [jnp.stack([re[:, i].reshape(bsz, 2, -1), im[:, i].reshape(bsz, 2, -1)], axis=2).astype(F32)
                for i in range(n_even)]

    ngrp, npst = s5_a_re.shape[2], s5_a_re.shape[3]
    nh, dh = state_mlstm_C.shape[3], state_mlstm_C.shape[4]
    zeros_s5 = jnp.zeros((nb, n_even, 2, ngrp, npst), F32)
    y_prompt, s5_fin, ml_fin = _trunk(
        x_prompt.reshape(nb * seq, d), mods_ctx, s5_state(zeros_s5, zeros_s5, nb),
        [jnp.zeros((nb, 2, nh, dh, dh), F32)] * n_odd, [jnp.zeros((nb, 2, nh, dh), F32)] * n_odd,
        [jnp.zeros((nb, 2, nh, 1), F32)] * n_odd, p, nb, seq, nb * seq)
    x_lat = add_pos(x_sample.reshape(db * dseq, d), _pos_embed(dseq, d, GRID_W), dseq, _tile(dseq, 512))
    y_sample, _, _ = _trunk(
        x_lat, mods_lat, s5_state(state_s5_re, state_s5_im, db),
        [state_mlstm_C[:, i].astype(F32) for i in range(n_odd)], [state_mlstm_n[:, i].astype(F32) for i in range(n_odd)],
        [state_mlstm_m[:, i].astype(F32)[..., None] for i in range(n_odd)], p, db, dseq, dseq)

    new_s5_re = jnp.stack([h[:, :, 0].reshape(nb, 2, ngrp, npst) for h in s5_fin], axis=1)
    new_s5_im = jnp.stack([h[:, :, 1].reshape(nb, 2, ngrp, npst) for h in s5_fin], axis=1)
    new_c = jnp.stack([f[0] for f in ml_fin], axis=1)
    new_n = jnp.stack([f[1] for f in ml_fin], axis=1)
    new_m = jnp.stack([f[2][..., 0] for f in ml_fin], axis=1)
    return (y_prompt.reshape(nb, seq, d), y_sample.reshape(db, dseq, d), new_s5_re, new_s5_im, new_c, new_n, new_m)
```

```python
import functools
import math

import jax
import jax.numpy as jnp
from jax import lax
from jax.experimental import pallas as pl
from jax.experimental.pallas import tpu as pltpu

F32 = jnp.float32
BF16 = jnp.bfloat16
EPS = 1e-6
HIGHEST = lax.Precision.HIGHEST
V7X_VMEM_LIMIT_BYTES = 56 * 1024 * 1024
LANES = 128
SUBLANES = 8
ML_CHUNK = 128
PK_TOPK = 16
GRID_W = 64
NEG_INF = float("-inf")


def _cp(*sem):
    return pltpu.CompilerParams(dimension_semantics=sem, vmem_limit_bytes=V7X_VMEM_LIMIT_BYTES)


def _dot(a, b, **kw):
    return jnp.dot(a, b, preferred_element_type=F32, **kw)


def _dot_nt(a, b):
    return lax.dot_general(a, b, (((1,), (1,)), ((), ())), preferred_element_type=F32)


def _silu(x):
    return x * jax.nn.sigmoid(x)


def _ada_kernel(c_ref, w_ref, b_ref, o_ref):
    o_ref[...] = _dot(_silu(c_ref[...]), w_ref[...], precision=HIGHEST) + b_ref[...]


def ada_mod(cond8, w, b):
    d, no = w.shape
    tn = 1536 if no % 1536 == 0 else no
    return pl.pallas_call(
        _ada_kernel, grid=(no // tn,),
        in_specs=[pl.BlockSpec((8, d), lambda j: (0, 0)), pl.BlockSpec((d, tn), lambda j: (0, j)),
                  pl.BlockSpec((1, tn), lambda j: (0, j))],
        out_specs=pl.BlockSpec((8, tn), lambda j: (0, j)),
        out_shape=jax.ShapeDtypeStruct((8, no), F32), compiler_params=_cp("parallel"), name="ada_mod",
    )(cond8, w, b.reshape(1, no))


def _normmod(x, g, sc, sh):
    y = x * lax.rsqrt(jnp.mean(x * x, axis=-1, keepdims=True) + EPS)
    return (y * g) * (1.0 + sc) + sh


def _mod_spec(d, tm, rows_per_mod, nd=2):
    if nd == 2:
        return pl.BlockSpec((1, 1, d), lambda i, j: ((i * tm) // rows_per_mod, 0, 0))
    return pl.BlockSpec((1, 1, d), lambda i: ((i * tm) // rows_per_mod, 0, 0))


def _nm_matmul_kernel(x_ref, g_ref, sc_ref, sh_ref, w_ref, o_ref, h_ref):
    @pl.when(pl.program_id(1) == 0)
    def _():
        h_ref[...] = _normmod(x_ref[...], g_ref[...], sc_ref[0], sh_ref[0]).astype(BF16)
    o_ref[...] = _dot(h_ref[...], w_ref[...])


def normmod_matmul(x, g, sc, sh, w_bf16, rows_per_mod, tm, tn):
    n, d = x.shape
    no = w_bf16.shape[1]
    return pl.pallas_call(
        _nm_matmul_kernel, grid=(n // tm, no // tn),
        in_specs=[pl.BlockSpec((tm, d), lambda i, j: (i, 0)), pl.BlockSpec((1, d), lambda i, j: (0, 0)),
                  _mod_spec(d, tm, rows_per_mod), _mod_spec(d, tm, rows_per_mod),
                  pl.BlockSpec((d, tn), lambda i, j: (0, j))],
        out_specs=pl.BlockSpec((tm, tn), lambda i, j: (i, j)),
        out_shape=jax.ShapeDtypeStruct((n, no), F32),
        scratch_shapes=[pltpu.VMEM((tm, d), BF16)],
        compiler_params=_cp("parallel", "arbitrary"), name="normmod_matmul",
    )(x, g.reshape(1, d), sc, sh, w_bf16)


def _nm_kernel(x_ref, g_ref, sc_ref, sh_ref, o_ref):
    o_ref[...] = _normmod(x_ref[...], g_ref[...], sc_ref[0], sh_ref[0]).astype(o_ref.dtype)


def normmod(x, g, sc, sh, rows_per_mod, tm):
    n, d = x.shape
    return pl.pallas_call(
        _nm_kernel, grid=(n // tm,),
        in_specs=[pl.BlockSpec((tm, d), lambda i: (i, 0)), pl.BlockSpec((1, d), lambda i: (0, 0)),
                  _mod_spec(d, tm, rows_per_mod, 1), _mod_spec(d, tm, rows_per_mod, 1)],
        out_specs=pl.BlockSpec((tm, d), lambda i: (i, 0)),
        out_shape=jax.ShapeDtypeStruct((n, d), BF16), compiler_params=_cp("parallel"), name="normmod",
    )(x, g.reshape(1, d), sc, sh)


def _final_norm_kernel(x_ref, g_ref, o_ref):
    x = x_ref[...]
    o_ref[...] = (x * lax.rsqrt(jnp.mean(x * x, axis=-1, keepdims=True) + EPS)) * g_ref[...]


def final_norm(x, g, tm):
    n, d = x.shape
    return pl.pallas_call(
        _final_norm_kernel, grid=(n // tm,),
        in_specs=[pl.BlockSpec((tm, d), lambda i: (i, 0)), pl.BlockSpec((1, d), lambda i: (0, 0))],
        out_specs=pl.BlockSpec((tm, d), lambda i: (i, 0)),
        out_shape=jax.ShapeDtypeStruct((n, d), F32), compiler_params=_cp("parallel"), name="final_norm",
    )(x, g.reshape(1, d))


def _add_rows_kernel(x_ref, p_ref, o_ref):
    o_ref[...] = x_ref[...] + p_ref[...]


def add_pos(x, pe, seq_len, tm):
    n, d = x.shape
    nb = seq_len // tm
    return pl.pallas_call(
        _add_rows_kernel, grid=(n // tm,),
        in_specs=[pl.BlockSpec((tm, d), lambda i: (i, 0)), pl.BlockSpec((tm, d), lambda i: (i % nb, 0))],
        out_specs=pl.BlockSpec((tm, d), lambda i: (i, 0)),
        out_shape=jax.ShapeDtypeStruct((n, d), F32), compiler_params=_cp("parallel"), name="add_pos",
    )(x, pe)


def _sconv_kernel(x_ref, w_ref, b_ref, s_ref, o_ref, *, act):
    x = x_ref[...]
    n_tok = x.shape[0]
    row = lax.broadcasted_iota(jnp.int32, x.shape, 0)
    prev = jnp.where(row == 0, 0.0, pltpu.roll(x, 1, 0))
    nxt = jnp.where(row == n_tok - 1, 0.0, pltpu.roll(x, n_tok - 1, 0))
    y = prev * w_ref[0:1, :] + x * w_ref[1:2, :] + nxt * w_ref[2:3, :] + b_ref[...]
    if act:
        y = _silu(y) * s_ref[...]
    o_ref[...] = y


def short_conv(a, ncols, w, b, scale, seq_len, act, cb=256):
    n = a.shape[0]
    return pl.pallas_call(
        functools.partial(_sconv_kernel, act=act), grid=(n // seq_len, ncols // cb),
        in_specs=[pl.BlockSpec((seq_len, cb), lambda s, j: (s, j)), pl.BlockSpec((3, cb), lambda s, j: (0, j)),
                  pl.BlockSpec((1, cb), lambda s, j: (0, j)), pl.BlockSpec((1, cb), lambda s, j: (0, j))],
        out_specs=pl.BlockSpec((seq_len, cb), lambda s, j: (s, j)),
        out_shape=jax.ShapeDtypeStruct((n, ncols), F32), compiler_params=_cp("parallel", "parallel"),
        name="short_conv",
    )(a, w, b.reshape(1, ncols), scale.reshape(1, ncols))


def dft_tables(n_tok):
    k = jnp.arange(n_tok, dtype=jnp.int32)
    blk = 1 << ((n_tok.bit_length() - 1) // 2)
    def thin(n):
        ang = ((k[:, None] * n[None, :]) % (2 * n_tok)).astype(F32) * (math.pi / n_tok)
        return jnp.cos(ang), jnp.sin(ang)
    (c_hi, s_hi), (c_lo, s_lo) = thin(jnp.arange(0, n_tok, blk, dtype=jnp.int32)), thin(jnp.arange(blk, dtype=jnp.int32))
    cos_t = (c_hi[:, :, None] * c_lo[:, None, :] - s_hi[:, :, None] * s_lo[:, None, :]).reshape(n_tok, n_tok)
    msin = -(s_hi[:, :, None] * c_lo[:, None, :] + c_hi[:, :, None] * s_lo[:, None, :]).reshape(n_tok, n_tok)
    alt = jnp.where(k % 2 == 0, 1.0, -1.0).astype(F32)
    a_t = msin.at[0, :].set(alt)
    a_tt = msin.at[:, 0].set(alt)
    return cos_t.astype(BF16), a_t.astype(BF16), a_tt.astype(BF16)


def _hyfilt_kernel(band_ref, w1_ref, b1_ref, w2_ref, b2_ref, w3_ref, fr_ref, dec_ref, h_ref, ss_ref, *,
                   n_tok, tl, hw, nbands):
    i = pl.program_id(0)
    pos = i * tl + lax.broadcasted_iota(jnp.int32, (tl, 1), 0)
    t = pos.astype(F32) / n_tok
    lane = lax.broadcasted_iota(jnp.int32, (tl, LANES), 1)
    ang = 2.0 * math.pi * t * band_ref[...]
    z = jnp.where(lane == 0, t, jnp.where(lane <= nbands, jnp.cos(ang),
                                          jnp.where(lane <= 2 * nbands, jnp.sin(ang), 0.0)))
    fr = fr_ref[...]
    h = jnp.sin(fr * (_dot(z, w1_ref[...], precision=HIGHEST) + b1_ref[...]))
    h = jnp.sin(fr * (_dot(h, w2_ref[...], precision=HIGHEST) + b2_ref[...]))
    h = _dot(h, w3_ref[...], precision=HIGHEST) * jnp.exp(-t * jnp.abs(dec_ref[...]))
    col = lax.broadcasted_iota(jnp.int32, h.shape, 1)
    is_bwd = (col // hw) % 2 == 1
    h = jnp.where(jnp.logical_and(is_bwd, pos == 0), 0.0, h)
    h_ref[...] = h.astype(BF16)

    @pl.when(i == 0)
    def _():
        ss_ref[...] = jnp.zeros_like(ss_ref)
    ss_ref[...] += jnp.sum(h * h, axis=0, keepdims=True)


def hyena_filter_taps(n_tok, w1, b1, w2, b2, w3, freq, decay, hw):
    emb, ffn = w1.shape
    nbands = (emb - 1) // 2
    tl = min(n_tok, 512)
    bands = jnp.linspace(1e-4, nbands - 1, nbands, dtype=F32)
    band_row = jnp.zeros((1, LANES), F32).at[0, 1:1 + nbands].set(bands).at[0, 1 + nbands:1 + 2 * nbands].set(bands)
    w1p = jnp.zeros((LANES, ffn), F32).at[:emb].set(w1)
    nc = w3.shape[1]
    full = lambda shp: pl.BlockSpec(shp, lambda i: (0, 0))
    return pl.pallas_call(
        functools.partial(_hyfilt_kernel, n_tok=n_tok, tl=tl, hw=hw, nbands=nbands), grid=(n_tok // tl,),
        in_specs=[full((1, LANES)), full((LANES, ffn)), full((1, ffn)), full((ffn, ffn)), full((1, ffn)),
                  full((ffn, nc)), full((1, ffn)), full((1, nc))],
        out_specs=[pl.BlockSpec((tl, nc), lambda i: (i, 0)), full((1, nc))],
        out_shape=[jax.ShapeDtypeStruct((n_tok, nc), BF16), jax.ShapeDtypeStruct((1, nc), F32)],
        compiler_params=_cp("arbitrary"), name="hyena_filter_taps",
    )(band_row, w1p, b1.reshape(1, ffn), w2, b2.reshape(1, ffn), w3, freq.reshape(1, ffn), decay.reshape(1, nc))


def _filt_dft_kernel(c_ref, a_ref, h_ref, ss_ref, kr_ref, ki_ref, *, tf, hw):
    i = pl.program_id(1)
    hf = h_ref[:, :hw]
    hb = h_ref[:, hw:]
    cc = c_ref[...]
    aa = a_ref[...]
    zrf, zif, zrb, zib = _dot(cc, hf), _dot(aa, hf), _dot(cc, hb), _dot(aa, hb)
    scale = lax.rsqrt(ss_ref[:, :hw] + ss_ref[:, hw:] + EPS)
    first = (i * tf + lax.broadcasted_iota(jnp.int32, (tf, 1), 0)) == 0
    scale = scale * jnp.where(first, 0.5, 1.0)
    kr_ref[0] = (zrf + zrb) * scale
    ki_ref[0] = jnp.where(first, zif + zib, zif - zib) * scale


def hyena_filter_spectrum(cos_t, a_t, taps, sumsq, hw, tf):
    n_tok = cos_t.shape[0]
    norder = taps.shape[1] // (2 * hw)
    out = jax.ShapeDtypeStruct((norder, n_tok, hw), F32)
    return pl.pallas_call(
        functools.partial(_filt_dft_kernel, tf=tf, hw=hw), grid=(norder, n_tok // tf),
        in_specs=[pl.BlockSpec((tf, n_tok), lambda o, i: (i, 0)), pl.BlockSpec((tf, n_tok), lambda o, i: (i, 0)),
                  pl.BlockSpec((n_tok, 2 * hw), lambda o, i: (0, o)), pl.BlockSpec((1, 2 * hw), lambda o, i: (0, o))],
        out_specs=[pl.BlockSpec((1, tf, hw), lambda o, i: (o, i, 0))] * 2,
        out_shape=[out, out], compiler_params=_cp("parallel", "parallel"), name="hyena_filter_spectrum",
    )(cos_t, a_t, taps, sumsq)


def _hy_fwd_kernel(c_ref, a_ref, z_ref, kr_ref, ki_ref, yr_ref, yi_ref, *, tf):
    i = pl.program_id(0)
    zb = z_ref[...].astype(BF16)
    zr = _dot(c_ref[...], zb)
    zi = _dot(a_ref[...], zb)
    kr = kr_ref[0]
    ki = ki_ref[0]
    first = (i * tf + lax.broadcasted_iota(jnp.int32, (tf, 1), 0)) == 0
    yr_ref[...] = jnp.where(first, zr * kr, zr * kr - zi * ki).astype(BF16)
    yi_ref[...] = jnp.where(first, zi * ki, zr * ki + zi * kr).astype(BF16)


def hyena_fwd(cos_t, a_t, z, zcol, kr, ki, order, nseq, hw, tf):
    n_tok = cos_t.shape[0]
    nf = n_tok // tf
    out = jax.ShapeDtypeStruct((nseq * n_tok, hw), BF16)
    return pl.pallas_call(
        functools.partial(_hy_fwd_kernel, tf=tf), grid=(nf, nseq),
        in_specs=[pl.BlockSpec((tf, n_tok), lambda i, b: (i, 0)), pl.BlockSpec((tf, n_tok), lambda i, b: (i, 0)),
                  pl.BlockSpec((n_tok, hw), lambda i, b: (b, zcol)),
                  pl.BlockSpec((1, tf, hw), lambda i, b: (order, i, 0)),
                  pl.BlockSpec((1, tf, hw), lambda i, b: (order, i, 0))],
        out_specs=[pl.BlockSpec((tf, hw), lambda i, b: (b * nf + i, 0))] * 2,
        out_shape=[out, out], compiler_params=_cp("parallel", "parallel"), name="hyena_fwd",
    )(cos_t, a_t, z, kr, ki)


def _hy_inv_kernel(c_ref, at_ref, yr_ref, yi_ref, zp_ref, gate_ref, bias_ref, o_ref, *, inv_len):
    conv = (_dot(c_ref[...], yr_ref[...]) + _dot(at_ref[...], yi_ref[...])) * inv_len
    o_ref[...] = gate_ref[...] * (conv + bias_ref[...] * zp_ref[...])


def hyena_inv(cos_t, a_tt, yr, yi, zprev, zcol, gates, gcol, bias_row, nseq, hw, tf):
    n_tok = cos_t.shape[0]
    nf = n_tok // tf
    return pl.pallas_call(
        functools.partial(_hy_inv_kernel, inv_len=1.0 / n_tok), grid=(nf, nseq),
        in_specs=[pl.BlockSpec((tf, n_tok), lambda i, b: (i, 0)), pl.BlockSpec((tf, n_tok), lambda i, b: (i, 0)),
                  pl.BlockSpec((n_tok, hw), lambda i, b: (b, 0)), pl.BlockSpec((n_tok, hw), lambda i, b: (b, 0)),
                  pl.BlockSpec((tf, hw), lambda i, b: (b * nf + i, zcol)),
                  pl.BlockSpec((tf, hw), lambda i, b: (b * nf + i, gcol)),
                  pl.BlockSpec((1, hw), lambda i, b: (0, 0))],
        out_specs=pl.BlockSpec((tf, hw), lambda i, b: (b * nf + i, 0)),
        out_shape=jax.ShapeDtypeStruct((nseq * n_tok, hw), F32),
        compiler_params=_cp("parallel", "parallel"), name="hyena_inv",
    )(cos_t, a_tt, yr, yi, zprev, gates, bias_row)


def _s5_kernel(u_ref, bre_ref, bim_ref, cre_ref, cim_ref, lam_ref, h0_ref, y_ref, hfin_ref, hre_s, him_s, st_s, *,
               tc, nc, ns):
    d = pl.program_id(0)
    c = pl.program_id(2)

    @pl.when(c == 0)
    def _():
        st_s[...] = h0_ref[0, 0]

    ub = u_ref[...].astype(BF16)
    hre_s[...] = _dot(ub, bre_ref[0])
    him_s[...] = _dot(ub, bim_ref[0])
    lr = lam_ref[0, 0:1, :]
    li = lam_ref[0, 1:2, :]

    def body(t, carry):
        hr, hi = carry
        r = jnp.where(d == 0, t, tc - 1 - t)
        nr = lr * hr - li * hi + hre_s[pl.ds(r, 1), :]
        ni = lr * hi + li * hr + him_s[pl.ds(r, 1), :]
        hre_s[pl.ds(r, 1), :] = nr
        him_s[pl.ds(r, 1), :] = ni
        return nr, ni

    hr, hi = lax.fori_loop(0, tc, body, (st_s[0:1, :], st_s[1:2, :]), unroll=8)
    st_s[0:1, :] = hr
    st_s[1:2, :] = hi
    y_ref[0] = _dot(hre_s[...].astype(BF16), cre_ref[0]) + _dot(him_s[...].astype(BF16), cim_ref[0])

    @pl.when(c == nc - 1)
    def _():
        hfin_ref[0, 0] = st_s[...]


def s5_scan(proj, ucol, bre, bim, cre, cim, lam, h0, nseq, seq_len, sw, tc):
    ns = bre.shape[2]
    nc = seq_len // tc

    def chunk(d, c):
        return c + d * (nc - 1 - 2 * c)

    return pl.pallas_call(
        functools.partial(_s5_kernel, tc=tc, nc=nc, ns=ns), grid=(2, nseq, nc),
        in_specs=[pl.BlockSpec((tc, sw), lambda d, b, c: (b * nc + chunk(d, c), ucol)),
                  pl.BlockSpec((1, sw, ns), lambda d, b, c: (d, 0, 0)),
                  pl.BlockSpec((1, sw, ns), lambda d, b, c: (d, 0, 0)),
                  pl.BlockSpec((1, ns, sw), lambda d, b, c: (d, 0, 0)),
                  pl.BlockSpec((1, ns, sw), lambda d, b, c: (d, 0, 0)),
                  pl.BlockSpec((1, 2, ns), lambda d, b, c: (d, 0, 0)),
                  pl.BlockSpec((1, 1, 2, ns), lambda d, b, c: (b, d, 0, 0))],
        out_specs=[pl.BlockSpec((1, tc, sw), lambda d, b, c: (d, b * nc + chunk(d, c), 0)),
                   pl.BlockSpec((1, 1, 2, ns), lambda d, b, c: (b, d, 0, 0))],
        out_shape=[jax.ShapeDtypeStruct((2, nseq * seq_len, sw), F32), jax.ShapeDtypeStruct((nseq, 2, 2, ns), F32)],
        scratch_shapes=[pltpu.VMEM((tc, ns), F32), pltpu.VMEM((tc, ns), F32), pltpu.VMEM((2, ns), F32)],
        compiler_params=_cp("parallel", "parallel", "arbitrary"), name="s5_scan",
    )(proj, bre, bim, cre, cim, lam, h0)


def _s5_glu_kernel(yf_ref, yb_ref, u_ref, d_ref, w_ref, b_ref, o_ref):
    y = jax.nn.gelu(yf_ref[0] + yb_ref[0] + d_ref[...] * u_ref[...])
    o_ref[...] = y * jax.nn.sigmoid(_dot(y.astype(BF16), w_ref[...]) + b_ref[...])


def s5_glu(y2, proj, ucol, d_skip, glu_w_bf16, glu_b, tm):
    _, n, sw = y2.shape
    return pl.pallas_call(
        _s5_glu_kernel, grid=(n // tm,),
        in_specs=[pl.BlockSpec((1, tm, sw), lambda i: (0, i, 0)), pl.BlockSpec((1, tm, sw), lambda i: (1, i, 0)),
                  pl.BlockSpec((tm, sw), lambda i: (i, ucol)), pl.BlockSpec((1, sw), lambda i: (0, 0)),
                  pl.BlockSpec((sw, sw), lambda i: (0, 0)), pl.BlockSpec((1, sw), lambda i: (0, 0))],
        out_specs=pl.BlockSpec((tm, sw), lambda i: (i, 0)),
        out_shape=jax.ShapeDtypeStruct((n, sw), F32), compiler_params=_cp("parallel"), name="s5_glu",
    )(y2, y2, proj, d_skip.reshape(1, sw), glu_w_bf16, glu_b.reshape(1, sw))


def _even_out_kernel(a_ref, b_ref, wa_ref, wb_ref, x_ref, gate_ref, o_ref):
    y = _dot(a_ref[...].astype(BF16), wa_ref[...]) + _dot(b_ref[...].astype(BF16), wb_ref[...])
    o_ref[...] = x_ref[...] + gate_ref[0] * y


def even_out(hy, s5o, w_bf16, x, gate, rows_per_mod, tm, tn):
    n, d = x.shape
    hw = hy.shape[1]
    sw = s5o.shape[1]
    return pl.pallas_call(
        _even_out_kernel, grid=(n // tm, d // tn),
        in_specs=[pl.BlockSpec((tm, hw), lambda i, j: (i, 0)), pl.BlockSpec((tm, sw), lambda i, j: (i, 0)),
                  pl.BlockSpec((hw, tn), lambda i, j: (0, j)), pl.BlockSpec((sw, tn), lambda i, j: (hw // sw, j)),
                  pl.BlockSpec((tm, tn), lambda i, j: (i, j)),
                  pl.BlockSpec((1, 1, tn), lambda i, j: ((i * tm) // rows_per_mod, 0, j))],
        out_specs=pl.BlockSpec((tm, tn), lambda i, j: (i, j)),
        out_shape=jax.ShapeDtypeStruct((n, d), F32), compiler_params=_cp("parallel", "parallel"), name="even_out",
    )(hy, s5o, w_bf16, w_bf16, x, gate)


def _log_sigmoid(x):
    return jnp.minimum(x, 0.0) - jnp.log1p(jnp.exp(-jnp.abs(x)))


def _gates_kernel(x_ref, g_ref, sc_ref, sh_ref, w_ref, b_ref, o_ref):
    h = _normmod(x_ref[...], g_ref[...], sc_ref[0], sh_ref[0]).astype(BF16)
    o_ref[0] = _dot(h, w_ref[0]) + b_ref[0]


def mlstm_gates(x, g, sc, sh, wg_bf16, bg, rows_per_mod, tm):
    n, d = x.shape
    return pl.pallas_call(
        _gates_kernel, grid=(n // tm, 2),
        in_specs=[pl.BlockSpec((tm, d), lambda i, j: (i, 0)), pl.BlockSpec((1, d), lambda i, j: (0, 0)),
                  _mod_spec(d, tm, rows_per_mod), _mod_spec(d, tm, rows_per_mod),
                  pl.BlockSpec((1, d, LANES), lambda i, j: (j, 0, 0)), pl.BlockSpec((1, 1, LANES), lambda i, j: (j, 0, 0))],
        out_specs=pl.BlockSpec((1, tm, LANES), lambda i, j: (j, i, 0)),
        out_shape=jax.ShapeDtypeStruct((2, n, LANES), F32), compiler_params=_cp("parallel", "parallel"),
        name="mlstm_gates",
    )(x, g.reshape(1, d), sc, sh, wg_bf16, bg)


def _mlstm_kernel(q_ref, k_ref, v_ref, g_ref, c0_ref, n0_ref, m0_ref, h_ref, cf_ref, nf_ref, mf_ref,
                  c_s, n_s, m_s, *, nh, dh, tc, nc):
    d = pl.program_id(0)
    c = pl.program_id(2)

    @pl.when(c == 0)
    def _():
        c_s[...] = c0_ref[0, 0]
        n_s[...] = n0_ref[0, 0]
        m_s[...] = m0_ref[0, 0]

    gates = g_ref[0]
    lane = lax.broadcasted_iota(jnp.int32, gates.shape, 1)
    logf = jnp.where(jnp.logical_and(lane >= nh, lane < 2 * nh), _log_sigmoid(gates), 0.0)
    r_i = lax.broadcasted_iota(jnp.int32, (tc, tc), 0)
    s_i = lax.broadcasted_iota(jnp.int32, (tc, tc), 1)
    causal = (r_i - s_i) * (1 - 2 * d) >= 0
    bcum = _dot(causal.astype(F32), logf, precision=HIGHEST)
    btot = jnp.sum(logf, axis=0, keepdims=True)
    gates_t = gates.T
    bcum_t = bcum.T
    for h in range(nh):
        hs = slice(h * dh, (h + 1) * dh)
        q = q_ref[:, hs]
        k = k_ref[:, hs]
        v = v_ref[:, hs]
        b_col = bcum[:, nh + h:nh + h + 1]
        b_row = bcum_t[nh + h:nh + h + 1, :]
        i_col = gates[:, h:h + 1]
        i_row = gates_t[h:h + 1, :]
        m = m_s[h:h + 1, :]
        a = b_col + m
        dmat = jnp.where(causal, b_col - b_row + i_row, NEG_INF)
        mq = jnp.maximum(a, jnp.max(dmat, axis=-1, keepdims=True))
        w_intra = jnp.exp(dmat - mq)
        w_inter = jnp.exp(a - mq)
        s = _dot_nt(q, k) * w_intra
        cm = c_s[h]
        nv = n_s[h:h + 1, :]
        num = _dot(s, v) + w_inter * _dot(q, cm)
        den = jnp.sum(s, axis=-1, keepdims=True) + w_inter * jnp.sum(q * nv, axis=-1, keepdims=True)
        h_ref[0, :, hs] = num / jnp.maximum(jnp.abs(den), jnp.exp(-mq))
        b_last = btot[:, nh + h:nh + h + 1]
        g = b_last - b_col + i_col
        m_new = jnp.maximum(b_last + m, jnp.max(g, axis=0, keepdims=True))
        kw = k * jnp.exp(g - m_new)
        keep = jnp.exp(b_last + m - m_new)
        c_s[h] = keep * cm + _dot(kw.T, v)
        n_s[h:h + 1, :] = keep * nv + jnp.sum(kw, axis=0, keepdims=True)
        m_s[h:h + 1, :] = m_new

    @pl.when(c == nc - 1)
    def _():
        cf_ref[0, 0] = c_s[...]
        nf_ref[0, 0] = n_s[...]
        mf_ref[0, 0] = m_s[...]


def mlstm_scan(qk, proj, vcol, gates, c0, n0, m0, nseq, seq_len, nh, dh):
    tc = ML_CHUNK
    nc = seq_len // tc
    w = nh * dh

    def chunk(d, c):
        return c + d * (nc - 1 - 2 * c)

    rowblk = lambda d, b, c: b * nc + chunk(d, c)
    st = lambda shp: pl.BlockSpec((1, 1) + shp, lambda d, b, c: (b, d) + (0,) * len(shp))
    return pl.pallas_call(
        functools.partial(_mlstm_kernel, nh=nh, dh=dh, tc=tc, nc=nc), grid=(2, nseq, nc),
        in_specs=[pl.BlockSpec((tc, w), lambda d, b, c: (rowblk(d, b, c), 0)),
                  pl.BlockSpec((tc, w), lambda d, b, c: (rowblk(d, b, c), 1)),
                  pl.BlockSpec((tc, w), lambda d, b, c: (rowblk(d, b, c), vcol)),
                  pl.BlockSpec((1, tc, LANES), lambda d, b, c: (d, rowblk(d, b, c), 0)),
                  st((nh, dh, dh)), st((nh, dh)), st((nh, 1))],
        out_specs=[pl.BlockSpec((1, tc, w), lambda d, b, c: (d, rowblk(d, b, c), 0)),
                   st((nh, dh, dh)), st((nh, dh)), st((nh, 1))],
        out_shape=[jax.ShapeDtypeStruct((2, nseq * seq_len, w), F32),
                   jax.ShapeDtypeStruct((nseq, 2, nh, dh, dh), F32), jax.ShapeDtypeStruct((nseq, 2, nh, dh), F32),
                   jax.ShapeDtypeStruct((nseq, 2, nh, 1), F32)],
        scratch_shapes=[pltpu.VMEM((nh, dh, dh), F32), pltpu.VMEM((nh, dh), F32), pltpu.VMEM((nh, 1), F32)],
        compiler_params=_cp("parallel", "parallel", "arbitrary"), name="mlstm_scan",
    )(qk, qk, proj, gates, c0, n0, m0)


def _odd_out_kernel(hf_ref, hb_ref, o_ref, ng_ref, w_ref, x_ref, gate_ref, out_ref, a_s, *, nh, dh):
    @pl.when(pl.program_id(1) == 0)
    def _():
        for h in range(nh):
            hs = slice(h * dh, (h + 1) * dh)
            blk = hf_ref[0, :, hs] + hb_ref[0, :, hs]
            blk = blk * lax.rsqrt(jnp.mean(blk * blk, axis=-1, keepdims=True) + EPS)
            a_s[:, hs] = ((blk * ng_ref[:, hs]) * _silu(o_ref[:, hs])).astype(BF16)
    out_ref[...] = x_ref[...] + gate_ref[0] * _dot(a_s[...], w_ref[...])


def odd_out(h2, proj, ocol, norm_g, w_bf16, x, gate, rows_per_mod, nh, dh, tm, tn):
    n, d = x.shape
    w = nh * dh
    return pl.pallas_call(
        functools.partial(_odd_out_kernel, nh=nh, dh=dh), grid=(n // tm, d // tn),
        in_specs=[pl.BlockSpec((1, tm, w), lambda i, j: (0, i, 0)), pl.BlockSpec((1, tm, w), lambda i, j: (1, i, 0)),
                  pl.BlockSpec((tm, w), lambda i, j: (i, ocol)), pl.BlockSpec((1, w), lambda i, j: (0, 0)),
                  pl.BlockSpec((w, tn), lambda i, j: (0, j)), pl.BlockSpec((tm, tn), lambda i, j: (i, j)),
                  pl.BlockSpec((1, 1, tn), lambda i, j: ((i * tm) // rows_per_mod, 0, j))],
        out_specs=pl.BlockSpec((tm, tn), lambda i, j: (i, j)),
        out_shape=jax.ShapeDtypeStruct((n, d), F32), scratch_shapes=[pltpu.VMEM((tm, w), BF16)],
        compiler_params=_cp("parallel", "arbitrary"), name="odd_out",
    )(h2, h2, proj, norm_g.reshape(1, w), w_bf16, x, gate)


def _top_values(curs, k, outs):
    curs = list(curs)
    for j in range(k):
        for a, out_s in enumerate(outs):
            m = jnp.max(curs[a], axis=0, keepdims=True)
            out_s[j:j + 1, :] = m
            curs[a] = jnp.where(curs[a] == m, NEG_INF, curs[a])


def _pair_candidates(k):
    return [(a, k // (a + 1)) for a in range(k)]


def _peer_score_kernel(h_ref, wq_ref, keys_ref, th_ref, s2_ref, w1_ref, w2_ref, q_s, v1_s, v2_s, cand_s, best_s, *,
                       nh, half, topk):
    q_s[...] = _dot_nt(wq_ref[...], h_ref[...])
    kk = topk + 1
    cand_s[...] = jnp.full(cand_s.shape, NEG_INF, F32)

    def head(hd, carry):
        base = pl.multiple_of(hd * 2 * half, 2 * half)
        s1 = _dot(keys_ref[hd, 0], q_s[pl.ds(base, half), :])
        s2 = _dot(keys_ref[hd, 1], q_s[pl.ds(base + half, half), :])
        _top_values((s1, s2), kk, (v1_s, v2_s))
        off = 0
        for a, cnt in _pair_candidates(kk):
            cand_s[off:off + cnt, :] = v1_s[a:a + 1, :] + v2_s[0:cnt, :]
            off += cnt
        _top_values((cand_s[...],), kk, (best_s,))
        best = best_s[0:topk, :]
        z = jnp.sum(jnp.exp(best - best[0:1, :]), axis=0, keepdims=True)
        tmid = 0.5 * (best_s[topk - 1:topk, :] + best_s[topk:topk + 1, :])
        th = tmid - s1
        w1 = jnp.exp(s1 - v1_s[0:1, :]) / z
        w2 = jnp.exp(s2 - v2_s[0:1, :])
        for lt in range(s1.shape[1] // LANES):
            sl = slice(lt * LANES, (lt + 1) * LANES)
            th_ref[hd, lt] = th[:, sl]
            s2_ref[hd, lt] = s2[:, sl]
            w1_ref[hd, lt] = w1[:, sl]
            w2_ref[hd, lt] = w2[:, sl]
        return carry

    lax.fori_loop(0, nh, head, 0)


def peer_scores(h_bf16, wq_t_bf16, keys, tt):
    n, d = h_bf16.shape
    nh, _, nk, half = keys.shape
    kk = PK_TOPK + 1
    ncand = -(-sum(c for _, c in _pair_candidates(kk)) // 8) * 8
    big = jax.ShapeDtypeStruct((nh, n // LANES, nk, LANES), F32)
    bspec = pl.BlockSpec((nh, tt // LANES, nk, LANES), lambda i: (0, i, 0, 0))
    return pl.pallas_call(
        functools.partial(_peer_score_kernel, nh=nh, half=half, topk=PK_TOPK), grid=(n // tt,),
        in_specs=[pl.BlockSpec((tt, d), lambda i: (i, 0)), pl.BlockSpec((nh * 2 * half, d), lambda i: (0, 0)),
                  pl.BlockSpec((nh, 2, nk, half), lambda i: (0, 0, 0, 0))],
        out_specs=[bspec, bspec, bspec, bspec],
        out_shape=[big, big, big, big],
        scratch_shapes=[pltpu.VMEM((nh * 2 * half, tt), F32), pltpu.VMEM((24, tt), F32), pltpu.VMEM((24, tt), F32),
                        pltpu.VMEM((ncand, tt), F32), pltpu.VMEM((24, tt), F32)],
        compiler_params=_cp("parallel"), name="peer_scores",
    )(h_bf16, wq_t_bf16, keys)


PEER_KEY_ROWS = 16
PEER_MXU_ROWS = 256
PEER_TILES_WITH_SCORES = 12
GELU_C1 = math.sqrt(2.0 / math.pi)
GELU_C2 = 0.044715 * GELU_C1


def _gelu_tanh(x):
    half_x = 0.5 * x
    return half_x + half_x * jnp.tanh(x * (GELU_C1 + GELU_C2 * (x * x)))


def _peer_dense_kernel(h_ref, u_ref, vt_ref, th_ref, s2_ref, w1_ref, w2_ref, x_ref, gate_ref, o_ref,
                       acc_s, st_s, g_s, wt_s, *, nh, nk, ec, tt, nchunk):
    e = pl.program_id(1)
    nlt = tt // LANES
    n_i1 = ec // nk
    nkt = nk // PEER_KEY_ROWS
    d_model = acc_s.shape[0]
    cur = e % 2

    @pl.when(e == 0)
    def _():
        acc_s[...] = jnp.zeros_like(acc_s)
        wt_s[1] = jnp.zeros(wt_s.shape[1:], BF16)

    def gate_tile(idx):
        lt, ks = idx // nkt, slice((idx % nkt) * PEER_KEY_ROWS, (idx % nkt + 1) * PEER_KEY_ROWS)
        subs = [slice(r, r + SUBLANES) for r in range(ks.start, ks.stop, SUBLANES)]
        g = [[jnp.zeros((SUBLANES, LANES), F32) for _ in subs] for _ in range(n_i1)]
        for hd in range(nh):
            s2t = [s2_ref[hd, lt, sub, :] for sub in subs]
            w2t = [w2_ref[hd, lt, sub, :] for sub in subs]
            for li in range(n_i1):
                thb = jnp.broadcast_to(th_ref[hd, lt, li:li + 1, :], (SUBLANES, LANES))
                w1b = jnp.broadcast_to(w1_ref[hd, lt, li:li + 1, :], (SUBLANES, LANES))
                for j in range(len(subs)):
                    g[li][j] = g[li][j] + jnp.where(s2t[j] >= thb, w2t[j] * w1b, 0.0)
        for li in range(n_i1):
            for j, sub in enumerate(subs):
                g_s[lt, li * nk + sub.start:li * nk + sub.stop, :] = g[li][j]

    def spread(tiles, nslices):
        return [tiles[j::nslices] for j in range(nslices)]

    ntile = nlt * nkt
    slices1 = ec // PEER_MXU_ROWS
    slices2 = d_model // PEER_MXU_ROWS

    @pl.when(e < nchunk)
    def _():
        hb = h_ref[...]
        for j, tiles in enumerate(spread(list(range(PEER_TILES_WITH_SCORES)), slices1)):
            rs = slice(j * PEER_MXU_ROWS, (j + 1) * PEER_MXU_ROWS)
            st = _gelu_tanh(_dot_nt(u_ref[rs, :], hb))
            for lt in range(nlt):
                st_s[lt, rs, :] = st[:, lt * LANES:(lt + 1) * LANES]
            for idx in tiles:
                gate_tile(idx)

    wt_prev = jnp.concatenate([wt_s[1 - cur, lt] for lt in range(nlt)], axis=1)
    for j, tiles in enumerate(spread(list(range(PEER_TILES_WITH_SCORES, ntile)), slices2)):
        rs = slice(j * PEER_MXU_ROWS, (j + 1) * PEER_MXU_ROWS)
        acc_s[rs, :] += _dot(vt_ref[0, rs, :], wt_prev)
        for idx in tiles:
            gate_tile(idx)
    def finish(lt, carry):
        wt_s[cur, lt] = (st_s[lt] * g_s[lt]).astype(BF16)
        return carry

    lax.fori_loop(0, nlt, finish, 0)

    @pl.when(e == nchunk)
    def _():
        o_ref[...] = x_ref[...] + gate_ref[0] * acc_s[...].T


def peer_dense(h_bf16, u_bf16, vt_bf16, th, s2, w1, w2, x, gate, rows_per_mod, tt, ec):
    n, d = x.shape
    nh, _, nk, _ = s2.shape
    nchunk = u_bf16.shape[0] // ec
    this = lambda e: jnp.minimum(e, nchunk - 1)
    prev = lambda e: jnp.maximum(e - 1, 0)
    bspec = pl.BlockSpec((nh, tt // LANES, nk, LANES), lambda i, e: (0, i, 0, 0))
    rspec = pl.BlockSpec((nh, tt // LANES, ec // nk, LANES), lambda i, e: (0, i, this(e), 0))
    tile_buf = (tt // LANES, ec, LANES)
    return pl.pallas_call(
        functools.partial(_peer_dense_kernel, nh=nh, nk=nk, ec=ec, tt=tt, nchunk=nchunk), grid=(n // tt, nchunk + 1),
        in_specs=[pl.BlockSpec((tt, d), lambda i, e: (i, 0)), pl.BlockSpec((ec, d), lambda i, e: (this(e), 0)),
                  pl.BlockSpec((1, d, ec), lambda i, e: (prev(e), 0, 0)), rspec, bspec, rspec, bspec,
                  pl.BlockSpec((tt, d), lambda i, e: (i, 0)),
                  pl.BlockSpec((1, 1, d), lambda i, e: ((i * tt) // rows_per_mod, 0, 0))],
        out_specs=pl.BlockSpec((tt, d), lambda i, e: (i, 0)),
        out_shape=jax.ShapeDtypeStruct((n, d), F32),
        scratch_shapes=[pltpu.VMEM((d, tt), F32), pltpu.VMEM(tile_buf, F32), pltpu.VMEM(tile_buf, F32),
                        pltpu.VMEM((2,) + tile_buf, BF16)],
        compiler_params=_cp("parallel", "arbitrary"), name="peer_dense",
    )(h_bf16, u_bf16, vt_bf16, th, s2, w1, w2, x, gate)


def _s5_params(a_re, a_im, b_re, b_im, c_re, c_im, log_step):
    lam = lax.complex(a_re.astype(F32), a_im.astype(F32))
    lam_bar = jnp.exp(lam * jnp.exp(log_step.astype(F32))[..., None])
    b_bar = ((lam_bar - 1.0) / lam)[..., None] * lax.complex(b_re.astype(F32), b_im.astype(F32))
    ngrp, npst, nch = b_bar.shape[1:]
    eye = jnp.eye(ngrp, dtype=F32)

    def b_mat(part):
        return jnp.einsum("dgpj,gh->dgjhp", part, eye).reshape(2, ngrp * nch, ngrp * npst)

    def c_mat(part):
        return jnp.einsum("dgjp,gh->dgphj", part, eye).reshape(2, ngrp * npst, ngrp * nch)

    bre, bim = b_mat(b_bar.real).astype(BF16), b_mat(b_bar.imag).astype(BF16)
    cre, cim = c_mat(c_re.astype(F32)).astype(BF16), c_mat(-c_im.astype(F32)).astype(BF16)
    lam2 = jnp.stack([lam_bar.real.reshape(2, -1), lam_bar.imag.reshape(2, -1)], axis=1)
    return bre, bim, cre, cim, lam2


def _pos_embed(n_tok, d, grid_w):
    rows = n_tok // grid_w
    quarter = d // 4
    omega = 1.0 / (10000.0 ** (jnp.arange(quarter, dtype=F32) / quarter))

    def emb1d(pos):
        ang = pos.astype(F32)[:, None] * omega[None]
        return jnp.concatenate([jnp.sin(ang), jnp.cos(ang)], axis=-1)

    er = emb1d(jnp.arange(rows))
    ec = emb1d(jnp.arange(grid_w))
    half = d // 2
    pe = jnp.concatenate([jnp.broadcast_to(er[:, None], (rows, grid_w, half)),
                          jnp.broadcast_to(ec[None], (rows, grid_w, half))], axis=-1)
    return pe.reshape(rows * grid_w, d)


def _tile(n, pref):
    return pref if n % pref == 0 else n


def _trunk(x, mods, s5_h0, ml_c0, ml_n0, ml_m0, p, nseq, seq_len, rows_per_mod):
    n, d = x.shape
    tm = _tile(min(rows_per_mod, n), 512)
    depth = p["norm_g"].shape[0]
    s5_fin, ml_fin = [], []
    for l in range(depth):
        sh1, sc1, g1, sh2, sc2, g2 = mods[l]
        i = l // 2
        if l % 2 == 0:
            hw = p["hy_bias"].shape[2]
            sw = p["s5_d"].shape[1]
            proj = normmod_matmul(x, p["norm_g"][l, 0], sc1, sh1, p["ev_w_in"][i].astype(BF16), rows_per_mod, tm,
                                  _tile(3 * hw + sw, 512))
            hy_in = short_conv(proj, 3 * hw, p["hy_conv_w"][i], p["hy_conv_b"][i], jnp.ones((3 * hw,), F32),
                               seq_len, act=False)
            cos_t, a_t, a_tt = dft_tables(seq_len)
            tf = _tile(seq_len, 512)
            taps, sumsq = hyena_filter_taps(seq_len, p["hy_w1"][i], p["hy_b1"][i], p["hy_w2"][i], p["hy_b2"][i],
                                            p["hy_w3"][i], p["hy_freq"][i], p["hy_decay"][i], hw)
            kr, ki = hyena_filter_spectrum(cos_t, a_t, taps, sumsq, hw, tf)
            bias = p["hy_bias"][i].astype(F32)
            z, zcol = hy_in, 0
            for o in range(bias.shape[0]):
                yr, yi = hyena_fwd(cos_t, a_t, z, zcol, kr, ki, o, nseq, hw, tf)
                z = hyena_inv(cos_t, a_tt, yr, yi, z, zcol, hy_in, 1 + o, bias[o:o + 1], nseq, hw, tf)
                zcol = 0
            bre, bim, cre, cim, lam2 = _s5_params(p["s5_a_re"][i], p["s5_a_im"][i], p["s5_b_re"][i], p["s5_b_im"][i],
                                                  p["s5_c_re"][i], p["s5_c_im"][i], p["s5_log_step"][i])
            ucol = 3 * hw // sw
            y2, hfin = s5_scan(proj, ucol, bre, bim, cre, cim, lam2, s5_h0[i], nseq, seq_len, sw, _tile(seq_len, 256))
            s5_fin.append(hfin)
            s5o = s5_glu(y2, proj, ucol, p["s5_d"][i], p["s5_glu_w"][i].astype(BF16), p["s5_glu_b"][i], tm)
            x = even_out(z, s5o, p["ev_w_out"][i].astype(BF16), x, g1, rows_per_mod, tm, _tile(d, 512))
        else:
            nh = p["od_gate_b"].shape[2]
            w = p["ml_norm_g"].shape[1]
            dh = w // nh
            w_in = p["od_w_in"][i]
            proj = normmod_matmul(x, p["norm_g"][l, 0], sc1, sh1, w_in[:, :4 * w].astype(BF16), rows_per_mod, tm,
                                  _tile(4 * w, 512))
            wg = w_in[:, 4 * w:].reshape(d, 4, nh)
            gb = p["od_gate_b"][i].astype(F32)
            wg2 = jnp.zeros((2, d, LANES), F32)
            bg2 = jnp.zeros((2, 1, LANES), F32)
            for dr in range(2):
                wg2 = wg2.at[dr, :, :nh].set(wg[:, dr]).at[dr, :, nh:2 * nh].set(wg[:, 2 + dr])
                bg2 = bg2.at[dr, 0, :nh].set(gb[dr]).at[dr, 0, nh:2 * nh].set(gb[2 + dr])
            gates = mlstm_gates(x, p["norm_g"][l, 0], sc1, sh1, wg2.astype(BF16), bg2, rows_per_mod, tm)
            qscale = jnp.concatenate([jnp.full((w,), dh ** -0.5, F32), jnp.ones((w,), F32)])
            qk = short_conv(proj, 2 * w, p["ml_conv_w"][i], p["ml_conv_b"][i], qscale, seq_len, act=True)
            h2, cf, nf, mf = mlstm_scan(qk, proj, 2, gates, ml_c0[i], ml_n0[i], ml_m0[i], nseq, seq_len, nh, dh)
            ml_fin.append((cf, nf, mf))
            x = odd_out(h2, proj, 3, p["ml_norm_g"][i], p["od_w_out"][i].astype(BF16), x, g1, rows_per_mod, nh, dh, tm,
                        _tile(d, 512))
        hn = normmod(x, p["norm_g"][l, 1], sc2, sh2, rows_per_mod, tm)
        tt = _tile(min(rows_per_mod, n), 512)
        th, s2, w1, w2 = peer_scores(hn, p["pk_w_q"][l].T.astype(BF16), p["pk_keys"][l].astype(F32), _tile(tt, 256))
        ec = 1024
        vt = p["pk_v"][l].astype(BF16).reshape(-1, ec, d).transpose(0, 2, 1)
        x = peer_dense(hn, p["pk_u"][l].astype(BF16), vt, th, s2, w1, w2, x, g2, rows_per_mod, tt, ec)
    y = final_norm(x, p["final_g"], tm)
    return y, s5_fin, ml_fin


def kernel(x_prompt, x_sample, state_s5_re, state_s5_im, state_mlstm_C, state_mlstm_n, state_mlstm_m, c, c_ctx, norm_g, ada_w, ada_b, final_g, ev_w_in, hy_conv_w, hy_conv_b, hy_w1, hy_b1, hy_w2, hy_b2, hy_w3, hy_freq, hy_decay, hy_bias, s5_a_re, s5_a_im, s5_b_re, s5_b_im, s5_c_re, s5_c_im, s5_log_step, s5_d, s5_glu_w, s5_glu_b, ev_w_out, od_w_in, od_gate_b, ml_conv_w, ml_conv_b, ml_norm_g, od_w_out, pk_w_q, pk_keys, pk_u, pk_v):
    p = dict(norm_g=norm_g, ada_w=ada_w, ada_b=ada_b, final_g=final_g, ev_w_in=ev_w_in,
             hy_conv_w=hy_conv_w, hy_conv_b=hy_conv_b, hy_w1=hy_w1, hy_b1=hy_b1, hy_w2=hy_w2, hy_b2=hy_b2,
             hy_w3=hy_w3, hy_freq=hy_freq, hy_decay=hy_decay, hy_bias=hy_bias, s5_a_re=s5_a_re,
             s5_a_im=s5_a_im, s5_b_re=s5_b_re, s5_b_im=s5_b_im, s5_c_re=s5_c_re, s5_c_im=s5_c_im,
             s5_log_step=s5_log_step, s5_d=s5_d, s5_glu_w=s5_glu_w, s5_glu_b=s5_glu_b, ev_w_out=ev_w_out,
             od_w_in=od_w_in, od_gate_b=od_gate_b, ml_conv_w=ml_conv_w, ml_conv_b=ml_conv_b,
             ml_norm_g=ml_norm_g, od_w_out=od_w_out, pk_w_q=pk_w_q, pk_keys=pk_keys, pk_u=pk_u, pk_v=pk_v)
    nb, seq, d = x_prompt.shape
    db, dseq, _ = x_sample.shape
    depth = norm_g.shape[0]
    n_even, n_odd = (depth + 1) // 2, depth // 2
    assert db + 1 <= 8

    cond8 = jnp.zeros((8, d), F32).at[0].set(c_ctx.astype(F32)).at[1:1 + db].set(c.astype(F32))
    mods_ctx, mods_lat = [], []
    for l in range(depth):
        mod = ada_mod(cond8, ada_w[l].astype(F32), ada_b[l].astype(F32))
        chunks = [mod[:, j * d:(j + 1) * d] for j in range(6)]
        mods_ctx.append([ch[0:1].reshape(1, 1, d) for ch in chunks])
        mods_lat.append([ch[1:1 + db].reshape(db, 1, d) for ch in chunks])

    def s5_state(re, im, bsz):
        return [jnp.stack([re[:, i].reshape(bsz, 2, -1), im[:, i].reshape(bsz, 2, -1)], axis=2).astype(F32)
                for i in range(n_even)]

    ngrp, npst = s5_a_re.shape[2], s5_a_re.shape[3]
    nh, dh = state_mlstm_C.shape[3], state_mlstm_C.shape[4]
    zeros_s5 = jnp.zeros((nb, n_even, 2, ngrp, npst), F32)
    y_prompt, s5_fin, ml_fin = _trunk(
        x_prompt.reshape(nb * seq, d), mods_ctx, s5_state(zeros_s5, zeros_s5, nb),
        [jnp.zeros((nb, 2, nh, dh, dh), F32)] * n_odd, [jnp.zeros((nb, 2, nh, dh), F32)] * n_odd,
        [jnp.zeros((nb, 2, nh, 1), F32)] * n_odd, p, nb, seq, nb * seq)
    x_lat = add_pos(x_sample.reshape(db * dseq, d), _pos_embed(dseq, d, GRID_W), dseq, _tile(dseq, 512))
    y_sample, _, _ = _trunk(
        x_lat, mods_lat, s5_state(state_s5_re, state_s5_im, db),
        [state_mlstm_C[:, i].astype(F32) for i in range(n_odd)], [state_mlstm_n[:, i].astype(F32) for i in range(n_odd)],
        [state_mlstm_m[:, i].astype(F32)[..., None] for i in range(n_odd)], p, db, dseq, dseq)

    new_s5_re = jnp.stack([h[:, :, 0].reshape(nb, 2, ngrp, npst) for h in s5_fin], axis=1)
    new_s5_im = jnp.stack([h[:, :, 1].reshape(nb, 2, ngrp, npst) for h in s5_fin], axis=1)
    new_c = jnp.stack([f[0] for f in ml_fin], axis=1)
    new_n = jnp.stack([f[1] for f in ml_fin], axis=1)
    new_m = jnp.stack([f[2][..., 0] for f in ml_fin], axis=1)
    return (y_prompt.reshape(nb, seq, d), y_sample.reshape(db, dseq, d), new_s5_re, new_s5_im, new_c, new_n, new_m)
```

```python
import functools
import math

import jax
import jax.numpy as jnp
from jax import lax
from jax.experimental import pallas as pl
from jax.experimental.pallas import tpu as pltpu

F32 = jnp.float32
BF16 = jnp.bfloat16
EPS = 1e-6
HIGHEST = lax.Precision.HIGHEST
V7X_VMEM_LIMIT_BYTES = 56 * 1024 * 1024
LANES = 128
SUBLANES = 8
ML_CHUNK = 128
PK_TOPK = 16
GRID_W = 64
NEG_INF = float("-inf")


def _cp(*sem):
    return pltpu.CompilerParams(dimension_semantics=sem, vmem_limit_bytes=V7X_VMEM_LIMIT_BYTES)


def _dot(a, b, **kw):
    return jnp.dot(a, b, preferred_element_type=F32, **kw)


def _dot_nt(a, b):
    return lax.dot_general(a, b, (((1,), (1,)), ((), ())), preferred_element_type=F32)


def _silu(x):
    return x * jax.nn.sigmoid(x)


def _ada_kernel(c_ref, w_ref, b_ref, o_ref):
    o_ref[...] = _dot(_silu(c_ref[...]), w_ref[...], precision=HIGHEST) + b_ref[...]


def ada_mod(cond8, w, b):
    d, no = w.shape
    tn = 1536 if no % 1536 == 0 else no
    return pl.pallas_call(
        _ada_kernel, grid=(no // tn,),
        in_specs=[pl.BlockSpec((8, d), lambda j: (0, 0)), pl.BlockSpec((d, tn), lambda j: (0, j)),
                  pl.BlockSpec((1, tn), lambda j: (0, j))],
        out_specs=pl.BlockSpec((8, tn), lambda j: (0, j)),
        out_shape=jax.ShapeDtypeStruct((8, no), F32), compiler_params=_cp("parallel"), name="ada_mod",
    )(cond8, w, b.reshape(1, no))


def _normmod(x, g, sc, sh):
    y = x * lax.rsqrt(jnp.mean(x * x, axis=-1, keepdims=True) + EPS)
    return (y * g) * (1.0 + sc) + sh


def _mod_spec(d, tm, rows_per_mod, nd=2):
    if nd == 2:
        return pl.BlockSpec((1, 1, d), lambda i, j: ((i * tm) // rows_per_mod, 0, 0))
    return pl.BlockSpec((1, 1, d), lambda i: ((i * tm) // rows_per_mod, 0, 0))


def _nm_matmul_kernel(x_ref, g_ref, sc_ref, sh_ref, w_ref, o_ref, h_ref):
    @pl.when(pl.program_id(1) == 0)
    def _():
        h_ref[...] = _normmod(x_ref[...], g_ref[...], sc_ref[0], sh_ref[0]).astype(BF16)
    o_ref[...] = _dot(h_ref[...], w_ref[...])


def normmod_matmul(x, g, sc, sh, w_bf16, rows_per_mod, tm, tn):
    n, d = x.shape
    no = w_bf16.shape[1]
    return pl.pallas_call(
        _nm_matmul_kernel, grid=(n // tm, no // tn),
        in_specs=[pl.BlockSpec((tm, d), lambda i, j: (i, 0)), pl.BlockSpec((1, d), lambda i, j: (0, 0)),
                  _mod_spec(d, tm, rows_per_mod), _mod_spec(d, tm, rows_per_mod),
                  pl.BlockSpec((d, tn), lambda i, j: (0, j))],
        out_specs=pl.BlockSpec((tm, tn), lambda i, j: (i, j)),
        out_shape=jax.ShapeDtypeStruct((n, no), F32),
        scratch_shapes=[pltpu.VMEM((tm, d), BF16)],
        compiler_params=_cp("parallel", "arbitrary"), name="normmod_matmul",
    )(x, g.reshape(1, d), sc, sh, w_bf16)


def _nm_kernel(x_ref, g_ref, sc_ref, sh_ref, o_ref):
    o_ref[...] = _normmod(x_ref[...], g_ref[...], sc_ref[0], sh_ref[0]).astype(o_ref.dtype)


def normmod(x, g, sc, sh, rows_per_mod, tm):
    n, d = x.shape
    return pl.pallas_call(
        _nm_kernel, grid=(n // tm,),
        in_specs=[pl.BlockSpec((tm, d), lambda i: (i, 0)), pl.BlockSpec((1, d), lambda i: (0, 0)),
                  _mod_spec(d, tm, rows_per_mod, 1), _mod_spec(d, tm, rows_per_mod, 1)],
        out_specs=pl.BlockSpec((tm, d), lambda i: (i, 0)),
        out_shape=jax.ShapeDtypeStruct((n, d), BF16), compiler_params=_cp("parallel"), name="normmod",
    )(x, g.reshape(1, d), sc, sh)


def _final_norm_kernel(x_ref, g_ref, o_ref):
    x = x_ref[...]
    o_ref[...] = (x * lax.rsqrt(jnp.mean(x * x, axis=-1, keepdims=True) + EPS)) * g_ref[...]


def final_norm(x, g, tm):
    n, d = x.shape
    return pl.pallas_call(
        _final_norm_kernel, grid=(n // tm,),
        in_specs=[pl.BlockSpec((tm, d), lambda i: (i, 0)), pl.BlockSpec((1, d), lambda i: (0, 0))],
        out_specs=pl.BlockSpec((tm, d), lambda i: (i, 0)),
        out_shape=jax.ShapeDtypeStruct((n, d), F32), compiler_params=_cp("parallel"), name="final_norm",
    )(x, g.reshape(1, d))


def _add_rows_kernel(x_ref, p_ref, o_ref):
    o_ref[...] = x_ref[...] + p_ref[...]


def add_pos(x, pe, seq_len, tm):
    n, d = x.shape
    nb = seq_len // tm
    return pl.pallas_call(
        _add_rows_kernel, grid=(n // tm,),
        in_specs=[pl.BlockSpec((tm, d), lambda i: (i, 0)), pl.BlockSpec((tm, d), lambda i: (i % nb, 0))],
        out_specs=pl.BlockSpec((tm, d), lambda i: (i, 0)),
        out_shape=jax.ShapeDtypeStruct((n, d), F32), compiler_params=_cp("parallel"), name="add_pos",
    )(x, pe)


def _sconv_kernel(x_ref, w_ref, b_ref, s_ref, o_ref, *, act):
    x = x_ref[...]
    n_tok = x.shape[0]
    row = lax.broadcasted_iota(jnp.int32, x.shape, 0)
    prev = jnp.where(row == 0, 0.0, pltpu.roll(x, 1, 0))
    nxt = jnp.where(row == n_tok - 1, 0.0, pltpu.roll(x, n_tok - 1, 0))
    y = prev * w_ref[0:1, :] + x * w_ref[1:2, :] + nxt * w_ref[2:3, :] + b_ref[...]
    if act:
        y = _silu(y) * s_ref[...]
    o_ref[...] = y


def short_conv(a, ncols, w, b, scale, seq_len, act, cb=256):
    n = a.shape[0]
    return pl.pallas_call(
        functools.partial(_sconv_kernel, act=act), grid=(n // seq_len, ncols // cb),
        in_specs=[pl.BlockSpec((seq_len, cb), lambda s, j: (s, j)), pl.BlockSpec((3, cb), lambda s, j: (0, j)),
                  pl.BlockSpec((1, cb), lambda s, j: (0, j)), pl.BlockSpec((1, cb), lambda s, j: (0, j))],
        out_specs=pl.BlockSpec((seq_len, cb), lambda s, j: (s, j)),
        out_shape=jax.ShapeDtypeStruct((n, ncols), F32), compiler_params=_cp("parallel", "parallel"),
        name="short_conv",
    )(a, w, b.reshape(1, ncols), scale.reshape(1, ncols))


def dft_tables(n_tok):
    k = jnp.arange(n_tok, dtype=jnp.int32)
    blk = 1 << ((n_tok.bit_length() - 1) // 2)
    def thin(n):
        ang = ((k[:, None] * n[None, :]) % (2 * n_tok)).astype(F32) * (math.pi / n_tok)
        return jnp.cos(ang), jnp.sin(ang)
    (c_hi, s_hi), (c_lo, s_lo) = thin(jnp.arange(0, n_tok, blk, dtype=jnp.int32)), thin(jnp.arange(blk, dtype=jnp.int32))
    cos_t = (c_hi[:, :, None] * c_lo[:, None, :] - s_hi[:, :, None] * s_lo[:, None, :]).reshape(n_tok, n_tok)
    msin = -(s_hi[:, :, None] * c_lo[:, None, :] + c_hi[:, :, None] * s_lo[:, None, :]).reshape(n_tok, n_tok)
    alt = jnp.where(k % 2 == 0, 1.0, -1.0).astype(F32)
    a_t = msin.at[0, :].set(alt)
    a_tt = msin.at[:, 0].set(alt)
    return cos_t.astype(BF16), a_t.astype(BF16), a_tt.astype(BF16)


def _hyfilt_kernel(band_ref, w1_ref, b1_ref, w2_ref, b2_ref, w3_ref, fr_ref, dec_ref, h_ref, ss_ref, *,
                   n_tok, tl, hw, nbands):
    i = pl.program_id(0)
    pos = i * tl + lax.broadcasted_iota(jnp.int32, (tl, 1), 0)
    t = pos.astype(F32) / n_tok
    lane = lax.broadcasted_iota(jnp.int32, (tl, LANES), 1)
    ang = 2.0 * math.pi * t * band_ref[...]
    z = jnp.where(lane == 0, t, jnp.where(lane <= nbands, jnp.cos(ang),
                                          jnp.where(lane <= 2 * nbands, jnp.sin(ang), 0.0)))
    fr = fr_ref[...]
    h = jnp.sin(fr * (_dot(z, w1_ref[...], precision=HIGHEST) + b1_ref[...]))
    h = jnp.sin(fr * (_dot(h, w2_ref[...], precision=HIGHEST) + b2_ref[...]))
    h = _dot(h, w3_ref[...], precision=HIGHEST) * jnp.exp(-t * jnp.abs(dec_ref[...]))
    col = lax.broadcasted_iota(jnp.int32, h.shape, 1)
    is_bwd = (col // hw) % 2 == 1
    h = jnp.where(jnp.logical_and(is_bwd, pos == 0), 0.0, h)
    h_ref[...] = h.astype(BF16)

    @pl.when(i == 0)
    def _():
        ss_ref[...] = jnp.zeros_like(ss_ref)
    ss_ref[...] += jnp.sum(h * h, axis=0, keepdims=True)


def hyena_filter_taps(n_tok, w1, b1, w2, b2, w3, freq, decay, hw):
    emb, ffn = w1.shape
    nbands = (emb - 1) // 2
    tl = min(n_tok, 512)
    bands = jnp.linspace(1e-4, nbands - 1, nbands, dtype=F32)
    band_row = jnp.zeros((1, LANES), F32).at[0, 1:1 + nbands].set(bands).at[0, 1 + nbands:1 + 2 * nbands].set(bands)
    w1p = jnp.zeros((LANES, ffn), F32).at[:emb].set(w1)
    nc = w3.shape[1]
    full = lambda shp: pl.BlockSpec(shp, lambda i: (0, 0))
    return pl.pallas_call(
        functools.partial(_hyfilt_kernel, n_tok=n_tok, tl=tl, hw=hw, nbands=nbands), grid=(n_tok // tl,),
        in_specs=[full((1, LANES)), full((LANES, ffn)), full((1, ffn)), full((ffn, ffn)), full((1, ffn)),
                  full((ffn, nc)), full((1, ffn)), full((1, nc))],
        out_specs=[pl.BlockSpec((tl, nc), lambda i: (i, 0)), full((1, nc))],
        out_shape=[jax.ShapeDtypeStruct((n_tok, nc), BF16), jax.ShapeDtypeStruct((1, nc), F32)],
        compiler_params=_cp("arbitrary"), name="hyena_filter_taps",
    )(band_row, w1p, b1.reshape(1, ffn), w2, b2.reshape(1, ffn), w3, freq.reshape(1, ffn), decay.reshape(1, nc))


def _filt_dft_kernel(c_ref, a_ref, h_ref, ss_ref, kr_ref, ki_ref, *, tf, hw):
    i = pl.program_id(1)
    hf = h_ref[:, :hw]
    hb = h_ref[:, hw:]
    cc = c_ref[...]
    aa = a_ref[...]
    zrf, zif, zrb, zib = _dot(cc, hf), _dot(aa, hf), _dot(cc, hb), _dot(aa, hb)
    scale = lax.rsqrt(ss_ref[:, :hw] + ss_ref[:, hw:] + EPS)
    first = (i * tf + lax.broadcasted_iota(jnp.int32, (tf, 1), 0)) == 0
    scale = scale * jnp.where(first, 0.5, 1.0)
    kr_ref[0] = (zrf + zrb) * scale
    ki_ref[0] = jnp.where(first, zif + zib, zif - zib) * scale


def hyena_filter_spectrum(cos_t, a_t, taps, sumsq, hw, tf):
    n_tok = cos_t.shape[0]
    norder = taps.shape[1] // (2 * hw)
    out = jax.ShapeDtypeStruct((norder, n_tok, hw), F32)
    return pl.pallas_call(
        functools.partial(_filt_dft_kernel, tf=tf, hw=hw), grid=(norder, n_tok // tf),
        in_specs=[pl.BlockSpec((tf, n_tok), lambda o, i: (i, 0)), pl.BlockSpec((tf, n_tok), lambda o, i: (i, 0)),
                  pl.BlockSpec((n_tok, 2 * hw), lambda o, i: (0, o)), pl.BlockSpec((1, 2 * hw), lambda o, i: (0, o))],
        out_specs=[pl.BlockSpec((1, tf, hw), lambda o, i: (o, i, 0))] * 2,
        out_shape=[out, out], compiler_params=_cp("parallel", "parallel"), name="hyena_filter_spectrum",
    )(cos_t, a_t, taps, sumsq)


def _hy_fwd_kernel(c_ref, a_ref, z_ref, kr_ref, ki_ref, yr_ref, yi_ref, *, tf):
    i = pl.program_id(0)
    zb = z_ref[...].astype(BF16)
    zr = _dot(c_ref[...], zb)
    zi = _dot(a_ref[...], zb)
    kr = kr_ref[0]
    ki = ki_ref[0]
    first = (i * tf + lax.broadcasted_iota(jnp.int32, (tf, 1), 0)) == 0
    yr_ref[...] = jnp.where(first, zr * kr, zr * kr - zi * ki).astype(BF16)
    yi_ref[...] = jnp.where(first, zi * ki, zr * ki + zi * kr).astype(BF16)


def hyena_fwd(cos_t, a_t, z, zcol, kr, ki, order, nseq, hw, tf):
    n_tok = cos_t.shape[0]
    nf = n_tok // tf
    out = jax.ShapeDtypeStruct((nseq * n_tok, hw), BF16)
    return pl.pallas_call(
        functools.partial(_hy_fwd_kernel, tf=tf), grid=(nf, nseq),
        in_specs=[pl.BlockSpec((tf, n_tok), lambda i, b: (i, 0)), pl.BlockSpec((tf, n_tok), lambda i, b: (i, 0)),
                  pl.BlockSpec((n_tok, hw), lambda i, b: (b, zcol)),
                  pl.BlockSpec((1, tf, hw), lambda i, b: (order, i, 0)),
                  pl.BlockSpec((1, tf, hw), lambda i, b: (order, i, 0))],
        out_specs=[pl.BlockSpec((tf, hw), lambda i, b: (b * nf + i, 0))] * 2,
        out_shape=[out, out], compiler_params=_cp("parallel", "parallel"), name="hyena_fwd",
    )(cos_t, a_t, z, kr, ki)


def _hy_inv_kernel(c_ref, at_ref, yr_ref, yi_ref, zp_ref, gate_ref, bias_ref, o_ref, *, inv_len):
    conv = (_dot(c_ref[...], yr_ref[...]) + _dot(at_ref[...], yi_ref[...])) * inv_len
    o_ref[...] = gate_ref[...] * (conv + bias_ref[...] * zp_ref[...])


def hyena_inv(cos_t, a_tt, yr, yi, zprev, zcol, gates, gcol, bias_row, nseq, hw, tf):
    n_tok = cos_t.shape[0]
    nf = n_tok // tf
    return pl.pallas_call(
        functools.partial(_hy_inv_kernel, inv_len=1.0 / n_tok), grid=(nf, nseq),
        in_specs=[pl.BlockSpec((tf, n_tok), lambda i, b: (i, 0)), pl.BlockSpec((tf, n_tok), lambda i, b: (i, 0)),
                  pl.BlockSpec((n_tok, hw), lambda i, b: (b, 0)), pl.BlockSpec((n_tok, hw), lambda i, b: (b, 0)),
                  pl.BlockSpec((tf, hw), lambda i, b: (b * nf + i, zcol)),
                  pl.BlockSpec((tf, hw), lambda i, b: (b * nf + i, gcol)),
                  pl.BlockSpec((1, hw), lambda i, b: (0, 0))],
        out_specs=pl.BlockSpec((tf, hw), lambda i, b: (b * nf + i, 0)),
        out_shape=jax.ShapeDtypeStruct((nseq * n_tok, hw), F32),
        compiler_params=_cp("parallel", "parallel"), name="hyena_inv",
    )(cos_t, a_tt, yr, yi, zprev, gates, bias_row)


S5_DIAG_BLOCKS = 2


def _s5_kernel(u_ref, bre_ref, bim_ref, cre_ref, cim_ref, lam_ref, h0_ref, y_ref, hfin_ref, hre_s, him_s, st_s, *,
               tc, nc, ns):
    d = pl.program_id(0)
    c = pl.program_id(2)

    @pl.when(c == 0)
    def _():
        st_s[...] = h0_ref[0, 0]

    sw = u_ref.shape[1]
    halves = [(slice(j * sw // S5_DIAG_BLOCKS, (j + 1) * sw // S5_DIAG_BLOCKS),
               slice(j * ns // S5_DIAG_BLOCKS, (j + 1) * ns // S5_DIAG_BLOCKS)) for j in range(S5_DIAG_BLOCKS)]
    ub = u_ref[...].astype(BF16)
    for us, hs in halves:
        hre_s[:, hs] = _dot(ub[:, us], bre_ref[0, us, hs])
        him_s[:, hs] = _dot(ub[:, us], bim_ref[0, us, hs])
    lr = lam_ref[0, 0:1, :]
    li = lam_ref[0, 1:2, :]

    def body(t, carry):
        hr, hi = carry
        r = jnp.where(d == 0, t, tc - 1 - t)
        nr = lr * hr - li * hi + hre_s[pl.ds(r, 1), :]
        ni = lr * hi + li * hr + him_s[pl.ds(r, 1), :]
        hre_s[pl.ds(r, 1), :] = nr
        him_s[pl.ds(r, 1), :] = ni
        return nr, ni

    hr, hi = lax.fori_loop(0, tc, body, (st_s[0:1, :], st_s[1:2, :]), unroll=8)
    st_s[0:1, :] = hr
    st_s[1:2, :] = hi
    for us, hs in halves:
        y_ref[0, :, us] = (_dot(hre_s[:, hs].astype(BF16), cre_ref[0, hs, us])
                           + _dot(him_s[:, hs].astype(BF16), cim_ref[0, hs, us]))

    @pl.when(c == nc - 1)
    def _():
        hfin_ref[0, 0] = st_s[...]


def s5_scan(proj, ucol, bre, bim, cre, cim, lam, h0, nseq, seq_len, sw, tc):
    ns = bre.shape[2]
    nc = seq_len // tc

    def chunk(d, c):
        return c + d * (nc - 1 - 2 * c)

    return pl.pallas_call(
        functools.partial(_s5_kernel, tc=tc, nc=nc, ns=ns), grid=(2, nseq, nc),
        in_specs=[pl.BlockSpec((tc, sw), lambda d, b, c: (b * nc + chunk(d, c), ucol)),
                  pl.BlockSpec((1, sw, ns), lambda d, b, c: (d, 0, 0)),
                  pl.BlockSpec((1, sw, ns), lambda d, b, c: (d, 0, 0)),
                  pl.BlockSpec((1, ns, sw), lambda d, b, c: (d, 0, 0)),
                  pl.BlockSpec((1, ns, sw), lambda d, b, c: (d, 0, 0)),
                  pl.BlockSpec((1, 2, ns), lambda d, b, c: (d, 0, 0)),
                  pl.BlockSpec((1, 1, 2, ns), lambda d, b, c: (b, d, 0, 0))],
        out_specs=[pl.BlockSpec((1, tc, sw), lambda d, b, c: (d, b * nc + chunk(d, c), 0)),
                   pl.BlockSpec((1, 1, 2, ns), lambda d, b, c: (b, d, 0, 0))],
        out_shape=[jax.ShapeDtypeStruct((2, nseq * seq_len, sw), F32), jax.ShapeDtypeStruct((nseq, 2, 2, ns), F32)],
        scratch_shapes=[pltpu.VMEM((tc, ns), F32), pltpu.VMEM((tc, ns), F32), pltpu.VMEM((2, ns), F32)],
        compiler_params=_cp("parallel", "parallel", "arbitrary"), name="s5_scan",
    )(proj, bre, bim, cre, cim, lam, h0)


def _s5_glu_kernel(yf_ref, yb_ref, u_ref, d_ref, w_ref, b_ref, o_ref):
    y = jax.nn.gelu(yf_ref[0] + yb_ref[0] + d_ref[...] * u_ref[...])
    o_ref[...] = y * jax.nn.sigmoid(_dot(y.astype(BF16), w_ref[...]) + b_ref[...])


def s5_glu(y2, proj, ucol, d_skip, glu_w_bf16, glu_b, tm):
    _, n, sw = y2.shape
    return pl.pallas_call(
        _s5_glu_kernel, grid=(n // tm,),
        in_specs=[pl.BlockSpec((1, tm, sw), lambda i: (0, i, 0)), pl.BlockSpec((1, tm, sw), lambda i: (1, i, 0)),
                  pl.BlockSpec((tm, sw), lambda i: (i, ucol)), pl.BlockSpec((1, sw), lambda i: (0, 0)),
                  pl.BlockSpec((sw, sw), lambda i: (0, 0)), pl.BlockSpec((1, sw), lambda i: (0, 0))],
        out_specs=pl.BlockSpec((tm, sw), lambda i: (i, 0)),
        out_shape=jax.ShapeDtypeStruct((n, sw), F32), compiler_params=_cp("parallel"), name="s5_glu",
    )(y2, y2, proj, d_skip.reshape(1, sw), glu_w_bf16, glu_b.reshape(1, sw))


def _even_out_kernel(a_ref, b_ref, wa_ref, wb_ref, x_ref, gate_ref, o_ref):
    y = _dot(a_ref[...].astype(BF16), wa_ref[...]) + _dot(b_ref[...].astype(BF16), wb_ref[...])
    o_ref[...] = x_ref[...] + gate_ref[0] * y


def even_out(hy, s5o, w_bf16, x, gate, rows_per_mod, tm, tn):
    n, d = x.shape
    hw = hy.shape[1]
    sw = s5o.shape[1]
    return pl.pallas_call(
        _even_out_kernel, grid=(n // tm, d // tn),
        in_specs=[pl.BlockSpec((tm, hw), lambda i, j: (i, 0)), pl.BlockSpec((tm, sw), lambda i, j: (i, 0)),
                  pl.BlockSpec((hw, tn), lambda i, j: (0, j)), pl.BlockSpec((sw, tn), lambda i, j: (hw // sw, j)),
                  pl.BlockSpec((tm, tn), lambda i, j: (i, j)),
                  pl.BlockSpec((1, 1, tn), lambda i, j: ((i * tm) // rows_per_mod, 0, j))],
        out_specs=pl.BlockSpec((tm, tn), lambda i, j: (i, j)),
        out_shape=jax.ShapeDtypeStruct((n, d), F32), compiler_params=_cp("parallel", "parallel"), name="even_out",
    )(hy, s5o, w_bf16, w_bf16, x, gate)


def _log_sigmoid(x):
    return jnp.minimum(x, 0.0) - jnp.log1p(jnp.exp(-jnp.abs(x)))


def _gates_kernel(x_ref, g_ref, sc_ref, sh_ref, w_ref, b_ref, o_ref):
    h = _normmod(x_ref[...], g_ref[...], sc_ref[0], sh_ref[0]).astype(BF16)
    o_ref[0] = _dot(h, w_ref[0]) + b_ref[0]


def mlstm_gates(x, g, sc, sh, wg_bf16, bg, rows_per_mod, tm):
    n, d = x.shape
    return pl.pallas_call(
        _gates_kernel, grid=(n // tm, 2),
        in_specs=[pl.BlockSpec((tm, d), lambda i, j: (i, 0)), pl.BlockSpec((1, d), lambda i, j: (0, 0)),
                  _mod_spec(d, tm, rows_per_mod), _mod_spec(d, tm, rows_per_mod),
                  pl.BlockSpec((1, d, LANES), lambda i, j: (j, 0, 0)), pl.BlockSpec((1, 1, LANES), lambda i, j: (j, 0, 0))],
        out_specs=pl.BlockSpec((1, tm, LANES), lambda i, j: (j, i, 0)),
        out_shape=jax.ShapeDtypeStruct((2, n, LANES), F32), compiler_params=_cp("parallel", "parallel"),
        name="mlstm_gates",
    )(x, g.reshape(1, d), sc, sh, wg_bf16, bg)


def _mlstm_kernel(q_ref, k_ref, v_ref, g_ref, c0_ref, n0_ref, m0_ref, h_ref, cf_ref, nf_ref, mf_ref,
                  c_s, m_s, *, nh, dh, tc, nc):
    d = pl.program_id(0)
    c = pl.program_id(2)

    @pl.when(c == 0)
    def _():
        for h in range(nh):
            c_s[h, :, :dh] = c0_ref[0, 0, h]
            c_s[h, :, dh:] = jnp.broadcast_to(n0_ref[0, 0, h:h + 1, :], (dh, dh)).T
        m_s[...] = m0_ref[0, 0]

    ones = jnp.ones((tc, dh), F32)
    gates = g_ref[0]
    lane = lax.broadcasted_iota(jnp.int32, gates.shape, 1)
    logf = jnp.where(jnp.logical_and(lane >= nh, lane < 2 * nh), _log_sigmoid(gates), 0.0)
    r_i = lax.broadcasted_iota(jnp.int32, (tc, tc), 0)
    s_i = lax.broadcasted_iota(jnp.int32, (tc, tc), 1)
    causal = (r_i - s_i) * (1 - 2 * d) >= 0
    bcum = _dot(causal.astype(F32), logf, precision=HIGHEST)
    btot = jnp.sum(logf, axis=0, keepdims=True)
    gates_t = gates.T
    bcum_t = bcum.T
    for h in range(nh):
        hs = slice(h * dh, (h + 1) * dh)
        q = q_ref[:, hs]
        k = k_ref[:, hs]
        v = v_ref[:, hs]
        b_col = bcum[:, nh + h:nh + h + 1]
        b_row = bcum_t[nh + h:nh + h + 1, :]
        i_col = gates[:, h:h + 1]
        i_row = gates_t[h:h + 1, :]
        m = m_s[h:h + 1, :]
        a = b_col + m
        dmat = jnp.where(causal, b_col - b_row + i_row, NEG_INF)
        mq = jnp.maximum(a, jnp.max(dmat, axis=-1, keepdims=True))
        w_intra = jnp.exp(dmat - mq)
        w_inter = jnp.exp(a - mq)
        s = _dot_nt(q, k) * w_intra
        cn = c_s[h]
        v1 = jnp.concatenate([v, ones], axis=1)
        both = _dot(s, v1) + w_inter * _dot(q, cn)
        h_ref[0, :, hs] = both[:, :dh] / jnp.maximum(jnp.abs(both[:, dh:]), jnp.exp(-mq))
        b_last = btot[:, nh + h:nh + h + 1]
        g = b_last - b_col + i_col
        m_new = jnp.maximum(b_last + m, jnp.max(g, axis=0, keepdims=True))
        kw = k * jnp.exp(g - m_new)
        keep = jnp.exp(b_last + m - m_new)
        c_s[h] = keep * cn + _dot(kw.T, v1)
        m_s[h:h + 1, :] = m_new

    @pl.when(c == nc - 1)
    def _():
        for h in range(nh):
            cf_ref[0, 0, h] = c_s[h, :, :dh]
            nf_ref[0, 0, h:h + 1, :] = c_s[h, :, dh:].T[0:1, :]
        mf_ref[0, 0] = m_s[...]


def mlstm_scan(qk, proj, vcol, gates, c0, n0, m0, nseq, seq_len, nh, dh):
    tc = ML_CHUNK
    nc = seq_len // tc
    w = nh * dh

    def chunk(d, c):
        return c + d * (nc - 1 - 2 * c)

    rowblk = lambda d, b, c: b * nc + chunk(d, c)
    st = lambda shp: pl.BlockSpec((1, 1) + shp, lambda d, b, c: (b, d) + (0,) * len(shp))
    return pl.pallas_call(
        functools.partial(_mlstm_kernel, nh=nh, dh=dh, tc=tc, nc=nc), grid=(2, nseq, nc),
        in_specs=[pl.BlockSpec((tc, w), lambda d, b, c: (rowblk(d, b, c), 0)),
                  pl.BlockSpec((tc, w), lambda d, b, c: (rowblk(d, b, c), 1)),
                  pl.BlockSpec((tc, w), lambda d, b, c: (rowblk(d, b, c), vcol)),
                  pl.BlockSpec((1, tc, LANES), lambda d, b, c: (d, rowblk(d, b, c), 0)),
                  st((nh, dh, dh)), st((nh, dh)), st((nh, 1))],
        out_specs=[pl.BlockSpec((1, tc, w), lambda d, b, c: (d, rowblk(d, b, c), 0)),
                   st((nh, dh, dh)), st((nh, dh)), st((nh, 1))],
        out_shape=[jax.ShapeDtypeStruct((2, nseq * seq_len, w), F32),
                   jax.ShapeDtypeStruct((nseq, 2, nh, dh, dh), F32), jax.ShapeDtypeStruct((nseq, 2, nh, dh), F32),
                   jax.ShapeDtypeStruct((nseq, 2, nh, 1), F32)],
        scratch_shapes=[pltpu.VMEM((nh, dh, 2 * dh), F32), pltpu.VMEM((nh, 1), F32)],
        compiler_params=_cp("parallel", "parallel", "arbitrary"), name="mlstm_scan",
    )(qk, qk, proj, gates, c0, n0, m0)


def _odd_out_kernel(hf_ref, hb_ref, o_ref, ng_ref, w_ref, x_ref, gate_ref, out_ref, a_s, *, nh, dh):
    @pl.when(pl.program_id(1) == 0)
    def _():
        for h in range(nh):
            hs = slice(h * dh, (h + 1) * dh)
            blk = hf_ref[0, :, hs] + hb_ref[0, :, hs]
            blk = blk * lax.rsqrt(jnp.mean(blk * blk, axis=-1, keepdims=True) + EPS)
            a_s[:, hs] = ((blk * ng_ref[:, hs]) * _silu(o_ref[:, hs])).astype(BF16)
    out_ref[...] = x_ref[...] + gate_ref[0] * _dot(a_s[...], w_ref[...])


def odd_out(h2, proj, ocol, norm_g, w_bf16, x, gate, rows_per_mod, nh, dh, tm, tn):
    n, d = x.shape
    w = nh * dh
    return pl.pallas_call(
        functools.partial(_odd_out_kernel, nh=nh, dh=dh), grid=(n // tm, d // tn),
        in_specs=[pl.BlockSpec((1, tm, w), lambda i, j: (0, i, 0)), pl.BlockSpec((1, tm, w), lambda i, j: (1, i, 0)),
                  pl.BlockSpec((tm, w), lambda i, j: (i, ocol)), pl.BlockSpec((1, w), lambda i, j: (0, 0)),
                  pl.BlockSpec((w, tn), lambda i, j: (0, j)), pl.BlockSpec((tm, tn), lambda i, j: (i, j)),
                  pl.BlockSpec((1, 1, tn), lambda i, j: ((i * tm) // rows_per_mod, 0, j))],
        out_specs=pl.BlockSpec((tm, tn), lambda i, j: (i, j)),
        out_shape=jax.ShapeDtypeStruct((n, d), F32), scratch_shapes=[pltpu.VMEM((tm, w), BF16)],
        compiler_params=_cp("parallel", "arbitrary"), name="odd_out",
    )(h2, h2, proj, norm_g.reshape(1, w), w_bf16, x, gate)


def _top_values(curs, k, outs):
    curs = list(curs)
    for j in range(k):
        for a, out_s in enumerate(outs):
            m = jnp.max(curs[a], axis=0, keepdims=True)
            out_s[j:j + 1, :] = m
            curs[a] = jnp.where(curs[a] == m, NEG_INF, curs[a])


def _pair_candidates(k):
    return [(a, k // (a + 1)) for a in range(k)]


PEER_HEADS_PER_TRIP = 4


def _peer_score_kernel(h_ref, wq_ref, keys_ref, th_ref, s2_ref, w1_ref, w2_ref, q_s, v_s, cand_s, best_s, *,
                       nh, half, topk):
    q_s[...] = _dot_nt(wq_ref[...], h_ref[...])
    kk = topk + 1
    cand_s[...] = jnp.full(cand_s.shape, NEG_INF, F32)
    group = range(PEER_HEADS_PER_TRIP)

    def heads(trip, carry):
        hds = [trip * PEER_HEADS_PER_TRIP + u for u in group]
        scores = []
        for hd in hds:
            base = pl.multiple_of(hd * 2 * half, 2 * half)
            scores.append(_dot(keys_ref[hd, 0], q_s[pl.ds(base, half), :]))
            scores.append(_dot(keys_ref[hd, 1], q_s[pl.ds(base + half, half), :]))
        _top_values(scores, kk, [v_s.at[u, c] for u in group for c in range(2)])
        for u in group:
            off = 0
            for a, cnt in _pair_candidates(kk):
                cand_s[u, off:off + cnt, :] = v_s[u, 0, a:a + 1, :] + v_s[u, 1, 0:cnt, :]
                off += cnt
        _top_values([cand_s[u] for u in group], kk, [best_s.at[u] for u in group])
        for u, hd in enumerate(hds):
            s1, s2 = scores[2 * u], scores[2 * u + 1]
            best = best_s[u, 0:topk, :]
            z = jnp.sum(jnp.exp(best - best[0:1, :]), axis=0, keepdims=True)
            tmid = 0.5 * (best_s[u, topk - 1:topk, :] + best_s[u, topk:topk + 1, :])
            th = tmid - s1
            w1 = jnp.exp(s1 - v_s[u, 0, 0:1, :]) / z
            w2 = jnp.exp(s2 - v_s[u, 1, 0:1, :])
            for lt in range(s1.shape[1] // LANES):
                sl = slice(lt * LANES, (lt + 1) * LANES)
                th_ref[hd, lt] = th[:, sl]
                s2_ref[hd, lt] = s2[:, sl]
                w1_ref[hd, lt] = w1[:, sl]
                w2_ref[hd, lt] = w2[:, sl]
        return carry

    lax.fori_loop(0, nh // PEER_HEADS_PER_TRIP, heads, 0)


def peer_scores(h_bf16, wq_t_bf16, keys, tt):
    n, d = h_bf16.shape
    nh, _, nk, half = keys.shape
    kk = PK_TOPK + 1
    ncand = -(-sum(c for _, c in _pair_candidates(kk)) // 8) * 8
    big = jax.ShapeDtypeStruct((nh, n // LANES, nk, LANES), F32)
    bspec = pl.BlockSpec((nh, tt // LANES, nk, LANES), lambda i: (0, i, 0, 0))
    return pl.pallas_call(
        functools.partial(_peer_score_kernel, nh=nh, half=half, topk=PK_TOPK), grid=(n // tt,),
        in_specs=[pl.BlockSpec((tt, d), lambda i: (i, 0)), pl.BlockSpec((nh * 2 * half, d), lambda i: (0, 0)),
                  pl.BlockSpec((nh, 2, nk, half), lambda i: (0, 0, 0, 0))],
        out_specs=[bspec, bspec, bspec, bspec],
        out_shape=[big, big, big, big],
        scratch_shapes=[pltpu.VMEM((nh * 2 * half, tt), F32), pltpu.VMEM((PEER_HEADS_PER_TRIP, 2, 24, tt), F32),
                        pltpu.VMEM((PEER_HEADS_PER_TRIP, ncand, tt), F32), pltpu.VMEM((PEER_HEADS_PER_TRIP, 24, tt), F32)],
        compiler_params=_cp("parallel"), name="peer_scores",
    )(h_bf16, wq_t_bf16, keys)


PEER_KEY_ROWS = 32
GELU_C1 = math.sqrt(2.0 / math.pi)
GELU_C2 = 0.044715 * GELU_C1


def _gelu_tanh(x):
    half_x = 0.5 * x
    return half_x + half_x * jnp.tanh(x * (GELU_C1 + GELU_C2 * (x * x)))


def _peer_dense_kernel(h_ref, u_ref, vt_ref, th_ref, s2_ref, w1_ref, w2_ref, x_ref, gate_ref, o_ref,
                       acc_s, st_s, wt_s, *, nh, nk, ec, tt):
    e = pl.program_id(1)
    nlt = tt // LANES
    n_i1 = ec // nk
    nkt = nk // PEER_KEY_ROWS
    nsub = PEER_KEY_ROWS // SUBLANES

    @pl.when(e == 0)
    def _():
        acc_s[...] = jnp.zeros_like(acc_s)

    st = _gelu_tanh(_dot_nt(u_ref[...], h_ref[...]))
    for lt in range(nlt):
        st_s[lt] = st[:, lt * LANES:(lt + 1) * LANES]

    def tile(idx, carry):
        lt = idx // nkt
        k0 = (idx % nkt) * PEER_KEY_ROWS
        subs = [pl.ds(pl.multiple_of(k0 + j * SUBLANES, SUBLANES), SUBLANES) for j in range(nsub)]
        g = [[jnp.zeros((SUBLANES, LANES), F32) for _ in subs] for _ in range(n_i1)]
        for hd in range(nh):
            s2t = [s2_ref[hd, lt, sub, :] for sub in subs]
            w2t = [w2_ref[hd, lt, sub, :] for sub in subs]
            for li in range(n_i1):
                thb = jnp.broadcast_to(th_ref[hd, lt, li:li + 1, :], (SUBLANES, LANES))
                w1b = jnp.broadcast_to(w1_ref[hd, lt, li:li + 1, :], (SUBLANES, LANES))
                for j in range(nsub):
                    g[li][j] = g[li][j] + jnp.where(s2t[j] >= thb, w2t[j] * w1b, 0.0)
        for li in range(n_i1):
            rows = pl.ds(pl.multiple_of(li * nk + k0, PEER_KEY_ROWS), PEER_KEY_ROWS)
            wt_s[lt, rows, :] = (st_s[lt, rows, :] * jnp.concatenate(g[li], axis=0)).astype(BF16)
        return carry

    lax.fori_loop(0, nlt * nkt, tile, 0)
    wt = jnp.concatenate([wt_s[lt] for lt in range(nlt)], axis=1)
    acc_s[...] += _dot(vt_ref[0], wt)

    @pl.when(e == pl.num_programs(1) - 1)
    def _():
        o_ref[...] = x_ref[...] + gate_ref[0] * acc_s[...].T


def peer_dense(h_bf16, u_bf16, vt_bf16, th, s2, w1, w2, x, gate, rows_per_mod, tt, ec):
    n, d = x.shape
    nh, _, nk, _ = s2.shape
    nchunk = u_bf16.shape[0] // ec
    bspec = pl.BlockSpec((nh, tt // LANES, nk, LANES), lambda i, e: (0, i, 0, 0))
    rspec = pl.BlockSpec((nh, tt // LANES, ec // nk, LANES), lambda i, e: (0, i, e, 0))
    tile_buf = (tt // LANES, ec, LANES)
    return pl.pallas_call(
        functools.partial(_peer_dense_kernel, nh=nh, nk=nk, ec=ec, tt=tt), grid=(n // tt, nchunk),
        in_specs=[pl.BlockSpec((tt, d), lambda i, e: (i, 0)), pl.BlockSpec((ec, d), lambda i, e: (e, 0)),
                  pl.BlockSpec((1, d, ec), lambda i, e: (e, 0, 0)), rspec, bspec, rspec, bspec,
                  pl.BlockSpec((tt, d), lambda i, e: (i, 0)),
                  pl.BlockSpec((1, 1, d), lambda i, e: ((i * tt) // rows_per_mod, 0, 0))],
        out_specs=pl.BlockSpec((tt, d), lambda i, e: (i, 0)),
        out_shape=jax.ShapeDtypeStruct((n, d), F32),
        scratch_shapes=[pltpu.VMEM((d, tt), F32), pltpu.VMEM(tile_buf, F32), pltpu.VMEM(tile_buf, BF16)],
        compiler_params=_cp("parallel", "arbitrary"), name="peer_dense",
    )(h_bf16, u_bf16, vt_bf16, th, s2, w1, w2, x, gate)


def _s5_params(a_re, a_im, b_re, b_im, c_re, c_im, log_step):
    lam = lax.complex(a_re.astype(F32), a_im.astype(F32))
    lam_bar = jnp.exp(lam * jnp.exp(log_step.astype(F32))[..., None])
    b_bar = ((lam_bar - 1.0) / lam)[..., None] * lax.complex(b_re.astype(F32), b_im.astype(F32))
    ngrp, npst, nch = b_bar.shape[1:]
    eye = jnp.eye(ngrp, dtype=F32)

    def b_mat(part):
        return jnp.einsum("dgpj,gh->dgjhp", part, eye).reshape(2, ngrp * nch, ngrp * npst)

    def c_mat(part):
        return jnp.einsum("dgjp,gh->dgphj", part, eye).reshape(2, ngrp * npst, ngrp * nch)

    bre, bim = b_mat(b_bar.real).astype(BF16), b_mat(b_bar.imag).astype(BF16)
    cre, cim = c_mat(c_re.astype(F32)).astype(BF16), c_mat(-c_im.astype(F32)).astype(BF16)
    lam2 = jnp.stack([lam_bar.real.reshape(2, -1), lam_bar.imag.reshape(2, -1)], axis=1)
    return bre, bim, cre, cim, lam2


def _pos_embed(n_tok, d, grid_w):
    rows = n_tok // grid_w
    quarter = d // 4
    omega = 1.0 / (10000.0 ** (jnp.arange(quarter, dtype=F32) / quarter))

    def emb1d(pos):
        ang = pos.astype(F32)[:, None] * omega[None]
        return jnp.concatenate([jnp.sin(ang), jnp.cos(ang)], axis=-1)

    er = emb1d(jnp.arange(rows))
    ec = emb1d(jnp.arange(grid_w))
    half = d // 2
    pe = jnp.concatenate([jnp.broadcast_to(er[:, None], (rows, grid_w, half)),
                          jnp.broadcast_to(ec[None], (rows, grid_w, half))], axis=-1)
    return pe.reshape(rows * grid_w, d)


def _tile(n, pref):
    return pref if n % pref == 0 else n


def _trunk(x, mods, s5_h0, ml_c0, ml_n0, ml_m0, p, nseq, seq_len, rows_per_mod):
    n, d = x.shape
    tm = _tile(min(rows_per_mod, n), 512)
    depth = p["norm_g"].shape[0]
    s5_fin, ml_fin = [], []
    for l in range(depth):
        sh1, sc1, g1, sh2, sc2, g2 = mods[l]
        i = l // 2
        if l % 2 == 0:
            hw = p["hy_bias"].shape[2]
            sw = p["s5_d"].shape[1]
            proj = normmod_matmul(x, p["norm_g"][l, 0], sc1, sh1, p["ev_w_in"][i].astype(BF16), rows_per_mod, tm,
                                  _tile(3 * hw + sw, 512))
            hy_in = short_conv(proj, 3 * hw, p["hy_conv_w"][i], p["hy_conv_b"][i], jnp.ones((3 * hw,), F32),
                               seq_len, act=False)
            cos_t, a_t, a_tt = dft_tables(seq_len)
            tf = _tile(seq_len, 512)
            taps, sumsq = hyena_filter_taps(seq_len, p["hy_w1"][i], p["hy_b1"][i], p["hy_w2"][i], p["hy_b2"][i],
                                            p["hy_w3"][i], p["hy_freq"][i], p["hy_decay"][i], hw)
            kr, ki = hyena_filter_spectrum(cos_t, a_t, taps, sumsq, hw, tf)
            bias = p["hy_bias"][i].astype(F32)
            z, zcol = hy_in, 0
            for o in range(bias.shape[0]):
                yr, yi = hyena_fwd(cos_t, a_t, z, zcol, kr, ki, o, nseq, hw, tf)
                z = hyena_inv(cos_t, a_tt, yr, yi, z, zcol, hy_in, 1 + o, bias[o:o + 1], nseq, hw, tf)
                zcol = 0
            bre, bim, cre, cim, lam2 = _s5_params(p["s5_a_re"][i], p["s5_a_im"][i], p["s5_b_re"][i], p["s5_b_im"][i],
                                                  p["s5_c_re"][i], p["s5_c_im"][i], p["s5_log_step"][i])
            ucol = 3 * hw // sw
            y2, hfin = s5_scan(proj, ucol, bre, bim, cre, cim, lam2, s5_h0[i], nseq, seq_len, sw, _tile(seq_len, 256))
            s5_fin.append(hfin)
            s5o = s5_glu(y2, proj, ucol, p["s5_d"][i], p["s5_glu_w"][i].astype(BF16), p["s5_glu_b"][i], tm)
            x = even_out(z, s5o, p["ev_w_out"][i].astype(BF16), x, g1, rows_per_mod, tm, _tile(d, 512))
        else:
            nh = p["od_gate_b"].shape[2]
            w = p["ml_norm_g"].shape[1]
            dh = w // nh
            w_in = p["od_w_in"][i]
            proj = normmod_matmul(x, p["norm_g"][l, 0], sc1, sh1, w_in[:, :4 * w].astype(BF16), rows_per_mod, tm,
                                  _tile(4 * w, 512))
            wg = w_in[:, 4 * w:].reshape(d, 4, nh)
            gb = p["od_gate_b"][i].astype(F32)
            wg2 = jnp.zeros((2, d, LANES), F32)
            bg2 = jnp.zeros((2, 1, LANES), F32)
            for dr in range(2):
                wg2 = wg2.at[dr, :, :nh].set(wg[:, dr]).at[dr, :, nh:2 * nh].set(wg[:, 2 + dr])
                bg2 = bg2.at[dr, 0, :nh].set(gb[dr]).at[dr, 0, nh:2 * nh].set(gb[2 + dr])
            gates = mlstm_gates(x, p["norm_g"][l, 0], sc1, sh1, wg2.astype(BF16), bg2, rows_per_mod, tm)
            qscale = jnp.concatenate([jnp.full((w,), dh ** -0.5, F32), jnp.ones((w,), F32)])
            qk = short_conv(proj, 2 * w, p["ml_conv_w"][i], p["ml_conv_b"][i], qscale, seq_len, act=True)
            h2, cf, nf, mf = mlstm_scan(qk, proj, 2, gates, ml_c0[i], ml_n0[i], ml_m0[i], nseq, seq_len, nh, dh)
            ml_fin.append((cf, nf, mf))
            x = odd_out(h2, proj, 3, p["ml_norm_g"][i], p["od_w_out"][i].astype(BF16), x, g1, rows_per_mod, nh, dh, tm,
                        _tile(d, 512))
        hn = normmod(x, p["norm_g"][l, 1], sc2, sh2, rows_per_mod, tm)
        tt = _tile(min(rows_per_mod, n), 512)
        th, s2, w1, w2 = peer_scores(hn, p["pk_w_q"][l].T.astype(BF16), p["pk_keys"][l].astype(F32), _tile(tt, 256))
        ec = 1024
        vt = p["pk_v"][l].astype(BF16).reshape(-1, ec, d).transpose(0, 2, 1)
        x = peer_dense(hn, p["pk_u"][l].astype(BF16), vt, th, s2, w1, w2, x, g2, rows_per_mod, tt, ec)
    y = final_norm(x, p["final_g"], tm)
    return y, s5_fin, ml_fin


def kernel(x_prompt, x_sample, state_s5_re, state_s5_im, state_mlstm_C, state_mlstm_n, state_mlstm_m, c, c_ctx, norm_g, ada_w, ada_b, final_g, ev_w_in, hy_conv_w, hy_conv_b, hy_w1, hy_b1, hy_w2, hy_b2, hy_w3, hy_freq, hy_decay, hy_bias, s5_a_re, s5_a_im, s5_b_re, s5_b_im, s5_c_re, s5_c_im, s5_log_step, s5_d, s5_glu_w, s5_glu_b, ev_w_out, od_w_in, od_gate_b, ml_conv_w, ml_conv_b, ml_norm_g, od_w_out, pk_w_q, pk_keys, pk_u, pk_v):
    p = dict(norm_g=norm_g, ada_w=ada_w, ada_b=ada_b, final_g=final_g, ev_w_in=ev_w_in,
             hy_conv_w=hy_conv_w, hy_conv_b=hy_conv_b, hy_w1=hy_w1, hy_b1=hy_b1, hy_w2=hy_w2, hy_b2=hy_b2,
             hy_w3=hy_w3, hy_freq=hy_freq, hy_decay=hy_decay, hy_bias=hy_bias, s5_a_re=s5_a_re,
             s5_a_im=s5_a_im, s5_b_re=s5_b_re, s5_b_im=s5_b_im, s5_c_re=s5_c_re, s5_c_im=s5_c_im,
             s5_log_step=s5_log_step, s5_d=s5_d, s5_glu_w=s5_glu_w, s5_glu_b=s5_glu_b, ev_w_out=ev_w_out,
             od_w_in=od_w_in, od_gate_b=od_gate_b, ml_conv_w=ml_conv_w, ml_conv_b=ml_conv_b,
             ml_norm_g=ml_norm_g, od_w_out=od_w_out, pk_w_q=pk_w_q, pk_keys=pk_keys, pk_u=pk_u, pk_v=pk_v)
    nb, seq, d = x_prompt.shape
    db, dseq, _ = x_sample.shape
    depth = norm_g.shape[0]
    n_even, n_odd = (depth + 1) // 2, depth // 2
    assert db + 1 <= 8

    cond8 = jnp.zeros((8, d), F32).at[0].set(c_ctx.astype(F32)).at[1:1 + db].set(c.astype(F32))
    mods_ctx, mods_lat = [], []
    for l in range(depth):
        mod = ada_mod(cond8, ada_w[l].astype(F32), ada_b[l].astype(F32))
        chunks = [mod[:, j * d:(j + 1) * d] for j in range(6)]
        mods_ctx.append([ch[0:1].reshape(1, 1, d) for ch in chunks])
        mods_lat.append([ch[1:1 + db].reshape(db, 1, d) for ch in chunks])

    def s5_state(re, im, bsz):
        return [jnp.stack([re[:, i].reshape(bsz, 2, -1), im[:, i].reshape(bsz, 2, -1)], axis=2).astype(F32)
                for i in range(n_even)]

    ngrp, npst = s5_a_re.shape[2], s5_a_re.shape[3]
    nh, dh = state_mlstm_C.shape[3], state_mlstm_C.shape[4]
    zeros_s5 = jnp.zeros((nb, n_even, 2, ngrp, npst), F32)
    y_prompt, s5_fin, ml_fin = _trunk(
        x_prompt.reshape(nb * seq, d), mods_ctx, s5_state(zeros_s5, zeros_s5, nb),
        [jnp.zeros((nb, 2, nh, dh, dh), F32)] * n_odd, [jnp.zeros((nb, 2, nh, dh), F32)] * n_odd,
        [jnp.zeros((nb, 2, nh, 1), F32)] * n_odd, p, nb, seq, nb * seq)
    x_lat = add_pos(x_sample.reshape(db * dseq, d), _pos_embed(dseq, d, GRID_W), dseq, _tile(dseq, 512))
    y_sample, _, _ = _trunk(
        x_lat, mods_lat, s5_state(state_s5_re, state_s5_im, db),
        [state_mlstm_C[:, i].astype(F32) for i in range(n_odd)], [state_mlstm_n[:, i].astype(F32) for i in range(n_odd)],
        [state_mlstm_m[:, i].astype(F32)[..., None] for i in range(n_odd)], p, db, dseq, dseq)

    new_s5_re = jnp.stack([h[:, :, 0].reshape(nb, 2, ngrp, npst) for h in s5_fin], axis=1)
    new_s5_im = jnp.stack([h[:, :, 1].reshape(nb, 2, ngrp, npst) for h in s5_fin], axis=1)
    new_c = jnp.stack([f[0] for f in ml_fin], axis=1)
    new_n = jnp.stack([f[1] for f in ml_fin], axis=1)
    new_m = jnp.stack([f[2][..., 0] for f in ml_fin], axis=1)
    return (y_prompt.reshape(nb, seq, d), y_sample.reshape(db, dseq, d), new_s5_re, new_s5_im, new_c, new_n, new_m)
```

```python
import functools
import math

import jax
import jax.numpy as jnp
from jax import lax
from jax.experimental import pallas as pl
from jax.experimental.pallas import tpu as pltpu

F32 = jnp.float32
BF16 = jnp.bfloat16
EPS = 1e-6
HIGHEST = lax.Precision.HIGHEST
V7X_VMEM_LIMIT_BYTES = 56 * 1024 * 1024
LANES = 128
SUBLANES = 8
ML_CHUNK = 128
PK_TOPK = 16
GRID_W = 64
NEG_INF = float("-inf")


def _cp(*sem):
    return pltpu.CompilerParams(dimension_semantics=sem, vmem_limit_bytes=V7X_VMEM_LIMIT_BYTES)


def _dot(a, b, **kw):
    return jnp.dot(a, b, preferred_element_type=F32, **kw)


def _dot_nt(a, b):
    return lax.dot_general(a, b, (((1,), (1,)), ((), ())), preferred_element_type=F32)


def _silu(x):
    return x * jax.nn.sigmoid(x)


def _ada_kernel(c_ref, w_ref, b_ref, o_ref):
    o_ref[...] = _dot(_silu(c_ref[...]), w_ref[...], precision=HIGHEST) + b_ref[...]


def ada_mod(cond8, w, b):
    d, no = w.shape
    tn = 1536 if no % 1536 == 0 else no
    return pl.pallas_call(
        _ada_kernel, grid=(no // tn,),
        in_specs=[pl.BlockSpec((8, d), lambda j: (0, 0)), pl.BlockSpec((d, tn), lambda j: (0, j)),
                  pl.BlockSpec((1, tn), lambda j: (0, j))],
        out_specs=pl.BlockSpec((8, tn), lambda j: (0, j)),
        out_shape=jax.ShapeDtypeStruct((8, no), F32), compiler_params=_cp("parallel"), name="ada_mod",
    )(cond8, w, b.reshape(1, no))


def _normmod(x, g, sc, sh):
    y = x * lax.rsqrt(jnp.mean(x * x, axis=-1, keepdims=True) + EPS)
    return (y * g) * (1.0 + sc) + sh


def _mod_spec(d, tm, rows_per_mod, nd=2):
    if nd == 2:
        return pl.BlockSpec((1, 1, d), lambda i, j: ((i * tm) // rows_per_mod, 0, 0))
    return pl.BlockSpec((1, 1, d), lambda i: ((i * tm) // rows_per_mod, 0, 0))


def _nm_matmul_kernel(x_ref, g_ref, sc_ref, sh_ref, w_ref, o_ref, h_ref):
    @pl.when(pl.program_id(1) == 0)
    def _():
        h_ref[...] = _normmod(x_ref[...], g_ref[...], sc_ref[0], sh_ref[0]).astype(BF16)
    o_ref[...] = _dot(h_ref[...], w_ref[...])


def normmod_matmul(x, g, sc, sh, w_bf16, rows_per_mod, tm, tn):
    n, d = x.shape
    no = w_bf16.shape[1]
    return pl.pallas_call(
        _nm_matmul_kernel, grid=(n // tm, no // tn),
        in_specs=[pl.BlockSpec((tm, d), lambda i, j: (i, 0)), pl.BlockSpec((1, d), lambda i, j: (0, 0)),
                  _mod_spec(d, tm, rows_per_mod), _mod_spec(d, tm, rows_per_mod),
                  pl.BlockSpec((d, tn), lambda i, j: (0, j))],
        out_specs=pl.BlockSpec((tm, tn), lambda i, j: (i, j)),
        out_shape=jax.ShapeDtypeStruct((n, no), F32),
        scratch_shapes=[pltpu.VMEM((tm, d), BF16)],
        compiler_params=_cp("parallel", "arbitrary"), name="normmod_matmul",
    )(x, g.reshape(1, d), sc, sh, w_bf16)


def _nm_kernel(x_ref, g_ref, sc_ref, sh_ref, o_ref):
    o_ref[...] = _normmod(x_ref[...], g_ref[...], sc_ref[0], sh_ref[0]).astype(o_ref.dtype)


def normmod(x, g, sc, sh, rows_per_mod, tm):
    n, d = x.shape
    return pl.pallas_call(
        _nm_kernel, grid=(n // tm,),
        in_specs=[pl.BlockSpec((tm, d), lambda i: (i, 0)), pl.BlockSpec((1, d), lambda i: (0, 0)),
                  _mod_spec(d, tm, rows_per_mod, 1), _mod_spec(d, tm, rows_per_mod, 1)],
        out_specs=pl.BlockSpec((tm, d), lambda i: (i, 0)),
        out_shape=jax.ShapeDtypeStruct((n, d), BF16), compiler_params=_cp("parallel"), name="normmod",
    )(x, g.reshape(1, d), sc, sh)


def _final_norm_kernel(x_ref, g_ref, o_ref):
    x = x_ref[...]
    o_ref[...] = (x * lax.rsqrt(jnp.mean(x * x, axis=-1, keepdims=True) + EPS)) * g_ref[...]


def final_norm(x, g, tm):
    n, d = x.shape
    return pl.pallas_call(
        _final_norm_kernel, grid=(n // tm,),
        in_specs=[pl.BlockSpec((tm, d), lambda i: (i, 0)), pl.BlockSpec((1, d), lambda i: (0, 0))],
        out_specs=pl.BlockSpec((tm, d), lambda i: (i, 0)),
        out_shape=jax.ShapeDtypeStruct((n, d), F32), compiler_params=_cp("parallel"), name="final_norm",
    )(x, g.reshape(1, d))


def _add_rows_kernel(x_ref, p_ref, o_ref):
    o_ref[...] = x_ref[...] + p_ref[...]


def add_pos(x, pe, seq_len, tm):
    n, d = x.shape
    nb = seq_len // tm
    return pl.pallas_call(
        _add_rows_kernel, grid=(n // tm,),
        in_specs=[pl.BlockSpec((tm, d), lambda i: (i, 0)), pl.BlockSpec((tm, d), lambda i: (i % nb, 0))],
        out_specs=pl.BlockSpec((tm, d), lambda i: (i, 0)),
        out_shape=jax.ShapeDtypeStruct((n, d), F32), compiler_params=_cp("parallel"), name="add_pos",
    )(x, pe)


def _sconv_kernel(x_ref, w_ref, b_ref, s_ref, o_ref, *, act):
    x = x_ref[...]
    n_tok = x.shape[0]
    row = lax.broadcasted_iota(jnp.int32, x.shape, 0)
    prev = jnp.where(row == 0, 0.0, pltpu.roll(x, 1, 0))
    nxt = jnp.where(row == n_tok - 1, 0.0, pltpu.roll(x, n_tok - 1, 0))
    y = prev * w_ref[0:1, :] + x * w_ref[1:2, :] + nxt * w_ref[2:3, :] + b_ref[...]
    if act:
        y = _silu(y) * s_ref[...]
    o_ref[...] = y


def short_conv(a, ncols, w, b, scale, seq_len, act, cb=256):
    n = a.shape[0]
    return pl.pallas_call(
        functools.partial(_sconv_kernel, act=act), grid=(n // seq_len, ncols // cb),
        in_specs=[pl.BlockSpec((seq_len, cb), lambda s, j: (s, j)), pl.BlockSpec((3, cb), lambda s, j: (0, j)),
                  pl.BlockSpec((1, cb), lambda s, j: (0, j)), pl.BlockSpec((1, cb), lambda s, j: (0, j))],
        out_specs=pl.BlockSpec((seq_len, cb), lambda s, j: (s, j)),
        out_shape=jax.ShapeDtypeStruct((n, ncols), F32), compiler_params=_cp("parallel", "parallel"),
        name="short_conv",
    )(a, w, b.reshape(1, ncols), scale.reshape(1, ncols))


def dft_tables(n_tok):
    k = jnp.arange(n_tok, dtype=jnp.int32)
    blk = 1 << ((n_tok.bit_length() - 1) // 2)
    def thin(n):
        ang = ((k[:, None] * n[None, :]) % (2 * n_tok)).astype(F32) * (math.pi / n_tok)
        return jnp.cos(ang), jnp.sin(ang)
    (c_hi, s_hi), (c_lo, s_lo) = thin(jnp.arange(0, n_tok, blk, dtype=jnp.int32)), thin(jnp.arange(blk, dtype=jnp.int32))
    cos_t = (c_hi[:, :, None] * c_lo[:, None, :] - s_hi[:, :, None] * s_lo[:, None, :]).reshape(n_tok, n_tok)
    msin = -(s_hi[:, :, None] * c_lo[:, None, :] + c_hi[:, :, None] * s_lo[:, None, :]).reshape(n_tok, n_tok)
    alt = jnp.where(k % 2 == 0, 1.0, -1.0).astype(F32)
    a_t = msin.at[0, :].set(alt)
    a_tt = msin.at[:, 0].set(alt)
    return cos_t.astype(BF16), a_t.astype(BF16), a_tt.astype(BF16)


def _hyfilt_kernel(band_ref, w1_ref, b1_ref, w2_ref, b2_ref, w3_ref, fr_ref, dec_ref, h_ref, ss_ref, *,
                   n_tok, tl, hw, nbands):
    i = pl.program_id(0)
    pos = i * tl + lax.broadcasted_iota(jnp.int32, (tl, 1), 0)
    t = pos.astype(F32) / n_tok
    lane = lax.broadcasted_iota(jnp.int32, (tl, LANES), 1)
    ang = 2.0 * math.pi * t * band_ref[...]
    z = jnp.where(lane == 0, t, jnp.where(lane <= nbands, jnp.cos(ang),
                                          jnp.where(lane <= 2 * nbands, jnp.sin(ang), 0.0)))
    fr = fr_ref[...]
    h = jnp.sin(fr * (_dot(z, w1_ref[...], precision=HIGHEST) + b1_ref[...]))
    h = jnp.sin(fr * (_dot(h, w2_ref[...], precision=HIGHEST) + b2_ref[...]))
    h = _dot(h, w3_ref[...], precision=HIGHEST) * jnp.exp(-t * jnp.abs(dec_ref[...]))
    col = lax.broadcasted_iota(jnp.int32, h.shape, 1)
    is_bwd = (col // hw) % 2 == 1
    h = jnp.where(jnp.logical_and(is_bwd, pos == 0), 0.0, h)
    h_ref[...] = h.astype(BF16)

    @pl.when(i == 0)
    def _():
        ss_ref[...] = jnp.zeros_like(ss_ref)
    ss_ref[...] += jnp.sum(h * h, axis=0, keepdims=True)


def hyena_filter_taps(n_tok, w1, b1, w2, b2, w3, freq, decay, hw):
    emb, ffn = w1.shape
    nbands = (emb - 1) // 2
    tl = min(n_tok, 512)
    bands = jnp.linspace(1e-4, nbands - 1, nbands, dtype=F32)
    band_row = jnp.zeros((1, LANES), F32).at[0, 1:1 + nbands].set(bands).at[0, 1 + nbands:1 + 2 * nbands].set(bands)
    w1p = jnp.zeros((LANES, ffn), F32).at[:emb].set(w1)
    nc = w3.shape[1]
    full = lambda shp: pl.BlockSpec(shp, lambda i: (0, 0))
    return pl.pallas_call(
        functools.partial(_hyfilt_kernel, n_tok=n_tok, tl=tl, hw=hw, nbands=nbands), grid=(n_tok // tl,),
        in_specs=[full((1, LANES)), full((LANES, ffn)), full((1, ffn)), full((ffn, ffn)), full((1, ffn)),
                  full((ffn, nc)), full((1, ffn)), full((1, nc))],
        out_specs=[pl.BlockSpec((tl, nc), lambda i: (i, 0)), full((1, nc))],
        out_shape=[jax.ShapeDtypeStruct((n_tok, nc), BF16), jax.ShapeDtypeStruct((1, nc), F32)],
        compiler_params=_cp("arbitrary"), name="hyena_filter_taps",
    )(band_row, w1p, b1.reshape(1, ffn), w2, b2.reshape(1, ffn), w3, freq.reshape(1, ffn), decay.reshape(1, nc))


def _filt_dft_kernel(c_ref, a_ref, h_ref, ss_ref, kr_ref, ki_ref, *, tf, hw):
    i = pl.program_id(1)
    hf = h_ref[:, :hw]
    hb = h_ref[:, hw:]
    cc = c_ref[...]
    aa = a_ref[...]
    zrf, zif, zrb, zib = _dot(cc, hf), _dot(aa, hf), _dot(cc, hb), _dot(aa, hb)
    scale = lax.rsqrt(ss_ref[:, :hw] + ss_ref[:, hw:] + EPS)
    first = (i * tf + lax.broadcasted_iota(jnp.int32, (tf, 1), 0)) == 0
    scale = scale * jnp.where(first, 0.5, 1.0)
    kr_ref[0] = (zrf + zrb) * scale
    ki_ref[0] = jnp.where(first, zif + zib, zif - zib) * scale


def hyena_filter_spectrum(cos_t, a_t, taps, sumsq, hw, tf):
    n_tok = cos_t.shape[0]
    norder = taps.shape[1] // (2 * hw)
    out = jax.ShapeDtypeStruct((norder, n_tok, hw), F32)
    return pl.pallas_call(
        functools.partial(_filt_dft_kernel, tf=tf, hw=hw), grid=(norder, n_tok // tf),
        in_specs=[pl.BlockSpec((tf, n_tok), lambda o, i: (i, 0)), pl.BlockSpec((tf, n_tok), lambda o, i: (i, 0)),
                  pl.BlockSpec((n_tok, 2 * hw), lambda o, i: (0, o)), pl.BlockSpec((1, 2 * hw), lambda o, i: (0, o))],
        out_specs=[pl.BlockSpec((1, tf, hw), lambda o, i: (o, i, 0))] * 2,
        out_shape=[out, out], compiler_params=_cp("parallel", "parallel"), name="hyena_filter_spectrum",
    )(cos_t, a_t, taps, sumsq)


def _hy_fwd_kernel(c_ref, a_ref, z_ref, kr_ref, ki_ref, yr_ref, yi_ref, *, tf):
    i = pl.program_id(0)
    zb = z_ref[...].astype(BF16)
    zr = _dot(c_ref[...], zb)
    zi = _dot(a_ref[...], zb)
    kr = kr_ref[0]
    ki = ki_ref[0]
    first = (i * tf + lax.broadcasted_iota(jnp.int32, (tf, 1), 0)) == 0
    yr_ref[...] = jnp.where(first, zr * kr, zr * kr - zi * ki).astype(BF16)
    yi_ref[...] = jnp.where(first, zi * ki, zr * ki + zi * kr).astype(BF16)


def hyena_fwd(cos_t, a_t, z, zcol, kr, ki, order, nseq, hw, tf):
    n_tok = cos_t.shape[0]
    nf = n_tok // tf
    out = jax.ShapeDtypeStruct((nseq * n_tok, hw), BF16)
    return pl.pallas_call(
        functools.partial(_hy_fwd_kernel, tf=tf), grid=(nf, nseq),
        in_specs=[pl.BlockSpec((tf, n_tok), lambda i, b: (i, 0)), pl.BlockSpec((tf, n_tok), lambda i, b: (i, 0)),
                  pl.BlockSpec((n_tok, hw), lambda i, b: (b, zcol)),
                  pl.BlockSpec((1, tf, hw), lambda i, b: (order, i, 0)),
                  pl.BlockSpec((1, tf, hw), lambda i, b: (order, i, 0))],
        out_specs=[pl.BlockSpec((tf, hw), lambda i, b: (b * nf + i, 0))] * 2,
        out_shape=[out, out], compiler_params=_cp("parallel", "parallel"), name="hyena_fwd",
    )(cos_t, a_t, z, kr, ki)


def _hy_inv_kernel(c_ref, at_ref, yr_ref, yi_ref, zp_ref, gate_ref, bias_ref, o_ref, *, inv_len):
    conv = (_dot(c_ref[...], yr_ref[...]) + _dot(at_ref[...], yi_ref[...])) * inv_len
    o_ref[...] = gate_ref[...] * (conv + bias_ref[...] * zp_ref[...])


def hyena_inv(cos_t, a_tt, yr, yi, zprev, zcol, gates, gcol, bias_row, nseq, hw, tf):
    n_tok = cos_t.shape[0]
    nf = n_tok // tf
    return pl.pallas_call(
        functools.partial(_hy_inv_kernel, inv_len=1.0 / n_tok), grid=(nf, nseq),
        in_specs=[pl.BlockSpec((tf, n_tok), lambda i, b: (i, 0)), pl.BlockSpec((tf, n_tok), lambda i, b: (i, 0)),
                  pl.BlockSpec((n_tok, hw), lambda i, b: (b, 0)), pl.BlockSpec((n_tok, hw), lambda i, b: (b, 0)),
                  pl.BlockSpec((tf, hw), lambda i, b: (b * nf + i, zcol)),
                  pl.BlockSpec((tf, hw), lambda i, b: (b * nf + i, gcol)),
                  pl.BlockSpec((1, hw), lambda i, b: (0, 0))],
        out_specs=pl.BlockSpec((tf, hw), lambda i, b: (b * nf + i, 0)),
        out_shape=jax.ShapeDtypeStruct((nseq * n_tok, hw), F32),
        compiler_params=_cp("parallel", "parallel"), name="hyena_inv",
    )(cos_t, a_tt, yr, yi, zprev, gates, bias_row)


S5_DIAG_BLOCKS = 2


def _s5_kernel(u_ref, bre_ref, bim_ref, cre_ref, cim_ref, lam_ref, h0_ref, y_ref, hfin_ref, hre_s, him_s, st_s, *,
               tc, nc, ns):
    d = pl.program_id(0)
    c = pl.program_id(2)

    @pl.when(c == 0)
    def _():
        st_s[...] = h0_ref[0, 0]

    sw = u_ref.shape[1]
    halves = [(slice(j * sw // S5_DIAG_BLOCKS, (j + 1) * sw // S5_DIAG_BLOCKS),
               slice(j * ns // S5_DIAG_BLOCKS, (j + 1) * ns // S5_DIAG_BLOCKS)) for j in range(S5_DIAG_BLOCKS)]
    ub = u_ref[...].astype(BF16)
    for us, hs in halves:
        hre_s[:, hs] = _dot(ub[:, us], bre_ref[0, us, hs])
        him_s[:, hs] = _dot(ub[:, us], bim_ref[0, us, hs])
    lr = lam_ref[0, 0:1, :]
    li = lam_ref[0, 1:2, :]

    def body(t, carry):
        hr, hi = carry
        r = jnp.where(d == 0, t, tc - 1 - t)
        nr = lr * hr - li * hi + hre_s[pl.ds(r, 1), :]
        ni = lr * hi + li * hr + him_s[pl.ds(r, 1), :]
        hre_s[pl.ds(r, 1), :] = nr
        him_s[pl.ds(r, 1), :] = ni
        return nr, ni

    hr, hi = lax.fori_loop(0, tc, body, (st_s[0:1, :], st_s[1:2, :]), unroll=8)
    st_s[0:1, :] = hr
    st_s[1:2, :] = hi
    for us, hs in halves:
        y_ref[0, :, us] = (_dot(hre_s[:, hs].astype(BF16), cre_ref[0, hs, us])
                           + _dot(him_s[:, hs].astype(BF16), cim_ref[0, hs, us]))

    @pl.when(c == nc - 1)
    def _():
        hfin_ref[0, 0] = st_s[...]


def s5_scan(proj, ucol, bre, bim, cre, cim, lam, h0, nseq, seq_len, sw, tc):
    ns = bre.shape[2]
    nc = seq_len // tc

    def chunk(d, c):
        return c + d * (nc - 1 - 2 * c)

    return pl.pallas_call(
        functools.partial(_s5_kernel, tc=tc, nc=nc, ns=ns), grid=(2, nseq, nc),
        in_specs=[pl.BlockSpec((tc, sw), lambda d, b, c: (b * nc + chunk(d, c), ucol)),
                  pl.BlockSpec((1, sw, ns), lambda d, b, c: (d, 0, 0)),
                  pl.BlockSpec((1, sw, ns), lambda d, b, c: (d, 0, 0)),
                  pl.BlockSpec((1, ns, sw), lambda d, b, c: (d, 0, 0)),
                  pl.BlockSpec((1, ns, sw), lambda d, b, c: (d, 0, 0)),
                  pl.BlockSpec((1, 2, ns), lambda d, b, c: (d, 0, 0)),
                  pl.BlockSpec((1, 1, 2, ns), lambda d, b, c: (b, d, 0, 0))],
        out_specs=[pl.BlockSpec((1, tc, sw), lambda d, b, c: (d, b * nc + chunk(d, c), 0)),
                   pl.BlockSpec((1, 1, 2, ns), lambda d, b, c: (b, d, 0, 0))],
        out_shape=[jax.ShapeDtypeStruct((2, nseq * seq_len, sw), F32), jax.ShapeDtypeStruct((nseq, 2, 2, ns), F32)],
        scratch_shapes=[pltpu.VMEM((tc, ns), F32), pltpu.VMEM((tc, ns), F32), pltpu.VMEM((2, ns), F32)],
        compiler_params=_cp("parallel", "parallel", "arbitrary"), name="s5_scan",
    )(proj, bre, bim, cre, cim, lam, h0)


def _s5_glu_kernel(yf_ref, yb_ref, u_ref, d_ref, w_ref, b_ref, o_ref):
    y = jax.nn.gelu(yf_ref[0] + yb_ref[0] + d_ref[...] * u_ref[...])
    o_ref[...] = y * jax.nn.sigmoid(_dot(y.astype(BF16), w_ref[...]) + b_ref[...])


def s5_glu(y2, proj, ucol, d_skip, glu_w_bf16, glu_b, tm):
    _, n, sw = y2.shape
    return pl.pallas_call(
        _s5_glu_kernel, grid=(n // tm,),
        in_specs=[pl.BlockSpec((1, tm, sw), lambda i: (0, i, 0)), pl.BlockSpec((1, tm, sw), lambda i: (1, i, 0)),
                  pl.BlockSpec((tm, sw), lambda i: (i, ucol)), pl.BlockSpec((1, sw), lambda i: (0, 0)),
                  pl.BlockSpec((sw, sw), lambda i: (0, 0)), pl.BlockSpec((1, sw), lambda i: (0, 0))],
        out_specs=pl.BlockSpec((tm, sw), lambda i: (i, 0)),
        out_shape=jax.ShapeDtypeStruct((n, sw), F32), compiler_params=_cp("parallel"), name="s5_glu",
    )(y2, y2, proj, d_skip.reshape(1, sw), glu_w_bf16, glu_b.reshape(1, sw))


def _even_out_kernel(a_ref, b_ref, wa_ref, wb_ref, x_ref, gate_ref, o_ref):
    y = _dot(a_ref[...].astype(BF16), wa_ref[...]) + _dot(b_ref[...].astype(BF16), wb_ref[...])
    o_ref[...] = x_ref[...] + gate_ref[0] * y


def even_out(hy, s5o, w_bf16, x, gate, rows_per_mod, tm, tn):
    n, d = x.shape
    hw = hy.shape[1]
    sw = s5o.shape[1]
    return pl.pallas_call(
        _even_out_kernel, grid=(n // tm, d // tn),
        in_specs=[pl.BlockSpec((tm, hw), lambda i, j: (i, 0)), pl.BlockSpec((tm, sw), lambda i, j: (i, 0)),
                  pl.BlockSpec((hw, tn), lambda i, j: (0, j)), pl.BlockSpec((sw, tn), lambda i, j: (hw // sw, j)),
                  pl.BlockSpec((tm, tn), lambda i, j: (i, j)),
                  pl.BlockSpec((1, 1, tn), lambda i, j: ((i * tm) // rows_per_mod, 0, j))],
        out_specs=pl.BlockSpec((tm, tn), lambda i, j: (i, j)),
        out_shape=jax.ShapeDtypeStruct((n, d), F32), compiler_params=_cp("parallel", "parallel"), name="even_out",
    )(hy, s5o, w_bf16, w_bf16, x, gate)


def _log_sigmoid(x):
    return jnp.minimum(x, 0.0) - jnp.log1p(jnp.exp(-jnp.abs(x)))


def _gates_kernel(x_ref, g_ref, sc_ref, sh_ref, w_ref, b_ref, o_ref):
    h = _normmod(x_ref[...], g_ref[...], sc_ref[0], sh_ref[0]).astype(BF16)
    o_ref[0] = _dot(h, w_ref[0]) + b_ref[0]


def mlstm_gates(x, g, sc, sh, wg_bf16, bg, rows_per_mod, tm):
    n, d = x.shape
    return pl.pallas_call(
        _gates_kernel, grid=(n // tm, 2),
        in_specs=[pl.BlockSpec((tm, d), lambda i, j: (i, 0)), pl.BlockSpec((1, d), lambda i, j: (0, 0)),
                  _mod_spec(d, tm, rows_per_mod), _mod_spec(d, tm, rows_per_mod),
                  pl.BlockSpec((1, d, LANES), lambda i, j: (j, 0, 0)), pl.BlockSpec((1, 1, LANES), lambda i, j: (j, 0, 0))],
        out_specs=pl.BlockSpec((1, tm, LANES), lambda i, j: (j, i, 0)),
        out_shape=jax.ShapeDtypeStruct((2, n, LANES), F32), compiler_params=_cp("parallel", "parallel"),
        name="mlstm_gates",
    )(x, g.reshape(1, d), sc, sh, wg_bf16, bg)


def _mlstm_kernel(q_ref, k_ref, v_ref, g_ref, c0_ref, n0_ref, m0_ref, h_ref, cf_ref, nf_ref, mf_ref,
                  c_s, m_s, *, nh, dh, tc, nc):
    d = pl.program_id(0)
    c = pl.program_id(2)

    @pl.when(c == 0)
    def _():
        for h in range(nh):
            c_s[h, :, :dh] = c0_ref[0, 0, h]
            c_s[h, :, dh:] = jnp.broadcast_to(n0_ref[0, 0, h:h + 1, :], (dh, dh)).T
        m_s[...] = m0_ref[0, 0]

    ones = jnp.ones((tc, dh), F32)
    gates = g_ref[0]
    lane = lax.broadcasted_iota(jnp.int32, gates.shape, 1)
    logf = jnp.where(jnp.logical_and(lane >= nh, lane < 2 * nh), _log_sigmoid(gates), 0.0)
    r_i = lax.broadcasted_iota(jnp.int32, (tc, tc), 0)
    s_i = lax.broadcasted_iota(jnp.int32, (tc, tc), 1)
    causal = (r_i - s_i) * (1 - 2 * d) >= 0
    bcum = _dot(causal.astype(F32), logf, precision=HIGHEST)
    btot = jnp.sum(logf, axis=0, keepdims=True)
    gates_t = gates.T
    bcum_t = bcum.T
    for h in range(nh):
        hs = slice(h * dh, (h + 1) * dh)
        q = q_ref[:, hs]
        k = k_ref[:, hs]
        v = v_ref[:, hs]
        b_col = bcum[:, nh + h:nh + h + 1]
        b_row = bcum_t[nh + h:nh + h + 1, :]
        i_col = gates[:, h:h + 1]
        i_row = gates_t[h:h + 1, :]
        m = m_s[h:h + 1, :]
        a = b_col + m
        dmat = jnp.where(causal, b_col - b_row + i_row, NEG_INF)
        mq = jnp.maximum(a, jnp.max(dmat, axis=-1, keepdims=True))
        w_intra = jnp.exp(dmat - mq)
        w_inter = jnp.exp(a - mq)
        s = _dot_nt(q, k) * w_intra
        cn = c_s[h]
        v1 = jnp.concatenate([v, ones], axis=1)
        both = _dot(s, v1) + w_inter * _dot(q, cn)
        h_ref[0, :, hs] = both[:, :dh] / jnp.maximum(jnp.abs(both[:, dh:]), jnp.exp(-mq))
        b_last = btot[:, nh + h:nh + h + 1]
        g = b_last - b_col + i_col
        m_new = jnp.maximum(b_last + m, jnp.max(g, axis=0, keepdims=True))
        kw = k * jnp.exp(g - m_new)
        keep = jnp.exp(b_last + m - m_new)
        c_s[h] = keep * cn + _dot(kw.T, v1)
        m_s[h:h + 1, :] = m_new

    @pl.when(c == nc - 1)
    def _():
        for h in range(nh):
            cf_ref[0, 0, h] = c_s[h, :, :dh]
            nf_ref[0, 0, h:h + 1, :] = c_s[h, :, dh:].T[0:1, :]
        mf_ref[0, 0] = m_s[...]


def mlstm_scan(qk, proj, vcol, gates, c0, n0, m0, nseq, seq_len, nh, dh):
    tc = ML_CHUNK
    nc = seq_len // tc
    w = nh * dh

    def chunk(d, c):
        return c + d * (nc - 1 - 2 * c)

    rowblk = lambda d, b, c: b * nc + chunk(d, c)
    st = lambda shp: pl.BlockSpec((1, 1) + shp, lambda d, b, c: (b, d) + (0,) * len(shp))
    return pl.pallas_call(
        functools.partial(_mlstm_kernel, nh=nh, dh=dh, tc=tc, nc=nc), grid=(2, nseq, nc),
        in_specs=[pl.BlockSpec((tc, w), lambda d, b, c: (rowblk(d, b, c), 0)),
                  pl.BlockSpec((tc, w), lambda d, b, c: (rowblk(d, b, c), 1)),
                  pl.BlockSpec((tc, w), lambda d, b, c: (rowblk(d, b, c), vcol)),
                  pl.BlockSpec((1, tc, LANES), lambda d, b, c: (d, rowblk(d, b, c), 0)),
                  st((nh, dh, dh)), st((nh, dh)), st((nh, 1))],
        out_specs=[pl.BlockSpec((1, tc, w), lambda d, b, c: (d, rowblk(d, b, c), 0)),
                   st((nh, dh, dh)), st((nh, dh)), st((nh, 1))],
        out_shape=[jax.ShapeDtypeStruct((2, nseq * seq_len, w), F32),
                   jax.ShapeDtypeStruct((nseq, 2, nh, dh, dh), F32), jax.ShapeDtypeStruct((nseq, 2, nh, dh), F32),
                   jax.ShapeDtypeStruct((nseq, 2, nh, 1), F32)],
        scratch_shapes=[pltpu.VMEM((nh, dh, 2 * dh), F32), pltpu.VMEM((nh, 1), F32)],
        compiler_params=_cp("parallel", "parallel", "arbitrary"), name="mlstm_scan",
    )(qk, qk, proj, gates, c0, n0, m0)


def _odd_out_kernel(hf_ref, hb_ref, o_ref, ng_ref, w_ref, x_ref, gate_ref, out_ref, a_s, *, nh, dh):
    @pl.when(pl.program_id(1) == 0)
    def _():
        for h in range(nh):
            hs = slice(h * dh, (h + 1) * dh)
            blk = hf_ref[0, :, hs] + hb_ref[0, :, hs]
            blk = blk * lax.rsqrt(jnp.mean(blk * blk, axis=-1, keepdims=True) + EPS)
            a_s[:, hs] = ((blk * ng_ref[:, hs]) * _silu(o_ref[:, hs])).astype(BF16)
    out_ref[...] = x_ref[...] + gate_ref[0] * _dot(a_s[...], w_ref[...])


def odd_out(h2, proj, ocol, norm_g, w_bf16, x, gate, rows_per_mod, nh, dh, tm, tn):
    n, d = x.shape
    w = nh * dh
    return pl.pallas_call(
        functools.partial(_odd_out_kernel, nh=nh, dh=dh), grid=(n // tm, d // tn),
        in_specs=[pl.BlockSpec((1, tm, w), lambda i, j: (0, i, 0)), pl.BlockSpec((1, tm, w), lambda i, j: (1, i, 0)),
                  pl.BlockSpec((tm, w), lambda i, j: (i, ocol)), pl.BlockSpec((1, w), lambda i, j: (0, 0)),
                  pl.BlockSpec((w, tn), lambda i, j: (0, j)), pl.BlockSpec((tm, tn), lambda i, j: (i, j)),
                  pl.BlockSpec((1, 1, tn), lambda i, j: ((i * tm) // rows_per_mod, 0, j))],
        out_specs=pl.BlockSpec((tm, tn), lambda i, j: (i, j)),
        out_shape=jax.ShapeDtypeStruct((n, d), F32), scratch_shapes=[pltpu.VMEM((tm, w), BF16)],
        compiler_params=_cp("parallel", "arbitrary"), name="odd_out",
    )(h2, h2, proj, norm_g.reshape(1, w), w_bf16, x, gate)


def _top_values(curs, k, outs):
    curs = list(curs)
    for j in range(k):
        for a, out_s in enumerate(outs):
            m = jnp.max(curs[a], axis=0, keepdims=True)
            out_s[j:j + 1, :] = m
            curs[a] = jnp.where(curs[a] == m, NEG_INF, curs[a])


def _pair_candidates(k):
    return [(a, k // (a + 1)) for a in range(k)]


PEER_HEADS_PER_TRIP = 8


def _peer_score_kernel(h_ref, wq_ref, keys_ref, th_ref, s2_ref, w1_ref, w2_ref, q_s, v_s, cand_s, best_s, *,
                       nh, half, topk):
    q_s[...] = _dot_nt(wq_ref[...], h_ref[...])
    kk = topk + 1
    cand_s[...] = jnp.full(cand_s.shape, NEG_INF, F32)
    group = range(PEER_HEADS_PER_TRIP)

    def heads(trip, carry):
        hds = [trip * PEER_HEADS_PER_TRIP + u for u in group]
        scores = []
        for hd in hds:
            base = pl.multiple_of(hd * 2 * half, 2 * half)
            scores.append(_dot(keys_ref[hd, 0], q_s[pl.ds(base, half), :]))
            scores.append(_dot(keys_ref[hd, 1], q_s[pl.ds(base + half, half), :]))
        _top_values(scores, kk, [v_s.at[u, c] for u in group for c in range(2)])
        for u in group:
            off = 0
            for a, cnt in _pair_candidates(kk):
                cand_s[u, off:off + cnt, :] = v_s[u, 0, a:a + 1, :] + v_s[u, 1, 0:cnt, :]
                off += cnt
        _top_values([cand_s[u] for u in group], kk, [best_s.at[u] for u in group])
        for u, hd in enumerate(hds):
            s1, s2 = scores[2 * u], scores[2 * u + 1]
            best = best_s[u, 0:topk, :]
            z = jnp.sum(jnp.exp(best - best[0:1, :]), axis=0, keepdims=True)
            tmid = 0.5 * (best_s[u, topk - 1:topk, :] + best_s[u, topk:topk + 1, :])
            th = tmid - s1
            w1 = jnp.exp(s1 - v_s[u, 0, 0:1, :]) / z
            w2 = jnp.exp(s2 - v_s[u, 1, 0:1, :])
            for lt in range(s1.shape[1] // LANES):
                sl = slice(lt * LANES, (lt + 1) * LANES)
                th_ref[hd, lt] = th[:, sl]
                s2_ref[hd, lt] = s2[:, sl]
                w1_ref[hd, lt] = w1[:, sl]
                w2_ref[hd, lt] = w2[:, sl]
        return carry

    lax.fori_loop(0, nh // PEER_HEADS_PER_TRIP, heads, 0)


def peer_scores(h_bf16, wq_t_bf16, keys, tt):
    n, d = h_bf16.shape
    nh, _, nk, half = keys.shape
    kk = PK_TOPK + 1
    ncand = -(-sum(c for _, c in _pair_candidates(kk)) // 8) * 8
    big = jax.ShapeDtypeStruct((nh, n // LANES, nk, LANES), F32)
    bspec = pl.BlockSpec((nh, tt // LANES, nk, LANES), lambda i: (0, i, 0, 0))
    return pl.pallas_call(
        functools.partial(_peer_score_kernel, nh=nh, half=half, topk=PK_TOPK), grid=(n // tt,),
        in_specs=[pl.BlockSpec((tt, d), lambda i: (i, 0)), pl.BlockSpec((nh * 2 * half, d), lambda i: (0, 0)),
                  pl.BlockSpec((nh, 2, nk, half), lambda i: (0, 0, 0, 0))],
        out_specs=[bspec, bspec, bspec, bspec],
        out_shape=[big, big, big, big],
        scratch_shapes=[pltpu.VMEM((nh * 2 * half, tt), F32), pltpu.VMEM((PEER_HEADS_PER_TRIP, 2, 24, tt), F32),
                        pltpu.VMEM((PEER_HEADS_PER_TRIP, ncand, tt), F32), pltpu.VMEM((PEER_HEADS_PER_TRIP, 24, tt), F32)],
        compiler_params=_cp("parallel"), name="peer_scores",
    )(h_bf16, wq_t_bf16, keys)


PEER_KEY_ROWS = 16
GELU_C1 = math.sqrt(2.0 / math.pi)
GELU_C2 = 0.044715 * GELU_C1


def _gelu_tanh(x):
    half_x = 0.5 * x
    return half_x + half_x * jnp.tanh(x * (GELU_C1 + GELU_C2 * (x * x)))


def _peer_dense_kernel(h_ref, u_ref, vt_ref, th_ref, s2_ref, w1_ref, w2_ref, x_ref, gate_ref, o_ref,
                       acc_s, st_s, wt_s, *, nh, nk, ec, tt):
    e = pl.program_id(1)
    nlt = tt // LANES
    n_i1 = ec // nk
    nkt = nk // PEER_KEY_ROWS
    nsub = PEER_KEY_ROWS // SUBLANES

    @pl.when(e == 0)
    def _():
        acc_s[...] = jnp.zeros_like(acc_s)

    st = _gelu_tanh(_dot_nt(u_ref[...], h_ref[...]))
    for lt in range(nlt):
        st_s[lt] = st[:, lt * LANES:(lt + 1) * LANES]

    def tile(idx, carry):
        lt = idx // nkt
        k0 = (idx % nkt) * PEER_KEY_ROWS
        subs = [pl.ds(pl.multiple_of(k0 + j * SUBLANES, SUBLANES), SUBLANES) for j in range(nsub)]
        g = [[jnp.zeros((SUBLANES, LANES), F32) for _ in subs] for _ in range(n_i1)]
        for hd in range(nh):
            s2t = [s2_ref[hd, lt, sub, :] for sub in subs]
            w2t = [w2_ref[hd, lt, sub, :] for sub in subs]
            for li in range(n_i1):
                thb = jnp.broadcast_to(th_ref[hd, lt, li:li + 1, :], (SUBLANES, LANES))
                w1b = jnp.broadcast_to(w1_ref[hd, lt, li:li + 1, :], (SUBLANES, LANES))
                for j in range(nsub):
                    g[li][j] = g[li][j] + jnp.where(s2t[j] >= thb, w2t[j] * w1b, 0.0)
        for li in range(n_i1):
            rows = pl.ds(pl.multiple_of(li * nk + k0, PEER_KEY_ROWS), PEER_KEY_ROWS)
            wt_s[lt, rows, :] = (st_s[lt, rows, :] * jnp.concatenate(g[li], axis=0)).astype(BF16)
        return carry

    lax.fori_loop(0, nlt * nkt, tile, 0)
    wt = jnp.concatenate([wt_s[lt] for lt in range(nlt)], axis=1)
    acc_s[...] += _dot(vt_ref[0], wt)

    @pl.when(e == pl.num_programs(1) - 1)
    def _():
        o_ref[...] = x_ref[...] + gate_ref[0] * acc_s[...].T


def peer_dense(h_bf16, u_bf16, vt_bf16, th, s2, w1, w2, x, gate, rows_per_mod, tt, ec):
    n, d = x.shape
    nh, _, nk, _ = s2.shape
    nchunk = u_bf16.shape[0] // ec
    bspec = pl.BlockSpec((nh, tt // LANES, nk, LANES), lambda i, e: (0, i, 0, 0))
    rspec = pl.BlockSpec((nh, tt // LANES, ec // nk, LANES), lambda i, e: (0, i, e, 0))
    tile_buf = (tt // LANES, ec, LANES)
    return pl.pallas_call(
        functools.partial(_peer_dense_kernel, nh=nh, nk=nk, ec=ec, tt=tt), grid=(n // tt, nchunk),
        in_specs=[pl.BlockSpec((tt, d), lambda i, e: (i, 0)), pl.BlockSpec((ec, d), lambda i, e: (e, 0)),
                  pl.BlockSpec((1, d, ec), lambda i, e: (e, 0, 0)), rspec, bspec, rspec, bspec,
                  pl.BlockSpec((tt, d), lambda i, e: (i, 0)),
                  pl.BlockSpec((1, 1, d), lambda i, e: ((i * tt) // rows_per_mod, 0, 0))],
        out_specs=pl.BlockSpec((tt, d), lambda i, e: (i, 0)),
        out_shape=jax.ShapeDtypeStruct((n, d), F32),
        scratch_shapes=[pltpu.VMEM((d, tt), F32), pltpu.VMEM(tile_buf, F32), pltpu.VMEM(tile_buf, BF16)],
        compiler_params=_cp("parallel", "arbitrary"), name="peer_dense",
    )(h_bf16, u_bf16, vt_bf16, th, s2, w1, w2, x, gate)


def _s5_params(a_re, a_im, b_re, b_im, c_re, c_im, log_step):
    lam = lax.complex(a_re.astype(F32), a_im.astype(F32))
    lam_bar = jnp.exp(lam * jnp.exp(log_step.astype(F32))[..., None])
    b_bar = ((lam_bar - 1.0) / lam)[..., None] * lax.complex(b_re.astype(F32), b_im.astype(F32))
    ngrp, npst, nch = b_bar.shape[1:]
    eye = jnp.eye(ngrp, dtype=F32)

    def b_mat(part):
        return jnp.einsum("dgpj,gh->dgjhp", part, eye).reshape(2, ngrp * nch, ngrp * npst)

    def c_mat(part):
        return jnp.einsum("dgjp,gh->dgphj", part, eye).reshape(2, ngrp * npst, ngrp * nch)

    bre, bim = b_mat(b_bar.real).astype(BF16), b_mat(b_bar.imag).astype(BF16)
    cre, cim = c_mat(c_re.astype(F32)).astype(BF16), c_mat(-c_im.astype(F32)).astype(BF16)
    lam2 = jnp.stack([lam_bar.real.reshape(2, -1), lam_bar.imag.reshape(2, -1)], axis=1)
    return bre, bim, cre, cim, lam2


def _pos_embed(n_tok, d, grid_w):
    rows = n_tok // grid_w
    quarter = d // 4
    omega = 1.0 / (10000.0 ** (jnp.arange(quarter, dtype=F32) / quarter))

    def emb1d(pos):
        ang = pos.astype(F32)[:, None] * omega[None]
        return jnp.concatenate([jnp.sin(ang), jnp.cos(ang)], axis=-1)

    er = emb1d(jnp.arange(rows))
    ec = emb1d(jnp.arange(grid_w))
    half = d // 2
    pe = jnp.concatenate([jnp.broadcast_to(er[:, None], (rows, grid_w, half)),
                          jnp.broadcast_to(ec[None], (rows, grid_w, half))], axis=-1)
    return pe.reshape(rows * grid_w, d)


def _tile(n, pref):
    return pref if n % pref == 0 else n


def _trunk(x, mods, s5_h0, ml_c0, ml_n0, ml_m0, p, nseq, seq_len, rows_per_mod):
    n, d = x.shape
    tm = _tile(min(rows_per_mod, n), 512)
    depth = p["norm_g"].shape[0]
    s5_fin, ml_fin = [], []
    for l in range(depth):
        sh1, sc1, g1, sh2, sc2, g2 = mods[l]
        i = l // 2
        if l % 2 == 0:
            hw = p["hy_bias"].shape[2]
            sw = p["s5_d"].shape[1]
            proj = normmod_matmul(x, p["norm_g"][l, 0], sc1, sh1, p["ev_w_in"][i].astype(BF16), rows_per_mod, tm,
                                  _tile(3 * hw + sw, 512))
            hy_in = short_conv(proj, 3 * hw, p["hy_conv_w"][i], p["hy_conv_b"][i], jnp.ones((3 * hw,), F32),
                               seq_len, act=False)
            cos_t, a_t, a_tt = dft_tables(seq_len)
            tf = _tile(seq_len, 512)
            taps, sumsq = hyena_filter_taps(seq_len, p["hy_w1"][i], p["hy_b1"][i], p["hy_w2"][i], p["hy_b2"][i],
                                            p["hy_w3"][i], p["hy_freq"][i], p["hy_decay"][i], hw)
            kr, ki = hyena_filter_spectrum(cos_t, a_t, taps, sumsq, hw, tf)
            bias = p["hy_bias"][i].astype(F32)
            z, zcol = hy_in, 0
            for o in range(bias.shape[0]):
                yr, yi = hyena_fwd(cos_t, a_t, z, zcol, kr, ki, o, nseq, hw, tf)
                z = hyena_inv(cos_t, a_tt, yr, yi, z, zcol, hy_in, 1 + o, bias[o:o + 1], nseq, hw, tf)
                zcol = 0
            bre, bim, cre, cim, lam2 = _s5_params(p["s5_a_re"][i], p["s5_a_im"][i], p["s5_b_re"][i], p["s5_b_im"][i],
                                                  p["s5_c_re"][i], p["s5_c_im"][i], p["s5_log_step"][i])
            ucol = 3 * hw // sw
            y2, hfin = s5_scan(proj, ucol, bre, bim, cre, cim, lam2, s5_h0[i], nseq, seq_len, sw, _tile(seq_len, 256))
            s5_fin.append(hfin)
            s5o = s5_glu(y2, proj, ucol, p["s5_d"][i], p["s5_glu_w"][i].astype(BF16), p["s5_glu_b"][i], tm)
            x = even_out(z, s5o, p["ev_w_out"][i].astype(BF16), x, g1, rows_per_mod, tm, _tile(d, 512))
        else:
            nh = p["od_gate_b"].shape[2]
            w = p["ml_norm_g"].shape[1]
            dh = w // nh
            w_in = p["od_w_in"][i]
            proj = normmod_matmul(x, p["norm_g"][l, 0], sc1, sh1, w_in[:, :4 * w].astype(BF16), rows_per_mod, tm,
                                  _tile(4 * w, 512))
            wg = w_in[:, 4 * w:].reshape(d, 4, nh)
            gb = p["od_gate_b"][i].astype(F32)
            wg2 = jnp.zeros((2, d, LANES), F32)
            bg2 = jnp.zeros((2, 1, LANES), F32)
            for dr in range(2):
                wg2 = wg2.at[dr, :, :nh].set(wg[:, dr]).at[dr, :, nh:2 * nh].set(wg[:, 2 + dr])
                bg2 = bg2.at[dr, 0, :nh].set(gb[dr]).at[dr, 0, nh:2 * nh].set(gb[2 + dr])
            gates = mlstm_gates(x, p["norm_g"][l, 0], sc1, sh1, wg2.astype(BF16), bg2, rows_per_mod, tm)
            qscale = jnp.concatenate([jnp.full((w,), dh ** -0.5, F32), jnp.ones((w,), F32)])
            qk = short_conv(proj, 2 * w, p["ml_conv_w"][i], p["ml_conv_b"][i], qscale, seq_len, act=True)
            h2, cf, nf, mf = mlstm_scan(qk, proj, 2, gates, ml_c0[i], ml_n0[i], ml_m0[i], nseq, seq_len, nh, dh)
            ml_fin.append((cf, nf, mf))
            x = odd_out(h2, proj, 3, p["ml_norm_g"][i], p["od_w_out"][i].astype(BF16), x, g1, rows_per_mod, nh, dh, tm,
                        _tile(d, 512))
        hn = normmod(x, p["norm_g"][l, 1], sc2, sh2, rows_per_mod, tm)
        tt = _tile(min(rows_per_mod, n), 512)
        th, s2, w1, w2 = peer_scores(hn, p["pk_w_q"][l].T.astype(BF16), p["pk_keys"][l].astype(F32), _tile(tt, 256))
        ec = 2048
        vt = p["pk_v"][l].astype(BF16).reshape(-1, ec, d).transpose(0, 2, 1)
        x = peer_dense(hn, p["pk_u"][l].astype(BF16), vt, th, s2, w1, w2, x, g2, rows_per_mod, tt, ec)
    y = final_norm(x, p["final_g"], tm)
    return y, s5_fin, ml_fin


def kernel(x_prompt, x_sample, state_s5_re, state_s5_im, state_mlstm_C, state_mlstm_n, state_mlstm_m, c, c_ctx, norm_g, ada_w, ada_b, final_g, ev_w_in, hy_conv_w, hy_conv_b, hy_w1, hy_b1, hy_w2, hy_b2, hy_w3, hy_freq, hy_decay, hy_bias, s5_a_re, s5_a_im, s5_b_re, s5_b_im, s5_c_re, s5_c_im, s5_log_step, s5_d, s5_glu_w, s5_glu_b, ev_w_out, od_w_in, od_gate_b, ml_conv_w, ml_conv_b, ml_norm_g, od_w_out, pk_w_q, pk_keys, pk_u, pk_v):
    p = dict(norm_g=norm_g, ada_w=ada_w, ada_b=ada_b, final_g=final_g, ev_w_in=ev_w_in,
             hy_conv_w=hy_conv_w, hy_conv_b=hy_conv_b, hy_w1=hy_w1, hy_b1=hy_b1, hy_w2=hy_w2, hy_b2=hy_b2,
             hy_w3=hy_w3, hy_freq=hy_freq, hy_decay=hy_decay, hy_bias=hy_bias, s5_a_re=s5_a_re,
             s5_a_im=s5_a_im, s5_b_re=s5_b_re, s5_b_im=s5_b_im, s5_c_re=s5_c_re, s5_c_im=s5_c_im,
             s5_log_step=s5_log_step, s5_d=s5_d, s5_glu_w=s5_glu_w, s5_glu_b=s5_glu_b, ev_w_out=ev_w_out,
             od_w_in=od_w_in, od_gate_b=od_gate_b, ml_conv_w=ml_conv_w, ml_conv_b=ml_conv_b,
             ml_norm_g=ml_norm_g, od_w_out=od_w_out, pk_w_q=pk_w_q, pk_keys=pk_keys, pk_u=pk_u, pk_v=pk_v)
    nb, seq, d = x_prompt.shape
    db, dseq, _ = x_sample.shape
    depth = norm_g.shape[0]
    n_even, n_odd = (depth + 1) // 2, depth // 2
    assert db + 1 <= 8

    cond8 = jnp.zeros((8, d), F32).at[0].set(c_ctx.astype(F32)).at[1:1 + db].set(c.astype(F32))
    mods_ctx, mods_lat = [], []
    for l in range(depth):
        mod = ada_mod(cond8, ada_w[l].astype(F32), ada_b[l].astype(F32))
        chunks = [mod[:, j * d:(j + 1) * d] for j in range(6)]
        mods_ctx.append([ch[0:1].reshape(1, 1, d) for ch in chunks])
        mods_lat.append([ch[1:1 + db].reshape(db, 1, d) for ch in chunks])

    def s5_state(re, im, bsz):
        return [jnp.stack([re[:, i].reshape(bsz, 2, -1), im[:, i].reshape(bsz, 2, -1)], axis=2).astype(F32)
                for i in range(n_even)]

    ngrp, npst = s5_a_re.shape[2], s5_a_re.shape[3]
    nh, dh = state_mlstm_C.shape[3], state_mlstm_C.shape[4]
    zeros_s5 = jnp.zeros((nb, n_even, 2, ngrp, npst), F32)
    y_prompt, s5_fin, ml_fin = _trunk(
        x_prompt.reshape(nb * seq, d), mods_ctx, s5_state(zeros_s5, zeros_s5, nb),
        [jnp.zeros((nb, 2, nh, dh, dh), F32)] * n_odd, [jnp.zeros((nb, 2, nh, dh), F32)] * n_odd,
        [jnp.zeros((nb, 2, nh, 1), F32)] * n_odd, p, nb, seq, nb * seq)
    x_lat = add_pos(x_sample.reshape(db * dseq, d), _pos_embed(dseq, d, GRID_W), dseq, _tile(dseq, 512))
    y_sample, _, _ = _trunk(
        x_lat, mods_lat, s5_state(state_s5_re, state_s5_im, db),
        [state_mlstm_C[:, i].astype(F32) for i in range(n_odd)], [state_mlstm_n[:, i].astype(F32) for i in range(n_odd)],
        [state_mlstm_m[:, i].astype(F32)[..., None] for i in range(n_odd)], p, db, dseq, dseq)

    new_s5_re = jnp.stack([h[:, :, 0].reshape(nb, 2, ngrp, npst) for h in s5_fin], axis=1)
    new_s5_im = jnp.stack([h[:, :, 1].reshape(nb, 2, ngrp, npst) for h in s5_fin], axis=1)
    new_c = jnp.stack([f[0] for f in ml_fin], axis=1)
    new_n = jnp.stack([f[1] for f in ml_fin], axis=1)
    new_m = jnp.stack([f[2][..., 0] for f in ml_fin], axis=1)
    return (y_prompt.reshape(nb, seq, d), y_sample.reshape(db, dseq, d), new_s5_re, new_s5_im, new_c, new_n, new_m)
```

```python
import functools
import math

import jax
import jax.numpy as jnp
from jax import lax
from jax.experimental import pallas as pl
from jax.experimental.pallas import tpu as pltpu

F32 = jnp.float32
BF16 = jnp.bfloat16
EPS = 1e-6
HIGHEST = lax.Precision.HIGHEST
V7X_VMEM_LIMIT_BYTES = 56 * 1024 * 1024
LANES = 128
SUBLANES = 8
ML_CHUNK = 128
PK_TOPK = 16
GRID_W = 64
NEG_INF = float("-inf")


def _cp(*sem):
    return pltpu.CompilerParams(dimension_semantics=sem, vmem_limit_bytes=V7X_VMEM_LIMIT_BYTES)


def _dot(a, b, **kw):
    return jnp.dot(a, b, preferred_element_type=F32, **kw)


def _dot_nt(a, b):
    return lax.dot_general(a, b, (((1,), (1,)), ((), ())), preferred_element_type=F32)


def _silu(x):
    return x * jax.nn.sigmoid(x)


def _ada_kernel(c_ref, w_ref, b_ref, o_ref):
    o_ref[...] = _dot(_silu(c_ref[...]), w_ref[...], precision=HIGHEST) + b_ref[...]


def ada_mod(cond8, w, b):
    d, no = w.shape
    tn = 1536 if no % 1536 == 0 else no
    return pl.pallas_call(
        _ada_kernel, grid=(no // tn,),
        in_specs=[pl.BlockSpec((8, d), lambda j: (0, 0)), pl.BlockSpec((d, tn), lambda j: (0, j)),
                  pl.BlockSpec((1, tn), lambda j: (0, j))],
        out_specs=pl.BlockSpec((8, tn), lambda j: (0, j)),
        out_shape=jax.ShapeDtypeStruct((8, no), F32), compiler_params=_cp("parallel"), name="ada_mod",
    )(cond8, w, b.reshape(1, no))


def _normmod(x, g, sc, sh):
    y = x * lax.rsqrt(jnp.mean(x * x, axis=-1, keepdims=True) + EPS)
    return (y * g) * (1.0 + sc) + sh


def _mod_spec(d, tm, rows_per_mod):
    return pl.BlockSpec((1, 1, d), lambda i, j: ((i * tm) // rows_per_mod, 0, 0))


def _nm_matmul_kernel(x_ref, g_ref, sc_ref, sh_ref, w_ref, o_ref, h_ref):
    @pl.when(pl.program_id(1) == 0)
    def _():
        h_ref[...] = _normmod(x_ref[...], g_ref[...], sc_ref[0], sh_ref[0]).astype(BF16)
    o_ref[...] = _dot(h_ref[...], w_ref[...])


def normmod_matmul(x, g, sc, sh, w_bf16, rows_per_mod, tm, tn):
    n, d = x.shape
    no = w_bf16.shape[1]
    return pl.pallas_call(
        _nm_matmul_kernel, grid=(n // tm, no // tn),
        in_specs=[pl.BlockSpec((tm, d), lambda i, j: (i, 0)), pl.BlockSpec((1, d), lambda i, j: (0, 0)),
                  _mod_spec(d, tm, rows_per_mod), _mod_spec(d, tm, rows_per_mod),
                  pl.BlockSpec((d, tn), lambda i, j: (0, j))],
        out_specs=pl.BlockSpec((tm, tn), lambda i, j: (i, j)),
        out_shape=jax.ShapeDtypeStruct((n, no), F32),
        scratch_shapes=[pltpu.VMEM((tm, d), BF16)],
        compiler_params=_cp("parallel", "arbitrary"), name="normmod_matmul",
    )(x, g.reshape(1, d), sc, sh, w_bf16)


def _add_rows_kernel(x_ref, p_ref, o_ref):
    o_ref[...] = x_ref[...] + p_ref[...]


def add_pos(x, pe, seq_len, tm):
    n, d = x.shape
    nb = seq_len // tm
    return pl.pallas_call(
        _add_rows_kernel, grid=(n // tm,),
        in_specs=[pl.BlockSpec((tm, d), lambda i: (i, 0)), pl.BlockSpec((tm, d), lambda i: (i % nb, 0))],
        out_specs=pl.BlockSpec((tm, d), lambda i: (i, 0)),
        out_shape=jax.ShapeDtypeStruct((n, d), F32), compiler_params=_cp("parallel"), name="add_pos",
    )(x, pe)


def _sconv_kernel(x_ref, w_ref, b_ref, s_ref, o_ref, *, act):
    x = x_ref[...]
    n_tok = x.shape[0]
    row = lax.broadcasted_iota(jnp.int32, x.shape, 0)
    prev = jnp.where(row == 0, 0.0, pltpu.roll(x, 1, 0))
    nxt = jnp.where(row == n_tok - 1, 0.0, pltpu.roll(x, n_tok - 1, 0))
    y = prev * w_ref[0:1, :] + x * w_ref[1:2, :] + nxt * w_ref[2:3, :] + b_ref[...]
    if act:
        y = _silu(y) * s_ref[...]
    o_ref[...] = y


def short_conv(a, ncols, w, b, scale, seq_len, act, cb=256):
    n = a.shape[0]
    return pl.pallas_call(
        functools.partial(_sconv_kernel, act=act), grid=(n // seq_len, ncols // cb),
        in_specs=[pl.BlockSpec((seq_len, cb), lambda s, j: (s, j)), pl.BlockSpec((3, cb), lambda s, j: (0, j)),
                  pl.BlockSpec((1, cb), lambda s, j: (0, j)), pl.BlockSpec((1, cb), lambda s, j: (0, j))],
        out_specs=pl.BlockSpec((seq_len, cb), lambda s, j: (s, j)),
        out_shape=jax.ShapeDtypeStruct((n, ncols), F32), compiler_params=_cp("parallel", "parallel"),
        name="short_conv",
    )(a, w, b.reshape(1, ncols), scale.reshape(1, ncols))


def dft_tables(n_tok):
    k = jnp.arange(n_tok, dtype=jnp.int32)
    blk = 1 << ((n_tok.bit_length() - 1) // 2)
    def thin(n):
        ang = ((k[:, None] * n[None, :]) % (2 * n_tok)).astype(F32) * (math.pi / n_tok)
        return jnp.cos(ang), jnp.sin(ang)
    (c_hi, s_hi), (c_lo, s_lo) = thin(jnp.arange(0, n_tok, blk, dtype=jnp.int32)), thin(jnp.arange(blk, dtype=jnp.int32))
    cos_t = (c_hi[:, :, None] * c_lo[:, None, :] - s_hi[:, :, None] * s_lo[:, None, :]).reshape(n_tok, n_tok)
    msin = -(s_hi[:, :, None] * c_lo[:, None, :] + c_hi[:, :, None] * s_lo[:, None, :]).reshape(n_tok, n_tok)
    alt = jnp.where(k % 2 == 0, 1.0, -1.0).astype(F32)
    a_t = msin.at[0, :].set(alt)
    a_tt = msin.at[:, 0].set(alt)
    return cos_t.astype(BF16), a_t.astype(BF16), a_tt.astype(BF16)


def _hyfilt_kernel(band_ref, w1_ref, b1_ref, w2_ref, b2_ref, w3_ref, fr_ref, dec_ref, h_ref, ss_ref, *,
                   n_tok, tl, hw, nbands):
    i = pl.program_id(0)
    pos = i * tl + lax.broadcasted_iota(jnp.int32, (tl, 1), 0)
    t = pos.astype(F32) / n_tok
    lane = lax.broadcasted_iota(jnp.int32, (tl, LANES), 1)
    ang = 2.0 * math.pi * t * band_ref[...]
    z = jnp.where(lane == 0, t, jnp.where(lane <= nbands, jnp.cos(ang),
                                          jnp.where(lane <= 2 * nbands, jnp.sin(ang), 0.0)))
    fr = fr_ref[...]
    h = jnp.sin(fr * (_dot(z, w1_ref[...], precision=HIGHEST) + b1_ref[...]))
    h = jnp.sin(fr * (_dot(h, w2_ref[...], precision=HIGHEST) + b2_ref[...]))
    h = _dot(h, w3_ref[...], precision=HIGHEST) * jnp.exp(-t * jnp.abs(dec_ref[...]))
    col = lax.broadcasted_iota(jnp.int32, h.shape, 1)
    is_bwd = (col // hw) % 2 == 1
    h = jnp.where(jnp.logical_and(is_bwd, pos == 0), 0.0, h)
    h_ref[...] = h.astype(BF16)

    @pl.when(i == 0)
    def _():
        ss_ref[...] = jnp.zeros_like(ss_ref)
    ss_ref[...] += jnp.sum(h * h, axis=0, keepdims=True)


def hyena_filter_taps(n_tok, w1, b1, w2, b2, w3, freq, decay, hw):
    emb, ffn = w1.shape
    nbands = (emb - 1) // 2
    tl = min(n_tok, 512)
    bands = jnp.linspace(1e-4, nbands - 1, nbands, dtype=F32)
    band_row = jnp.zeros((1, LANES), F32).at[0, 1:1 + nbands].set(bands).at[0, 1 + nbands:1 + 2 * nbands].set(bands)
    w1p = jnp.zeros((LANES, ffn), F32).at[:emb].set(w1)
    nc = w3.shape[1]
    full = lambda shp: pl.BlockSpec(shp, lambda i: (0, 0))
    return pl.pallas_call(
        functools.partial(_hyfilt_kernel, n_tok=n_tok, tl=tl, hw=hw, nbands=nbands), grid=(n_tok // tl,),
        in_specs=[full((1, LANES)), full((LANES, ffn)), full((1, ffn)), full((ffn, ffn)), full((1, ffn)),
                  full((ffn, nc)), full((1, ffn)), full((1, nc))],
        out_specs=[pl.BlockSpec((tl, nc), lambda i: (i, 0)), full((1, nc))],
        out_shape=[jax.ShapeDtypeStruct((n_tok, nc), BF16), jax.ShapeDtypeStruct((1, nc), F32)],
        compiler_params=_cp("arbitrary"), name="hyena_filter_taps",
    )(band_row, w1p, b1.reshape(1, ffn), w2, b2.reshape(1, ffn), w3, freq.reshape(1, ffn), decay.reshape(1, nc))


def _filt_dft_kernel(c_ref, a_ref, h_ref, ss_ref, kr_ref, ki_ref, *, tf, hw):
    i = pl.program_id(1)
    hf = h_ref[:, :hw]
    hb = h_ref[:, hw:]
    cc = c_ref[...]
    aa = a_ref[...]
    zrf, zif, zrb, zib = _dot(cc, hf), _dot(aa, hf), _dot(cc, hb), _dot(aa, hb)
    scale = lax.rsqrt(ss_ref[:, :hw] + ss_ref[:, hw:] + EPS)
    first = (i * tf + lax.broadcasted_iota(jnp.int32, (tf, 1), 0)) == 0
    scale = scale * jnp.where(first, 0.5, 1.0)
    kr_ref[0] = (zrf + zrb) * scale
    ki_ref[0] = jnp.where(first, zif + zib, zif - zib) * scale


def hyena_filter_spectrum(cos_t, a_t, taps, sumsq, hw, tf):
    n_tok = cos_t.shape[0]
    norder = taps.shape[1] // (2 * hw)
    out = jax.ShapeDtypeStruct((norder, n_tok, hw), F32)
    return pl.pallas_call(
        functools.partial(_filt_dft_kernel, tf=tf, hw=hw), grid=(norder, n_tok // tf),
        in_specs=[pl.BlockSpec((tf, n_tok), lambda o, i: (i, 0)), pl.BlockSpec((tf, n_tok), lambda o, i: (i, 0)),
                  pl.BlockSpec((n_tok, 2 * hw), lambda o, i: (0, o)), pl.BlockSpec((1, 2 * hw), lambda o, i: (0, o))],
        out_specs=[pl.BlockSpec((1, tf, hw), lambda o, i: (o, i, 0))] * 2,
        out_shape=[out, out], compiler_params=_cp("parallel", "parallel"), name="hyena_filter_spectrum",
    )(cos_t, a_t, taps, sumsq)


def _hy_fwd_kernel(c_ref, a_ref, z_ref, kr_ref, ki_ref, yr_ref, yi_ref, *, tf):
    i = pl.program_id(0)
    zb = z_ref[...].astype(BF16)
    zr = _dot(c_ref[...], zb)
    zi = _dot(a_ref[...], zb)
    kr = kr_ref[0]
    ki = ki_ref[0]
    first = (i * tf + lax.broadcasted_iota(jnp.int32, (tf, 1), 0)) == 0
    yr_ref[...] = jnp.where(first, zr * kr, zr * kr - zi * ki).astype(BF16)
    yi_ref[...] = jnp.where(first, zi * ki, zr * ki + zi * kr).astype(BF16)


def hyena_fwd(cos_t, a_t, z, zcol, kr, ki, order, nseq, hw, tf):
    n_tok = cos_t.shape[0]
    nf = n_tok // tf
    out = jax.ShapeDtypeStruct((nseq * n_tok, hw), BF16)
    return pl.pallas_call(
        functools.partial(_hy_fwd_kernel, tf=tf), grid=(nf, nseq),
        in_specs=[pl.BlockSpec((tf, n_tok), lambda i, b: (i, 0)), pl.BlockSpec((tf, n_tok), lambda i, b: (i, 0)),
                  pl.BlockSpec((n_tok, hw), lambda i, b: (b, zcol)),
                  pl.BlockSpec((1, tf, hw), lambda i, b: (order, i, 0)),
                  pl.BlockSpec((1, tf, hw), lambda i, b: (order, i, 0))],
        out_specs=[pl.BlockSpec((tf, hw), lambda i, b: (b * nf + i, 0))] * 2,
        out_shape=[out, out], compiler_params=_cp("parallel", "parallel"), name="hyena_fwd",
    )(cos_t, a_t, z, kr, ki)


def _hy_inv_kernel(c_ref, at_ref, yr_ref, yi_ref, zp_ref, gate_ref, bias_ref, o_ref, *, inv_len):
    conv = (_dot(c_ref[...], yr_ref[...]) + _dot(at_ref[...], yi_ref[...])) * inv_len
    o_ref[...] = gate_ref[...] * (conv + bias_ref[...] * zp_ref[...])


def hyena_inv(cos_t, a_tt, yr, yi, zprev, zcol, gates, gcol, bias_row, nseq, hw, tf):
    n_tok = cos_t.shape[0]
    nf = n_tok // tf
    return pl.pallas_call(
        functools.partial(_hy_inv_kernel, inv_len=1.0 / n_tok), grid=(nf, nseq),
        in_specs=[pl.BlockSpec((tf, n_tok), lambda i, b: (i, 0)), pl.BlockSpec((tf, n_tok), lambda i, b: (i, 0)),
                  pl.BlockSpec((n_tok, hw), lambda i, b: (b, 0)), pl.BlockSpec((n_tok, hw), lambda i, b: (b, 0)),
                  pl.BlockSpec((tf, hw), lambda i, b: (b * nf + i, zcol)),
                  pl.BlockSpec((tf, hw), lambda i, b: (b * nf + i, gcol)),
                  pl.BlockSpec((1, hw), lambda i, b: (0, 0))],
        out_specs=pl.BlockSpec((tf, hw), lambda i, b: (b * nf + i, 0)),
        out_shape=jax.ShapeDtypeStruct((nseq * n_tok, hw), F32),
        compiler_params=_cp("parallel", "parallel"), name="hyena_inv",
    )(cos_t, a_tt, yr, yi, zprev, gates, bias_row)


S5_DIAG_BLOCKS = 2


def _s5_kernel(u_ref, bre_ref, bim_ref, cre_ref, cim_ref, lam_ref, h0_ref, y_ref, hfin_ref, hre_s, him_s, st_s, *,
               tc, nc, ns):
    d = pl.program_id(0)
    c = pl.program_id(2)

    @pl.when(c == 0)
    def _():
        st_s[...] = h0_ref[0, 0]

    sw = u_ref.shape[1]
    halves = [(slice(j * sw // S5_DIAG_BLOCKS, (j + 1) * sw // S5_DIAG_BLOCKS),
               slice(j * ns // S5_DIAG_BLOCKS, (j + 1) * ns // S5_DIAG_BLOCKS)) for j in range(S5_DIAG_BLOCKS)]
    ub = u_ref[...].astype(BF16)
    for us, hs in halves:
        hre_s[:, hs] = _dot(ub[:, us], bre_ref[0, us, hs])
        him_s[:, hs] = _dot(ub[:, us], bim_ref[0, us, hs])
    lr = lam_ref[0, 0:1, :]
    li = lam_ref[0, 1:2, :]

    def body(t, carry):
        hr, hi = carry
        r = jnp.where(d == 0, t, tc - 1 - t)
        nr = lr * hr - li * hi + hre_s[pl.ds(r, 1), :]
        ni = lr * hi + li * hr + him_s[pl.ds(r, 1), :]
        hre_s[pl.ds(r, 1), :] = nr
        him_s[pl.ds(r, 1), :] = ni
        return nr, ni

    hr, hi = lax.fori_loop(0, tc, body, (st_s[0:1, :], st_s[1:2, :]), unroll=8)
    st_s[0:1, :] = hr
    st_s[1:2, :] = hi
    for us, hs in halves:
        y_ref[0, :, us] = (_dot(hre_s[:, hs].astype(BF16), cre_ref[0, hs, us])
                           + _dot(him_s[:, hs].astype(BF16), cim_ref[0, hs, us]))

    @pl.when(c == nc - 1)
    def _():
        hfin_ref[0, 0] = st_s[...]


def s5_scan(proj, ucol, bre, bim, cre, cim, lam, h0, nseq, seq_len, sw, tc):
    ns = bre.shape[2]
    nc = seq_len // tc

    def chunk(d, c):
        return c + d * (nc - 1 - 2 * c)

    return pl.pallas_call(
        functools.partial(_s5_kernel, tc=tc, nc=nc, ns=ns), grid=(2, nseq, nc),
        in_specs=[pl.BlockSpec((tc, sw), lambda d, b, c: (b * nc + chunk(d, c), ucol)),
                  pl.BlockSpec((1, sw, ns), lambda d, b, c: (d, 0, 0)),
                  pl.BlockSpec((1, sw, ns), lambda d, b, c: (d, 0, 0)),
                  pl.BlockSpec((1, ns, sw), lambda d, b, c: (d, 0, 0)),
                  pl.BlockSpec((1, ns, sw), lambda d, b, c: (d, 0, 0)),
                  pl.BlockSpec((1, 2, ns), lambda d, b, c: (d, 0, 0)),
                  pl.BlockSpec((1, 1, 2, ns), lambda d, b, c: (b, d, 0, 0))],
        out_specs=[pl.BlockSpec((1, tc, sw), lambda d, b, c: (d, b * nc + chunk(d, c), 0)),
                   pl.BlockSpec((1, 1, 2, ns), lambda d, b, c: (b, d, 0, 0))],
        out_shape=[jax.ShapeDtypeStruct((2, nseq * seq_len, sw), F32), jax.ShapeDtypeStruct((nseq, 2, 2, ns), F32)],
        scratch_shapes=[pltpu.VMEM((tc, ns), F32), pltpu.VMEM((tc, ns), F32), pltpu.VMEM((2, ns), F32)],
        compiler_params=_cp("parallel", "parallel", "arbitrary"), name="s5_scan",
    )(proj, bre, bim, cre, cim, lam, h0)


def _s5_glu_kernel(yf_ref, yb_ref, u_ref, d_ref, w_ref, b_ref, o_ref):
    y = jax.nn.gelu(yf_ref[0] + yb_ref[0] + d_ref[...] * u_ref[...])
    o_ref[...] = y * jax.nn.sigmoid(_dot(y.astype(BF16), w_ref[...]) + b_ref[...])


def s5_glu(y2, proj, ucol, d_skip, glu_w_bf16, glu_b, tm):
    _, n, sw = y2.shape
    return pl.pallas_call(
        _s5_glu_kernel, grid=(n // tm,),
        in_specs=[pl.BlockSpec((1, tm, sw), lambda i: (0, i, 0)), pl.BlockSpec((1, tm, sw), lambda i: (1, i, 0)),
                  pl.BlockSpec((tm, sw), lambda i: (i, ucol)), pl.BlockSpec((1, sw), lambda i: (0, 0)),
                  pl.BlockSpec((sw, sw), lambda i: (0, 0)), pl.BlockSpec((1, sw), lambda i: (0, 0))],
        out_specs=pl.BlockSpec((tm, sw), lambda i: (i, 0)),
        out_shape=jax.ShapeDtypeStruct((n, sw), F32), compiler_params=_cp("parallel"), name="s5_glu",
    )(y2, y2, proj, d_skip.reshape(1, sw), glu_w_bf16, glu_b.reshape(1, sw))


def _residual_and_next_norm(x_ref, gate_ref, y, g2_ref, sc2_ref, sh2_ref, o_ref, hn_ref):
    xn = x_ref[...] + gate_ref[0] * y
    o_ref[...] = xn
    hn_ref[...] = _normmod(xn, g2_ref[...], sc2_ref[0], sh2_ref[0]).astype(BF16)


def _row_specs(d, tm, rows_per_mod):
    mod = pl.BlockSpec((1, 1, d), lambda i: ((i * tm) // rows_per_mod, 0, 0))
    return [pl.BlockSpec((tm, d), lambda i: (i, 0)), mod, pl.BlockSpec((1, d), lambda i: (0, 0)), mod, mod]


def _row_outs(n, d, tm):
    spec = pl.BlockSpec((tm, d), lambda i: (i, 0))
    return [spec, spec], [jax.ShapeDtypeStruct((n, d), F32), jax.ShapeDtypeStruct((n, d), BF16)]


def _even_out_kernel(a_ref, b_ref, wa_ref, wb_ref, x_ref, gate_ref, g2_ref, sc2_ref, sh2_ref, o_ref, hn_ref):
    y = _dot(a_ref[...].astype(BF16), wa_ref[...]) + _dot(b_ref[...].astype(BF16), wb_ref[...])
    _residual_and_next_norm(x_ref, gate_ref, y, g2_ref, sc2_ref, sh2_ref, o_ref, hn_ref)


def even_out(hy, s5o, w_bf16, x, gate, g2, sc2, sh2, rows_per_mod, tm):
    n, d = x.shape
    hw = hy.shape[1]
    sw = s5o.shape[1]
    out_specs, out_shape = _row_outs(n, d, tm)
    return pl.pallas_call(
        _even_out_kernel, grid=(n // tm,),
        in_specs=[pl.BlockSpec((tm, hw), lambda i: (i, 0)), pl.BlockSpec((tm, sw), lambda i: (i, 0)),
                  pl.BlockSpec((hw, d), lambda i: (0, 0)), pl.BlockSpec((sw, d), lambda i: (hw // sw, 0))]
        + _row_specs(d, tm, rows_per_mod),
        out_specs=out_specs, out_shape=out_shape, compiler_params=_cp("parallel"), name="even_out",
    )(hy, s5o, w_bf16, w_bf16, x, gate, g2.reshape(1, d), sc2, sh2)


def _log_sigmoid(x):
    return jnp.minimum(x, 0.0) - jnp.log1p(jnp.exp(-jnp.abs(x)))


def _mlstm_kernel(q_ref, k_ref, v_ref, g_ref, gb_ref, c0_ref, n0_ref, m0_ref, h_ref, cf_ref, nf_ref, mf_ref,
                  c_s, m_s, *, nh, dh, tc, nc):
    d = pl.program_id(0)
    c = pl.program_id(2)

    @pl.when(c == 0)
    def _():
        for h in range(nh):
            c_s[h, :, :dh] = c0_ref[0, 0, h]
            c_s[h, :, dh:] = jnp.broadcast_to(n0_ref[0, 0, h:h + 1, :], (dh, dh)).T
        m_s[...] = m0_ref[0, 0]

    ones = jnp.ones((tc, dh), F32)
    gates = g_ref[...] + gb_ref[0]
    lane = lax.broadcasted_iota(jnp.int32, gates.shape, 1)
    logf = jnp.where(jnp.logical_and(lane >= nh, lane < 2 * nh), _log_sigmoid(gates), 0.0)
    r_i = lax.broadcasted_iota(jnp.int32, (tc, tc), 0)
    s_i = lax.broadcasted_iota(jnp.int32, (tc, tc), 1)
    causal = (r_i - s_i) * (1 - 2 * d) >= 0
    bcum = _dot(causal.astype(F32), logf, precision=HIGHEST)
    btot = jnp.sum(logf, axis=0, keepdims=True)
    gates_t = gates.T
    bcum_t = bcum.T
    for h in range(nh):
        hs = slice(h * dh, (h + 1) * dh)
        q = q_ref[:, hs]
        k = k_ref[:, hs]
        v = v_ref[:, hs]
        b_col = bcum[:, nh + h:nh + h + 1]
        b_row = bcum_t[nh + h:nh + h + 1, :]
        i_col = gates[:, h:h + 1]
        i_row = gates_t[h:h + 1, :]
        m = m_s[h:h + 1, :]
        a = b_col + m
        dmat = jnp.where(causal, b_col - b_row + i_row, NEG_INF)
        mq = jnp.maximum(a, jnp.max(dmat, axis=-1, keepdims=True))
        w_intra = jnp.exp(dmat - mq)
        w_inter = jnp.exp(a - mq)
        s = _dot_nt(q, k) * w_intra
        cn = c_s[h]
        v1 = jnp.concatenate([v, ones], axis=1)
        both = _dot(s, v1) + w_inter * _dot(q, cn)
        h_ref[0, :, hs] = both[:, :dh] / jnp.maximum(jnp.abs(both[:, dh:]), jnp.exp(-mq))
        b_last = btot[:, nh + h:nh + h + 1]
        g = b_last - b_col + i_col
        m_new = jnp.maximum(b_last + m, jnp.max(g, axis=0, keepdims=True))
        kw = k * jnp.exp(g - m_new)
        keep = jnp.exp(b_last + m - m_new)
        c_s[h] = keep * cn + _dot(kw.T, v1)
        m_s[h:h + 1, :] = m_new

    @pl.when(c == nc - 1)
    def _():
        for h in range(nh):
            cf_ref[0, 0, h] = c_s[h, :, :dh]
            nf_ref[0, 0, h:h + 1, :] = c_s[h, :, dh:].T[0:1, :]
        mf_ref[0, 0] = m_s[...]


def mlstm_scan(qk, proj, vcol, gate_bias, c0, n0, m0, nseq, seq_len, nh, dh):
    tc = ML_CHUNK
    nc = seq_len // tc
    w = nh * dh

    def chunk(d, c):
        return c + d * (nc - 1 - 2 * c)

    rowblk = lambda d, b, c: b * nc + chunk(d, c)
    st = lambda shp: pl.BlockSpec((1, 1) + shp, lambda d, b, c: (b, d) + (0,) * len(shp))
    return pl.pallas_call(
        functools.partial(_mlstm_kernel, nh=nh, dh=dh, tc=tc, nc=nc), grid=(2, nseq, nc),
        in_specs=[pl.BlockSpec((tc, w), lambda d, b, c: (rowblk(d, b, c), 0)),
                  pl.BlockSpec((tc, w), lambda d, b, c: (rowblk(d, b, c), 1)),
                  pl.BlockSpec((tc, w), lambda d, b, c: (rowblk(d, b, c), vcol)),
                  pl.BlockSpec((tc, LANES), lambda d, b, c: (rowblk(d, b, c), 4 * w // LANES + d)),
                  pl.BlockSpec((1, 1, LANES), lambda d, b, c: (d, 0, 0)),
                  st((nh, dh, dh)), st((nh, dh)), st((nh, 1))],
        out_specs=[pl.BlockSpec((1, tc, w), lambda d, b, c: (d, rowblk(d, b, c), 0)),
                   st((nh, dh, dh)), st((nh, dh)), st((nh, 1))],
        out_shape=[jax.ShapeDtypeStruct((2, nseq * seq_len, w), F32),
                   jax.ShapeDtypeStruct((nseq, 2, nh, dh, dh), F32), jax.ShapeDtypeStruct((nseq, 2, nh, dh), F32),
                   jax.ShapeDtypeStruct((nseq, 2, nh, 1), F32)],
        scratch_shapes=[pltpu.VMEM((nh, dh, 2 * dh), F32), pltpu.VMEM((nh, 1), F32)],
        compiler_params=_cp("parallel", "parallel", "arbitrary"), name="mlstm_scan",
    )(qk, qk, proj, proj, gate_bias, c0, n0, m0)


def _odd_out_kernel(hf_ref, hb_ref, og_ref, ng_ref, w_ref, x_ref, gate_ref, g2_ref, sc2_ref, sh2_ref, o_ref, hn_ref,
                    a_s, *, nh, dh):
    for h in range(nh):
        hs = slice(h * dh, (h + 1) * dh)
        blk = hf_ref[0, :, hs] + hb_ref[0, :, hs]
        blk = blk * lax.rsqrt(jnp.mean(blk * blk, axis=-1, keepdims=True) + EPS)
        a_s[:, hs] = ((blk * ng_ref[:, hs]) * _silu(og_ref[:, hs])).astype(BF16)
    _residual_and_next_norm(x_ref, gate_ref, _dot(a_s[...], w_ref[...]), g2_ref, sc2_ref, sh2_ref, o_ref, hn_ref)


def odd_out(h2, proj, ocol, norm_g, w_bf16, x, gate, g2, sc2, sh2, rows_per_mod, nh, dh, tm):
    n, d = x.shape
    w = nh * dh
    out_specs, out_shape = _row_outs(n, d, tm)
    return pl.pallas_call(
        functools.partial(_odd_out_kernel, nh=nh, dh=dh), grid=(n // tm,),
        in_specs=[pl.BlockSpec((1, tm, w), lambda i: (0, i, 0)), pl.BlockSpec((1, tm, w), lambda i: (1, i, 0)),
                  pl.BlockSpec((tm, w), lambda i: (i, ocol)), pl.BlockSpec((1, w), lambda i: (0, 0)),
                  pl.BlockSpec((w, d), lambda i: (0, 0))] + _row_specs(d, tm, rows_per_mod),
        out_specs=out_specs, out_shape=out_shape, scratch_shapes=[pltpu.VMEM((tm, w), BF16)],
        compiler_params=_cp("parallel"), name="odd_out",
    )(h2, h2, proj, norm_g.reshape(1, w), w_bf16, x, gate, g2.reshape(1, d), sc2, sh2)


def _top_values(curs, k, outs):
    curs = list(curs)
    for j in range(k):
        for a, out_s in enumerate(outs):
            m = jnp.max(curs[a], axis=0, keepdims=True)
            out_s[j:j + 1, :] = m
            curs[a] = jnp.where(curs[a] == m, NEG_INF, curs[a])


def _pair_candidates(k):
    return [(a, k // (a + 1)) for a in range(k)]


PEER_HEADS_PER_TRIP = 8


def _peer_score_kernel(h_ref, wq_ref, keys_ref, th_ref, s2_ref, w1_ref, w2_ref, q_s, v_s, cand_s, best_s, *,
                       nh, half, topk):
    q_s[...] = _dot_nt(wq_ref[...], h_ref[...])
    kk = topk + 1
    cand_s[...] = jnp.full(cand_s.shape, NEG_INF, F32)
    group = range(PEER_HEADS_PER_TRIP)

    def heads(trip, carry):
        hds = [trip * PEER_HEADS_PER_TRIP + u for u in group]
        scores = []
        for hd in hds:
            base = pl.multiple_of(hd * 2 * half, 2 * half)
            scores.append(_dot(keys_ref[hd, 0], q_s[pl.ds(base, half), :]))
            scores.append(_dot(keys_ref[hd, 1], q_s[pl.ds(base + half, half), :]))
        _top_values(scores, kk, [v_s.at[u, c] for u in group for c in range(2)])
        for u in group:
            off = 0
            for a, cnt in _pair_candidates(kk):
                cand_s[u, off:off + cnt, :] = v_s[u, 0, a:a + 1, :] + v_s[u, 1, 0:cnt, :]
                off += cnt
        _top_values([cand_s[u] for u in group], kk, [best_s.at[u] for u in group])
        for u, hd in enumerate(hds):
            s1, s2 = scores[2 * u], scores[2 * u + 1]
            best = best_s[u, 0:topk, :]
            z = jnp.sum(jnp.exp(best - best[0:1, :]), axis=0, keepdims=True)
            tmid = 0.5 * (best_s[u, topk - 1:topk, :] + best_s[u, topk:topk + 1, :])
            th = tmid - s1
            w1 = jnp.exp(s1 - v_s[u, 0, 0:1, :]) / z
            w2 = jnp.exp(s2 - v_s[u, 1, 0:1, :])
            for lt in range(s1.shape[1] // LANES):
                sl = slice(lt * LANES, (lt + 1) * LANES)
                th_ref[hd, lt] = th[:, sl]
                s2_ref[hd, lt] = s2[:, sl]
                w1_ref[hd, lt] = w1[:, sl]
                w2_ref[hd, lt] = w2[:, sl]
        return carry

    lax.fori_loop(0, nh // PEER_HEADS_PER_TRIP, heads, 0)


def peer_scores(h_bf16, wq_t_bf16, keys, tt):
    n, d = h_bf16.shape
    nh, _, nk, half = keys.shape
    kk = PK_TOPK + 1
    ncand = -(-sum(c for _, c in _pair_candidates(kk)) // 8) * 8
    big = jax.ShapeDtypeStruct((nh, n // LANES, nk, LANES), F32)
    bspec = pl.BlockSpec((nh, tt // LANES, nk, LANES), lambda i: (0, i, 0, 0))
    return pl.pallas_call(
        functools.partial(_peer_score_kernel, nh=nh, half=half, topk=PK_TOPK), grid=(n // tt,),
        in_specs=[pl.BlockSpec((tt, d), lambda i: (i, 0)), pl.BlockSpec((nh * 2 * half, d), lambda i: (0, 0)),
                  pl.BlockSpec((nh, 2, nk, half), lambda i: (0, 0, 0, 0))],
        out_specs=[bspec, bspec, bspec, bspec],
        out_shape=[big, big, big, big],
        scratch_shapes=[pltpu.VMEM((nh * 2 * half, tt), F32), pltpu.VMEM((PEER_HEADS_PER_TRIP, 2, 24, tt), F32),
                        pltpu.VMEM((PEER_HEADS_PER_TRIP, ncand, tt), F32), pltpu.VMEM((PEER_HEADS_PER_TRIP, 24, tt), F32)],
        compiler_params=_cp("parallel"), name="peer_scores",
    )(h_bf16, wq_t_bf16, keys)


PEER_KEY_ROWS = 16
GELU_C1 = math.sqrt(2.0 / math.pi)
GELU_C2 = 0.044715 * GELU_C1


def _gelu_tanh(x):
    half_x = 0.5 * x
    return half_x + half_x * jnp.tanh(x * (GELU_C1 + GELU_C2 * (x * x)))


def _peer_dense_kernel(h_ref, u_ref, vt_ref, th_ref, s2_ref, w1_ref, w2_ref, x_ref, gate_ref, fg_ref, o_ref,
                       acc_s, st_s, wt_s, *, nh, nk, ec, tt, final):
    e = pl.program_id(1)
    nlt = tt // LANES
    n_i1 = ec // nk
    nkt = nk // PEER_KEY_ROWS
    nsub = PEER_KEY_ROWS // SUBLANES

    @pl.when(e == 0)
    def _():
        acc_s[...] = jnp.zeros_like(acc_s)

    st = _gelu_tanh(_dot_nt(u_ref[...], h_ref[...]))
    for lt in range(nlt):
        st_s[lt] = st[:, lt * LANES:(lt + 1) * LANES]

    def tile(idx, carry):
        lt = idx // nkt
        k0 = (idx % nkt) * PEER_KEY_ROWS
        subs = [pl.ds(pl.multiple_of(k0 + j * SUBLANES, SUBLANES), SUBLANES) for j in range(nsub)]
        g = [[jnp.zeros((SUBLANES, LANES), F32) for _ in subs] for _ in range(n_i1)]
        for hd in range(nh):
            s2t = [s2_ref[hd, lt, sub, :] for sub in subs]
            w2t = [w2_ref[hd, lt, sub, :] for sub in subs]
            for li in range(n_i1):
                thb = jnp.broadcast_to(th_ref[hd, lt, li:li + 1, :], (SUBLANES, LANES))
                w1b = jnp.broadcast_to(w1_ref[hd, lt, li:li + 1, :], (SUBLANES, LANES))
                for j in range(nsub):
                    g[li][j] = g[li][j] + jnp.where(s2t[j] >= thb, w2t[j] * w1b, 0.0)
        for li in range(n_i1):
            rows = pl.ds(pl.multiple_of(li * nk + k0, PEER_KEY_ROWS), PEER_KEY_ROWS)
            wt_s[lt, rows, :] = (st_s[lt, rows, :] * jnp.concatenate(g[li], axis=0)).astype(BF16)
        return carry

    lax.fori_loop(0, nlt * nkt, tile, 0)
    wt = jnp.concatenate([wt_s[lt] for lt in range(nlt)], axis=1)
    acc_s[...] += _dot(vt_ref[0], wt)

    @pl.when(e == pl.num_programs(1) - 1)
    def _():
        xn = x_ref[...] + gate_ref[0] * acc_s[...].T
        if final:
            xn = (xn * lax.rsqrt(jnp.mean(xn * xn, axis=-1, keepdims=True) + EPS)) * fg_ref[...]
        o_ref[...] = xn


def peer_dense(h_bf16, u_bf16, vt_bf16, th, s2, w1, w2, x, gate, final_g, final, rows_per_mod, tt, ec):
    n, d = x.shape
    nh, _, nk, _ = s2.shape
    nchunk = u_bf16.shape[0] // ec
    bspec = pl.BlockSpec((nh, tt // LANES, nk, LANES), lambda i, e: (0, i, 0, 0))
    rspec = pl.BlockSpec((nh, tt // LANES, ec // nk, LANES), lambda i, e: (0, i, e, 0))
    tile_buf = (tt // LANES, ec, LANES)
    return pl.pallas_call(
        functools.partial(_peer_dense_kernel, nh=nh, nk=nk, ec=ec, tt=tt, final=final), grid=(n // tt, nchunk),
        in_specs=[pl.BlockSpec((tt, d), lambda i, e: (i, 0)), pl.BlockSpec((ec, d), lambda i, e: (e, 0)),
                  pl.BlockSpec((1, d, ec), lambda i, e: (e, 0, 0)), rspec, bspec, rspec, bspec,
                  pl.BlockSpec((tt, d), lambda i, e: (i, 0)),
                  pl.BlockSpec((1, 1, d), lambda i, e: ((i * tt) // rows_per_mod, 0, 0)),
                  pl.BlockSpec((1, d), lambda i, e: (0, 0))],
        out_specs=pl.BlockSpec((tt, d), lambda i, e: (i, 0)),
        out_shape=jax.ShapeDtypeStruct((n, d), F32),
        scratch_shapes=[pltpu.VMEM((d, tt), F32), pltpu.VMEM(tile_buf, F32), pltpu.VMEM(tile_buf, BF16)],
        compiler_params=_cp("parallel", "arbitrary"), name="peer_dense",
    )(h_bf16, u_bf16, vt_bf16, th, s2, w1, w2, x, gate, final_g.reshape(1, d))


def _s5_params(a_re, a_im, b_re, b_im, c_re, c_im, log_step):
    lam = lax.complex(a_re.astype(F32), a_im.astype(F32))
    lam_bar = jnp.exp(lam * jnp.exp(log_step.astype(F32))[..., None])
    b_bar = ((lam_bar - 1.0) / lam)[..., None] * lax.complex(b_re.astype(F32), b_im.astype(F32))
    ngrp, npst, nch = b_bar.shape[1:]
    eye = jnp.eye(ngrp, dtype=F32)

    def b_mat(part):
        return jnp.einsum("dgpj,gh->dgjhp", part, eye).reshape(2, ngrp * nch, ngrp * npst)

    def c_mat(part):
        return jnp.einsum("dgjp,gh->dgphj", part, eye).reshape(2, ngrp * npst, ngrp * nch)

    bre, bim = b_mat(b_bar.real).astype(BF16), b_mat(b_bar.imag).astype(BF16)
    cre, cim = c_mat(c_re.astype(F32)).astype(BF16), c_mat(-c_im.astype(F32)).astype(BF16)
    lam2 = jnp.stack([lam_bar.real.reshape(2, -1), lam_bar.imag.reshape(2, -1)], axis=1)
    return bre, bim, cre, cim, lam2


def _pos_embed(n_tok, d, grid_w):
    rows = n_tok // grid_w
    quarter = d // 4
    omega = 1.0 / (10000.0 ** (jnp.arange(quarter, dtype=F32) / quarter))

    def emb1d(pos):
        ang = pos.astype(F32)[:, None] * omega[None]
        return jnp.concatenate([jnp.sin(ang), jnp.cos(ang)], axis=-1)

    er = emb1d(jnp.arange(rows))
    ec = emb1d(jnp.arange(grid_w))
    half = d // 2
    pe = jnp.concatenate([jnp.broadcast_to(er[:, None], (rows, grid_w, half)),
                          jnp.broadcast_to(ec[None], (rows, grid_w, half))], axis=-1)
    return pe.reshape(rows * grid_w, d)


def _tile(n, pref):
    return pref if n % pref == 0 else n


def _trunk(x, mods, s5_h0, ml_c0, ml_n0, ml_m0, p, nseq, seq_len, rows_per_mod):
    n, d = x.shape
    tm = _tile(min(rows_per_mod, n), 512)
    depth = p["norm_g"].shape[0]
    s5_fin, ml_fin = [], []
    for l in range(depth):
        sh1, sc1, g1, sh2, sc2, g2 = mods[l]
        i = l // 2
        if l % 2 == 0:
            hw = p["hy_bias"].shape[2]
            sw = p["s5_d"].shape[1]
            proj = normmod_matmul(x, p["norm_g"][l, 0], sc1, sh1, p["ev_w_in"][i].astype(BF16), rows_per_mod, tm,
                                  _tile(3 * hw + sw, 512))
            hy_in = short_conv(proj, 3 * hw, p["hy_conv_w"][i], p["hy_conv_b"][i], jnp.ones((3 * hw,), F32),
                               seq_len, act=False)
            cos_t, a_t, a_tt = dft_tables(seq_len)
            tf = _tile(seq_len, 512)
            taps, sumsq = hyena_filter_taps(seq_len, p["hy_w1"][i], p["hy_b1"][i], p["hy_w2"][i], p["hy_b2"][i],
                                            p["hy_w3"][i], p["hy_freq"][i], p["hy_decay"][i], hw)
            kr, ki = hyena_filter_spectrum(cos_t, a_t, taps, sumsq, hw, tf)
            bias = p["hy_bias"][i].astype(F32)
            z, zcol = hy_in, 0
            for o in range(bias.shape[0]):
                yr, yi = hyena_fwd(cos_t, a_t, z, zcol, kr, ki, o, nseq, hw, tf)
                z = hyena_inv(cos_t, a_tt, yr, yi, z, zcol, hy_in, 1 + o, bias[o:o + 1], nseq, hw, tf)
                zcol = 0
            bre, bim, cre, cim, lam2 = _s5_params(p["s5_a_re"][i], p["s5_a_im"][i], p["s5_b_re"][i], p["s5_b_im"][i],
                                                  p["s5_c_re"][i], p["s5_c_im"][i], p["s5_log_step"][i])
            ucol = 3 * hw // sw
            y2, hfin = s5_scan(proj, ucol, bre, bim, cre, cim, lam2, s5_h0[i], nseq, seq_len, sw, _tile(seq_len, 256))
            s5_fin.append(hfin)
            s5o = s5_glu(y2, proj, ucol, p["s5_d"][i], p["s5_glu_w"][i].astype(BF16), p["s5_glu_b"][i], tm)
            x, hn = even_out(z, s5o, p["ev_w_out"][i].astype(BF16), x, g1, p["norm_g"][l, 1], sc2, sh2, rows_per_mod, tm)
        else:
            nh = p["od_gate_b"].shape[2]
            w = p["ml_norm_g"].shape[1]
            dh = w // nh
            w_in = p["od_w_in"][i]
            wg = w_in[:, 4 * w:].reshape(d, 4, nh)
            gb = p["od_gate_b"][i].astype(F32)
            wg2 = jnp.zeros((d, 2, LANES), w_in.dtype)
            bg2 = jnp.zeros((2, 1, LANES), F32)
            for dr in range(2):
                wg2 = wg2.at[:, dr, :nh].set(wg[:, dr]).at[:, dr, nh:2 * nh].set(wg[:, 2 + dr])
                bg2 = bg2.at[dr, 0, :nh].set(gb[dr]).at[dr, 0, nh:2 * nh].set(gb[2 + dr])
            w_all = jnp.concatenate([w_in[:, :4 * w], wg2.reshape(d, 2 * LANES)], axis=1).astype(BF16)
            proj = normmod_matmul(x, p["norm_g"][l, 0], sc1, sh1, w_all, rows_per_mod, tm, w_all.shape[1] // 2)
            qscale = jnp.concatenate([jnp.full((w,), dh ** -0.5, F32), jnp.ones((w,), F32)])
            qk = short_conv(proj, 2 * w, p["ml_conv_w"][i], p["ml_conv_b"][i], qscale, seq_len, act=True)
            h2, cf, nf, mf = mlstm_scan(qk, proj, 2, bg2, ml_c0[i], ml_n0[i], ml_m0[i], nseq, seq_len, nh, dh)
            ml_fin.append((cf, nf, mf))
            x, hn = odd_out(h2, proj, 3, p["ml_norm_g"][i], p["od_w_out"][i].astype(BF16), x, g1, p["norm_g"][l, 1],
                            sc2, sh2, rows_per_mod, nh, dh, tm)
        tt = _tile(min(rows_per_mod, n), 512)
        th, s2, w1, w2 = peer_scores(hn, p["pk_w_q"][l].T.astype(BF16), p["pk_keys"][l].astype(F32), _tile(tt, 256))
        ec = 2048
        vt = p["pk_v"][l].astype(BF16).reshape(-1, ec, d).transpose(0, 2, 1)
        x = peer_dense(hn, p["pk_u"][l].astype(BF16), vt, th, s2, w1, w2, x, g2, p["final_g"], l == depth - 1,
                       rows_per_mod, tt, ec)
    return x, s5_fin, ml_fin


def kernel(x_prompt, x_sample, state_s5_re, state_s5_im, state_mlstm_C, state_mlstm_n, state_mlstm_m, c, c_ctx, norm_g, ada_w, ada_b, final_g, ev_w_in, hy_conv_w, hy_conv_b, hy_w1, hy_b1, hy_w2, hy_b2, hy_w3, hy_freq, hy_decay, hy_bias, s5_a_re, s5_a_im, s5_b_re, s5_b_im, s5_c_re, s5_c_im, s5_log_step, s5_d, s5_glu_w, s5_glu_b, ev_w_out, od_w_in, od_gate_b, ml_conv_w, ml_conv_b, ml_norm_g, od_w_out, pk_w_q, pk_keys, pk_u, pk_v):
    p = dict(norm_g=norm_g, ada_w=ada_w, ada_b=ada_b, final_g=final_g, ev_w_in=ev_w_in,
             hy_conv_w=hy_conv_w, hy_conv_b=hy_conv_b, hy_w1=hy_w1, hy_b1=hy_b1, hy_w2=hy_w2, hy_b2=hy_b2,
             hy_w3=hy_w3, hy_freq=hy_freq, hy_decay=hy_decay, hy_bias=hy_bias, s5_a_re=s5_a_re,
             s5_a_im=s5_a_im, s5_b_re=s5_b_re, s5_b_im=s5_b_im, s5_c_re=s5_c_re, s5_c_im=s5_c_im,
             s5_log_step=s5_log_step, s5_d=s5_d, s5_glu_w=s5_glu_w, s5_glu_b=s5_glu_b, ev_w_out=ev_w_out,
             od_w_in=od_w_in, od_gate_b=od_gate_b, ml_conv_w=ml_conv_w, ml_conv_b=ml_conv_b,
             ml_norm_g=ml_norm_g, od_w_out=od_w_out, pk_w_q=pk_w_q, pk_keys=pk_keys, pk_u=pk_u, pk_v=pk_v)
    nb, seq, d = x_prompt.shape
    db, dseq, _ = x_sample.shape
    depth = norm_g.shape[0]
    n_even, n_odd = (depth + 1) // 2, depth // 2
    assert db + 1 <= 8

    cond8 = jnp.zeros((8, d), F32).at[0].set(c_ctx.astype(F32)).at[1:1 + db].set(c.astype(F32))
    mods_ctx, mods_lat = [], []
    for l in range(depth):
        mod = ada_mod(cond8, ada_w[l].astype(F32), ada_b[l].astype(F32))
        chunks = [mod[:, j * d:(j + 1) * d] for j in range(6)]
        mods_ctx.append([ch[0:1].reshape(1, 1, d) for ch in chunks])
        mods_lat.append([ch[1:1 + db].reshape(db, 1, d) for ch in chunks])

    def s5_state(re, im, bsz):
        return [jnp.stack([re[:, i].reshape(bsz, 2, -1), im[:, i].reshape(bsz, 2, -1)], axis=2).astype(F32)
                for i in range(n_even)]

    ngrp, npst = s5_a_re.shape[2], s5_a_re.shape[3]
    nh, dh = state_mlstm_C.shape[3], state_mlstm_C.shape[4]
    zeros_s5 = jnp.zeros((nb, n_even, 2, ngrp, npst), F32)
    y_prompt, s5_fin, ml_fin = _trunk(
        x_prompt.reshape(nb * seq, d), mods_ctx, s5_state(zeros_s5, zeros_s5, nb),
        [jnp.zeros((nb, 2, nh, dh, dh), F32)] * n_odd, [jnp.zeros((nb, 2, nh, dh), F32)] * n_odd,
        [jnp.zeros((nb, 2, nh, 1), F32)] * n_odd, p, nb, seq, nb * seq)
    x_lat = add_pos(x_sample.reshape(db * dseq, d), _pos_embed(dseq, d, GRID_W), dseq, _tile(dseq, 512))
    y_sample, _, _ = _trunk(
        x_lat, mods_lat, s5_state(state_s5_re, state_s5_im, db),
        [state_mlstm_C[:, i].astype(F32) for i in range(n_odd)], [state_mlstm_n[:, i].astype(F32) for i in range(n_odd)],
        [state_mlstm_m[:, i].astype(F32)[..., None] for i in range(n_odd)], p, db, dseq, dseq)

    new_s5_re = jnp.stack([h[:, :, 0].reshape(nb, 2, ngrp, npst) for h in s5_fin], axis=1)
    new_s5_im = jnp.stack([h[:, :, 1].reshape(nb, 2, ngrp, npst) for h in s5_fin], axis=1)
    new_c = jnp.stack([f[0] for f in ml_fin], axis=1)
    new_n = jnp.stack([f[1] for f in ml_fin], axis=1)
    new_m = jnp.stack([f[2][..., 0] for f in ml_fin], axis=1)
    return (y_prompt.reshape(nb, seq, d), y_sample.reshape(db, dseq, d), new_s5_re, new_s5_im, new_c, new_n, new_m)
```

```python
import functools
import math

import jax
import jax.numpy as jnp
from jax import lax
from jax.experimental import pallas as pl
from jax.experimental.pallas import tpu as pltpu

F32 = jnp.float32
BF16 = jnp.bfloat16
EPS = 1e-6
HIGHEST = lax.Precision.HIGHEST
V7X_VMEM_LIMIT_BYTES = 56 * 1024 * 1024
LANES = 128
SUBLANES = 8
ML_CHUNK = 128
PK_TOPK = 16
GRID_W = 64
NEG_INF = float("-inf")


def _cp(*sem):
    return pltpu.CompilerParams(dimension_semantics=sem, vmem_limit_bytes=V7X_VMEM_LIMIT_BYTES)


def _dot(a, b, **kw):
    return jnp.dot(a, b, preferred_element_type=F32, **kw)


def _dot_nt(a, b):
    return lax.dot_general(a, b, (((1,), (1,)), ((), ())), preferred_element_type=F32)


def _silu(x):
    return x * jax.nn.sigmoid(x)


def _ada_kernel(c_ref, w_ref, b_ref, o_ref):
    o_ref[...] = _dot(_silu(c_ref[...]), w_ref[...], precision=HIGHEST) + b_ref[...]


def ada_mod(cond8, w, b):
    d, no = w.shape
    tn = 1536 if no % 1536 == 0 else no
    return pl.pallas_call(
        _ada_kernel, grid=(no // tn,),
        in_specs=[pl.BlockSpec((8, d), lambda j: (0, 0)), pl.BlockSpec((d, tn), lambda j: (0, j)),
                  pl.BlockSpec((1, tn), lambda j: (0, j))],
        out_specs=pl.BlockSpec((8, tn), lambda j: (0, j)),
        out_shape=jax.ShapeDtypeStruct((8, no), F32), compiler_params=_cp("parallel"), name="ada_mod",
    )(cond8, w, b.reshape(1, no))


def _normmod(x, g, sc, sh):
    y = x * lax.rsqrt(jnp.mean(x * x, axis=-1, keepdims=True) + EPS)
    return (y * g) * (1.0 + sc) + sh


def _mod_spec(d, tm, rows_per_mod):
    return pl.BlockSpec((1, 1, d), lambda i, j: ((i * tm) // rows_per_mod, 0, 0))


def _nm_matmul_kernel(x_ref, g_ref, sc_ref, sh_ref, w_ref, o_ref, h_ref):
    @pl.when(pl.program_id(1) == 0)
    def _():
        h_ref[...] = _normmod(x_ref[...], g_ref[...], sc_ref[0], sh_ref[0]).astype(BF16)
    o_ref[...] = _dot(h_ref[...], w_ref[...])


def normmod_matmul(x, g, sc, sh, w_bf16, rows_per_mod, tm, tn):
    n, d = x.shape
    no = w_bf16.shape[1]
    return pl.pallas_call(
        _nm_matmul_kernel, grid=(n // tm, no // tn),
        in_specs=[pl.BlockSpec((tm, d), lambda i, j: (i, 0)), pl.BlockSpec((1, d), lambda i, j: (0, 0)),
                  _mod_spec(d, tm, rows_per_mod), _mod_spec(d, tm, rows_per_mod),
                  pl.BlockSpec((d, tn), lambda i, j: (0, j))],
        out_specs=pl.BlockSpec((tm, tn), lambda i, j: (i, j)),
        out_shape=jax.ShapeDtypeStruct((n, no), F32),
        scratch_shapes=[pltpu.VMEM((tm, d), BF16)],
        compiler_params=_cp("parallel", "arbitrary"), name="normmod_matmul",
    )(x, g.reshape(1, d), sc, sh, w_bf16)


def _add_rows_kernel(x_ref, p_ref, o_ref):
    o_ref[...] = x_ref[...] + p_ref[...]


def add_pos(x, pe, seq_len, tm):
    n, d = x.shape
    nb = seq_len // tm
    return pl.pallas_call(
        _add_rows_kernel, grid=(n // tm,),
        in_specs=[pl.BlockSpec((tm, d), lambda i: (i, 0)), pl.BlockSpec((tm, d), lambda i: (i % nb, 0))],
        out_specs=pl.BlockSpec((tm, d), lambda i: (i, 0)),
        out_shape=jax.ShapeDtypeStruct((n, d), F32), compiler_params=_cp("parallel"), name="add_pos",
    )(x, pe)


def _sconv_kernel(x_ref, w_ref, b_ref, s_ref, o_ref, *, act):
    x = x_ref[...]
    n_tok = x.shape[0]
    row = lax.broadcasted_iota(jnp.int32, x.shape, 0)
    prev = jnp.where(row == 0, 0.0, pltpu.roll(x, 1, 0))
    nxt = jnp.where(row == n_tok - 1, 0.0, pltpu.roll(x, n_tok - 1, 0))
    y = prev * w_ref[0:1, :] + x * w_ref[1:2, :] + nxt * w_ref[2:3, :] + b_ref[...]
    if act:
        y = _silu(y) * s_ref[...]
    o_ref[...] = y.astype(o_ref.dtype)


def short_conv(a, ncols, w, b, scale, seq_len, act, out_dtype, cb=256):
    n = a.shape[0]
    return pl.pallas_call(
        functools.partial(_sconv_kernel, act=act), grid=(n // seq_len, ncols // cb),
        in_specs=[pl.BlockSpec((seq_len, cb), lambda s, j: (s, j)), pl.BlockSpec((3, cb), lambda s, j: (0, j)),
                  pl.BlockSpec((1, cb), lambda s, j: (0, j)), pl.BlockSpec((1, cb), lambda s, j: (0, j))],
        out_specs=pl.BlockSpec((seq_len, cb), lambda s, j: (s, j)),
        out_shape=jax.ShapeDtypeStruct((n, ncols), out_dtype), compiler_params=_cp("parallel", "parallel"),
        name="short_conv",
    )(a, w, b.reshape(1, ncols), scale.reshape(1, ncols))


def dft_tables(n_tok):
    k = jnp.arange(n_tok, dtype=jnp.int32)
    blk = 1 << ((n_tok.bit_length() - 1) // 2)
    def thin(n):
        ang = ((k[:, None] * n[None, :]) % (2 * n_tok)).astype(F32) * (math.pi / n_tok)
        return jnp.cos(ang), jnp.sin(ang)
    (c_hi, s_hi), (c_lo, s_lo) = thin(jnp.arange(0, n_tok, blk, dtype=jnp.int32)), thin(jnp.arange(blk, dtype=jnp.int32))
    cos_t = (c_hi[:, :, None] * c_lo[:, None, :] - s_hi[:, :, None] * s_lo[:, None, :]).reshape(n_tok, n_tok)
    msin = -(s_hi[:, :, None] * c_lo[:, None, :] + c_hi[:, :, None] * s_lo[:, None, :]).reshape(n_tok, n_tok)
    alt = jnp.where(k % 2 == 0, 1.0, -1.0).astype(F32)
    a_t = msin.at[0, :].set(alt)
    a_tt = msin.at[:, 0].set(alt)
    return cos_t.astype(BF16), a_t.astype(BF16), a_tt.astype(BF16)


def _hyfilt_kernel(band_ref, w1_ref, b1_ref, w2_ref, b2_ref, w3_ref, fr_ref, dec_ref, h_ref, ss_ref, *,
                   n_tok, tl, hw, nbands):
    i = pl.program_id(0)
    pos = i * tl + lax.broadcasted_iota(jnp.int32, (tl, 1), 0)
    t = pos.astype(F32) / n_tok
    lane = lax.broadcasted_iota(jnp.int32, (tl, LANES), 1)
    ang = 2.0 * math.pi * t * band_ref[...]
    z = jnp.where(lane == 0, t, jnp.where(lane <= nbands, jnp.cos(ang),
                                          jnp.where(lane <= 2 * nbands, jnp.sin(ang), 0.0)))
    fr = fr_ref[...]
    h = jnp.sin(fr * (_dot(z, w1_ref[...], precision=HIGHEST) + b1_ref[...]))
    h = jnp.sin(fr * (_dot(h, w2_ref[...], precision=HIGHEST) + b2_ref[...]))
    h = _dot(h, w3_ref[...], precision=HIGHEST) * jnp.exp(-t * jnp.abs(dec_ref[...]))
    col = lax.broadcasted_iota(jnp.int32, h.shape, 1)
    is_bwd = (col // hw) % 2 == 1
    h = jnp.where(jnp.logical_and(is_bwd, pos == 0), 0.0, h)
    h_ref[...] = h.astype(BF16)

    @pl.when(i == 0)
    def _():
        ss_ref[...] = jnp.zeros_like(ss_ref)
    ss_ref[...] += jnp.sum(h * h, axis=0, keepdims=True)


def hyena_filter_taps(n_tok, w1, b1, w2, b2, w3, freq, decay, hw):
    emb, ffn = w1.shape
    nbands = (emb - 1) // 2
    tl = min(n_tok, 512)
    bands = jnp.linspace(1e-4, nbands - 1, nbands, dtype=F32)
    band_row = jnp.zeros((1, LANES), F32).at[0, 1:1 + nbands].set(bands).at[0, 1 + nbands:1 + 2 * nbands].set(bands)
    w1p = jnp.zeros((LANES, ffn), F32).at[:emb].set(w1)
    nc = w3.shape[1]
    full = lambda shp: pl.BlockSpec(shp, lambda i: (0, 0))
    return pl.pallas_call(
        functools.partial(_hyfilt_kernel, n_tok=n_tok, tl=tl, hw=hw, nbands=nbands), grid=(n_tok // tl,),
        in_specs=[full((1, LANES)), full((LANES, ffn)), full((1, ffn)), full((ffn, ffn)), full((1, ffn)),
                  full((ffn, nc)), full((1, ffn)), full((1, nc))],
        out_specs=[pl.BlockSpec((tl, nc), lambda i: (i, 0)), full((1, nc))],
        out_shape=[jax.ShapeDtypeStruct((n_tok, nc), BF16), jax.ShapeDtypeStruct((1, nc), F32)],
        compiler_params=_cp("arbitrary"), name="hyena_filter_taps",
    )(band_row, w1p, b1.reshape(1, ffn), w2, b2.reshape(1, ffn), w3, freq.reshape(1, ffn), decay.reshape(1, nc))


def _filt_dft_kernel(c_ref, a_ref, h_ref, ss_ref, kr_ref, ki_ref, *, tf, hw):
    i = pl.program_id(1)
    hf = h_ref[:, :hw]
    hb = h_ref[:, hw:]
    cc = c_ref[...]
    aa = a_ref[...]
    zrf, zif, zrb, zib = _dot(cc, hf), _dot(aa, hf), _dot(cc, hb), _dot(aa, hb)
    scale = lax.rsqrt(ss_ref[:, :hw] + ss_ref[:, hw:] + EPS)
    first = (i * tf + lax.broadcasted_iota(jnp.int32, (tf, 1), 0)) == 0
    scale = scale * jnp.where(first, 0.5, 1.0)
    kr_ref[0] = (zrf + zrb) * scale
    ki_ref[0] = jnp.where(first, zif + zib, zif - zib) * scale


def hyena_filter_spectrum(cos_t, a_t, taps, sumsq, hw, tf):
    n_tok = cos_t.shape[0]
    norder = taps.shape[1] // (2 * hw)
    out = jax.ShapeDtypeStruct((norder, n_tok, hw), F32)
    return pl.pallas_call(
        functools.partial(_filt_dft_kernel, tf=tf, hw=hw), grid=(norder, n_tok // tf),
        in_specs=[pl.BlockSpec((tf, n_tok), lambda o, i: (i, 0)), pl.BlockSpec((tf, n_tok), lambda o, i: (i, 0)),
                  pl.BlockSpec((n_tok, 2 * hw), lambda o, i: (0, o)), pl.BlockSpec((1, 2 * hw), lambda o, i: (0, o))],
        out_specs=[pl.BlockSpec((1, tf, hw), lambda o, i: (o, i, 0))] * 2,
        out_shape=[out, out], compiler_params=_cp("parallel", "parallel"), name="hyena_filter_spectrum",
    )(cos_t, a_t, taps, sumsq)


def _hy_fwd_kernel(c_ref, a_ref, z_ref, kr_ref, ki_ref, yr_ref, yi_ref, *, tf):
    i = pl.program_id(0)
    zb = z_ref[...].astype(BF16)
    zr = _dot(c_ref[...], zb)
    zi = _dot(a_ref[...], zb)
    kr = kr_ref[0]
    ki = ki_ref[0]
    first = (i * tf + lax.broadcasted_iota(jnp.int32, (tf, 1), 0)) == 0
    yr_ref[...] = jnp.where(first, zr * kr, zr * kr - zi * ki).astype(BF16)
    yi_ref[...] = jnp.where(first, zi * ki, zr * ki + zi * kr).astype(BF16)


def hyena_fwd(cos_t, a_t, z, zcol, kr, ki, order, nseq, hw, tf):
    n_tok = cos_t.shape[0]
    nf = n_tok // tf
    out = jax.ShapeDtypeStruct((nseq * n_tok, hw), BF16)
    return pl.pallas_call(
        functools.partial(_hy_fwd_kernel, tf=tf), grid=(nf, nseq),
        in_specs=[pl.BlockSpec((tf, n_tok), lambda i, b: (i, 0)), pl.BlockSpec((tf, n_tok), lambda i, b: (i, 0)),
                  pl.BlockSpec((n_tok, hw), lambda i, b: (b, zcol)),
                  pl.BlockSpec((1, tf, hw), lambda i, b: (order, i, 0)),
                  pl.BlockSpec((1, tf, hw), lambda i, b: (order, i, 0))],
        out_specs=[pl.BlockSpec((tf, hw), lambda i, b: (b * nf + i, 0))] * 2,
        out_shape=[out, out], compiler_params=_cp("parallel", "parallel"), name="hyena_fwd",
    )(cos_t, a_t, z, kr, ki)


def _hy_inv_kernel(c_ref, at_ref, yr_ref, yi_ref, zp_ref, gate_ref, bias_ref, o_ref, *, inv_len):
    conv = (_dot(c_ref[...], yr_ref[...]) + _dot(at_ref[...], yi_ref[...])) * inv_len
    o_ref[...] = gate_ref[...] * (conv + bias_ref[...] * zp_ref[...])


def hyena_inv(cos_t, a_tt, yr, yi, zprev, zcol, gates, gcol, bias_row, nseq, hw, tf):
    n_tok = cos_t.shape[0]
    nf = n_tok // tf
    return pl.pallas_call(
        functools.partial(_hy_inv_kernel, inv_len=1.0 / n_tok), grid=(nf, nseq),
        in_specs=[pl.BlockSpec((tf, n_tok), lambda i, b: (i, 0)), pl.BlockSpec((tf, n_tok), lambda i, b: (i, 0)),
                  pl.BlockSpec((n_tok, hw), lambda i, b: (b, 0)), pl.BlockSpec((n_tok, hw), lambda i, b: (b, 0)),
                  pl.BlockSpec((tf, hw), lambda i, b: (b * nf + i, zcol)),
                  pl.BlockSpec((tf, hw), lambda i, b: (b * nf + i, gcol)),
                  pl.BlockSpec((1, hw), lambda i, b: (0, 0))],
        out_specs=pl.BlockSpec((tf, hw), lambda i, b: (b * nf + i, 0)),
        out_shape=jax.ShapeDtypeStruct((nseq * n_tok, hw), F32),
        compiler_params=_cp("parallel", "parallel"), name="hyena_inv",
    )(cos_t, a_tt, yr, yi, zprev, gates, bias_row)


S5_DIAG_BLOCKS = 2


S5_SEQS_PER_STEP = 4


def _s5_kernel(u_ref, bre_ref, bim_ref, cre_ref, cim_ref, lam_ref, h0_ref, y_ref, hfin_ref, hre_s, him_s, st_s, *,
               tc, nc, ns, nb):
    d = pl.program_id(0)
    c = pl.program_id(2)

    @pl.when(c == 0)
    def _():
        st_s[...] = h0_ref[:, 0]

    sw = u_ref.shape[2]
    halves = [(slice(j * sw // S5_DIAG_BLOCKS, (j + 1) * sw // S5_DIAG_BLOCKS),
               slice(j * ns // S5_DIAG_BLOCKS, (j + 1) * ns // S5_DIAG_BLOCKS)) for j in range(S5_DIAG_BLOCKS)]
    for j in range(nb):
        ub = u_ref[j].astype(BF16)
        for us, hs in halves:
            hre_s[j, :, hs] = _dot(ub[:, us], bre_ref[0, us, hs])
            him_s[j, :, hs] = _dot(ub[:, us], bim_ref[0, us, hs])
    lr = lam_ref[0, 0:1, :]
    li = lam_ref[0, 1:2, :]

    def body(t, carry):
        r = jnp.where(d == 0, t, tc - 1 - t)
        new = []
        for j, (hr, hi) in enumerate(carry):
            nr = lr * hr - li * hi + hre_s[j, pl.ds(r, 1), :]
            ni = lr * hi + li * hr + him_s[j, pl.ds(r, 1), :]
            hre_s[j, pl.ds(r, 1), :] = nr
            him_s[j, pl.ds(r, 1), :] = ni
            new.append((nr, ni))
        return tuple(new)

    start = tuple((st_s[j, 0:1, :], st_s[j, 1:2, :]) for j in range(nb))
    for j, (hr, hi) in enumerate(lax.fori_loop(0, tc, body, start, unroll=4)):
        st_s[j, 0:1, :] = hr
        st_s[j, 1:2, :] = hi
    for j in range(nb):
        for us, hs in halves:
            y_ref[0, j, :, us] = (_dot(hre_s[j, :, hs].astype(BF16), cre_ref[0, hs, us])
                                  + _dot(him_s[j, :, hs].astype(BF16), cim_ref[0, hs, us]))

    @pl.when(c == nc - 1)
    def _():
        hfin_ref[:, 0] = st_s[...]


def s5_scan(proj, ucol, bre, bim, cre, cim, lam, h0, nseq, seq_len, sw, tc):
    ns = bre.shape[2]
    nc = seq_len // tc
    nb = S5_SEQS_PER_STEP if nseq % S5_SEQS_PER_STEP == 0 else 1

    def chunk(d, c):
        return c + d * (nc - 1 - 2 * c)

    y, hfin = pl.pallas_call(
        functools.partial(_s5_kernel, tc=tc, nc=nc, ns=ns, nb=nb), grid=(2, nseq // nb, nc),
        in_specs=[pl.BlockSpec((nb, tc, sw), lambda d, b, c: (b, chunk(d, c), ucol)),
                  pl.BlockSpec((1, sw, ns), lambda d, b, c: (d, 0, 0)),
                  pl.BlockSpec((1, sw, ns), lambda d, b, c: (d, 0, 0)),
                  pl.BlockSpec((1, ns, sw), lambda d, b, c: (d, 0, 0)),
                  pl.BlockSpec((1, ns, sw), lambda d, b, c: (d, 0, 0)),
                  pl.BlockSpec((1, 2, ns), lambda d, b, c: (d, 0, 0)),
                  pl.BlockSpec((nb, 1, 2, ns), lambda d, b, c: (b, d, 0, 0))],
        out_specs=[pl.BlockSpec((1, nb, tc, sw), lambda d, b, c: (d, b, chunk(d, c), 0)),
                   pl.BlockSpec((nb, 1, 2, ns), lambda d, b, c: (b, d, 0, 0))],
        out_shape=[jax.ShapeDtypeStruct((2, nseq, seq_len, sw), F32), jax.ShapeDtypeStruct((nseq, 2, 2, ns), F32)],
        scratch_shapes=[pltpu.VMEM((nb, tc, ns), F32), pltpu.VMEM((nb, tc, ns), F32), pltpu.VMEM((nb, 2, ns), F32)],
        compiler_params=_cp("parallel", "parallel", "arbitrary"), name="s5_scan",
    )(proj.reshape(nseq, seq_len, proj.shape[1]), bre, bim, cre, cim, lam, h0)
    return y.reshape(2, nseq * seq_len, sw), hfin


def _s5_glu_kernel(yf_ref, yb_ref, u_ref, d_ref, w_ref, b_ref, o_ref):
    y = jax.nn.gelu(yf_ref[0] + yb_ref[0] + d_ref[...] * u_ref[...])
    o_ref[...] = y * jax.nn.sigmoid(_dot(y.astype(BF16), w_ref[...]) + b_ref[...])


def s5_glu(y2, proj, ucol, d_skip, glu_w_bf16, glu_b, tm):
    _, n, sw = y2.shape
    return pl.pallas_call(
        _s5_glu_kernel, grid=(n // tm,),
        in_specs=[pl.BlockSpec((1, tm, sw), lambda i: (0, i, 0)), pl.BlockSpec((1, tm, sw), lambda i: (1, i, 0)),
                  pl.BlockSpec((tm, sw), lambda i: (i, ucol)), pl.BlockSpec((1, sw), lambda i: (0, 0)),
                  pl.BlockSpec((sw, sw), lambda i: (0, 0)), pl.BlockSpec((1, sw), lambda i: (0, 0))],
        out_specs=pl.BlockSpec((tm, sw), lambda i: (i, 0)),
        out_shape=jax.ShapeDtypeStruct((n, sw), F32), compiler_params=_cp("parallel"), name="s5_glu",
    )(y2, y2, proj, d_skip.reshape(1, sw), glu_w_bf16, glu_b.reshape(1, sw))


def _residual_and_next_norm(x_ref, gate_ref, y, g2_ref, sc2_ref, sh2_ref, o_ref, hn_ref):
    xn = x_ref[...] + gate_ref[0] * y
    o_ref[...] = xn
    hn_ref[...] = _normmod(xn, g2_ref[...], sc2_ref[0], sh2_ref[0]).astype(BF16)


def _row_specs(d, tm, rows_per_mod):
    mod = pl.BlockSpec((1, 1, d), lambda i: ((i * tm) // rows_per_mod, 0, 0))
    return [pl.BlockSpec((tm, d), lambda i: (i, 0)), mod, pl.BlockSpec((1, d), lambda i: (0, 0)), mod, mod]


def _row_outs(n, d, tm):
    spec = pl.BlockSpec((tm, d), lambda i: (i, 0))
    return [spec, spec], [jax.ShapeDtypeStruct((n, d), F32), jax.ShapeDtypeStruct((n, d), BF16)]


def _even_out_kernel(a_ref, b_ref, wa_ref, wb_ref, x_ref, gate_ref, g2_ref, sc2_ref, sh2_ref, o_ref, hn_ref):
    y = _dot(a_ref[...].astype(BF16), wa_ref[...]) + _dot(b_ref[...].astype(BF16), wb_ref[...])
    _residual_and_next_norm(x_ref, gate_ref, y, g2_ref, sc2_ref, sh2_ref, o_ref, hn_ref)


def even_out(hy, s5o, w_bf16, x, gate, g2, sc2, sh2, rows_per_mod, tm):
    n, d = x.shape
    hw = hy.shape[1]
    sw = s5o.shape[1]
    out_specs, out_shape = _row_outs(n, d, tm)
    return pl.pallas_call(
        _even_out_kernel, grid=(n // tm,),
        in_specs=[pl.BlockSpec((tm, hw), lambda i: (i, 0)), pl.BlockSpec((tm, sw), lambda i: (i, 0)),
                  pl.BlockSpec((hw, d), lambda i: (0, 0)), pl.BlockSpec((sw, d), lambda i: (hw // sw, 0))]
        + _row_specs(d, tm, rows_per_mod),
        out_specs=out_specs, out_shape=out_shape, compiler_params=_cp("parallel"), name="even_out",
    )(hy, s5o, w_bf16, w_bf16, x, gate, g2.reshape(1, d), sc2, sh2)


def _log_sigmoid(x):
    return jnp.minimum(x, 0.0) - jnp.log1p(jnp.exp(-jnp.abs(x)))


def _mlstm_kernel(q_ref, k_ref, v_ref, g_ref, gb_ref, c0_ref, n0_ref, m0_ref, h_ref, cf_ref, nf_ref, mf_ref,
                  c_s, m_s, *, nh, dh, tc, nc):
    d = pl.program_id(0)
    c = pl.program_id(2)

    @pl.when(c == 0)
    def _():
        for h in range(nh):
            c_s[h, :, :dh] = c0_ref[0, 0, h]
            c_s[h, :, dh:] = jnp.broadcast_to(n0_ref[0, 0, h:h + 1, :], (dh, dh)).T
        m_s[...] = m0_ref[0, 0]

    ones = jnp.ones((tc, dh), F32)
    gates = g_ref[...] + gb_ref[0]
    lane = lax.broadcasted_iota(jnp.int32, gates.shape, 1)
    logf = jnp.where(jnp.logical_and(lane >= nh, lane < 2 * nh), _log_sigmoid(gates), 0.0)
    r_i = lax.broadcasted_iota(jnp.int32, (tc, tc), 0)
    s_i = lax.broadcasted_iota(jnp.int32, (tc, tc), 1)
    causal = (r_i - s_i) * (1 - 2 * d) >= 0
    bcum = _dot(causal.astype(F32), logf, precision=HIGHEST)
    btot = jnp.sum(logf, axis=0, keepdims=True)
    gates_t = gates.T
    bcum_t = bcum.T
    for h in range(nh):
        hs = slice(h * dh, (h + 1) * dh)
        q = q_ref[:, hs]
        k = k_ref[:, hs]
        v = v_ref[:, hs]
        b_col = bcum[:, nh + h:nh + h + 1]
        b_row = bcum_t[nh + h:nh + h + 1, :]
        i_col = gates[:, h:h + 1]
        i_row = gates_t[h:h + 1, :]
        m = m_s[h:h + 1, :]
        a = b_col + m
        dmat = jnp.where(causal, b_col - b_row + i_row, NEG_INF)
        mq = jnp.maximum(a, jnp.max(dmat, axis=-1, keepdims=True))
        w_intra = jnp.exp(dmat - mq)
        w_inter = jnp.exp(a - mq)
        s = _dot_nt(q, k) * w_intra
        cn = c_s[h]
        v1 = jnp.concatenate([v, ones], axis=1)
        both = _dot(s, v1) + w_inter * _dot(q, cn)
        h_ref[0, :, hs] = both[:, :dh] / jnp.maximum(jnp.abs(both[:, dh:]), jnp.exp(-mq))
        b_last = btot[:, nh + h:nh + h + 1]
        g = b_last - b_col + i_col
        m_new = jnp.maximum(b_last + m, jnp.max(g, axis=0, keepdims=True))
        kw = k * jnp.exp(g - m_new)
        keep = jnp.exp(b_last + m - m_new)
        c_s[h] = keep * cn + _dot(kw.T, v1)
        m_s[h:h + 1, :] = m_new

    @pl.when(c == nc - 1)
    def _():
        for h in range(nh):
            cf_ref[0, 0, h] = c_s[h, :, :dh]
            nf_ref[0, 0, h:h + 1, :] = c_s[h, :, dh:].T[0:1, :]
        mf_ref[0, 0] = m_s[...]


def mlstm_scan(qk, proj, vcol, gate_bias, c0, n0, m0, nseq, seq_len, nh, dh):
    tc = ML_CHUNK
    nc = seq_len // tc
    w = nh * dh

    def chunk(d, c):
        return c + d * (nc - 1 - 2 * c)

    rowblk = lambda d, b, c: b * nc + chunk(d, c)
    st = lambda shp: pl.BlockSpec((1, 1) + shp, lambda d, b, c: (b, d) + (0,) * len(shp))
    return pl.pallas_call(
        functools.partial(_mlstm_kernel, nh=nh, dh=dh, tc=tc, nc=nc), grid=(2, nseq, nc),
        in_specs=[pl.BlockSpec((tc, w), lambda d, b, c: (rowblk(d, b, c), 0)),
                  pl.BlockSpec((tc, w), lambda d, b, c: (rowblk(d, b, c), 1)),
                  pl.BlockSpec((tc, w), lambda d, b, c: (rowblk(d, b, c), vcol)),
                  pl.BlockSpec((tc, LANES), lambda d, b, c: (rowblk(d, b, c), 4 * w // LANES + d)),
                  pl.BlockSpec((1, 1, LANES), lambda d, b, c: (d, 0, 0)),
                  st((nh, dh, dh)), st((nh, dh)), st((nh, 1))],
        out_specs=[pl.BlockSpec((1, tc, w), lambda d, b, c: (d, rowblk(d, b, c), 0)),
                   st((nh, dh, dh)), st((nh, dh)), st((nh, 1))],
        out_shape=[jax.ShapeDtypeStruct((2, nseq * seq_len, w), F32),
                   jax.ShapeDtypeStruct((nseq, 2, nh, dh, dh), F32), jax.ShapeDtypeStruct((nseq, 2, nh, dh), F32),
                   jax.ShapeDtypeStruct((nseq, 2, nh, 1), F32)],
        scratch_shapes=[pltpu.VMEM((nh, dh, 2 * dh), F32), pltpu.VMEM((nh, 1), F32)],
        compiler_params=_cp("parallel", "parallel", "arbitrary"), name="mlstm_scan",
    )(qk, qk, proj, proj, gate_bias, c0, n0, m0)


def _odd_out_kernel(hf_ref, hb_ref, og_ref, ng_ref, w_ref, x_ref, gate_ref, g2_ref, sc2_ref, sh2_ref, o_ref, hn_ref,
                    a_s, *, nh, dh):
    for h in range(nh):
        hs = slice(h * dh, (h + 1) * dh)
        blk = hf_ref[0, :, hs] + hb_ref[0, :, hs]
        blk = blk * lax.rsqrt(jnp.mean(blk * blk, axis=-1, keepdims=True) + EPS)
        a_s[:, hs] = ((blk * ng_ref[:, hs]) * _silu(og_ref[:, hs])).astype(BF16)
    _residual_and_next_norm(x_ref, gate_ref, _dot(a_s[...], w_ref[...]), g2_ref, sc2_ref, sh2_ref, o_ref, hn_ref)


def odd_out(h2, proj, ocol, norm_g, w_bf16, x, gate, g2, sc2, sh2, rows_per_mod, nh, dh, tm):
    n, d = x.shape
    w = nh * dh
    out_specs, out_shape = _row_outs(n, d, tm)
    return pl.pallas_call(
        functools.partial(_odd_out_kernel, nh=nh, dh=dh), grid=(n // tm,),
        in_specs=[pl.BlockSpec((1, tm, w), lambda i: (0, i, 0)), pl.BlockSpec((1, tm, w), lambda i: (1, i, 0)),
                  pl.BlockSpec((tm, w), lambda i: (i, ocol)), pl.BlockSpec((1, w), lambda i: (0, 0)),
                  pl.BlockSpec((w, d), lambda i: (0, 0))] + _row_specs(d, tm, rows_per_mod),
        out_specs=out_specs, out_shape=out_shape, scratch_shapes=[pltpu.VMEM((tm, w), BF16)],
        compiler_params=_cp("parallel"), name="odd_out",
    )(h2, h2, proj, norm_g.reshape(1, w), w_bf16, x, gate, g2.reshape(1, d), sc2, sh2)


def _top_values(curs, k, outs):
    curs = list(curs)
    for j in range(k):
        for a, out_s in enumerate(outs):
            m = jnp.max(curs[a], axis=0, keepdims=True)
            out_s[j:j + 1, :] = m
            curs[a] = jnp.where(curs[a] == m, NEG_INF, curs[a])


def _pair_candidates(k):
    return [(a, k // (a + 1)) for a in range(k)]


PEER_HEADS_PER_TRIP = 8


def _peer_score_kernel(h_ref, wq_ref, keys_ref, th_ref, s2_ref, w1_ref, w2_ref, q_s, v_s, cand_s, best_s, *,
                       nh, half, topk):
    q_s[...] = _dot_nt(wq_ref[...], h_ref[...])
    kk = topk + 1
    cand_s[...] = jnp.full(cand_s.shape, NEG_INF, F32)
    group = range(PEER_HEADS_PER_TRIP)

    def heads(trip, carry):
        hds = [trip * PEER_HEADS_PER_TRIP + u for u in group]
        scores = []
        for hd in hds:
            base = pl.multiple_of(hd * 2 * half, 2 * half)
            scores.append(_dot(keys_ref[hd, 0], q_s[pl.ds(base, half), :]))
            scores.append(_dot(keys_ref[hd, 1], q_s[pl.ds(base + half, half), :]))
        _top_values(scores, kk, [v_s.at[u, c] for u in group for c in range(2)])
        for u in group:
            off = 0
            for a, cnt in _pair_candidates(kk):
                cand_s[u, off:off + cnt, :] = v_s[u, 0, a:a + 1, :] + v_s[u, 1, 0:cnt, :]
                off += cnt
        _top_values([cand_s[u] for u in group], kk, [best_s.at[u] for u in group])
        for u, hd in enumerate(hds):
            s1, s2 = scores[2 * u], scores[2 * u + 1]
            best = best_s[u, 0:topk, :]
            z = jnp.sum(jnp.exp(best - best[0:1, :]), axis=0, keepdims=True)
            tmid = 0.5 * (best_s[u, topk - 1:topk, :] + best_s[u, topk:topk + 1, :])
            th = tmid - s1
            w1 = jnp.exp(s1 - v_s[u, 0, 0:1, :]) / z
            w2 = jnp.exp(s2 - v_s[u, 1, 0:1, :])
            for lt in range(s1.shape[1] // LANES):
                sl = slice(lt * LANES, (lt + 1) * LANES)
                th_ref[hd, lt] = th[:, sl]
                s2_ref[hd, lt] = s2[:, sl]
                w1_ref[hd, lt] = w1[:, sl]
                w2_ref[hd, lt] = w2[:, sl]
        return carry

    lax.fori_loop(0, nh // PEER_HEADS_PER_TRIP, heads, 0)


def peer_scores(h_bf16, wq_t_bf16, keys, tt):
    n, d = h_bf16.shape
    nh, _, nk, half = keys.shape
    kk = PK_TOPK + 1
    ncand = -(-sum(c for _, c in _pair_candidates(kk)) // 8) * 8
    big = jax.ShapeDtypeStruct((nh, n // LANES, nk, LANES), F32)
    bspec = pl.BlockSpec((nh, tt // LANES, nk, LANES), lambda i: (0, i, 0, 0))
    return pl.pallas_call(
        functools.partial(_peer_score_kernel, nh=nh, half=half, topk=PK_TOPK), grid=(n // tt,),
        in_specs=[pl.BlockSpec((tt, d), lambda i: (i, 0)), pl.BlockSpec((nh * 2 * half, d), lambda i: (0, 0)),
                  pl.BlockSpec((nh, 2, nk, half), lambda i: (0, 0, 0, 0))],
        out_specs=[bspec, bspec, bspec, bspec],
        out_shape=[big, big, big, big],
        scratch_shapes=[pltpu.VMEM((nh * 2 * half, tt), F32), pltpu.VMEM((PEER_HEADS_PER_TRIP, 2, 24, tt), F32),
                        pltpu.VMEM((PEER_HEADS_PER_TRIP, ncand, tt), F32), pltpu.VMEM((PEER_HEADS_PER_TRIP, 24, tt), F32)],
        compiler_params=_cp("parallel"), name="peer_scores",
    )(h_bf16, wq_t_bf16, keys)


PEER_KEY_ROWS = 16
GELU_C1 = math.sqrt(2.0 / math.pi)
GELU_C2 = 0.044715 * GELU_C1


def _gelu_tanh(x):
    half_x = 0.5 * x
    return half_x + half_x * jnp.tanh(x * (GELU_C1 + GELU_C2 * (x * x)))


def _peer_dense_kernel(h_ref, u_ref, vt_ref, th_ref, s2_ref, w1_ref, w2_ref, x_ref, gate_ref, fg_ref, o_ref,
                       acc_s, st_s, wt_s, *, nh, nk, ec, tt, final):
    e = pl.program_id(1)
    nlt = tt // LANES
    n_i1 = ec // nk
    nkt = nk // PEER_KEY_ROWS
    nsub = PEER_KEY_ROWS // SUBLANES

    @pl.when(e == 0)
    def _():
        acc_s[...] = jnp.zeros_like(acc_s)

    st = _gelu_tanh(_dot_nt(u_ref[...], h_ref[...]))
    for lt in range(nlt):
        st_s[lt] = st[:, lt * LANES:(lt + 1) * LANES]

    def tile(idx, carry):
        lt = idx // nkt
        k0 = (idx % nkt) * PEER_KEY_ROWS
        subs = [pl.ds(pl.multiple_of(k0 + j * SUBLANES, SUBLANES), SUBLANES) for j in range(nsub)]
        g = [[jnp.zeros((SUBLANES, LANES), F32) for _ in subs] for _ in range(n_i1)]
        for hd in range(nh):
            s2t = [s2_ref[hd, lt, sub, :] for sub in subs]
            w2t = [w2_ref[hd, lt, sub, :] for sub in subs]
            for li in range(n_i1):
                thb = jnp.broadcast_to(th_ref[hd, lt, li:li + 1, :], (SUBLANES, LANES))
                w1b = jnp.broadcast_to(w1_ref[hd, lt, li:li + 1, :], (SUBLANES, LANES))
                for j in range(nsub):
                    g[li][j] = g[li][j] + jnp.where(s2t[j] >= thb, w2t[j] * w1b, 0.0)
        for li in range(n_i1):
            rows = pl.ds(pl.multiple_of(li * nk + k0, PEER_KEY_ROWS), PEER_KEY_ROWS)
            wt_s[lt, rows, :] = (st_s[lt, rows, :] * jnp.concatenate(g[li], axis=0)).astype(BF16)
        return carry

    lax.fori_loop(0, nlt * nkt, tile, 0)
    wt = jnp.concatenate([wt_s[lt] for lt in range(nlt)], axis=1)
    acc_s[...] += _dot(vt_ref[0], wt)

    @pl.when(e == pl.num_programs(1) - 1)
    def _():
        xn = x_ref[...] + gate_ref[0] * acc_s[...].T
        if final:
            xn = (xn * lax.rsqrt(jnp.mean(xn * xn, axis=-1, keepdims=True) + EPS)) * fg_ref[...]
        o_ref[...] = xn


def peer_dense(h_bf16, u_bf16, vt_bf16, th, s2, w1, w2, x, gate, final_g, final, rows_per_mod, tt, ec):
    n, d = x.shape
    nh, _, nk, _ = s2.shape
    nchunk = u_bf16.shape[0] // ec
    bspec = pl.BlockSpec((nh, tt // LANES, nk, LANES), lambda i, e: (0, i, 0, 0))
    rspec = pl.BlockSpec((nh, tt // LANES, ec // nk, LANES), lambda i, e: (0, i, e, 0))
    tile_buf = (tt // LANES, ec, LANES)
    return pl.pallas_call(
        functools.partial(_peer_dense_kernel, nh=nh, nk=nk, ec=ec, tt=tt, final=final), grid=(n // tt, nchunk),
        in_specs=[pl.BlockSpec((tt, d), lambda i, e: (i, 0)), pl.BlockSpec((ec, d), lambda i, e: (e, 0)),
                  pl.BlockSpec((1, d, ec), lambda i, e: (e, 0, 0)), rspec, bspec, rspec, bspec,
                  pl.BlockSpec((tt, d), lambda i, e: (i, 0)),
                  pl.BlockSpec((1, 1, d), lambda i, e: ((i * tt) // rows_per_mod, 0, 0)),
                  pl.BlockSpec((1, d), lambda i, e: (0, 0))],
        out_specs=pl.BlockSpec((tt, d), lambda i, e: (i, 0)),
        out_shape=jax.ShapeDtypeStruct((n, d), F32),
        scratch_shapes=[pltpu.VMEM((d, tt), F32), pltpu.VMEM(tile_buf, F32), pltpu.VMEM(tile_buf, BF16)],
        compiler_params=_cp("parallel", "arbitrary"), name="peer_dense",
    )(h_bf16, u_bf16, vt_bf16, th, s2, w1, w2, x, gate, final_g.reshape(1, d))


def _s5_params(a_re, a_im, b_re, b_im, c_re, c_im, log_step):
    lam = lax.complex(a_re.astype(F32), a_im.astype(F32))
    lam_bar = jnp.exp(lam * jnp.exp(log_step.astype(F32))[..., None])
    b_bar = ((lam_bar - 1.0) / lam)[..., None] * lax.complex(b_re.astype(F32), b_im.astype(F32))
    ngrp, npst, nch = b_bar.shape[1:]
    eye = jnp.eye(ngrp, dtype=F32)

    def b_mat(part):
        return jnp.einsum("dgpj,gh->dgjhp", part, eye).reshape(2, ngrp * nch, ngrp * npst)

    def c_mat(part):
        return jnp.einsum("dgjp,gh->dgphj", part, eye).reshape(2, ngrp * npst, ngrp * nch)

    bre, bim = b_mat(b_bar.real).astype(BF16), b_mat(b_bar.imag).astype(BF16)
    cre, cim = c_mat(c_re.astype(F32)).astype(BF16), c_mat(-c_im.astype(F32)).astype(BF16)
    lam2 = jnp.stack([lam_bar.real.reshape(2, -1), lam_bar.imag.reshape(2, -1)], axis=1)
    return bre, bim, cre, cim, lam2


def _pos_embed(n_tok, d, grid_w):
    rows = n_tok // grid_w
    quarter = d // 4
    omega = 1.0 / (10000.0 ** (jnp.arange(quarter, dtype=F32) / quarter))

    def emb1d(pos):
        ang = pos.astype(F32)[:, None] * omega[None]
        return jnp.concatenate([jnp.sin(ang), jnp.cos(ang)], axis=-1)

    er = emb1d(jnp.arange(rows))
    ec = emb1d(jnp.arange(grid_w))
    half = d // 2
    pe = jnp.concatenate([jnp.broadcast_to(er[:, None], (rows, grid_w, half)),
                          jnp.broadcast_to(ec[None], (rows, grid_w, half))], axis=-1)
    return pe.reshape(rows * grid_w, d)


def _tile(n, pref):
    return pref if n % pref == 0 else n


def _trunk(x, mods, s5_h0, ml_c0, ml_n0, ml_m0, p, nseq, seq_len, rows_per_mod):
    n, d = x.shape
    tm = _tile(min(rows_per_mod, n), 512)
    depth = p["norm_g"].shape[0]
    s5_fin, ml_fin = [], []
    for l in range(depth):
        sh1, sc1, g1, sh2, sc2, g2 = mods[l]
        i = l // 2
        if l % 2 == 0:
            hw = p["hy_bias"].shape[2]
            sw = p["s5_d"].shape[1]
            proj = normmod_matmul(x, p["norm_g"][l, 0], sc1, sh1, p["ev_w_in"][i].astype(BF16), rows_per_mod, tm,
                                  3 * hw + sw)
            hy_in = short_conv(proj, 3 * hw, p["hy_conv_w"][i], p["hy_conv_b"][i], jnp.ones((3 * hw,), F32),
                               seq_len, act=False, out_dtype=F32)
            cos_t, a_t, a_tt = dft_tables(seq_len)
            tf = _tile(seq_len, 512)
            taps, sumsq = hyena_filter_taps(seq_len, p["hy_w1"][i], p["hy_b1"][i], p["hy_w2"][i], p["hy_b2"][i],
                                            p["hy_w3"][i], p["hy_freq"][i], p["hy_decay"][i], hw)
            kr, ki = hyena_filter_spectrum(cos_t, a_t, taps, sumsq, hw, tf)
            bias = p["hy_bias"][i].astype(F32)
            z, zcol = hy_in, 0
            for o in range(bias.shape[0]):
                yr, yi = hyena_fwd(cos_t, a_t, z, zcol, kr, ki, o, nseq, hw, tf)
                z = hyena_inv(cos_t, a_tt, yr, yi, z, zcol, hy_in, 1 + o, bias[o:o + 1], nseq, hw, tf)
                zcol = 0
            bre, bim, cre, cim, lam2 = _s5_params(p["s5_a_re"][i], p["s5_a_im"][i], p["s5_b_re"][i], p["s5_b_im"][i],
                                                  p["s5_c_re"][i], p["s5_c_im"][i], p["s5_log_step"][i])
            ucol = 3 * hw // sw
            y2, hfin = s5_scan(proj, ucol, bre, bim, cre, cim, lam2, s5_h0[i], nseq, seq_len, sw, _tile(seq_len, 256))
            s5_fin.append(hfin)
            s5o = s5_glu(y2, proj, ucol, p["s5_d"][i], p["s5_glu_w"][i].astype(BF16), p["s5_glu_b"][i], tm)
            x, hn = even_out(z, s5o, p["ev_w_out"][i].astype(BF16), x, g1, p["norm_g"][l, 1], sc2, sh2, rows_per_mod, tm)
        else:
            nh = p["od_gate_b"].shape[2]
            w = p["ml_norm_g"].shape[1]
            dh = w // nh
            w_in = p["od_w_in"][i]
            wg = w_in[:, 4 * w:].reshape(d, 4, nh)
            gb = p["od_gate_b"][i].astype(F32)
            wg2 = jnp.zeros((d, 2, LANES), w_in.dtype)
            bg2 = jnp.zeros((2, 1, LANES), F32)
            for dr in range(2):
                wg2 = wg2.at[:, dr, :nh].set(wg[:, dr]).at[:, dr, nh:2 * nh].set(wg[:, 2 + dr])
                bg2 = bg2.at[dr, 0, :nh].set(gb[dr]).at[dr, 0, nh:2 * nh].set(gb[2 + dr])
            w_all = jnp.concatenate([w_in[:, :4 * w], wg2.reshape(d, 2 * LANES)], axis=1).astype(BF16)
            proj = normmod_matmul(x, p["norm_g"][l, 0], sc1, sh1, w_all, rows_per_mod, tm, w_all.shape[1] // 2)
            qscale = jnp.concatenate([jnp.full((w,), dh ** -0.5, F32), jnp.ones((w,), F32)])
            qk = short_conv(proj, 2 * w, p["ml_conv_w"][i], p["ml_conv_b"][i], qscale, seq_len, act=True, out_dtype=BF16)
            h2, cf, nf, mf = mlstm_scan(qk, proj, 2, bg2, ml_c0[i], ml_n0[i], ml_m0[i], nseq, seq_len, nh, dh)
            ml_fin.append((cf, nf, mf))
            x, hn = odd_out(h2, proj, 3, p["ml_norm_g"][i], p["od_w_out"][i].astype(BF16), x, g1, p["norm_g"][l, 1],
                            sc2, sh2, rows_per_mod, nh, dh, tm)
        tt = _tile(min(rows_per_mod, n), 512)
        th, s2, w1, w2 = peer_scores(hn, p["pk_w_q"][l].T.astype(BF16), p["pk_keys"][l].astype(F32), _tile(tt, 256))
        ec = 2048
        vt = p["pk_v"][l].astype(BF16).reshape(-1, ec, d).transpose(0, 2, 1)
        x = peer_dense(hn, p["pk_u"][l].astype(BF16), vt, th, s2, w1, w2, x, g2, p["final_g"], l == depth - 1,
                       rows_per_mod, tt, ec)
    return x, s5_fin, ml_fin


def kernel(x_prompt, x_sample, state_s5_re, state_s5_im, state_mlstm_C, state_mlstm_n, state_mlstm_m, c, c_ctx, norm_g, ada_w, ada_b, final_g, ev_w_in, hy_conv_w, hy_conv_b, hy_w1, hy_b1, hy_w2, hy_b2, hy_w3, hy_freq, hy_decay, hy_bias, s5_a_re, s5_a_im, s5_b_re, s5_b_im, s5_c_re, s5_c_im, s5_log_step, s5_d, s5_glu_w, s5_glu_b, ev_w_out, od_w_in, od_gate_b, ml_conv_w, ml_conv_b, ml_norm_g, od_w_out, pk_w_q, pk_keys, pk_u, pk_v):
    p = dict(norm_g=norm_g, ada_w=ada_w, ada_b=ada_b, final_g=final_g, ev_w_in=ev_w_in,
             hy_conv_w=hy_conv_w, hy_conv_b=hy_conv_b, hy_w1=hy_w1, hy_b1=hy_b1, hy_w2=hy_w2, hy_b2=hy_b2,
             hy_w3=hy_w3, hy_freq=hy_freq, hy_decay=hy_decay, hy_bias=hy_bias, s5_a_re=s5_a_re,
             s5_a_im=s5_a_im, s5_b_re=s5_b_re, s5_b_im=s5_b_im, s5_c_re=s5_c_re, s5_c_im=s5_c_im,
             s5_log_step=s5_log_step, s5_d=s5_d, s5_glu_w=s5_glu_w, s5_glu_b=s5_glu_b, ev_w_out=ev_w_out,
             od_w_in=od_w_in, od_gate_b=od_gate_b, ml_conv_w=ml_conv_w, ml_conv_b=ml_conv_b,
             ml_norm_g=ml_norm_g, od_w_out=od_w_out, pk_w_q=pk_w_q, pk_keys=pk_keys, pk_u=pk_u, pk_v=pk_v)
    nb, seq, d = x_prompt.shape
    db, dseq, _ = x_sample.shape
    depth = norm_g.shape[0]
    n_even, n_odd = (depth + 1) // 2, depth // 2
    assert db + 1 <= 8

    cond8 = jnp.zeros((8, d), F32).at[0].set(c_ctx.astype(F32)).at[1:1 + db].set(c.astype(F32))
    mods_ctx, mods_lat = [], []
    for l in range(depth):
        mod = ada_mod(cond8, ada_w[l].astype(F32), ada_b[l].astype(F32))
        chunks = [mod[:, j * d:(j + 1) * d] for j in range(6)]
        mods_ctx.append([ch[0:1].reshape(1, 1, d) for ch in chunks])
        mods_lat.append([ch[1:1 + db].reshape(db, 1, d) for ch in chunks])

    def s5_state(re, im, bsz):
        return [jnp.stack([re[:, i].reshape(bsz, 2, -1), im[:, i].reshape(bsz, 2, -1)], axis=2).astype(F32)
                for i in range(n_even)]

    ngrp, npst = s5_a_re.shape[2], s5_a_re.shape[3]
    nh, dh = state_mlstm_C.shape[3], state_mlstm_C.shape[4]
    zeros_s5 = jnp.zeros((nb, n_even, 2, ngrp, npst), F32)
    y_prompt, s5_fin, ml_fin = _trunk(
        x_prompt.reshape(nb * seq, d), mods_ctx, s5_state(zeros_s5, zeros_s5, nb),
        [jnp.zeros((nb, 2, nh, dh, dh), F32)] * n_odd, [jnp.zeros((nb, 2, nh, dh), F32)] * n_odd,
        [jnp.zeros((nb, 2, nh, 1), F32)] * n_odd, p, nb, seq, nb * seq)
    x_lat = add_pos(x_sample.reshape(db * dseq, d), _pos_embed(dseq, d, GRID_W), dseq, _tile(dseq, 512))
    y_sample, _, _ = _trunk(
        x_lat, mods_lat, s5_state(state_s5_re, state_s5_im, db),
        [state_mlstm_C[:, i].astype(F32) for i in range(n_odd)], [state_mlstm_n[:, i].astype(F32) for i in range(n_odd)],
        [state_mlstm_m[:, i].astype(F32)[..., None] for i in range(n_odd)], p, db, dseq, dseq)

    new_s5_re = jnp.stack([h[:, :, 0].reshape(nb, 2, ngrp, npst) for h in s5_fin], axis=1)
    new_s5_im = jnp.stack([h[:, :, 1].reshape(nb, 2, ngrp, npst) for h in s5_fin], axis=1)
    new_c = jnp.stack([f[0] for f in ml_fin], axis=1)
    new_n = jnp.stack([f[1] for f in ml_fin], axis=1)
    new_m = jnp.stack([f[2][..., 0] for f in ml_fin], axis=1)
    return (y_prompt.reshape(nb, seq, d), y_sample.reshape(db, dseq, d), new_s5_re, new_s5_im, new_c, new_n, new_m)
```

```python
import functools
import math

import jax
import jax.numpy as jnp
from jax import lax
from jax.experimental import pallas as pl
from jax.experimental.pallas import tpu as pltpu

F32 = jnp.float32
BF16 = jnp.bfloat16
EPS = 1e-6
HIGHEST = lax.Precision.HIGHEST
V7X_VMEM_LIMIT_BYTES = 56 * 1024 * 1024
LANES = 128
SUBLANES = 8
ML_CHUNK = 128
PK_TOPK = 16
GRID_W = 64
NEG_INF = float("-inf")


def _cp(*sem):
    return pltpu.CompilerParams(dimension_semantics=sem, vmem_limit_bytes=V7X_VMEM_LIMIT_BYTES)


def _dot(a, b, **kw):
    return jnp.dot(a, b, preferred_element_type=F32, **kw)


def _dot_nt(a, b):
    return lax.dot_general(a, b, (((1,), (1,)), ((), ())), preferred_element_type=F32)


def _silu(x):
    return x * jax.nn.sigmoid(x)


def _ada_kernel(c_ref, w_ref, b_ref, o_ref):
    o_ref[...] = _dot(_silu(c_ref[...]), w_ref[...], precision=HIGHEST) + b_ref[...]


def ada_mod(cond8, w, b):
    d, no = w.shape
    tn = 1536 if no % 1536 == 0 else no
    return pl.pallas_call(
        _ada_kernel, grid=(no // tn,),
        in_specs=[pl.BlockSpec((8, d), lambda j: (0, 0)), pl.BlockSpec((d, tn), lambda j: (0, j)),
                  pl.BlockSpec((1, tn), lambda j: (0, j))],
        out_specs=pl.BlockSpec((8, tn), lambda j: (0, j)),
        out_shape=jax.ShapeDtypeStruct((8, no), F32), compiler_params=_cp("parallel"), name="ada_mod",
    )(cond8, w, b.reshape(1, no))


def _normmod(x, g, sc, sh):
    y = x * lax.rsqrt(jnp.mean(x * x, axis=-1, keepdims=True) + EPS)
    return (y * g) * (1.0 + sc) + sh


def _mod_spec(d, tm, rows_per_mod):
    return pl.BlockSpec((1, 1, d), lambda i, j: ((i * tm) // rows_per_mod, 0, 0))


def _nm_matmul_kernel(x_ref, g_ref, sc_ref, sh_ref, w_ref, o_ref, h_ref):
    @pl.when(pl.program_id(1) == 0)
    def _():
        h_ref[...] = _normmod(x_ref[...], g_ref[...], sc_ref[0], sh_ref[0]).astype(BF16)
    o_ref[...] = _dot(h_ref[...], w_ref[...])


def normmod_matmul(x, g, sc, sh, w_bf16, rows_per_mod, tm, tn):
    n, d = x.shape
    no = w_bf16.shape[1]
    return pl.pallas_call(
        _nm_matmul_kernel, grid=(n // tm, no // tn),
        in_specs=[pl.BlockSpec((tm, d), lambda i, j: (i, 0)), pl.BlockSpec((1, d), lambda i, j: (0, 0)),
                  _mod_spec(d, tm, rows_per_mod), _mod_spec(d, tm, rows_per_mod),
                  pl.BlockSpec((d, tn), lambda i, j: (0, j))],
        out_specs=pl.BlockSpec((tm, tn), lambda i, j: (i, j)),
        out_shape=jax.ShapeDtypeStruct((n, no), F32),
        scratch_shapes=[pltpu.VMEM((tm, d), BF16)],
        compiler_params=_cp("parallel", "arbitrary"), name="normmod_matmul",
    )(x, g.reshape(1, d), sc, sh, w_bf16)


def _add_rows_kernel(x_ref, p_ref, o_ref):
    o_ref[...] = x_ref[...] + p_ref[...]


def add_pos(x, pe, seq_len, tm):
    n, d = x.shape
    nb = seq_len // tm
    return pl.pallas_call(
        _add_rows_kernel, grid=(n // tm,),
        in_specs=[pl.BlockSpec((tm, d), lambda i: (i, 0)), pl.BlockSpec((tm, d), lambda i: (i % nb, 0))],
        out_specs=pl.BlockSpec((tm, d), lambda i: (i, 0)),
        out_shape=jax.ShapeDtypeStruct((n, d), F32), compiler_params=_cp("parallel"), name="add_pos",
    )(x, pe)


def _sconv_kernel(x_ref, w_ref, b_ref, s_ref, o_ref, *, act):
    x = x_ref[...]
    n_tok = x.shape[0]
    row = lax.broadcasted_iota(jnp.int32, x.shape, 0)
    prev = jnp.where(row == 0, 0.0, pltpu.roll(x, 1, 0))
    nxt = jnp.where(row == n_tok - 1, 0.0, pltpu.roll(x, n_tok - 1, 0))
    y = prev * w_ref[0:1, :] + x * w_ref[1:2, :] + nxt * w_ref[2:3, :] + b_ref[...]
    if act:
        y = _silu(y) * s_ref[...]
    o_ref[...] = y.astype(o_ref.dtype)


def short_conv(a, ncols, w, b, scale, seq_len, act, out_dtype, cb=256):
    n = a.shape[0]
    return pl.pallas_call(
        functools.partial(_sconv_kernel, act=act), grid=(n // seq_len, ncols // cb),
        in_specs=[pl.BlockSpec((seq_len, cb), lambda s, j: (s, j)), pl.BlockSpec((3, cb), lambda s, j: (0, j)),
                  pl.BlockSpec((1, cb), lambda s, j: (0, j)), pl.BlockSpec((1, cb), lambda s, j: (0, j))],
        out_specs=pl.BlockSpec((seq_len, cb), lambda s, j: (s, j)),
        out_shape=jax.ShapeDtypeStruct((n, ncols), out_dtype), compiler_params=_cp("parallel", "parallel"),
        name="short_conv",
    )(a, w, b.reshape(1, ncols), scale.reshape(1, ncols))


def dft_tables(n_tok):
    k = jnp.arange(n_tok, dtype=jnp.int32)
    blk = 1 << ((n_tok.bit_length() - 1) // 2)
    def thin(n):
        ang = ((k[:, None] * n[None, :]) % (2 * n_tok)).astype(F32) * (math.pi / n_tok)
        return jnp.cos(ang), jnp.sin(ang)
    (c_hi, s_hi), (c_lo, s_lo) = thin(jnp.arange(0, n_tok, blk, dtype=jnp.int32)), thin(jnp.arange(blk, dtype=jnp.int32))
    cos_t = (c_hi[:, :, None] * c_lo[:, None, :] - s_hi[:, :, None] * s_lo[:, None, :]).reshape(n_tok, n_tok)
    msin = -(s_hi[:, :, None] * c_lo[:, None, :] + c_hi[:, :, None] * s_lo[:, None, :]).reshape(n_tok, n_tok)
    alt = jnp.where(k % 2 == 0, 1.0, -1.0).astype(F32)
    a_t = msin.at[0, :].set(alt)
    a_tt = msin.at[:, 0].set(alt)
    return cos_t.astype(BF16), a_t.astype(BF16), a_tt.astype(BF16)


def _hyfilt_kernel(band_ref, w1_ref, b1_ref, w2_ref, b2_ref, w3_ref, fr_ref, dec_ref, h_ref, ss_ref, *,
                   n_tok, tl, hw, nbands):
    i = pl.program_id(0)
    pos = i * tl + lax.broadcasted_iota(jnp.int32, (tl, 1), 0)
    t = pos.astype(F32) / n_tok
    lane = lax.broadcasted_iota(jnp.int32, (tl, LANES), 1)
    ang = 2.0 * math.pi * t * band_ref[...]
    z = jnp.where(lane == 0, t, jnp.where(lane <= nbands, jnp.cos(ang),
                                          jnp.where(lane <= 2 * nbands, jnp.sin(ang), 0.0)))
    fr = fr_ref[...]
    h = jnp.sin(fr * (_dot(z, w1_ref[...], precision=HIGHEST) + b1_ref[...]))
    h = jnp.sin(fr * (_dot(h, w2_ref[...], precision=HIGHEST) + b2_ref[...]))
    h = _dot(h, w3_ref[...], precision=HIGHEST) * jnp.exp(-t * jnp.abs(dec_ref[...]))
    col = lax.broadcasted_iota(jnp.int32, h.shape, 1)
    is_bwd = (col // hw) % 2 == 1
    h = jnp.where(jnp.logical_and(is_bwd, pos == 0), 0.0, h)
    h_ref[...] = h.astype(BF16)

    @pl.when(i == 0)
    def _():
        ss_ref[...] = jnp.zeros_like(ss_ref)
    ss_ref[...] += jnp.sum(h * h, axis=0, keepdims=True)


def hyena_filter_taps(n_tok, w1, b1, w2, b2, w3, freq, decay, hw):
    emb, ffn = w1.shape
    nbands = (emb - 1) // 2
    tl = min(n_tok, 512)
    bands = jnp.linspace(1e-4, nbands - 1, nbands, dtype=F32)
    band_row = jnp.zeros((1, LANES), F32).at[0, 1:1 + nbands].set(bands).at[0, 1 + nbands:1 + 2 * nbands].set(bands)
    w1p = jnp.zeros((LANES, ffn), F32).at[:emb].set(w1)
    nc = w3.shape[1]
    full = lambda shp: pl.BlockSpec(shp, lambda i: (0, 0))
    return pl.pallas_call(
        functools.partial(_hyfilt_kernel, n_tok=n_tok, tl=tl, hw=hw, nbands=nbands), grid=(n_tok // tl,),
        in_specs=[full((1, LANES)), full((LANES, ffn)), full((1, ffn)), full((ffn, ffn)), full((1, ffn)),
                  full((ffn, nc)), full((1, ffn)), full((1, nc))],
        out_specs=[pl.BlockSpec((tl, nc), lambda i: (i, 0)), full((1, nc))],
        out_shape=[jax.ShapeDtypeStruct((n_tok, nc), BF16), jax.ShapeDtypeStruct((1, nc), F32)],
        compiler_params=_cp("arbitrary"), name="hyena_filter_taps",
    )(band_row, w1p, b1.reshape(1, ffn), w2, b2.reshape(1, ffn), w3, freq.reshape(1, ffn), decay.reshape(1, nc))


def _filt_dft_kernel(c_ref, a_ref, h_ref, ss_ref, kr_ref, ki_ref, *, tf, hw):
    i = pl.program_id(1)
    hf = h_ref[:, :hw]
    hb = h_ref[:, hw:]
    cc = c_ref[...]
    aa = a_ref[...]
    zrf, zif, zrb, zib = _dot(cc, hf), _dot(aa, hf), _dot(cc, hb), _dot(aa, hb)
    scale = lax.rsqrt(ss_ref[:, :hw] + ss_ref[:, hw:] + EPS)
    first = (i * tf + lax.broadcasted_iota(jnp.int32, (tf, 1), 0)) == 0
    scale = scale * jnp.where(first, 0.5, 1.0)
    kr_ref[0] = (zrf + zrb) * scale
    ki_ref[0] = jnp.where(first, zif + zib, zif - zib) * scale


def hyena_filter_spectrum(cos_t, a_t, taps, sumsq, hw, tf):
    n_tok = cos_t.shape[0]
    norder = taps.shape[1] // (2 * hw)
    out = jax.ShapeDtypeStruct((norder, n_tok, hw), F32)
    return pl.pallas_call(
        functools.partial(_filt_dft_kernel, tf=tf, hw=hw), grid=(norder, n_tok // tf),
        in_specs=[pl.BlockSpec((tf, n_tok), lambda o, i: (i, 0)), pl.BlockSpec((tf, n_tok), lambda o, i: (i, 0)),
                  pl.BlockSpec((n_tok, 2 * hw), lambda o, i: (0, o)), pl.BlockSpec((1, 2 * hw), lambda o, i: (0, o))],
        out_specs=[pl.BlockSpec((1, tf, hw), lambda o, i: (o, i, 0))] * 2,
        out_shape=[out, out], compiler_params=_cp("parallel", "parallel"), name="hyena_filter_spectrum",
    )(cos_t, a_t, taps, sumsq)


def _hy_fwd_kernel(c_ref, a_ref, z_ref, kr_ref, ki_ref, yr_ref, yi_ref, *, tf):
    i = pl.program_id(0)
    zb = z_ref[...].astype(BF16)
    zr = _dot(c_ref[...], zb)
    zi = _dot(a_ref[...], zb)
    kr = kr_ref[0]
    ki = ki_ref[0]
    first = (i * tf + lax.broadcasted_iota(jnp.int32, (tf, 1), 0)) == 0
    yr_ref[...] = jnp.where(first, zr * kr, zr * kr - zi * ki).astype(BF16)
    yi_ref[...] = jnp.where(first, zi * ki, zr * ki + zi * kr).astype(BF16)


def hyena_fwd(cos_t, a_t, z, zcol, kr, ki, order, nseq, hw, tf):
    n_tok = cos_t.shape[0]
    nf = n_tok // tf
    out = jax.ShapeDtypeStruct((nseq * n_tok, hw), BF16)
    return pl.pallas_call(
        functools.partial(_hy_fwd_kernel, tf=tf), grid=(nf, nseq),
        in_specs=[pl.BlockSpec((tf, n_tok), lambda i, b: (i, 0)), pl.BlockSpec((tf, n_tok), lambda i, b: (i, 0)),
                  pl.BlockSpec((n_tok, hw), lambda i, b: (b, zcol)),
                  pl.BlockSpec((1, tf, hw), lambda i, b: (order, i, 0)),
                  pl.BlockSpec((1, tf, hw), lambda i, b: (order, i, 0))],
        out_specs=[pl.BlockSpec((tf, hw), lambda i, b: (b * nf + i, 0))] * 2,
        out_shape=[out, out], compiler_params=_cp("parallel", "parallel"), name="hyena_fwd",
    )(cos_t, a_t, z, kr, ki)


def _hy_inv_kernel(c_ref, at_ref, yr_ref, yi_ref, zp_ref, gate_ref, bias_ref, o_ref, *, inv_len):
    conv = (_dot(c_ref[...], yr_ref[...]) + _dot(at_ref[...], yi_ref[...])) * inv_len
    o_ref[...] = gate_ref[...] * (conv + bias_ref[...] * zp_ref[...])


def hyena_inv(cos_t, a_tt, yr, yi, zprev, zcol, gates, gcol, bias_row, nseq, hw, tf):
    n_tok = cos_t.shape[0]
    nf = n_tok // tf
    return pl.pallas_call(
        functools.partial(_hy_inv_kernel, inv_len=1.0 / n_tok), grid=(nf, nseq),
        in_specs=[pl.BlockSpec((tf, n_tok), lambda i, b: (i, 0)), pl.BlockSpec((tf, n_tok), lambda i, b: (i, 0)),
                  pl.BlockSpec((n_tok, hw), lambda i, b: (b, 0)), pl.BlockSpec((n_tok, hw), lambda i, b: (b, 0)),
                  pl.BlockSpec((tf, hw), lambda i, b: (b * nf + i, zcol)),
                  pl.BlockSpec((tf, hw), lambda i, b: (b * nf + i, gcol)),
                  pl.BlockSpec((1, hw), lambda i, b: (0, 0))],
        out_specs=pl.BlockSpec((tf, hw), lambda i, b: (b * nf + i, 0)),
        out_shape=jax.ShapeDtypeStruct((nseq * n_tok, hw), F32),
        compiler_params=_cp("parallel", "parallel"), name="hyena_inv",
    )(cos_t, a_tt, yr, yi, zprev, gates, bias_row)


S5_DIAG_BLOCKS = 2


S5_SEQS_PER_STEP = 4


def _s5_kernel(u_ref, bre_ref, bim_ref, cre_ref, cim_ref, lam_ref, h0_ref, y_ref, hfin_ref, hre_s, him_s, st_s, *,
               tc, nc, ns, nb):
    d = pl.program_id(0)
    c = pl.program_id(2)

    @pl.when(c == 0)
    def _():
        st_s[...] = h0_ref[:, 0]

    sw = u_ref.shape[2]
    halves = [(slice(j * sw // S5_DIAG_BLOCKS, (j + 1) * sw // S5_DIAG_BLOCKS),
               slice(j * ns // S5_DIAG_BLOCKS, (j + 1) * ns // S5_DIAG_BLOCKS)) for j in range(S5_DIAG_BLOCKS)]
    for j in range(nb):
        ub = u_ref[j].astype(BF16)
        for us, hs in halves:
            hre_s[j, :, hs] = _dot(ub[:, us], bre_ref[0, us, hs])
            him_s[j, :, hs] = _dot(ub[:, us], bim_ref[0, us, hs])
    lr = lam_ref[0, 0:1, :]
    li = lam_ref[0, 1:2, :]

    def body(t, carry):
        r = jnp.where(d == 0, t, tc - 1 - t)
        new = []
        for j, (hr, hi) in enumerate(carry):
            nr = lr * hr - li * hi + hre_s[j, pl.ds(r, 1), :]
            ni = lr * hi + li * hr + him_s[j, pl.ds(r, 1), :]
            hre_s[j, pl.ds(r, 1), :] = nr
            him_s[j, pl.ds(r, 1), :] = ni
            new.append((nr, ni))
        return tuple(new)

    start = tuple((st_s[j, 0:1, :], st_s[j, 1:2, :]) for j in range(nb))
    for j, (hr, hi) in enumerate(lax.fori_loop(0, tc, body, start, unroll=4)):
        st_s[j, 0:1, :] = hr
        st_s[j, 1:2, :] = hi
    for j in range(nb):
        for us, hs in halves:
            y_ref[0, j, :, us] = (_dot(hre_s[j, :, hs].astype(BF16), cre_ref[0, hs, us])
                                  + _dot(him_s[j, :, hs].astype(BF16), cim_ref[0, hs, us]))

    @pl.when(c == nc - 1)
    def _():
        hfin_ref[:, 0] = st_s[...]


def s5_scan(proj, ucol, bre, bim, cre, cim, lam, h0, nseq, seq_len, sw, tc):
    ns = bre.shape[2]
    nc = seq_len // tc
    nb = S5_SEQS_PER_STEP if nseq % S5_SEQS_PER_STEP == 0 else 1

    def chunk(d, c):
        return c + d * (nc - 1 - 2 * c)

    y, hfin = pl.pallas_call(
        functools.partial(_s5_kernel, tc=tc, nc=nc, ns=ns, nb=nb), grid=(2, nseq // nb, nc),
        in_specs=[pl.BlockSpec((nb, tc, sw), lambda d, b, c: (b, chunk(d, c), ucol)),
                  pl.BlockSpec((1, sw, ns), lambda d, b, c: (d, 0, 0)),
                  pl.BlockSpec((1, sw, ns), lambda d, b, c: (d, 0, 0)),
                  pl.BlockSpec((1, ns, sw), lambda d, b, c: (d, 0, 0)),
                  pl.BlockSpec((1, ns, sw), lambda d, b, c: (d, 0, 0)),
                  pl.BlockSpec((1, 2, ns), lambda d, b, c: (d, 0, 0)),
                  pl.BlockSpec((nb, 1, 2, ns), lambda d, b, c: (b, d, 0, 0))],
        out_specs=[pl.BlockSpec((1, nb, tc, sw), lambda d, b, c: (d, b, chunk(d, c), 0)),
                   pl.BlockSpec((nb, 1, 2, ns), lambda d, b, c: (b, d, 0, 0))],
        out_shape=[jax.ShapeDtypeStruct((2, nseq, seq_len, sw), F32), jax.ShapeDtypeStruct((nseq, 2, 2, ns), F32)],
        scratch_shapes=[pltpu.VMEM((nb, tc, ns), F32), pltpu.VMEM((nb, tc, ns), F32), pltpu.VMEM((nb, 2, ns), F32)],
        compiler_params=_cp("parallel", "parallel", "arbitrary"), name="s5_scan",
    )(proj.reshape(nseq, seq_len, proj.shape[1]), bre, bim, cre, cim, lam, h0)
    return y.reshape(2, nseq * seq_len, sw), hfin


def _s5_glu_kernel(yf_ref, yb_ref, u_ref, d_ref, w_ref, b_ref, o_ref):
    y = jax.nn.gelu(yf_ref[0] + yb_ref[0] + d_ref[...] * u_ref[...])
    o_ref[...] = y * jax.nn.sigmoid(_dot(y.astype(BF16), w_ref[...]) + b_ref[...])


def s5_glu(y2, proj, ucol, d_skip, glu_w_bf16, glu_b, tm):
    _, n, sw = y2.shape
    return pl.pallas_call(
        _s5_glu_kernel, grid=(n // tm,),
        in_specs=[pl.BlockSpec((1, tm, sw), lambda i: (0, i, 0)), pl.BlockSpec((1, tm, sw), lambda i: (1, i, 0)),
                  pl.BlockSpec((tm, sw), lambda i: (i, ucol)), pl.BlockSpec((1, sw), lambda i: (0, 0)),
                  pl.BlockSpec((sw, sw), lambda i: (0, 0)), pl.BlockSpec((1, sw), lambda i: (0, 0))],
        out_specs=pl.BlockSpec((tm, sw), lambda i: (i, 0)),
        out_shape=jax.ShapeDtypeStruct((n, sw), F32), compiler_params=_cp("parallel"), name="s5_glu",
    )(y2, y2, proj, d_skip.reshape(1, sw), glu_w_bf16, glu_b.reshape(1, sw))


def _residual_and_next_norm(x_ref, gate_ref, y, g2_ref, sc2_ref, sh2_ref, o_ref, hn_ref):
    xn = x_ref[...] + gate_ref[0] * y
    o_ref[...] = xn
    hn_ref[...] = _normmod(xn, g2_ref[...], sc2_ref[0], sh2_ref[0]).astype(BF16)


def _row_specs(d, tm, rows_per_mod):
    mod = pl.BlockSpec((1, 1, d), lambda i: ((i * tm) // rows_per_mod, 0, 0))
    return [pl.BlockSpec((tm, d), lambda i: (i, 0)), mod, pl.BlockSpec((1, d), lambda i: (0, 0)), mod, mod]


def _row_outs(n, d, tm):
    spec = pl.BlockSpec((tm, d), lambda i: (i, 0))
    return [spec, spec], [jax.ShapeDtypeStruct((n, d), F32), jax.ShapeDtypeStruct((n, d), BF16)]


def _even_out_kernel(a_ref, b_ref, wa_ref, wb_ref, x_ref, gate_ref, g2_ref, sc2_ref, sh2_ref, o_ref, hn_ref):
    y = _dot(a_ref[...].astype(BF16), wa_ref[...]) + _dot(b_ref[...].astype(BF16), wb_ref[...])
    _residual_and_next_norm(x_ref, gate_ref, y, g2_ref, sc2_ref, sh2_ref, o_ref, hn_ref)


def even_out(hy, s5o, w_bf16, x, gate, g2, sc2, sh2, rows_per_mod, tm):
    n, d = x.shape
    hw = hy.shape[1]
    sw = s5o.shape[1]
    out_specs, out_shape = _row_outs(n, d, tm)
    return pl.pallas_call(
        _even_out_kernel, grid=(n // tm,),
        in_specs=[pl.BlockSpec((tm, hw), lambda i: (i, 0)), pl.BlockSpec((tm, sw), lambda i: (i, 0)),
                  pl.BlockSpec((hw, d), lambda i: (0, 0)), pl.BlockSpec((sw, d), lambda i: (hw // sw, 0))]
        + _row_specs(d, tm, rows_per_mod),
        out_specs=out_specs, out_shape=out_shape, compiler_params=_cp("parallel"), name="even_out",
    )(hy, s5o, w_bf16, w_bf16, x, gate, g2.reshape(1, d), sc2, sh2)


ML_HEAD_GROUP = 8


def _log_sigmoid(x):
    return jnp.minimum(x, 0.0) - jnp.log1p(jnp.exp(-jnp.abs(x)))


def _mlstm_kernel(q_ref, k_ref, v_ref, g_ref, gb_ref, c0_ref, n0_ref, m0_ref, h_ref, cf_ref, nf_ref, mf_ref,
                  c_s, m_s, *, nh, dh, tc, nc):
    d = pl.program_id(0)
    c = pl.program_id(2)

    @pl.when(c == 0)
    def _():
        for h in range(nh):
            c_s[h, :, :dh] = c0_ref[0, 0, h]
            c_s[h, :, dh:] = jnp.broadcast_to(n0_ref[0, 0, h:h + 1, :], (dh, dh)).T
        m_s[...] = m0_ref[0, 0]

    ones = jnp.ones((tc, dh), F32)
    gates = g_ref[...] + gb_ref[0]
    lane = lax.broadcasted_iota(jnp.int32, gates.shape, 1)
    logf = jnp.where(jnp.logical_and(lane >= nh, lane < 2 * nh), _log_sigmoid(gates), 0.0)
    r_i = lax.broadcasted_iota(jnp.int32, (tc, tc), 0)
    s_i = lax.broadcasted_iota(jnp.int32, (tc, tc), 1)
    causal = (r_i - s_i) * (1 - 2 * d) >= 0
    bcum = _dot(causal.astype(F32), logf, precision=HIGHEST)
    btot = jnp.sum(logf, axis=0, keepdims=True)
    gates_t = gates.T
    bcum_t = bcum.T
    for g0 in range(0, nh, ML_HEAD_GROUP):
        hds = list(range(g0, min(g0 + ML_HEAD_GROUP, nh)))
        hsl = {h: slice(h * dh, (h + 1) * dh) for h in hds}
        b_col = {h: bcum[:, nh + h:nh + h + 1] for h in hds}
        m_old = {h: m_s[h:h + 1, :] for h in hds}
        a = {h: b_col[h] + m_old[h] for h in hds}
        src = {h: jnp.where(causal, gates_t[h:h + 1, :] - bcum_t[nh + h:nh + h + 1, :], NEG_INF) for h in hds}
        mq = {h: jnp.maximum(a[h], b_col[h] + jnp.max(src[h], axis=-1, keepdims=True)) for h in hds}
        s = {h: _dot_nt(q_ref[:, hsl[h]], k_ref[:, hsl[h]]) * jnp.exp(src[h] + (b_col[h] - mq[h])) for h in hds}
        v1 = {h: jnp.concatenate([v_ref[:, hsl[h]], ones], axis=1) for h in hds}
        cn = {h: c_s[h] for h in hds}
        qw = {h: jnp.exp(a[h] - mq[h]) * q_ref[:, hsl[h]] for h in hds}
        both = {h: _dot(s[h], v1[h]) + _dot(qw[h], cn[h]) for h in hds}
        for h in hds:
            h_ref[0, :, hsl[h]] = both[h][:, :dh] / jnp.maximum(jnp.abs(both[h][:, dh:]), jnp.exp(-mq[h]))
        b_last = {h: btot[:, nh + h:nh + h + 1] for h in hds}
        g = {h: b_last[h] - b_col[h] + gates[:, h:h + 1] for h in hds}
        m_new = {h: jnp.maximum(b_last[h] + m_old[h], jnp.max(g[h], axis=0, keepdims=True)) for h in hds}
        kw = {h: k_ref[:, hsl[h]] * jnp.exp(g[h] - m_new[h]) for h in hds}
        for h in hds:
            c_s[h] = jnp.exp(b_last[h] + m_old[h] - m_new[h]) * cn[h] + _dot(kw[h].T, v1[h])
            m_s[h:h + 1, :] = m_new[h]

    @pl.when(c == nc - 1)
    def _():
        for h in range(nh):
            cf_ref[0, 0, h] = c_s[h, :, :dh]
            nf_ref[0, 0, h:h + 1, :] = c_s[h, :, dh:].T[0:1, :]
        mf_ref[0, 0] = m_s[...]


def mlstm_scan(qk, proj, vcol, gate_bias, c0, n0, m0, nseq, seq_len, nh, dh):
    tc = ML_CHUNK
    nc = seq_len // tc
    w = nh * dh

    def chunk(d, c):
        return c + d * (nc - 1 - 2 * c)

    rowblk = lambda d, b, c: b * nc + chunk(d, c)
    st = lambda shp: pl.BlockSpec((1, 1) + shp, lambda d, b, c: (b, d) + (0,) * len(shp))
    return pl.pallas_call(
        functools.partial(_mlstm_kernel, nh=nh, dh=dh, tc=tc, nc=nc), grid=(2, nseq, nc),
        in_specs=[pl.BlockSpec((tc, w), lambda d, b, c: (rowblk(d, b, c), 0)),
                  pl.BlockSpec((tc, w), lambda d, b, c: (rowblk(d, b, c), 1)),
                  pl.BlockSpec((tc, w), lambda d, b, c: (rowblk(d, b, c), vcol)),
                  pl.BlockSpec((tc, LANES), lambda d, b, c: (rowblk(d, b, c), 4 * w // LANES + d)),
                  pl.BlockSpec((1, 1, LANES), lambda d, b, c: (d, 0, 0)),
                  st((nh, dh, dh)), st((nh, dh)), st((nh, 1))],
        out_specs=[pl.BlockSpec((1, tc, w), lambda d, b, c: (d, rowblk(d, b, c), 0)),
                   st((nh, dh, dh)), st((nh, dh)), st((nh, 1))],
        out_shape=[jax.ShapeDtypeStruct((2, nseq * seq_len, w), F32),
                   jax.ShapeDtypeStruct((nseq, 2, nh, dh, dh), F32), jax.ShapeDtypeStruct((nseq, 2, nh, dh), F32),
                   jax.ShapeDtypeStruct((nseq, 2, nh, 1), F32)],
        scratch_shapes=[pltpu.VMEM((nh, dh, 2 * dh), F32), pltpu.VMEM((nh, 1), F32)],
        compiler_params=_cp("parallel", "parallel", "arbitrary"), name="mlstm_scan",
    )(qk, qk, proj, proj, gate_bias, c0, n0, m0)


def _odd_out_kernel(hf_ref, hb_ref, og_ref, ng_ref, w_ref, x_ref, gate_ref, g2_ref, sc2_ref, sh2_ref, o_ref, hn_ref,
                    a_s, *, nh, dh):
    for h in range(nh):
        hs = slice(h * dh, (h + 1) * dh)
        blk = hf_ref[0, :, hs] + hb_ref[0, :, hs]
        blk = blk * lax.rsqrt(jnp.mean(blk * blk, axis=-1, keepdims=True) + EPS)
        a_s[:, hs] = ((blk * ng_ref[:, hs]) * _silu(og_ref[:, hs])).astype(BF16)
    _residual_and_next_norm(x_ref, gate_ref, _dot(a_s[...], w_ref[...]), g2_ref, sc2_ref, sh2_ref, o_ref, hn_ref)


def odd_out(h2, proj, ocol, norm_g, w_bf16, x, gate, g2, sc2, sh2, rows_per_mod, nh, dh, tm):
    n, d = x.shape
    w = nh * dh
    out_specs, out_shape = _row_outs(n, d, tm)
    return pl.pallas_call(
        functools.partial(_odd_out_kernel, nh=nh, dh=dh), grid=(n // tm,),
        in_specs=[pl.BlockSpec((1, tm, w), lambda i: (0, i, 0)), pl.BlockSpec((1, tm, w), lambda i: (1, i, 0)),
                  pl.BlockSpec((tm, w), lambda i: (i, ocol)), pl.BlockSpec((1, w), lambda i: (0, 0)),
                  pl.BlockSpec((w, d), lambda i: (0, 0))] + _row_specs(d, tm, rows_per_mod),
        out_specs=out_specs, out_shape=out_shape, scratch_shapes=[pltpu.VMEM((tm, w), BF16)],
        compiler_params=_cp("parallel"), name="odd_out",
    )(h2, h2, proj, norm_g.reshape(1, w), w_bf16, x, gate, g2.reshape(1, d), sc2, sh2)


def _top_values(curs, k, outs):
    curs = list(curs)
    for j in range(k):
        for a, out_s in enumerate(outs):
            m = jnp.max(curs[a], axis=0, keepdims=True)
            out_s[j:j + 1, :] = m
            curs[a] = jnp.where(curs[a] == m, NEG_INF, curs[a])


def _pair_candidates(k):
    return [(a, k // (a + 1)) for a in range(k)]


PEER_HEADS_PER_TRIP = 8


def _peer_score_kernel(h_ref, wq_ref, keys_ref, th_ref, s2_ref, w1_ref, w2_ref, q_s, v_s, cand_s, best_s, *,
                       nh, half, topk):
    q_s[...] = _dot_nt(wq_ref[...], h_ref[...])
    kk = topk + 1
    cand_s[...] = jnp.full(cand_s.shape, NEG_INF, F32)
    group = range(PEER_HEADS_PER_TRIP)

    def heads(trip, carry):
        hds = [trip * PEER_HEADS_PER_TRIP + u for u in group]
        scores = []
        for hd in hds:
            base = pl.multiple_of(hd * 2 * half, 2 * half)
            scores.append(_dot(keys_ref[hd, 0], q_s[pl.ds(base, half), :]))
            scores.append(_dot(keys_ref[hd, 1], q_s[pl.ds(base + half, half), :]))
        _top_values(scores, kk, [v_s.at[u, c] for u in group for c in range(2)])
        for u in group:
            off = 0
            for a, cnt in _pair_candidates(kk):
                cand_s[u, off:off + cnt, :] = v_s[u, 0, a:a + 1, :] + v_s[u, 1, 0:cnt, :]
                off += cnt
        _top_values([cand_s[u] for u in group], kk, [best_s.at[u] for u in group])
        for u, hd in enumerate(hds):
            s1, s2 = scores[2 * u], scores[2 * u + 1]
            best = best_s[u, 0:topk, :]
            z = jnp.sum(jnp.exp(best - best[0:1, :]), axis=0, keepdims=True)
            tmid = 0.5 * (best_s[u, topk - 1:topk, :] + best_s[u, topk:topk + 1, :])
            th = tmid - s1
            w1 = jnp.exp(s1 - v_s[u, 0, 0:1, :]) / z
            w2 = jnp.exp(s2 - v_s[u, 1, 0:1, :])
            for lt in range(s1.shape[1] // LANES):
                sl = slice(lt * LANES, (lt + 1) * LANES)
                th_ref[hd, lt] = th[:, sl]
                s2_ref[hd, lt] = s2[:, sl]
                w1_ref[hd, lt] = w1[:, sl]
                w2_ref[hd, lt] = w2[:, sl]
        return carry

    lax.fori_loop(0, nh // PEER_HEADS_PER_TRIP, heads, 0)


def peer_scores(h_bf16, wq_t_bf16, keys, tt):
    n, d = h_bf16.shape
    nh, _, nk, half = keys.shape
    kk = PK_TOPK + 1
    ncand = -(-sum(c for _, c in _pair_candidates(kk)) // 8) * 8
    big = jax.ShapeDtypeStruct((nh, n // LANES, nk, LANES), F32)
    bspec = pl.BlockSpec((nh, tt // LANES, nk, LANES), lambda i: (0, i, 0, 0))
    return pl.pallas_call(
        functools.partial(_peer_score_kernel, nh=nh, half=half, topk=PK_TOPK), grid=(n // tt,),
        in_specs=[pl.BlockSpec((tt, d), lambda i: (i, 0)), pl.BlockSpec((nh * 2 * half, d), lambda i: (0, 0)),
                  pl.BlockSpec((nh, 2, nk, half), lambda i: (0, 0, 0, 0))],
        out_specs=[bspec, bspec, bspec, bspec],
        out_shape=[big, big, big, big],
        scratch_shapes=[pltpu.VMEM((nh * 2 * half, tt), F32), pltpu.VMEM((PEER_HEADS_PER_TRIP, 2, 24, tt), F32),
                        pltpu.VMEM((PEER_HEADS_PER_TRIP, ncand, tt), F32), pltpu.VMEM((PEER_HEADS_PER_TRIP, 24, tt), F32)],
        compiler_params=_cp("parallel"), name="peer_scores",
    )(h_bf16, wq_t_bf16, keys)


PEER_KEY_ROWS = 16
GELU_C1 = math.sqrt(2.0 / math.pi)
GELU_C2 = 0.044715 * GELU_C1


def _gelu_tanh(x):
    half_x = 0.5 * x
    return half_x + half_x * jnp.tanh(x * (GELU_C1 + GELU_C2 * (x * x)))


def _peer_dense_kernel(h_ref, u_ref, vt_ref, th_ref, s2_ref, w1_ref, w2_ref, x_ref, gate_ref, fg_ref, o_ref,
                       acc_s, st_s, wt_s, *, nh, nk, ec, tt, final):
    e = pl.program_id(1)
    nlt = tt // LANES
    n_i1 = ec // nk
    nkt = nk // PEER_KEY_ROWS
    nsub = PEER_KEY_ROWS // SUBLANES

    @pl.when(e == 0)
    def _():
        acc_s[...] = jnp.zeros_like(acc_s)

    st = _gelu_tanh(_dot_nt(u_ref[...], h_ref[...]))
    for lt in range(nlt):
        st_s[lt] = st[:, lt * LANES:(lt + 1) * LANES]

    def tile(idx, carry):
        lt = idx // nkt
        k0 = (idx % nkt) * PEER_KEY_ROWS
        subs = [pl.ds(pl.multiple_of(k0 + j * SUBLANES, SUBLANES), SUBLANES) for j in range(nsub)]
        g = [[jnp.zeros((SUBLANES, LANES), F32) for _ in subs] for _ in range(n_i1)]
        for hd in range(nh):
            s2t = [s2_ref[hd, lt, sub, :] for sub in subs]
            w2t = [w2_ref[hd, lt, sub, :] for sub in subs]
            for li in range(n_i1):
                thb = jnp.broadcast_to(th_ref[hd, lt, li:li + 1, :], (SUBLANES, LANES))
                w1b = jnp.broadcast_to(w1_ref[hd, lt, li:li + 1, :], (SUBLANES, LANES))
                for j in range(nsub):
                    g[li][j] = g[li][j] + jnp.where(s2t[j] >= thb, w2t[j] * w1b, 0.0)
        for li in range(n_i1):
            rows = pl.ds(pl.multiple_of(li * nk + k0, PEER_KEY_ROWS), PEER_KEY_ROWS)
            wt_s[lt, rows, :] = (st_s[lt, rows, :] * jnp.concatenate(g[li], axis=0)).astype(BF16)
        return carry

    lax.fori_loop(0, nlt * nkt, tile, 0)
    wt = jnp.concatenate([wt_s[lt] for lt in range(nlt)], axis=1)
    acc_s[...] += _dot(vt_ref[0], wt)

    @pl.when(e == pl.num_programs(1) - 1)
    def _():
        xn = x_ref[...] + gate_ref[0] * acc_s[...].T
        if final:
            xn = (xn * lax.rsqrt(jnp.mean(xn * xn, axis=-1, keepdims=True) + EPS)) * fg_ref[...]
        o_ref[...] = xn


def peer_dense(h_bf16, u_bf16, vt_bf16, th, s2, w1, w2, x, gate, final_g, final, rows_per_mod, tt, ec):
    n, d = x.shape
    nh, _, nk, _ = s2.shape
    nchunk = u_bf16.shape[0] // ec
    bspec = pl.BlockSpec((nh, tt // LANES, nk, LANES), lambda i, e: (0, i, 0, 0))
    rspec = pl.BlockSpec((nh, tt // LANES, ec // nk, LANES), lambda i, e: (0, i, e, 0))
    tile_buf = (tt // LANES, ec, LANES)
    return pl.pallas_call(
        functools.partial(_peer_dense_kernel, nh=nh, nk=nk, ec=ec, tt=tt, final=final), grid=(n // tt, nchunk),
        in_specs=[pl.BlockSpec((tt, d), lambda i, e: (i, 0)), pl.BlockSpec((ec, d), lambda i, e: (e, 0)),
                  pl.BlockSpec((1, d, ec), lambda i, e: (e, 0, 0)), rspec, bspec, rspec, bspec,
                  pl.BlockSpec((tt, d), lambda i, e: (i, 0)),
                  pl.BlockSpec((1, 1, d), lambda i, e: ((i * tt) // rows_per_mod, 0, 0)),
                  pl.BlockSpec((1, d), lambda i, e: (0, 0))],
        out_specs=pl.BlockSpec((tt, d), lambda i, e: (i, 0)),
        out_shape=jax.ShapeDtypeStruct((n, d), F32),
        scratch_shapes=[pltpu.VMEM((d, tt), F32), pltpu.VMEM(tile_buf, F32), pltpu.VMEM(tile_buf, BF16)],
        compiler_params=_cp("parallel", "arbitrary"), name="peer_dense",
    )(h_bf16, u_bf16, vt_bf16, th, s2, w1, w2, x, gate, final_g.reshape(1, d))


def _s5_params(a_re, a_im, b_re, b_im, c_re, c_im, log_step):
    lam = lax.complex(a_re.astype(F32), a_im.astype(F32))
    lam_bar = jnp.exp(lam * jnp.exp(log_step.astype(F32))[..., None])
    b_bar = ((lam_bar - 1.0) / lam)[..., None] * lax.complex(b_re.astype(F32), b_im.astype(F32))
    ngrp, npst, nch = b_bar.shape[1:]
    eye = jnp.eye(ngrp, dtype=F32)

    def b_mat(part):
        return jnp.einsum("dgpj,gh->dgjhp", part, eye).reshape(2, ngrp * nch, ngrp * npst)

    def c_mat(part):
        return jnp.einsum("dgjp,gh->dgphj", part, eye).reshape(2, ngrp * npst, ngrp * nch)

    bre, bim = b_mat(b_bar.real).astype(BF16), b_mat(b_bar.imag).astype(BF16)
    cre, cim = c_mat(c_re.astype(F32)).astype(BF16), c_mat(-c_im.astype(F32)).astype(BF16)
    lam2 = jnp.stack([lam_bar.real.reshape(2, -1), lam_bar.imag.reshape(2, -1)], axis=1)
    return bre, bim, cre, cim, lam2


def _pos_embed(n_tok, d, grid_w):
    rows = n_tok // grid_w
    quarter = d // 4
    omega = 1.0 / (10000.0 ** (jnp.arange(quarter, dtype=F32) / quarter))

    def emb1d(pos):
        ang = pos.astype(F32)[:, None] * omega[None]
        return jnp.concatenate([jnp.sin(ang), jnp.cos(ang)], axis=-1)

    er = emb1d(jnp.arange(rows))
    ec = emb1d(jnp.arange(grid_w))
    half = d // 2
    pe = jnp.concatenate([jnp.broadcast_to(er[:, None], (rows, grid_w, half)),
                          jnp.broadcast_to(ec[None], (rows, grid_w, half))], axis=-1)
    return pe.reshape(rows * grid_w, d)


def _tile(n, pref):
    return pref if n % pref == 0 else n


def _trunk(x, mods, s5_h0, ml_c0, ml_n0, ml_m0, p, nseq, seq_len, rows_per_mod):
    n, d = x.shape
    tm = _tile(min(rows_per_mod, n), 512)
    depth = p["norm_g"].shape[0]
    s5_fin, ml_fin = [], []
    for l in range(depth):
        sh1, sc1, g1, sh2, sc2, g2 = mods[l]
        i = l // 2
        if l % 2 == 0:
            hw = p["hy_bias"].shape[2]
            sw = p["s5_d"].shape[1]
            proj = normmod_matmul(x, p["norm_g"][l, 0], sc1, sh1, p["ev_w_in"][i].astype(BF16), rows_per_mod, tm,
                                  3 * hw + sw)
            hy_in = short_conv(proj, 3 * hw, p["hy_conv_w"][i], p["hy_conv_b"][i], jnp.ones((3 * hw,), F32),
                               seq_len, act=False, out_dtype=F32)
            cos_t, a_t, a_tt = dft_tables(seq_len)
            tf = _tile(seq_len, 512)
            taps, sumsq = hyena_filter_taps(seq_len, p["hy_w1"][i], p["hy_b1"][i], p["hy_w2"][i], p["hy_b2"][i],
                                            p["hy_w3"][i], p["hy_freq"][i], p["hy_decay"][i], hw)
            kr, ki = hyena_filter_spectrum(cos_t, a_t, taps, sumsq, hw, tf)
            bias = p["hy_bias"][i].astype(F32)
            z, zcol = hy_in, 0
            for o in range(bias.shape[0]):
                yr, yi = hyena_fwd(cos_t, a_t, z, zcol, kr, ki, o, nseq, hw, tf)
                z = hyena_inv(cos_t, a_tt, yr, yi, z, zcol, hy_in, 1 + o, bias[o:o + 1], nseq, hw, tf)
                zcol = 0
            bre, bim, cre, cim, lam2 = _s5_params(p["s5_a_re"][i], p["s5_a_im"][i], p["s5_b_re"][i], p["s5_b_im"][i],
                                                  p["s5_c_re"][i], p["s5_c_im"][i], p["s5_log_step"][i])
            ucol = 3 * hw // sw
            y2, hfin = s5_scan(proj, ucol, bre, bim, cre, cim, lam2, s5_h0[i], nseq, seq_len, sw, _tile(seq_len, 256))
            s5_fin.append(hfin)
            s5o = s5_glu(y2, proj, ucol, p["s5_d"][i], p["s5_glu_w"][i].astype(BF16), p["s5_glu_b"][i], tm)
            x, hn = even_out(z, s5o, p["ev_w_out"][i].astype(BF16), x, g1, p["norm_g"][l, 1], sc2, sh2, rows_per_mod, tm)
        else:
            nh = p["od_gate_b"].shape[2]
            w = p["ml_norm_g"].shape[1]
            dh = w // nh
            w_in = p["od_w_in"][i]
            wg = w_in[:, 4 * w:].reshape(d, 4, nh)
            gb = p["od_gate_b"][i].astype(F32)
            wg2 = jnp.zeros((d, 2, LANES), w_in.dtype)
            bg2 = jnp.zeros((2, 1, LANES), F32)
            for dr in range(2):
                wg2 = wg2.at[:, dr, :nh].set(wg[:, dr]).at[:, dr, nh:2 * nh].set(wg[:, 2 + dr])
                bg2 = bg2.at[dr, 0, :nh].set(gb[dr]).at[dr, 0, nh:2 * nh].set(gb[2 + dr])
            w_all = jnp.concatenate([w_in[:, :4 * w], wg2.reshape(d, 2 * LANES)], axis=1).astype(BF16)
            proj = normmod_matmul(x, p["norm_g"][l, 0], sc1, sh1, w_all, rows_per_mod, tm, w_all.shape[1] // 2)
            qscale = jnp.concatenate([jnp.full((w,), dh ** -0.5, F32), jnp.ones((w,), F32)])
            qk = short_conv(proj, 2 * w, p["ml_conv_w"][i], p["ml_conv_b"][i], qscale, seq_len, act=True, out_dtype=BF16)
            h2, cf, nf, mf = mlstm_scan(qk, proj, 2, bg2, ml_c0[i], ml_n0[i], ml_m0[i], nseq, seq_len, nh, dh)
            ml_fin.append((cf, nf, mf))
            x, hn = odd_out(h2, proj, 3, p["ml_norm_g"][i], p["od_w_out"][i].astype(BF16), x, g1, p["norm_g"][l, 1],
                            sc2, sh2, rows_per_mod, nh, dh, tm)
        tt = _tile(min(rows_per_mod, n), 512)
        th, s2, w1, w2 = peer_scores(hn, p["pk_w_q"][l].T.astype(BF16), p["pk_keys"][l].astype(F32), _tile(tt, 256))
        ec = 2048
        vt = p["pk_v"][l].astype(BF16).reshape(-1, ec, d).transpose(0, 2, 1)
        x = peer_dense(hn, p["pk_u"][l].astype(BF16), vt, th, s2, w1, w2, x, g2, p["final_g"], l == depth - 1,
                       rows_per_mod, tt, ec)
    return x, s5_fin, ml_fin


def kernel(x_prompt, x_sample, state_s5_re, state_s5_im, state_mlstm_C, state_mlstm_n, state_mlstm_m, c, c_ctx, norm_g, ada_w, ada_b, final_g, ev_w_in, hy_conv_w, hy_conv_b, hy_w1, hy_b1, hy_w2, hy_b2, hy_w3, hy_freq, hy_decay, hy_bias, s5_a_re, s5_a_im, s5_b_re, s5_b_im, s5_c_re, s5_c_im, s5_log_step, s5_d, s5_glu_w, s5_glu_b, ev_w_out, od_w_in, od_gate_b, ml_conv_w, ml_conv_b, ml_norm_g, od_w_out, pk_w_q, pk_keys, pk_u, pk_v):
    p = dict(norm_g=norm_g, ada_w=ada_w, ada_b=ada_b, final_g=final_g, ev_w_in=ev_w_in,
             hy_conv_w=hy_conv_w, hy_conv_b=hy_conv_b, hy_w1=hy_w1, hy_b1=hy_b1, hy_w2=hy_w2, hy_b2=hy_b2,
             hy_w3=hy_w3, hy_freq=hy_freq, hy_decay=hy_decay, hy_bias=hy_bias, s5_a_re=s5_a_re,
             s5_a_im=s5_a_im, s5_b_re=s5_b_re, s5_b_im=s5_b_im, s5_c_re=s5_c_re, s5_c_im=s5_c_im,
             s5_log_step=s5_log_step, s5_d=s5_d, s5_glu_w=s5_glu_w, s5_glu_b=s5_glu_b, ev_w_out=ev_w_out,
             od_w_in=od_w_in, od_gate_b=od_gate_b, ml_conv_w=ml_conv_w, ml_conv_b=ml_conv_b,
             ml_norm_g=ml_norm_g, od_w_out=od_w_out, pk_w_q=pk_w_q, pk_keys=pk_keys, pk_u=pk_u, pk_v=pk_v)
    nb, seq, d = x_prompt.shape
    db, dseq, _ = x_sample.shape
    depth = norm_g.shape[0]
    n_even, n_odd = (depth + 1) // 2, depth // 2
    assert db + 1 <= 8

    cond8 = jnp.zeros((8, d), F32).at[0].set(c_ctx.astype(F32)).at[1:1 + db].set(c.astype(F32))
    mods_ctx, mods_lat = [], []
    for l in range(depth):
        mod = ada_mod(cond8, ada_w[l].astype(F32), ada_b[l].astype(F32))
        chunks = [mod[:, j * d:(j + 1) * d] for j in range(6)]
        mods_ctx.append([ch[0:1].reshape(1, 1, d) for ch in chunks])
        mods_lat.append([ch[1:1 + db].reshape(db, 1, d) for ch in chunks])

    def s5_state(re, im, bsz):
        return [jnp.stack([re[:, i].reshape(bsz, 2, -1), im[:, i].reshape(bsz, 2, -1)], axis=2).astype(F32)
                for i in range(n_even)]

    ngrp, npst = s5_a_re.shape[2], s5_a_re.shape[3]
    nh, dh = state_mlstm_C.shape[3], state_mlstm_C.shape[4]
    zeros_s5 = jnp.zeros((nb, n_even, 2, ngrp, npst), F32)
    y_prompt, s5_fin, ml_fin = _trunk(
        x_prompt.reshape(nb * seq, d), mods_ctx, s5_state(zeros_s5, zeros_s5, nb),
        [jnp.zeros((nb, 2, nh, dh, dh), F32)] * n_odd, [jnp.zeros((nb, 2, nh, dh), F32)] * n_odd,
        [jnp.zeros((nb, 2, nh, 1), F32)] * n_odd, p, nb, seq, nb * seq)
    x_lat = add_pos(x_sample.reshape(db * dseq, d), _pos_embed(dseq, d, GRID_W), dseq, _tile(dseq, 512))
    y_sample, _, _ = _trunk(
        x_lat, mods_lat, s5_state(state_s5_re, state_s5_im, db),
        [state_mlstm_C[:, i].astype(F32) for i in range(n_odd)], [state_mlstm_n[:, i].astype(F32) for i in range(n_odd)],
        [state_mlstm_m[:, i].astype(F32)[..., None] for i in range(n_odd)], p, db, dseq, dseq)

    new_s5_re = jnp.stack([h[:, :, 0].reshape(nb, 2, ngrp, npst) for h in s5_fin], axis=1)
    new_s5_im = jnp.stack([h[:, :, 1].reshape(nb, 2, ngrp, npst) for h in s5_fin], axis=1)
    new_c = jnp.stack([f[0] for f in ml_fin], axis=1)
    new_n = jnp.stack([f[1] for f in ml_fin], axis=1)
    new_m = jnp.stack([f[2][..., 0] for f in ml_fin], axis=1)
    return (y_prompt.reshape(nb, seq, d), y_sample.reshape(db, dseq, d), new_s5_re, new_s5_im, new_c, new_n, new_m)
```

```python
import functools
import math

import jax
import jax.numpy as jnp
from jax import lax
from jax.experimental import pallas as pl
from jax.experimental.pallas import tpu as pltpu

F32 = jnp.float32
BF16 = jnp.bfloat16
EPS = 1e-6
HIGHEST = lax.Precision.HIGHEST
V7X_VMEM_LIMIT_BYTES = 56 * 1024 * 1024
LANES = 128
SUBLANES = 8
ML_CHUNK = 128
PK_TOPK = 16
GRID_W = 64
NEG_INF = float("-inf")


def _cp(*sem):
    return pltpu.CompilerParams(dimension_semantics=sem, vmem_limit_bytes=V7X_VMEM_LIMIT_BYTES)


def _dot(a, b, **kw):
    return jnp.dot(a, b, preferred_element_type=F32, **kw)


def _dot_nt(a, b):
    return lax.dot_general(a, b, (((1,), (1,)), ((), ())), preferred_element_type=F32)


def _silu(x):
    return x * jax.nn.sigmoid(x)


def _ada_kernel(c_ref, w_ref, b_ref, o_ref):
    o_ref[...] = _dot(_silu(c_ref[...]), w_ref[...], precision=HIGHEST) + b_ref[...]


def ada_mod(cond8, w, b):
    d, no = w.shape
    tn = 1536 if no % 1536 == 0 else no
    return pl.pallas_call(
        _ada_kernel, grid=(no // tn,),
        in_specs=[pl.BlockSpec((8, d), lambda j: (0, 0)), pl.BlockSpec((d, tn), lambda j: (0, j)),
                  pl.BlockSpec((1, tn), lambda j: (0, j))],
        out_specs=pl.BlockSpec((8, tn), lambda j: (0, j)),
        out_shape=jax.ShapeDtypeStruct((8, no), F32), compiler_params=_cp("parallel"), name="ada_mod",
    )(cond8, w, b.reshape(1, no))


def _normmod(x, g, sc, sh):
    y = x * lax.rsqrt(jnp.mean(x * x, axis=-1, keepdims=True) + EPS)
    return (y * g) * (1.0 + sc) + sh


def _mod_spec(d, tm, rows_per_mod):
    return pl.BlockSpec((1, 1, d), lambda i, j: ((i * tm) // rows_per_mod, 0, 0))


def _nm_matmul_kernel(x_ref, g_ref, sc_ref, sh_ref, w_ref, o_ref, h_ref):
    @pl.when(pl.program_id(1) == 0)
    def _():
        h_ref[...] = _normmod(x_ref[...], g_ref[...], sc_ref[0], sh_ref[0]).astype(BF16)
    o_ref[...] = _dot(h_ref[...], w_ref[...])


def normmod_matmul(x, g, sc, sh, w_bf16, rows_per_mod, tm, tn):
    n, d = x.shape
    no = w_bf16.shape[1]
    return pl.pallas_call(
        _nm_matmul_kernel, grid=(n // tm, no // tn),
        in_specs=[pl.BlockSpec((tm, d), lambda i, j: (i, 0)), pl.BlockSpec((1, d), lambda i, j: (0, 0)),
                  _mod_spec(d, tm, rows_per_mod), _mod_spec(d, tm, rows_per_mod),
                  pl.BlockSpec((d, tn), lambda i, j: (0, j))],
        out_specs=pl.BlockSpec((tm, tn), lambda i, j: (i, j)),
        out_shape=jax.ShapeDtypeStruct((n, no), F32),
        scratch_shapes=[pltpu.VMEM((tm, d), BF16)],
        compiler_params=_cp("parallel", "arbitrary"), name="normmod_matmul",
    )(x, g.reshape(1, d), sc, sh, w_bf16)


def _add_rows_kernel(x_ref, p_ref, o_ref):
    o_ref[...] = x_ref[...] + p_ref[...]


def add_pos(x, pe, seq_len, tm):
    n, d = x.shape
    nb = seq_len // tm
    return pl.pallas_call(
        _add_rows_kernel, grid=(n // tm,),
        in_specs=[pl.BlockSpec((tm, d), lambda i: (i, 0)), pl.BlockSpec((tm, d), lambda i: (i % nb, 0))],
        out_specs=pl.BlockSpec((tm, d), lambda i: (i, 0)),
        out_shape=jax.ShapeDtypeStruct((n, d), F32), compiler_params=_cp("parallel"), name="add_pos",
    )(x, pe)


def _sconv_kernel(x_ref, w_ref, b_ref, s_ref, o_ref, *, act):
    x = x_ref[...]
    n_tok = x.shape[0]
    row = lax.broadcasted_iota(jnp.int32, x.shape, 0)
    prev = jnp.where(row == 0, 0.0, pltpu.roll(x, 1, 0))
    nxt = jnp.where(row == n_tok - 1, 0.0, pltpu.roll(x, n_tok - 1, 0))
    y = prev * w_ref[0:1, :] + x * w_ref[1:2, :] + nxt * w_ref[2:3, :] + b_ref[...]
    if act:
        y = _silu(y) * s_ref[...]
    o_ref[...] = y.astype(o_ref.dtype)


def short_conv(a, ncols, w, b, scale, seq_len, act, out_dtype, cb=256):
    n = a.shape[0]
    return pl.pallas_call(
        functools.partial(_sconv_kernel, act=act), grid=(n // seq_len, ncols // cb),
        in_specs=[pl.BlockSpec((seq_len, cb), lambda s, j: (s, j)), pl.BlockSpec((3, cb), lambda s, j: (0, j)),
                  pl.BlockSpec((1, cb), lambda s, j: (0, j)), pl.BlockSpec((1, cb), lambda s, j: (0, j))],
        out_specs=pl.BlockSpec((seq_len, cb), lambda s, j: (s, j)),
        out_shape=jax.ShapeDtypeStruct((n, ncols), out_dtype), compiler_params=_cp("parallel", "parallel"),
        name="short_conv",
    )(a, w, b.reshape(1, ncols), scale.reshape(1, ncols))


def dft_tables(n_tok):
    k = jnp.arange(n_tok, dtype=jnp.int32)
    blk = 1 << ((n_tok.bit_length() - 1) // 2)
    def thin(n):
        ang = ((k[:, None] * n[None, :]) % (2 * n_tok)).astype(F32) * (math.pi / n_tok)
        return jnp.cos(ang), jnp.sin(ang)
    (c_hi, s_hi), (c_lo, s_lo) = thin(jnp.arange(0, n_tok, blk, dtype=jnp.int32)), thin(jnp.arange(blk, dtype=jnp.int32))
    cos_t = (c_hi[:, :, None] * c_lo[:, None, :] - s_hi[:, :, None] * s_lo[:, None, :]).reshape(n_tok, n_tok)
    msin = -(s_hi[:, :, None] * c_lo[:, None, :] + c_hi[:, :, None] * s_lo[:, None, :]).reshape(n_tok, n_tok)
    alt = jnp.where(k % 2 == 0, 1.0, -1.0).astype(F32)
    a_t = msin.at[0, :].set(alt)
    a_tt = msin.at[:, 0].set(alt)
    return cos_t.astype(BF16), a_t.astype(BF16), a_tt.astype(BF16)


def _hyfilt_kernel(band_ref, w1_ref, b1_ref, w2_ref, b2_ref, w3_ref, fr_ref, dec_ref, h_ref, ss_ref, *,
                   n_tok, tl, hw, nbands):
    i = pl.program_id(0)
    pos = i * tl + lax.broadcasted_iota(jnp.int32, (tl, 1), 0)
    t = pos.astype(F32) / n_tok
    lane = lax.broadcasted_iota(jnp.int32, (tl, LANES), 1)
    ang = 2.0 * math.pi * t * band_ref[...]
    z = jnp.where(lane == 0, t, jnp.where(lane <= nbands, jnp.cos(ang),
                                          jnp.where(lane <= 2 * nbands, jnp.sin(ang), 0.0)))
    fr = fr_ref[...]
    h = jnp.sin(fr * (_dot(z, w1_ref[...], precision=HIGHEST) + b1_ref[...]))
    h = jnp.sin(fr * (_dot(h, w2_ref[...], precision=HIGHEST) + b2_ref[...]))
    h = _dot(h, w3_ref[...], precision=HIGHEST) * jnp.exp(-t * jnp.abs(dec_ref[...]))
    col = lax.broadcasted_iota(jnp.int32, h.shape, 1)
    is_bwd = (col // hw) % 2 == 1
    h = jnp.where(jnp.logical_and(is_bwd, pos == 0), 0.0, h)
    h_ref[...] = h.astype(BF16)

    @pl.when(i == 0)
    def _():
        ss_ref[...] = jnp.zeros_like(ss_ref)
    ss_ref[...] += jnp.sum(h * h, axis=0, keepdims=True)


def hyena_filter_taps(n_tok, w1, b1, w2, b2, w3, freq, decay, hw):
    emb, ffn = w1.shape
    nbands = (emb - 1) // 2
    tl = min(n_tok, 512)
    bands = jnp.linspace(1e-4, nbands - 1, nbands, dtype=F32)
    band_row = jnp.zeros((1, LANES), F32).at[0, 1:1 + nbands].set(bands).at[0, 1 + nbands:1 + 2 * nbands].set(bands)
    w1p = jnp.zeros((LANES, ffn), F32).at[:emb].set(w1)
    nc = w3.shape[1]
    full = lambda shp: pl.BlockSpec(shp, lambda i: (0, 0))
    return pl.pallas_call(
        functools.partial(_hyfilt_kernel, n_tok=n_tok, tl=tl, hw=hw, nbands=nbands), grid=(n_tok // tl,),
        in_specs=[full((1, LANES)), full((LANES, ffn)), full((1, ffn)), full((ffn, ffn)), full((1, ffn)),
                  full((ffn, nc)), full((1, ffn)), full((1, nc))],
        out_specs=[pl.BlockSpec((tl, nc), lambda i: (i, 0)), full((1, nc))],
        out_shape=[jax.ShapeDtypeStruct((n_tok, nc), BF16), jax.ShapeDtypeStruct((1, nc), F32)],
        compiler_params=_cp("arbitrary"), name="hyena_filter_taps",
    )(band_row, w1p, b1.reshape(1, ffn), w2, b2.reshape(1, ffn), w3, freq.reshape(1, ffn), decay.reshape(1, nc))


def _filt_dft_kernel(c_ref, a_ref, h_ref, ss_ref, kr_ref, ki_ref, *, tf, hw):
    i = pl.program_id(1)
    hf = h_ref[:, :hw]
    hb = h_ref[:, hw:]
    cc = c_ref[...]
    aa = a_ref[...]
    zrf, zif, zrb, zib = _dot(cc, hf), _dot(aa, hf), _dot(cc, hb), _dot(aa, hb)
    scale = lax.rsqrt(ss_ref[:, :hw] + ss_ref[:, hw:] + EPS)
    first = (i * tf + lax.broadcasted_iota(jnp.int32, (tf, 1), 0)) == 0
    scale = scale * jnp.where(first, 0.5, 1.0)
    kr_ref[0] = (zrf + zrb) * scale
    ki_ref[0] = jnp.where(first, zif + zib, zif - zib) * scale


def hyena_filter_spectrum(cos_t, a_t, taps, sumsq, hw, tf):
    n_tok = cos_t.shape[0]
    norder = taps.shape[1] // (2 * hw)
    out = jax.ShapeDtypeStruct((norder, n_tok, hw), F32)
    return pl.pallas_call(
        functools.partial(_filt_dft_kernel, tf=tf, hw=hw), grid=(norder, n_tok // tf),
        in_specs=[pl.BlockSpec((tf, n_tok), lambda o, i: (i, 0)), pl.BlockSpec((tf, n_tok), lambda o, i: (i, 0)),
                  pl.BlockSpec((n_tok, 2 * hw), lambda o, i: (0, o)), pl.BlockSpec((1, 2 * hw), lambda o, i: (0, o))],
        out_specs=[pl.BlockSpec((1, tf, hw), lambda o, i: (o, i, 0))] * 2,
        out_shape=[out, out], compiler_params=_cp("parallel", "parallel"), name="hyena_filter_spectrum",
    )(cos_t, a_t, taps, sumsq)


def _hy_fwd_kernel(c_ref, a_ref, z_ref, kr_ref, ki_ref, yr_ref, yi_ref, *, tf):
    i = pl.program_id(0)
    zb = z_ref[...].astype(BF16)
    zr = _dot(c_ref[...], zb)
    zi = _dot(a_ref[...], zb)
    kr = kr_ref[0]
    ki = ki_ref[0]
    first = (i * tf + lax.broadcasted_iota(jnp.int32, (tf, 1), 0)) == 0
    yr_ref[...] = jnp.where(first, zr * kr, zr * kr - zi * ki).astype(BF16)
    yi_ref[...] = jnp.where(first, zi * ki, zr * ki + zi * kr).astype(BF16)


def hyena_fwd(cos_t, a_t, z, zcol, kr, ki, order, nseq, hw, tf):
    n_tok = cos_t.shape[0]
    nf = n_tok // tf
    out = jax.ShapeDtypeStruct((nseq * n_tok, hw), BF16)
    return pl.pallas_call(
        functools.partial(_hy_fwd_kernel, tf=tf), grid=(nf, nseq),
        in_specs=[pl.BlockSpec((tf, n_tok), lambda i, b: (i, 0)), pl.BlockSpec((tf, n_tok), lambda i, b: (i, 0)),
                  pl.BlockSpec((n_tok, hw), lambda i, b: (b, zcol)),
                  pl.BlockSpec((1, tf, hw), lambda i, b: (order, i, 0)),
                  pl.BlockSpec((1, tf, hw), lambda i, b: (order, i, 0))],
        out_specs=[pl.BlockSpec((tf, hw), lambda i, b: (b * nf + i, 0))] * 2,
        out_shape=[out, out], compiler_params=_cp("parallel", "parallel"), name="hyena_fwd",
    )(cos_t, a_t, z, kr, ki)


def _hy_inv_kernel(c_ref, at_ref, yr_ref, yi_ref, zp_ref, gate_ref, bias_ref, o_ref, *, inv_len):
    conv = (_dot(c_ref[...], yr_ref[...]) + _dot(at_ref[...], yi_ref[...])) * inv_len
    o_ref[...] = gate_ref[...] * (conv + bias_ref[...] * zp_ref[...])


def hyena_inv(cos_t, a_tt, yr, yi, zprev, zcol, gates, gcol, bias_row, nseq, hw, tf):
    n_tok = cos_t.shape[0]
    nf = n_tok // tf
    return pl.pallas_call(
        functools.partial(_hy_inv_kernel, inv_len=1.0 / n_tok), grid=(nf, nseq),
        in_specs=[pl.BlockSpec((tf, n_tok), lambda i, b: (i, 0)), pl.BlockSpec((tf, n_tok), lambda i, b: (i, 0)),
                  pl.BlockSpec((n_tok, hw), lambda i, b: (b, 0)), pl.BlockSpec((n_tok, hw), lambda i, b: (b, 0)),
                  pl.BlockSpec((tf, hw), lambda i, b: (b * nf + i, zcol)),
                  pl.BlockSpec((tf, hw), lambda i, b: (b * nf + i, gcol)),
                  pl.BlockSpec((1, hw), lambda i, b: (0, 0))],
        out_specs=pl.BlockSpec((tf, hw), lambda i, b: (b * nf + i, 0)),
        out_shape=jax.ShapeDtypeStruct((nseq * n_tok, hw), F32),
        compiler_params=_cp("parallel", "parallel"), name="hyena_inv",
    )(cos_t, a_tt, yr, yi, zprev, gates, bias_row)


S5_DIAG_BLOCKS = 2


S5_SEQS_PER_STEP = 4


def _s5_kernel(u_ref, bre_ref, bim_ref, cre_ref, cim_ref, lam_ref, h0_ref, y_ref, hfin_ref, hre_s, him_s, st_s, *,
               tc, nc, ns, nb):
    d = pl.program_id(0)
    c = pl.program_id(2)

    @pl.when(c == 0)
    def _():
        st_s[...] = h0_ref[:, 0]

    sw = u_ref.shape[2]
    halves = [(slice(j * sw // S5_DIAG_BLOCKS, (j + 1) * sw // S5_DIAG_BLOCKS),
               slice(j * ns // S5_DIAG_BLOCKS, (j + 1) * ns // S5_DIAG_BLOCKS)) for j in range(S5_DIAG_BLOCKS)]
    for j in range(nb):
        ub = u_ref[j].astype(BF16)
        for us, hs in halves:
            hre_s[j, :, hs] = _dot(ub[:, us], bre_ref[0, us, hs])
            him_s[j, :, hs] = _dot(ub[:, us], bim_ref[0, us, hs])
    lr = lam_ref[0, 0:1, :]
    li = lam_ref[0, 1:2, :]

    def body(t, carry):
        r = jnp.where(d == 0, t, tc - 1 - t)
        new = []
        for j, (hr, hi) in enumerate(carry):
            nr = lr * hr - li * hi + hre_s[j, pl.ds(r, 1), :]
            ni = lr * hi + li * hr + him_s[j, pl.ds(r, 1), :]
            hre_s[j, pl.ds(r, 1), :] = nr
            him_s[j, pl.ds(r, 1), :] = ni
            new.append((nr, ni))
        return tuple(new)

    start = tuple((st_s[j, 0:1, :], st_s[j, 1:2, :]) for j in range(nb))
    for j, (hr, hi) in enumerate(lax.fori_loop(0, tc, body, start, unroll=4)):
        st_s[j, 0:1, :] = hr
        st_s[j, 1:2, :] = hi
    for j in range(nb):
        for us, hs in halves:
            y_ref[0, j, :, us] = (_dot(hre_s[j, :, hs].astype(BF16), cre_ref[0, hs, us])
                                  + _dot(him_s[j, :, hs].astype(BF16), cim_ref[0, hs, us]))

    @pl.when(c == nc - 1)
    def _():
        hfin_ref[:, 0] = st_s[...]


def s5_scan(proj, ucol, bre, bim, cre, cim, lam, h0, nseq, seq_len, sw, tc):
    ns = bre.shape[2]
    nc = seq_len // tc
    nb = S5_SEQS_PER_STEP if nseq % S5_SEQS_PER_STEP == 0 else 1

    def chunk(d, c):
        return c + d * (nc - 1 - 2 * c)

    y, hfin = pl.pallas_call(
        functools.partial(_s5_kernel, tc=tc, nc=nc, ns=ns, nb=nb), grid=(2, nseq // nb, nc),
        in_specs=[pl.BlockSpec((nb, tc, sw), lambda d, b, c: (b, chunk(d, c), ucol)),
                  pl.BlockSpec((1, sw, ns), lambda d, b, c: (d, 0, 0)),
                  pl.BlockSpec((1, sw, ns), lambda d, b, c: (d, 0, 0)),
                  pl.BlockSpec((1, ns, sw), lambda d, b, c: (d, 0, 0)),
                  pl.BlockSpec((1, ns, sw), lambda d, b, c: (d, 0, 0)),
                  pl.BlockSpec((1, 2, ns), lambda d, b, c: (d, 0, 0)),
                  pl.BlockSpec((nb, 1, 2, ns), lambda d, b, c: (b, d, 0, 0))],
        out_specs=[pl.BlockSpec((1, nb, tc, sw), lambda d, b, c: (d, b, chunk(d, c), 0)),
                   pl.BlockSpec((nb, 1, 2, ns), lambda d, b, c: (b, d, 0, 0))],
        out_shape=[jax.ShapeDtypeStruct((2, nseq, seq_len, sw), F32), jax.ShapeDtypeStruct((nseq, 2, 2, ns), F32)],
        scratch_shapes=[pltpu.VMEM((nb, tc, ns), F32), pltpu.VMEM((nb, tc, ns), F32), pltpu.VMEM((nb, 2, ns), F32)],
        compiler_params=_cp("parallel", "parallel", "arbitrary"), name="s5_scan",
    )(proj.reshape(nseq, seq_len, proj.shape[1]), bre, bim, cre, cim, lam, h0)
    return y.reshape(2, nseq * seq_len, sw), hfin


def _s5_glu_kernel(yf_ref, yb_ref, u_ref, d_ref, w_ref, b_ref, o_ref):
    y = jax.nn.gelu(yf_ref[0] + yb_ref[0] + d_ref[...] * u_ref[...])
    o_ref[...] = y * jax.nn.sigmoid(_dot(y.astype(BF16), w_ref[...]) + b_ref[...])


def s5_glu(y2, proj, ucol, d_skip, glu_w_bf16, glu_b, tm):
    _, n, sw = y2.shape
    return pl.pallas_call(
        _s5_glu_kernel, grid=(n // tm,),
        in_specs=[pl.BlockSpec((1, tm, sw), lambda i: (0, i, 0)), pl.BlockSpec((1, tm, sw), lambda i: (1, i, 0)),
                  pl.BlockSpec((tm, sw), lambda i: (i, ucol)), pl.BlockSpec((1, sw), lambda i: (0, 0)),
                  pl.BlockSpec((sw, sw), lambda i: (0, 0)), pl.BlockSpec((1, sw), lambda i: (0, 0))],
        out_specs=pl.BlockSpec((tm, sw), lambda i: (i, 0)),
        out_shape=jax.ShapeDtypeStruct((n, sw), F32), compiler_params=_cp("parallel"), name="s5_glu",
    )(y2, y2, proj, d_skip.reshape(1, sw), glu_w_bf16, glu_b.reshape(1, sw))


def _residual_and_next_norm(x_ref, gate_ref, y, g2_ref, sc2_ref, sh2_ref, o_ref, hn_ref):
    xn = x_ref[...] + gate_ref[0] * y
    o_ref[...] = xn
    hn_ref[...] = _normmod(xn, g2_ref[...], sc2_ref[0], sh2_ref[0]).astype(BF16)


def _row_specs(d, tm, rows_per_mod):
    mod = pl.BlockSpec((1, 1, d), lambda i: ((i * tm) // rows_per_mod, 0, 0))
    return [pl.BlockSpec((tm, d), lambda i: (i, 0)), mod, pl.BlockSpec((1, d), lambda i: (0, 0)), mod, mod]


def _row_outs(n, d, tm):
    spec = pl.BlockSpec((tm, d), lambda i: (i, 0))
    return [spec, spec], [jax.ShapeDtypeStruct((n, d), F32), jax.ShapeDtypeStruct((n, d), BF16)]


def _even_out_kernel(a_ref, b_ref, wa_ref, wb_ref, x_ref, gate_ref, g2_ref, sc2_ref, sh2_ref, o_ref, hn_ref):
    y = _dot(a_ref[...].astype(BF16), wa_ref[...]) + _dot(b_ref[...].astype(BF16), wb_ref[...])
    _residual_and_next_norm(x_ref, gate_ref, y, g2_ref, sc2_ref, sh2_ref, o_ref, hn_ref)


def even_out(hy, s5o, w_bf16, x, gate, g2, sc2, sh2, rows_per_mod, tm):
    n, d = x.shape
    hw = hy.shape[1]
    sw = s5o.shape[1]
    out_specs, out_shape = _row_outs(n, d, tm)
    return pl.pallas_call(
        _even_out_kernel, grid=(n // tm,),
        in_specs=[pl.BlockSpec((tm, hw), lambda i: (i, 0)), pl.BlockSpec((tm, sw), lambda i: (i, 0)),
                  pl.BlockSpec((hw, d), lambda i: (0, 0)), pl.BlockSpec((sw, d), lambda i: (hw // sw, 0))]
        + _row_specs(d, tm, rows_per_mod),
        out_specs=out_specs, out_shape=out_shape, compiler_params=_cp("parallel"), name="even_out",
    )(hy, s5o, w_bf16, w_bf16, x, gate, g2.reshape(1, d), sc2, sh2)


ML_HEAD_GROUP = 8


def _log_sigmoid(x):
    return jnp.minimum(x, 0.0) - jnp.log1p(jnp.exp(-jnp.abs(x)))


def _mlstm_kernel(q_ref, k_ref, v_ref, g_ref, gb_ref, c0_ref, n0_ref, m0_ref, h_ref, cf_ref, nf_ref, mf_ref,
                  c_s, m_s, *, nh, dh, tc, nc):
    d = pl.program_id(0)
    c = pl.program_id(2)

    @pl.when(c == 0)
    def _():
        for h in range(nh):
            c_s[h, :, :dh] = c0_ref[0, 0, h]
            c_s[h, :, dh:] = jnp.broadcast_to(n0_ref[0, 0, h:h + 1, :], (dh, dh)).T
        m_s[...] = m0_ref[0, 0]

    ones = jnp.ones((tc, dh), F32)
    gates = g_ref[...] + gb_ref[0]
    lane = lax.broadcasted_iota(jnp.int32, gates.shape, 1)
    logf = jnp.where(jnp.logical_and(lane >= nh, lane < 2 * nh), _log_sigmoid(gates), 0.0)
    r_i = lax.broadcasted_iota(jnp.int32, (tc, tc), 0)
    s_i = lax.broadcasted_iota(jnp.int32, (tc, tc), 1)
    causal = (r_i - s_i) * (1 - 2 * d) >= 0
    bcum = _dot(causal.astype(F32), logf, precision=HIGHEST)
    btot = jnp.sum(logf, axis=0, keepdims=True)
    gates_t = gates.T
    bcum_t = bcum.T
    for g0 in range(0, nh, ML_HEAD_GROUP):
        hds = list(range(g0, min(g0 + ML_HEAD_GROUP, nh)))
        hsl = {h: slice(h * dh, (h + 1) * dh) for h in hds}
        b_col = {h: bcum[:, nh + h:nh + h + 1] for h in hds}
        m_old = {h: m_s[h:h + 1, :] for h in hds}
        a = {h: b_col[h] + m_old[h] for h in hds}
        src = {h: jnp.where(causal, gates_t[h:h + 1, :] - bcum_t[nh + h:nh + h + 1, :], NEG_INF) for h in hds}
        mq = {h: jnp.maximum(a[h], b_col[h] + jnp.max(src[h], axis=-1, keepdims=True)) for h in hds}
        s = {h: _dot_nt(q_ref[:, hsl[h]], k_ref[:, hsl[h]]) * jnp.exp(src[h] + (b_col[h] - mq[h])) for h in hds}
        v1 = {h: jnp.concatenate([v_ref[:, hsl[h]], ones], axis=1) for h in hds}
        cn = {h: c_s[h] for h in hds}
        qw = {h: jnp.exp(a[h] - mq[h]) * q_ref[:, hsl[h]] for h in hds}
        both = {h: _dot(s[h], v1[h]) + _dot(qw[h], cn[h]) for h in hds}
        for h in hds:
            h_ref[0, :, hsl[h]] = both[h][:, :dh] / jnp.maximum(jnp.abs(both[h][:, dh:]), jnp.exp(-mq[h]))
        b_last = {h: btot[:, nh + h:nh + h + 1] for h in hds}
        g = {h: b_last[h] - b_col[h] + gates[:, h:h + 1] for h in hds}
        m_new = {h: jnp.maximum(b_last[h] + m_old[h], jnp.max(g[h], axis=0, keepdims=True)) for h in hds}
        kw = {h: k_ref[:, hsl[h]] * jnp.exp(g[h] - m_new[h]) for h in hds}
        for h in hds:
            c_s[h] = jnp.exp(b_last[h] + m_old[h] - m_new[h]) * cn[h] + _dot(kw[h].T, v1[h])
            m_s[h:h + 1, :] = m_new[h]

    @pl.when(c == nc - 1)
    def _():
        for h in range(nh):
            cf_ref[0, 0, h] = c_s[h, :, :dh]
            nf_ref[0, 0, h:h + 1, :] = c_s[h, :, dh:].T[0:1, :]
        mf_ref[0, 0] = m_s[...]


def mlstm_scan(qk, proj, vcol, gate_bias, c0, n0, m0, nseq, seq_len, nh, dh):
    tc = ML_CHUNK
    nc = seq_len // tc
    w = nh * dh

    def chunk(d, c):
        return c + d * (nc - 1 - 2 * c)

    rowblk = lambda d, b, c: b * nc + chunk(d, c)
    st = lambda shp: pl.BlockSpec((1, 1) + shp, lambda d, b, c: (b, d) + (0,) * len(shp))
    return pl.pallas_call(
        functools.partial(_mlstm_kernel, nh=nh, dh=dh, tc=tc, nc=nc), grid=(2, nseq, nc),
        in_specs=[pl.BlockSpec((tc, w), lambda d, b, c: (rowblk(d, b, c), 0)),
                  pl.BlockSpec((tc, w), lambda d, b, c: (rowblk(d, b, c), 1)),
                  pl.BlockSpec((tc, w), lambda d, b, c: (rowblk(d, b, c), vcol)),
                  pl.BlockSpec((tc, LANES), lambda d, b, c: (rowblk(d, b, c), 4 * w // LANES + d)),
                  pl.BlockSpec((1, 1, LANES), lambda d, b, c: (d, 0, 0)),
                  st((nh, dh, dh)), st((nh, dh)), st((nh, 1))],
        out_specs=[pl.BlockSpec((1, tc, w), lambda d, b, c: (d, rowblk(d, b, c), 0)),
                   st((nh, dh, dh)), st((nh, dh)), st((nh, 1))],
        out_shape=[jax.ShapeDtypeStruct((2, nseq * seq_len, w), F32),
                   jax.ShapeDtypeStruct((nseq, 2, nh, dh, dh), F32), jax.ShapeDtypeStruct((nseq, 2, nh, dh), F32),
                   jax.ShapeDtypeStruct((nseq, 2, nh, 1), F32)],
        scratch_shapes=[pltpu.VMEM((nh, dh, 2 * dh), F32), pltpu.VMEM((nh, 1), F32)],
        compiler_params=_cp("parallel", "parallel", "arbitrary"), name="mlstm_scan",
    )(qk, qk, proj, proj, gate_bias, c0, n0, m0)


def _odd_out_kernel(hf_ref, hb_ref, og_ref, ng_ref, w_ref, x_ref, gate_ref, g2_ref, sc2_ref, sh2_ref, o_ref, hn_ref,
                    a_s, *, nh, dh):
    for h in range(nh):
        hs = slice(h * dh, (h + 1) * dh)
        blk = hf_ref[0, :, hs] + hb_ref[0, :, hs]
        blk = blk * lax.rsqrt(jnp.mean(blk * blk, axis=-1, keepdims=True) + EPS)
        a_s[:, hs] = ((blk * ng_ref[:, hs]) * _silu(og_ref[:, hs])).astype(BF16)
    _residual_and_next_norm(x_ref, gate_ref, _dot(a_s[...], w_ref[...]), g2_ref, sc2_ref, sh2_ref, o_ref, hn_ref)


def odd_out(h2, proj, ocol, norm_g, w_bf16, x, gate, g2, sc2, sh2, rows_per_mod, nh, dh, tm):
    n, d = x.shape
    w = nh * dh
    out_specs, out_shape = _row_outs(n, d, tm)
    return pl.pallas_call(
        functools.partial(_odd_out_kernel, nh=nh, dh=dh), grid=(n // tm,),
        in_specs=[pl.BlockSpec((1, tm, w), lambda i: (0, i, 0)), pl.BlockSpec((1, tm, w), lambda i: (1, i, 0)),
                  pl.BlockSpec((tm, w), lambda i: (i, ocol)), pl.BlockSpec((1, w), lambda i: (0, 0)),
                  pl.BlockSpec((w, d), lambda i: (0, 0))] + _row_specs(d, tm, rows_per_mod),
        out_specs=out_specs, out_shape=out_shape, scratch_shapes=[pltpu.VMEM((tm, w), BF16)],
        compiler_params=_cp("parallel"), name="odd_out",
    )(h2, h2, proj, norm_g.reshape(1, w), w_bf16, x, gate, g2.reshape(1, d), sc2, sh2)


def _sort_network(n):
    pairs, p = [], 1
    while p < n:
        k = p
        while k >= 1:
            for j in range(k % p, n - k, 2 * k):
                for i in range(min(k, n - j - k)):
                    if (i + j) // (2 * p) == (i + j + k) // (2 * p):
                        pairs.append((i + j, i + j + k))
            k //= 2
        p *= 2
    return pairs


def _top_values_tiled(arrays, k, outs):
    cols = []
    for arr in arrays:
        tiles = [arr[r:r + SUBLANES, :] for r in range(0, arr.shape[0], SUBLANES)]
        for lo, hi in _sort_network(len(tiles)):
            tiles[lo], tiles[hi] = jnp.maximum(tiles[lo], tiles[hi]), jnp.minimum(tiles[lo], tiles[hi])
        cols.append(tiles)
    for j in range(k):
        for tiles, out_s in zip(cols, outs):
            m = jnp.max(tiles[0], axis=0, keepdims=True)
            out_s[j:j + 1, :] = m
            hit = tiles[0] == m
            for i in range(min(len(tiles), k - 1 - j)):
                below = tiles[i + 1] if i + 1 < len(tiles) else NEG_INF
                tiles[i] = jnp.where(hit, below, tiles[i])


def _pair_candidates(k):
    return [(a, k // (a + 1)) for a in range(k)]


PEER_HEADS_PER_TRIP = 8


def _peer_score_kernel(h_ref, wq_ref, keys_ref, th_ref, s2_ref, w1_ref, w2_ref, q_s, v_s, cand_s, best_s, *,
                       nh, half, topk):
    q_s[...] = _dot_nt(wq_ref[...], h_ref[...])
    kk = topk + 1
    cand_s[...] = jnp.full(cand_s.shape, NEG_INF, F32)
    group = range(PEER_HEADS_PER_TRIP)

    def heads(trip, carry):
        hds = [trip * PEER_HEADS_PER_TRIP + u for u in group]
        scores = []
        for hd in hds:
            base = pl.multiple_of(hd * 2 * half, 2 * half)
            scores.append(_dot(keys_ref[hd, 0], q_s[pl.ds(base, half), :]))
            scores.append(_dot(keys_ref[hd, 1], q_s[pl.ds(base + half, half), :]))
        _top_values_tiled(scores, kk, [v_s.at[u, c] for u in group for c in range(2)])
        for u in group:
            off = 0
            for a, cnt in _pair_candidates(kk):
                cand_s[u, off:off + cnt, :] = v_s[u, 0, a:a + 1, :] + v_s[u, 1, 0:cnt, :]
                off += cnt
        _top_values_tiled([cand_s[u] for u in group], kk, [best_s.at[u] for u in group])
        for u, hd in enumerate(hds):
            s1, s2 = scores[2 * u], scores[2 * u + 1]
            best = best_s[u, 0:topk, :]
            z = jnp.sum(jnp.exp(best - best[0:1, :]), axis=0, keepdims=True)
            tmid = 0.5 * (best_s[u, topk - 1:topk, :] + best_s[u, topk:topk + 1, :])
            th = tmid - s1
            w1 = jnp.exp(s1 - v_s[u, 0, 0:1, :]) / z
            w2 = jnp.exp(s2 - v_s[u, 1, 0:1, :])
            for lt in range(s1.shape[1] // LANES):
                sl = slice(lt * LANES, (lt + 1) * LANES)
                th_ref[hd, lt] = th[:, sl]
                s2_ref[hd, lt] = s2[:, sl]
                w1_ref[hd, lt] = w1[:, sl]
                w2_ref[hd, lt] = w2[:, sl]
        return carry

    lax.fori_loop(0, nh // PEER_HEADS_PER_TRIP, heads, 0)


def peer_scores(h_bf16, wq_t_bf16, keys, tt):
    n, d = h_bf16.shape
    nh, _, nk, half = keys.shape
    kk = PK_TOPK + 1
    ncand = SUBLANES
    while ncand < sum(c for _, c in _pair_candidates(kk)):
        ncand *= 2
    big = jax.ShapeDtypeStruct((nh, n // LANES, nk, LANES), F32)
    bspec = pl.BlockSpec((nh, tt // LANES, nk, LANES), lambda i: (0, i, 0, 0))
    return pl.pallas_call(
        functools.partial(_peer_score_kernel, nh=nh, half=half, topk=PK_TOPK), grid=(n // tt,),
        in_specs=[pl.BlockSpec((tt, d), lambda i: (i, 0)), pl.BlockSpec((nh * 2 * half, d), lambda i: (0, 0)),
                  pl.BlockSpec((nh, 2, nk, half), lambda i: (0, 0, 0, 0))],
        out_specs=[bspec, bspec, bspec, bspec],
        out_shape=[big, big, big, big],
        scratch_shapes=[pltpu.VMEM((nh * 2 * half, tt), F32), pltpu.VMEM((PEER_HEADS_PER_TRIP, 2, 24, tt), F32),
                        pltpu.VMEM((PEER_HEADS_PER_TRIP, ncand, tt), F32), pltpu.VMEM((PEER_HEADS_PER_TRIP, 24, tt), F32)],
        compiler_params=_cp("parallel"), name="peer_scores",
    )(h_bf16, wq_t_bf16, keys)


PEER_KEY_ROWS = 16
GELU_C1 = math.sqrt(2.0 / math.pi)
GELU_C2 = 0.044715 * GELU_C1


def _gelu_tanh(x):
    half_x = 0.5 * x
    return half_x + half_x * jnp.tanh(x * (GELU_C1 + GELU_C2 * (x * x)))


def _peer_dense_kernel(h_ref, u_ref, vt_ref, th_ref, s2_ref, w1_ref, w2_ref, x_ref, gate_ref, fg_ref, o_ref,
                       acc_s, st_s, wt_s, *, nh, nk, ec, tt, final):
    e = pl.program_id(1)
    nlt = tt // LANES
    n_i1 = ec // nk
    nkt = nk // PEER_KEY_ROWS
    nsub = PEER_KEY_ROWS // SUBLANES

    @pl.when(e == 0)
    def _():
        acc_s[...] = jnp.zeros_like(acc_s)

    st = _gelu_tanh(_dot_nt(u_ref[...], h_ref[...]))
    for lt in range(nlt):
        st_s[lt] = st[:, lt * LANES:(lt + 1) * LANES]

    def tile(idx, carry):
        lt = idx // nkt
        k0 = (idx % nkt) * PEER_KEY_ROWS
        subs = [pl.ds(pl.multiple_of(k0 + j * SUBLANES, SUBLANES), SUBLANES) for j in range(nsub)]
        g = [[jnp.zeros((SUBLANES, LANES), F32) for _ in subs] for _ in range(n_i1)]
        for hd in range(nh):
            s2t = [s2_ref[hd, lt, sub, :] for sub in subs]
            w2t = [w2_ref[hd, lt, sub, :] for sub in subs]
            for li in range(n_i1):
                thb = jnp.broadcast_to(th_ref[hd, lt, li:li + 1, :], (SUBLANES, LANES))
                w1b = jnp.broadcast_to(w1_ref[hd, lt, li:li + 1, :], (SUBLANES, LANES))
                for j in range(nsub):
                    g[li][j] = g[li][j] + jnp.where(s2t[j] >= thb, w2t[j] * w1b, 0.0)
        for li in range(n_i1):
            rows = pl.ds(pl.multiple_of(li * nk + k0, PEER_KEY_ROWS), PEER_KEY_ROWS)
            wt_s[lt, rows, :] = (st_s[lt, rows, :] * jnp.concatenate(g[li], axis=0)).astype(BF16)
        return carry

    lax.fori_loop(0, nlt * nkt, tile, 0)
    wt = jnp.concatenate([wt_s[lt] for lt in range(nlt)], axis=1)
    acc_s[...] += _dot(vt_ref[0], wt)

    @pl.when(e == pl.num_programs(1) - 1)
    def _():
        xn = x_ref[...] + gate_ref[0] * acc_s[...].T
        if final:
            xn = (xn * lax.rsqrt(jnp.mean(xn * xn, axis=-1, keepdims=True) + EPS)) * fg_ref[...]
        o_ref[...] = xn


def peer_dense(h_bf16, u_bf16, vt_bf16, th, s2, w1, w2, x, gate, final_g, final, rows_per_mod, tt, ec):
    n, d = x.shape
    nh, _, nk, _ = s2.shape
    nchunk = u_bf16.shape[0] // ec
    bspec = pl.BlockSpec((nh, tt // LANES, nk, LANES), lambda i, e: (0, i, 0, 0))
    rspec = pl.BlockSpec((nh, tt // LANES, ec // nk, LANES), lambda i, e: (0, i, e, 0))
    tile_buf = (tt // LANES, ec, LANES)
    return pl.pallas_call(
        functools.partial(_peer_dense_kernel, nh=nh, nk=nk, ec=ec, tt=tt, final=final), grid=(n // tt, nchunk),
        in_specs=[pl.BlockSpec((tt, d), lambda i, e: (i, 0)), pl.BlockSpec((ec, d), lambda i, e: (e, 0)),
                  pl.BlockSpec((1, d, ec), lambda i, e: (e, 0, 0)), rspec, bspec, rspec, bspec,
                  pl.BlockSpec((tt, d), lambda i, e: (i, 0)),
                  pl.BlockSpec((1, 1, d), lambda i, e: ((i * tt) // rows_per_mod, 0, 0)),
                  pl.BlockSpec((1, d), lambda i, e: (0, 0))],
        out_specs=pl.BlockSpec((tt, d), lambda i, e: (i, 0)),
        out_shape=jax.ShapeDtypeStruct((n, d), F32),
        scratch_shapes=[pltpu.VMEM((d, tt), F32), pltpu.VMEM(tile_buf, F32), pltpu.VMEM(tile_buf, BF16)],
        compiler_params=_cp("parallel", "arbitrary"), name="peer_dense",
    )(h_bf16, u_bf16, vt_bf16, th, s2, w1, w2, x, gate, final_g.reshape(1, d))


def _s5_params(a_re, a_im, b_re, b_im, c_re, c_im, log_step):
    lam = lax.complex(a_re.astype(F32), a_im.astype(F32))
    lam_bar = jnp.exp(lam * jnp.exp(log_step.astype(F32))[..., None])
    b_bar = ((lam_bar - 1.0) / lam)[..., None] * lax.complex(b_re.astype(F32), b_im.astype(F32))
    ngrp, npst, nch = b_bar.shape[1:]
    eye = jnp.eye(ngrp, dtype=F32)

    def b_mat(part):
        return jnp.einsum("dgpj,gh->dgjhp", part, eye).reshape(2, ngrp * nch, ngrp * npst)

    def c_mat(part):
        return jnp.einsum("dgjp,gh->dgphj", part, eye).reshape(2, ngrp * npst, ngrp * nch)

    bre, bim = b_mat(b_bar.real).astype(BF16), b_mat(b_bar.imag).astype(BF16)
    cre, cim = c_mat(c_re.astype(F32)).astype(BF16), c_mat(-c_im.astype(F32)).astype(BF16)
    lam2 = jnp.stack([lam_bar.real.reshape(2, -1), lam_bar.imag.reshape(2, -1)], axis=1)
    return bre, bim, cre, cim, lam2


def _pos_embed(n_tok, d, grid_w):
    rows = n_tok // grid_w
    quarter = d // 4
    omega = 1.0 / (10000.0 ** (jnp.arange(quarter, dtype=F32) / quarter))

    def emb1d(pos):
        ang = pos.astype(F32)[:, None] * omega[None]
        return jnp.concatenate([jnp.sin(ang), jnp.cos(ang)], axis=-1)

    er = emb1d(jnp.arange(rows))
    ec = emb1d(jnp.arange(grid_w))
    half = d // 2
    pe = jnp.concatenate([jnp.broadcast_to(er[:, None], (rows, grid_w, half)),
                          jnp.broadcast_to(ec[None], (rows, grid_w, half))], axis=-1)
    return pe.reshape(rows * grid_w, d)


def _tile(n, pref):
    return pref if n % pref == 0 else n


def _trunk(x, mods, s5_h0, ml_c0, ml_n0, ml_m0, p, nseq, seq_len, rows_per_mod):
    n, d = x.shape
    tm = _tile(min(rows_per_mod, n), 512)
    depth = p["norm_g"].shape[0]
    s5_fin, ml_fin = [], []
    for l in range(depth):
        sh1, sc1, g1, sh2, sc2, g2 = mods[l]
        i = l // 2
        if l % 2 == 0:
            hw = p["hy_bias"].shape[2]
            sw = p["s5_d"].shape[1]
            proj = normmod_matmul(x, p["norm_g"][l, 0], sc1, sh1, p["ev_w_in"][i].astype(BF16), rows_per_mod, tm,
                                  3 * hw + sw)
            hy_in = short_conv(proj, 3 * hw, p["hy_conv_w"][i], p["hy_conv_b"][i], jnp.ones((3 * hw,), F32),
                               seq_len, act=False, out_dtype=F32)
            cos_t, a_t, a_tt = dft_tables(seq_len)
            tf = _tile(seq_len, 512)
            taps, sumsq = hyena_filter_taps(seq_len, p["hy_w1"][i], p["hy_b1"][i], p["hy_w2"][i], p["hy_b2"][i],
                                            p["hy_w3"][i], p["hy_freq"][i], p["hy_decay"][i], hw)
            kr, ki = hyena_filter_spectrum(cos_t, a_t, taps, sumsq, hw, tf)
            bias = p["hy_bias"][i].astype(F32)
            z, zcol = hy_in, 0
            for o in range(bias.shape[0]):
                yr, yi = hyena_fwd(cos_t, a_t, z, zcol, kr, ki, o, nseq, hw, tf)
                z = hyena_inv(cos_t, a_tt, yr, yi, z, zcol, hy_in, 1 + o, bias[o:o + 1], nseq, hw, tf)
                zcol = 0
            bre, bim, cre, cim, lam2 = _s5_params(p["s5_a_re"][i], p["s5_a_im"][i], p["s5_b_re"][i], p["s5_b_im"][i],
                                                  p["s5_c_re"][i], p["s5_c_im"][i], p["s5_log_step"][i])
            ucol = 3 * hw // sw
            y2, hfin = s5_scan(proj, ucol, bre, bim, cre, cim, lam2, s5_h0[i], nseq, seq_len, sw, _tile(seq_len, 256))
            s5_fin.append(hfin)
            s5o = s5_glu(y2, proj, ucol, p["s5_d"][i], p["s5_glu_w"][i].astype(BF16), p["s5_glu_b"][i], tm)
            x, hn = even_out(z, s5o, p["ev_w_out"][i].astype(BF16), x, g1, p["norm_g"][l, 1], sc2, sh2, rows_per_mod, tm)
        else:
            nh = p["od_gate_b"].shape[2]
            w = p["ml_norm_g"].shape[1]
            dh = w // nh
            w_in = p["od_w_in"][i]
            wg = w_in[:, 4 * w:].reshape(d, 4, nh)
            gb = p["od_gate_b"][i].astype(F32)
            wg2 = jnp.zeros((d, 2, LANES), w_in.dtype)
            bg2 = jnp.zeros((2, 1, LANES), F32)
            for dr in range(2):
                wg2 = wg2.at[:, dr, :nh].set(wg[:, dr]).at[:, dr, nh:2 * nh].set(wg[:, 2 + dr])
                bg2 = bg2.at[dr, 0, :nh].set(gb[dr]).at[dr, 0, nh:2 * nh].set(gb[2 + dr])
            w_all = jnp.concatenate([w_in[:, :4 * w], wg2.reshape(d, 2 * LANES)], axis=1).astype(BF16)
            proj = normmod_matmul(x, p["norm_g"][l, 0], sc1, sh1, w_all, rows_per_mod, tm, w_all.shape[1] // 2)
            qscale = jnp.concatenate([jnp.full((w,), dh ** -0.5, F32), jnp.ones((w,), F32)])
            qk = short_conv(proj, 2 * w, p["ml_conv_w"][i], p["ml_conv_b"][i], qscale, seq_len, act=True, out_dtype=BF16)
            h2, cf, nf, mf = mlstm_scan(qk, proj, 2, bg2, ml_c0[i], ml_n0[i], ml_m0[i], nseq, seq_len, nh, dh)
            ml_fin.append((cf, nf, mf))
            x, hn = odd_out(h2, proj, 3, p["ml_norm_g"][i], p["od_w_out"][i].astype(BF16), x, g1, p["norm_g"][l, 1],
                            sc2, sh2, rows_per_mod, nh, dh, tm)
        tt = _tile(min(rows_per_mod, n), 512)
        th, s2, w1, w2 = peer_scores(hn, p["pk_w_q"][l].T.astype(BF16), p["pk_keys"][l].astype(F32), _tile(tt, 256))
        ec = 2048
        vt = p["pk_v"][l].astype(BF16).reshape(-1, ec, d).transpose(0, 2, 1)
        x = peer_dense(hn, p["pk_u"][l].astype(BF16), vt, th, s2, w1, w2, x, g2, p["final_g"], l == depth - 1,
                       rows_per_mod, tt, ec)
    return x, s5_fin, ml_fin


def kernel(x_prompt, x_sample, state_s5_re, state_s5_im, state_mlstm_C, state_mlstm_n, state_mlstm_m, c, c_ctx, norm_g, ada_w, ada_b, final_g, ev_w_in, hy_conv_w, hy_conv_b, hy_w1, hy_b1, hy_w2, hy_b2, hy_w3, hy_freq, hy_decay, hy_bias, s5_a_re, s5_a_im, s5_b_re, s5_b_im, s5_c_re, s5_c_im, s5_log_step, s5_d, s5_glu_w, s5_glu_b, ev_w_out, od_w_in, od_gate_b, ml_conv_w, ml_conv_b, ml_norm_g, od_w_out, pk_w_q, pk_keys, pk_u, pk_v):
    p = dict(norm_g=norm_g, ada_w=ada_w, ada_b=ada_b, final_g=final_g, ev_w_in=ev_w_in,
             hy_conv_w=hy_conv_w, hy_conv_b=hy_conv_b, hy_w1=hy_w1, hy_b1=hy_b1, hy_w2=hy_w2, hy_b2=hy_b2,
             hy_w3=hy_w3, hy_freq=hy_freq, hy_decay=hy_decay, hy_bias=hy_bias, s5_a_re=s5_a_re,
             s5_a_im=s5_a_im, s5_b_re=s5_b_re, s5_b_im=s5_b_im, s5_c_re=s5_c_re, s5_c_im=s5_c_im,
             s5_log_step=s5_log_step, s5_d=s5_d, s5_glu_w=s5_glu_w, s5_glu_b=s5_glu_b, ev_w_out=ev_w_out,
             od_w_in=od_w_in, od_gate_b=od_gate_b, ml_conv_w=ml_conv_w, ml_conv_b=ml_conv_b,
             ml_norm_g=ml_norm_g, od_w_out=od_w_out, pk_w_q=pk_w_q, pk_keys=pk_keys, pk_u=pk_u, pk_v=pk_v)
    nb, seq, d = x_prompt.shape
    db, dseq, _ = x_sample.shape
    depth = norm_g.shape[0]
    n_even, n_odd = (depth + 1) // 2, depth // 2
    assert db + 1 <= 8

    cond8 = jnp.zeros((8, d), F32).at[0].set(c_ctx.astype(F32)).at[1:1 + db].set(c.astype(F32))
    mods_ctx, mods_lat = [], []
    for l in range(depth):
        mod = ada_mod(cond8, ada_w[l].astype(F32), ada_b[l].astype(F32))
        chunks = [mod[:, j * d:(j + 1) * d] for j in range(6)]
        mods_ctx.append([ch[0:1].reshape(1, 1, d) for ch in chunks])
        mods_lat.append([ch[1:1 + db].reshape(db, 1, d) for ch in chunks])

    def s5_state(re, im, bsz):
        return [jnp.stack([re[:, i].reshape(bsz, 2, -1), im[:, i].reshape(bsz, 2, -1)], axis=2).astype(F32)
                for i in range(n_even)]

    ngrp, npst = s5_a_re.shape[2], s5_a_re.shape[3]
    nh, dh = state_mlstm_C.shape[3], state_mlstm_C.shape[4]
    zeros_s5 = jnp.zeros((nb, n_even, 2, ngrp, npst), F32)
    y_prompt, s5_fin, ml_fin = _trunk(
        x_prompt.reshape(nb * seq, d), mods_ctx, s5_state(zeros_s5, zeros_s5, nb),
        [jnp.zeros((nb, 2, nh, dh, dh), F32)] * n_odd, [jnp.zeros((nb, 2, nh, dh), F32)] * n_odd,
        [jnp.zeros((nb, 2, nh, 1), F32)] * n_odd, p, nb, seq, nb * seq)
    x_lat = add_pos(x_sample.reshape(db * dseq, d), _pos_embed(dseq, d, GRID_W), dseq, _tile(dseq, 512))
    y_sample, _, _ = _trunk(
        x_lat, mods_lat, s5_state(state_s5_re, state_s5_im, db),
        [state_mlstm_C[:, i].astype(F32) for i in range(n_odd)], [state_mlstm_n[:, i].astype(F32) for i in range(n_odd)],
        [state_mlstm_m[:, i].astype(F32)[..., None] for i in range(n_odd)], p, db, dseq, dseq)

    new_s5_re = jnp.stack([h[:, :, 0].reshape(nb, 2, ngrp, npst) for h in s5_fin], axis=1)
    new_s5_im = jnp.stack([h[:, :, 1].reshape(nb, 2, ngrp, npst) for h in s5_fin], axis=1)
    new_c = jnp.stack([f[0] for f in ml_fin], axis=1)
    new_n = jnp.stack([f[1] for f in ml_fin], axis=1)
    new_m = jnp.stack([f[2][..., 0] for f in ml_fin], axis=1)
    return (y_prompt.reshape(nb, seq, d), y_sample.reshape(db, dseq, d), new_s5_re, new_s5_im, new_c, new_n, new_m)
```

```python
import functools
import math

import jax
import jax.numpy as jnp
from jax import lax
from jax.experimental import pallas as pl
from jax.experimental.pallas import tpu as pltpu

F32 = jnp.float32
BF16 = jnp.bfloat16
EPS = 1e-6
HIGHEST = lax.Precision.HIGHEST
V7X_VMEM_LIMIT_BYTES = 56 * 1024 * 1024
LANES = 128
SUBLANES = 8
ML_CHUNK = 128
PK_TOPK = 16
GRID_W = 64
NEG_INF = float("-inf")
ROW_TILE = 512
ADA_COLS = 1536
CONV_COLS = 256
S5_CHUNK = 256
PEER_SCORE_TOKENS = 256
PEER_TOKENS = 512
PEER_EXPERTS = 2048


def _cp(*sem):
    return pltpu.CompilerParams(dimension_semantics=sem, vmem_limit_bytes=V7X_VMEM_LIMIT_BYTES)


def _dot(a, b, **kw):
    return jnp.dot(a, b, preferred_element_type=F32, **kw)


def _dot_nt(a, b):
    return lax.dot_general(a, b, (((1,), (1,)), ((), ())), preferred_element_type=F32)


def _silu(x):
    return x * jax.nn.sigmoid(x)


def _ada_kernel(c_ref, w_ref, b_ref, o_ref):
    o_ref[...] = _dot(_silu(c_ref[...]), w_ref[...], precision=HIGHEST) + b_ref[...]


def ada_mod(cond8, w, b):
    d, no = w.shape
    tn = _tile(no, ADA_COLS)
    return pl.pallas_call(
        _ada_kernel, grid=(no // tn,),
        in_specs=[pl.BlockSpec((8, d), lambda j: (0, 0)), pl.BlockSpec((d, tn), lambda j: (0, j)),
                  pl.BlockSpec((1, tn), lambda j: (0, j))],
        out_specs=pl.BlockSpec((8, tn), lambda j: (0, j)),
        out_shape=jax.ShapeDtypeStruct((8, no), F32), compiler_params=_cp("parallel"), name="ada_mod",
    )(cond8, w, b.reshape(1, no))


def _normmod(x, g, sc, sh):
    y = x * lax.rsqrt(jnp.mean(x * x, axis=-1, keepdims=True) + EPS)
    return (y * g) * (1.0 + sc) + sh


def _mod_spec(d, tm, rows_per_mod):
    return pl.BlockSpec((1, 1, d), lambda i, j: ((i * tm) // rows_per_mod, 0, 0))


def _nm_matmul_kernel(x_ref, g_ref, sc_ref, sh_ref, w_ref, o_ref, h_ref):
    @pl.when(pl.program_id(1) == 0)
    def _():
        h_ref[...] = _normmod(x_ref[...], g_ref[...], sc_ref[0], sh_ref[0]).astype(BF16)
    o_ref[...] = _dot(h_ref[...], w_ref[...])


def normmod_matmul(x, g, sc, sh, w_bf16, rows_per_mod, tm, tn):
    n, d = x.shape
    no = w_bf16.shape[1]
    return pl.pallas_call(
        _nm_matmul_kernel, grid=(n // tm, no // tn),
        in_specs=[pl.BlockSpec((tm, d), lambda i, j: (i, 0)), pl.BlockSpec((1, d), lambda i, j: (0, 0)),
                  _mod_spec(d, tm, rows_per_mod), _mod_spec(d, tm, rows_per_mod),
                  pl.BlockSpec((d, tn), lambda i, j: (0, j))],
        out_specs=pl.BlockSpec((tm, tn), lambda i, j: (i, j)),
        out_shape=jax.ShapeDtypeStruct((n, no), F32),
        scratch_shapes=[pltpu.VMEM((tm, d), BF16)],
        compiler_params=_cp("parallel", "arbitrary"), name="normmod_matmul",
    )(x, g.reshape(1, d), sc, sh, w_bf16)


def _add_rows_kernel(x_ref, p_ref, o_ref):
    o_ref[...] = x_ref[...] + p_ref[...]


def add_pos(x, pe, seq_len, tm):
    n, d = x.shape
    nb = seq_len // tm
    return pl.pallas_call(
        _add_rows_kernel, grid=(n // tm,),
        in_specs=[pl.BlockSpec((tm, d), lambda i: (i, 0)), pl.BlockSpec((tm, d), lambda i: (i % nb, 0))],
        out_specs=pl.BlockSpec((tm, d), lambda i: (i, 0)),
        out_shape=jax.ShapeDtypeStruct((n, d), F32), compiler_params=_cp("parallel"), name="add_pos",
    )(x, pe)


def _sconv_kernel(x_ref, w_ref, b_ref, s_ref, o_ref, *, act):
    x = x_ref[...]
    n_tok = x.shape[0]
    row = lax.broadcasted_iota(jnp.int32, x.shape, 0)
    prev = jnp.where(row == 0, 0.0, pltpu.roll(x, 1, 0))
    nxt = jnp.where(row == n_tok - 1, 0.0, pltpu.roll(x, n_tok - 1, 0))
    y = prev * w_ref[0:1, :] + x * w_ref[1:2, :] + nxt * w_ref[2:3, :] + b_ref[...]
    if act:
        y = _silu(y) * s_ref[...]
    o_ref[...] = y.astype(o_ref.dtype)


def short_conv(a, ncols, w, b, scale, seq_len, act, out_dtype):
    n = a.shape[0]
    cb = _tile(ncols, CONV_COLS)
    return pl.pallas_call(
        functools.partial(_sconv_kernel, act=act), grid=(n // seq_len, ncols // cb),
        in_specs=[pl.BlockSpec((seq_len, cb), lambda s, j: (s, j)), pl.BlockSpec((3, cb), lambda s, j: (0, j)),
                  pl.BlockSpec((1, cb), lambda s, j: (0, j)), pl.BlockSpec((1, cb), lambda s, j: (0, j))],
        out_specs=pl.BlockSpec((seq_len, cb), lambda s, j: (s, j)),
        out_shape=jax.ShapeDtypeStruct((n, ncols), out_dtype), compiler_params=_cp("parallel", "parallel"),
        name="short_conv",
    )(a, w, b.reshape(1, ncols), scale.reshape(1, ncols))


def dft_tables(n_tok):
    k = jnp.arange(n_tok, dtype=jnp.int32)
    blk = 1 << ((n_tok.bit_length() - 1) // 2)
    def thin(n):
        ang = ((k[:, None] * n[None, :]) % (2 * n_tok)).astype(F32) * (math.pi / n_tok)
        return jnp.cos(ang), jnp.sin(ang)
    (c_hi, s_hi), (c_lo, s_lo) = thin(jnp.arange(0, n_tok, blk, dtype=jnp.int32)), thin(jnp.arange(blk, dtype=jnp.int32))
    cos_t = (c_hi[:, :, None] * c_lo[:, None, :] - s_hi[:, :, None] * s_lo[:, None, :]).reshape(n_tok, n_tok)
    msin = -(s_hi[:, :, None] * c_lo[:, None, :] + c_hi[:, :, None] * s_lo[:, None, :]).reshape(n_tok, n_tok)
    alt = jnp.where(k % 2 == 0, 1.0, -1.0).astype(F32)
    a_t = msin.at[0, :].set(alt)
    a_tt = msin.at[:, 0].set(alt)
    return cos_t.astype(BF16), a_t.astype(BF16), a_tt.astype(BF16)


def _hyfilt_kernel(band_ref, w1_ref, b1_ref, w2_ref, b2_ref, w3_ref, fr_ref, dec_ref, h_ref, ss_ref, *,
                   n_tok, tl, hw, nbands):
    i = pl.program_id(0)
    pos = i * tl + lax.broadcasted_iota(jnp.int32, (tl, 1), 0)
    t = pos.astype(F32) / n_tok
    lane = lax.broadcasted_iota(jnp.int32, (tl, LANES), 1)
    ang = 2.0 * math.pi * t * band_ref[...]
    z = jnp.where(lane == 0, t, jnp.where(lane <= nbands, jnp.cos(ang),
                                          jnp.where(lane <= 2 * nbands, jnp.sin(ang), 0.0)))
    fr = fr_ref[...]
    h = jnp.sin(fr * (_dot(z, w1_ref[...], precision=HIGHEST) + b1_ref[...]))
    h = jnp.sin(fr * (_dot(h, w2_ref[...], precision=HIGHEST) + b2_ref[...]))
    h = _dot(h, w3_ref[...], precision=HIGHEST) * jnp.exp(-t * jnp.abs(dec_ref[...]))
    col = lax.broadcasted_iota(jnp.int32, h.shape, 1)
    is_bwd = (col // hw) % 2 == 1
    h = jnp.where(jnp.logical_and(is_bwd, pos == 0), 0.0, h)
    h_ref[...] = h.astype(BF16)

    @pl.when(i == 0)
    def _():
        ss_ref[...] = jnp.zeros_like(ss_ref)
    ss_ref[...] += jnp.sum(h * h, axis=0, keepdims=True)


def hyena_filter_taps(n_tok, w1, b1, w2, b2, w3, freq, decay, hw):
    emb, ffn = w1.shape
    nbands = (emb - 1) // 2
    tl = _tile(n_tok, ROW_TILE)
    bands = jnp.linspace(1e-4, nbands - 1, nbands, dtype=F32)
    band_row = jnp.zeros((1, LANES), F32).at[0, 1:1 + nbands].set(bands).at[0, 1 + nbands:1 + 2 * nbands].set(bands)
    w1p = jnp.zeros((LANES, ffn), F32).at[:emb].set(w1)
    nc = w3.shape[1]
    full = lambda shp: pl.BlockSpec(shp, lambda i: (0, 0))
    return pl.pallas_call(
        functools.partial(_hyfilt_kernel, n_tok=n_tok, tl=tl, hw=hw, nbands=nbands), grid=(n_tok // tl,),
        in_specs=[full((1, LANES)), full((LANES, ffn)), full((1, ffn)), full((ffn, ffn)), full((1, ffn)),
                  full((ffn, nc)), full((1, ffn)), full((1, nc))],
        out_specs=[pl.BlockSpec((tl, nc), lambda i: (i, 0)), full((1, nc))],
        out_shape=[jax.ShapeDtypeStruct((n_tok, nc), BF16), jax.ShapeDtypeStruct((1, nc), F32)],
        compiler_params=_cp("arbitrary"), name="hyena_filter_taps",
    )(band_row, w1p, b1.reshape(1, ffn), w2, b2.reshape(1, ffn), w3, freq.reshape(1, ffn), decay.reshape(1, nc))


def _filt_dft_kernel(c_ref, a_ref, h_ref, ss_ref, kr_ref, ki_ref, *, tf, hw):
    i = pl.program_id(1)
    hf = h_ref[:, :hw]
    hb = h_ref[:, hw:]
    cc = c_ref[...]
    aa = a_ref[...]
    zrf, zif, zrb, zib = _dot(cc, hf), _dot(aa, hf), _dot(cc, hb), _dot(aa, hb)
    scale = lax.rsqrt(ss_ref[:, :hw] + ss_ref[:, hw:] + EPS)
    first = (i * tf + lax.broadcasted_iota(jnp.int32, (tf, 1), 0)) == 0
    scale = scale * jnp.where(first, 0.5, 1.0)
    kr_ref[0] = (zrf + zrb) * scale
    ki_ref[0] = jnp.where(first, zif + zib, zif - zib) * scale


def hyena_filter_spectrum(cos_t, a_t, taps, sumsq, hw, tf):
    n_tok = cos_t.shape[0]
    norder = taps.shape[1] // (2 * hw)
    out = jax.ShapeDtypeStruct((norder, n_tok, hw), F32)
    return pl.pallas_call(
        functools.partial(_filt_dft_kernel, tf=tf, hw=hw), grid=(norder, n_tok // tf),
        in_specs=[pl.BlockSpec((tf, n_tok), lambda o, i: (i, 0)), pl.BlockSpec((tf, n_tok), lambda o, i: (i, 0)),
                  pl.BlockSpec((n_tok, 2 * hw), lambda o, i: (0, o)), pl.BlockSpec((1, 2 * hw), lambda o, i: (0, o))],
        out_specs=[pl.BlockSpec((1, tf, hw), lambda o, i: (o, i, 0))] * 2,
        out_shape=[out, out], compiler_params=_cp("parallel", "parallel"), name="hyena_filter_spectrum",
    )(cos_t, a_t, taps, sumsq)


def _hy_fwd_kernel(c_ref, a_ref, z_ref, kr_ref, ki_ref, yr_ref, yi_ref, *, tf):
    i = pl.program_id(0)
    zb = z_ref[...].astype(BF16)
    zr = _dot(c_ref[...], zb)
    zi = _dot(a_ref[...], zb)
    kr = kr_ref[0]
    ki = ki_ref[0]
    first = (i * tf + lax.broadcasted_iota(jnp.int32, (tf, 1), 0)) == 0
    yr_ref[...] = jnp.where(first, zr * kr, zr * kr - zi * ki).astype(BF16)
    yi_ref[...] = jnp.where(first, zi * ki, zr * ki + zi * kr).astype(BF16)


def hyena_fwd(cos_t, a_t, z, zcol, kr, ki, order, nseq, hw, tf):
    n_tok = cos_t.shape[0]
    nf = n_tok // tf
    out = jax.ShapeDtypeStruct((nseq * n_tok, hw), BF16)
    return pl.pallas_call(
        functools.partial(_hy_fwd_kernel, tf=tf), grid=(nf, nseq),
        in_specs=[pl.BlockSpec((tf, n_tok), lambda i, b: (i, 0)), pl.BlockSpec((tf, n_tok), lambda i, b: (i, 0)),
                  pl.BlockSpec((n_tok, hw), lambda i, b: (b, zcol)),
                  pl.BlockSpec((1, tf, hw), lambda i, b: (order, i, 0)),
                  pl.BlockSpec((1, tf, hw), lambda i, b: (order, i, 0))],
        out_specs=[pl.BlockSpec((tf, hw), lambda i, b: (b * nf + i, 0))] * 2,
        out_shape=[out, out], compiler_params=_cp("parallel", "parallel"), name="hyena_fwd",
    )(cos_t, a_t, z, kr, ki)


def _hy_inv_kernel(c_ref, at_ref, yr_ref, yi_ref, zp_ref, gate_ref, bias_ref, o_ref, *, inv_len):
    conv = (_dot(c_ref[...], yr_ref[...]) + _dot(at_ref[...], yi_ref[...])) * inv_len
    o_ref[...] = gate_ref[...] * (conv + bias_ref[...] * zp_ref[...])


def hyena_inv(cos_t, a_tt, yr, yi, zprev, zcol, gates, gcol, bias_row, nseq, hw, tf):
    n_tok = cos_t.shape[0]
    nf = n_tok // tf
    return pl.pallas_call(
        functools.partial(_hy_inv_kernel, inv_len=1.0 / n_tok), grid=(nf, nseq),
        in_specs=[pl.BlockSpec((tf, n_tok), lambda i, b: (i, 0)), pl.BlockSpec((tf, n_tok), lambda i, b: (i, 0)),
                  pl.BlockSpec((n_tok, hw), lambda i, b: (b, 0)), pl.BlockSpec((n_tok, hw), lambda i, b: (b, 0)),
                  pl.BlockSpec((tf, hw), lambda i, b: (b * nf + i, zcol)),
                  pl.BlockSpec((tf, hw), lambda i, b: (b * nf + i, gcol)),
                  pl.BlockSpec((1, hw), lambda i, b: (0, 0))],
        out_specs=pl.BlockSpec((tf, hw), lambda i, b: (b * nf + i, 0)),
        out_shape=jax.ShapeDtypeStruct((nseq * n_tok, hw), F32),
        compiler_params=_cp("parallel", "parallel"), name="hyena_inv",
    )(cos_t, a_tt, yr, yi, zprev, gates, bias_row)


S5_DIAG_BLOCKS = 2


S5_SEQS_PER_STEP = 4


def _s5_kernel(u_ref, bre_ref, bim_ref, cre_ref, cim_ref, lam_ref, h0_ref, y_ref, hfin_ref, hre_s, him_s, st_s, *,
               tc, nc, ns, nb):
    d = pl.program_id(0)
    c = pl.program_id(2)

    @pl.when(c == 0)
    def _():
        st_s[...] = h0_ref[:, 0]

    sw = u_ref.shape[2]
    halves = [(slice(j * sw // S5_DIAG_BLOCKS, (j + 1) * sw // S5_DIAG_BLOCKS),
               slice(j * ns // S5_DIAG_BLOCKS, (j + 1) * ns // S5_DIAG_BLOCKS)) for j in range(S5_DIAG_BLOCKS)]
    for j in range(nb):
        ub = u_ref[j].astype(BF16)
        for us, hs in halves:
            hre_s[j, :, hs] = _dot(ub[:, us], bre_ref[0, us, hs])
            him_s[j, :, hs] = _dot(ub[:, us], bim_ref[0, us, hs])
    lr = lam_ref[0, 0:1, :]
    li = lam_ref[0, 1:2, :]

    def body(t, carry):
        r = jnp.where(d == 0, t, tc - 1 - t)
        new = []
        for j, (hr, hi) in enumerate(carry):
            nr = lr * hr - li * hi + hre_s[j, pl.ds(r, 1), :]
            ni = lr * hi + li * hr + him_s[j, pl.ds(r, 1), :]
            hre_s[j, pl.ds(r, 1), :] = nr
            him_s[j, pl.ds(r, 1), :] = ni
            new.append((nr, ni))
        return tuple(new)

    start = tuple((st_s[j, 0:1, :], st_s[j, 1:2, :]) for j in range(nb))
    for j, (hr, hi) in enumerate(lax.fori_loop(0, tc, body, start, unroll=4)):
        st_s[j, 0:1, :] = hr
        st_s[j, 1:2, :] = hi
    for j in range(nb):
        for us, hs in halves:
            y_ref[0, j, :, us] = (_dot(hre_s[j, :, hs].astype(BF16), cre_ref[0, hs, us])
                                  + _dot(him_s[j, :, hs].astype(BF16), cim_ref[0, hs, us]))

    @pl.when(c == nc - 1)
    def _():
        hfin_ref[:, 0] = st_s[...]


def s5_scan(proj, ucol, bre, bim, cre, cim, lam, h0, nseq, seq_len, sw, tc):
    ns = bre.shape[2]
    nc = seq_len // tc
    nb = S5_SEQS_PER_STEP if nseq % S5_SEQS_PER_STEP == 0 else 1

    def chunk(d, c):
        return c + d * (nc - 1 - 2 * c)

    y, hfin = pl.pallas_call(
        functools.partial(_s5_kernel, tc=tc, nc=nc, ns=ns, nb=nb), grid=(2, nseq // nb, nc),
        in_specs=[pl.BlockSpec((nb, tc, sw), lambda d, b, c: (b, chunk(d, c), ucol)),
                  pl.BlockSpec((1, sw, ns), lambda d, b, c: (d, 0, 0)),
                  pl.BlockSpec((1, sw, ns), lambda d, b, c: (d, 0, 0)),
                  pl.BlockSpec((1, ns, sw), lambda d, b, c: (d, 0, 0)),
                  pl.BlockSpec((1, ns, sw), lambda d, b, c: (d, 0, 0)),
                  pl.BlockSpec((1, 2, ns), lambda d, b, c: (d, 0, 0)),
                  pl.BlockSpec((nb, 1, 2, ns), lambda d, b, c: (b, d, 0, 0))],
        out_specs=[pl.BlockSpec((1, nb, tc, sw), lambda d, b, c: (d, b, chunk(d, c), 0)),
                   pl.BlockSpec((nb, 1, 2, ns), lambda d, b, c: (b, d, 0, 0))],
        out_shape=[jax.ShapeDtypeStruct((2, nseq, seq_len, sw), F32), jax.ShapeDtypeStruct((nseq, 2, 2, ns), F32)],
        scratch_shapes=[pltpu.VMEM((nb, tc, ns), F32), pltpu.VMEM((nb, tc, ns), F32), pltpu.VMEM((nb, 2, ns), F32)],
        compiler_params=_cp("parallel", "parallel", "arbitrary"), name="s5_scan",
    )(proj.reshape(nseq, seq_len, proj.shape[1]), bre, bim, cre, cim, lam, h0)
    return y.reshape(2, nseq * seq_len, sw), hfin


def _s5_glu_kernel(yf_ref, yb_ref, u_ref, d_ref, w_ref, b_ref, o_ref):
    y = jax.nn.gelu(yf_ref[0] + yb_ref[0] + d_ref[...] * u_ref[...])
    o_ref[...] = y * jax.nn.sigmoid(_dot(y.astype(BF16), w_ref[...]) + b_ref[...])


def s5_glu(y2, proj, ucol, d_skip, glu_w_bf16, glu_b, tm):
    _, n, sw = y2.shape
    return pl.pallas_call(
        _s5_glu_kernel, grid=(n // tm,),
        in_specs=[pl.BlockSpec((1, tm, sw), lambda i: (0, i, 0)), pl.BlockSpec((1, tm, sw), lambda i: (1, i, 0)),
                  pl.BlockSpec((tm, sw), lambda i: (i, ucol)), pl.BlockSpec((1, sw), lambda i: (0, 0)),
                  pl.BlockSpec((sw, sw), lambda i: (0, 0)), pl.BlockSpec((1, sw), lambda i: (0, 0))],
        out_specs=pl.BlockSpec((tm, sw), lambda i: (i, 0)),
        out_shape=jax.ShapeDtypeStruct((n, sw), F32), compiler_params=_cp("parallel"), name="s5_glu",
    )(y2, y2, proj, d_skip.reshape(1, sw), glu_w_bf16, glu_b.reshape(1, sw))


def _residual_and_next_norm(x_ref, gate_ref, y, g2_ref, sc2_ref, sh2_ref, o_ref, hn_ref):
    xn = x_ref[...] + gate_ref[0] * y
    o_ref[...] = xn
    hn_ref[...] = _normmod(xn, g2_ref[...], sc2_ref[0], sh2_ref[0]).astype(BF16)


def _row_specs(d, tm, rows_per_mod):
    mod = pl.BlockSpec((1, 1, d), lambda i: ((i * tm) // rows_per_mod, 0, 0))
    return [pl.BlockSpec((tm, d), lambda i: (i, 0)), mod, pl.BlockSpec((1, d), lambda i: (0, 0)), mod, mod]


def _row_outs(n, d, tm):
    spec = pl.BlockSpec((tm, d), lambda i: (i, 0))
    return [spec, spec], [jax.ShapeDtypeStruct((n, d), F32), jax.ShapeDtypeStruct((n, d), BF16)]


def _even_out_kernel(a_ref, b_ref, wa_ref, wb_ref, x_ref, gate_ref, g2_ref, sc2_ref, sh2_ref, o_ref, hn_ref):
    y = _dot(a_ref[...].astype(BF16), wa_ref[...]) + _dot(b_ref[...].astype(BF16), wb_ref[...])
    _residual_and_next_norm(x_ref, gate_ref, y, g2_ref, sc2_ref, sh2_ref, o_ref, hn_ref)


def even_out(hy, s5o, w_bf16, x, gate, g2, sc2, sh2, rows_per_mod, tm):
    n, d = x.shape
    hw = hy.shape[1]
    sw = s5o.shape[1]
    out_specs, out_shape = _row_outs(n, d, tm)
    return pl.pallas_call(
        _even_out_kernel, grid=(n // tm,),
        in_specs=[pl.BlockSpec((tm, hw), lambda i: (i, 0)), pl.BlockSpec((tm, sw), lambda i: (i, 0)),
                  pl.BlockSpec((hw, d), lambda i: (0, 0)), pl.BlockSpec((sw, d), lambda i: (hw // sw, 0))]
        + _row_specs(d, tm, rows_per_mod),
        out_specs=out_specs, out_shape=out_shape, compiler_params=_cp("parallel"), name="even_out",
    )(hy, s5o, w_bf16, w_bf16, x, gate, g2.reshape(1, d), sc2, sh2)


ML_HEAD_GROUP = 8


def _log_sigmoid(x):
    return jnp.minimum(x, 0.0) - jnp.log1p(jnp.exp(-jnp.abs(x)))


def _mlstm_kernel(q_ref, k_ref, v_ref, g_ref, gb_ref, c0_ref, n0_ref, m0_ref, h_ref, cf_ref, nf_ref, mf_ref,
                  c_s, m_s, *, nh, dh, tc, nc):
    d = pl.program_id(0)
    c = pl.program_id(2)

    @pl.when(c == 0)
    def _():
        for h in range(nh):
            c_s[h, :, :dh] = c0_ref[0, 0, h]
            c_s[h, :, dh:] = jnp.broadcast_to(n0_ref[0, 0, h:h + 1, :], (dh, dh)).T
        m_s[...] = m0_ref[0, 0]

    ones = jnp.ones((tc, dh), F32)
    gates = g_ref[...] + gb_ref[0]
    lane = lax.broadcasted_iota(jnp.int32, gates.shape, 1)
    logf = jnp.where(jnp.logical_and(lane >= nh, lane < 2 * nh), _log_sigmoid(gates), 0.0)
    r_i = lax.broadcasted_iota(jnp.int32, (tc, tc), 0)
    s_i = lax.broadcasted_iota(jnp.int32, (tc, tc), 1)
    causal = (r_i - s_i) * (1 - 2 * d) >= 0
    bcum = _dot(causal.astype(F32), logf, precision=HIGHEST)
    btot = jnp.sum(logf, axis=0, keepdims=True)
    gates_t = gates.T
    bcum_t = bcum.T
    for g0 in range(0, nh, ML_HEAD_GROUP):
        hds = list(range(g0, min(g0 + ML_HEAD_GROUP, nh)))
        hsl = {h: slice(h * dh, (h + 1) * dh) for h in hds}
        b_col = {h: bcum[:, nh + h:nh + h + 1] for h in hds}
        m_old = {h: m_s[h:h + 1, :] for h in hds}
        a = {h: b_col[h] + m_old[h] for h in hds}
        src = {h: jnp.where(causal, gates_t[h:h + 1, :] - bcum_t[nh + h:nh + h + 1, :], NEG_INF) for h in hds}
        mq = {h: jnp.maximum(a[h], b_col[h] + jnp.max(src[h], axis=-1, keepdims=True)) for h in hds}
        s = {h: _dot_nt(q_ref[:, hsl[h]], k_ref[:, hsl[h]]) * jnp.exp(src[h] + (b_col[h] - mq[h])) for h in hds}
        v1 = {h: jnp.concatenate([v_ref[:, hsl[h]], ones], axis=1) for h in hds}
        cn = {h: c_s[h] for h in hds}
        qw = {h: jnp.exp(a[h] - mq[h]) * q_ref[:, hsl[h]] for h in hds}
        both = {h: _dot(s[h], v1[h]) + _dot(qw[h], cn[h]) for h in hds}
        for h in hds:
            h_ref[0, :, hsl[h]] = both[h][:, :dh] / jnp.maximum(jnp.abs(both[h][:, dh:]), jnp.exp(-mq[h]))
        b_last = {h: btot[:, nh + h:nh + h + 1] for h in hds}
        g = {h: b_last[h] - b_col[h] + gates[:, h:h + 1] for h in hds}
        m_new = {h: jnp.maximum(b_last[h] + m_old[h], jnp.max(g[h], axis=0, keepdims=True)) for h in hds}
        kw = {h: k_ref[:, hsl[h]] * jnp.exp(g[h] - m_new[h]) for h in hds}
        for h in hds:
            c_s[h] = jnp.exp(b_last[h] + m_old[h] - m_new[h]) * cn[h] + _dot(kw[h].T, v1[h])
            m_s[h:h + 1, :] = m_new[h]

    @pl.when(c == nc - 1)
    def _():
        for h in range(nh):
            cf_ref[0, 0, h] = c_s[h, :, :dh]
            nf_ref[0, 0, h:h + 1, :] = c_s[h, :, dh:].T[0:1, :]
        mf_ref[0, 0] = m_s[...]


def mlstm_scan(qk, proj, vcol, gate_bias, c0, n0, m0, nseq, seq_len, nh, dh):
    tc = ML_CHUNK
    nc = seq_len // tc
    w = nh * dh

    def chunk(d, c):
        return c + d * (nc - 1 - 2 * c)

    rowblk = lambda d, b, c: b * nc + chunk(d, c)
    st = lambda shp: pl.BlockSpec((1, 1) + shp, lambda d, b, c: (b, d) + (0,) * len(shp))
    return pl.pallas_call(
        functools.partial(_mlstm_kernel, nh=nh, dh=dh, tc=tc, nc=nc), grid=(2, nseq, nc),
        in_specs=[pl.BlockSpec((tc, w), lambda d, b, c: (rowblk(d, b, c), 0)),
                  pl.BlockSpec((tc, w), lambda d, b, c: (rowblk(d, b, c), 1)),
                  pl.BlockSpec((tc, w), lambda d, b, c: (rowblk(d, b, c), vcol)),
                  pl.BlockSpec((tc, LANES), lambda d, b, c: (rowblk(d, b, c), 4 * w // LANES + d)),
                  pl.BlockSpec((1, 1, LANES), lambda d, b, c: (d, 0, 0)),
                  st((nh, dh, dh)), st((nh, dh)), st((nh, 1))],
        out_specs=[pl.BlockSpec((1, tc, w), lambda d, b, c: (d, rowblk(d, b, c), 0)),
                   st((nh, dh, dh)), st((nh, dh)), st((nh, 1))],
        out_shape=[jax.ShapeDtypeStruct((2, nseq * seq_len, w), F32),
                   jax.ShapeDtypeStruct((nseq, 2, nh, dh, dh), F32), jax.ShapeDtypeStruct((nseq, 2, nh, dh), F32),
                   jax.ShapeDtypeStruct((nseq, 2, nh, 1), F32)],
        scratch_shapes=[pltpu.VMEM((nh, dh, 2 * dh), F32), pltpu.VMEM((nh, 1), F32)],
        compiler_params=_cp("parallel", "parallel", "arbitrary"), name="mlstm_scan",
    )(qk, qk, proj, proj, gate_bias, c0, n0, m0)


def _odd_out_kernel(hf_ref, hb_ref, og_ref, ng_ref, w_ref, x_ref, gate_ref, g2_ref, sc2_ref, sh2_ref, o_ref, hn_ref,
                    a_s, *, nh, dh):
    for h in range(nh):
        hs = slice(h * dh, (h + 1) * dh)
        blk = hf_ref[0, :, hs] + hb_ref[0, :, hs]
        blk = blk * lax.rsqrt(jnp.mean(blk * blk, axis=-1, keepdims=True) + EPS)
        a_s[:, hs] = ((blk * ng_ref[:, hs]) * _silu(og_ref[:, hs])).astype(BF16)
    _residual_and_next_norm(x_ref, gate_ref, _dot(a_s[...], w_ref[...]), g2_ref, sc2_ref, sh2_ref, o_ref, hn_ref)


def odd_out(h2, proj, ocol, norm_g, w_bf16, x, gate, g2, sc2, sh2, rows_per_mod, nh, dh, tm):
    n, d = x.shape
    w = nh * dh
    out_specs, out_shape = _row_outs(n, d, tm)
    return pl.pallas_call(
        functools.partial(_odd_out_kernel, nh=nh, dh=dh), grid=(n // tm,),
        in_specs=[pl.BlockSpec((1, tm, w), lambda i: (0, i, 0)), pl.BlockSpec((1, tm, w), lambda i: (1, i, 0)),
                  pl.BlockSpec((tm, w), lambda i: (i, ocol)), pl.BlockSpec((1, w), lambda i: (0, 0)),
                  pl.BlockSpec((w, d), lambda i: (0, 0))] + _row_specs(d, tm, rows_per_mod),
        out_specs=out_specs, out_shape=out_shape, scratch_shapes=[pltpu.VMEM((tm, w), BF16)],
        compiler_params=_cp("parallel"), name="odd_out",
    )(h2, h2, proj, norm_g.reshape(1, w), w_bf16, x, gate, g2.reshape(1, d), sc2, sh2)


def _sort_network(n):
    pairs, p = [], 1
    while p < n:
        k = p
        while k >= 1:
            for j in range(k % p, n - k, 2 * k):
                for i in range(min(k, n - j - k)):
                    if (i + j) // (2 * p) == (i + j + k) // (2 * p):
                        pairs.append((i + j, i + j + k))
            k //= 2
        p *= 2
    return pairs


def _top_values_tiled(arrays, k, outs):
    cols = []
    for arr in arrays:
        tiles = [arr[r:r + SUBLANES, :] for r in range(0, arr.shape[0], SUBLANES)]
        for lo, hi in _sort_network(len(tiles)):
            tiles[lo], tiles[hi] = jnp.maximum(tiles[lo], tiles[hi]), jnp.minimum(tiles[lo], tiles[hi])
        cols.append(tiles)
    for j in range(k):
        for tiles, out_s in zip(cols, outs):
            m = jnp.max(tiles[0], axis=0, keepdims=True)
            out_s[j:j + 1, :] = m
            hit = tiles[0] == m
            for i in range(min(len(tiles), k - 1 - j)):
                below = tiles[i + 1] if i + 1 < len(tiles) else NEG_INF
                tiles[i] = jnp.where(hit, below, tiles[i])


def _pair_candidates(k):
    return [(a, k // (a + 1)) for a in range(k)]


PEER_HEADS_PER_TRIP = 8


def _peer_score_kernel(h_ref, wq_ref, keys_ref, th_ref, s2_ref, w1_ref, w2_ref, q_s, v_s, cand_s, best_s, *,
                       nh, half, topk):
    q_s[...] = _dot_nt(wq_ref[...], h_ref[...])
    kk = topk + 1
    cand_s[...] = jnp.full(cand_s.shape, NEG_INF, F32)
    group = range(PEER_HEADS_PER_TRIP)

    def heads(trip, carry):
        hds = [trip * PEER_HEADS_PER_TRIP + u for u in group]
        scores = []
        for hd in hds:
            base = pl.multiple_of(hd * 2 * half, 2 * half)
            scores.append(_dot(keys_ref[hd, 0], q_s[pl.ds(base, half), :]))
            scores.append(_dot(keys_ref[hd, 1], q_s[pl.ds(base + half, half), :]))
        _top_values_tiled(scores, kk, [v_s.at[u, c] for u in group for c in range(2)])
        for u in group:
            off = 0
            for a, cnt in _pair_candidates(kk):
                cand_s[u, off:off + cnt, :] = v_s[u, 0, a:a + 1, :] + v_s[u, 1, 0:cnt, :]
                off += cnt
        _top_values_tiled([cand_s[u] for u in group], kk, [best_s.at[u] for u in group])
        for u, hd in enumerate(hds):
            s1, s2 = scores[2 * u], scores[2 * u + 1]
            best = best_s[u, 0:topk, :]
            z = jnp.sum(jnp.exp(best - best[0:1, :]), axis=0, keepdims=True)
            tmid = 0.5 * (best_s[u, topk - 1:topk, :] + best_s[u, topk:topk + 1, :])
            th = tmid - s1
            w1 = jnp.exp(s1 - v_s[u, 0, 0:1, :]) / z
            w2 = jnp.exp(s2 - v_s[u, 1, 0:1, :])
            for lt in range(s1.shape[1] // LANES):
                sl = slice(lt * LANES, (lt + 1) * LANES)
                th_ref[hd, lt] = th[:, sl]
                s2_ref[hd, lt] = s2[:, sl]
                w1_ref[hd, lt] = w1[:, sl]
                w2_ref[hd, lt] = w2[:, sl]
        return carry

    lax.fori_loop(0, nh // PEER_HEADS_PER_TRIP, heads, 0)


def peer_scores(h_bf16, wq_t_bf16, keys, tt):
    n, d = h_bf16.shape
    nh, _, nk, half = keys.shape
    kk = PK_TOPK + 1
    top_rows = -(-kk // SUBLANES) * SUBLANES
    ncand = SUBLANES
    while ncand < sum(c for _, c in _pair_candidates(kk)):
        ncand *= 2
    big = jax.ShapeDtypeStruct((nh, n // LANES, nk, LANES), F32)
    bspec = pl.BlockSpec((nh, tt // LANES, nk, LANES), lambda i: (0, i, 0, 0))
    return pl.pallas_call(
        functools.partial(_peer_score_kernel, nh=nh, half=half, topk=PK_TOPK), grid=(n // tt,),
        in_specs=[pl.BlockSpec((tt, d), lambda i: (i, 0)), pl.BlockSpec((nh * 2 * half, d), lambda i: (0, 0)),
                  pl.BlockSpec((nh, 2, nk, half), lambda i: (0, 0, 0, 0))],
        out_specs=[bspec, bspec, bspec, bspec],
        out_shape=[big, big, big, big],
        scratch_shapes=[pltpu.VMEM((nh * 2 * half, tt), F32), pltpu.VMEM((PEER_HEADS_PER_TRIP, 2, top_rows, tt), F32),
                        pltpu.VMEM((PEER_HEADS_PER_TRIP, ncand, tt), F32),
                        pltpu.VMEM((PEER_HEADS_PER_TRIP, top_rows, tt), F32)],
        compiler_params=_cp("parallel"), name="peer_scores",
    )(h_bf16, wq_t_bf16, keys)


PEER_KEY_ROWS = 16
GELU_C1 = math.sqrt(2.0 / math.pi)
GELU_C2 = 0.044715 * GELU_C1


def _gelu_tanh(x):
    half_x = 0.5 * x
    return half_x + half_x * jnp.tanh(x * (GELU_C1 + GELU_C2 * (x * x)))


def _peer_dense_kernel(h_ref, u_ref, vt_ref, th_ref, s2_ref, w1_ref, w2_ref, x_ref, gate_ref, fg_ref, o_ref,
                       acc_s, st_s, wt_s, *, nh, nk, ec, tt, final):
    e = pl.program_id(1)
    nlt = tt // LANES
    n_i1 = ec // nk
    nkt = nk // PEER_KEY_ROWS
    nsub = PEER_KEY_ROWS // SUBLANES

    @pl.when(e == 0)
    def _():
        acc_s[...] = jnp.zeros_like(acc_s)

    st = _gelu_tanh(_dot_nt(u_ref[...], h_ref[...]))
    for lt in range(nlt):
        st_s[lt] = st[:, lt * LANES:(lt + 1) * LANES]

    def tile(idx, carry):
        lt = idx // nkt
        k0 = (idx % nkt) * PEER_KEY_ROWS
        subs = [pl.ds(pl.multiple_of(k0 + j * SUBLANES, SUBLANES), SUBLANES) for j in range(nsub)]
        g = [[jnp.zeros((SUBLANES, LANES), F32) for _ in subs] for _ in range(n_i1)]
        for hd in range(nh):
            s2t = [s2_ref[hd, lt, sub, :] for sub in subs]
            w2t = [w2_ref[hd, lt, sub, :] for sub in subs]
            for li in range(n_i1):
                thb = jnp.broadcast_to(th_ref[hd, lt, li:li + 1, :], (SUBLANES, LANES))
                w1b = jnp.broadcast_to(w1_ref[hd, lt, li:li + 1, :], (SUBLANES, LANES))
                for j in range(nsub):
                    g[li][j] = g[li][j] + jnp.where(s2t[j] >= thb, w2t[j] * w1b, 0.0)
        for li in range(n_i1):
            rows = pl.ds(pl.multiple_of(li * nk + k0, PEER_KEY_ROWS), PEER_KEY_ROWS)
            wt_s[lt, rows, :] = (st_s[lt, rows, :] * jnp.concatenate(g[li], axis=0)).astype(BF16)
        return carry

    lax.fori_loop(0, nlt * nkt, tile, 0)
    wt = jnp.concatenate([wt_s[lt] for lt in range(nlt)], axis=1)
    acc_s[...] += _dot(vt_ref[0], wt)

    @pl.when(e == pl.num_programs(1) - 1)
    def _():
        xn = x_ref[...] + gate_ref[0] * acc_s[...].T
        if final:
            xn = (xn * lax.rsqrt(jnp.mean(xn * xn, axis=-1, keepdims=True) + EPS)) * fg_ref[...]
        o_ref[...] = xn


def peer_dense(h_bf16, u_bf16, vt_bf16, th, s2, w1, w2, x, gate, final_g, final, rows_per_mod, tt, ec):
    n, d = x.shape
    nh, _, nk, _ = s2.shape
    nchunk = u_bf16.shape[0] // ec
    bspec = pl.BlockSpec((nh, tt // LANES, nk, LANES), lambda i, e: (0, i, 0, 0))
    rspec = pl.BlockSpec((nh, tt // LANES, ec // nk, LANES), lambda i, e: (0, i, e, 0))
    tile_buf = (tt // LANES, ec, LANES)
    return pl.pallas_call(
        functools.partial(_peer_dense_kernel, nh=nh, nk=nk, ec=ec, tt=tt, final=final), grid=(n // tt, nchunk),
        in_specs=[pl.BlockSpec((tt, d), lambda i, e: (i, 0)), pl.BlockSpec((ec, d), lambda i, e: (e, 0)),
                  pl.BlockSpec((1, d, ec), lambda i, e: (e, 0, 0)), rspec, bspec, rspec, bspec,
                  pl.BlockSpec((tt, d), lambda i, e: (i, 0)),
                  pl.BlockSpec((1, 1, d), lambda i, e: ((i * tt) // rows_per_mod, 0, 0)),
                  pl.BlockSpec((1, d), lambda i, e: (0, 0))],
        out_specs=pl.BlockSpec((tt, d), lambda i, e: (i, 0)),
        out_shape=jax.ShapeDtypeStruct((n, d), F32),
        scratch_shapes=[pltpu.VMEM((d, tt), F32), pltpu.VMEM(tile_buf, F32), pltpu.VMEM(tile_buf, BF16)],
        compiler_params=_cp("parallel", "arbitrary"), name="peer_dense",
    )(h_bf16, u_bf16, vt_bf16, th, s2, w1, w2, x, gate, final_g.reshape(1, d))


def _s5_params(a_re, a_im, b_re, b_im, c_re, c_im, log_step):
    lam = lax.complex(a_re.astype(F32), a_im.astype(F32))
    lam_bar = jnp.exp(lam * jnp.exp(log_step.astype(F32))[..., None])
    b_bar = ((lam_bar - 1.0) / lam)[..., None] * lax.complex(b_re.astype(F32), b_im.astype(F32))
    ngrp, npst, nch = b_bar.shape[1:]
    eye = jnp.eye(ngrp, dtype=F32)

    def b_mat(part):
        return jnp.einsum("dgpj,gh->dgjhp", part, eye).reshape(2, ngrp * nch, ngrp * npst)

    def c_mat(part):
        return jnp.einsum("dgjp,gh->dgphj", part, eye).reshape(2, ngrp * npst, ngrp * nch)

    bre, bim = b_mat(b_bar.real).astype(BF16), b_mat(b_bar.imag).astype(BF16)
    cre, cim = c_mat(c_re.astype(F32)).astype(BF16), c_mat(-c_im.astype(F32)).astype(BF16)
    lam2 = jnp.stack([lam_bar.real.reshape(2, -1), lam_bar.imag.reshape(2, -1)], axis=1)
    return bre, bim, cre, cim, lam2


def _pos_embed(n_tok, d, grid_w):
    rows = n_tok // grid_w
    quarter = d // 4
    omega = 1.0 / (10000.0 ** (jnp.arange(quarter, dtype=F32) / quarter))

    def emb1d(pos):
        ang = pos.astype(F32)[:, None] * omega[None]
        return jnp.concatenate([jnp.sin(ang), jnp.cos(ang)], axis=-1)

    er = emb1d(jnp.arange(rows))
    ec = emb1d(jnp.arange(grid_w))
    half = d // 2
    pe = jnp.concatenate([jnp.broadcast_to(er[:, None], (rows, grid_w, half)),
                          jnp.broadcast_to(ec[None], (rows, grid_w, half))], axis=-1)
    return pe.reshape(rows * grid_w, d)


def _tile(n, pref):
    return pref if n % pref == 0 else n


def _trunk(x, mods, s5_h0, ml_c0, ml_n0, ml_m0, p, nseq, seq_len, rows_per_mod):
    n, d = x.shape
    tm = _tile(min(rows_per_mod, n), ROW_TILE)
    depth = p["norm_g"].shape[0]
    s5_fin, ml_fin = [], []
    for l in range(depth):
        sh1, sc1, g1, sh2, sc2, g2 = mods[l]
        i = l // 2
        if l % 2 == 0:
            hw = p["hy_bias"].shape[2]
            sw = p["s5_d"].shape[1]
            proj = normmod_matmul(x, p["norm_g"][l, 0], sc1, sh1, p["ev_w_in"][i].astype(BF16), rows_per_mod, tm,
                                  3 * hw + sw)
            hy_in = short_conv(proj, 3 * hw, p["hy_conv_w"][i], p["hy_conv_b"][i], jnp.ones((3 * hw,), F32),
                               seq_len, act=False, out_dtype=F32)
            cos_t, a_t, a_tt = dft_tables(seq_len)
            tf = _tile(seq_len, ROW_TILE)
            taps, sumsq = hyena_filter_taps(seq_len, p["hy_w1"][i], p["hy_b1"][i], p["hy_w2"][i], p["hy_b2"][i],
                                            p["hy_w3"][i], p["hy_freq"][i], p["hy_decay"][i], hw)
            kr, ki = hyena_filter_spectrum(cos_t, a_t, taps, sumsq, hw, tf)
            bias = p["hy_bias"][i].astype(F32)
            z, zcol = hy_in, 0
            for o in range(bias.shape[0]):
                yr, yi = hyena_fwd(cos_t, a_t, z, zcol, kr, ki, o, nseq, hw, tf)
                z = hyena_inv(cos_t, a_tt, yr, yi, z, zcol, hy_in, 1 + o, bias[o:o + 1], nseq, hw, tf)
                zcol = 0
            bre, bim, cre, cim, lam2 = _s5_params(p["s5_a_re"][i], p["s5_a_im"][i], p["s5_b_re"][i], p["s5_b_im"][i],
                                                  p["s5_c_re"][i], p["s5_c_im"][i], p["s5_log_step"][i])
            ucol = 3 * hw // sw
            y2, hfin = s5_scan(proj, ucol, bre, bim, cre, cim, lam2, s5_h0[i], nseq, seq_len, sw,
                               _tile(seq_len, S5_CHUNK))
            s5_fin.append(hfin)
            s5o = s5_glu(y2, proj, ucol, p["s5_d"][i], p["s5_glu_w"][i].astype(BF16), p["s5_glu_b"][i], tm)
            x, hn = even_out(z, s5o, p["ev_w_out"][i].astype(BF16), x, g1, p["norm_g"][l, 1], sc2, sh2, rows_per_mod, tm)
        else:
            nh = p["od_gate_b"].shape[2]
            w = p["ml_norm_g"].shape[1]
            dh = w // nh
            w_in = p["od_w_in"][i]
            wg = w_in[:, 4 * w:].reshape(d, 4, nh)
            gb = p["od_gate_b"][i].astype(F32)
            wg2 = jnp.zeros((d, 2, LANES), w_in.dtype)
            bg2 = jnp.zeros((2, 1, LANES), F32)
            for dr in range(2):
                wg2 = wg2.at[:, dr, :nh].set(wg[:, dr]).at[:, dr, nh:2 * nh].set(wg[:, 2 + dr])
                bg2 = bg2.at[dr, 0, :nh].set(gb[dr]).at[dr, 0, nh:2 * nh].set(gb[2 + dr])
            w_all = jnp.concatenate([w_in[:, :4 * w], wg2.reshape(d, 2 * LANES)], axis=1).astype(BF16)
            proj = normmod_matmul(x, p["norm_g"][l, 0], sc1, sh1, w_all, rows_per_mod, tm, w_all.shape[1] // 2)
            qscale = jnp.concatenate([jnp.full((w,), dh ** -0.5, F32), jnp.ones((w,), F32)])
            qk = short_conv(proj, 2 * w, p["ml_conv_w"][i], p["ml_conv_b"][i], qscale, seq_len, act=True, out_dtype=BF16)
            h2, cf, nf, mf = mlstm_scan(qk, proj, 2, bg2, ml_c0[i], ml_n0[i], ml_m0[i], nseq, seq_len, nh, dh)
            ml_fin.append((cf, nf, mf))
            x, hn = odd_out(h2, proj, 3, p["ml_norm_g"][i], p["od_w_out"][i].astype(BF16), x, g1, p["norm_g"][l, 1],
                            sc2, sh2, rows_per_mod, nh, dh, tm)
        tt = _tile(min(rows_per_mod, n), PEER_TOKENS)
        th, s2, w1, w2 = peer_scores(hn, p["pk_w_q"][l].T.astype(BF16), p["pk_keys"][l].astype(F32),
                                     _tile(tt, PEER_SCORE_TOKENS))
        ec = _tile(p["pk_u"].shape[1], PEER_EXPERTS)
        vt = p["pk_v"][l].astype(BF16).reshape(-1, ec, d).transpose(0, 2, 1)
        x = peer_dense(hn, p["pk_u"][l].astype(BF16), vt, th, s2, w1, w2, x, g2, p["final_g"], l == depth - 1,
                       rows_per_mod, tt, ec)
    return x, s5_fin, ml_fin


def kernel(x_prompt, x_sample, state_s5_re, state_s5_im, state_mlstm_C, state_mlstm_n, state_mlstm_m, c, c_ctx, norm_g, ada_w, ada_b, final_g, ev_w_in, hy_conv_w, hy_conv_b, hy_w1, hy_b1, hy_w2, hy_b2, hy_w3, hy_freq, hy_decay, hy_bias, s5_a_re, s5_a_im, s5_b_re, s5_b_im, s5_c_re, s5_c_im, s5_log_step, s5_d, s5_glu_w, s5_glu_b, ev_w_out, od_w_in, od_gate_b, ml_conv_w, ml_conv_b, ml_norm_g, od_w_out, pk_w_q, pk_keys, pk_u, pk_v):
    p = dict(norm_g=norm_g, ada_w=ada_w, ada_b=ada_b, final_g=final_g, ev_w_in=ev_w_in,
             hy_conv_w=hy_conv_w, hy_conv_b=hy_conv_b, hy_w1=hy_w1, hy_b1=hy_b1, hy_w2=hy_w2, hy_b2=hy_b2,
             hy_w3=hy_w3, hy_freq=hy_freq, hy_decay=hy_decay, hy_bias=hy_bias, s5_a_re=s5_a_re,
             s5_a_im=s5_a_im, s5_b_re=s5_b_re, s5_b_im=s5_b_im, s5_c_re=s5_c_re, s5_c_im=s5_c_im,
             s5_log_step=s5_log_step, s5_d=s5_d, s5_glu_w=s5_glu_w, s5_glu_b=s5_glu_b, ev_w_out=ev_w_out,
             od_w_in=od_w_in, od_gate_b=od_gate_b, ml_conv_w=ml_conv_w, ml_conv_b=ml_conv_b,
             ml_norm_g=ml_norm_g, od_w_out=od_w_out, pk_w_q=pk_w_q, pk_keys=pk_keys, pk_u=pk_u, pk_v=pk_v)
    nb, seq, d = x_prompt.shape
    db, dseq, _ = x_sample.shape
    depth = norm_g.shape[0]
    n_even, n_odd = (depth + 1) // 2, depth // 2
    assert db + 1 <= 8

    cond8 = jnp.zeros((8, d), F32).at[0].set(c_ctx.astype(F32)).at[1:1 + db].set(c.astype(F32))
    mods_ctx, mods_lat = [], []
    for l in range(depth):
        mod = ada_mod(cond8, ada_w[l].astype(F32), ada_b[l].astype(F32))
        chunks = [mod[:, j * d:(j + 1) * d] for j in range(6)]
        mods_ctx.append([ch[0:1].reshape(1, 1, d) for ch in chunks])
        mods_lat.append([ch[1:1 + db].reshape(db, 1, d) for ch in chunks])

    def s5_state(re, im, bsz):
        return [jnp.stack([re[:, i].reshape(bsz, 2, -1), im[:, i].reshape(bsz, 2, -1)], axis=2).astype(F32)
                for i in range(n_even)]

    ngrp, npst = s5_a_re.shape[2], s5_a_re.shape[3]
    nh, dh = state_mlstm_C.shape[3], state_mlstm_C.shape[4]
    zeros_s5 = jnp.zeros((nb, n_even, 2, ngrp, npst), F32)
    y_prompt, s5_fin, ml_fin = _trunk(
        x_prompt.reshape(nb * seq, d), mods_ctx, s5_state(zeros_s5, zeros_s5, nb),
        [jnp.zeros((nb, 2, nh, dh, dh), F32)] * n_odd, [jnp.zeros((nb, 2, nh, dh), F32)] * n_odd,
        [jnp.zeros((nb, 2, nh, 1), F32)] * n_odd, p, nb, seq, nb * seq)
    x_lat = add_pos(x_sample.reshape(db * dseq, d), _pos_embed(dseq, d, GRID_W), dseq, _tile(dseq, ROW_TILE))
    y_sample, _, _ = _trunk(
        x_lat, mods_lat, s5_state(state_s5_re, state_s5_im, db),
        [state_mlstm_C[:, i].astype(F32) for i in range(n_odd)], [state_mlstm_n[:, i].astype(F32) for i in range(n_odd)],
        [state_mlstm_m[:, i].astype(F32)[..., None] for i in range(n_odd)], p, db, dseq, dseq)

    new_s5_re = jnp.stack([h[:, :, 0].reshape(nb, 2, ngrp, npst) for h in s5_fin], axis=1)
    new_s5_im = jnp.stack([h[:, :, 1].reshape(nb, 2, ngrp, npst) for h in s5_fin], axis=1)
    new_c = jnp.stack([f[0] for f in ml_fin], axis=1)
    new_n = jnp.stack([f[1] for f in ml_fin], axis=1)
    new_m = jnp.stack([f[2][..., 0] for f in ml_fin], axis=1)
    return (y_prompt.reshape(nb, seq, d), y_sample.reshape(db, dseq, d), new_s5_re, new_s5_im, new_c, new_n, new_m)
```

```python
import functools
import math

import jax
import jax.numpy as jnp
from jax import lax
from jax.experimental import pallas as pl
from jax.experimental.pallas import tpu as pltpu

F32 = jnp.float32
BF16 = jnp.bfloat16
EPS = 1e-6
HIGHEST = lax.Precision.HIGHEST
V7X_VMEM_LIMIT_BYTES = 56 * 1024 * 1024
LANES = 128
SUBLANES = 8
ML_CHUNK = 128
PK_TOPK = 16
GRID_W = 64
NEG_INF = float("-inf")
ROW_TILE = 1024
DFT_TILE = 512
ADA_COLS = 1536
CONV_COLS = 256
S5_CHUNK = 256
PEER_SCORE_TOKENS = 256
PEER_TOKENS = 512
PEER_EXPERTS = 2048


def _cp(*sem):
    return pltpu.CompilerParams(dimension_semantics=sem, vmem_limit_bytes=V7X_VMEM_LIMIT_BYTES)


def _dot(a, b, **kw):
    return jnp.dot(a, b, preferred_element_type=F32, **kw)


def _dot_nt(a, b):
    return lax.dot_general(a, b, (((1,), (1,)), ((), ())), preferred_element_type=F32)


def _silu(x):
    return x * jax.nn.sigmoid(x)


def _ada_kernel(c_ref, w_ref, b_ref, o_ref):
    o_ref[...] = _dot(_silu(c_ref[...]), w_ref[...], precision=HIGHEST) + b_ref[...]


def ada_mod(cond8, w, b):
    d, no = w.shape
    tn = _tile(no, ADA_COLS)
    return pl.pallas_call(
        _ada_kernel, grid=(no // tn,),
        in_specs=[pl.BlockSpec((8, d), lambda j: (0, 0)), pl.BlockSpec((d, tn), lambda j: (0, j)),
                  pl.BlockSpec((1, tn), lambda j: (0, j))],
        out_specs=pl.BlockSpec((8, tn), lambda j: (0, j)),
        out_shape=jax.ShapeDtypeStruct((8, no), F32), compiler_params=_cp("parallel"), name="ada_mod",
    )(cond8, w, b.reshape(1, no))


def _normmod(x, g, sc, sh):
    y = x * lax.rsqrt(jnp.mean(x * x, axis=-1, keepdims=True) + EPS)
    return (y * g) * (1.0 + sc) + sh


def _mod_spec(d, tm, rows_per_mod):
    return pl.BlockSpec((1, 1, d), lambda i, j: ((i * tm) // rows_per_mod, 0, 0))


def _nm_matmul_kernel(x_ref, g_ref, sc_ref, sh_ref, w_ref, o_ref, h_ref):
    @pl.when(pl.program_id(1) == 0)
    def _():
        h_ref[...] = _normmod(x_ref[...], g_ref[...], sc_ref[0], sh_ref[0]).astype(BF16)
    o_ref[...] = _dot(h_ref[...], w_ref[...])


def normmod_matmul(x, g, sc, sh, w_bf16, rows_per_mod, tm, tn):
    n, d = x.shape
    no = w_bf16.shape[1]
    return pl.pallas_call(
        _nm_matmul_kernel, grid=(n // tm, no // tn),
        in_specs=[pl.BlockSpec((tm, d), lambda i, j: (i, 0)), pl.BlockSpec((1, d), lambda i, j: (0, 0)),
                  _mod_spec(d, tm, rows_per_mod), _mod_spec(d, tm, rows_per_mod),
                  pl.BlockSpec((d, tn), lambda i, j: (0, j))],
        out_specs=pl.BlockSpec((tm, tn), lambda i, j: (i, j)),
        out_shape=jax.ShapeDtypeStruct((n, no), F32),
        scratch_shapes=[pltpu.VMEM((tm, d), BF16)],
        compiler_params=_cp("parallel", "arbitrary"), name="normmod_matmul",
    )(x, g.reshape(1, d), sc, sh, w_bf16)


def _add_rows_kernel(x_ref, p_ref, o_ref):
    o_ref[...] = x_ref[...] + p_ref[...]


def add_pos(x, pe, seq_len, tm):
    n, d = x.shape
    nb = seq_len // tm
    return pl.pallas_call(
        _add_rows_kernel, grid=(n // tm,),
        in_specs=[pl.BlockSpec((tm, d), lambda i: (i, 0)), pl.BlockSpec((tm, d), lambda i: (i % nb, 0))],
        out_specs=pl.BlockSpec((tm, d), lambda i: (i, 0)),
        out_shape=jax.ShapeDtypeStruct((n, d), F32), compiler_params=_cp("parallel"), name="add_pos",
    )(x, pe)


def _sconv_kernel(x_ref, w_ref, b_ref, s_ref, o_ref, *, act):
    x = x_ref[...]
    n_tok = x.shape[0]
    row = lax.broadcasted_iota(jnp.int32, x.shape, 0)
    prev = jnp.where(row == 0, 0.0, pltpu.roll(x, 1, 0))
    nxt = jnp.where(row == n_tok - 1, 0.0, pltpu.roll(x, n_tok - 1, 0))
    y = prev * w_ref[0:1, :] + x * w_ref[1:2, :] + nxt * w_ref[2:3, :] + b_ref[...]
    if act:
        y = _silu(y) * s_ref[...]
    o_ref[...] = y.astype(o_ref.dtype)


def short_conv(a, ncols, w, b, scale, seq_len, act, out_dtype):
    n = a.shape[0]
    cb = _tile(ncols, CONV_COLS)
    return pl.pallas_call(
        functools.partial(_sconv_kernel, act=act), grid=(n // seq_len, ncols // cb),
        in_specs=[pl.BlockSpec((seq_len, cb), lambda s, j: (s, j)), pl.BlockSpec((3, cb), lambda s, j: (0, j)),
                  pl.BlockSpec((1, cb), lambda s, j: (0, j)), pl.BlockSpec((1, cb), lambda s, j: (0, j))],
        out_specs=pl.BlockSpec((seq_len, cb), lambda s, j: (s, j)),
        out_shape=jax.ShapeDtypeStruct((n, ncols), out_dtype), compiler_params=_cp("parallel", "parallel"),
        name="short_conv",
    )(a, w, b.reshape(1, ncols), scale.reshape(1, ncols))


def dft_tables(n_tok):
    k = jnp.arange(n_tok, dtype=jnp.int32)
    blk = 1 << ((n_tok.bit_length() - 1) // 2)
    def thin(n):
        ang = ((k[:, None] * n[None, :]) % (2 * n_tok)).astype(F32) * (math.pi / n_tok)
        return jnp.cos(ang), jnp.sin(ang)
    (c_hi, s_hi), (c_lo, s_lo) = thin(jnp.arange(0, n_tok, blk, dtype=jnp.int32)), thin(jnp.arange(blk, dtype=jnp.int32))
    cos_t = (c_hi[:, :, None] * c_lo[:, None, :] - s_hi[:, :, None] * s_lo[:, None, :]).reshape(n_tok, n_tok)
    msin = -(s_hi[:, :, None] * c_lo[:, None, :] + c_hi[:, :, None] * s_lo[:, None, :]).reshape(n_tok, n_tok)
    alt = jnp.where(k % 2 == 0, 1.0, -1.0).astype(F32)
    a_t = msin.at[0, :].set(alt)
    a_tt = msin.at[:, 0].set(alt)
    return cos_t.astype(BF16), a_t.astype(BF16), a_tt.astype(BF16)


def _hyfilt_kernel(band_ref, w1_ref, b1_ref, w2_ref, b2_ref, w3_ref, fr_ref, dec_ref, h_ref, ss_ref, *,
                   n_tok, tl, hw, nbands):
    i = pl.program_id(0)
    pos = i * tl + lax.broadcasted_iota(jnp.int32, (tl, 1), 0)
    t = pos.astype(F32) / n_tok
    lane = lax.broadcasted_iota(jnp.int32, (tl, LANES), 1)
    ang = 2.0 * math.pi * t * band_ref[...]
    z = jnp.where(lane == 0, t, jnp.where(lane <= nbands, jnp.cos(ang),
                                          jnp.where(lane <= 2 * nbands, jnp.sin(ang), 0.0)))
    fr = fr_ref[...]
    h = jnp.sin(fr * (_dot(z, w1_ref[...], precision=HIGHEST) + b1_ref[...]))
    h = jnp.sin(fr * (_dot(h, w2_ref[...], precision=HIGHEST) + b2_ref[...]))
    h = _dot(h, w3_ref[...], precision=HIGHEST) * jnp.exp(-t * jnp.abs(dec_ref[...]))
    col = lax.broadcasted_iota(jnp.int32, h.shape, 1)
    is_bwd = (col // hw) % 2 == 1
    h = jnp.where(jnp.logical_and(is_bwd, pos == 0), 0.0, h)
    h_ref[...] = h.astype(BF16)

    @pl.when(i == 0)
    def _():
        ss_ref[...] = jnp.zeros_like(ss_ref)
    ss_ref[...] += jnp.sum(h * h, axis=0, keepdims=True)


def hyena_filter_taps(n_tok, w1, b1, w2, b2, w3, freq, decay, hw):
    emb, ffn = w1.shape
    nbands = (emb - 1) // 2
    tl = _tile(n_tok, DFT_TILE)
    bands = jnp.linspace(1e-4, nbands - 1, nbands, dtype=F32)
    band_row = jnp.zeros((1, LANES), F32).at[0, 1:1 + nbands].set(bands).at[0, 1 + nbands:1 + 2 * nbands].set(bands)
    w1p = jnp.zeros((LANES, ffn), F32).at[:emb].set(w1)
    nc = w3.shape[1]
    full = lambda shp: pl.BlockSpec(shp, lambda i: (0, 0))
    return pl.pallas_call(
        functools.partial(_hyfilt_kernel, n_tok=n_tok, tl=tl, hw=hw, nbands=nbands), grid=(n_tok // tl,),
        in_specs=[full((1, LANES)), full((LANES, ffn)), full((1, ffn)), full((ffn, ffn)), full((1, ffn)),
                  full((ffn, nc)), full((1, ffn)), full((1, nc))],
        out_specs=[pl.BlockSpec((tl, nc), lambda i: (i, 0)), full((1, nc))],
        out_shape=[jax.ShapeDtypeStruct((n_tok, nc), BF16), jax.ShapeDtypeStruct((1, nc), F32)],
        compiler_params=_cp("arbitrary"), name="hyena_filter_taps",
    )(band_row, w1p, b1.reshape(1, ffn), w2, b2.reshape(1, ffn), w3, freq.reshape(1, ffn), decay.reshape(1, nc))


def _filt_dft_kernel(c_ref, a_ref, h_ref, ss_ref, kr_ref, ki_ref, *, tf, hw):
    i = pl.program_id(1)
    hf = h_ref[:, :hw]
    hb = h_ref[:, hw:]
    cc = c_ref[...]
    aa = a_ref[...]
    zrf, zif, zrb, zib = _dot(cc, hf), _dot(aa, hf), _dot(cc, hb), _dot(aa, hb)
    scale = lax.rsqrt(ss_ref[:, :hw] + ss_ref[:, hw:] + EPS)
    first = (i * tf + lax.broadcasted_iota(jnp.int32, (tf, 1), 0)) == 0
    scale = scale * jnp.where(first, 0.5, 1.0)
    kr_ref[0] = (zrf + zrb) * scale
    ki_ref[0] = jnp.where(first, zif + zib, zif - zib) * scale


def hyena_filter_spectrum(cos_t, a_t, taps, sumsq, hw, tf):
    n_tok = cos_t.shape[0]
    norder = taps.shape[1] // (2 * hw)
    out = jax.ShapeDtypeStruct((norder, n_tok, hw), F32)
    return pl.pallas_call(
        functools.partial(_filt_dft_kernel, tf=tf, hw=hw), grid=(norder, n_tok // tf),
        in_specs=[pl.BlockSpec((tf, n_tok), lambda o, i: (i, 0)), pl.BlockSpec((tf, n_tok), lambda o, i: (i, 0)),
                  pl.BlockSpec((n_tok, 2 * hw), lambda o, i: (0, o)), pl.BlockSpec((1, 2 * hw), lambda o, i: (0, o))],
        out_specs=[pl.BlockSpec((1, tf, hw), lambda o, i: (o, i, 0))] * 2,
        out_shape=[out, out], compiler_params=_cp("parallel", "parallel"), name="hyena_filter_spectrum",
    )(cos_t, a_t, taps, sumsq)


def _hy_fwd_kernel(c_ref, a_ref, z_ref, kr_ref, ki_ref, yr_ref, yi_ref, *, tf):
    i = pl.program_id(0)
    zb = z_ref[...].astype(BF16)
    zr = _dot(c_ref[...], zb)
    zi = _dot(a_ref[...], zb)
    kr = kr_ref[0]
    ki = ki_ref[0]
    first = (i * tf + lax.broadcasted_iota(jnp.int32, (tf, 1), 0)) == 0
    yr_ref[...] = jnp.where(first, zr * kr, zr * kr - zi * ki).astype(BF16)
    yi_ref[...] = jnp.where(first, zi * ki, zr * ki + zi * kr).astype(BF16)


def hyena_fwd(cos_t, a_t, z, zcol, kr, ki, order, nseq, hw, tf):
    n_tok = cos_t.shape[0]
    nf = n_tok // tf
    out = jax.ShapeDtypeStruct((nseq * n_tok, hw), BF16)
    return pl.pallas_call(
        functools.partial(_hy_fwd_kernel, tf=tf), grid=(nf, nseq),
        in_specs=[pl.BlockSpec((tf, n_tok), lambda i, b: (i, 0)), pl.BlockSpec((tf, n_tok), lambda i, b: (i, 0)),
                  pl.BlockSpec((n_tok, hw), lambda i, b: (b, zcol)),
                  pl.BlockSpec((1, tf, hw), lambda i, b: (order, i, 0)),
                  pl.BlockSpec((1, tf, hw), lambda i, b: (order, i, 0))],
        out_specs=[pl.BlockSpec((tf, hw), lambda i, b: (b * nf + i, 0))] * 2,
        out_shape=[out, out], compiler_params=_cp("parallel", "parallel"), name="hyena_fwd",
    )(cos_t, a_t, z, kr, ki)


def _hy_inv_kernel(c_ref, at_ref, yr_ref, yi_ref, zp_ref, gate_ref, bias_ref, o_ref, *, inv_len):
    conv = (_dot(c_ref[...], yr_ref[...]) + _dot(at_ref[...], yi_ref[...])) * inv_len
    o_ref[...] = gate_ref[...] * (conv + bias_ref[...] * zp_ref[...])


def hyena_inv(cos_t, a_tt, yr, yi, zprev, zcol, gates, gcol, bias_row, nseq, hw, tf):
    n_tok = cos_t.shape[0]
    nf = n_tok // tf
    return pl.pallas_call(
        functools.partial(_hy_inv_kernel, inv_len=1.0 / n_tok), grid=(nf, nseq),
        in_specs=[pl.BlockSpec((tf, n_tok), lambda i, b: (i, 0)), pl.BlockSpec((tf, n_tok), lambda i, b: (i, 0)),
                  pl.BlockSpec((n_tok, hw), lambda i, b: (b, 0)), pl.BlockSpec((n_tok, hw), lambda i, b: (b, 0)),
                  pl.BlockSpec((tf, hw), lambda i, b: (b * nf + i, zcol)),
                  pl.BlockSpec((tf, hw), lambda i, b: (b * nf + i, gcol)),
                  pl.BlockSpec((1, hw), lambda i, b: (0, 0))],
        out_specs=pl.BlockSpec((tf, hw), lambda i, b: (b * nf + i, 0)),
        out_shape=jax.ShapeDtypeStruct((nseq * n_tok, hw), F32),
        compiler_params=_cp("parallel", "parallel"), name="hyena_inv",
    )(cos_t, a_tt, yr, yi, zprev, gates, bias_row)


S5_DIAG_BLOCKS = 2


S5_SEQS_PER_STEP = 4


def _s5_kernel(u_ref, bre_ref, bim_ref, cre_ref, cim_ref, lam_ref, h0_ref, y_ref, hfin_ref, hre_s, him_s, st_s, *,
               tc, nc, ns, nb):
    d = pl.program_id(0)
    c = pl.program_id(2)

    @pl.when(c == 0)
    def _():
        st_s[...] = h0_ref[:, 0]

    sw = u_ref.shape[2]
    halves = [(slice(j * sw // S5_DIAG_BLOCKS, (j + 1) * sw // S5_DIAG_BLOCKS),
               slice(j * ns // S5_DIAG_BLOCKS, (j + 1) * ns // S5_DIAG_BLOCKS)) for j in range(S5_DIAG_BLOCKS)]
    for j in range(nb):
        ub = u_ref[j].astype(BF16)
        for us, hs in halves:
            hre_s[j, :, hs] = _dot(ub[:, us], bre_ref[0, us, hs])
            him_s[j, :, hs] = _dot(ub[:, us], bim_ref[0, us, hs])
    lr = lam_ref[0, 0:1, :]
    li = lam_ref[0, 1:2, :]

    def body(t, carry):
        r = jnp.where(d == 0, t, tc - 1 - t)
        new = []
        for j, (hr, hi) in enumerate(carry):
            nr = lr * hr - li * hi + hre_s[j, pl.ds(r, 1), :]
            ni = lr * hi + li * hr + him_s[j, pl.ds(r, 1), :]
            hre_s[j, pl.ds(r, 1), :] = nr
            him_s[j, pl.ds(r, 1), :] = ni
            new.append((nr, ni))
        return tuple(new)

    start = tuple((st_s[j, 0:1, :], st_s[j, 1:2, :]) for j in range(nb))
    for j, (hr, hi) in enumerate(lax.fori_loop(0, tc, body, start, unroll=4)):
        st_s[j, 0:1, :] = hr
        st_s[j, 1:2, :] = hi
    for j in range(nb):
        for us, hs in halves:
            y_ref[0, j, :, us] = (_dot(hre_s[j, :, hs].astype(BF16), cre_ref[0, hs, us])
                                  + _dot(him_s[j, :, hs].astype(BF16), cim_ref[0, hs, us]))

    @pl.when(c == nc - 1)
    def _():
        hfin_ref[:, 0] = st_s[...]


def s5_scan(proj, ucol, bre, bim, cre, cim, lam, h0, nseq, seq_len, sw, tc):
    ns = bre.shape[2]
    nc = seq_len // tc
    nb = S5_SEQS_PER_STEP if nseq % S5_SEQS_PER_STEP == 0 else 1

    def chunk(d, c):
        return c + d * (nc - 1 - 2 * c)

    y, hfin = pl.pallas_call(
        functools.partial(_s5_kernel, tc=tc, nc=nc, ns=ns, nb=nb), grid=(2, nseq // nb, nc),
        in_specs=[pl.BlockSpec((nb, tc, sw), lambda d, b, c: (b, chunk(d, c), ucol)),
                  pl.BlockSpec((1, sw, ns), lambda d, b, c: (d, 0, 0)),
                  pl.BlockSpec((1, sw, ns), lambda d, b, c: (d, 0, 0)),
                  pl.BlockSpec((1, ns, sw), lambda d, b, c: (d, 0, 0)),
                  pl.BlockSpec((1, ns, sw), lambda d, b, c: (d, 0, 0)),
                  pl.BlockSpec((1, 2, ns), lambda d, b, c: (d, 0, 0)),
                  pl.BlockSpec((nb, 1, 2, ns), lambda d, b, c: (b, d, 0, 0))],
        out_specs=[pl.BlockSpec((1, nb, tc, sw), lambda d, b, c: (d, b, chunk(d, c), 0)),
                   pl.BlockSpec((nb, 1, 2, ns), lambda d, b, c: (b, d, 0, 0))],
        out_shape=[jax.ShapeDtypeStruct((2, nseq, seq_len, sw), F32), jax.ShapeDtypeStruct((nseq, 2, 2, ns), F32)],
        scratch_shapes=[pltpu.VMEM((nb, tc, ns), F32), pltpu.VMEM((nb, tc, ns), F32), pltpu.VMEM((nb, 2, ns), F32)],
        compiler_params=_cp("parallel", "parallel", "arbitrary"), name="s5_scan",
    )(proj.reshape(nseq, seq_len, proj.shape[1]), bre, bim, cre, cim, lam, h0)
    return y.reshape(2, nseq * seq_len, sw), hfin


def _s5_glu_kernel(yf_ref, yb_ref, u_ref, d_ref, w_ref, b_ref, o_ref):
    y = jax.nn.gelu(yf_ref[0] + yb_ref[0] + d_ref[...] * u_ref[...])
    o_ref[...] = y * jax.nn.sigmoid(_dot(y.astype(BF16), w_ref[...]) + b_ref[...])


def s5_glu(y2, proj, ucol, d_skip, glu_w_bf16, glu_b, tm):
    _, n, sw = y2.shape
    return pl.pallas_call(
        _s5_glu_kernel, grid=(n // tm,),
        in_specs=[pl.BlockSpec((1, tm, sw), lambda i: (0, i, 0)), pl.BlockSpec((1, tm, sw), lambda i: (1, i, 0)),
                  pl.BlockSpec((tm, sw), lambda i: (i, ucol)), pl.BlockSpec((1, sw), lambda i: (0, 0)),
                  pl.BlockSpec((sw, sw), lambda i: (0, 0)), pl.BlockSpec((1, sw), lambda i: (0, 0))],
        out_specs=pl.BlockSpec((tm, sw), lambda i: (i, 0)),
        out_shape=jax.ShapeDtypeStruct((n, sw), F32), compiler_params=_cp("parallel"), name="s5_glu",
    )(y2, y2, proj, d_skip.reshape(1, sw), glu_w_bf16, glu_b.reshape(1, sw))


def _residual_and_next_norm(x_ref, gate_ref, y, g2_ref, sc2_ref, sh2_ref, o_ref, hn_ref):
    xn = x_ref[...] + gate_ref[0] * y
    o_ref[...] = xn
    hn_ref[...] = _normmod(xn, g2_ref[...], sc2_ref[0], sh2_ref[0]).astype(BF16)


def _row_specs(d, tm, rows_per_mod):
    mod = pl.BlockSpec((1, 1, d), lambda i: ((i * tm) // rows_per_mod, 0, 0))
    return [pl.BlockSpec((tm, d), lambda i: (i, 0)), mod, pl.BlockSpec((1, d), lambda i: (0, 0)), mod, mod]


def _row_outs(n, d, tm):
    spec = pl.BlockSpec((tm, d), lambda i: (i, 0))
    return [spec, spec], [jax.ShapeDtypeStruct((n, d), F32), jax.ShapeDtypeStruct((n, d), BF16)]


def _even_out_kernel(a_ref, b_ref, wa_ref, wb_ref, x_ref, gate_ref, g2_ref, sc2_ref, sh2_ref, o_ref, hn_ref):
    y = _dot(a_ref[...].astype(BF16), wa_ref[...]) + _dot(b_ref[...].astype(BF16), wb_ref[...])
    _residual_and_next_norm(x_ref, gate_ref, y, g2_ref, sc2_ref, sh2_ref, o_ref, hn_ref)


def even_out(hy, s5o, w_bf16, x, gate, g2, sc2, sh2, rows_per_mod, tm):
    n, d = x.shape
    hw = hy.shape[1]
    sw = s5o.shape[1]
    out_specs, out_shape = _row_outs(n, d, tm)
    return pl.pallas_call(
        _even_out_kernel, grid=(n // tm,),
        in_specs=[pl.BlockSpec((tm, hw), lambda i: (i, 0)), pl.BlockSpec((tm, sw), lambda i: (i, 0)),
                  pl.BlockSpec((hw, d), lambda i: (0, 0)), pl.BlockSpec((sw, d), lambda i: (hw // sw, 0))]
        + _row_specs(d, tm, rows_per_mod),
        out_specs=out_specs, out_shape=out_shape, compiler_params=_cp("parallel"), name="even_out",
    )(hy, s5o, w_bf16, w_bf16, x, gate, g2.reshape(1, d), sc2, sh2)


ML_HEAD_GROUP = 8


def _log_sigmoid(x):
    return jnp.minimum(x, 0.0) - jnp.log1p(jnp.exp(-jnp.abs(x)))


def _mlstm_kernel(q_ref, k_ref, v_ref, g_ref, gb_ref, c0_ref, n0_ref, m0_ref, h_ref, cf_ref, nf_ref, mf_ref,
                  c_s, m_s, *, nh, dh, tc, nc):
    d = pl.program_id(0)
    c = pl.program_id(2)

    @pl.when(c == 0)
    def _():
        for h in range(nh):
            c_s[h, :, :dh] = c0_ref[0, 0, h]
            c_s[h, :, dh:] = jnp.broadcast_to(n0_ref[0, 0, h:h + 1, :], (dh, dh)).T
        m_s[...] = m0_ref[0, 0]

    ones = jnp.ones((tc, dh), F32)
    gates = g_ref[...] + gb_ref[0]
    lane = lax.broadcasted_iota(jnp.int32, gates.shape, 1)
    logf = jnp.where(jnp.logical_and(lane >= nh, lane < 2 * nh), _log_sigmoid(gates), 0.0)
    r_i = lax.broadcasted_iota(jnp.int32, (tc, tc), 0)
    s_i = lax.broadcasted_iota(jnp.int32, (tc, tc), 1)
    causal = (r_i - s_i) * (1 - 2 * d) >= 0
    bcum = _dot(causal.astype(F32), logf, precision=HIGHEST)
    btot = jnp.sum(logf, axis=0, keepdims=True)
    gates_t = gates.T
    bcum_t = bcum.T
    for g0 in range(0, nh, ML_HEAD_GROUP):
        hds = list(range(g0, min(g0 + ML_HEAD_GROUP, nh)))
        hsl = {h: slice(h * dh, (h + 1) * dh) for h in hds}
        b_col = {h: bcum[:, nh + h:nh + h + 1] for h in hds}
        m_old = {h: m_s[h:h + 1, :] for h in hds}
        a = {h: b_col[h] + m_old[h] for h in hds}
        src = {h: jnp.where(causal, gates_t[h:h + 1, :] - bcum_t[nh + h:nh + h + 1, :], NEG_INF) for h in hds}
        mq = {h: jnp.maximum(a[h], b_col[h] + jnp.max(src[h], axis=-1, keepdims=True)) for h in hds}
        s = {h: _dot_nt(q_ref[:, hsl[h]], k_ref[:, hsl[h]]) * jnp.exp(src[h] + (b_col[h] - mq[h])) for h in hds}
        v1 = {h: jnp.concatenate([v_ref[:, hsl[h]], ones], axis=1) for h in hds}
        cn = {h: c_s[h] for h in hds}
        qw = {h: jnp.exp(a[h] - mq[h]) * q_ref[:, hsl[h]] for h in hds}
        both = {h: _dot(s[h], v1[h]) + _dot(qw[h], cn[h]) for h in hds}
        for h in hds:
            h_ref[0, :, hsl[h]] = both[h][:, :dh] / jnp.maximum(jnp.abs(both[h][:, dh:]), jnp.exp(-mq[h]))
        b_last = {h: btot[:, nh + h:nh + h + 1] for h in hds}
        g = {h: b_last[h] - b_col[h] + gates[:, h:h + 1] for h in hds}
        m_new = {h: jnp.maximum(b_last[h] + m_old[h], jnp.max(g[h], axis=0, keepdims=True)) for h in hds}
        kw = {h: k_ref[:, hsl[h]] * jnp.exp(g[h] - m_new[h]) for h in hds}
        for h in hds:
            c_s[h] = jnp.exp(b_last[h] + m_old[h] - m_new[h]) * cn[h] + _dot(kw[h].T, v1[h])
            m_s[h:h + 1, :] = m_new[h]

    @pl.when(c == nc - 1)
    def _():
        for h in range(nh):
            cf_ref[0, 0, h] = c_s[h, :, :dh]
            nf_ref[0, 0, h:h + 1, :] = c_s[h, :, dh:].T[0:1, :]
        mf_ref[0, 0] = m_s[...]


def mlstm_scan(qk, proj, vcol, gate_bias, c0, n0, m0, nseq, seq_len, nh, dh):
    tc = ML_CHUNK
    nc = seq_len // tc
    w = nh * dh

    def chunk(d, c):
        return c + d * (nc - 1 - 2 * c)

    rowblk = lambda d, b, c: b * nc + chunk(d, c)
    st = lambda shp: pl.BlockSpec((1, 1) + shp, lambda d, b, c: (b, d) + (0,) * len(shp))
    return pl.pallas_call(
        functools.partial(_mlstm_kernel, nh=nh, dh=dh, tc=tc, nc=nc), grid=(2, nseq, nc),
        in_specs=[pl.BlockSpec((tc, w), lambda d, b, c: (rowblk(d, b, c), 0)),
                  pl.BlockSpec((tc, w), lambda d, b, c: (rowblk(d, b, c), 1)),
                  pl.BlockSpec((tc, w), lambda d, b, c: (rowblk(d, b, c), vcol)),
                  pl.BlockSpec((tc, LANES), lambda d, b, c: (rowblk(d, b, c), 4 * w // LANES + d)),
                  pl.BlockSpec((1, 1, LANES), lambda d, b, c: (d, 0, 0)),
                  st((nh, dh, dh)), st((nh, dh)), st((nh, 1))],
        out_specs=[pl.BlockSpec((1, tc, w), lambda d, b, c: (d, rowblk(d, b, c), 0)),
                   st((nh, dh, dh)), st((nh, dh)), st((nh, 1))],
        out_shape=[jax.ShapeDtypeStruct((2, nseq * seq_len, w), F32),
                   jax.ShapeDtypeStruct((nseq, 2, nh, dh, dh), F32), jax.ShapeDtypeStruct((nseq, 2, nh, dh), F32),
                   jax.ShapeDtypeStruct((nseq, 2, nh, 1), F32)],
        scratch_shapes=[pltpu.VMEM((nh, dh, 2 * dh), F32), pltpu.VMEM((nh, 1), F32)],
        compiler_params=_cp("parallel", "parallel", "arbitrary"), name="mlstm_scan",
    )(qk, qk, proj, proj, gate_bias, c0, n0, m0)


def _odd_out_kernel(hf_ref, hb_ref, og_ref, ng_ref, w_ref, x_ref, gate_ref, g2_ref, sc2_ref, sh2_ref, o_ref, hn_ref,
                    a_s, *, nh, dh):
    for h in range(nh):
        hs = slice(h * dh, (h + 1) * dh)
        blk = hf_ref[0, :, hs] + hb_ref[0, :, hs]
        blk = blk * lax.rsqrt(jnp.mean(blk * blk, axis=-1, keepdims=True) + EPS)
        a_s[:, hs] = ((blk * ng_ref[:, hs]) * _silu(og_ref[:, hs])).astype(BF16)
    _residual_and_next_norm(x_ref, gate_ref, _dot(a_s[...], w_ref[...]), g2_ref, sc2_ref, sh2_ref, o_ref, hn_ref)


def odd_out(h2, proj, ocol, norm_g, w_bf16, x, gate, g2, sc2, sh2, rows_per_mod, nh, dh, tm):
    n, d = x.shape
    w = nh * dh
    out_specs, out_shape = _row_outs(n, d, tm)
    return pl.pallas_call(
        functools.partial(_odd_out_kernel, nh=nh, dh=dh), grid=(n // tm,),
        in_specs=[pl.BlockSpec((1, tm, w), lambda i: (0, i, 0)), pl.BlockSpec((1, tm, w), lambda i: (1, i, 0)),
                  pl.BlockSpec((tm, w), lambda i: (i, ocol)), pl.BlockSpec((1, w), lambda i: (0, 0)),
                  pl.BlockSpec((w, d), lambda i: (0, 0))] + _row_specs(d, tm, rows_per_mod),
        out_specs=out_specs, out_shape=out_shape, scratch_shapes=[pltpu.VMEM((tm, w), BF16)],
        compiler_params=_cp("parallel"), name="odd_out",
    )(h2, h2, proj, norm_g.reshape(1, w), w_bf16, x, gate, g2.reshape(1, d), sc2, sh2)


def _sort_network(n):
    pairs, p = [], 1
    while p < n:
        k = p
        while k >= 1:
            for j in range(k % p, n - k, 2 * k):
                for i in range(min(k, n - j - k)):
                    if (i + j) // (2 * p) == (i + j + k) // (2 * p):
                        pairs.append((i + j, i + j + k))
            k //= 2
        p *= 2
    return pairs


def _top_values_tiled(arrays, k, outs):
    cols = []
    for arr in arrays:
        tiles = [arr[r:r + SUBLANES, :] for r in range(0, arr.shape[0], SUBLANES)]
        for lo, hi in _sort_network(len(tiles)):
            tiles[lo], tiles[hi] = jnp.maximum(tiles[lo], tiles[hi]), jnp.minimum(tiles[lo], tiles[hi])
        cols.append(tiles)
    for j in range(k):
        for tiles, out_s in zip(cols, outs):
            m = jnp.max(tiles[0], axis=0, keepdims=True)
            out_s[j:j + 1, :] = m
            hit = tiles[0] == m
            for i in range(min(len(tiles), k - 1 - j)):
                below = tiles[i + 1] if i + 1 < len(tiles) else NEG_INF
                tiles[i] = jnp.where(hit, below, tiles[i])


def _pair_candidates(k):
    return [(a, k // (a + 1)) for a in range(k)]


PEER_HEADS_PER_TRIP = 8


def _peer_score_kernel(h_ref, wq_ref, keys_ref, th_ref, s2_ref, w1_ref, w2_ref, q_s, v_s, cand_s, best_s, *,
                       nh, half, topk):
    q_s[...] = _dot_nt(wq_ref[...], h_ref[...])
    kk = topk + 1
    cand_s[...] = jnp.full(cand_s.shape, NEG_INF, F32)
    group = range(PEER_HEADS_PER_TRIP)

    def heads(trip, carry):
        hds = [trip * PEER_HEADS_PER_TRIP + u for u in group]
        scores = []
        for hd in hds:
            base = pl.multiple_of(hd * 2 * half, 2 * half)
            scores.append(_dot(keys_ref[hd, 0], q_s[pl.ds(base, half), :]))
            scores.append(_dot(keys_ref[hd, 1], q_s[pl.ds(base + half, half), :]))
        _top_values_tiled(scores, kk, [v_s.at[u, c] for u in group for c in range(2)])
        for u in group:
            off = 0
            for a, cnt in _pair_candidates(kk):
                cand_s[u, off:off + cnt, :] = v_s[u, 0, a:a + 1, :] + v_s[u, 1, 0:cnt, :]
                off += cnt
        _top_values_tiled([cand_s[u] for u in group], kk, [best_s.at[u] for u in group])
        for u, hd in enumerate(hds):
            s1, s2 = scores[2 * u], scores[2 * u + 1]
            best = best_s[u, 0:topk, :]
            z = jnp.sum(jnp.exp(best - best[0:1, :]), axis=0, keepdims=True)
            tmid = 0.5 * (best_s[u, topk - 1:topk, :] + best_s[u, topk:topk + 1, :])
            th = tmid - s1
            w1 = jnp.exp(s1 - v_s[u, 0, 0:1, :]) / z
            w2 = jnp.exp(s2 - v_s[u, 1, 0:1, :])
            for lt in range(s1.shape[1] // LANES):
                sl = slice(lt * LANES, (lt + 1) * LANES)
                th_ref[hd, lt] = th[:, sl]
                s2_ref[hd, lt] = s2[:, sl]
                w1_ref[hd, lt] = w1[:, sl]
                w2_ref[hd, lt] = w2[:, sl]
        return carry

    lax.fori_loop(0, nh // PEER_HEADS_PER_TRIP, heads, 0)


def peer_scores(h_bf16, wq_t_bf16, keys, tt):
    n, d = h_bf16.shape
    nh, _, nk, half = keys.shape
    kk = PK_TOPK + 1
    top_rows = -(-kk // SUBLANES) * SUBLANES
    ncand = SUBLANES
    while ncand < sum(c for _, c in _pair_candidates(kk)):
        ncand *= 2
    big = jax.ShapeDtypeStruct((nh, n // LANES, nk, LANES), F32)
    bspec = pl.BlockSpec((nh, tt // LANES, nk, LANES), lambda i: (0, i, 0, 0))
    return pl.pallas_call(
        functools.partial(_peer_score_kernel, nh=nh, half=half, topk=PK_TOPK), grid=(n // tt,),
        in_specs=[pl.BlockSpec((tt, d), lambda i: (i, 0)), pl.BlockSpec((nh * 2 * half, d), lambda i: (0, 0)),
                  pl.BlockSpec((nh, 2, nk, half), lambda i: (0, 0, 0, 0))],
        out_specs=[bspec, bspec, bspec, bspec],
        out_shape=[big, big, big, big],
        scratch_shapes=[pltpu.VMEM((nh * 2 * half, tt), F32), pltpu.VMEM((PEER_HEADS_PER_TRIP, 2, top_rows, tt), F32),
                        pltpu.VMEM((PEER_HEADS_PER_TRIP, ncand, tt), F32),
                        pltpu.VMEM((PEER_HEADS_PER_TRIP, top_rows, tt), F32)],
        compiler_params=_cp("parallel"), name="peer_scores",
    )(h_bf16, wq_t_bf16, keys)


PEER_KEY_ROWS = 16
GELU_C1 = math.sqrt(2.0 / math.pi)
GELU_C2 = 0.044715 * GELU_C1


def _gelu_tanh(x):
    half_x = 0.5 * x
    return half_x + half_x * jnp.tanh(x * (GELU_C1 + GELU_C2 * (x * x)))


def _peer_dense_kernel(h_ref, u_ref, vt_ref, th_ref, s2_ref, w1_ref, w2_ref, x_ref, gate_ref, fg_ref, o_ref,
                       acc_s, st_s, wt_s, *, nh, nk, ec, tt, final):
    e = pl.program_id(1)
    nlt = tt // LANES
    n_i1 = ec // nk
    nkt = nk // PEER_KEY_ROWS
    nsub = PEER_KEY_ROWS // SUBLANES

    @pl.when(e == 0)
    def _():
        acc_s[...] = jnp.zeros_like(acc_s)

    st = _gelu_tanh(_dot_nt(u_ref[...], h_ref[...]))
    for lt in range(nlt):
        st_s[lt] = st[:, lt * LANES:(lt + 1) * LANES]

    def tile(idx, carry):
        lt = idx // nkt
        k0 = (idx % nkt) * PEER_KEY_ROWS
        subs = [pl.ds(pl.multiple_of(k0 + j * SUBLANES, SUBLANES), SUBLANES) for j in range(nsub)]
        g = [[jnp.zeros((SUBLANES, LANES), F32) for _ in subs] for _ in range(n_i1)]
        for hd in range(nh):
            s2t = [s2_ref[hd, lt, sub, :] for sub in subs]
            w2t = [w2_ref[hd, lt, sub, :] for sub in subs]
            for li in range(n_i1):
                thb = jnp.broadcast_to(th_ref[hd, lt, li:li + 1, :], (SUBLANES, LANES))
                w1b = jnp.broadcast_to(w1_ref[hd, lt, li:li + 1, :], (SUBLANES, LANES))
                for j in range(nsub):
                    g[li][j] = g[li][j] + jnp.where(s2t[j] >= thb, w2t[j] * w1b, 0.0)
        for li in range(n_i1):
            rows = pl.ds(pl.multiple_of(li * nk + k0, PEER_KEY_ROWS), PEER_KEY_ROWS)
            wt_s[lt, rows, :] = (st_s[lt, rows, :] * jnp.concatenate(g[li], axis=0)).astype(BF16)
        return carry

    lax.fori_loop(0, nlt * nkt, tile, 0)
    wt = jnp.concatenate([wt_s[lt] for lt in range(nlt)], axis=1)
    acc_s[...] += _dot(vt_ref[0], wt)

    @pl.when(e == pl.num_programs(1) - 1)
    def _():
        xn = x_ref[...] + gate_ref[0] * acc_s[...].T
        if final:
            xn = (xn * lax.rsqrt(jnp.mean(xn * xn, axis=-1, keepdims=True) + EPS)) * fg_ref[...]
        o_ref[...] = xn


def peer_dense(h_bf16, u_bf16, vt_bf16, th, s2, w1, w2, x, gate, final_g, final, rows_per_mod, tt, ec):
    n, d = x.shape
    nh, _, nk, _ = s2.shape
    nchunk = u_bf16.shape[0] // ec
    bspec = pl.BlockSpec((nh, tt // LANES, nk, LANES), lambda i, e: (0, i, 0, 0))
    rspec = pl.BlockSpec((nh, tt // LANES, ec // nk, LANES), lambda i, e: (0, i, e, 0))
    tile_buf = (tt // LANES, ec, LANES)
    return pl.pallas_call(
        functools.partial(_peer_dense_kernel, nh=nh, nk=nk, ec=ec, tt=tt, final=final), grid=(n // tt, nchunk),
        in_specs=[pl.BlockSpec((tt, d), lambda i, e: (i, 0)), pl.BlockSpec((ec, d), lambda i, e: (e, 0)),
                  pl.BlockSpec((1, d, ec), lambda i, e: (e, 0, 0)), rspec, bspec, rspec, bspec,
                  pl.BlockSpec((tt, d), lambda i, e: (i, 0)),
                  pl.BlockSpec((1, 1, d), lambda i, e: ((i * tt) // rows_per_mod, 0, 0)),
                  pl.BlockSpec((1, d), lambda i, e: (0, 0))],
        out_specs=pl.BlockSpec((tt, d), lambda i, e: (i, 0)),
        out_shape=jax.ShapeDtypeStruct((n, d), F32),
        scratch_shapes=[pltpu.VMEM((d, tt), F32), pltpu.VMEM(tile_buf, F32), pltpu.VMEM(tile_buf, BF16)],
        compiler_params=_cp("parallel", "arbitrary"), name="peer_dense",
    )(h_bf16, u_bf16, vt_bf16, th, s2, w1, w2, x, gate, final_g.reshape(1, d))


def _s5_params(a_re, a_im, b_re, b_im, c_re, c_im, log_step):
    lam = lax.complex(a_re.astype(F32), a_im.astype(F32))
    lam_bar = jnp.exp(lam * jnp.exp(log_step.astype(F32))[..., None])
    b_bar = ((lam_bar - 1.0) / lam)[..., None] * lax.complex(b_re.astype(F32), b_im.astype(F32))
    ngrp, npst, nch = b_bar.shape[1:]
    eye = jnp.eye(ngrp, dtype=F32)

    def b_mat(part):
        return jnp.einsum("dgpj,gh->dgjhp", part, eye).reshape(2, ngrp * nch, ngrp * npst)

    def c_mat(part):
        return jnp.einsum("dgjp,gh->dgphj", part, eye).reshape(2, ngrp * npst, ngrp * nch)

    bre, bim = b_mat(b_bar.real).astype(BF16), b_mat(b_bar.imag).astype(BF16)
    cre, cim = c_mat(c_re.astype(F32)).astype(BF16), c_mat(-c_im.astype(F32)).astype(BF16)
    lam2 = jnp.stack([lam_bar.real.reshape(2, -1), lam_bar.imag.reshape(2, -1)], axis=1)
    return bre, bim, cre, cim, lam2


def _pos_embed(n_tok, d, grid_w):
    rows = n_tok // grid_w
    quarter = d // 4
    omega = 1.0 / (10000.0 ** (jnp.arange(quarter, dtype=F32) / quarter))

    def emb1d(pos):
        ang = pos.astype(F32)[:, None] * omega[None]
        return jnp.concatenate([jnp.sin(ang), jnp.cos(ang)], axis=-1)

    er = emb1d(jnp.arange(rows))
    ec = emb1d(jnp.arange(grid_w))
    half = d // 2
    pe = jnp.concatenate([jnp.broadcast_to(er[:, None], (rows, grid_w, half)),
                          jnp.broadcast_to(ec[None], (rows, grid_w, half))], axis=-1)
    return pe.reshape(rows * grid_w, d)


def _tile(n, pref):
    return pref if n % pref == 0 else n


def _trunk(x, mods, s5_h0, ml_c0, ml_n0, ml_m0, p, nseq, seq_len, rows_per_mod):
    n, d = x.shape
    tm = _tile(min(rows_per_mod, n), ROW_TILE)
    depth = p["norm_g"].shape[0]
    s5_fin, ml_fin = [], []
    for l in range(depth):
        sh1, sc1, g1, sh2, sc2, g2 = mods[l]
        i = l // 2
        if l % 2 == 0:
            hw = p["hy_bias"].shape[2]
            sw = p["s5_d"].shape[1]
            proj = normmod_matmul(x, p["norm_g"][l, 0], sc1, sh1, p["ev_w_in"][i].astype(BF16), rows_per_mod, tm,
                                  3 * hw + sw)
            hy_in = short_conv(proj, 3 * hw, p["hy_conv_w"][i], p["hy_conv_b"][i], jnp.ones((3 * hw,), F32),
                               seq_len, act=False, out_dtype=F32)
            cos_t, a_t, a_tt = dft_tables(seq_len)
            tf = _tile(seq_len, DFT_TILE)
            taps, sumsq = hyena_filter_taps(seq_len, p["hy_w1"][i], p["hy_b1"][i], p["hy_w2"][i], p["hy_b2"][i],
                                            p["hy_w3"][i], p["hy_freq"][i], p["hy_decay"][i], hw)
            kr, ki = hyena_filter_spectrum(cos_t, a_t, taps, sumsq, hw, tf)
            bias = p["hy_bias"][i].astype(F32)
            z, zcol = hy_in, 0
            for o in range(bias.shape[0]):
                yr, yi = hyena_fwd(cos_t, a_t, z, zcol, kr, ki, o, nseq, hw, tf)
                z = hyena_inv(cos_t, a_tt, yr, yi, z, zcol, hy_in, 1 + o, bias[o:o + 1], nseq, hw, tf)
                zcol = 0
            bre, bim, cre, cim, lam2 = _s5_params(p["s5_a_re"][i], p["s5_a_im"][i], p["s5_b_re"][i], p["s5_b_im"][i],
                                                  p["s5_c_re"][i], p["s5_c_im"][i], p["s5_log_step"][i])
            ucol = 3 * hw // sw
            y2, hfin = s5_scan(proj, ucol, bre, bim, cre, cim, lam2, s5_h0[i], nseq, seq_len, sw,
                               _tile(seq_len, S5_CHUNK))
            s5_fin.append(hfin)
            s5o = s5_glu(y2, proj, ucol, p["s5_d"][i], p["s5_glu_w"][i].astype(BF16), p["s5_glu_b"][i], tm)
            x, hn = even_out(z, s5o, p["ev_w_out"][i].astype(BF16), x, g1, p["norm_g"][l, 1], sc2, sh2, rows_per_mod, tm)
        else:
            nh = p["od_gate_b"].shape[2]
            w = p["ml_norm_g"].shape[1]
            dh = w // nh
            w_in = p["od_w_in"][i]
            wg = w_in[:, 4 * w:].reshape(d, 4, nh)
            gb = p["od_gate_b"][i].astype(F32)
            wg2 = jnp.zeros((d, 2, LANES), w_in.dtype)
            bg2 = jnp.zeros((2, 1, LANES), F32)
            for dr in range(2):
                wg2 = wg2.at[:, dr, :nh].set(wg[:, dr]).at[:, dr, nh:2 * nh].set(wg[:, 2 + dr])
                bg2 = bg2.at[dr, 0, :nh].set(gb[dr]).at[dr, 0, nh:2 * nh].set(gb[2 + dr])
            w_all = jnp.concatenate([w_in[:, :4 * w], wg2.reshape(d, 2 * LANES)], axis=1).astype(BF16)
            proj = normmod_matmul(x, p["norm_g"][l, 0], sc1, sh1, w_all, rows_per_mod, tm, w_all.shape[1] // 2)
            qscale = jnp.concatenate([jnp.full((w,), dh ** -0.5, F32), jnp.ones((w,), F32)])
            qk = short_conv(proj, 2 * w, p["ml_conv_w"][i], p["ml_conv_b"][i], qscale, seq_len, act=True, out_dtype=BF16)
            h2, cf, nf, mf = mlstm_scan(qk, proj, 2, bg2, ml_c0[i], ml_n0[i], ml_m0[i], nseq, seq_len, nh, dh)
            ml_fin.append((cf, nf, mf))
            x, hn = odd_out(h2, proj, 3, p["ml_norm_g"][i], p["od_w_out"][i].astype(BF16), x, g1, p["norm_g"][l, 1],
                            sc2, sh2, rows_per_mod, nh, dh, tm)
        tt = _tile(min(rows_per_mod, n), PEER_TOKENS)
        th, s2, w1, w2 = peer_scores(hn, p["pk_w_q"][l].T.astype(BF16), p["pk_keys"][l].astype(F32),
                                     _tile(tt, PEER_SCORE_TOKENS))
        ec = _tile(p["pk_u"].shape[1], PEER_EXPERTS)
        vt = p["pk_v"][l].astype(BF16).reshape(-1, ec, d).transpose(0, 2, 1)
        x = peer_dense(hn, p["pk_u"][l].astype(BF16), vt, th, s2, w1, w2, x, g2, p["final_g"], l == depth - 1,
                       rows_per_mod, tt, ec)
    return x, s5_fin, ml_fin


def kernel(x_prompt, x_sample, state_s5_re, state_s5_im, state_mlstm_C, state_mlstm_n, state_mlstm_m, c, c_ctx, norm_g, ada_w, ada_b, final_g, ev_w_in, hy_conv_w, hy_conv_b, hy_w1, hy_b1, hy_w2, hy_b2, hy_w3, hy_freq, hy_decay, hy_bias, s5_a_re, s5_a_im, s5_b_re, s5_b_im, s5_c_re, s5_c_im, s5_log_step, s5_d, s5_glu_w, s5_glu_b, ev_w_out, od_w_in, od_gate_b, ml_conv_w, ml_conv_b, ml_norm_g, od_w_out, pk_w_q, pk_keys, pk_u, pk_v):
    p = dict(norm_g=norm_g, ada_w=ada_w, ada_b=ada_b, final_g=final_g, ev_w_in=ev_w_in,
             hy_conv_w=hy_conv_w, hy_conv_b=hy_conv_b, hy_w1=hy_w1, hy_b1=hy_b1, hy_w2=hy_w2, hy_b2=hy_b2,
             hy_w3=hy_w3, hy_freq=hy_freq, hy_decay=hy_decay, hy_bias=hy_bias, s5_a_re=s5_a_re,
             s5_a_im=s5_a_im, s5_b_re=s5_b_re, s5_b_im=s5_b_im, s5_c_re=s5_c_re, s5_c_im=s5_c_im,
             s5_log_step=s5_log_step, s5_d=s5_d, s5_glu_w=s5_glu_w, s5_glu_b=s5_glu_b, ev_w_out=ev_w_out,
             od_w_in=od_w_in, od_gate_b=od_gate_b, ml_conv_w=ml_conv_w, ml_conv_b=ml_conv_b,
             ml_norm_g=ml_norm_g, od_w_out=od_w_out, pk_w_q=pk_w_q, pk_keys=pk_keys, pk_u=pk_u, pk_v=pk_v)
    nb, seq, d = x_prompt.shape
    db, dseq, _ = x_sample.shape
    depth = norm_g.shape[0]
    n_even, n_odd = (depth + 1) // 2, depth // 2
    assert db + 1 <= 8

    cond8 = jnp.zeros((8, d), F32).at[0].set(c_ctx.astype(F32)).at[1:1 + db].set(c.astype(F32))
    mods_ctx, mods_lat = [], []
    for l in range(depth):
        mod = ada_mod(cond8, ada_w[l].astype(F32), ada_b[l].astype(F32))
        chunks = [mod[:, j * d:(j + 1) * d] for j in range(6)]
        mods_ctx.append([ch[0:1].reshape(1, 1, d) for ch in chunks])
        mods_lat.append([ch[1:1 + db].reshape(db, 1, d) for ch in chunks])

    def s5_state(re, im, bsz):
        return [jnp.stack([re[:, i].reshape(bsz, 2, -1), im[:, i].reshape(bsz, 2, -1)], axis=2).astype(F32)
                for i in range(n_even)]

    ngrp, npst = s5_a_re.shape[2], s5_a_re.shape[3]
    nh, dh = state_mlstm_C.shape[3], state_mlstm_C.shape[4]
    zeros_s5 = jnp.zeros((nb, n_even, 2, ngrp, npst), F32)
    y_prompt, s5_fin, ml_fin = _trunk(
        x_prompt.reshape(nb * seq, d), mods_ctx, s5_state(zeros_s5, zeros_s5, nb),
        [jnp.zeros((nb, 2, nh, dh, dh), F32)] * n_odd, [jnp.zeros((nb, 2, nh, dh), F32)] * n_odd,
        [jnp.zeros((nb, 2, nh, 1), F32)] * n_odd, p, nb, seq, nb * seq)
    x_lat = add_pos(x_sample.reshape(db * dseq, d), _pos_embed(dseq, d, GRID_W), dseq, _tile(dseq, ROW_TILE))
    y_sample, _, _ = _trunk(
        x_lat, mods_lat, s5_state(state_s5_re, state_s5_im, db),
        [state_mlstm_C[:, i].astype(F32) for i in range(n_odd)], [state_mlstm_n[:, i].astype(F32) for i in range(n_odd)],
        [state_mlstm_m[:, i].astype(F32)[..., None] for i in range(n_odd)], p, db, dseq, dseq)

    new_s5_re = jnp.stack([h[:, :, 0].reshape(nb, 2, ngrp, npst) for h in s5_fin], axis=1)
    new_s5_im = jnp.stack([h[:, :, 1].reshape(nb, 2, ngrp, npst) for h in s5_fin], axis=1)
    new_c = jnp.stack([f[0] for f in ml_fin], axis=1)
    new_n = jnp.stack([f[1] for f in ml_fin], axis=1)
    new_m = jnp.stack([f[2][..., 0] for f in ml_fin], axis=1)
    return (y_prompt.reshape(nb, seq, d), y_sample.reshape(db, dseq, d), new_s5_re, new_s5_im, new_c, new_n, new_m)
```

```python
import functools
import math

import jax
import jax.numpy as jnp
from jax import lax
from jax.experimental import pallas as pl
from jax.experimental.pallas import tpu as pltpu

F32 = jnp.float32
BF16 = jnp.bfloat16
EPS = 1e-6
HIGHEST = lax.Precision.HIGHEST
V7X_VMEM_LIMIT_BYTES = 56 * 1024 * 1024
LANES = 128
SUBLANES = 8
ML_CHUNK = 128
PK_TOPK = 16
GRID_W = 64
NEG_INF = float("-inf")
ROW_TILE = 1024
DFT_TILE = 512
ADA_COLS = 1536
CONV_COLS = 256
S5_CHUNK = 256
PEER_SCORE_TOKENS = 256
PEER_TOKENS = 512
PEER_EXPERTS = 2048


def _cp(*sem):
    return pltpu.CompilerParams(dimension_semantics=sem, vmem_limit_bytes=V7X_VMEM_LIMIT_BYTES)


def _dot(a, b, **kw):
    return jnp.dot(a, b, preferred_element_type=F32, **kw)


def _dot_nt(a, b):
    return lax.dot_general(a, b, (((1,), (1,)), ((), ())), preferred_element_type=F32)


def _silu(x):
    return x * jax.nn.sigmoid(x)


def _ada_kernel(c_ref, w_ref, b_ref, o_ref):
    o_ref[...] = _dot(_silu(c_ref[...]), w_ref[...], precision=HIGHEST) + b_ref[...]


def ada_mod(cond8, w, b):
    d, no = w.shape
    tn = _tile(no, ADA_COLS)
    return pl.pallas_call(
        _ada_kernel, grid=(no // tn,),
        in_specs=[pl.BlockSpec((8, d), lambda j: (0, 0)), pl.BlockSpec((d, tn), lambda j: (0, j)),
                  pl.BlockSpec((1, tn), lambda j: (0, j))],
        out_specs=pl.BlockSpec((8, tn), lambda j: (0, j)),
        out_shape=jax.ShapeDtypeStruct((8, no), F32), compiler_params=_cp("parallel"), name="ada_mod",
    )(cond8, w, b.reshape(1, no))


def _normmod(x, g, sc, sh):
    y = x * lax.rsqrt(jnp.mean(x * x, axis=-1, keepdims=True) + EPS)
    return (y * g) * (1.0 + sc) + sh


def _mod_spec(d, tm, rows_per_mod):
    return pl.BlockSpec((1, 1, d), lambda i, j: ((i * tm) // rows_per_mod, 0, 0))


def _nm_matmul_kernel(x_ref, g_ref, sc_ref, sh_ref, w_ref, o_ref, h_ref):
    @pl.when(pl.program_id(1) == 0)
    def _():
        h_ref[...] = _normmod(x_ref[...], g_ref[...], sc_ref[0], sh_ref[0]).astype(BF16)
    o_ref[...] = _dot(h_ref[...], w_ref[...])


def normmod_matmul(x, g, sc, sh, w_bf16, rows_per_mod, tm, tn):
    n, d = x.shape
    no = w_bf16.shape[1]
    return pl.pallas_call(
        _nm_matmul_kernel, grid=(n // tm, no // tn),
        in_specs=[pl.BlockSpec((tm, d), lambda i, j: (i, 0)), pl.BlockSpec((1, d), lambda i, j: (0, 0)),
                  _mod_spec(d, tm, rows_per_mod), _mod_spec(d, tm, rows_per_mod),
                  pl.BlockSpec((d, tn), lambda i, j: (0, j))],
        out_specs=pl.BlockSpec((tm, tn), lambda i, j: (i, j)),
        out_shape=jax.ShapeDtypeStruct((n, no), F32),
        scratch_shapes=[pltpu.VMEM((tm, d), BF16)],
        compiler_params=_cp("parallel", "arbitrary"), name="normmod_matmul",
    )(x, g.reshape(1, d), sc, sh, w_bf16)


def _add_rows_kernel(x_ref, p_ref, o_ref):
    o_ref[...] = x_ref[...] + p_ref[...]


def add_pos(x, pe, seq_len, tm):
    n, d = x.shape
    nb = seq_len // tm
    return pl.pallas_call(
        _add_rows_kernel, grid=(n // tm,),
        in_specs=[pl.BlockSpec((tm, d), lambda i: (i, 0)), pl.BlockSpec((tm, d), lambda i: (i % nb, 0))],
        out_specs=pl.BlockSpec((tm, d), lambda i: (i, 0)),
        out_shape=jax.ShapeDtypeStruct((n, d), F32), compiler_params=_cp("parallel"), name="add_pos",
    )(x, pe)


def _sconv_kernel(x_ref, w_ref, b_ref, s_ref, o_ref, *, act):
    x = x_ref[...]
    n_tok = x.shape[0]
    row = lax.broadcasted_iota(jnp.int32, x.shape, 0)
    prev = jnp.where(row == 0, 0.0, pltpu.roll(x, 1, 0))
    nxt = jnp.where(row == n_tok - 1, 0.0, pltpu.roll(x, n_tok - 1, 0))
    y = prev * w_ref[0:1, :] + x * w_ref[1:2, :] + nxt * w_ref[2:3, :] + b_ref[...]
    if act:
        y = _silu(y) * s_ref[...]
    o_ref[...] = y.astype(o_ref.dtype)


def short_conv(a, ncols, w, b, scale, seq_len, act, out_dtype):
    n = a.shape[0]
    cb = _tile(ncols, CONV_COLS)
    return pl.pallas_call(
        functools.partial(_sconv_kernel, act=act), grid=(n // seq_len, ncols // cb),
        in_specs=[pl.BlockSpec((seq_len, cb), lambda s, j: (s, j)), pl.BlockSpec((3, cb), lambda s, j: (0, j)),
                  pl.BlockSpec((1, cb), lambda s, j: (0, j)), pl.BlockSpec((1, cb), lambda s, j: (0, j))],
        out_specs=pl.BlockSpec((seq_len, cb), lambda s, j: (s, j)),
        out_shape=jax.ShapeDtypeStruct((n, ncols), out_dtype), compiler_params=_cp("parallel", "parallel"),
        name="short_conv",
    )(a, w, b.reshape(1, ncols), scale.reshape(1, ncols))


def dft_tables(n_tok):
    k = jnp.arange(n_tok, dtype=jnp.int32)
    blk = 1 << ((n_tok.bit_length() - 1) // 2)
    def thin(n):
        ang = ((k[:, None] * n[None, :]) % (2 * n_tok)).astype(F32) * (math.pi / n_tok)
        return jnp.cos(ang), jnp.sin(ang)
    (c_hi, s_hi), (c_lo, s_lo) = thin(jnp.arange(0, n_tok, blk, dtype=jnp.int32)), thin(jnp.arange(blk, dtype=jnp.int32))
    cos_t = (c_hi[:, :, None] * c_lo[:, None, :] - s_hi[:, :, None] * s_lo[:, None, :]).reshape(n_tok, n_tok)
    msin = -(s_hi[:, :, None] * c_lo[:, None, :] + c_hi[:, :, None] * s_lo[:, None, :]).reshape(n_tok, n_tok)
    alt = jnp.where(k % 2 == 0, 1.0, -1.0).astype(F32)
    a_t = msin.at[0, :].set(alt)
    a_tt = msin.at[:, 0].set(alt)
    return cos_t.astype(BF16), a_t.astype(BF16), a_tt.astype(BF16)


def _hyfilt_kernel(band_ref, w1_ref, b1_ref, w2_ref, b2_ref, w3_ref, fr_ref, dec_ref, h_ref, ss_ref, *,
                   n_tok, tl, hw, nbands):
    i = pl.program_id(0)
    pos = i * tl + lax.broadcasted_iota(jnp.int32, (tl, 1), 0)
    t = pos.astype(F32) / n_tok
    lane = lax.broadcasted_iota(jnp.int32, (tl, LANES), 1)
    ang = 2.0 * math.pi * t * band_ref[...]
    z = jnp.where(lane == 0, t, jnp.where(lane <= nbands, jnp.cos(ang),
                                          jnp.where(lane <= 2 * nbands, jnp.sin(ang), 0.0)))
    fr = fr_ref[...]
    h = jnp.sin(fr * (_dot(z, w1_ref[...], precision=HIGHEST) + b1_ref[...]))
    h = jnp.sin(fr * (_dot(h, w2_ref[...], precision=HIGHEST) + b2_ref[...]))
    h = _dot(h, w3_ref[...], precision=HIGHEST) * jnp.exp(-t * jnp.abs(dec_ref[...]))
    col = lax.broadcasted_iota(jnp.int32, h.shape, 1)
    is_bwd = (col // hw) % 2 == 1
    h = jnp.where(jnp.logical_and(is_bwd, pos == 0), 0.0, h)
    h_ref[...] = h.astype(BF16)

    @pl.when(i == 0)
    def _():
        ss_ref[...] = jnp.zeros_like(ss_ref)
    ss_ref[...] += jnp.sum(h * h, axis=0, keepdims=True)


def hyena_filter_taps(n_tok, w1, b1, w2, b2, w3, freq, decay, hw):
    emb, ffn = w1.shape
    nbands = (emb - 1) // 2
    tl = _tile(n_tok, DFT_TILE)
    bands = jnp.linspace(1e-4, nbands - 1, nbands, dtype=F32)
    band_row = jnp.zeros((1, LANES), F32).at[0, 1:1 + nbands].set(bands).at[0, 1 + nbands:1 + 2 * nbands].set(bands)
    w1p = jnp.zeros((LANES, ffn), F32).at[:emb].set(w1)
    nc = w3.shape[1]
    full = lambda shp: pl.BlockSpec(shp, lambda i: (0, 0))
    return pl.pallas_call(
        functools.partial(_hyfilt_kernel, n_tok=n_tok, tl=tl, hw=hw, nbands=nbands), grid=(n_tok // tl,),
        in_specs=[full((1, LANES)), full((LANES, ffn)), full((1, ffn)), full((ffn, ffn)), full((1, ffn)),
                  full((ffn, nc)), full((1, ffn)), full((1, nc))],
        out_specs=[pl.BlockSpec((tl, nc), lambda i: (i, 0)), full((1, nc))],
        out_shape=[jax.ShapeDtypeStruct((n_tok, nc), BF16), jax.ShapeDtypeStruct((1, nc), F32)],
        compiler_params=_cp("arbitrary"), name="hyena_filter_taps",
    )(band_row, w1p, b1.reshape(1, ffn), w2, b2.reshape(1, ffn), w3, freq.reshape(1, ffn), decay.reshape(1, nc))


def _filt_dft_kernel(c_ref, a_ref, h_ref, ss_ref, kr_ref, ki_ref, *, tf, hw):
    i = pl.program_id(1)
    hf = h_ref[:, :hw]
    hb = h_ref[:, hw:]
    cc = c_ref[...]
    aa = a_ref[...]
    zrf, zif, zrb, zib = _dot(cc, hf), _dot(aa, hf), _dot(cc, hb), _dot(aa, hb)
    scale = lax.rsqrt(ss_ref[:, :hw] + ss_ref[:, hw:] + EPS)
    first = (i * tf + lax.broadcasted_iota(jnp.int32, (tf, 1), 0)) == 0
    scale = scale * jnp.where(first, 0.5, 1.0)
    kr_ref[0] = (zrf + zrb) * scale
    ki_ref[0] = jnp.where(first, zif + zib, zif - zib) * scale


def hyena_filter_spectrum(cos_t, a_t, taps, sumsq, hw, tf):
    n_tok = cos_t.shape[0]
    norder = taps.shape[1] // (2 * hw)
    out = jax.ShapeDtypeStruct((norder, n_tok, hw), F32)
    return pl.pallas_call(
        functools.partial(_filt_dft_kernel, tf=tf, hw=hw), grid=(norder, n_tok // tf),
        in_specs=[pl.BlockSpec((tf, n_tok), lambda o, i: (i, 0)), pl.BlockSpec((tf, n_tok), lambda o, i: (i, 0)),
                  pl.BlockSpec((n_tok, 2 * hw), lambda o, i: (0, o)), pl.BlockSpec((1, 2 * hw), lambda o, i: (0, o))],
        out_specs=[pl.BlockSpec((1, tf, hw), lambda o, i: (o, i, 0))] * 2,
        out_shape=[out, out], compiler_params=_cp("parallel", "parallel"), name="hyena_filter_spectrum",
    )(cos_t, a_t, taps, sumsq)


def _hy_fwd_kernel(c_ref, a_ref, z_ref, kr_ref, ki_ref, yr_ref, yi_ref, *, tf):
    i = pl.program_id(0)
    zb = z_ref[...].astype(BF16)
    zr = _dot(c_ref[...], zb)
    zi = _dot(a_ref[...], zb)
    kr = kr_ref[0]
    ki = ki_ref[0]
    first = (i * tf + lax.broadcasted_iota(jnp.int32, (tf, 1), 0)) == 0
    yr_ref[...] = jnp.where(first, zr * kr, zr * kr - zi * ki).astype(BF16)
    yi_ref[...] = jnp.where(first, zi * ki, zr * ki + zi * kr).astype(BF16)


def hyena_fwd(cos_t, a_t, z, zcol, kr, ki, order, nseq, hw, tf):
    n_tok = cos_t.shape[0]
    nf = n_tok // tf
    out = jax.ShapeDtypeStruct((nseq * n_tok, hw), BF16)
    return pl.pallas_call(
        functools.partial(_hy_fwd_kernel, tf=tf), grid=(nf, nseq),
        in_specs=[pl.BlockSpec((tf, n_tok), lambda i, b: (i, 0)), pl.BlockSpec((tf, n_tok), lambda i, b: (i, 0)),
                  pl.BlockSpec((n_tok, hw), lambda i, b: (b, zcol)),
                  pl.BlockSpec((1, tf, hw), lambda i, b: (order, i, 0)),
                  pl.BlockSpec((1, tf, hw), lambda i, b: (order, i, 0))],
        out_specs=[pl.BlockSpec((tf, hw), lambda i, b: (b * nf + i, 0))] * 2,
        out_shape=[out, out], compiler_params=_cp("parallel", "parallel"), name="hyena_fwd",
    )(cos_t, a_t, z, kr, ki)


def _hy_inv_kernel(c_ref, at_ref, yr_ref, yi_ref, zp_ref, gate_ref, bias_ref, o_ref, *, inv_len):
    conv = (_dot(c_ref[...], yr_ref[...]) + _dot(at_ref[...], yi_ref[...])) * inv_len
    o_ref[...] = gate_ref[...] * (conv + bias_ref[...] * zp_ref[...])


def hyena_inv(cos_t, a_tt, yr, yi, zprev, zcol, gates, gcol, bias_row, nseq, hw, tf):
    n_tok = cos_t.shape[0]
    nf = n_tok // tf
    return pl.pallas_call(
        functools.partial(_hy_inv_kernel, inv_len=1.0 / n_tok), grid=(nf, nseq),
        in_specs=[pl.BlockSpec((tf, n_tok), lambda i, b: (i, 0)), pl.BlockSpec((tf, n_tok), lambda i, b: (i, 0)),
                  pl.BlockSpec((n_tok, hw), lambda i, b: (b, 0)), pl.BlockSpec((n_tok, hw), lambda i, b: (b, 0)),
                  pl.BlockSpec((tf, hw), lambda i, b: (b * nf + i, zcol)),
                  pl.BlockSpec((tf, hw), lambda i, b: (b * nf + i, gcol)),
                  pl.BlockSpec((1, hw), lambda i, b: (0, 0))],
        out_specs=pl.BlockSpec((tf, hw), lambda i, b: (b * nf + i, 0)),
        out_shape=jax.ShapeDtypeStruct((nseq * n_tok, hw), F32),
        compiler_params=_cp("parallel", "parallel"), name="hyena_inv",
    )(cos_t, a_tt, yr, yi, zprev, gates, bias_row)


S5_DIAG_BLOCKS = 2


S5_SEQS_PER_STEP = 4


def _s5_kernel(u_ref, bre_ref, bim_ref, cre_ref, cim_ref, lam_ref, h0_ref, y_ref, hfin_ref, hre_s, him_s, st_s, *,
               tc, nc, ns, nb):
    d = pl.program_id(0)
    c = pl.program_id(2)

    @pl.when(c == 0)
    def _():
        st_s[...] = h0_ref[:, 0]

    sw = u_ref.shape[2]
    halves = [(slice(j * sw // S5_DIAG_BLOCKS, (j + 1) * sw // S5_DIAG_BLOCKS),
               slice(j * ns // S5_DIAG_BLOCKS, (j + 1) * ns // S5_DIAG_BLOCKS)) for j in range(S5_DIAG_BLOCKS)]
    for j in range(nb):
        ub = u_ref[j].astype(BF16)
        for us, hs in halves:
            hre_s[j, :, hs] = _dot(ub[:, us], bre_ref[0, us, hs])
            him_s[j, :, hs] = _dot(ub[:, us], bim_ref[0, us, hs])
    lr = lam_ref[0, 0:1, :]
    li = lam_ref[0, 1:2, :]

    def body(t, carry):
        r = jnp.where(d == 0, t, tc - 1 - t)
        new = []
        for j, (hr, hi) in enumerate(carry):
            nr = lr * hr - li * hi + hre_s[j, pl.ds(r, 1), :]
            ni = lr * hi + li * hr + him_s[j, pl.ds(r, 1), :]
            hre_s[j, pl.ds(r, 1), :] = nr
            him_s[j, pl.ds(r, 1), :] = ni
            new.append((nr, ni))
        return tuple(new)

    start = tuple((st_s[j, 0:1, :], st_s[j, 1:2, :]) for j in range(nb))
    for j, (hr, hi) in enumerate(lax.fori_loop(0, tc, body, start, unroll=4)):
        st_s[j, 0:1, :] = hr
        st_s[j, 1:2, :] = hi
    for j in range(nb):
        for us, hs in halves:
            y_ref[0, j, :, us] = (_dot(hre_s[j, :, hs].astype(BF16), cre_ref[0, hs, us])
                                  + _dot(him_s[j, :, hs].astype(BF16), cim_ref[0, hs, us]))

    @pl.when(c == nc - 1)
    def _():
        hfin_ref[:, 0] = st_s[...]


def s5_scan(proj, ucol, bre, bim, cre, cim, lam, h0, nseq, seq_len, sw, tc):
    ns = bre.shape[2]
    nc = seq_len // tc
    nb = S5_SEQS_PER_STEP if nseq % S5_SEQS_PER_STEP == 0 else 1

    def chunk(d, c):
        return c + d * (nc - 1 - 2 * c)

    y, hfin = pl.pallas_call(
        functools.partial(_s5_kernel, tc=tc, nc=nc, ns=ns, nb=nb), grid=(2, nseq // nb, nc),
        in_specs=[pl.BlockSpec((nb, tc, sw), lambda d, b, c: (b, chunk(d, c), ucol)),
                  pl.BlockSpec((1, sw, ns), lambda d, b, c: (d, 0, 0)),
                  pl.BlockSpec((1, sw, ns), lambda d, b, c: (d, 0, 0)),
                  pl.BlockSpec((1, ns, sw), lambda d, b, c: (d, 0, 0)),
                  pl.BlockSpec((1, ns, sw), lambda d, b, c: (d, 0, 0)),
                  pl.BlockSpec((1, 2, ns), lambda d, b, c: (d, 0, 0)),
                  pl.BlockSpec((nb, 1, 2, ns), lambda d, b, c: (b, d, 0, 0))],
        out_specs=[pl.BlockSpec((1, nb, tc, sw), lambda d, b, c: (d, b, chunk(d, c), 0)),
                   pl.BlockSpec((nb, 1, 2, ns), lambda d, b, c: (b, d, 0, 0))],
        out_shape=[jax.ShapeDtypeStruct((2, nseq, seq_len, sw), F32), jax.ShapeDtypeStruct((nseq, 2, 2, ns), F32)],
        scratch_shapes=[pltpu.VMEM((nb, tc, ns), F32), pltpu.VMEM((nb, tc, ns), F32), pltpu.VMEM((nb, 2, ns), F32)],
        compiler_params=_cp("parallel", "parallel", "arbitrary"), name="s5_scan",
    )(proj.reshape(nseq, seq_len, proj.shape[1]), bre, bim, cre, cim, lam, h0)
    return y.reshape(2, nseq * seq_len, sw), hfin


def _s5_glu_kernel(yf_ref, yb_ref, u_ref, d_ref, w_ref, b_ref, o_ref):
    y = jax.nn.gelu(yf_ref[0] + yb_ref[0] + d_ref[...] * u_ref[...])
    o_ref[...] = y * jax.nn.sigmoid(_dot(y.astype(BF16), w_ref[...]) + b_ref[...])


def s5_glu(y2, proj, ucol, d_skip, glu_w_bf16, glu_b, tm):
    _, n, sw = y2.shape
    return pl.pallas_call(
        _s5_glu_kernel, grid=(n // tm,),
        in_specs=[pl.BlockSpec((1, tm, sw), lambda i: (0, i, 0)), pl.BlockSpec((1, tm, sw), lambda i: (1, i, 0)),
                  pl.BlockSpec((tm, sw), lambda i: (i, ucol)), pl.BlockSpec((1, sw), lambda i: (0, 0)),
                  pl.BlockSpec((sw, sw), lambda i: (0, 0)), pl.BlockSpec((1, sw), lambda i: (0, 0))],
        out_specs=pl.BlockSpec((tm, sw), lambda i: (i, 0)),
        out_shape=jax.ShapeDtypeStruct((n, sw), F32), compiler_params=_cp("parallel"), name="s5_glu",
    )(y2, y2, proj, d_skip.reshape(1, sw), glu_w_bf16, glu_b.reshape(1, sw))


def _residual_and_next_norm(x_ref, gate_ref, y, g2_ref, sc2_ref, sh2_ref, o_ref, hn_ref):
    xn = x_ref[...] + gate_ref[0] * y
    o_ref[...] = xn
    hn_ref[...] = _normmod(xn, g2_ref[...], sc2_ref[0], sh2_ref[0]).astype(BF16)


def _row_specs(d, tm, rows_per_mod):
    mod = pl.BlockSpec((1, 1, d), lambda i: ((i * tm) // rows_per_mod, 0, 0))
    return [pl.BlockSpec((tm, d), lambda i: (i, 0)), mod, pl.BlockSpec((1, d), lambda i: (0, 0)), mod, mod]


def _row_outs(n, d, tm):
    spec = pl.BlockSpec((tm, d), lambda i: (i, 0))
    return [spec, spec], [jax.ShapeDtypeStruct((n, d), F32), jax.ShapeDtypeStruct((n, d), BF16)]


def _even_out_kernel(a_ref, b_ref, wa_ref, wb_ref, x_ref, gate_ref, g2_ref, sc2_ref, sh2_ref, o_ref, hn_ref):
    y = _dot(a_ref[...].astype(BF16), wa_ref[...]) + _dot(b_ref[...].astype(BF16), wb_ref[...])
    _residual_and_next_norm(x_ref, gate_ref, y, g2_ref, sc2_ref, sh2_ref, o_ref, hn_ref)


def even_out(hy, s5o, w_bf16, x, gate, g2, sc2, sh2, rows_per_mod, tm):
    n, d = x.shape
    hw = hy.shape[1]
    sw = s5o.shape[1]
    out_specs, out_shape = _row_outs(n, d, tm)
    return pl.pallas_call(
        _even_out_kernel, grid=(n // tm,),
        in_specs=[pl.BlockSpec((tm, hw), lambda i: (i, 0)), pl.BlockSpec((tm, sw), lambda i: (i, 0)),
                  pl.BlockSpec((hw, d), lambda i: (0, 0)), pl.BlockSpec((sw, d), lambda i: (hw // sw, 0))]
        + _row_specs(d, tm, rows_per_mod),
        out_specs=out_specs, out_shape=out_shape, compiler_params=_cp("parallel"), name="even_out",
    )(hy, s5o, w_bf16, w_bf16, x, gate, g2.reshape(1, d), sc2, sh2)


ML_HEAD_GROUP = 8


def _log_sigmoid(x):
    return jnp.minimum(x, 0.0) - jnp.log1p(jnp.exp(-jnp.abs(x)))


def _mlstm_kernel(q_ref, k_ref, v_ref, g_ref, gb_ref, c0_ref, n0_ref, m0_ref, h_ref, cf_ref, nf_ref, mf_ref,
                  c_s, m_s, *, nh, dh, tc, nc):
    d = pl.program_id(0)
    c = pl.program_id(2)

    @pl.when(c == 0)
    def _():
        for h in range(nh):
            c_s[h, :, :dh] = c0_ref[0, 0, h]
            c_s[h, :, dh:] = jnp.broadcast_to(n0_ref[0, 0, h:h + 1, :], (dh, dh)).T
        m_s[...] = m0_ref[0, 0]

    ones = jnp.ones((tc, dh), F32)
    gates = g_ref[...] + gb_ref[0]
    lane = lax.broadcasted_iota(jnp.int32, gates.shape, 1)
    logf = jnp.where(jnp.logical_and(lane >= nh, lane < 2 * nh), _log_sigmoid(gates), 0.0)
    r_i = lax.broadcasted_iota(jnp.int32, (tc, tc), 0)
    s_i = lax.broadcasted_iota(jnp.int32, (tc, tc), 1)
    causal = (r_i - s_i) * (1 - 2 * d) >= 0
    bcum = _dot(causal.astype(F32), logf, precision=HIGHEST)
    btot = jnp.sum(logf, axis=0, keepdims=True)
    gates_t = gates.T
    bcum_t = bcum.T
    for g0 in range(0, nh, ML_HEAD_GROUP):
        hds = list(range(g0, min(g0 + ML_HEAD_GROUP, nh)))
        hsl = {h: slice(h * dh, (h + 1) * dh) for h in hds}
        b_col = {h: bcum[:, nh + h:nh + h + 1] for h in hds}
        m_old = {h: m_s[h:h + 1, :] for h in hds}
        a = {h: b_col[h] + m_old[h] for h in hds}
        src = {h: jnp.where(causal, gates_t[h:h + 1, :] - bcum_t[nh + h:nh + h + 1, :], NEG_INF) for h in hds}
        mq = {h: jnp.maximum(a[h], b_col[h] + jnp.max(src[h], axis=-1, keepdims=True)) for h in hds}
        rel = {h: jnp.broadcast_to(b_col[h] - mq[h], (tc, tc)) for h in hds}
        s = {h: _dot_nt(q_ref[:, hsl[h]], k_ref[:, hsl[h]]) * jnp.exp(src[h] + rel[h]) for h in hds}
        v1 = {h: jnp.concatenate([v_ref[:, hsl[h]], ones], axis=1) for h in hds}
        cn = {h: c_s[h] for h in hds}
        qw = {h: jnp.exp(rel[h][:, :dh] + m_old[h]) * q_ref[:, hsl[h]] for h in hds}
        both = {h: _dot(s[h], v1[h]) + _dot(qw[h], cn[h]) for h in hds}
        for h in hds:
            h_ref[0, :, hsl[h]] = both[h][:, :dh] / jnp.maximum(jnp.abs(both[h][:, dh:]), jnp.exp(-mq[h]))
        b_last = {h: btot[:, nh + h:nh + h + 1] for h in hds}
        g = {h: b_last[h] - bcum_t[nh + h:nh + h + 1, :] + gates_t[h:h + 1, :] for h in hds}
        m_new = {h: jnp.maximum(b_last[h] + m_old[h], jnp.max(g[h], axis=1, keepdims=True)) for h in hds}
        kw_t = {h: k_ref[:, hsl[h]].astype(F32).T * jnp.exp(g[h] - m_new[h]) for h in hds}
        for h in hds:
            c_s[h] = jnp.exp(b_last[h] + m_old[h] - m_new[h]) * cn[h] + _dot(kw_t[h], v1[h])
            m_s[h:h + 1, :] = m_new[h]

    @pl.when(c == nc - 1)
    def _():
        for h in range(nh):
            cf_ref[0, 0, h] = c_s[h, :, :dh]
            nf_ref[0, 0, h:h + 1, :] = c_s[h, :, dh:].T[0:1, :]
        mf_ref[0, 0] = m_s[...]


def mlstm_scan(qk, proj, vcol, gate_bias, c0, n0, m0, nseq, seq_len, nh, dh):
    tc = ML_CHUNK
    nc = seq_len // tc
    w = nh * dh

    def chunk(d, c):
        return c + d * (nc - 1 - 2 * c)

    rowblk = lambda d, b, c: b * nc + chunk(d, c)
    st = lambda shp: pl.BlockSpec((1, 1) + shp, lambda d, b, c: (b, d) + (0,) * len(shp))
    return pl.pallas_call(
        functools.partial(_mlstm_kernel, nh=nh, dh=dh, tc=tc, nc=nc), grid=(2, nseq, nc),
        in_specs=[pl.BlockSpec((tc, w), lambda d, b, c: (rowblk(d, b, c), 0)),
                  pl.BlockSpec((tc, w), lambda d, b, c: (rowblk(d, b, c), 1)),
                  pl.BlockSpec((tc, w), lambda d, b, c: (rowblk(d, b, c), vcol)),
                  pl.BlockSpec((tc, LANES), lambda d, b, c: (rowblk(d, b, c), 4 * w // LANES + d)),
                  pl.BlockSpec((1, 1, LANES), lambda d, b, c: (d, 0, 0)),
                  st((nh, dh, dh)), st((nh, dh)), st((nh, 1))],
        out_specs=[pl.BlockSpec((1, tc, w), lambda d, b, c: (d, rowblk(d, b, c), 0)),
                   st((nh, dh, dh)), st((nh, dh)), st((nh, 1))],
        out_shape=[jax.ShapeDtypeStruct((2, nseq * seq_len, w), F32),
                   jax.ShapeDtypeStruct((nseq, 2, nh, dh, dh), F32), jax.ShapeDtypeStruct((nseq, 2, nh, dh), F32),
                   jax.ShapeDtypeStruct((nseq, 2, nh, 1), F32)],
        scratch_shapes=[pltpu.VMEM((nh, dh, 2 * dh), F32), pltpu.VMEM((nh, 1), F32)],
        compiler_params=_cp("parallel", "parallel", "arbitrary"), name="mlstm_scan",
    )(qk, qk, proj, proj, gate_bias, c0, n0, m0)


def _odd_out_kernel(hf_ref, hb_ref, og_ref, ng_ref, w_ref, x_ref, gate_ref, g2_ref, sc2_ref, sh2_ref, o_ref, hn_ref,
                    a_s, *, nh, dh):
    for h in range(nh):
        hs = slice(h * dh, (h + 1) * dh)
        blk = hf_ref[0, :, hs] + hb_ref[0, :, hs]
        blk = blk * lax.rsqrt(jnp.mean(blk * blk, axis=-1, keepdims=True) + EPS)
        a_s[:, hs] = ((blk * ng_ref[:, hs]) * _silu(og_ref[:, hs])).astype(BF16)
    _residual_and_next_norm(x_ref, gate_ref, _dot(a_s[...], w_ref[...]), g2_ref, sc2_ref, sh2_ref, o_ref, hn_ref)


def odd_out(h2, proj, ocol, norm_g, w_bf16, x, gate, g2, sc2, sh2, rows_per_mod, nh, dh, tm):
    n, d = x.shape
    w = nh * dh
    out_specs, out_shape = _row_outs(n, d, tm)
    return pl.pallas_call(
        functools.partial(_odd_out_kernel, nh=nh, dh=dh), grid=(n // tm,),
        in_specs=[pl.BlockSpec((1, tm, w), lambda i: (0, i, 0)), pl.BlockSpec((1, tm, w), lambda i: (1, i, 0)),
                  pl.BlockSpec((tm, w), lambda i: (i, ocol)), pl.BlockSpec((1, w), lambda i: (0, 0)),
                  pl.BlockSpec((w, d), lambda i: (0, 0))] + _row_specs(d, tm, rows_per_mod),
        out_specs=out_specs, out_shape=out_shape, scratch_shapes=[pltpu.VMEM((tm, w), BF16)],
        compiler_params=_cp("parallel"), name="odd_out",
    )(h2, h2, proj, norm_g.reshape(1, w), w_bf16, x, gate, g2.reshape(1, d), sc2, sh2)


def _sort_network(n):
    pairs, p = [], 1
    while p < n:
        k = p
        while k >= 1:
            for j in range(k % p, n - k, 2 * k):
                for i in range(min(k, n - j - k)):
                    if (i + j) // (2 * p) == (i + j + k) // (2 * p):
                        pairs.append((i + j, i + j + k))
            k //= 2
        p *= 2
    return pairs


def _top_values_tiled(arrays, k, outs):
    cols = []
    for arr in arrays:
        tiles = [arr[r:r + SUBLANES, :] for r in range(0, arr.shape[0], SUBLANES)]
        for lo, hi in _sort_network(len(tiles)):
            tiles[lo], tiles[hi] = jnp.maximum(tiles[lo], tiles[hi]), jnp.minimum(tiles[lo], tiles[hi])
        cols.append(tiles)
    for j in range(k):
        for tiles, out_s in zip(cols, outs):
            m = jnp.max(tiles[0], axis=0, keepdims=True)
            out_s[j:j + 1, :] = m
            hit = tiles[0] == m
            for i in range(min(len(tiles), k - 1 - j)):
                below = tiles[i + 1] if i + 1 < len(tiles) else NEG_INF
                tiles[i] = jnp.where(hit, below, tiles[i])


def _pair_candidates(k):
    return [(a, k // (a + 1)) for a in range(k)]


PEER_HEADS_PER_TRIP = 8


def _peer_score_kernel(h_ref, wq_ref, keys_ref, th_ref, s2_ref, w1_ref, w2_ref, q_s, v_s, cand_s, best_s, *,
                       nh, half, topk):
    q_s[...] = _dot_nt(wq_ref[...], h_ref[...])
    kk = topk + 1
    cand_s[...] = jnp.full(cand_s.shape, NEG_INF, F32)
    group = range(PEER_HEADS_PER_TRIP)

    def heads(trip, carry):
        hds = [trip * PEER_HEADS_PER_TRIP + u for u in group]
        scores = []
        for hd in hds:
            base = pl.multiple_of(hd * 2 * half, 2 * half)
            scores.append(_dot(keys_ref[hd, 0], q_s[pl.ds(base, half), :]))
            scores.append(_dot(keys_ref[hd, 1], q_s[pl.ds(base + half, half), :]))
        _top_values_tiled(scores, kk, [v_s.at[u, c] for u in group for c in range(2)])
        for u in group:
            off = 0
            for a, cnt in _pair_candidates(kk):
                cand_s[u, off:off + cnt, :] = v_s[u, 0, a:a + 1, :] + v_s[u, 1, 0:cnt, :]
                off += cnt
        _top_values_tiled([cand_s[u] for u in group], kk, [best_s.at[u] for u in group])
        for u, hd in enumerate(hds):
            s1, s2 = scores[2 * u], scores[2 * u + 1]
            best = best_s[u, 0:topk, :]
            z = jnp.sum(jnp.exp(best - best[0:1, :]), axis=0, keepdims=True)
            tmid = 0.5 * (best_s[u, topk - 1:topk, :] + best_s[u, topk:topk + 1, :])
            th = tmid - s1
            w1 = jnp.exp(s1 - v_s[u, 0, 0:1, :]) / z
            w2 = jnp.exp(s2 - v_s[u, 1, 0:1, :])
            for lt in range(s1.shape[1] // LANES):
                sl = slice(lt * LANES, (lt + 1) * LANES)
                th_ref[hd, lt] = th[:, sl]
                s2_ref[hd, lt] = s2[:, sl]
                w1_ref[hd, lt] = w1[:, sl]
                w2_ref[hd, lt] = w2[:, sl]
        return carry

    lax.fori_loop(0, nh // PEER_HEADS_PER_TRIP, heads, 0)


def peer_scores(h_bf16, wq_t_bf16, keys, tt):
    n, d = h_bf16.shape
    nh, _, nk, half = keys.shape
    kk = PK_TOPK + 1
    top_rows = -(-kk // SUBLANES) * SUBLANES
    ncand = SUBLANES
    while ncand < sum(c for _, c in _pair_candidates(kk)):
        ncand *= 2
    big = jax.ShapeDtypeStruct((nh, n // LANES, nk, LANES), F32)
    bspec = pl.BlockSpec((nh, tt // LANES, nk, LANES), lambda i: (0, i, 0, 0))
    return pl.pallas_call(
        functools.partial(_peer_score_kernel, nh=nh, half=half, topk=PK_TOPK), grid=(n // tt,),
        in_specs=[pl.BlockSpec((tt, d), lambda i: (i, 0)), pl.BlockSpec((nh * 2 * half, d), lambda i: (0, 0)),
                  pl.BlockSpec((nh, 2, nk, half), lambda i: (0, 0, 0, 0))],
        out_specs=[bspec, bspec, bspec, bspec],
        out_shape=[big, big, big, big],
        scratch_shapes=[pltpu.VMEM((nh * 2 * half, tt), F32), pltpu.VMEM((PEER_HEADS_PER_TRIP, 2, top_rows, tt), F32),
                        pltpu.VMEM((PEER_HEADS_PER_TRIP, ncand, tt), F32),
                        pltpu.VMEM((PEER_HEADS_PER_TRIP, top_rows, tt), F32)],
        compiler_params=_cp("parallel"), name="peer_scores",
    )(h_bf16, wq_t_bf16, keys)


PEER_KEY_ROWS = 16
GELU_C1 = math.sqrt(2.0 / math.pi)
GELU_C2 = 0.044715 * GELU_C1


def _gelu_tanh(x):
    half_x = 0.5 * x
    return half_x + half_x * jnp.tanh(x * (GELU_C1 + GELU_C2 * (x * x)))


def _peer_dense_kernel(h_ref, u_ref, vt_ref, th_ref, s2_ref, w1_ref, w2_ref, x_ref, gate_ref, fg_ref, o_ref,
                       acc_s, st_s, wt_s, *, nh, nk, ec, tt, final):
    e = pl.program_id(1)
    nlt = tt // LANES
    n_i1 = ec // nk
    nkt = nk // PEER_KEY_ROWS
    nsub = PEER_KEY_ROWS // SUBLANES

    @pl.when(e == 0)
    def _():
        acc_s[...] = jnp.zeros_like(acc_s)

    st = _gelu_tanh(_dot_nt(u_ref[...], h_ref[...]))
    for lt in range(nlt):
        st_s[lt] = st[:, lt * LANES:(lt + 1) * LANES]

    def tile(idx, carry):
        lt = idx // nkt
        k0 = (idx % nkt) * PEER_KEY_ROWS
        subs = [pl.ds(pl.multiple_of(k0 + j * SUBLANES, SUBLANES), SUBLANES) for j in range(nsub)]
        g = [[jnp.zeros((SUBLANES, LANES), F32) for _ in subs] for _ in range(n_i1)]
        for hd in range(nh):
            s2t = [s2_ref[hd, lt, sub, :] for sub in subs]
            w2t = [w2_ref[hd, lt, sub, :] for sub in subs]
            for li in range(n_i1):
                thb = jnp.broadcast_to(th_ref[hd, lt, li:li + 1, :], (SUBLANES, LANES))
                w1b = jnp.broadcast_to(w1_ref[hd, lt, li:li + 1, :], (SUBLANES, LANES))
                for j in range(nsub):
                    g[li][j] = g[li][j] + jnp.where(s2t[j] >= thb, w2t[j] * w1b, 0.0)
        for li in range(n_i1):
            rows = pl.ds(pl.multiple_of(li * nk + k0, PEER_KEY_ROWS), PEER_KEY_ROWS)
            wt_s[lt, rows, :] = (st_s[lt, rows, :] * jnp.concatenate(g[li], axis=0)).astype(BF16)
        return carry

    lax.fori_loop(0, nlt * nkt, tile, 0)
    wt = jnp.concatenate([wt_s[lt] for lt in range(nlt)], axis=1)
    acc_s[...] += _dot(vt_ref[0], wt)

    @pl.when(e == pl.num_programs(1) - 1)
    def _():
        xn = x_ref[...] + gate_ref[0] * acc_s[...].T
        if final:
            xn = (xn * lax.rsqrt(jnp.mean(xn * xn, axis=-1, keepdims=True) + EPS)) * fg_ref[...]
        o_ref[...] = xn


def peer_dense(h_bf16, u_bf16, vt_bf16, th, s2, w1, w2, x, gate, final_g, final, rows_per_mod, tt, ec):
    n, d = x.shape
    nh, _, nk, _ = s2.shape
    nchunk = u_bf16.shape[0] // ec
    bspec = pl.BlockSpec((nh, tt // LANES, nk, LANES), lambda i, e: (0, i, 0, 0))
    rspec = pl.BlockSpec((nh, tt // LANES, ec // nk, LANES), lambda i, e: (0, i, e, 0))
    tile_buf = (tt // LANES, ec, LANES)
    return pl.pallas_call(
        functools.partial(_peer_dense_kernel, nh=nh, nk=nk, ec=ec, tt=tt, final=final), grid=(n // tt, nchunk),
        in_specs=[pl.BlockSpec((tt, d), lambda i, e: (i, 0)), pl.BlockSpec((ec, d), lambda i, e: (e, 0)),
                  pl.BlockSpec((1, d, ec), lambda i, e: (e, 0, 0)), rspec, bspec, rspec, bspec,
                  pl.BlockSpec((tt, d), lambda i, e: (i, 0)),
                  pl.BlockSpec((1, 1, d), lambda i, e: ((i * tt) // rows_per_mod, 0, 0)),
                  pl.BlockSpec((1, d), lambda i, e: (0, 0))],
        out_specs=pl.BlockSpec((tt, d), lambda i, e: (i, 0)),
        out_shape=jax.ShapeDtypeStruct((n, d), F32),
        scratch_shapes=[pltpu.VMEM((d, tt), F32), pltpu.VMEM(tile_buf, F32), pltpu.VMEM(tile_buf, BF16)],
        compiler_params=_cp("parallel", "arbitrary"), name="peer_dense",
    )(h_bf16, u_bf16, vt_bf16, th, s2, w1, w2, x, gate, final_g.reshape(1, d))


def _s5_params(a_re, a_im, b_re, b_im, c_re, c_im, log_step):
    lam = lax.complex(a_re.astype(F32), a_im.astype(F32))
    lam_bar = jnp.exp(lam * jnp.exp(log_step.astype(F32))[..., None])
    b_bar = ((lam_bar - 1.0) / lam)[..., None] * lax.complex(b_re.astype(F32), b_im.astype(F32))
    ngrp, npst, nch = b_bar.shape[1:]
    eye = jnp.eye(ngrp, dtype=F32)

    def b_mat(part):
        return jnp.einsum("dgpj,gh->dgjhp", part, eye).reshape(2, ngrp * nch, ngrp * npst)

    def c_mat(part):
        return jnp.einsum("dgjp,gh->dgphj", part, eye).reshape(2, ngrp * npst, ngrp * nch)

    bre, bim = b_mat(b_bar.real).astype(BF16), b_mat(b_bar.imag).astype(BF16)
    cre, cim = c_mat(c_re.astype(F32)).astype(BF16), c_mat(-c_im.astype(F32)).astype(BF16)
    lam2 = jnp.stack([lam_bar.real.reshape(2, -1), lam_bar.imag.reshape(2, -1)], axis=1)
    return bre, bim, cre, cim, lam2


def _pos_embed(n_tok, d, grid_w):
    rows = n_tok // grid_w
    quarter = d // 4
    omega = 1.0 / (10000.0 ** (jnp.arange(quarter, dtype=F32) / quarter))

    def emb1d(pos):
        ang = pos.astype(F32)[:, None] * omega[None]
        return jnp.concatenate([jnp.sin(ang), jnp.cos(ang)], axis=-1)

    er = emb1d(jnp.arange(rows))
    ec = emb1d(jnp.arange(grid_w))
    half = d // 2
    pe = jnp.concatenate([jnp.broadcast_to(er[:, None], (rows, grid_w, half)),
                          jnp.broadcast_to(ec[None], (rows, grid_w, half))], axis=-1)
    return pe.reshape(rows * grid_w, d)


def _tile(n, pref):
    return pref if n % pref == 0 else n


def _trunk(x, mods, s5_h0, ml_c0, ml_n0, ml_m0, p, nseq, seq_len, rows_per_mod):
    n, d = x.shape
    tm = _tile(min(rows_per_mod, n), ROW_TILE)
    depth = p["norm_g"].shape[0]
    s5_fin, ml_fin = [], []
    for l in range(depth):
        sh1, sc1, g1, sh2, sc2, g2 = mods[l]
        i = l // 2
        if l % 2 == 0:
            hw = p["hy_bias"].shape[2]
            sw = p["s5_d"].shape[1]
            proj = normmod_matmul(x, p["norm_g"][l, 0], sc1, sh1, p["ev_w_in"][i].astype(BF16), rows_per_mod, tm,
                                  3 * hw + sw)
            hy_in = short_conv(proj, 3 * hw, p["hy_conv_w"][i], p["hy_conv_b"][i], jnp.ones((3 * hw,), F32),
                               seq_len, act=False, out_dtype=F32)
            cos_t, a_t, a_tt = dft_tables(seq_len)
            tf = _tile(seq_len, DFT_TILE)
            taps, sumsq = hyena_filter_taps(seq_len, p["hy_w1"][i], p["hy_b1"][i], p["hy_w2"][i], p["hy_b2"][i],
                                            p["hy_w3"][i], p["hy_freq"][i], p["hy_decay"][i], hw)
            kr, ki = hyena_filter_spectrum(cos_t, a_t, taps, sumsq, hw, tf)
            bias = p["hy_bias"][i].astype(F32)
            z, zcol = hy_in, 0
            for o in range(bias.shape[0]):
                yr, yi = hyena_fwd(cos_t, a_t, z, zcol, kr, ki, o, nseq, hw, tf)
                z = hyena_inv(cos_t, a_tt, yr, yi, z, zcol, hy_in, 1 + o, bias[o:o + 1], nseq, hw, tf)
                zcol = 0
            bre, bim, cre, cim, lam2 = _s5_params(p["s5_a_re"][i], p["s5_a_im"][i], p["s5_b_re"][i], p["s5_b_im"][i],
                                                  p["s5_c_re"][i], p["s5_c_im"][i], p["s5_log_step"][i])
            ucol = 3 * hw // sw
            y2, hfin = s5_scan(proj, ucol, bre, bim, cre, cim, lam2, s5_h0[i], nseq, seq_len, sw,
                               _tile(seq_len, S5_CHUNK))
            s5_fin.append(hfin)
            s5o = s5_glu(y2, proj, ucol, p["s5_d"][i], p["s5_glu_w"][i].astype(BF16), p["s5_glu_b"][i], tm)
            x, hn = even_out(z, s5o, p["ev_w_out"][i].astype(BF16), x, g1, p["norm_g"][l, 1], sc2, sh2, rows_per_mod, tm)
        else:
            nh = p["od_gate_b"].shape[2]
            w = p["ml_norm_g"].shape[1]
            dh = w // nh
            w_in = p["od_w_in"][i]
            wg = w_in[:, 4 * w:].reshape(d, 4, nh)
            gb = p["od_gate_b"][i].astype(F32)
            wg2 = jnp.zeros((d, 2, LANES), w_in.dtype)
            bg2 = jnp.zeros((2, 1, LANES), F32)
            for dr in range(2):
                wg2 = wg2.at[:, dr, :nh].set(wg[:, dr]).at[:, dr, nh:2 * nh].set(wg[:, 2 + dr])
                bg2 = bg2.at[dr, 0, :nh].set(gb[dr]).at[dr, 0, nh:2 * nh].set(gb[2 + dr])
            w_all = jnp.concatenate([w_in[:, :4 * w], wg2.reshape(d, 2 * LANES)], axis=1).astype(BF16)
            proj = normmod_matmul(x, p["norm_g"][l, 0], sc1, sh1, w_all, rows_per_mod, tm, w_all.shape[1] // 2)
            qscale = jnp.concatenate([jnp.full((w,), dh ** -0.5, F32), jnp.ones((w,), F32)])
            qk = short_conv(proj, 2 * w, p["ml_conv_w"][i], p["ml_conv_b"][i], qscale, seq_len, act=True, out_dtype=BF16)
            h2, cf, nf, mf = mlstm_scan(qk, proj, 2, bg2, ml_c0[i], ml_n0[i], ml_m0[i], nseq, seq_len, nh, dh)
            ml_fin.append((cf, nf, mf))
            x, hn = odd_out(h2, proj, 3, p["ml_norm_g"][i], p["od_w_out"][i].astype(BF16), x, g1, p["norm_g"][l, 1],
                            sc2, sh2, rows_per_mod, nh, dh, tm)
        tt = _tile(min(rows_per_mod, n), PEER_TOKENS)
        th, s2, w1, w2 = peer_scores(hn, p["pk_w_q"][l].T.astype(BF16), p["pk_keys"][l].astype(F32),
                                     _tile(tt, PEER_SCORE_TOKENS))
        ec = _tile(p["pk_u"].shape[1], PEER_EXPERTS)
        vt = p["pk_v"][l].astype(BF16).reshape(-1, ec, d).transpose(0, 2, 1)
        x = peer_dense(hn, p["pk_u"][l].astype(BF16), vt, th, s2, w1, w2, x, g2, p["final_g"], l == depth - 1,
                       rows_per_mod, tt, ec)
    return x, s5_fin, ml_fin


def kernel(x_prompt, x_sample, state_s5_re, state_s5_im, state_mlstm_C, state_mlstm_n, state_mlstm_m, c, c_ctx, norm_g, ada_w, ada_b, final_g, ev_w_in, hy_conv_w, hy_conv_b, hy_w1, hy_b1, hy_w2, hy_b2, hy_w3, hy_freq, hy_decay, hy_bias, s5_a_re, s5_a_im, s5_b_re, s5_b_im, s5_c_re, s5_c_im, s5_log_step, s5_d, s5_glu_w, s5_glu_b, ev_w_out, od_w_in, od_gate_b, ml_conv_w, ml_conv_b, ml_norm_g, od_w_out, pk_w_q, pk_keys, pk_u, pk_v):
    p = dict(norm_g=norm_g, ada_w=ada_w, ada_b=ada_b, final_g=final_g, ev_w_in=ev_w_in,
             hy_conv_w=hy_conv_w, hy_conv_b=hy_conv_b, hy_w1=hy_w1, hy_b1=hy_b1, hy_w2=hy_w2, hy_b2=hy_b2,
             hy_w3=hy_w3, hy_freq=hy_freq, hy_decay=hy_decay, hy_bias=hy_bias, s5_a_re=s5_a_re,
             s5_a_im=s5_a_im, s5_b_re=s5_b_re, s5_b_im=s5_b_im, s5_c_re=s5_c_re, s5_c_im=s5_c_im,
             s5_log_step=s5_log_step, s5_d=s5_d, s5_glu_w=s5_glu_w, s5_glu_b=s5_glu_b, ev_w_out=ev_w_out,
             od_w_in=od_w_in, od_gate_b=od_gate_b, ml_conv_w=ml_conv_w, ml_conv_b=ml_conv_b,
             ml_norm_g=ml_norm_g, od_w_out=od_w_out, pk_w_q=pk_w_q, pk_keys=pk_keys, pk_u=pk_u, pk_v=pk_v)
    nb, seq, d = x_prompt.shape
    db, dseq, _ = x_sample.shape
    depth = norm_g.shape[0]
    n_even, n_odd = (depth + 1) // 2, depth // 2
    assert db + 1 <= 8

    cond8 = jnp.zeros((8, d), F32).at[0].set(c_ctx.astype(F32)).at[1:1 + db].set(c.astype(F32))
    mods_ctx, mods_lat = [], []
    for l in range(depth):
        mod = ada_mod(cond8, ada_w[l].astype(F32), ada_b[l].astype(F32))
        chunks = [mod[:, j * d:(j + 1) * d] for j in range(6)]
        mods_ctx.append([ch[0:1].reshape(1, 1, d) for ch in chunks])
        mods_lat.append([ch[1:1 + db].reshape(db, 1, d) for ch in chunks])

    def s5_state(re, im, bsz):
        return [jnp.stack([re[:, i].reshape(bsz, 2, -1), im[:, i].reshape(bsz, 2, -1)], axis=2).astype(F32)
                for i in range(n_even)]

    ngrp, npst = s5_a_re.shape[2], s5_a_re.shape[3]
    nh, dh = state_mlstm_C.shape[3], state_mlstm_C.shape[4]
    zeros_s5 = jnp.zeros((nb, n_even, 2, ngrp, npst), F32)
    y_prompt, s5_fin, ml_fin = _trunk(
        x_prompt.reshape(nb * seq, d), mods_ctx, s5_state(zeros_s5, zeros_s5, nb),
        [jnp.zeros((nb, 2, nh, dh, dh), F32)] * n_odd, [jnp.zeros((nb, 2, nh, dh), F32)] * n_odd,
        [jnp.zeros((nb, 2, nh, 1), F32)] * n_odd, p, nb, seq, nb * seq)
    x_lat = add_pos(x_sample.reshape(db * dseq, d), _pos_embed(dseq, d, GRID_W), dseq, _tile(dseq, ROW_TILE))
    y_sample, _, _ = _trunk(
        x_lat, mods_lat, s5_state(state_s5_re, state_s5_im, db),
        [state_mlstm_C[:, i].astype(F32) for i in range(n_odd)], [state_mlstm_n[:, i].astype(F32) for i in range(n_odd)],
        [state_mlstm_m[:, i].astype(F32)[..., None] for i in range(n_odd)], p, db, dseq, dseq)

    new_s5_re = jnp.stack([h[:, :, 0].reshape(nb, 2, ngrp, npst) for h in s5_fin], axis=1)
    new_s5_im = jnp.stack([h[:, :, 1].reshape(nb, 2, ngrp, npst) for h in s5_fin], axis=1)
    new_c = jnp.stack([f[0] for f in ml_fin], axis=1)
    new_n = jnp.stack([f[1] for f in ml_fin], axis=1)
    new_m = jnp.stack([f[2][..., 0] for f in ml_fin], axis=1)
    return (y_prompt.reshape(nb, seq, d), y_sample.reshape(db, dseq, d), new_s5_re, new_s5_im, new_c, new_n, new_m)
```

```python
import functools
import math

import jax
import jax.numpy as jnp
from jax import lax
from jax.experimental import pallas as pl
from jax.experimental.pallas import tpu as pltpu

F32 = jnp.float32
BF16 = jnp.bfloat16
EPS = 1e-6
HIGHEST = lax.Precision.HIGHEST
V7X_VMEM_LIMIT_BYTES = 56 * 1024 * 1024
LANES = 128
SUBLANES = 8
ML_CHUNK = 128
PK_TOPK = 16
GRID_W = 64
NEG_INF = float("-inf")
ROW_TILE = 1024
DFT_TILE = 512
ADA_COLS = 1536
CONV_COLS = 256
S5_CHUNK = 256
PEER_SCORE_TOKENS = 256
PEER_TOKENS = 512
PEER_EXPERTS = 2048


def _cp(*sem):
    return pltpu.CompilerParams(dimension_semantics=sem, vmem_limit_bytes=V7X_VMEM_LIMIT_BYTES)


def _dot(a, b, **kw):
    return jnp.dot(a, b, preferred_element_type=F32, **kw)


def _dot_nt(a, b):
    return lax.dot_general(a, b, (((1,), (1,)), ((), ())), preferred_element_type=F32)


def _silu(x):
    return x * jax.nn.sigmoid(x)


def _ada_kernel(c_ref, w_ref, b_ref, o_ref):
    o_ref[...] = _dot(_silu(c_ref[...]), w_ref[...], precision=HIGHEST) + b_ref[...]


def ada_mod(cond8, w, b):
    d, no = w.shape
    tn = _tile(no, ADA_COLS)
    return pl.pallas_call(
        _ada_kernel, grid=(no // tn,),
        in_specs=[pl.BlockSpec((8, d), lambda j: (0, 0)), pl.BlockSpec((d, tn), lambda j: (0, j)),
                  pl.BlockSpec((1, tn), lambda j: (0, j))],
        out_specs=pl.BlockSpec((8, tn), lambda j: (0, j)),
        out_shape=jax.ShapeDtypeStruct((8, no), F32), compiler_params=_cp("parallel"), name="ada_mod",
    )(cond8, w, b.reshape(1, no))


def _normmod(x, g, sc, sh):
    y = x * lax.rsqrt(jnp.mean(x * x, axis=-1, keepdims=True) + EPS)
    return (y * g) * (1.0 + sc) + sh


def _mod_spec(d, tm, rows_per_mod):
    return pl.BlockSpec((1, 1, d), lambda i, j: ((i * tm) // rows_per_mod, 0, 0))


def _nm_matmul_kernel(x_ref, g_ref, sc_ref, sh_ref, w_ref, o_ref, h_ref):
    @pl.when(pl.program_id(1) == 0)
    def _():
        h_ref[...] = _normmod(x_ref[...], g_ref[...], sc_ref[0], sh_ref[0]).astype(BF16)
    o_ref[...] = _dot(h_ref[...], w_ref[...])


def normmod_matmul(x, g, sc, sh, w_bf16, rows_per_mod, tm, tn):
    n, d = x.shape
    no = w_bf16.shape[1]
    return pl.pallas_call(
        _nm_matmul_kernel, grid=(n // tm, no // tn),
        in_specs=[pl.BlockSpec((tm, d), lambda i, j: (i, 0)), pl.BlockSpec((1, d), lambda i, j: (0, 0)),
                  _mod_spec(d, tm, rows_per_mod), _mod_spec(d, tm, rows_per_mod),
                  pl.BlockSpec((d, tn), lambda i, j: (0, j))],
        out_specs=pl.BlockSpec((tm, tn), lambda i, j: (i, j)),
        out_shape=jax.ShapeDtypeStruct((n, no), F32),
        scratch_shapes=[pltpu.VMEM((tm, d), BF16)],
        compiler_params=_cp("parallel", "arbitrary"), name="normmod_matmul",
    )(x, g.reshape(1, d), sc, sh, w_bf16)


def _add_rows_kernel(x_ref, p_ref, o_ref):
    o_ref[...] = x_ref[...] + p_ref[...]


def add_pos(x, pe, seq_len, tm):
    n, d = x.shape
    nb = seq_len // tm
    return pl.pallas_call(
        _add_rows_kernel, grid=(n // tm,),
        in_specs=[pl.BlockSpec((tm, d), lambda i: (i, 0)), pl.BlockSpec((tm, d), lambda i: (i % nb, 0))],
        out_specs=pl.BlockSpec((tm, d), lambda i: (i, 0)),
        out_shape=jax.ShapeDtypeStruct((n, d), F32), compiler_params=_cp("parallel"), name="add_pos",
    )(x, pe)


def _sconv_kernel(x_ref, w_ref, b_ref, s_ref, o_ref, *, act):
    x = x_ref[...]
    n_tok = x.shape[0]
    row = lax.broadcasted_iota(jnp.int32, x.shape, 0)
    prev = jnp.where(row == 0, 0.0, pltpu.roll(x, 1, 0))
    nxt = jnp.where(row == n_tok - 1, 0.0, pltpu.roll(x, n_tok - 1, 0))
    y = prev * w_ref[0:1, :] + x * w_ref[1:2, :] + nxt * w_ref[2:3, :] + b_ref[...]
    if act:
        y = _silu(y) * s_ref[...]
    o_ref[...] = y.astype(o_ref.dtype)


def short_conv(a, ncols, w, b, scale, seq_len, act, out_dtype):
    n = a.shape[0]
    cb = _tile(ncols, CONV_COLS)
    return pl.pallas_call(
        functools.partial(_sconv_kernel, act=act), grid=(n // seq_len, ncols // cb),
        in_specs=[pl.BlockSpec((seq_len, cb), lambda s, j: (s, j)), pl.BlockSpec((3, cb), lambda s, j: (0, j)),
                  pl.BlockSpec((1, cb), lambda s, j: (0, j)), pl.BlockSpec((1, cb), lambda s, j: (0, j))],
        out_specs=pl.BlockSpec((seq_len, cb), lambda s, j: (s, j)),
        out_shape=jax.ShapeDtypeStruct((n, ncols), out_dtype), compiler_params=_cp("parallel", "parallel"),
        name="short_conv",
    )(a, w, b.reshape(1, ncols), scale.reshape(1, ncols))


def dft_tables(n_tok):
    k = jnp.arange(n_tok, dtype=jnp.int32)
    blk = 1 << ((n_tok.bit_length() - 1) // 2)
    def thin(n):
        ang = ((k[:, None] * n[None, :]) % (2 * n_tok)).astype(F32) * (math.pi / n_tok)
        return jnp.cos(ang), jnp.sin(ang)
    (c_hi, s_hi), (c_lo, s_lo) = thin(jnp.arange(0, n_tok, blk, dtype=jnp.int32)), thin(jnp.arange(blk, dtype=jnp.int32))
    cos_t = (c_hi[:, :, None] * c_lo[:, None, :] - s_hi[:, :, None] * s_lo[:, None, :]).reshape(n_tok, n_tok)
    msin = -(s_hi[:, :, None] * c_lo[:, None, :] + c_hi[:, :, None] * s_lo[:, None, :]).reshape(n_tok, n_tok)
    alt = jnp.where(k % 2 == 0, 1.0, -1.0).astype(F32)
    a_t = msin.at[0, :].set(alt)
    a_tt = msin.at[:, 0].set(alt)
    return cos_t.astype(BF16), a_t.astype(BF16), a_tt.astype(BF16)


def _hyfilt_kernel(band_ref, w1_ref, b1_ref, w2_ref, b2_ref, w3_ref, fr_ref, dec_ref, h_ref, ss_ref, *,
                   n_tok, tl, hw, nbands):
    i = pl.program_id(0)
    pos = i * tl + lax.broadcasted_iota(jnp.int32, (tl, 1), 0)
    t = pos.astype(F32) / n_tok
    lane = lax.broadcasted_iota(jnp.int32, (tl, LANES), 1)
    ang = 2.0 * math.pi * t * band_ref[...]
    z = jnp.where(lane == 0, t, jnp.where(lane <= nbands, jnp.cos(ang),
                                          jnp.where(lane <= 2 * nbands, jnp.sin(ang), 0.0)))
    fr = fr_ref[...]
    h = jnp.sin(fr * (_dot(z, w1_ref[...], precision=HIGHEST) + b1_ref[...]))
    h = jnp.sin(fr * (_dot(h, w2_ref[...], precision=HIGHEST) + b2_ref[...]))
    h = _dot(h, w3_ref[...], precision=HIGHEST) * jnp.exp(-t * jnp.abs(dec_ref[...]))
    col = lax.broadcasted_iota(jnp.int32, h.shape, 1)
    is_bwd = (col // hw) % 2 == 1
    h = jnp.where(jnp.logical_and(is_bwd, pos == 0), 0.0, h)
    h_ref[...] = h.astype(BF16)

    @pl.when(i == 0)
    def _():
        ss_ref[...] = jnp.zeros_like(ss_ref)
    ss_ref[...] += jnp.sum(h * h, axis=0, keepdims=True)


def hyena_filter_taps(n_tok, w1, b1, w2, b2, w3, freq, decay, hw):
    emb, ffn = w1.shape
    nbands = (emb - 1) // 2
    tl = _tile(n_tok, DFT_TILE)
    bands = jnp.linspace(1e-4, nbands - 1, nbands, dtype=F32)
    band_row = jnp.zeros((1, LANES), F32).at[0, 1:1 + nbands].set(bands).at[0, 1 + nbands:1 + 2 * nbands].set(bands)
    w1p = jnp.zeros((LANES, ffn), F32).at[:emb].set(w1)
    nc = w3.shape[1]
    full = lambda shp: pl.BlockSpec(shp, lambda i: (0, 0))
    return pl.pallas_call(
        functools.partial(_hyfilt_kernel, n_tok=n_tok, tl=tl, hw=hw, nbands=nbands), grid=(n_tok // tl,),
        in_specs=[full((1, LANES)), full((LANES, ffn)), full((1, ffn)), full((ffn, ffn)), full((1, ffn)),
                  full((ffn, nc)), full((1, ffn)), full((1, nc))],
        out_specs=[pl.BlockSpec((tl, nc), lambda i: (i, 0)), full((1, nc))],
        out_shape=[jax.ShapeDtypeStruct((n_tok, nc), BF16), jax.ShapeDtypeStruct((1, nc), F32)],
        compiler_params=_cp("arbitrary"), name="hyena_filter_taps",
    )(band_row, w1p, b1.reshape(1, ffn), w2, b2.reshape(1, ffn), w3, freq.reshape(1, ffn), decay.reshape(1, nc))


def _filt_dft_kernel(c_ref, a_ref, h_ref, ss_ref, kr_ref, ki_ref, *, tf, hw):
    i = pl.program_id(1)
    hf = h_ref[:, :hw]
    hb = h_ref[:, hw:]
    cc = c_ref[...]
    aa = a_ref[...]
    zrf, zif, zrb, zib = _dot(cc, hf), _dot(aa, hf), _dot(cc, hb), _dot(aa, hb)
    scale = lax.rsqrt(ss_ref[:, :hw] + ss_ref[:, hw:] + EPS)
    first = (i * tf + lax.broadcasted_iota(jnp.int32, (tf, 1), 0)) == 0
    scale = scale * jnp.where(first, 0.5, 1.0)
    kr_ref[0] = (zrf + zrb) * scale
    ki_ref[0] = jnp.where(first, zif + zib, zif - zib) * scale


def hyena_filter_spectrum(cos_t, a_t, taps, sumsq, hw, tf):
    n_tok = cos_t.shape[0]
    norder = taps.shape[1] // (2 * hw)
    out = jax.ShapeDtypeStruct((norder, n_tok, hw), F32)
    return pl.pallas_call(
        functools.partial(_filt_dft_kernel, tf=tf, hw=hw), grid=(norder, n_tok // tf),
        in_specs=[pl.BlockSpec((tf, n_tok), lambda o, i: (i, 0)), pl.BlockSpec((tf, n_tok), lambda o, i: (i, 0)),
                  pl.BlockSpec((n_tok, 2 * hw), lambda o, i: (0, o)), pl.BlockSpec((1, 2 * hw), lambda o, i: (0, o))],
        out_specs=[pl.BlockSpec((1, tf, hw), lambda o, i: (o, i, 0))] * 2,
        out_shape=[out, out], compiler_params=_cp("parallel", "parallel"), name="hyena_filter_spectrum",
    )(cos_t, a_t, taps, sumsq)


def _hy_fwd_kernel(c_ref, a_ref, z_ref, kr_ref, ki_ref, yr_ref, yi_ref, *, tf):
    i = pl.program_id(0)
    zb = z_ref[...].astype(BF16)
    zr = _dot(c_ref[...], zb)
    zi = _dot(a_ref[...], zb)
    kr = kr_ref[0]
    ki = ki_ref[0]
    first = (i * tf + lax.broadcasted_iota(jnp.int32, (tf, 1), 0)) == 0
    yr_ref[...] = jnp.where(first, zr * kr, zr * kr - zi * ki).astype(BF16)
    yi_ref[...] = jnp.where(first, zi * ki, zr * ki + zi * kr).astype(BF16)


def hyena_fwd(cos_t, a_t, z, zcol, kr, ki, order, nseq, hw, tf):
    n_tok = cos_t.shape[0]
    nf = n_tok // tf
    out = jax.ShapeDtypeStruct((nseq * n_tok, hw), BF16)
    return pl.pallas_call(
        functools.partial(_hy_fwd_kernel, tf=tf), grid=(nf, nseq),
        in_specs=[pl.BlockSpec((tf, n_tok), lambda i, b: (i, 0)), pl.BlockSpec((tf, n_tok), lambda i, b: (i, 0)),
                  pl.BlockSpec((n_tok, hw), lambda i, b: (b, zcol)),
                  pl.BlockSpec((1, tf, hw), lambda i, b: (order, i, 0)),
                  pl.BlockSpec((1, tf, hw), lambda i, b: (order, i, 0))],
        out_specs=[pl.BlockSpec((tf, hw), lambda i, b: (b * nf + i, 0))] * 2,
        out_shape=[out, out], compiler_params=_cp("parallel", "parallel"), name="hyena_fwd",
    )(cos_t, a_t, z, kr, ki)


def _hy_inv_kernel(c_ref, at_ref, yr_ref, yi_ref, zp_ref, gate_ref, bias_ref, o_ref, *, inv_len):
    conv = (_dot(c_ref[...], yr_ref[...]) + _dot(at_ref[...], yi_ref[...])) * inv_len
    o_ref[...] = gate_ref[...] * (conv + bias_ref[...] * zp_ref[...])


def hyena_inv(cos_t, a_tt, yr, yi, zprev, zcol, gates, gcol, bias_row, nseq, hw, tf):
    n_tok = cos_t.shape[0]
    nf = n_tok // tf
    return pl.pallas_call(
        functools.partial(_hy_inv_kernel, inv_len=1.0 / n_tok), grid=(nf, nseq),
        in_specs=[pl.BlockSpec((tf, n_tok), lambda i, b: (i, 0)), pl.BlockSpec((tf, n_tok), lambda i, b: (i, 0)),
                  pl.BlockSpec((n_tok, hw), lambda i, b: (b, 0)), pl.BlockSpec((n_tok, hw), lambda i, b: (b, 0)),
                  pl.BlockSpec((tf, hw), lambda i, b: (b * nf + i, zcol)),
                  pl.BlockSpec((tf, hw), lambda i, b: (b * nf + i, gcol)),
                  pl.BlockSpec((1, hw), lambda i, b: (0, 0))],
        out_specs=pl.BlockSpec((tf, hw), lambda i, b: (b * nf + i, 0)),
        out_shape=jax.ShapeDtypeStruct((nseq * n_tok, hw), F32),
        compiler_params=_cp("parallel", "parallel"), name="hyena_inv",
    )(cos_t, a_tt, yr, yi, zprev, gates, bias_row)


S5_DIAG_BLOCKS = 2


S5_SEQS_PER_STEP = 4


def _s5_kernel(u_ref, bre_ref, bim_ref, cre_ref, cim_ref, lam_ref, h0_ref, y_ref, hfin_ref, hre_s, him_s, st_s, *,
               tc, nc, ns, nb):
    d = pl.program_id(0)
    c = pl.program_id(2)

    @pl.when(c == 0)
    def _():
        st_s[...] = h0_ref[:, 0]

    sw = u_ref.shape[2]
    halves = [(slice(j * sw // S5_DIAG_BLOCKS, (j + 1) * sw // S5_DIAG_BLOCKS),
               slice(j * ns // S5_DIAG_BLOCKS, (j + 1) * ns // S5_DIAG_BLOCKS)) for j in range(S5_DIAG_BLOCKS)]
    for j in range(nb):
        ub = u_ref[j].astype(BF16)
        for us, hs in halves:
            hre_s[j, :, hs] = _dot(ub[:, us], bre_ref[0, us, hs])
            him_s[j, :, hs] = _dot(ub[:, us], bim_ref[0, us, hs])
    lr = lam_ref[0, 0:1, :]
    li = lam_ref[0, 1:2, :]

    def body(t, carry):
        r = jnp.where(d == 0, t, tc - 1 - t)
        new = []
        for j, (hr, hi) in enumerate(carry):
            nr = lr * hr - li * hi + hre_s[j, pl.ds(r, 1), :]
            ni = lr * hi + li * hr + him_s[j, pl.ds(r, 1), :]
            hre_s[j, pl.ds(r, 1), :] = nr
            him_s[j, pl.ds(r, 1), :] = ni
            new.append((nr, ni))
        return tuple(new)

    start = tuple((st_s[j, 0:1, :], st_s[j, 1:2, :]) for j in range(nb))
    for j, (hr, hi) in enumerate(lax.fori_loop(0, tc, body, start, unroll=4)):
        st_s[j, 0:1, :] = hr
        st_s[j, 1:2, :] = hi
    for j in range(nb):
        for us, hs in halves:
            y_ref[0, j, :, us] = (_dot(hre_s[j, :, hs].astype(BF16), cre_ref[0, hs, us])
                                  + _dot(him_s[j, :, hs].astype(BF16), cim_ref[0, hs, us]))

    @pl.when(c == nc - 1)
    def _():
        hfin_ref[:, 0] = st_s[...]


def s5_scan(proj, ucol, bre, bim, cre, cim, lam, h0, nseq, seq_len, sw, tc):
    ns = bre.shape[2]
    nc = seq_len // tc
    nb = S5_SEQS_PER_STEP if nseq % S5_SEQS_PER_STEP == 0 else 1

    def chunk(d, c):
        return c + d * (nc - 1 - 2 * c)

    y, hfin = pl.pallas_call(
        functools.partial(_s5_kernel, tc=tc, nc=nc, ns=ns, nb=nb), grid=(2, nseq // nb, nc),
        in_specs=[pl.BlockSpec((nb, tc, sw), lambda d, b, c: (b, chunk(d, c), ucol)),
                  pl.BlockSpec((1, sw, ns), lambda d, b, c: (d, 0, 0)),
                  pl.BlockSpec((1, sw, ns), lambda d, b, c: (d, 0, 0)),
                  pl.BlockSpec((1, ns, sw), lambda d, b, c: (d, 0, 0)),
                  pl.BlockSpec((1, ns, sw), lambda d, b, c: (d, 0, 0)),
                  pl.BlockSpec((1, 2, ns), lambda d, b, c: (d, 0, 0)),
                  pl.BlockSpec((nb, 1, 2, ns), lambda d, b, c: (b, d, 0, 0))],
        out_specs=[pl.BlockSpec((1, nb, tc, sw), lambda d, b, c: (d, b, chunk(d, c), 0)),
                   pl.BlockSpec((nb, 1, 2, ns), lambda d, b, c: (b, d, 0, 0))],
        out_shape=[jax.ShapeDtypeStruct((2, nseq, seq_len, sw), F32), jax.ShapeDtypeStruct((nseq, 2, 2, ns), F32)],
        scratch_shapes=[pltpu.VMEM((nb, tc, ns), F32), pltpu.VMEM((nb, tc, ns), F32), pltpu.VMEM((nb, 2, ns), F32)],
        compiler_params=_cp("parallel", "parallel", "arbitrary"), name="s5_scan",
    )(proj.reshape(nseq, seq_len, proj.shape[1]), bre, bim, cre, cim, lam, h0)
    return y.reshape(2, nseq * seq_len, sw), hfin


def _s5_glu_kernel(yf_ref, yb_ref, u_ref, d_ref, w_ref, b_ref, o_ref):
    y = jax.nn.gelu(yf_ref[0] + yb_ref[0] + d_ref[...] * u_ref[...])
    o_ref[...] = y * jax.nn.sigmoid(_dot(y.astype(BF16), w_ref[...]) + b_ref[...])


def s5_glu(y2, proj, ucol, d_skip, glu_w_bf16, glu_b, tm):
    _, n, sw = y2.shape
    return pl.pallas_call(
        _s5_glu_kernel, grid=(n // tm,),
        in_specs=[pl.BlockSpec((1, tm, sw), lambda i: (0, i, 0)), pl.BlockSpec((1, tm, sw), lambda i: (1, i, 0)),
                  pl.BlockSpec((tm, sw), lambda i: (i, ucol)), pl.BlockSpec((1, sw), lambda i: (0, 0)),
                  pl.BlockSpec((sw, sw), lambda i: (0, 0)), pl.BlockSpec((1, sw), lambda i: (0, 0))],
        out_specs=pl.BlockSpec((tm, sw), lambda i: (i, 0)),
        out_shape=jax.ShapeDtypeStruct((n, sw), F32), compiler_params=_cp("parallel"), name="s5_glu",
    )(y2, y2, proj, d_skip.reshape(1, sw), glu_w_bf16, glu_b.reshape(1, sw))


def _residual_and_next_norm(x_ref, gate_ref, y, g2_ref, sc2_ref, sh2_ref, o_ref, hn_ref):
    xn = x_ref[...] + gate_ref[0] * y
    o_ref[...] = xn
    hn_ref[...] = _normmod(xn, g2_ref[...], sc2_ref[0], sh2_ref[0]).astype(BF16)


def _row_specs(d, tm, rows_per_mod):
    mod = pl.BlockSpec((1, 1, d), lambda i: ((i * tm) // rows_per_mod, 0, 0))
    return [pl.BlockSpec((tm, d), lambda i: (i, 0)), mod, pl.BlockSpec((1, d), lambda i: (0, 0)), mod, mod]


def _row_outs(n, d, tm):
    spec = pl.BlockSpec((tm, d), lambda i: (i, 0))
    return [spec, spec], [jax.ShapeDtypeStruct((n, d), F32), jax.ShapeDtypeStruct((n, d), BF16)]


def _even_out_kernel(a_ref, b_ref, wa_ref, wb_ref, x_ref, gate_ref, g2_ref, sc2_ref, sh2_ref, o_ref, hn_ref):
    y = _dot(a_ref[...].astype(BF16), wa_ref[...]) + _dot(b_ref[...].astype(BF16), wb_ref[...])
    _residual_and_next_norm(x_ref, gate_ref, y, g2_ref, sc2_ref, sh2_ref, o_ref, hn_ref)


def even_out(hy, s5o, w_bf16, x, gate, g2, sc2, sh2, rows_per_mod, tm):
    n, d = x.shape
    hw = hy.shape[1]
    sw = s5o.shape[1]
    out_specs, out_shape = _row_outs(n, d, tm)
    return pl.pallas_call(
        _even_out_kernel, grid=(n // tm,),
        in_specs=[pl.BlockSpec((tm, hw), lambda i: (i, 0)), pl.BlockSpec((tm, sw), lambda i: (i, 0)),
                  pl.BlockSpec((hw, d), lambda i: (0, 0)), pl.BlockSpec((sw, d), lambda i: (hw // sw, 0))]
        + _row_specs(d, tm, rows_per_mod),
        out_specs=out_specs, out_shape=out_shape, compiler_params=_cp("parallel"), name="even_out",
    )(hy, s5o, w_bf16, w_bf16, x, gate, g2.reshape(1, d), sc2, sh2)


ML_HEAD_GROUP = 8


def _log_sigmoid(x):
    return jnp.minimum(x, 0.0) - jnp.log1p(jnp.exp(-jnp.abs(x)))


def _mlstm_kernel(q_ref, k_ref, v_ref, g_ref, gb_ref, c0_ref, n0_ref, m0_ref, h_ref, cf_ref, nf_ref, mf_ref,
                  c_s, m_s, *, nh, dh, tc, nc):
    d = pl.program_id(0)
    c = pl.program_id(2)

    @pl.when(c == 0)
    def _():
        for h in range(nh):
            c_s[h, :, :dh] = c0_ref[0, 0, h]
            c_s[h, :, dh:] = jnp.broadcast_to(n0_ref[0, 0, h:h + 1, :], (dh, dh)).T
        m_s[...] = m0_ref[0, 0]

    ones = jnp.ones((tc, dh), F32)
    gates = g_ref[...] + gb_ref[0]
    lane = lax.broadcasted_iota(jnp.int32, gates.shape, 1)
    logf = jnp.where(jnp.logical_and(lane >= nh, lane < 2 * nh), _log_sigmoid(gates), 0.0)
    r_i = lax.broadcasted_iota(jnp.int32, (tc, tc), 0)
    s_i = lax.broadcasted_iota(jnp.int32, (tc, tc), 1)
    causal = (r_i - s_i) * (1 - 2 * d) >= 0
    bcum = _dot(causal.astype(F32), logf, precision=HIGHEST)
    btot = jnp.sum(logf, axis=0, keepdims=True)
    gates_t = gates.T
    bcum_t = bcum.T
    for g0 in range(0, nh, ML_HEAD_GROUP):
        hds = list(range(g0, min(g0 + ML_HEAD_GROUP, nh)))
        hsl = {h: slice(h * dh, (h + 1) * dh) for h in hds}
        b_col = {h: bcum[:, nh + h:nh + h + 1] for h in hds}
        m_old = {h: m_s[h:h + 1, :] for h in hds}
        a = {h: b_col[h] + m_old[h] for h in hds}
        src = {h: jnp.where(causal, gates_t[h:h + 1, :] - bcum_t[nh + h:nh + h + 1, :], NEG_INF) for h in hds}
        mq = {h: jnp.maximum(a[h], b_col[h] + jnp.max(src[h], axis=-1, keepdims=True)) for h in hds}
        rel = {h: jnp.broadcast_to(b_col[h] - mq[h], (tc, tc)) for h in hds}
        s = {h: _dot_nt(q_ref[:, hsl[h]], k_ref[:, hsl[h]]) * jnp.exp(src[h] + rel[h]) for h in hds}
        v1 = {h: jnp.concatenate([v_ref[:, hsl[h]], ones], axis=1) for h in hds}
        cn = {h: c_s[h] for h in hds}
        qw = {h: jnp.exp(rel[h][:, :dh] + m_old[h]) * q_ref[:, hsl[h]] for h in hds}
        both = {h: _dot(s[h], v1[h]) + _dot(qw[h], cn[h]) for h in hds}
        for h in hds:
            h_ref[0, :, hsl[h]] = both[h][:, :dh] / jnp.maximum(jnp.abs(both[h][:, dh:]), jnp.exp(-mq[h]))
        b_last = {h: btot[:, nh + h:nh + h + 1] for h in hds}
        g = {h: b_last[h] - bcum_t[nh + h:nh + h + 1, :] + gates_t[h:h + 1, :] for h in hds}
        m_new = {h: jnp.maximum(b_last[h] + m_old[h], jnp.max(g[h], axis=1, keepdims=True)) for h in hds}
        kw_t = {h: k_ref[:, hsl[h]].astype(F32).T * jnp.exp(g[h] - m_new[h]) for h in hds}
        for h in hds:
            c_s[h] = jnp.exp(b_last[h] + m_old[h] - m_new[h]) * cn[h] + _dot(kw_t[h], v1[h])
            m_s[h:h + 1, :] = m_new[h]

    @pl.when(c == nc - 1)
    def _():
        for h in range(nh):
            cf_ref[0, 0, h] = c_s[h, :, :dh]
            nf_ref[0, 0, h:h + 1, :] = c_s[h, :, dh:].T[0:1, :]
        mf_ref[0, 0] = m_s[...]


def mlstm_scan(qk, proj, vcol, gate_bias, c0, n0, m0, nseq, seq_len, nh, dh):
    tc = ML_CHUNK
    nc = seq_len // tc
    w = nh * dh

    def chunk(d, c):
        return c + d * (nc - 1 - 2 * c)

    rowblk = lambda d, b, c: b * nc + chunk(d, c)
    st = lambda shp: pl.BlockSpec((1, 1) + shp, lambda d, b, c: (b, d) + (0,) * len(shp))
    return pl.pallas_call(
        functools.partial(_mlstm_kernel, nh=nh, dh=dh, tc=tc, nc=nc), grid=(2, nseq, nc),
        in_specs=[pl.BlockSpec((tc, w), lambda d, b, c: (rowblk(d, b, c), 0)),
                  pl.BlockSpec((tc, w), lambda d, b, c: (rowblk(d, b, c), 1)),
                  pl.BlockSpec((tc, w), lambda d, b, c: (rowblk(d, b, c), vcol)),
                  pl.BlockSpec((tc, LANES), lambda d, b, c: (rowblk(d, b, c), 4 * w // LANES + d)),
                  pl.BlockSpec((1, 1, LANES), lambda d, b, c: (d, 0, 0)),
                  st((nh, dh, dh)), st((nh, dh)), st((nh, 1))],
        out_specs=[pl.BlockSpec((1, tc, w), lambda d, b, c: (d, rowblk(d, b, c), 0)),
                   st((nh, dh, dh)), st((nh, dh)), st((nh, 1))],
        out_shape=[jax.ShapeDtypeStruct((2, nseq * seq_len, w), F32),
                   jax.ShapeDtypeStruct((nseq, 2, nh, dh, dh), F32), jax.ShapeDtypeStruct((nseq, 2, nh, dh), F32),
                   jax.ShapeDtypeStruct((nseq, 2, nh, 1), F32)],
        scratch_shapes=[pltpu.VMEM((nh, dh, 2 * dh), F32), pltpu.VMEM((nh, 1), F32)],
        compiler_params=_cp("parallel", "parallel", "arbitrary"), name="mlstm_scan",
    )(qk, qk, proj, proj, gate_bias, c0, n0, m0)


def _odd_out_kernel(hf_ref, hb_ref, og_ref, ng_ref, w_ref, x_ref, gate_ref, g2_ref, sc2_ref, sh2_ref, o_ref, hn_ref,
                    a_s, *, nh, dh):
    for h in range(nh):
        hs = slice(h * dh, (h + 1) * dh)
        blk = hf_ref[0, :, hs] + hb_ref[0, :, hs]
        blk = blk * lax.rsqrt(jnp.mean(blk * blk, axis=-1, keepdims=True) + EPS)
        a_s[:, hs] = ((blk * ng_ref[:, hs]) * _silu(og_ref[:, hs])).astype(BF16)
    _residual_and_next_norm(x_ref, gate_ref, _dot(a_s[...], w_ref[...]), g2_ref, sc2_ref, sh2_ref, o_ref, hn_ref)


def odd_out(h2, proj, ocol, norm_g, w_bf16, x, gate, g2, sc2, sh2, rows_per_mod, nh, dh, tm):
    n, d = x.shape
    w = nh * dh
    out_specs, out_shape = _row_outs(n, d, tm)
    return pl.pallas_call(
        functools.partial(_odd_out_kernel, nh=nh, dh=dh), grid=(n // tm,),
        in_specs=[pl.BlockSpec((1, tm, w), lambda i: (0, i, 0)), pl.BlockSpec((1, tm, w), lambda i: (1, i, 0)),
                  pl.BlockSpec((tm, w), lambda i: (i, ocol)), pl.BlockSpec((1, w), lambda i: (0, 0)),
                  pl.BlockSpec((w, d), lambda i: (0, 0))] + _row_specs(d, tm, rows_per_mod),
        out_specs=out_specs, out_shape=out_shape, scratch_shapes=[pltpu.VMEM((tm, w), BF16)],
        compiler_params=_cp("parallel"), name="odd_out",
    )(h2, h2, proj, norm_g.reshape(1, w), w_bf16, x, gate, g2.reshape(1, d), sc2, sh2)


def _sort_network(n):
    pairs, p = [], 1
    while p < n:
        k = p
        while k >= 1:
            for j in range(k % p, n - k, 2 * k):
                for i in range(min(k, n - j - k)):
                    if (i + j) // (2 * p) == (i + j + k) // (2 * p):
                        pairs.append((i + j, i + j + k))
            k //= 2
        p *= 2
    return pairs


def _top_values_tiled(arrays, k, outs):
    cols = []
    for arr in arrays:
        tiles = [arr[r:r + SUBLANES, :] for r in range(0, arr.shape[0], SUBLANES)]
        for lo, hi in _sort_network(len(tiles)):
            tiles[lo], tiles[hi] = jnp.maximum(tiles[lo], tiles[hi]), jnp.minimum(tiles[lo], tiles[hi])
        cols.append(tiles)
    for j in range(k):
        for tiles, out_s in zip(cols, outs):
            m = jnp.max(tiles[0], axis=0, keepdims=True)
            out_s[j:j + 1, :] = m
            hit = tiles[0] == m
            for i in range(min(len(tiles), k - 1 - j)):
                below = tiles[i + 1] if i + 1 < len(tiles) else NEG_INF
                tiles[i] = jnp.where(hit, below, tiles[i])


def _pair_candidates(k):
    return [(a, k // (a + 1)) for a in range(k)]


PEER_HEADS_PER_TRIP = 8


def _peer_score_kernel(h_ref, wq_ref, keys_ref, th_ref, s2_ref, w1_ref, w2_ref, q_s, v_s, cand_s, best_s, *,
                       nh, half, topk):
    q_s[...] = _dot_nt(wq_ref[...], h_ref[...])
    kk = topk + 1
    cand_s[...] = jnp.full(cand_s.shape, NEG_INF, F32)
    group = range(PEER_HEADS_PER_TRIP)

    def heads(trip, carry):
        hds = [trip * PEER_HEADS_PER_TRIP + u for u in group]
        scores = []
        for hd in hds:
            base = pl.multiple_of(hd * 2 * half, 2 * half)
            scores.append(_dot(keys_ref[hd, 0], q_s[pl.ds(base, half), :]))
            scores.append(_dot(keys_ref[hd, 1], q_s[pl.ds(base + half, half), :]))
        _top_values_tiled(scores, kk, [v_s.at[u, c] for u in group for c in range(2)])
        for u in group:
            off = 0
            for a, cnt in _pair_candidates(kk):
                cand_s[u, off:off + cnt, :] = v_s[u, 0, a:a + 1, :] + v_s[u, 1, 0:cnt, :]
                off += cnt
        _top_values_tiled([cand_s[u] for u in group], kk, [best_s.at[u] for u in group])
        for u, hd in enumerate(hds):
            s1, s2 = scores[2 * u], scores[2 * u + 1]
            best = best_s[u, 0:topk, :]
            z = jnp.sum(jnp.exp(best - best[0:1, :]), axis=0, keepdims=True)
            tmid = 0.5 * (best_s[u, topk - 1:topk, :] + best_s[u, topk:topk + 1, :])
            th = tmid - s1
            w1 = jnp.exp(s1 - v_s[u, 0, 0:1, :]) / z
            w2 = jnp.exp(s2 - v_s[u, 1, 0:1, :])
            for lt in range(s1.shape[1] // LANES):
                sl = slice(lt * LANES, (lt + 1) * LANES)
                th_ref[hd, lt] = th[:, sl]
                s2_ref[hd, lt] = s2[:, sl]
                w1_ref[hd, lt] = w1[:, sl]
                w2_ref[hd, lt] = w2[:, sl]
        return carry

    lax.fori_loop(0, nh // PEER_HEADS_PER_TRIP, heads, 0)


def peer_scores(h_bf16, wq_t_bf16, keys, tt):
    n, d = h_bf16.shape
    nh, _, nk, half = keys.shape
    kk = PK_TOPK + 1
    top_rows = -(-kk // SUBLANES) * SUBLANES
    ncand = SUBLANES
    while ncand < sum(c for _, c in _pair_candidates(kk)):
        ncand *= 2
    big = jax.ShapeDtypeStruct((nh, n // LANES, nk, LANES), F32)
    bspec = pl.BlockSpec((nh, tt // LANES, nk, LANES), lambda i: (0, i, 0, 0))
    return pl.pallas_call(
        functools.partial(_peer_score_kernel, nh=nh, half=half, topk=PK_TOPK), grid=(n // tt,),
        in_specs=[pl.BlockSpec((tt, d), lambda i: (i, 0)), pl.BlockSpec((nh * 2 * half, d), lambda i: (0, 0)),
                  pl.BlockSpec((nh, 2, nk, half), lambda i: (0, 0, 0, 0))],
        out_specs=[bspec, bspec, bspec, bspec],
        out_shape=[big, big, big, big],
        scratch_shapes=[pltpu.VMEM((nh * 2 * half, tt), F32), pltpu.VMEM((PEER_HEADS_PER_TRIP, 2, top_rows, tt), F32),
                        pltpu.VMEM((PEER_HEADS_PER_TRIP, ncand, tt), F32),
                        pltpu.VMEM((PEER_HEADS_PER_TRIP, top_rows, tt), F32)],
        compiler_params=_cp("parallel"), name="peer_scores",
    )(h_bf16, wq_t_bf16, keys)


PEER_KEY_ROWS = 16
GELU_C1 = math.sqrt(2.0 / math.pi)
GELU_C2 = 0.044715 * GELU_C1


def _gelu_tanh(x):
    half_x = 0.5 * x
    return half_x + half_x * jnp.tanh(x * (GELU_C1 + GELU_C2 * (x * x)))


def _peer_dense_kernel(h_ref, u_ref, vt_ref, th_ref, s2_ref, w1_ref, w2_ref, x_ref, gate_ref, fg_ref, o_ref,
                       acc_s, st_s, wt_s, *, nh, nk, ec, tt, final):
    e = pl.program_id(1)
    nlt = tt // LANES
    n_i1 = ec // nk
    nkt = nk // PEER_KEY_ROWS
    nsub = PEER_KEY_ROWS // SUBLANES

    @pl.when(e == 0)
    def _():
        acc_s[...] = jnp.zeros_like(acc_s)

    st = _gelu_tanh(_dot_nt(u_ref[...], h_ref[...]))
    for lt in range(nlt):
        st_s[lt] = st[:, lt * LANES:(lt + 1) * LANES]

    def tile(idx, carry):
        lt = idx // nkt
        k0 = (idx % nkt) * PEER_KEY_ROWS
        subs = [pl.ds(pl.multiple_of(k0 + j * SUBLANES, SUBLANES), SUBLANES) for j in range(nsub)]
        g = [[jnp.zeros((SUBLANES, LANES), F32) for _ in subs] for _ in range(n_i1)]
        for hd in range(nh):
            s2t = [s2_ref[hd, lt, sub, :] for sub in subs]
            w2t = [w2_ref[hd, lt, sub, :] for sub in subs]
            for li in range(n_i1):
                thb = jnp.broadcast_to(th_ref[hd, lt, li:li + 1, :], (SUBLANES, LANES))
                w1b = jnp.broadcast_to(w1_ref[hd, lt, li:li + 1, :], (SUBLANES, LANES))
                for j in range(nsub):
                    g[li][j] = g[li][j] + jnp.where(s2t[j] >= thb, w2t[j] * w1b, 0.0)
        for li in range(n_i1):
            rows = pl.ds(pl.multiple_of(li * nk + k0, PEER_KEY_ROWS), PEER_KEY_ROWS)
            wt_s[lt, rows, :] = (st_s[lt, rows, :] * jnp.concatenate(g[li], axis=0)).astype(BF16)
        return carry

    lax.fori_loop(0, nlt * nkt, tile, 0, unroll=2)
    wt = jnp.concatenate([wt_s[lt] for lt in range(nlt)], axis=1)
    acc_s[...] += _dot(vt_ref[0], wt)

    @pl.when(e == pl.num_programs(1) - 1)
    def _():
        xn = x_ref[...] + gate_ref[0] * acc_s[...].T
        if final:
            xn = (xn * lax.rsqrt(jnp.mean(xn * xn, axis=-1, keepdims=True) + EPS)) * fg_ref[...]
        o_ref[...] = xn


def peer_dense(h_bf16, u_bf16, vt_bf16, th, s2, w1, w2, x, gate, final_g, final, rows_per_mod, tt, ec):
    n, d = x.shape
    nh, _, nk, _ = s2.shape
    nchunk = u_bf16.shape[0] // ec
    bspec = pl.BlockSpec((nh, tt // LANES, nk, LANES), lambda i, e: (0, i, 0, 0))
    rspec = pl.BlockSpec((nh, tt // LANES, ec // nk, LANES), lambda i, e: (0, i, e, 0))
    tile_buf = (tt // LANES, ec, LANES)
    return pl.pallas_call(
        functools.partial(_peer_dense_kernel, nh=nh, nk=nk, ec=ec, tt=tt, final=final), grid=(n // tt, nchunk),
        in_specs=[pl.BlockSpec((tt, d), lambda i, e: (i, 0)), pl.BlockSpec((ec, d), lambda i, e: (e, 0)),
                  pl.BlockSpec((1, d, ec), lambda i, e: (e, 0, 0)), rspec, bspec, rspec, bspec,
                  pl.BlockSpec((tt, d), lambda i, e: (i, 0)),
                  pl.BlockSpec((1, 1, d), lambda i, e: ((i * tt) // rows_per_mod, 0, 0)),
                  pl.BlockSpec((1, d), lambda i, e: (0, 0))],
        out_specs=pl.BlockSpec((tt, d), lambda i, e: (i, 0)),
        out_shape=jax.ShapeDtypeStruct((n, d), F32),
        scratch_shapes=[pltpu.VMEM((d, tt), F32), pltpu.VMEM(tile_buf, F32), pltpu.VMEM(tile_buf, BF16)],
        compiler_params=_cp("parallel", "arbitrary"), name="peer_dense",
    )(h_bf16, u_bf16, vt_bf16, th, s2, w1, w2, x, gate, final_g.reshape(1, d))


def _s5_params(a_re, a_im, b_re, b_im, c_re, c_im, log_step):
    lam = lax.complex(a_re.astype(F32), a_im.astype(F32))
    lam_bar = jnp.exp(lam * jnp.exp(log_step.astype(F32))[..., None])
    b_bar = ((lam_bar - 1.0) / lam)[..., None] * lax.complex(b_re.astype(F32), b_im.astype(F32))
    ngrp, npst, nch = b_bar.shape[1:]
    eye = jnp.eye(ngrp, dtype=F32)

    def b_mat(part):
        return jnp.einsum("dgpj,gh->dgjhp", part, eye).reshape(2, ngrp * nch, ngrp * npst)

    def c_mat(part):
        return jnp.einsum("dgjp,gh->dgphj", part, eye).reshape(2, ngrp * npst, ngrp * nch)

    bre, bim = b_mat(b_bar.real).astype(BF16), b_mat(b_bar.imag).astype(BF16)
    cre, cim = c_mat(c_re.astype(F32)).astype(BF16), c_mat(-c_im.astype(F32)).astype(BF16)
    lam2 = jnp.stack([lam_bar.real.reshape(2, -1), lam_bar.imag.reshape(2, -1)], axis=1)
    return bre, bim, cre, cim, lam2


def _pos_embed(n_tok, d, grid_w):
    rows = n_tok // grid_w
    quarter = d // 4
    omega = 1.0 / (10000.0 ** (jnp.arange(quarter, dtype=F32) / quarter))

    def emb1d(pos):
        ang = pos.astype(F32)[:, None] * omega[None]
        return jnp.concatenate([jnp.sin(ang), jnp.cos(ang)], axis=-1)

    er = emb1d(jnp.arange(rows))
    ec = emb1d(jnp.arange(grid_w))
    half = d // 2
    pe = jnp.concatenate([jnp.broadcast_to(er[:, None], (rows, grid_w, half)),
                          jnp.broadcast_to(ec[None], (rows, grid_w, half))], axis=-1)
    return pe.reshape(rows * grid_w, d)


def _tile(n, pref):
    return pref if n % pref == 0 else n


def _trunk(x, mods, s5_h0, ml_c0, ml_n0, ml_m0, p, nseq, seq_len, rows_per_mod):
    n, d = x.shape
    tm = _tile(min(rows_per_mod, n), ROW_TILE)
    depth = p["norm_g"].shape[0]
    s5_fin, ml_fin = [], []
    for l in range(depth):
        sh1, sc1, g1, sh2, sc2, g2 = mods[l]
        i = l // 2
        if l % 2 == 0:
            hw = p["hy_bias"].shape[2]
            sw = p["s5_d"].shape[1]
            proj = normmod_matmul(x, p["norm_g"][l, 0], sc1, sh1, p["ev_w_in"][i].astype(BF16), rows_per_mod, tm,
                                  3 * hw + sw)
            hy_in = short_conv(proj, 3 * hw, p["hy_conv_w"][i], p["hy_conv_b"][i], jnp.ones((3 * hw,), F32),
                               seq_len, act=False, out_dtype=F32)
            cos_t, a_t, a_tt = dft_tables(seq_len)
            tf = _tile(seq_len, DFT_TILE)
            taps, sumsq = hyena_filter_taps(seq_len, p["hy_w1"][i], p["hy_b1"][i], p["hy_w2"][i], p["hy_b2"][i],
                                            p["hy_w3"][i], p["hy_freq"][i], p["hy_decay"][i], hw)
            kr, ki = hyena_filter_spectrum(cos_t, a_t, taps, sumsq, hw, tf)
            bias = p["hy_bias"][i].astype(F32)
            z, zcol = hy_in, 0
            for o in range(bias.shape[0]):
                yr, yi = hyena_fwd(cos_t, a_t, z, zcol, kr, ki, o, nseq, hw, tf)
                z = hyena_inv(cos_t, a_tt, yr, yi, z, zcol, hy_in, 1 + o, bias[o:o + 1], nseq, hw, tf)
                zcol = 0
            bre, bim, cre, cim, lam2 = _s5_params(p["s5_a_re"][i], p["s5_a_im"][i], p["s5_b_re"][i], p["s5_b_im"][i],
                                                  p["s5_c_re"][i], p["s5_c_im"][i], p["s5_log_step"][i])
            ucol = 3 * hw // sw
            y2, hfin = s5_scan(proj, ucol, bre, bim, cre, cim, lam2, s5_h0[i], nseq, seq_len, sw,
                               _tile(seq_len, S5_CHUNK))
            s5_fin.append(hfin)
            s5o = s5_glu(y2, proj, ucol, p["s5_d"][i], p["s5_glu_w"][i].astype(BF16), p["s5_glu_b"][i], tm)
            x, hn = even_out(z, s5o, p["ev_w_out"][i].astype(BF16), x, g1, p["norm_g"][l, 1], sc2, sh2, rows_per_mod, tm)
        else:
            nh = p["od_gate_b"].shape[2]
            w = p["ml_norm_g"].shape[1]
            dh = w // nh
            w_in = p["od_w_in"][i]
            wg = w_in[:, 4 * w:].reshape(d, 4, nh)
            gb = p["od_gate_b"][i].astype(F32)
            wg2 = jnp.zeros((d, 2, LANES), w_in.dtype)
            bg2 = jnp.zeros((2, 1, LANES), F32)
            for dr in range(2):
                wg2 = wg2.at[:, dr, :nh].set(wg[:, dr]).at[:, dr, nh:2 * nh].set(wg[:, 2 + dr])
                bg2 = bg2.at[dr, 0, :nh].set(gb[dr]).at[dr, 0, nh:2 * nh].set(gb[2 + dr])
            w_all = jnp.concatenate([w_in[:, :4 * w], wg2.reshape(d, 2 * LANES)], axis=1).astype(BF16)
            proj = normmod_matmul(x, p["norm_g"][l, 0], sc1, sh1, w_all, rows_per_mod, tm, w_all.shape[1] // 2)
            qscale = jnp.concatenate([jnp.full((w,), dh ** -0.5, F32), jnp.ones((w,), F32)])
            qk = short_conv(proj, 2 * w, p["ml_conv_w"][i], p["ml_conv_b"][i], qscale, seq_len, act=True, out_dtype=BF16)
            h2, cf, nf, mf = mlstm_scan(qk, proj, 2, bg2, ml_c0[i], ml_n0[i], ml_m0[i], nseq, seq_len, nh, dh)
            ml_fin.append((cf, nf, mf))
            x, hn = odd_out(h2, proj, 3, p["ml_norm_g"][i], p["od_w_out"][i].astype(BF16), x, g1, p["norm_g"][l, 1],
                            sc2, sh2, rows_per_mod, nh, dh, tm)
        tt = _tile(min(rows_per_mod, n), PEER_TOKENS)
        th, s2, w1, w2 = peer_scores(hn, p["pk_w_q"][l].T.astype(BF16), p["pk_keys"][l].astype(F32),
                                     _tile(tt, PEER_SCORE_TOKENS))
        ec = _tile(p["pk_u"].shape[1], PEER_EXPERTS)
        vt = p["pk_v"][l].astype(BF16).reshape(-1, ec, d).transpose(0, 2, 1)
        x = peer_dense(hn, p["pk_u"][l].astype(BF16), vt, th, s2, w1, w2, x, g2, p["final_g"], l == depth - 1,
                       rows_per_mod, tt, ec)
    return x, s5_fin, ml_fin


def kernel(x_prompt, x_sample, state_s5_re, state_s5_im, state_mlstm_C, state_mlstm_n, state_mlstm_m, c, c_ctx, norm_g, ada_w, ada_b, final_g, ev_w_in, hy_conv_w, hy_conv_b, hy_w1, hy_b1, hy_w2, hy_b2, hy_w3, hy_freq, hy_decay, hy_bias, s5_a_re, s5_a_im, s5_b_re, s5_b_im, s5_c_re, s5_c_im, s5_log_step, s5_d, s5_glu_w, s5_glu_b, ev_w_out, od_w_in, od_gate_b, ml_conv_w, ml_conv_b, ml_norm_g, od_w_out, pk_w_q, pk_keys, pk_u, pk_v):
    p = dict(norm_g=norm_g, ada_w=ada_w, ada_b=ada_b, final_g=final_g, ev_w_in=ev_w_in,
             hy_conv_w=hy_conv_w, hy_conv_b=hy_conv_b, hy_w1=hy_w1, hy_b1=hy_b1, hy_w2=hy_w2, hy_b2=hy_b2,
             hy_w3=hy_w3, hy_freq=hy_freq, hy_decay=hy_decay, hy_bias=hy_bias, s5_a_re=s5_a_re,
             s5_a_im=s5_a_im, s5_b_re=s5_b_re, s5_b_im=s5_b_im, s5_c_re=s5_c_re, s5_c_im=s5_c_im,
             s5_log_step=s5_log_step, s5_d=s5_d, s5_glu_w=s5_glu_w, s5_glu_b=s5_glu_b, ev_w_out=ev_w_out,
             od_w_in=od_w_in, od_gate_b=od_gate_b, ml_conv_w=ml_conv_w, ml_conv_b=ml_conv_b,
             ml_norm_g=ml_norm_g, od_w_out=od_w_out, pk_w_q=pk_w_q, pk_keys=pk_keys, pk_u=pk_u, pk_v=pk_v)
    nb, seq, d = x_prompt.shape
    db, dseq, _ = x_sample.shape
    depth = norm_g.shape[0]
    n_even, n_odd = (depth + 1) // 2, depth // 2
    assert db + 1 <= 8

    cond8 = jnp.zeros((8, d), F32).at[0].set(c_ctx.astype(F32)).at[1:1 + db].set(c.astype(F32))
    mods_ctx, mods_lat = [], []
    for l in range(depth):
        mod = ada_mod(cond8, ada_w[l].astype(F32), ada_b[l].astype(F32))
        chunks = [mod[:, j * d:(j + 1) * d] for j in range(6)]
        mods_ctx.append([ch[0:1].reshape(1, 1, d) for ch in chunks])
        mods_lat.append([ch[1:1 + db].reshape(db, 1, d) for ch in chunks])

    def s5_state(re, im, bsz):
        return [jnp.stack([re[:, i].reshape(bsz, 2, -1), im[:, i].reshape(bsz, 2, -1)], axis=2).astype(F32)
                for i in range(n_even)]

    ngrp, npst = s5_a_re.shape[2], s5_a_re.shape[3]
    nh, dh = state_mlstm_C.shape[3], state_mlstm_C.shape[4]
    zeros_s5 = jnp.zeros((nb, n_even, 2, ngrp, npst), F32)
    y_prompt, s5_fin, ml_fin = _trunk(
        x_prompt.reshape(nb * seq, d), mods_ctx, s5_state(zeros_s5, zeros_s5, nb),
        [jnp.zeros((nb, 2, nh, dh, dh), F32)] * n_odd, [jnp.zeros((nb, 2, nh, dh), F32)] * n_odd,
        [jnp.zeros((nb, 2, nh, 1), F32)] * n_odd, p, nb, seq, nb * seq)
    x_lat = add_pos(x_sample.reshape(db * dseq, d), _pos_embed(dseq, d, GRID_W), dseq, _tile(dseq, ROW_TILE))
    y_sample, _, _ = _trunk(
        x_lat, mods_lat, s5_state(state_s5_re, state_s5_im, db),
        [state_mlstm_C[:, i].astype(F32) for i in range(n_odd)], [state_mlstm_n[:, i].astype(F32) for i in range(n_odd)],
        [state_mlstm_m[:, i].astype(F32)[..., None] for i in range(n_odd)], p, db, dseq, dseq)

    new_s5_re = jnp.stack([h[:, :, 0].reshape(nb, 2, ngrp, npst) for h in s5_fin], axis=1)
    new_s5_im = jnp.stack([h[:, :, 1].reshape(nb, 2, ngrp, npst) for h in s5_fin], axis=1)
    new_c = jnp.stack([f[0] for f in ml_fin], axis=1)
    new_n = jnp.stack([f[1] for f in ml_fin], axis=1)
    new_m = jnp.stack([f[2][..., 0] for f in ml_fin], axis=1)
    return (y_prompt.reshape(nb, seq, d), y_sample.reshape(db, dseq, d), new_s5_re, new_s5_im, new_c, new_n, new_m)
```

```python
import functools
import math

import jax
import jax.numpy as jnp
from jax import lax
from jax.experimental import pallas as pl
from jax.experimental.pallas import tpu as pltpu

F32 = jnp.float32
BF16 = jnp.bfloat16
EPS = 1e-6
HIGHEST = lax.Precision.HIGHEST
V7X_VMEM_LIMIT_BYTES = 56 * 1024 * 1024
LANES = 128
SUBLANES = 8
ML_CHUNK = 128
PK_TOPK = 16
GRID_W = 64
NEG_INF = float("-inf")
ROW_TILE = 1024
DFT_TILE = 512
ADA_COLS = 1536
CONV_COLS = 256
S5_CHUNK = 256
PEER_SCORE_TOKENS = 256
PEER_TOKENS = 512
PEER_EXPERTS = 2048


def _cp(*sem):
    return pltpu.CompilerParams(dimension_semantics=sem, vmem_limit_bytes=V7X_VMEM_LIMIT_BYTES)


def _dot(a, b, **kw):
    return jnp.dot(a, b, preferred_element_type=F32, **kw)


def _dot_nt(a, b):
    return lax.dot_general(a, b, (((1,), (1,)), ((), ())), preferred_element_type=F32)


def _silu(x):
    return x * jax.nn.sigmoid(x)


def _ada_kernel(c_ref, w_ref, b_ref, o_ref):
    o_ref[...] = _dot(_silu(c_ref[...]), w_ref[...], precision=HIGHEST) + b_ref[...]


def ada_mod(cond8, w, b):
    d, no = w.shape
    tn = _tile(no, ADA_COLS)
    return pl.pallas_call(
        _ada_kernel, grid=(no // tn,),
        in_specs=[pl.BlockSpec((8, d), lambda j: (0, 0)), pl.BlockSpec((d, tn), lambda j: (0, j)),
                  pl.BlockSpec((1, tn), lambda j: (0, j))],
        out_specs=pl.BlockSpec((8, tn), lambda j: (0, j)),
        out_shape=jax.ShapeDtypeStruct((8, no), F32), compiler_params=_cp("parallel"), name="ada_mod",
    )(cond8, w, b.reshape(1, no))


def _normmod(x, g, sc, sh):
    y = x * lax.rsqrt(jnp.mean(x * x, axis=-1, keepdims=True) + EPS)
    return (y * g) * (1.0 + sc) + sh


def _mod_spec(d, tm, rows_per_mod):
    return pl.BlockSpec((1, 1, d), lambda i, j: ((i * tm) // rows_per_mod, 0, 0))


def _nm_matmul_kernel(x_ref, g_ref, sc_ref, sh_ref, w_ref, o_ref, h_ref):
    @pl.when(pl.program_id(1) == 0)
    def _():
        h_ref[...] = _normmod(x_ref[...], g_ref[...], sc_ref[0], sh_ref[0]).astype(BF16)
    o_ref[...] = _dot(h_ref[...], w_ref[...])


def normmod_matmul(x, g, sc, sh, w_bf16, rows_per_mod, tm, tn):
    n, d = x.shape
    no = w_bf16.shape[1]
    return pl.pallas_call(
        _nm_matmul_kernel, grid=(n // tm, no // tn),
        in_specs=[pl.BlockSpec((tm, d), lambda i, j: (i, 0)), pl.BlockSpec((1, d), lambda i, j: (0, 0)),
                  _mod_spec(d, tm, rows_per_mod), _mod_spec(d, tm, rows_per_mod),
                  pl.BlockSpec((d, tn), lambda i, j: (0, j))],
        out_specs=pl.BlockSpec((tm, tn), lambda i, j: (i, j)),
        out_shape=jax.ShapeDtypeStruct((n, no), F32),
        scratch_shapes=[pltpu.VMEM((tm, d), BF16)],
        compiler_params=_cp("parallel", "arbitrary"), name="normmod_matmul",
    )(x, g.reshape(1, d), sc, sh, w_bf16)


def _add_rows_kernel(x_ref, p_ref, o_ref):
    o_ref[...] = x_ref[...] + p_ref[...]


def add_pos(x, pe, seq_len, tm):
    n, d = x.shape
    nb = seq_len // tm
    return pl.pallas_call(
        _add_rows_kernel, grid=(n // tm,),
        in_specs=[pl.BlockSpec((tm, d), lambda i: (i, 0)), pl.BlockSpec((tm, d), lambda i: (i % nb, 0))],
        out_specs=pl.BlockSpec((tm, d), lambda i: (i, 0)),
        out_shape=jax.ShapeDtypeStruct((n, d), F32), compiler_params=_cp("parallel"), name="add_pos",
    )(x, pe)


def _sconv_kernel(x_ref, w_ref, b_ref, s_ref, o_ref, *, act):
    x = x_ref[...]
    n_tok = x.shape[0]
    row = lax.broadcasted_iota(jnp.int32, x.shape, 0)
    prev = jnp.where(row == 0, 0.0, pltpu.roll(x, 1, 0))
    nxt = jnp.where(row == n_tok - 1, 0.0, pltpu.roll(x, n_tok - 1, 0))
    y = prev * w_ref[0:1, :] + x * w_ref[1:2, :] + nxt * w_ref[2:3, :] + b_ref[...]
    if act:
        y = _silu(y) * s_ref[...]
    o_ref[...] = y.astype(o_ref.dtype)


def short_conv(a, ncols, w, b, scale, seq_len, act, out_dtype):
    n = a.shape[0]
    cb = _tile(ncols, CONV_COLS)
    return pl.pallas_call(
        functools.partial(_sconv_kernel, act=act), grid=(n // seq_len, ncols // cb),
        in_specs=[pl.BlockSpec((seq_len, cb), lambda s, j: (s, j)), pl.BlockSpec((3, cb), lambda s, j: (0, j)),
                  pl.BlockSpec((1, cb), lambda s, j: (0, j)), pl.BlockSpec((1, cb), lambda s, j: (0, j))],
        out_specs=pl.BlockSpec((seq_len, cb), lambda s, j: (s, j)),
        out_shape=jax.ShapeDtypeStruct((n, ncols), out_dtype), compiler_params=_cp("parallel", "parallel"),
        name="short_conv",
    )(a, w, b.reshape(1, ncols), scale.reshape(1, ncols))


def dft_tables(n_tok):
    k = jnp.arange(n_tok, dtype=jnp.int32)
    blk = 1 << ((n_tok.bit_length() - 1) // 2)
    def thin(n):
        ang = ((k[:, None] * n[None, :]) % (2 * n_tok)).astype(F32) * (math.pi / n_tok)
        return jnp.cos(ang), jnp.sin(ang)
    (c_hi, s_hi), (c_lo, s_lo) = thin(jnp.arange(0, n_tok, blk, dtype=jnp.int32)), thin(jnp.arange(blk, dtype=jnp.int32))
    cos_t = (c_hi[:, :, None] * c_lo[:, None, :] - s_hi[:, :, None] * s_lo[:, None, :]).reshape(n_tok, n_tok)
    msin = -(s_hi[:, :, None] * c_lo[:, None, :] + c_hi[:, :, None] * s_lo[:, None, :]).reshape(n_tok, n_tok)
    alt = jnp.where(k % 2 == 0, 1.0, -1.0).astype(F32)
    a_t = msin.at[0, :].set(alt)
    a_tt = msin.at[:, 0].set(alt)
    return cos_t.astype(BF16), a_t.astype(BF16), a_tt.astype(BF16)


def _hyfilt_kernel(band_ref, w1_ref, b1_ref, w2_ref, b2_ref, w3_ref, fr_ref, dec_ref, h_ref, ss_ref, *,
                   n_tok, tl, hw, nbands):
    i = pl.program_id(0)
    pos = i * tl + lax.broadcasted_iota(jnp.int32, (tl, 1), 0)
    t = pos.astype(F32) / n_tok
    lane = lax.broadcasted_iota(jnp.int32, (tl, LANES), 1)
    ang = 2.0 * math.pi * t * band_ref[...]
    z = jnp.where(lane == 0, t, jnp.where(lane <= nbands, jnp.cos(ang),
                                          jnp.where(lane <= 2 * nbands, jnp.sin(ang), 0.0)))
    fr = fr_ref[...]
    h = jnp.sin(fr * (_dot(z, w1_ref[...], precision=HIGHEST) + b1_ref[...]))
    h = jnp.sin(fr * (_dot(h, w2_ref[...], precision=HIGHEST) + b2_ref[...]))
    h = _dot(h, w3_ref[...], precision=HIGHEST) * jnp.exp(-t * jnp.abs(dec_ref[...]))
    col = lax.broadcasted_iota(jnp.int32, h.shape, 1)
    is_bwd = (col // hw) % 2 == 1
    h = jnp.where(jnp.logical_and(is_bwd, pos == 0), 0.0, h)
    h_ref[...] = h.astype(BF16)

    @pl.when(i == 0)
    def _():
        ss_ref[...] = jnp.zeros_like(ss_ref)
    ss_ref[...] += jnp.sum(h * h, axis=0, keepdims=True)


def hyena_filter_taps(n_tok, w1, b1, w2, b2, w3, freq, decay, hw):
    emb, ffn = w1.shape
    nbands = (emb - 1) // 2
    tl = _tile(n_tok, DFT_TILE)
    bands = jnp.linspace(1e-4, nbands - 1, nbands, dtype=F32)
    band_row = jnp.zeros((1, LANES), F32).at[0, 1:1 + nbands].set(bands).at[0, 1 + nbands:1 + 2 * nbands].set(bands)
    w1p = jnp.zeros((LANES, ffn), F32).at[:emb].set(w1)
    nc = w3.shape[1]
    full = lambda shp: pl.BlockSpec(shp, lambda i: (0, 0))
    return pl.pallas_call(
        functools.partial(_hyfilt_kernel, n_tok=n_tok, tl=tl, hw=hw, nbands=nbands), grid=(n_tok // tl,),
        in_specs=[full((1, LANES)), full((LANES, ffn)), full((1, ffn)), full((ffn, ffn)), full((1, ffn)),
                  full((ffn, nc)), full((1, ffn)), full((1, nc))],
        out_specs=[pl.BlockSpec((tl, nc), lambda i: (i, 0)), full((1, nc))],
        out_shape=[jax.ShapeDtypeStruct((n_tok, nc), BF16), jax.ShapeDtypeStruct((1, nc), F32)],
        compiler_params=_cp("arbitrary"), name="hyena_filter_taps",
    )(band_row, w1p, b1.reshape(1, ffn), w2, b2.reshape(1, ffn), w3, freq.reshape(1, ffn), decay.reshape(1, nc))


def _filt_dft_kernel(c_ref, a_ref, h_ref, ss_ref, kr_ref, ki_ref, *, tf, hw):
    i = pl.program_id(1)
    hf = h_ref[:, :hw]
    hb = h_ref[:, hw:]
    cc = c_ref[...]
    aa = a_ref[...]
    zrf, zif, zrb, zib = _dot(cc, hf), _dot(aa, hf), _dot(cc, hb), _dot(aa, hb)
    scale = lax.rsqrt(ss_ref[:, :hw] + ss_ref[:, hw:] + EPS)
    first = (i * tf + lax.broadcasted_iota(jnp.int32, (tf, 1), 0)) == 0
    scale = scale * jnp.where(first, 0.5, 1.0)
    kr_ref[0] = (zrf + zrb) * scale
    ki_ref[0] = jnp.where(first, zif + zib, zif - zib) * scale


def hyena_filter_spectrum(cos_t, a_t, taps, sumsq, hw, tf):
    n_tok = cos_t.shape[0]
    norder = taps.shape[1] // (2 * hw)
    out = jax.ShapeDtypeStruct((norder, n_tok, hw), F32)
    return pl.pallas_call(
        functools.partial(_filt_dft_kernel, tf=tf, hw=hw), grid=(norder, n_tok // tf),
        in_specs=[pl.BlockSpec((tf, n_tok), lambda o, i: (i, 0)), pl.BlockSpec((tf, n_tok), lambda o, i: (i, 0)),
                  pl.BlockSpec((n_tok, 2 * hw), lambda o, i: (0, o)), pl.BlockSpec((1, 2 * hw), lambda o, i: (0, o))],
        out_specs=[pl.BlockSpec((1, tf, hw), lambda o, i: (o, i, 0))] * 2,
        out_shape=[out, out], compiler_params=_cp("parallel", "parallel"), name="hyena_filter_spectrum",
    )(cos_t, a_t, taps, sumsq)


def _hy_fwd_kernel(c_ref, a_ref, z_ref, kr_ref, ki_ref, yr_ref, yi_ref, *, tf):
    i = pl.program_id(0)
    zb = z_ref[...].astype(BF16)
    zr = _dot(c_ref[...], zb)
    zi = _dot(a_ref[...], zb)
    kr = kr_ref[0]
    ki = ki_ref[0]
    first = (i * tf + lax.broadcasted_iota(jnp.int32, (tf, 1), 0)) == 0
    yr_ref[...] = jnp.where(first, zr * kr, zr * kr - zi * ki).astype(BF16)
    yi_ref[...] = jnp.where(first, zi * ki, zr * ki + zi * kr).astype(BF16)


def hyena_fwd(cos_t, a_t, z, zcol, kr, ki, order, nseq, hw, tf):
    n_tok = cos_t.shape[0]
    nf = n_tok // tf
    out = jax.ShapeDtypeStruct((nseq * n_tok, hw), BF16)
    return pl.pallas_call(
        functools.partial(_hy_fwd_kernel, tf=tf), grid=(nf, nseq),
        in_specs=[pl.BlockSpec((tf, n_tok), lambda i, b: (i, 0)), pl.BlockSpec((tf, n_tok), lambda i, b: (i, 0)),
                  pl.BlockSpec((n_tok, hw), lambda i, b: (b, zcol)),
                  pl.BlockSpec((1, tf, hw), lambda i, b: (order, i, 0)),
                  pl.BlockSpec((1, tf, hw), lambda i, b: (order, i, 0))],
        out_specs=[pl.BlockSpec((tf, hw), lambda i, b: (b * nf + i, 0))] * 2,
        out_shape=[out, out], compiler_params=_cp("parallel", "parallel"), name="hyena_fwd",
    )(cos_t, a_t, z, kr, ki)


def _hy_inv_kernel(c_ref, at_ref, yr_ref, yi_ref, zp_ref, gate_ref, bias_ref, o_ref, *, inv_len):
    conv = (_dot(c_ref[...], yr_ref[...]) + _dot(at_ref[...], yi_ref[...])) * inv_len
    o_ref[...] = gate_ref[...] * (conv + bias_ref[...] * zp_ref[...])


def hyena_inv(cos_t, a_tt, yr, yi, zprev, zcol, gates, gcol, bias_row, nseq, hw, tf):
    n_tok = cos_t.shape[0]
    nf = n_tok // tf
    return pl.pallas_call(
        functools.partial(_hy_inv_kernel, inv_len=1.0 / n_tok), grid=(nf, nseq),
        in_specs=[pl.BlockSpec((tf, n_tok), lambda i, b: (i, 0)), pl.BlockSpec((tf, n_tok), lambda i, b: (i, 0)),
                  pl.BlockSpec((n_tok, hw), lambda i, b: (b, 0)), pl.BlockSpec((n_tok, hw), lambda i, b: (b, 0)),
                  pl.BlockSpec((tf, hw), lambda i, b: (b * nf + i, zcol)),
                  pl.BlockSpec((tf, hw), lambda i, b: (b * nf + i, gcol)),
                  pl.BlockSpec((1, hw), lambda i, b: (0, 0))],
        out_specs=pl.BlockSpec((tf, hw), lambda i, b: (b * nf + i, 0)),
        out_shape=jax.ShapeDtypeStruct((nseq * n_tok, hw), F32),
        compiler_params=_cp("parallel", "parallel"), name="hyena_inv",
    )(cos_t, a_tt, yr, yi, zprev, gates, bias_row)


S5_DIAG_BLOCKS = 2


S5_SEQS_PER_STEP = 4


def _s5_kernel(u_ref, bre_ref, bim_ref, cre_ref, cim_ref, lam_ref, h0_ref, y_ref, hfin_ref, hre_s, him_s, st_s, *,
               tc, nc, ns, nb):
    d = pl.program_id(0)
    c = pl.program_id(2)

    @pl.when(c == 0)
    def _():
        st_s[...] = h0_ref[:, 0]

    sw = u_ref.shape[2]
    halves = [(slice(j * sw // S5_DIAG_BLOCKS, (j + 1) * sw // S5_DIAG_BLOCKS),
               slice(j * ns // S5_DIAG_BLOCKS, (j + 1) * ns // S5_DIAG_BLOCKS)) for j in range(S5_DIAG_BLOCKS)]
    for j in range(nb):
        ub = u_ref[j].astype(BF16)
        for us, hs in halves:
            hre_s[j, :, hs] = _dot(ub[:, us], bre_ref[0, us, hs])
            him_s[j, :, hs] = _dot(ub[:, us], bim_ref[0, us, hs])
    lr = lam_ref[0, 0:1, :]
    li = lam_ref[0, 1:2, :]

    def body(t, carry):
        r = jnp.where(d == 0, t, tc - 1 - t)
        new = []
        for j, (hr, hi) in enumerate(carry):
            nr = lr * hr - li * hi + hre_s[j, pl.ds(r, 1), :]
            ni = lr * hi + li * hr + him_s[j, pl.ds(r, 1), :]
            hre_s[j, pl.ds(r, 1), :] = nr
            him_s[j, pl.ds(r, 1), :] = ni
            new.append((nr, ni))
        return tuple(new)

    start = tuple((st_s[j, 0:1, :], st_s[j, 1:2, :]) for j in range(nb))
    for j, (hr, hi) in enumerate(lax.fori_loop(0, tc, body, start, unroll=4)):
        st_s[j, 0:1, :] = hr
        st_s[j, 1:2, :] = hi
    for j in range(nb):
        for us, hs in halves:
            y_ref[0, j, :, us] = (_dot(hre_s[j, :, hs].astype(BF16), cre_ref[0, hs, us])
                                  + _dot(him_s[j, :, hs].astype(BF16), cim_ref[0, hs, us]))

    @pl.when(c == nc - 1)
    def _():
        hfin_ref[:, 0] = st_s[...]


def s5_scan(proj, ucol, bre, bim, cre, cim, lam, h0, nseq, seq_len, sw, tc):
    ns = bre.shape[2]
    nc = seq_len // tc
    nb = S5_SEQS_PER_STEP if nseq % S5_SEQS_PER_STEP == 0 else 1

    def chunk(d, c):
        return c + d * (nc - 1 - 2 * c)

    y, hfin = pl.pallas_call(
        functools.partial(_s5_kernel, tc=tc, nc=nc, ns=ns, nb=nb), grid=(2, nseq // nb, nc),
        in_specs=[pl.BlockSpec((nb, tc, sw), lambda d, b, c: (b, chunk(d, c), ucol)),
                  pl.BlockSpec((1, sw, ns), lambda d, b, c: (d, 0, 0)),
                  pl.BlockSpec((1, sw, ns), lambda d, b, c: (d, 0, 0)),
                  pl.BlockSpec((1, ns, sw), lambda d, b, c: (d, 0, 0)),
                  pl.BlockSpec((1, ns, sw), lambda d, b, c: (d, 0, 0)),
                  pl.BlockSpec((1, 2, ns), lambda d, b, c: (d, 0, 0)),
                  pl.BlockSpec((nb, 1, 2, ns), lambda d, b, c: (b, d, 0, 0))],
        out_specs=[pl.BlockSpec((1, nb, tc, sw), lambda d, b, c: (d, b, chunk(d, c), 0)),
                   pl.BlockSpec((nb, 1, 2, ns), lambda d, b, c: (b, d, 0, 0))],
        out_shape=[jax.ShapeDtypeStruct((2, nseq, seq_len, sw), F32), jax.ShapeDtypeStruct((nseq, 2, 2, ns), F32)],
        scratch_shapes=[pltpu.VMEM((nb, tc, ns), F32), pltpu.VMEM((nb, tc, ns), F32), pltpu.VMEM((nb, 2, ns), F32)],
        compiler_params=_cp("parallel", "parallel", "arbitrary"), name="s5_scan",
    )(proj.reshape(nseq, seq_len, proj.shape[1]), bre, bim, cre, cim, lam, h0)
    return y.reshape(2, nseq * seq_len, sw), hfin


def _s5_glu_kernel(yf_ref, yb_ref, u_ref, d_ref, w_ref, b_ref, o_ref):
    y = jax.nn.gelu(yf_ref[0] + yb_ref[0] + d_ref[...] * u_ref[...])
    o_ref[...] = y * jax.nn.sigmoid(_dot(y.astype(BF16), w_ref[...]) + b_ref[...])


def s5_glu(y2, proj, ucol, d_skip, glu_w_bf16, glu_b, tm):
    _, n, sw = y2.shape
    return pl.pallas_call(
        _s5_glu_kernel, grid=(n // tm,),
        in_specs=[pl.BlockSpec((1, tm, sw), lambda i: (0, i, 0)), pl.BlockSpec((1, tm, sw), lambda i: (1, i, 0)),
                  pl.BlockSpec((tm, sw), lambda i: (i, ucol)), pl.BlockSpec((1, sw), lambda i: (0, 0)),
                  pl.BlockSpec((sw, sw), lambda i: (0, 0)), pl.BlockSpec((1, sw), lambda i: (0, 0))],
        out_specs=pl.BlockSpec((tm, sw), lambda i: (i, 0)),
        out_shape=jax.ShapeDtypeStruct((n, sw), F32), compiler_params=_cp("parallel"), name="s5_glu",
    )(y2, y2, proj, d_skip.reshape(1, sw), glu_w_bf16, glu_b.reshape(1, sw))


def _residual_and_next_norm(x_ref, gate_ref, y, g2_ref, sc2_ref, sh2_ref, o_ref, hn_ref):
    xn = x_ref[...] + gate_ref[0] * y
    o_ref[...] = xn
    hn_ref[...] = _normmod(xn, g2_ref[...], sc2_ref[0], sh2_ref[0]).astype(BF16)


def _row_specs(d, tm, rows_per_mod):
    mod = pl.BlockSpec((1, 1, d), lambda i: ((i * tm) // rows_per_mod, 0, 0))
    return [pl.BlockSpec((tm, d), lambda i: (i, 0)), mod, pl.BlockSpec((1, d), lambda i: (0, 0)), mod, mod]


def _row_outs(n, d, tm):
    spec = pl.BlockSpec((tm, d), lambda i: (i, 0))
    return [spec, spec], [jax.ShapeDtypeStruct((n, d), F32), jax.ShapeDtypeStruct((n, d), BF16)]


def _even_out_kernel(a_ref, b_ref, wa_ref, wb_ref, x_ref, gate_ref, g2_ref, sc2_ref, sh2_ref, o_ref, hn_ref):
    y = _dot(a_ref[...].astype(BF16), wa_ref[...]) + _dot(b_ref[...].astype(BF16), wb_ref[...])
    _residual_and_next_norm(x_ref, gate_ref, y, g2_ref, sc2_ref, sh2_ref, o_ref, hn_ref)


def even_out(hy, s5o, w_bf16, x, gate, g2, sc2, sh2, rows_per_mod, tm):
    n, d = x.shape
    hw = hy.shape[1]
    sw = s5o.shape[1]
    out_specs, out_shape = _row_outs(n, d, tm)
    return pl.pallas_call(
        _even_out_kernel, grid=(n // tm,),
        in_specs=[pl.BlockSpec((tm, hw), lambda i: (i, 0)), pl.BlockSpec((tm, sw), lambda i: (i, 0)),
                  pl.BlockSpec((hw, d), lambda i: (0, 0)), pl.BlockSpec((sw, d), lambda i: (hw // sw, 0))]
        + _row_specs(d, tm, rows_per_mod),
        out_specs=out_specs, out_shape=out_shape, compiler_params=_cp("parallel"), name="even_out",
    )(hy, s5o, w_bf16, w_bf16, x, gate, g2.reshape(1, d), sc2, sh2)


ML_HEAD_GROUP = 8


def _log_sigmoid(x):
    return jnp.minimum(x, 0.0) - jnp.log1p(jnp.exp(-jnp.abs(x)))


def _mlstm_kernel(q_ref, k_ref, v_ref, g_ref, gb_ref, c0_ref, n0_ref, m0_ref, h_ref, cf_ref, nf_ref, mf_ref,
                  c_s, m_s, *, nh, dh, tc, nc):
    d = pl.program_id(0)
    c = pl.program_id(2)

    @pl.when(c == 0)
    def _():
        for h in range(nh):
            c_s[h, :, :dh] = c0_ref[0, 0, h]
            c_s[h, :, dh:] = jnp.broadcast_to(n0_ref[0, 0, h:h + 1, :], (dh, dh)).T
        m_s[...] = m0_ref[0, 0]

    ones = jnp.ones((tc, dh), F32)
    gates = g_ref[...] + gb_ref[0]
    lane = lax.broadcasted_iota(jnp.int32, gates.shape, 1)
    logf = jnp.where(jnp.logical_and(lane >= nh, lane < 2 * nh), _log_sigmoid(gates), 0.0)
    r_i = lax.broadcasted_iota(jnp.int32, (tc, tc), 0)
    s_i = lax.broadcasted_iota(jnp.int32, (tc, tc), 1)
    causal = (r_i - s_i) * (1 - 2 * d) >= 0
    bcum = _dot(causal.astype(F32), logf, precision=HIGHEST)
    btot = jnp.sum(logf, axis=0, keepdims=True)
    gates_t = gates.T
    bcum_t = bcum.T
    for g0 in range(0, nh, ML_HEAD_GROUP):
        hds = list(range(g0, min(g0 + ML_HEAD_GROUP, nh)))
        hsl = {h: slice(h * dh, (h + 1) * dh) for h in hds}
        b_col = {h: bcum[:, nh + h:nh + h + 1] for h in hds}
        m_old = {h: m_s[h:h + 1, :] for h in hds}
        a = {h: b_col[h] + m_old[h] for h in hds}
        src = {h: jnp.where(causal, gates_t[h:h + 1, :] - bcum_t[nh + h:nh + h + 1, :], NEG_INF) for h in hds}
        mq = {h: jnp.maximum(a[h], b_col[h] + jnp.max(src[h], axis=-1, keepdims=True)) for h in hds}
        rel = {h: jnp.broadcast_to(b_col[h] - mq[h], (tc, tc)) for h in hds}
        s = {h: _dot_nt(q_ref[:, hsl[h]], k_ref[:, hsl[h]]) * jnp.exp(src[h] + rel[h]) for h in hds}
        v1 = {h: jnp.concatenate([v_ref[:, hsl[h]], ones], axis=1) for h in hds}
        cn = {h: c_s[h] for h in hds}
        qw = {h: jnp.exp(rel[h][:, :dh] + m_old[h]) * q_ref[:, hsl[h]] for h in hds}
        both = {h: _dot(s[h], v1[h]) + _dot(qw[h], cn[h]) for h in hds}
        for h in hds:
            h_ref[0, :, hsl[h]] = both[h][:, :dh] / jnp.maximum(jnp.abs(both[h][:, dh:]), jnp.exp(-mq[h]))
        b_last = {h: btot[:, nh + h:nh + h + 1] for h in hds}
        g = {h: b_last[h] - bcum_t[nh + h:nh + h + 1, :] + gates_t[h:h + 1, :] for h in hds}
        m_new = {h: jnp.maximum(b_last[h] + m_old[h], jnp.max(g[h], axis=1, keepdims=True)) for h in hds}
        kw_t = {h: k_ref[:, hsl[h]].astype(F32).T * jnp.exp(g[h] - m_new[h]) for h in hds}
        for h in hds:
            c_s[h] = jnp.exp(b_last[h] + m_old[h] - m_new[h]) * cn[h] + _dot(kw_t[h], v1[h])
            m_s[h:h + 1, :] = m_new[h]

    @pl.when(c == nc - 1)
    def _():
        for h in range(nh):
            cf_ref[0, 0, h] = c_s[h, :, :dh]
            nf_ref[0, 0, h:h + 1, :] = c_s[h, :, dh:].T[0:1, :]
        mf_ref[0, 0] = m_s[...]


def mlstm_scan(qk, proj, vcol, gate_bias, c0, n0, m0, nseq, seq_len, nh, dh):
    tc = ML_CHUNK
    nc = seq_len // tc
    w = nh * dh

    def chunk(d, c):
        return c + d * (nc - 1 - 2 * c)

    rowblk = lambda d, b, c: b * nc + chunk(d, c)
    st = lambda shp: pl.BlockSpec((1, 1) + shp, lambda d, b, c: (b, d) + (0,) * len(shp))
    return pl.pallas_call(
        functools.partial(_mlstm_kernel, nh=nh, dh=dh, tc=tc, nc=nc), grid=(2, nseq, nc),
        in_specs=[pl.BlockSpec((tc, w), lambda d, b, c: (rowblk(d, b, c), 0)),
                  pl.BlockSpec((tc, w), lambda d, b, c: (rowblk(d, b, c), 1)),
                  pl.BlockSpec((tc, w), lambda d, b, c: (rowblk(d, b, c), vcol)),
                  pl.BlockSpec((tc, LANES), lambda d, b, c: (rowblk(d, b, c), 4 * w // LANES + d)),
                  pl.BlockSpec((1, 1, LANES), lambda d, b, c: (d, 0, 0)),
                  st((nh, dh, dh)), st((nh, dh)), st((nh, 1))],
        out_specs=[pl.BlockSpec((1, tc, w), lambda d, b, c: (d, rowblk(d, b, c), 0)),
                   st((nh, dh, dh)), st((nh, dh)), st((nh, 1))],
        out_shape=[jax.ShapeDtypeStruct((2, nseq * seq_len, w), F32),
                   jax.ShapeDtypeStruct((nseq, 2, nh, dh, dh), F32), jax.ShapeDtypeStruct((nseq, 2, nh, dh), F32),
                   jax.ShapeDtypeStruct((nseq, 2, nh, 1), F32)],
        scratch_shapes=[pltpu.VMEM((nh, dh, 2 * dh), F32), pltpu.VMEM((nh, 1), F32)],
        compiler_params=_cp("parallel", "parallel", "arbitrary"), name="mlstm_scan",
    )(qk, qk, proj, proj, gate_bias, c0, n0, m0)


def _odd_out_kernel(hf_ref, hb_ref, og_ref, ng_ref, w_ref, x_ref, gate_ref, g2_ref, sc2_ref, sh2_ref, o_ref, hn_ref,
                    a_s, *, nh, dh):
    for h in range(nh):
        hs = slice(h * dh, (h + 1) * dh)
        blk = hf_ref[0, :, hs] + hb_ref[0, :, hs]
        blk = blk * lax.rsqrt(jnp.mean(blk * blk, axis=-1, keepdims=True) + EPS)
        a_s[:, hs] = ((blk * ng_ref[:, hs]) * _silu(og_ref[:, hs])).astype(BF16)
    _residual_and_next_norm(x_ref, gate_ref, _dot(a_s[...], w_ref[...]), g2_ref, sc2_ref, sh2_ref, o_ref, hn_ref)


def odd_out(h2, proj, ocol, norm_g, w_bf16, x, gate, g2, sc2, sh2, rows_per_mod, nh, dh, tm):
    n, d = x.shape
    w = nh * dh
    out_specs, out_shape = _row_outs(n, d, tm)
    return pl.pallas_call(
        functools.partial(_odd_out_kernel, nh=nh, dh=dh), grid=(n // tm,),
        in_specs=[pl.BlockSpec((1, tm, w), lambda i: (0, i, 0)), pl.BlockSpec((1, tm, w), lambda i: (1, i, 0)),
                  pl.BlockSpec((tm, w), lambda i: (i, ocol)), pl.BlockSpec((1, w), lambda i: (0, 0)),
                  pl.BlockSpec((w, d), lambda i: (0, 0))] + _row_specs(d, tm, rows_per_mod),
        out_specs=out_specs, out_shape=out_shape, scratch_shapes=[pltpu.VMEM((tm, w), BF16)],
        compiler_params=_cp("parallel"), name="odd_out",
    )(h2, h2, proj, norm_g.reshape(1, w), w_bf16, x, gate, g2.reshape(1, d), sc2, sh2)


def _sort_network(n):
    pairs, p = [], 1
    while p < n:
        k = p
        while k >= 1:
            for j in range(k % p, n - k, 2 * k):
                for i in range(min(k, n - j - k)):
                    if (i + j) // (2 * p) == (i + j + k) // (2 * p):
                        pairs.append((i + j, i + j + k))
            k //= 2
        p *= 2
    return pairs


def _top_values_tiled(arrays, k, outs):
    cols = []
    for arr in arrays:
        tiles = [arr[r:r + SUBLANES, :] for r in range(0, arr.shape[0], SUBLANES)]
        for lo, hi in _sort_network(len(tiles)):
            tiles[lo], tiles[hi] = jnp.maximum(tiles[lo], tiles[hi]), jnp.minimum(tiles[lo], tiles[hi])
        cols.append(tiles)
    for j in range(k):
        for tiles, out_s in zip(cols, outs):
            m = jnp.max(tiles[0], axis=0, keepdims=True)
            out_s[j:j + 1, :] = m
            hit = tiles[0] == m
            for i in range(min(len(tiles), k - 1 - j)):
                below = tiles[i + 1] if i + 1 < len(tiles) else NEG_INF
                tiles[i] = jnp.where(hit, below, tiles[i])


def _pair_candidates(k):
    return [(a, k // (a + 1)) for a in range(k)]


PEER_HEADS_PER_TRIP = 8


def _peer_score_kernel(h_ref, wq_ref, keys_ref, th_ref, s2_ref, w1_ref, w2_ref, q_s, v_s, cand_s, best_s, *,
                       nh, half, topk):
    q_s[...] = _dot_nt(wq_ref[...], h_ref[...])
    kk = topk + 1
    cand_s[...] = jnp.full(cand_s.shape, NEG_INF, F32)
    group = range(PEER_HEADS_PER_TRIP)

    def heads(trip, carry):
        hds = [trip * PEER_HEADS_PER_TRIP + u for u in group]
        scores = []
        for hd in hds:
            base = pl.multiple_of(hd * 2 * half, 2 * half)
            scores.append(_dot(keys_ref[hd, 0], q_s[pl.ds(base, half), :]))
            scores.append(_dot(keys_ref[hd, 1], q_s[pl.ds(base + half, half), :]))
        _top_values_tiled(scores, kk, [v_s.at[u, c] for u in group for c in range(2)])
        for u in group:
            off = 0
            for a, cnt in _pair_candidates(kk):
                cand_s[u, off:off + cnt, :] = v_s[u, 0, a:a + 1, :] + v_s[u, 1, 0:cnt, :]
                off += cnt
        _top_values_tiled([cand_s[u] for u in group], kk, [best_s.at[u] for u in group])
        for u, hd in enumerate(hds):
            s1, s2 = scores[2 * u], scores[2 * u + 1]
            best = best_s[u, 0:topk, :]
            z = jnp.sum(jnp.exp(best - best[0:1, :]), axis=0, keepdims=True)
            tmid = 0.5 * (best_s[u, topk - 1:topk, :] + best_s[u, topk:topk + 1, :])
            th = tmid - s1
            w1 = jnp.exp(s1 - v_s[u, 0, 0:1, :]) / z
            w2 = jnp.exp(s2 - v_s[u, 1, 0:1, :])
            for lt in range(s1.shape[1] // LANES):
                sl = slice(lt * LANES, (lt + 1) * LANES)
                s2_ref[hd, lt] = s2[:, sl]
                w2_ref[hd, lt] = w2[:, sl]
                for c in range(th_ref.shape[1]):
                    rows = slice(c * th_ref.shape[4], (c + 1) * th_ref.shape[4])
                    th_ref[0, c, hd, lt] = th[rows, sl]
                    w1_ref[0, c, hd, lt] = w1[rows, sl]
        return carry

    lax.fori_loop(0, nh // PEER_HEADS_PER_TRIP, heads, 0)


def peer_scores(h_bf16, wq_t_bf16, keys, tt, tt_dense, ec):
    n, d = h_bf16.shape
    nh, _, nk, half = keys.shape
    nchunk, rpc = nk * nk // ec, ec // nk
    per_dense = tt_dense // tt
    first = jax.ShapeDtypeStruct((n // tt_dense, nchunk, nh, tt_dense // LANES, rpc, LANES), F32)
    fspec = pl.BlockSpec((1, nchunk, nh, tt // LANES, rpc, LANES), lambda i: (i // per_dense, 0, 0, i % per_dense, 0, 0))
    kk = PK_TOPK + 1
    top_rows = -(-kk // SUBLANES) * SUBLANES
    ncand = SUBLANES
    while ncand < sum(c for _, c in _pair_candidates(kk)):
        ncand *= 2
    big = jax.ShapeDtypeStruct((nh, n // LANES, nk, LANES), F32)
    bspec = pl.BlockSpec((nh, tt // LANES, nk, LANES), lambda i: (0, i, 0, 0))
    return pl.pallas_call(
        functools.partial(_peer_score_kernel, nh=nh, half=half, topk=PK_TOPK), grid=(n // tt,),
        in_specs=[pl.BlockSpec((tt, d), lambda i: (i, 0)), pl.BlockSpec((nh * 2 * half, d), lambda i: (0, 0)),
                  pl.BlockSpec((nh, 2, nk, half), lambda i: (0, 0, 0, 0))],
        out_specs=[fspec, bspec, fspec, bspec],
        out_shape=[first, big, first, big],
        scratch_shapes=[pltpu.VMEM((nh * 2 * half, tt), F32), pltpu.VMEM((PEER_HEADS_PER_TRIP, 2, top_rows, tt), F32),
                        pltpu.VMEM((PEER_HEADS_PER_TRIP, ncand, tt), F32),
                        pltpu.VMEM((PEER_HEADS_PER_TRIP, top_rows, tt), F32)],
        compiler_params=_cp("parallel"), name="peer_scores",
    )(h_bf16, wq_t_bf16, keys)


PEER_KEY_ROWS = 16
GELU_C1 = math.sqrt(2.0 / math.pi)
GELU_C2 = 0.044715 * GELU_C1


def _gelu_tanh(x):
    half_x = 0.5 * x
    return half_x + half_x * jnp.tanh(x * (GELU_C1 + GELU_C2 * (x * x)))


def _peer_dense_kernel(h_ref, u_ref, vt_ref, th_ref, s2_ref, w1_ref, w2_ref, x_ref, gate_ref, fg_ref, o_ref,
                       acc_s, st_s, wt_s, *, nh, nk, ec, tt, final):
    e = pl.program_id(1)
    nlt = tt // LANES
    n_i1 = ec // nk
    nkt = nk // PEER_KEY_ROWS
    nsub = PEER_KEY_ROWS // SUBLANES

    @pl.when(e == 0)
    def _():
        acc_s[...] = jnp.zeros_like(acc_s)

    st = _gelu_tanh(_dot_nt(u_ref[...], h_ref[...]))
    for lt in range(nlt):
        st_s[lt] = st[:, lt * LANES:(lt + 1) * LANES]

    def tile(idx, carry):
        lt = idx // nkt
        k0 = (idx % nkt) * PEER_KEY_ROWS
        subs = [pl.ds(pl.multiple_of(k0 + j * SUBLANES, SUBLANES), SUBLANES) for j in range(nsub)]
        g = [[jnp.zeros((SUBLANES, LANES), F32) for _ in subs] for _ in range(n_i1)]
        for hd in range(nh):
            s2t = [s2_ref[hd, lt, sub, :] for sub in subs]
            w2t = [w2_ref[hd, lt, sub, :] for sub in subs]
            for li in range(n_i1):
                thb = jnp.broadcast_to(th_ref[0, 0, hd, lt, li:li + 1, :], (SUBLANES, LANES))
                w1b = jnp.broadcast_to(w1_ref[0, 0, hd, lt, li:li + 1, :], (SUBLANES, LANES))
                for j in range(nsub):
                    g[li][j] = g[li][j] + jnp.where(s2t[j] >= thb, w2t[j] * w1b, 0.0)
        for li in range(n_i1):
            rows = pl.ds(pl.multiple_of(li * nk + k0, PEER_KEY_ROWS), PEER_KEY_ROWS)
            wt_s[lt, rows, :] = (st_s[lt, rows, :] * jnp.concatenate(g[li], axis=0)).astype(BF16)
        return carry

    lax.fori_loop(0, nlt * nkt, tile, 0, unroll=2)
    wt = jnp.concatenate([wt_s[lt] for lt in range(nlt)], axis=1)
    acc_s[...] += _dot(vt_ref[0], wt)

    @pl.when(e == pl.num_programs(1) - 1)
    def _():
        xn = x_ref[...] + gate_ref[0] * acc_s[...].T
        if final:
            xn = (xn * lax.rsqrt(jnp.mean(xn * xn, axis=-1, keepdims=True) + EPS)) * fg_ref[...]
        o_ref[...] = xn


def peer_dense(h_bf16, u_bf16, vt_bf16, th, s2, w1, w2, x, gate, final_g, final, rows_per_mod, tt, ec):
    n, d = x.shape
    nh, _, nk, _ = s2.shape
    nchunk = u_bf16.shape[0] // ec
    bspec = pl.BlockSpec((nh, tt // LANES, nk, LANES), lambda i, e: (0, i, 0, 0))
    rspec = pl.BlockSpec((1, 1, nh, tt // LANES, ec // nk, LANES), lambda i, e: (i, e, 0, 0, 0, 0))
    tile_buf = (tt // LANES, ec, LANES)
    return pl.pallas_call(
        functools.partial(_peer_dense_kernel, nh=nh, nk=nk, ec=ec, tt=tt, final=final), grid=(n // tt, nchunk),
        in_specs=[pl.BlockSpec((tt, d), lambda i, e: (i, 0)), pl.BlockSpec((ec, d), lambda i, e: (e, 0)),
                  pl.BlockSpec((1, d, ec), lambda i, e: (e, 0, 0)), rspec, bspec, rspec, bspec,
                  pl.BlockSpec((tt, d), lambda i, e: (i, 0)),
                  pl.BlockSpec((1, 1, d), lambda i, e: ((i * tt) // rows_per_mod, 0, 0)),
                  pl.BlockSpec((1, d), lambda i, e: (0, 0))],
        out_specs=pl.BlockSpec((tt, d), lambda i, e: (i, 0)),
        out_shape=jax.ShapeDtypeStruct((n, d), F32),
        scratch_shapes=[pltpu.VMEM((d, tt), F32), pltpu.VMEM(tile_buf, F32), pltpu.VMEM(tile_buf, BF16)],
        compiler_params=_cp("parallel", "arbitrary"), name="peer_dense",
    )(h_bf16, u_bf16, vt_bf16, th, s2, w1, w2, x, gate, final_g.reshape(1, d))


def _s5_params(a_re, a_im, b_re, b_im, c_re, c_im, log_step):
    lam = lax.complex(a_re.astype(F32), a_im.astype(F32))
    lam_bar = jnp.exp(lam * jnp.exp(log_step.astype(F32))[..., None])
    b_bar = ((lam_bar - 1.0) / lam)[..., None] * lax.complex(b_re.astype(F32), b_im.astype(F32))
    ngrp, npst, nch = b_bar.shape[1:]
    eye = jnp.eye(ngrp, dtype=F32)

    def b_mat(part):
        return jnp.einsum("dgpj,gh->dgjhp", part, eye).reshape(2, ngrp * nch, ngrp * npst)

    def c_mat(part):
        return jnp.einsum("dgjp,gh->dgphj", part, eye).reshape(2, ngrp * npst, ngrp * nch)

    bre, bim = b_mat(b_bar.real).astype(BF16), b_mat(b_bar.imag).astype(BF16)
    cre, cim = c_mat(c_re.astype(F32)).astype(BF16), c_mat(-c_im.astype(F32)).astype(BF16)
    lam2 = jnp.stack([lam_bar.real.reshape(2, -1), lam_bar.imag.reshape(2, -1)], axis=1)
    return bre, bim, cre, cim, lam2


def _pos_embed(n_tok, d, grid_w):
    rows = n_tok // grid_w
    quarter = d // 4
    omega = 1.0 / (10000.0 ** (jnp.arange(quarter, dtype=F32) / quarter))

    def emb1d(pos):
        ang = pos.astype(F32)[:, None] * omega[None]
        return jnp.concatenate([jnp.sin(ang), jnp.cos(ang)], axis=-1)

    er = emb1d(jnp.arange(rows))
    ec = emb1d(jnp.arange(grid_w))
    half = d // 2
    pe = jnp.concatenate([jnp.broadcast_to(er[:, None], (rows, grid_w, half)),
                          jnp.broadcast_to(ec[None], (rows, grid_w, half))], axis=-1)
    return pe.reshape(rows * grid_w, d)


def _tile(n, pref):
    return pref if n % pref == 0 else n


def _trunk(x, mods, s5_h0, ml_c0, ml_n0, ml_m0, p, nseq, seq_len, rows_per_mod):
    n, d = x.shape
    tm = _tile(min(rows_per_mod, n), ROW_TILE)
    depth = p["norm_g"].shape[0]
    s5_fin, ml_fin = [], []
    for l in range(depth):
        sh1, sc1, g1, sh2, sc2, g2 = mods[l]
        i = l // 2
        if l % 2 == 0:
            hw = p["hy_bias"].shape[2]
            sw = p["s5_d"].shape[1]
            proj = normmod_matmul(x, p["norm_g"][l, 0], sc1, sh1, p["ev_w_in"][i].astype(BF16), rows_per_mod, tm,
                                  3 * hw + sw)
            hy_in = short_conv(proj, 3 * hw, p["hy_conv_w"][i], p["hy_conv_b"][i], jnp.ones((3 * hw,), F32),
                               seq_len, act=False, out_dtype=F32)
            cos_t, a_t, a_tt = dft_tables(seq_len)
            tf = _tile(seq_len, DFT_TILE)
            taps, sumsq = hyena_filter_taps(seq_len, p["hy_w1"][i], p["hy_b1"][i], p["hy_w2"][i], p["hy_b2"][i],
                                            p["hy_w3"][i], p["hy_freq"][i], p["hy_decay"][i], hw)
            kr, ki = hyena_filter_spectrum(cos_t, a_t, taps, sumsq, hw, tf)
            bias = p["hy_bias"][i].astype(F32)
            z, zcol = hy_in, 0
            for o in range(bias.shape[0]):
                yr, yi = hyena_fwd(cos_t, a_t, z, zcol, kr, ki, o, nseq, hw, tf)
                z = hyena_inv(cos_t, a_tt, yr, yi, z, zcol, hy_in, 1 + o, bias[o:o + 1], nseq, hw, tf)
                zcol = 0
            bre, bim, cre, cim, lam2 = _s5_params(p["s5_a_re"][i], p["s5_a_im"][i], p["s5_b_re"][i], p["s5_b_im"][i],
                                                  p["s5_c_re"][i], p["s5_c_im"][i], p["s5_log_step"][i])
            ucol = 3 * hw // sw
            y2, hfin = s5_scan(proj, ucol, bre, bim, cre, cim, lam2, s5_h0[i], nseq, seq_len, sw,
                               _tile(seq_len, S5_CHUNK))
            s5_fin.append(hfin)
            s5o = s5_glu(y2, proj, ucol, p["s5_d"][i], p["s5_glu_w"][i].astype(BF16), p["s5_glu_b"][i], tm)
            x, hn = even_out(z, s5o, p["ev_w_out"][i].astype(BF16), x, g1, p["norm_g"][l, 1], sc2, sh2, rows_per_mod, tm)
        else:
            nh = p["od_gate_b"].shape[2]
            w = p["ml_norm_g"].shape[1]
            dh = w // nh
            w_in = p["od_w_in"][i]
            wg = w_in[:, 4 * w:].reshape(d, 4, nh)
            gb = p["od_gate_b"][i].astype(F32)
            wg2 = jnp.zeros((d, 2, LANES), w_in.dtype)
            bg2 = jnp.zeros((2, 1, LANES), F32)
            for dr in range(2):
                wg2 = wg2.at[:, dr, :nh].set(wg[:, dr]).at[:, dr, nh:2 * nh].set(wg[:, 2 + dr])
                bg2 = bg2.at[dr, 0, :nh].set(gb[dr]).at[dr, 0, nh:2 * nh].set(gb[2 + dr])
            w_all = jnp.concatenate([w_in[:, :4 * w], wg2.reshape(d, 2 * LANES)], axis=1).astype(BF16)
            proj = normmod_matmul(x, p["norm_g"][l, 0], sc1, sh1, w_all, rows_per_mod, tm, w_all.shape[1] // 2)
            qscale = jnp.concatenate([jnp.full((w,), dh ** -0.5, F32), jnp.ones((w,), F32)])
            qk = short_conv(proj, 2 * w, p["ml_conv_w"][i], p["ml_conv_b"][i], qscale, seq_len, act=True, out_dtype=BF16)
            h2, cf, nf, mf = mlstm_scan(qk, proj, 2, bg2, ml_c0[i], ml_n0[i], ml_m0[i], nseq, seq_len, nh, dh)
            ml_fin.append((cf, nf, mf))
            x, hn = odd_out(h2, proj, 3, p["ml_norm_g"][i], p["od_w_out"][i].astype(BF16), x, g1, p["norm_g"][l, 1],
                            sc2, sh2, rows_per_mod, nh, dh, tm)
        tt = _tile(min(rows_per_mod, n), PEER_TOKENS)
        ec = _tile(p["pk_u"].shape[1], PEER_EXPERTS)
        th, s2, w1, w2 = peer_scores(hn, p["pk_w_q"][l].T.astype(BF16), p["pk_keys"][l].astype(F32),
                                     _tile(tt, PEER_SCORE_TOKENS), tt, ec)
        vt = p["pk_v"][l].astype(BF16).reshape(-1, ec, d).transpose(0, 2, 1)
        x = peer_dense(hn, p["pk_u"][l].astype(BF16), vt, th, s2, w1, w2, x, g2, p["final_g"], l == depth - 1,
                       rows_per_mod, tt, ec)
    return x, s5_fin, ml_fin


def kernel(x_prompt, x_sample, state_s5_re, state_s5_im, state_mlstm_C, state_mlstm_n, state_mlstm_m, c, c_ctx, norm_g, ada_w, ada_b, final_g, ev_w_in, hy_conv_w, hy_conv_b, hy_w1, hy_b1, hy_w2, hy_b2, hy_w3, hy_freq, hy_decay, hy_bias, s5_a_re, s5_a_im, s5_b_re, s5_b_im, s5_c_re, s5_c_im, s5_log_step, s5_d, s5_glu_w, s5_glu_b, ev_w_out, od_w_in, od_gate_b, ml_conv_w, ml_conv_b, ml_norm_g, od_w_out, pk_w_q, pk_keys, pk_u, pk_v):
    p = dict(norm_g=norm_g, ada_w=ada_w, ada_b=ada_b, final_g=final_g, ev_w_in=ev_w_in,
             hy_conv_w=hy_conv_w, hy_conv_b=hy_conv_b, hy_w1=hy_w1, hy_b1=hy_b1, hy_w2=hy_w2, hy_b2=hy_b2,
             hy_w3=hy_w3, hy_freq=hy_freq, hy_decay=hy_decay, hy_bias=hy_bias, s5_a_re=s5_a_re,
             s5_a_im=s5_a_im, s5_b_re=s5_b_re, s5_b_im=s5_b_im, s5_c_re=s5_c_re, s5_c_im=s5_c_im,
             s5_log_step=s5_log_step, s5_d=s5_d, s5_glu_w=s5_glu_w, s5_glu_b=s5_glu_b, ev_w_out=ev_w_out,
             od_w_in=od_w_in, od_gate_b=od_gate_b, ml_conv_w=ml_conv_w, ml_conv_b=ml_conv_b,
             ml_norm_g=ml_norm_g, od_w_out=od_w_out, pk_w_q=pk_w_q, pk_keys=pk_keys, pk_u=pk_u, pk_v=pk_v)
    nb, seq, d = x_prompt.shape
    db, dseq, _ = x_sample.shape
    depth = norm_g.shape[0]
    n_even, n_odd = (depth + 1) // 2, depth // 2
    assert db + 1 <= 8

    cond8 = jnp.zeros((8, d), F32).at[0].set(c_ctx.astype(F32)).at[1:1 + db].set(c.astype(F32))
    mods_ctx, mods_lat = [], []
    for l in range(depth):
        mod = ada_mod(cond8, ada_w[l].astype(F32), ada_b[l].astype(F32))
        chunks = [mod[:, j * d:(j + 1) * d] for j in range(6)]
        mods_ctx.append([ch[0:1].reshape(1, 1, d) for ch in chunks])
        mods_lat.append([ch[1:1 + db].reshape(db, 1, d) for ch in chunks])

    def s5_state(re, im, bsz):
        return [jnp.stack([re[:, i].reshape(bsz, 2, -1), im[:, i].reshape(bsz, 2, -1)], axis=2).astype(F32)
                for i in range(n_even)]

    ngrp, npst = s5_a_re.shape[2], s5_a_re.shape[3]
    nh, dh = state_mlstm_C.shape[3], state_mlstm_C.shape[4]
    zeros_s5 = jnp.zeros((nb, n_even, 2, ngrp, npst), F32)
    y_prompt, s5_fin, ml_fin = _trunk(
        x_prompt.reshape(nb * seq, d), mods_ctx, s5_state(zeros_s5, zeros_s5, nb),
        [jnp.zeros((nb, 2, nh, dh, dh), F32)] * n_odd, [jnp.zeros((nb, 2, nh, dh), F32)] * n_odd,
        [jnp.zeros((nb, 2, nh, 1), F32)] * n_odd, p, nb, seq, nb * seq)
    x_lat = add_pos(x_sample.reshape(db * dseq, d), _pos_embed(dseq, d, GRID_W), dseq, _tile(dseq, ROW_TILE))
    y_sample, _, _ = _trunk(
        x_lat, mods_lat, s5_state(state_s5_re, state_s5_im, db),
        [state_mlstm_C[:, i].astype(F32) for i in range(n_odd)], [state_mlstm_n[:, i].astype(F32) for i in range(n_odd)],
        [state_mlstm_m[:, i].astype(F32)[..., None] for i in range(n_odd)], p, db, dseq, dseq)

    new_s5_re = jnp.stack([h[:, :, 0].reshape(nb, 2, ngrp, npst) for h in s5_fin], axis=1)
    new_s5_im = jnp.stack([h[:, :, 1].reshape(nb, 2, ngrp, npst) for h in s5_fin], axis=1)
    new_c = jnp.stack([f[0] for f in ml_fin], axis=1)
    new_n = jnp.stack([f[1] for f in ml_fin], axis=1)
    new_m = jnp.stack([f[2][..., 0] for f in ml_fin], axis=1)
    return (y_prompt.reshape(nb, seq, d), y_sample.reshape(db, dseq, d), new_s5_re, new_s5_im, new_c, new_n, new_m)
```

```python
import functools
import math

import jax
import jax.numpy as jnp
from jax import lax
from jax.experimental import pallas as pl
from jax.experimental.pallas import tpu as pltpu

F32 = jnp.float32
BF16 = jnp.bfloat16
EPS = 1e-6
HIGHEST = lax.Precision.HIGHEST
V7X_VMEM_LIMIT_BYTES = 56 * 1024 * 1024
LANES = 128
SUBLANES = 8
ML_CHUNK = 128
PK_TOPK = 16
GRID_W = 64
NEG_INF = float("-inf")
ROW_TILE = 1024
DFT_TILE = 512
ADA_COLS = 1536
CONV_COLS = 256
S5_CHUNK = 256
PEER_SCORE_TOKENS = 256
PEER_TOKENS = 1024
PEER_EXPERTS = 1024


def _cp(*sem):
    return pltpu.CompilerParams(dimension_semantics=sem, vmem_limit_bytes=V7X_VMEM_LIMIT_BYTES)


def _dot(a, b, **kw):
    return jnp.dot(a, b, preferred_element_type=F32, **kw)


def _dot_nt(a, b):
    return lax.dot_general(a, b, (((1,), (1,)), ((), ())), preferred_element_type=F32)


def _silu(x):
    return x * jax.nn.sigmoid(x)


def _ada_kernel(c_ref, w_ref, b_ref, o_ref):
    o_ref[...] = _dot(_silu(c_ref[...]), w_ref[...], precision=HIGHEST) + b_ref[...]


def ada_mod(cond8, w, b):
    d, no = w.shape
    tn = _tile(no, ADA_COLS)
    return pl.pallas_call(
        _ada_kernel, grid=(no // tn,),
        in_specs=[pl.BlockSpec((8, d), lambda j: (0, 0)), pl.BlockSpec((d, tn), lambda j: (0, j)),
                  pl.BlockSpec((1, tn), lambda j: (0, j))],
        out_specs=pl.BlockSpec((8, tn), lambda j: (0, j)),
        out_shape=jax.ShapeDtypeStruct((8, no), F32), compiler_params=_cp("parallel"), name="ada_mod",
    )(cond8, w, b.reshape(1, no))


def _normmod(x, g, sc, sh):
    y = x * lax.rsqrt(jnp.mean(x * x, axis=-1, keepdims=True) + EPS)
    return (y * g) * (1.0 + sc) + sh


def _mod_spec(d, tm, rows_per_mod):
    return pl.BlockSpec((1, 1, d), lambda i, j: ((i * tm) // rows_per_mod, 0, 0))


def _nm_matmul_kernel(x_ref, g_ref, sc_ref, sh_ref, w_ref, o_ref, h_ref):
    @pl.when(pl.program_id(1) == 0)
    def _():
        h_ref[...] = _normmod(x_ref[...], g_ref[...], sc_ref[0], sh_ref[0]).astype(BF16)
    o_ref[...] = _dot(h_ref[...], w_ref[...])


def normmod_matmul(x, g, sc, sh, w_bf16, rows_per_mod, tm, tn):
    n, d = x.shape
    no = w_bf16.shape[1]
    return pl.pallas_call(
        _nm_matmul_kernel, grid=(n // tm, no // tn),
        in_specs=[pl.BlockSpec((tm, d), lambda i, j: (i, 0)), pl.BlockSpec((1, d), lambda i, j: (0, 0)),
                  _mod_spec(d, tm, rows_per_mod), _mod_spec(d, tm, rows_per_mod),
                  pl.BlockSpec((d, tn), lambda i, j: (0, j))],
        out_specs=pl.BlockSpec((tm, tn), lambda i, j: (i, j)),
        out_shape=jax.ShapeDtypeStruct((n, no), F32),
        scratch_shapes=[pltpu.VMEM((tm, d), BF16)],
        compiler_params=_cp("parallel", "arbitrary"), name="normmod_matmul",
    )(x, g.reshape(1, d), sc, sh, w_bf16)


def _add_rows_kernel(x_ref, p_ref, o_ref):
    o_ref[...] = x_ref[...] + p_ref[...]


def add_pos(x, pe, seq_len, tm):
    n, d = x.shape
    nb = seq_len // tm
    return pl.pallas_call(
        _add_rows_kernel, grid=(n // tm,),
        in_specs=[pl.BlockSpec((tm, d), lambda i: (i, 0)), pl.BlockSpec((tm, d), lambda i: (i % nb, 0))],
        out_specs=pl.BlockSpec((tm, d), lambda i: (i, 0)),
        out_shape=jax.ShapeDtypeStruct((n, d), F32), compiler_params=_cp("parallel"), name="add_pos",
    )(x, pe)


def _sconv_kernel(x_ref, w_ref, b_ref, s_ref, o_ref, *, act):
    x = x_ref[...]
    n_tok = x.shape[0]
    row = lax.broadcasted_iota(jnp.int32, x.shape, 0)
    prev = jnp.where(row == 0, 0.0, pltpu.roll(x, 1, 0))
    nxt = jnp.where(row == n_tok - 1, 0.0, pltpu.roll(x, n_tok - 1, 0))
    y = prev * w_ref[0:1, :] + x * w_ref[1:2, :] + nxt * w_ref[2:3, :] + b_ref[...]
    if act:
        y = _silu(y) * s_ref[...]
    o_ref[...] = y.astype(o_ref.dtype)


def short_conv(a, ncols, w, b, scale, seq_len, act, out_dtype):
    n = a.shape[0]
    cb = _tile(ncols, CONV_COLS)
    return pl.pallas_call(
        functools.partial(_sconv_kernel, act=act), grid=(n // seq_len, ncols // cb),
        in_specs=[pl.BlockSpec((seq_len, cb), lambda s, j: (s, j)), pl.BlockSpec((3, cb), lambda s, j: (0, j)),
                  pl.BlockSpec((1, cb), lambda s, j: (0, j)), pl.BlockSpec((1, cb), lambda s, j: (0, j))],
        out_specs=pl.BlockSpec((seq_len, cb), lambda s, j: (s, j)),
        out_shape=jax.ShapeDtypeStruct((n, ncols), out_dtype), compiler_params=_cp("parallel", "parallel"),
        name="short_conv",
    )(a, w, b.reshape(1, ncols), scale.reshape(1, ncols))


def dft_tables(n_tok):
    k = jnp.arange(n_tok, dtype=jnp.int32)
    blk = 1 << ((n_tok.bit_length() - 1) // 2)
    def thin(n):
        ang = ((k[:, None] * n[None, :]) % (2 * n_tok)).astype(F32) * (math.pi / n_tok)
        return jnp.cos(ang), jnp.sin(ang)
    (c_hi, s_hi), (c_lo, s_lo) = thin(jnp.arange(0, n_tok, blk, dtype=jnp.int32)), thin(jnp.arange(blk, dtype=jnp.int32))
    cos_t = (c_hi[:, :, None] * c_lo[:, None, :] - s_hi[:, :, None] * s_lo[:, None, :]).reshape(n_tok, n_tok)
    msin = -(s_hi[:, :, None] * c_lo[:, None, :] + c_hi[:, :, None] * s_lo[:, None, :]).reshape(n_tok, n_tok)
    alt = jnp.where(k % 2 == 0, 1.0, -1.0).astype(F32)
    a_t = msin.at[0, :].set(alt)
    a_tt = msin.at[:, 0].set(alt)
    return cos_t.astype(BF16), a_t.astype(BF16), a_tt.astype(BF16)


def _hyfilt_kernel(band_ref, w1_ref, b1_ref, w2_ref, b2_ref, w3_ref, fr_ref, dec_ref, h_ref, ss_ref, *,
                   n_tok, tl, hw, nbands):
    i = pl.program_id(0)
    pos = i * tl + lax.broadcasted_iota(jnp.int32, (tl, 1), 0)
    t = pos.astype(F32) / n_tok
    lane = lax.broadcasted_iota(jnp.int32, (tl, LANES), 1)
    ang = 2.0 * math.pi * t * band_ref[...]
    z = jnp.where(lane == 0, t, jnp.where(lane <= nbands, jnp.cos(ang),
                                          jnp.where(lane <= 2 * nbands, jnp.sin(ang), 0.0)))
    fr = fr_ref[...]
    h = jnp.sin(fr * (_dot(z, w1_ref[...], precision=HIGHEST) + b1_ref[...]))
    h = jnp.sin(fr * (_dot(h, w2_ref[...], precision=HIGHEST) + b2_ref[...]))
    h = _dot(h, w3_ref[...], precision=HIGHEST) * jnp.exp(-t * jnp.abs(dec_ref[...]))
    col = lax.broadcasted_iota(jnp.int32, h.shape, 1)
    is_bwd = (col // hw) % 2 == 1
    h = jnp.where(jnp.logical_and(is_bwd, pos == 0), 0.0, h)
    h_ref[...] = h.astype(BF16)

    @pl.when(i == 0)
    def _():
        ss_ref[...] = jnp.zeros_like(ss_ref)
    ss_ref[...] += jnp.sum(h * h, axis=0, keepdims=True)


def hyena_filter_taps(n_tok, w1, b1, w2, b2, w3, freq, decay, hw):
    emb, ffn = w1.shape
    nbands = (emb - 1) // 2
    tl = _tile(n_tok, DFT_TILE)
    bands = jnp.linspace(1e-4, nbands - 1, nbands, dtype=F32)
    band_row = jnp.zeros((1, LANES), F32).at[0, 1:1 + nbands].set(bands).at[0, 1 + nbands:1 + 2 * nbands].set(bands)
    w1p = jnp.zeros((LANES, ffn), F32).at[:emb].set(w1)
    nc = w3.shape[1]
    full = lambda shp: pl.BlockSpec(shp, lambda i: (0, 0))
    return pl.pallas_call(
        functools.partial(_hyfilt_kernel, n_tok=n_tok, tl=tl, hw=hw, nbands=nbands), grid=(n_tok // tl,),
        in_specs=[full((1, LANES)), full((LANES, ffn)), full((1, ffn)), full((ffn, ffn)), full((1, ffn)),
                  full((ffn, nc)), full((1, ffn)), full((1, nc))],
        out_specs=[pl.BlockSpec((tl, nc), lambda i: (i, 0)), full((1, nc))],
        out_shape=[jax.ShapeDtypeStruct((n_tok, nc), BF16), jax.ShapeDtypeStruct((1, nc), F32)],
        compiler_params=_cp("arbitrary"), name="hyena_filter_taps",
    )(band_row, w1p, b1.reshape(1, ffn), w2, b2.reshape(1, ffn), w3, freq.reshape(1, ffn), decay.reshape(1, nc))


def _filt_dft_kernel(c_ref, a_ref, h_ref, ss_ref, kr_ref, ki_ref, *, tf, hw):
    i = pl.program_id(1)
    hf = h_ref[:, :hw]
    hb = h_ref[:, hw:]
    cc = c_ref[...]
    aa = a_ref[...]
    zrf, zif, zrb, zib = _dot(cc, hf), _dot(aa, hf), _dot(cc, hb), _dot(aa, hb)
    scale = lax.rsqrt(ss_ref[:, :hw] + ss_ref[:, hw:] + EPS)
    first = (i * tf + lax.broadcasted_iota(jnp.int32, (tf, 1), 0)) == 0
    scale = scale * jnp.where(first, 0.5, 1.0)
    kr_ref[0] = (zrf + zrb) * scale
    ki_ref[0] = jnp.where(first, zif + zib, zif - zib) * scale


def hyena_filter_spectrum(cos_t, a_t, taps, sumsq, hw, tf):
    n_tok = cos_t.shape[0]
    norder = taps.shape[1] // (2 * hw)
    out = jax.ShapeDtypeStruct((norder, n_tok, hw), F32)
    return pl.pallas_call(
        functools.partial(_filt_dft_kernel, tf=tf, hw=hw), grid=(norder, n_tok // tf),
        in_specs=[pl.BlockSpec((tf, n_tok), lambda o, i: (i, 0)), pl.BlockSpec((tf, n_tok), lambda o, i: (i, 0)),
                  pl.BlockSpec((n_tok, 2 * hw), lambda o, i: (0, o)), pl.BlockSpec((1, 2 * hw), lambda o, i: (0, o))],
        out_specs=[pl.BlockSpec((1, tf, hw), lambda o, i: (o, i, 0))] * 2,
        out_shape=[out, out], compiler_params=_cp("parallel", "parallel"), name="hyena_filter_spectrum",
    )(cos_t, a_t, taps, sumsq)


def _hy_fwd_kernel(c_ref, a_ref, z_ref, kr_ref, ki_ref, yr_ref, yi_ref, *, tf):
    i = pl.program_id(0)
    zb = z_ref[...].astype(BF16)
    zr = _dot(c_ref[...], zb)
    zi = _dot(a_ref[...], zb)
    kr = kr_ref[0]
    ki = ki_ref[0]
    first = (i * tf + lax.broadcasted_iota(jnp.int32, (tf, 1), 0)) == 0
    yr_ref[...] = jnp.where(first, zr * kr, zr * kr - zi * ki).astype(BF16)
    yi_ref[...] = jnp.where(first, zi * ki, zr * ki + zi * kr).astype(BF16)


def hyena_fwd(cos_t, a_t, z, zcol, kr, ki, order, nseq, hw, tf):
    n_tok = cos_t.shape[0]
    nf = n_tok // tf
    out = jax.ShapeDtypeStruct((nseq * n_tok, hw), BF16)
    return pl.pallas_call(
        functools.partial(_hy_fwd_kernel, tf=tf), grid=(nf, nseq),
        in_specs=[pl.BlockSpec((tf, n_tok), lambda i, b: (i, 0)), pl.BlockSpec((tf, n_tok), lambda i, b: (i, 0)),
                  pl.BlockSpec((n_tok, hw), lambda i, b: (b, zcol)),
                  pl.BlockSpec((1, tf, hw), lambda i, b: (order, i, 0)),
                  pl.BlockSpec((1, tf, hw), lambda i, b: (order, i, 0))],
        out_specs=[pl.BlockSpec((tf, hw), lambda i, b: (b * nf + i, 0))] * 2,
        out_shape=[out, out], compiler_params=_cp("parallel", "parallel"), name="hyena_fwd",
    )(cos_t, a_t, z, kr, ki)


def _hy_inv_kernel(c_ref, at_ref, yr_ref, yi_ref, zp_ref, gate_ref, bias_ref, o_ref, *, inv_len):
    conv = (_dot(c_ref[...], yr_ref[...]) + _dot(at_ref[...], yi_ref[...])) * inv_len
    o_ref[...] = gate_ref[...] * (conv + bias_ref[...] * zp_ref[...])


def hyena_inv(cos_t, a_tt, yr, yi, zprev, zcol, gates, gcol, bias_row, nseq, hw, tf):
    n_tok = cos_t.shape[0]
    nf = n_tok // tf
    return pl.pallas_call(
        functools.partial(_hy_inv_kernel, inv_len=1.0 / n_tok), grid=(nf, nseq),
        in_specs=[pl.BlockSpec((tf, n_tok), lambda i, b: (i, 0)), pl.BlockSpec((tf, n_tok), lambda i, b: (i, 0)),
                  pl.BlockSpec((n_tok, hw), lambda i, b: (b, 0)), pl.BlockSpec((n_tok, hw), lambda i, b: (b, 0)),
                  pl.BlockSpec((tf, hw), lambda i, b: (b * nf + i, zcol)),
                  pl.BlockSpec((tf, hw), lambda i, b: (b * nf + i, gcol)),
                  pl.BlockSpec((1, hw), lambda i, b: (0, 0))],
        out_specs=pl.BlockSpec((tf, hw), lambda i, b: (b * nf + i, 0)),
        out_shape=jax.ShapeDtypeStruct((nseq * n_tok, hw), F32),
        compiler_params=_cp("parallel", "parallel"), name="hyena_inv",
    )(cos_t, a_tt, yr, yi, zprev, gates, bias_row)


S5_DIAG_BLOCKS = 2


S5_SEQS_PER_STEP = 4


def _s5_kernel(u_ref, bre_ref, bim_ref, cre_ref, cim_ref, lam_ref, h0_ref, y_ref, hfin_ref, hre_s, him_s, st_s, *,
               tc, nc, ns, nb):
    d = pl.program_id(0)
    c = pl.program_id(2)

    @pl.when(c == 0)
    def _():
        st_s[...] = h0_ref[:, 0]

    sw = u_ref.shape[2]
    halves = [(slice(j * sw // S5_DIAG_BLOCKS, (j + 1) * sw // S5_DIAG_BLOCKS),
               slice(j * ns // S5_DIAG_BLOCKS, (j + 1) * ns // S5_DIAG_BLOCKS)) for j in range(S5_DIAG_BLOCKS)]
    for j in range(nb):
        ub = u_ref[j].astype(BF16)
        for us, hs in halves:
            hre_s[j, :, hs] = _dot(ub[:, us], bre_ref[0, us, hs])
            him_s[j, :, hs] = _dot(ub[:, us], bim_ref[0, us, hs])
    lr = lam_ref[0, 0:1, :]
    li = lam_ref[0, 1:2, :]

    def body(t, carry):
        r = jnp.where(d == 0, t, tc - 1 - t)
        new = []
        for j, (hr, hi) in enumerate(carry):
            nr = lr * hr - li * hi + hre_s[j, pl.ds(r, 1), :]
            ni = lr * hi + li * hr + him_s[j, pl.ds(r, 1), :]
            hre_s[j, pl.ds(r, 1), :] = nr
            him_s[j, pl.ds(r, 1), :] = ni
            new.append((nr, ni))
        return tuple(new)

    start = tuple((st_s[j, 0:1, :], st_s[j, 1:2, :]) for j in range(nb))
    for j, (hr, hi) in enumerate(lax.fori_loop(0, tc, body, start, unroll=4)):
        st_s[j, 0:1, :] = hr
        st_s[j, 1:2, :] = hi
    for j in range(nb):
        for us, hs in halves:
            y_ref[0, j, :, us] = (_dot(hre_s[j, :, hs].astype(BF16), cre_ref[0, hs, us])
                                  + _dot(him_s[j, :, hs].astype(BF16), cim_ref[0, hs, us]))

    @pl.when(c == nc - 1)
    def _():
        hfin_ref[:, 0] = st_s[...]


def s5_scan(proj, ucol, bre, bim, cre, cim, lam, h0, nseq, seq_len, sw, tc):
    ns = bre.shape[2]
    nc = seq_len // tc
    nb = S5_SEQS_PER_STEP if nseq % S5_SEQS_PER_STEP == 0 else 1

    def chunk(d, c):
        return c + d * (nc - 1 - 2 * c)

    y, hfin = pl.pallas_call(
        functools.partial(_s5_kernel, tc=tc, nc=nc, ns=ns, nb=nb), grid=(2, nseq // nb, nc),
        in_specs=[pl.BlockSpec((nb, tc, sw), lambda d, b, c: (b, chunk(d, c), ucol)),
                  pl.BlockSpec((1, sw, ns), lambda d, b, c: (d, 0, 0)),
                  pl.BlockSpec((1, sw, ns), lambda d, b, c: (d, 0, 0)),
                  pl.BlockSpec((1, ns, sw), lambda d, b, c: (d, 0, 0)),
                  pl.BlockSpec((1, ns, sw), lambda d, b, c: (d, 0, 0)),
                  pl.BlockSpec((1, 2, ns), lambda d, b, c: (d, 0, 0)),
                  pl.BlockSpec((nb, 1, 2, ns), lambda d, b, c: (b, d, 0, 0))],
        out_specs=[pl.BlockSpec((1, nb, tc, sw), lambda d, b, c: (d, b, chunk(d, c), 0)),
                   pl.BlockSpec((nb, 1, 2, ns), lambda d, b, c: (b, d, 0, 0))],
        out_shape=[jax.ShapeDtypeStruct((2, nseq, seq_len, sw), F32), jax.ShapeDtypeStruct((nseq, 2, 2, ns), F32)],
        scratch_shapes=[pltpu.VMEM((nb, tc, ns), F32), pltpu.VMEM((nb, tc, ns), F32), pltpu.VMEM((nb, 2, ns), F32)],
        compiler_params=_cp("parallel", "parallel", "arbitrary"), name="s5_scan",
    )(proj.reshape(nseq, seq_len, proj.shape[1]), bre, bim, cre, cim, lam, h0)
    return y.reshape(2, nseq * seq_len, sw), hfin


def _s5_glu_kernel(yf_ref, yb_ref, u_ref, d_ref, w_ref, b_ref, o_ref):
    y = jax.nn.gelu(yf_ref[0] + yb_ref[0] + d_ref[...] * u_ref[...])
    o_ref[...] = y * jax.nn.sigmoid(_dot(y.astype(BF16), w_ref[...]) + b_ref[...])


def s5_glu(y2, proj, ucol, d_skip, glu_w_bf16, glu_b, tm):
    _, n, sw = y2.shape
    return pl.pallas_call(
        _s5_glu_kernel, grid=(n // tm,),
        in_specs=[pl.BlockSpec((1, tm, sw), lambda i: (0, i, 0)), pl.BlockSpec((1, tm, sw), lambda i: (1, i, 0)),
                  pl.BlockSpec((tm, sw), lambda i: (i, ucol)), pl.BlockSpec((1, sw), lambda i: (0, 0)),
                  pl.BlockSpec((sw, sw), lambda i: (0, 0)), pl.BlockSpec((1, sw), lambda i: (0, 0))],
        out_specs=pl.BlockSpec((tm, sw), lambda i: (i, 0)),
        out_shape=jax.ShapeDtypeStruct((n, sw), F32), compiler_params=_cp("parallel"), name="s5_glu",
    )(y2, y2, proj, d_skip.reshape(1, sw), glu_w_bf16, glu_b.reshape(1, sw))


def _residual_and_next_norm(x_ref, gate_ref, y, g2_ref, sc2_ref, sh2_ref, o_ref, hn_ref):
    xn = x_ref[...] + gate_ref[0] * y
    o_ref[...] = xn
    hn_ref[...] = _normmod(xn, g2_ref[...], sc2_ref[0], sh2_ref[0]).astype(BF16)


def _row_specs(d, tm, rows_per_mod):
    mod = pl.BlockSpec((1, 1, d), lambda i: ((i * tm) // rows_per_mod, 0, 0))
    return [pl.BlockSpec((tm, d), lambda i: (i, 0)), mod, pl.BlockSpec((1, d), lambda i: (0, 0)), mod, mod]


def _row_outs(n, d, tm):
    spec = pl.BlockSpec((tm, d), lambda i: (i, 0))
    return [spec, spec], [jax.ShapeDtypeStruct((n, d), F32), jax.ShapeDtypeStruct((n, d), BF16)]


def _even_out_kernel(a_ref, b_ref, wa_ref, wb_ref, x_ref, gate_ref, g2_ref, sc2_ref, sh2_ref, o_ref, hn_ref):
    y = _dot(a_ref[...].astype(BF16), wa_ref[...]) + _dot(b_ref[...].astype(BF16), wb_ref[...])
    _residual_and_next_norm(x_ref, gate_ref, y, g2_ref, sc2_ref, sh2_ref, o_ref, hn_ref)


def even_out(hy, s5o, w_bf16, x, gate, g2, sc2, sh2, rows_per_mod, tm):
    n, d = x.shape
    hw = hy.shape[1]
    sw = s5o.shape[1]
    out_specs, out_shape = _row_outs(n, d, tm)
    return pl.pallas_call(
        _even_out_kernel, grid=(n // tm,),
        in_specs=[pl.BlockSpec((tm, hw), lambda i: (i, 0)), pl.BlockSpec((tm, sw), lambda i: (i, 0)),
                  pl.BlockSpec((hw, d), lambda i: (0, 0)), pl.BlockSpec((sw, d), lambda i: (hw // sw, 0))]
        + _row_specs(d, tm, rows_per_mod),
        out_specs=out_specs, out_shape=out_shape, compiler_params=_cp("parallel"), name="even_out",
    )(hy, s5o, w_bf16, w_bf16, x, gate, g2.reshape(1, d), sc2, sh2)


ML_HEAD_GROUP = 8


def _log_sigmoid(x):
    return jnp.minimum(x, 0.0) - jnp.log1p(jnp.exp(-jnp.abs(x)))


def _mlstm_kernel(q_ref, k_ref, v_ref, g_ref, gb_ref, c0_ref, n0_ref, m0_ref, h_ref, cf_ref, nf_ref, mf_ref,
                  c_s, m_s, *, nh, dh, tc, nc):
    d = pl.program_id(0)
    c = pl.program_id(2)

    @pl.when(c == 0)
    def _():
        for h in range(nh):
            c_s[h, :, :dh] = c0_ref[0, 0, h]
            c_s[h, :, dh:] = jnp.broadcast_to(n0_ref[0, 0, h:h + 1, :], (dh, dh)).T
        m_s[...] = m0_ref[0, 0]

    ones = jnp.ones((tc, dh), F32)
    gates = g_ref[...] + gb_ref[0]
    lane = lax.broadcasted_iota(jnp.int32, gates.shape, 1)
    logf = jnp.where(jnp.logical_and(lane >= nh, lane < 2 * nh), _log_sigmoid(gates), 0.0)
    r_i = lax.broadcasted_iota(jnp.int32, (tc, tc), 0)
    s_i = lax.broadcasted_iota(jnp.int32, (tc, tc), 1)
    causal = (r_i - s_i) * (1 - 2 * d) >= 0
    bcum = _dot(causal.astype(F32), logf, precision=HIGHEST)
    btot = jnp.sum(logf, axis=0, keepdims=True)
    gates_t = gates.T
    bcum_t = bcum.T
    for g0 in range(0, nh, ML_HEAD_GROUP):
        hds = list(range(g0, min(g0 + ML_HEAD_GROUP, nh)))
        hsl = {h: slice(h * dh, (h + 1) * dh) for h in hds}
        b_col = {h: bcum[:, nh + h:nh + h + 1] for h in hds}
        m_old = {h: m_s[h:h + 1, :] for h in hds}
        a = {h: b_col[h] + m_old[h] for h in hds}
        src = {h: jnp.where(causal, gates_t[h:h + 1, :] - bcum_t[nh + h:nh + h + 1, :], NEG_INF) for h in hds}
        mq = {h: jnp.maximum(a[h], b_col[h] + jnp.max(src[h], axis=-1, keepdims=True)) for h in hds}
        rel = {h: jnp.broadcast_to(b_col[h] - mq[h], (tc, tc)) for h in hds}
        s = {h: _dot_nt(q_ref[:, hsl[h]], k_ref[:, hsl[h]]) * jnp.exp(src[h] + rel[h]) for h in hds}
        v1 = {h: jnp.concatenate([v_ref[:, hsl[h]], ones], axis=1) for h in hds}
        cn = {h: c_s[h] for h in hds}
        qw = {h: jnp.exp(rel[h][:, :dh] + m_old[h]) * q_ref[:, hsl[h]] for h in hds}
        both = {h: _dot(s[h], v1[h]) + _dot(qw[h], cn[h]) for h in hds}
        for h in hds:
            h_ref[0, :, hsl[h]] = both[h][:, :dh] / jnp.maximum(jnp.abs(both[h][:, dh:]), jnp.exp(-mq[h]))
        b_last = {h: btot[:, nh + h:nh + h + 1] for h in hds}
        g = {h: b_last[h] - bcum_t[nh + h:nh + h + 1, :] + gates_t[h:h + 1, :] for h in hds}
        m_new = {h: jnp.maximum(b_last[h] + m_old[h], jnp.max(g[h], axis=1, keepdims=True)) for h in hds}
        kw_t = {h: k_ref[:, hsl[h]].astype(F32).T * jnp.exp(g[h] - m_new[h]) for h in hds}
        for h in hds:
            c_s[h] = jnp.exp(b_last[h] + m_old[h] - m_new[h]) * cn[h] + _dot(kw_t[h], v1[h])
            m_s[h:h + 1, :] = m_new[h]

    @pl.when(c == nc - 1)
    def _():
        for h in range(nh):
            cf_ref[0, 0, h] = c_s[h, :, :dh]
            nf_ref[0, 0, h:h + 1, :] = c_s[h, :, dh:].T[0:1, :]
        mf_ref[0, 0] = m_s[...]


def mlstm_scan(qk, proj, vcol, gate_bias, c0, n0, m0, nseq, seq_len, nh, dh):
    tc = ML_CHUNK
    nc = seq_len // tc
    w = nh * dh

    def chunk(d, c):
        return c + d * (nc - 1 - 2 * c)

    rowblk = lambda d, b, c: b * nc + chunk(d, c)
    st = lambda shp: pl.BlockSpec((1, 1) + shp, lambda d, b, c: (b, d) + (0,) * len(shp))
    return pl.pallas_call(
        functools.partial(_mlstm_kernel, nh=nh, dh=dh, tc=tc, nc=nc), grid=(2, nseq, nc),
        in_specs=[pl.BlockSpec((tc, w), lambda d, b, c: (rowblk(d, b, c), 0)),
                  pl.BlockSpec((tc, w), lambda d, b, c: (rowblk(d, b, c), 1)),
                  pl.BlockSpec((tc, w), lambda d, b, c: (rowblk(d, b, c), vcol)),
                  pl.BlockSpec((tc, LANES), lambda d, b, c: (rowblk(d, b, c), 4 * w // LANES + d)),
                  pl.BlockSpec((1, 1, LANES), lambda d, b, c: (d, 0, 0)),
                  st((nh, dh, dh)), st((nh, dh)), st((nh, 1))],
        out_specs=[pl.BlockSpec((1, tc, w), lambda d, b, c: (d, rowblk(d, b, c), 0)),
                   st((nh, dh, dh)), st((nh, dh)), st((nh, 1))],
        out_shape=[jax.ShapeDtypeStruct((2, nseq * seq_len, w), F32),
                   jax.ShapeDtypeStruct((nseq, 2, nh, dh, dh), F32), jax.ShapeDtypeStruct((nseq, 2, nh, dh), F32),
                   jax.ShapeDtypeStruct((nseq, 2, nh, 1), F32)],
        scratch_shapes=[pltpu.VMEM((nh, dh, 2 * dh), F32), pltpu.VMEM((nh, 1), F32)],
        compiler_params=_cp("parallel", "parallel", "arbitrary"), name="mlstm_scan",
    )(qk, qk, proj, proj, gate_bias, c0, n0, m0)


def _odd_out_kernel(hf_ref, hb_ref, og_ref, ng_ref, w_ref, x_ref, gate_ref, g2_ref, sc2_ref, sh2_ref, o_ref, hn_ref,
                    a_s, *, nh, dh):
    for h in range(nh):
        hs = slice(h * dh, (h + 1) * dh)
        blk = hf_ref[0, :, hs] + hb_ref[0, :, hs]
        blk = blk * lax.rsqrt(jnp.mean(blk * blk, axis=-1, keepdims=True) + EPS)
        a_s[:, hs] = ((blk * ng_ref[:, hs]) * _silu(og_ref[:, hs])).astype(BF16)
    _residual_and_next_norm(x_ref, gate_ref, _dot(a_s[...], w_ref[...]), g2_ref, sc2_ref, sh2_ref, o_ref, hn_ref)


def odd_out(h2, proj, ocol, norm_g, w_bf16, x, gate, g2, sc2, sh2, rows_per_mod, nh, dh, tm):
    n, d = x.shape
    w = nh * dh
    out_specs, out_shape = _row_outs(n, d, tm)
    return pl.pallas_call(
        functools.partial(_odd_out_kernel, nh=nh, dh=dh), grid=(n // tm,),
        in_specs=[pl.BlockSpec((1, tm, w), lambda i: (0, i, 0)), pl.BlockSpec((1, tm, w), lambda i: (1, i, 0)),
                  pl.BlockSpec((tm, w), lambda i: (i, ocol)), pl.BlockSpec((1, w), lambda i: (0, 0)),
                  pl.BlockSpec((w, d), lambda i: (0, 0))] + _row_specs(d, tm, rows_per_mod),
        out_specs=out_specs, out_shape=out_shape, scratch_shapes=[pltpu.VMEM((tm, w), BF16)],
        compiler_params=_cp("parallel"), name="odd_out",
    )(h2, h2, proj, norm_g.reshape(1, w), w_bf16, x, gate, g2.reshape(1, d), sc2, sh2)


def _sort_network(n):
    pairs, p = [], 1
    while p < n:
        k = p
        while k >= 1:
            for j in range(k % p, n - k, 2 * k):
                for i in range(min(k, n - j - k)):
                    if (i + j) // (2 * p) == (i + j + k) // (2 * p):
                        pairs.append((i + j, i + j + k))
            k //= 2
        p *= 2
    return pairs


def _top_values_tiled(arrays, k, outs):
    cols = []
    for arr in arrays:
        tiles = [arr[r:r + SUBLANES, :] for r in range(0, arr.shape[0], SUBLANES)]
        for lo, hi in _sort_network(len(tiles)):
            tiles[lo], tiles[hi] = jnp.maximum(tiles[lo], tiles[hi]), jnp.minimum(tiles[lo], tiles[hi])
        cols.append(tiles)
    for j in range(k):
        for tiles, out_s in zip(cols, outs):
            m = jnp.max(tiles[0], axis=0, keepdims=True)
            out_s[j:j + 1, :] = m
            hit = tiles[0] == m
            for i in range(min(len(tiles), k - 1 - j)):
                below = tiles[i + 1] if i + 1 < len(tiles) else NEG_INF
                tiles[i] = jnp.where(hit, below, tiles[i])


def _pair_candidates(k):
    return [(a, k // (a + 1)) for a in range(k)]


PEER_HEADS_PER_TRIP = 8


def _peer_score_kernel(h_ref, wq_ref, keys_ref, th_ref, s2_ref, w1_ref, w2_ref, q_s, v_s, cand_s, best_s, *,
                       nh, half, topk):
    q_s[...] = _dot_nt(wq_ref[...], h_ref[...])
    kk = topk + 1
    cand_s[...] = jnp.full(cand_s.shape, NEG_INF, F32)
    group = range(PEER_HEADS_PER_TRIP)

    def heads(trip, carry):
        hds = [trip * PEER_HEADS_PER_TRIP + u for u in group]
        scores = []
        for hd in hds:
            base = pl.multiple_of(hd * 2 * half, 2 * half)
            scores.append(_dot(keys_ref[hd, 0], q_s[pl.ds(base, half), :]))
            scores.append(_dot(keys_ref[hd, 1], q_s[pl.ds(base + half, half), :]))
        _top_values_tiled(scores, kk, [v_s.at[u, c] for u in group for c in range(2)])
        for u in group:
            off = 0
            for a, cnt in _pair_candidates(kk):
                cand_s[u, off:off + cnt, :] = v_s[u, 0, a:a + 1, :] + v_s[u, 1, 0:cnt, :]
                off += cnt
        _top_values_tiled([cand_s[u] for u in group], kk, [best_s.at[u] for u in group])
        for u, hd in enumerate(hds):
            s1, s2 = scores[2 * u], scores[2 * u + 1]
            best = best_s[u, 0:topk, :]
            z = jnp.sum(jnp.exp(best - best[0:1, :]), axis=0, keepdims=True)
            tmid = 0.5 * (best_s[u, topk - 1:topk, :] + best_s[u, topk:topk + 1, :])
            th = tmid - s1
            w1 = jnp.exp(s1 - v_s[u, 0, 0:1, :]) / z
            w2 = jnp.exp(s2 - v_s[u, 1, 0:1, :])
            for lt in range(s1.shape[1] // LANES):
                sl = slice(lt * LANES, (lt + 1) * LANES)
                th_ref[hd, lt] = th[:, sl]
                s2_ref[hd, lt] = s2[:, sl]
                w1_ref[hd, lt] = w1[:, sl]
                w2_ref[hd, lt] = w2[:, sl]
        return carry

    lax.fori_loop(0, nh // PEER_HEADS_PER_TRIP, heads, 0)


def peer_scores(h_bf16, wq_t_bf16, keys, tt):
    n, d = h_bf16.shape
    nh, _, nk, half = keys.shape
    kk = PK_TOPK + 1
    top_rows = -(-kk // SUBLANES) * SUBLANES
    ncand = SUBLANES
    while ncand < sum(c for _, c in _pair_candidates(kk)):
        ncand *= 2
    big = jax.ShapeDtypeStruct((nh, n // LANES, nk, LANES), F32)
    bspec = pl.BlockSpec((nh, tt // LANES, nk, LANES), lambda i: (0, i, 0, 0))
    return pl.pallas_call(
        functools.partial(_peer_score_kernel, nh=nh, half=half, topk=PK_TOPK), grid=(n // tt,),
        in_specs=[pl.BlockSpec((tt, d), lambda i: (i, 0)), pl.BlockSpec((nh * 2 * half, d), lambda i: (0, 0)),
                  pl.BlockSpec((nh, 2, nk, half), lambda i: (0, 0, 0, 0))],
        out_specs=[bspec, bspec, bspec, bspec],
        out_shape=[big, big, big, big],
        scratch_shapes=[pltpu.VMEM((nh * 2 * half, tt), F32), pltpu.VMEM((PEER_HEADS_PER_TRIP, 2, top_rows, tt), F32),
                        pltpu.VMEM((PEER_HEADS_PER_TRIP, ncand, tt), F32),
                        pltpu.VMEM((PEER_HEADS_PER_TRIP, top_rows, tt), F32)],
        compiler_params=_cp("parallel"), name="peer_scores",
    )(h_bf16, wq_t_bf16, keys)


PEER_KEY_ROWS = 32
GELU_C1 = math.sqrt(2.0 / math.pi)
GELU_C2 = 0.044715 * GELU_C1


def _gelu_tanh(x):
    half_x = 0.5 * x
    return half_x + half_x * jnp.tanh(x * (GELU_C1 + GELU_C2 * (x * x)))


def _peer_dense_kernel(h_ref, u_ref, vt_ref, th_ref, s2_ref, w1_ref, w2_ref, x_ref, gate_ref, fg_ref, o_ref,
                       acc_s, st_s, wt_s, *, nh, nk, ec, tt, final):
    e = pl.program_id(1)
    nlt = tt // LANES
    n_i1 = ec // nk
    nkt = nk // PEER_KEY_ROWS
    nsub = PEER_KEY_ROWS // SUBLANES

    @pl.when(e == 0)
    def _():
        acc_s[...] = jnp.zeros_like(acc_s)

    st = _gelu_tanh(_dot_nt(u_ref[...], h_ref[...]))
    for lt in range(nlt):
        st_s[lt] = st[:, lt * LANES:(lt + 1) * LANES]

    def tile(idx, carry):
        lt = idx // nkt
        k0 = (idx % nkt) * PEER_KEY_ROWS
        subs = [pl.ds(pl.multiple_of(k0 + j * SUBLANES, SUBLANES), SUBLANES) for j in range(nsub)]
        g = [[jnp.zeros((SUBLANES, LANES), F32) for _ in subs] for _ in range(n_i1)]
        for hd in range(nh):
            s2t = [s2_ref[hd, lt, sub, :] for sub in subs]
            w2t = [w2_ref[hd, lt, sub, :] for sub in subs]
            for li in range(n_i1):
                thb = jnp.broadcast_to(th_ref[hd, lt, li:li + 1, :], (SUBLANES, LANES))
                w1b = jnp.broadcast_to(w1_ref[hd, lt, li:li + 1, :], (SUBLANES, LANES))
                for j in range(nsub):
                    g[li][j] = g[li][j] + jnp.where(s2t[j] >= thb, w2t[j] * w1b, 0.0)
        for li in range(n_i1):
            rows = pl.ds(pl.multiple_of(li * nk + k0, PEER_KEY_ROWS), PEER_KEY_ROWS)
            wt_s[lt, rows, :] = (st_s[lt, rows, :] * jnp.concatenate(g[li], axis=0)).astype(BF16)
        return carry

    lax.fori_loop(0, nlt * nkt, tile, 0, unroll=2)
    wt = jnp.concatenate([wt_s[lt] for lt in range(nlt)], axis=1)
    acc_s[...] += _dot(vt_ref[0], wt)

    @pl.when(e == pl.num_programs(1) - 1)
    def _():
        xn = x_ref[...] + gate_ref[0] * acc_s[...].T
        if final:
            xn = (xn * lax.rsqrt(jnp.mean(xn * xn, axis=-1, keepdims=True) + EPS)) * fg_ref[...]
        o_ref[...] = xn


def peer_dense(h_bf16, u_bf16, vt_bf16, th, s2, w1, w2, x, gate, final_g, final, rows_per_mod, tt, ec):
    n, d = x.shape
    nh, _, nk, _ = s2.shape
    nchunk = u_bf16.shape[0] // ec
    once = pl.Buffered(1)
    bspec = pl.BlockSpec((nh, tt // LANES, nk, LANES), lambda i, e: (0, i, 0, 0), pipeline_mode=once)
    rspec = pl.BlockSpec((nh, tt // LANES, ec // nk, LANES), lambda i, e: (0, i, e, 0))
    tile_buf = (tt // LANES, ec, LANES)
    return pl.pallas_call(
        functools.partial(_peer_dense_kernel, nh=nh, nk=nk, ec=ec, tt=tt, final=final), grid=(n // tt, nchunk),
        in_specs=[pl.BlockSpec((tt, d), lambda i, e: (i, 0)), pl.BlockSpec((ec, d), lambda i, e: (e, 0)),
                  pl.BlockSpec((1, d, ec), lambda i, e: (e, 0, 0)), rspec, bspec, rspec, bspec,
                  pl.BlockSpec((tt, d), lambda i, e: (i, 0), pipeline_mode=once),
                  pl.BlockSpec((1, 1, d), lambda i, e: ((i * tt) // rows_per_mod, 0, 0)),
                  pl.BlockSpec((1, d), lambda i, e: (0, 0))],
        out_specs=pl.BlockSpec((tt, d), lambda i, e: (i, 0)),
        out_shape=jax.ShapeDtypeStruct((n, d), F32),
        scratch_shapes=[pltpu.VMEM((d, tt), F32), pltpu.VMEM(tile_buf, F32), pltpu.VMEM(tile_buf, BF16)],
        compiler_params=_cp("parallel", "arbitrary"), name="peer_dense",
    )(h_bf16, u_bf16, vt_bf16, th, s2, w1, w2, x, gate, final_g.reshape(1, d))


def _s5_params(a_re, a_im, b_re, b_im, c_re, c_im, log_step):
    lam = lax.complex(a_re.astype(F32), a_im.astype(F32))
    lam_bar = jnp.exp(lam * jnp.exp(log_step.astype(F32))[..., None])
    b_bar = ((lam_bar - 1.0) / lam)[..., None] * lax.complex(b_re.astype(F32), b_im.astype(F32))
    ngrp, npst, nch = b_bar.shape[1:]
    eye = jnp.eye(ngrp, dtype=F32)

    def b_mat(part):
        return jnp.einsum("dgpj,gh->dgjhp", part, eye).reshape(2, ngrp * nch, ngrp * npst)

    def c_mat(part):
        return jnp.einsum("dgjp,gh->dgphj", part, eye).reshape(2, ngrp * npst, ngrp * nch)

    bre, bim = b_mat(b_bar.real).astype(BF16), b_mat(b_bar.imag).astype(BF16)
    cre, cim = c_mat(c_re.astype(F32)).astype(BF16), c_mat(-c_im.astype(F32)).astype(BF16)
    lam2 = jnp.stack([lam_bar.real.reshape(2, -1), lam_bar.imag.reshape(2, -1)], axis=1)
    return bre, bim, cre, cim, lam2


def _pos_embed(n_tok, d, grid_w):
    rows = n_tok // grid_w
    quarter = d // 4
    omega = 1.0 / (10000.0 ** (jnp.arange(quarter, dtype=F32) / quarter))

    def emb1d(pos):
        ang = pos.astype(F32)[:, None] * omega[None]
        return jnp.concatenate([jnp.sin(ang), jnp.cos(ang)], axis=-1)

    er = emb1d(jnp.arange(rows))
    ec = emb1d(jnp.arange(grid_w))
    half = d // 2
    pe = jnp.concatenate([jnp.broadcast_to(er[:, None], (rows, grid_w, half)),
                          jnp.broadcast_to(ec[None], (rows, grid_w, half))], axis=-1)
    return pe.reshape(rows * grid_w, d)


def _tile(n, pref):
    return pref if n % pref == 0 else n


def _trunk(x, mods, s5_h0, ml_c0, ml_n0, ml_m0, p, nseq, seq_len, rows_per_mod):
    n, d = x.shape
    tm = _tile(min(rows_per_mod, n), ROW_TILE)
    depth = p["norm_g"].shape[0]
    s5_fin, ml_fin = [], []
    for l in range(depth):
        sh1, sc1, g1, sh2, sc2, g2 = mods[l]
        i = l // 2
        if l % 2 == 0:
            hw = p["hy_bias"].shape[2]
            sw = p["s5_d"].shape[1]
            proj = normmod_matmul(x, p["norm_g"][l, 0], sc1, sh1, p["ev_w_in"][i].astype(BF16), rows_per_mod, tm,
                                  3 * hw + sw)
            hy_in = short_conv(proj, 3 * hw, p["hy_conv_w"][i], p["hy_conv_b"][i], jnp.ones((3 * hw,), F32),
                               seq_len, act=False, out_dtype=F32)
            cos_t, a_t, a_tt = dft_tables(seq_len)
            tf = _tile(seq_len, DFT_TILE)
            taps, sumsq = hyena_filter_taps(seq_len, p["hy_w1"][i], p["hy_b1"][i], p["hy_w2"][i], p["hy_b2"][i],
                                            p["hy_w3"][i], p["hy_freq"][i], p["hy_decay"][i], hw)
            kr, ki = hyena_filter_spectrum(cos_t, a_t, taps, sumsq, hw, tf)
            bias = p["hy_bias"][i].astype(F32)
            z, zcol = hy_in, 0
            for o in range(bias.shape[0]):
                yr, yi = hyena_fwd(cos_t, a_t, z, zcol, kr, ki, o, nseq, hw, tf)
                z = hyena_inv(cos_t, a_tt, yr, yi, z, zcol, hy_in, 1 + o, bias[o:o + 1], nseq, hw, tf)
                zcol = 0
            bre, bim, cre, cim, lam2 = _s5_params(p["s5_a_re"][i], p["s5_a_im"][i], p["s5_b_re"][i], p["s5_b_im"][i],
                                                  p["s5_c_re"][i], p["s5_c_im"][i], p["s5_log_step"][i])
            ucol = 3 * hw // sw
            y2, hfin = s5_scan(proj, ucol, bre, bim, cre, cim, lam2, s5_h0[i], nseq, seq_len, sw,
                               _tile(seq_len, S5_CHUNK))
            s5_fin.append(hfin)
            s5o = s5_glu(y2, proj, ucol, p["s5_d"][i], p["s5_glu_w"][i].astype(BF16), p["s5_glu_b"][i], tm)
            x, hn = even_out(z, s5o, p["ev_w_out"][i].astype(BF16), x, g1, p["norm_g"][l, 1], sc2, sh2, rows_per_mod, tm)
        else:
            nh = p["od_gate_b"].shape[2]
            w = p["ml_norm_g"].shape[1]
            dh = w // nh
            w_in = p["od_w_in"][i]
            wg = w_in[:, 4 * w:].reshape(d, 4, nh)
            gb = p["od_gate_b"][i].astype(F32)
            wg2 = jnp.zeros((d, 2, LANES), w_in.dtype)
            bg2 = jnp.zeros((2, 1, LANES), F32)
            for dr in range(2):
                wg2 = wg2.at[:, dr, :nh].set(wg[:, dr]).at[:, dr, nh:2 * nh].set(wg[:, 2 + dr])
                bg2 = bg2.at[dr, 0, :nh].set(gb[dr]).at[dr, 0, nh:2 * nh].set(gb[2 + dr])
            w_all = jnp.concatenate([w_in[:, :4 * w], wg2.reshape(d, 2 * LANES)], axis=1).astype(BF16)
            proj = normmod_matmul(x, p["norm_g"][l, 0], sc1, sh1, w_all, rows_per_mod, tm, w_all.shape[1] // 2)
            qscale = jnp.concatenate([jnp.full((w,), dh ** -0.5, F32), jnp.ones((w,), F32)])
            qk = short_conv(proj, 2 * w, p["ml_conv_w"][i], p["ml_conv_b"][i], qscale, seq_len, act=True, out_dtype=BF16)
            h2, cf, nf, mf = mlstm_scan(qk, proj, 2, bg2, ml_c0[i], ml_n0[i], ml_m0[i], nseq, seq_len, nh, dh)
            ml_fin.append((cf, nf, mf))
            x, hn = odd_out(h2, proj, 3, p["ml_norm_g"][i], p["od_w_out"][i].astype(BF16), x, g1, p["norm_g"][l, 1],
                            sc2, sh2, rows_per_mod, nh, dh, tm)
        tt = _tile(min(rows_per_mod, n), PEER_TOKENS)
        th, s2, w1, w2 = peer_scores(hn, p["pk_w_q"][l].T.astype(BF16), p["pk_keys"][l].astype(F32),
                                     _tile(tt, PEER_SCORE_TOKENS))
        ec = _tile(p["pk_u"].shape[1], PEER_EXPERTS)
        vt = p["pk_v"][l].astype(BF16).reshape(-1, ec, d).transpose(0, 2, 1)
        x = peer_dense(hn, p["pk_u"][l].astype(BF16), vt, th, s2, w1, w2, x, g2, p["final_g"], l == depth - 1,
                       rows_per_mod, tt, ec)
    return x, s5_fin, ml_fin


def kernel(x_prompt, x_sample, state_s5_re, state_s5_im, state_mlstm_C, state_mlstm_n, state_mlstm_m, c, c_ctx, norm_g, ada_w, ada_b, final_g, ev_w_in, hy_conv_w, hy_conv_b, hy_w1, hy_b1, hy_w2, hy_b2, hy_w3, hy_freq, hy_decay, hy_bias, s5_a_re, s5_a_im, s5_b_re, s5_b_im, s5_c_re, s5_c_im, s5_log_step, s5_d, s5_glu_w, s5_glu_b, ev_w_out, od_w_in, od_gate_b, ml_conv_w, ml_conv_b, ml_norm_g, od_w_out, pk_w_q, pk_keys, pk_u, pk_v):
    p = dict(norm_g=norm_g, ada_w=ada_w, ada_b=ada_b, final_g=final_g, ev_w_in=ev_w_in,
             hy_conv_w=hy_conv_w, hy_conv_b=hy_conv_b, hy_w1=hy_w1, hy_b1=hy_b1, hy_w2=hy_w2, hy_b2=hy_b2,
             hy_w3=hy_w3, hy_freq=hy_freq, hy_decay=hy_decay, hy_bias=hy_bias, s5_a_re=s5_a_re,
             s5_a_im=s5_a_im, s5_b_re=s5_b_re, s5_b_im=s5_b_im, s5_c_re=s5_c_re, s5_c_im=s5_c_im,
             s5_log_step=s5_log_step, s5_d=s5_d, s5_glu_w=s5_glu_w, s5_glu_b=s5_glu_b, ev_w_out=ev_w_out,
             od_w_in=od_w_in, od_gate_b=od_gate_b, ml_conv_w=ml_conv_w, ml_conv_b=ml_conv_b,
             ml_norm_g=ml_norm_g, od_w_out=od_w_out, pk_w_q=pk_w_q, pk_keys=pk_keys, pk_u=pk_u, pk_v=pk_v)
    nb, seq, d = x_prompt.shape
    db, dseq, _ = x_sample.shape
    depth = norm_g.shape[0]
    n_even, n_odd = (depth + 1) // 2, depth // 2
    assert db + 1 <= 8

    cond8 = jnp.zeros((8, d), F32).at[0].set(c_ctx.astype(F32)).at[1:1 + db].set(c.astype(F32))
    mods_ctx, mods_lat = [], []
    for l in range(depth):
        mod = ada_mod(cond8, ada_w[l].astype(F32), ada_b[l].astype(F32))
        chunks = [mod[:, j * d:(j + 1) * d] for j in range(6)]
        mods_ctx.append([ch[0:1].reshape(1, 1, d) for ch in chunks])
        mods_lat.append([ch[1:1 + db].reshape(db, 1, d) for ch in chunks])

    def s5_state(re, im, bsz):
        return [jnp.stack([re[:, i].reshape(bsz, 2, -1), im[:, i].reshape(bsz, 2, -1)], axis=2).astype(F32)
                for i in range(n_even)]

    ngrp, npst = s5_a_re.shape[2], s5_a_re.shape[3]
    nh, dh = state_mlstm_C.shape[3], state_mlstm_C.shape[4]
    zeros_s5 = jnp.zeros((nb, n_even, 2, ngrp, npst), F32)
    y_prompt, s5_fin, ml_fin = _trunk(
        x_prompt.reshape(nb * seq, d), mods_ctx, s5_state(zeros_s5, zeros_s5, nb),
        [jnp.zeros((nb, 2, nh, dh, dh), F32)] * n_odd, [jnp.zeros((nb, 2, nh, dh), F32)] * n_odd,
        [jnp.zeros((nb, 2, nh, 1), F32)] * n_odd, p, nb, seq, nb * seq)
    x_lat = add_pos(x_sample.reshape(db * dseq, d), _pos_embed(dseq, d, GRID_W), dseq, _tile(dseq, ROW_TILE))
    y_sample, _, _ = _trunk(
        x_lat, mods_lat, s5_state(state_s5_re, state_s5_im, db),
        [state_mlstm_C[:, i].astype(F32) for i in range(n_odd)], [state_mlstm_n[:, i].astype(F32) for i in range(n_odd)],
        [state_mlstm_m[:, i].astype(F32)[..., None] for i in range(n_odd)], p, db, dseq, dseq)

    new_s5_re = jnp.stack([h[:, :, 0].reshape(nb, 2, ngrp, npst) for h in s5_fin], axis=1)
    new_s5_im = jnp.stack([h[:, :, 1].reshape(nb, 2, ngrp, npst) for h in s5_fin], axis=1)
    new_c = jnp.stack([f[0] for f in ml_fin], axis=1)
    new_n = jnp.stack([f[1] for f in ml_fin], axis=1)
    new_m = jnp.stack([f[2][..., 0] for f in ml_fin], axis=1)
    return (y_prompt.reshape(nb, seq, d), y_sample.reshape(db, dseq, d), new_s5_re, new_s5_im, new_c, new_n, new_m)
```

```python
import functools
import math

import jax
import jax.numpy as jnp
from jax import lax
from jax.experimental import pallas as pl
from jax.experimental.pallas import tpu as pltpu

F32 = jnp.float32
BF16 = jnp.bfloat16
EPS = 1e-6
HIGHEST = lax.Precision.HIGHEST
V7X_VMEM_LIMIT_BYTES = 56 * 1024 * 1024
LANES = 128
SUBLANES = 8
ML_CHUNK = 128
PK_TOPK = 16
GRID_W = 64
NEG_INF = float("-inf")
ROW_TILE = 1024
DFT_TILE = 512
ADA_COLS = 1536
CONV_COLS = 256
S5_CHUNK = 256
PEER_SCORE_TOKENS = 512
PEER_TOKENS = 512
PEER_EXPERTS = 2048


def _cp(*sem):
    return pltpu.CompilerParams(dimension_semantics=sem, vmem_limit_bytes=V7X_VMEM_LIMIT_BYTES)


def _dot(a, b, **kw):
    return jnp.dot(a, b, preferred_element_type=F32, **kw)


def _dot_nt(a, b):
    return lax.dot_general(a, b, (((1,), (1,)), ((), ())), preferred_element_type=F32)


def _silu(x):
    return x * jax.nn.sigmoid(x)


def _ada_kernel(c_ref, w_ref, b_ref, o_ref):
    o_ref[...] = _dot(_silu(c_ref[...]), w_ref[...], precision=HIGHEST) + b_ref[...]


def ada_mod(cond8, w, b):
    d, no = w.shape
    tn = _tile(no, ADA_COLS)
    return pl.pallas_call(
        _ada_kernel, grid=(no // tn,),
        in_specs=[pl.BlockSpec((8, d), lambda j: (0, 0)), pl.BlockSpec((d, tn), lambda j: (0, j)),
                  pl.BlockSpec((1, tn), lambda j: (0, j))],
        out_specs=pl.BlockSpec((8, tn), lambda j: (0, j)),
        out_shape=jax.ShapeDtypeStruct((8, no), F32), compiler_params=_cp("parallel"), name="ada_mod",
    )(cond8, w, b.reshape(1, no))


def _normmod(x, g, sc, sh):
    y = x * lax.rsqrt(jnp.mean(x * x, axis=-1, keepdims=True) + EPS)
    return (y * g) * (1.0 + sc) + sh


def _mod_spec(d, tm, rows_per_mod):
    return pl.BlockSpec((1, 1, d), lambda i, j: ((i * tm) // rows_per_mod, 0, 0))


def _nm_matmul_kernel(x_ref, g_ref, sc_ref, sh_ref, w_ref, o_ref, h_ref):
    @pl.when(pl.program_id(1) == 0)
    def _():
        h_ref[...] = _normmod(x_ref[...], g_ref[...], sc_ref[0], sh_ref[0]).astype(BF16)
    o_ref[...] = _dot(h_ref[...], w_ref[...])


def normmod_matmul(x, g, sc, sh, w_bf16, rows_per_mod, tm, tn):
    n, d = x.shape
    no = w_bf16.shape[1]
    return pl.pallas_call(
        _nm_matmul_kernel, grid=(n // tm, no // tn),
        in_specs=[pl.BlockSpec((tm, d), lambda i, j: (i, 0)), pl.BlockSpec((1, d), lambda i, j: (0, 0)),
                  _mod_spec(d, tm, rows_per_mod), _mod_spec(d, tm, rows_per_mod),
                  pl.BlockSpec((d, tn), lambda i, j: (0, j))],
        out_specs=pl.BlockSpec((tm, tn), lambda i, j: (i, j)),
        out_shape=jax.ShapeDtypeStruct((n, no), F32),
        scratch_shapes=[pltpu.VMEM((tm, d), BF16)],
        compiler_params=_cp("parallel", "arbitrary"), name="normmod_matmul",
    )(x, g.reshape(1, d), sc, sh, w_bf16)


def _add_rows_kernel(x_ref, p_ref, o_ref):
    o_ref[...] = x_ref[...] + p_ref[...]


def add_pos(x, pe, seq_len, tm):
    n, d = x.shape
    nb = seq_len // tm
    return pl.pallas_call(
        _add_rows_kernel, grid=(n // tm,),
        in_specs=[pl.BlockSpec((tm, d), lambda i: (i, 0)), pl.BlockSpec((tm, d), lambda i: (i % nb, 0))],
        out_specs=pl.BlockSpec((tm, d), lambda i: (i, 0)),
        out_shape=jax.ShapeDtypeStruct((n, d), F32), compiler_params=_cp("parallel"), name="add_pos",
    )(x, pe)


def _sconv_kernel(x_ref, w_ref, b_ref, s_ref, o_ref, *, act):
    x = x_ref[...]
    n_tok = x.shape[0]
    row = lax.broadcasted_iota(jnp.int32, x.shape, 0)
    prev = jnp.where(row == 0, 0.0, pltpu.roll(x, 1, 0))
    nxt = jnp.where(row == n_tok - 1, 0.0, pltpu.roll(x, n_tok - 1, 0))
    y = prev * w_ref[0:1, :] + x * w_ref[1:2, :] + nxt * w_ref[2:3, :] + b_ref[...]
    if act:
        y = _silu(y) * s_ref[...]
    o_ref[...] = y.astype(o_ref.dtype)


def short_conv(a, ncols, w, b, scale, seq_len, act, out_dtype):
    n = a.shape[0]
    cb = _tile(ncols, CONV_COLS)
    return pl.pallas_call(
        functools.partial(_sconv_kernel, act=act), grid=(n // seq_len, ncols // cb),
        in_specs=[pl.BlockSpec((seq_len, cb), lambda s, j: (s, j)), pl.BlockSpec((3, cb), lambda s, j: (0, j)),
                  pl.BlockSpec((1, cb), lambda s, j: (0, j)), pl.BlockSpec((1, cb), lambda s, j: (0, j))],
        out_specs=pl.BlockSpec((seq_len, cb), lambda s, j: (s, j)),
        out_shape=jax.ShapeDtypeStruct((n, ncols), out_dtype), compiler_params=_cp("parallel", "parallel"),
        name="short_conv",
    )(a, w, b.reshape(1, ncols), scale.reshape(1, ncols))


def dft_tables(n_tok):
    k = jnp.arange(n_tok, dtype=jnp.int32)
    blk = 1 << ((n_tok.bit_length() - 1) // 2)
    def thin(n):
        ang = ((k[:, None] * n[None, :]) % (2 * n_tok)).astype(F32) * (math.pi / n_tok)
        return jnp.cos(ang), jnp.sin(ang)
    (c_hi, s_hi), (c_lo, s_lo) = thin(jnp.arange(0, n_tok, blk, dtype=jnp.int32)), thin(jnp.arange(blk, dtype=jnp.int32))
    cos_t = (c_hi[:, :, None] * c_lo[:, None, :] - s_hi[:, :, None] * s_lo[:, None, :]).reshape(n_tok, n_tok)
    msin = -(s_hi[:, :, None] * c_lo[:, None, :] + c_hi[:, :, None] * s_lo[:, None, :]).reshape(n_tok, n_tok)
    alt = jnp.where(k % 2 == 0, 1.0, -1.0).astype(F32)
    a_t = msin.at[0, :].set(alt)
    a_tt = msin.at[:, 0].set(alt)
    return cos_t.astype(BF16), a_t.astype(BF16), a_tt.astype(BF16)


def _hyfilt_kernel(band_ref, w1_ref, b1_ref, w2_ref, b2_ref, w3_ref, fr_ref, dec_ref, h_ref, ss_ref, *,
                   n_tok, tl, hw, nbands):
    i = pl.program_id(0)
    pos = i * tl + lax.broadcasted_iota(jnp.int32, (tl, 1), 0)
    t = pos.astype(F32) / n_tok
    lane = lax.broadcasted_iota(jnp.int32, (tl, LANES), 1)
    ang = 2.0 * math.pi * t * band_ref[...]
    z = jnp.where(lane == 0, t, jnp.where(lane <= nbands, jnp.cos(ang),
                                          jnp.where(lane <= 2 * nbands, jnp.sin(ang), 0.0)))
    fr = fr_ref[...]
    h = jnp.sin(fr * (_dot(z, w1_ref[...], precision=HIGHEST) + b1_ref[...]))
    h = jnp.sin(fr * (_dot(h, w2_ref[...], precision=HIGHEST) + b2_ref[...]))
    h = _dot(h, w3_ref[...], precision=HIGHEST) * jnp.exp(-t * jnp.abs(dec_ref[...]))
    col = lax.broadcasted_iota(jnp.int32, h.shape, 1)
    is_bwd = (col // hw) % 2 == 1
    h = jnp.where(jnp.logical_and(is_bwd, pos == 0), 0.0, h)
    h_ref[...] = h.astype(BF16)

    @pl.when(i == 0)
    def _():
        ss_ref[...] = jnp.zeros_like(ss_ref)
    ss_ref[...] += jnp.sum(h * h, axis=0, keepdims=True)


def hyena_filter_taps(n_tok, w1, b1, w2, b2, w3, freq, decay, hw):
    emb, ffn = w1.shape
    nbands = (emb - 1) // 2
    tl = _tile(n_tok, DFT_TILE)
    bands = jnp.linspace(1e-4, nbands - 1, nbands, dtype=F32)
    band_row = jnp.zeros((1, LANES), F32).at[0, 1:1 + nbands].set(bands).at[0, 1 + nbands:1 + 2 * nbands].set(bands)
    w1p = jnp.zeros((LANES, ffn), F32).at[:emb].set(w1)
    nc = w3.shape[1]
    full = lambda shp: pl.BlockSpec(shp, lambda i: (0, 0))
    return pl.pallas_call(
        functools.partial(_hyfilt_kernel, n_tok=n_tok, tl=tl, hw=hw, nbands=nbands), grid=(n_tok // tl,),
        in_specs=[full((1, LANES)), full((LANES, ffn)), full((1, ffn)), full((ffn, ffn)), full((1, ffn)),
                  full((ffn, nc)), full((1, ffn)), full((1, nc))],
        out_specs=[pl.BlockSpec((tl, nc), lambda i: (i, 0)), full((1, nc))],
        out_shape=[jax.ShapeDtypeStruct((n_tok, nc), BF16), jax.ShapeDtypeStruct((1, nc), F32)],
        compiler_params=_cp("arbitrary"), name="hyena_filter_taps",
    )(band_row, w1p, b1.reshape(1, ffn), w2, b2.reshape(1, ffn), w3, freq.reshape(1, ffn), decay.reshape(1, nc))


def _filt_dft_kernel(c_ref, a_ref, h_ref, ss_ref, kr_ref, ki_ref, *, tf, hw):
    i = pl.program_id(1)
    hf = h_ref[:, :hw]
    hb = h_ref[:, hw:]
    cc = c_ref[...]
    aa = a_ref[...]
    zrf, zif, zrb, zib = _dot(cc, hf), _dot(aa, hf), _dot(cc, hb), _dot(aa, hb)
    scale = lax.rsqrt(ss_ref[:, :hw] + ss_ref[:, hw:] + EPS)
    first = (i * tf + lax.broadcasted_iota(jnp.int32, (tf, 1), 0)) == 0
    scale = scale * jnp.where(first, 0.5, 1.0)
    kr_ref[0] = (zrf + zrb) * scale
    ki_ref[0] = jnp.where(first, zif + zib, zif - zib) * scale


def hyena_filter_spectrum(cos_t, a_t, taps, sumsq, hw, tf):
    n_tok = cos_t.shape[0]
    norder = taps.shape[1] // (2 * hw)
    out = jax.ShapeDtypeStruct((norder, n_tok, hw), F32)
    return pl.pallas_call(
        functools.partial(_filt_dft_kernel, tf=tf, hw=hw), grid=(norder, n_tok // tf),
        in_specs=[pl.BlockSpec((tf, n_tok), lambda o, i: (i, 0)), pl.BlockSpec((tf, n_tok), lambda o, i: (i, 0)),
                  pl.BlockSpec((n_tok, 2 * hw), lambda o, i: (0, o)), pl.BlockSpec((1, 2 * hw), lambda o, i: (0, o))],
        out_specs=[pl.BlockSpec((1, tf, hw), lambda o, i: (o, i, 0))] * 2,
        out_shape=[out, out], compiler_params=_cp("parallel", "parallel"), name="hyena_filter_spectrum",
    )(cos_t, a_t, taps, sumsq)


def _hy_fwd_kernel(c_ref, a_ref, z_ref, kr_ref, ki_ref, yr_ref, yi_ref, *, tf):
    i = pl.program_id(0)
    zb = z_ref[...].astype(BF16)
    zr = _dot(c_ref[...], zb)
    zi = _dot(a_ref[...], zb)
    kr = kr_ref[0]
    ki = ki_ref[0]
    first = (i * tf + lax.broadcasted_iota(jnp.int32, (tf, 1), 0)) == 0
    yr_ref[...] = jnp.where(first, zr * kr, zr * kr - zi * ki).astype(BF16)
    yi_ref[...] = jnp.where(first, zi * ki, zr * ki + zi * kr).astype(BF16)


def hyena_fwd(cos_t, a_t, z, zcol, kr, ki, order, nseq, hw, tf):
    n_tok = cos_t.shape[0]
    nf = n_tok // tf
    out = jax.ShapeDtypeStruct((nseq * n_tok, hw), BF16)
    return pl.pallas_call(
        functools.partial(_hy_fwd_kernel, tf=tf), grid=(nf, nseq),
        in_specs=[pl.BlockSpec((tf, n_tok), lambda i, b: (i, 0)), pl.BlockSpec((tf, n_tok), lambda i, b: (i, 0)),
                  pl.BlockSpec((n_tok, hw), lambda i, b: (b, zcol)),
                  pl.BlockSpec((1, tf, hw), lambda i, b: (order, i, 0)),
                  pl.BlockSpec((1, tf, hw), lambda i, b: (order, i, 0))],
        out_specs=[pl.BlockSpec((tf, hw), lambda i, b: (b * nf + i, 0))] * 2,
        out_shape=[out, out], compiler_params=_cp("parallel", "parallel"), name="hyena_fwd",
    )(cos_t, a_t, z, kr, ki)


def _hy_inv_kernel(c_ref, at_ref, yr_ref, yi_ref, zp_ref, gate_ref, bias_ref, o_ref, *, inv_len):
    conv = (_dot(c_ref[...], yr_ref[...]) + _dot(at_ref[...], yi_ref[...])) * inv_len
    o_ref[...] = gate_ref[...] * (conv + bias_ref[...] * zp_ref[...])


def hyena_inv(cos_t, a_tt, yr, yi, zprev, zcol, gates, gcol, bias_row, nseq, hw, tf):
    n_tok = cos_t.shape[0]
    nf = n_tok // tf
    return pl.pallas_call(
        functools.partial(_hy_inv_kernel, inv_len=1.0 / n_tok), grid=(nf, nseq),
        in_specs=[pl.BlockSpec((tf, n_tok), lambda i, b: (i, 0)), pl.BlockSpec((tf, n_tok), lambda i, b: (i, 0)),
                  pl.BlockSpec((n_tok, hw), lambda i, b: (b, 0)), pl.BlockSpec((n_tok, hw), lambda i, b: (b, 0)),
                  pl.BlockSpec((tf, hw), lambda i, b: (b * nf + i, zcol)),
                  pl.BlockSpec((tf, hw), lambda i, b: (b * nf + i, gcol)),
                  pl.BlockSpec((1, hw), lambda i, b: (0, 0))],
        out_specs=pl.BlockSpec((tf, hw), lambda i, b: (b * nf + i, 0)),
        out_shape=jax.ShapeDtypeStruct((nseq * n_tok, hw), F32),
        compiler_params=_cp("parallel", "parallel"), name="hyena_inv",
    )(cos_t, a_tt, yr, yi, zprev, gates, bias_row)


S5_DIAG_BLOCKS = 2


S5_SEQS_PER_STEP = 4


def _s5_kernel(u_ref, bre_ref, bim_ref, cre_ref, cim_ref, lam_ref, h0_ref, y_ref, hfin_ref, hre_s, him_s, st_s, *,
               tc, nc, ns, nb):
    d = pl.program_id(0)
    c = pl.program_id(2)

    @pl.when(c == 0)
    def _():
        st_s[...] = h0_ref[:, 0]

    sw = u_ref.shape[2]
    halves = [(slice(j * sw // S5_DIAG_BLOCKS, (j + 1) * sw // S5_DIAG_BLOCKS),
               slice(j * ns // S5_DIAG_BLOCKS, (j + 1) * ns // S5_DIAG_BLOCKS)) for j in range(S5_DIAG_BLOCKS)]
    for j in range(nb):
        ub = u_ref[j].astype(BF16)
        for us, hs in halves:
            hre_s[j, :, hs] = _dot(ub[:, us], bre_ref[0, us, hs])
            him_s[j, :, hs] = _dot(ub[:, us], bim_ref[0, us, hs])
    lr = lam_ref[0, 0:1, :]
    li = lam_ref[0, 1:2, :]

    def body(t, carry):
        r = jnp.where(d == 0, t, tc - 1 - t)
        new = []
        for j, (hr, hi) in enumerate(carry):
            nr = lr * hr - li * hi + hre_s[j, pl.ds(r, 1), :]
            ni = lr * hi + li * hr + him_s[j, pl.ds(r, 1), :]
            hre_s[j, pl.ds(r, 1), :] = nr
            him_s[j, pl.ds(r, 1), :] = ni
            new.append((nr, ni))
        return tuple(new)

    start = tuple((st_s[j, 0:1, :], st_s[j, 1:2, :]) for j in range(nb))
    for j, (hr, hi) in enumerate(lax.fori_loop(0, tc, body, start, unroll=4)):
        st_s[j, 0:1, :] = hr
        st_s[j, 1:2, :] = hi
    for j in range(nb):
        for us, hs in halves:
            y_ref[0, j, :, us] = (_dot(hre_s[j, :, hs].astype(BF16), cre_ref[0, hs, us])
                                  + _dot(him_s[j, :, hs].astype(BF16), cim_ref[0, hs, us]))

    @pl.when(c == nc - 1)
    def _():
        hfin_ref[:, 0] = st_s[...]


def s5_scan(proj, ucol, bre, bim, cre, cim, lam, h0, nseq, seq_len, sw, tc):
    ns = bre.shape[2]
    nc = seq_len // tc
    nb = S5_SEQS_PER_STEP if nseq % S5_SEQS_PER_STEP == 0 else 1

    def chunk(d, c):
        return c + d * (nc - 1 - 2 * c)

    y, hfin = pl.pallas_call(
        functools.partial(_s5_kernel, tc=tc, nc=nc, ns=ns, nb=nb), grid=(2, nseq // nb, nc),
        in_specs=[pl.BlockSpec((nb, tc, sw), lambda d, b, c: (b, chunk(d, c), ucol)),
                  pl.BlockSpec((1, sw, ns), lambda d, b, c: (d, 0, 0)),
                  pl.BlockSpec((1, sw, ns), lambda d, b, c: (d, 0, 0)),
                  pl.BlockSpec((1, ns, sw), lambda d, b, c: (d, 0, 0)),
                  pl.BlockSpec((1, ns, sw), lambda d, b, c: (d, 0, 0)),
                  pl.BlockSpec((1, 2, ns), lambda d, b, c: (d, 0, 0)),
                  pl.BlockSpec((nb, 1, 2, ns), lambda d, b, c: (b, d, 0, 0))],
        out_specs=[pl.BlockSpec((1, nb, tc, sw), lambda d, b, c: (d, b, chunk(d, c), 0)),
                   pl.BlockSpec((nb, 1, 2, ns), lambda d, b, c: (b, d, 0, 0))],
        out_shape=[jax.ShapeDtypeStruct((2, nseq, seq_len, sw), F32), jax.ShapeDtypeStruct((nseq, 2, 2, ns), F32)],
        scratch_shapes=[pltpu.VMEM((nb, tc, ns), F32), pltpu.VMEM((nb, tc, ns), F32), pltpu.VMEM((nb, 2, ns), F32)],
        compiler_params=_cp("parallel", "parallel", "arbitrary"), name="s5_scan",
    )(proj.reshape(nseq, seq_len, proj.shape[1]), bre, bim, cre, cim, lam, h0)
    return y.reshape(2, nseq * seq_len, sw), hfin


def _residual_and_next_norm(x_ref, gate_ref, y, g2_ref, sc2_ref, sh2_ref, o_ref, hn_ref):
    xn = x_ref[...] + gate_ref[0] * y
    o_ref[...] = xn
    hn_ref[...] = _normmod(xn, g2_ref[...], sc2_ref[0], sh2_ref[0]).astype(BF16)


def _row_specs(d, tm, rows_per_mod):
    mod = pl.BlockSpec((1, 1, d), lambda i: ((i * tm) // rows_per_mod, 0, 0))
    return [pl.BlockSpec((tm, d), lambda i: (i, 0)), mod, pl.BlockSpec((1, d), lambda i: (0, 0)), mod, mod]


def _row_outs(n, d, tm):
    spec = pl.BlockSpec((tm, d), lambda i: (i, 0))
    return [spec, spec], [jax.ShapeDtypeStruct((n, d), F32), jax.ShapeDtypeStruct((n, d), BF16)]


def _even_out_kernel(a_ref, yf_ref, yb_ref, u_ref, ds_ref, gw_ref, gb_ref, wa_ref, wb_ref, x_ref, gate_ref,
                     g2_ref, sc2_ref, sh2_ref, o_ref, hn_ref):
    ys = jax.nn.gelu(yf_ref[0] + yb_ref[0] + ds_ref[...] * u_ref[...])
    s5o = ys * jax.nn.sigmoid(_dot(ys.astype(BF16), gw_ref[...]) + gb_ref[...])
    y = _dot(a_ref[...].astype(BF16), wa_ref[...]) + _dot(s5o.astype(BF16), wb_ref[...])
    _residual_and_next_norm(x_ref, gate_ref, y, g2_ref, sc2_ref, sh2_ref, o_ref, hn_ref)


def even_out(hy, y2, proj, ucol, d_skip, glu_w_bf16, glu_b, w_bf16, x, gate, g2, sc2, sh2, rows_per_mod, tm):
    n, d = x.shape
    hw = hy.shape[1]
    sw = y2.shape[2]
    out_specs, out_shape = _row_outs(n, d, tm)
    return pl.pallas_call(
        _even_out_kernel, grid=(n // tm,),
        in_specs=[pl.BlockSpec((tm, hw), lambda i: (i, 0)),
                  pl.BlockSpec((1, tm, sw), lambda i: (0, i, 0)), pl.BlockSpec((1, tm, sw), lambda i: (1, i, 0)),
                  pl.BlockSpec((tm, sw), lambda i: (i, ucol)), pl.BlockSpec((1, sw), lambda i: (0, 0)),
                  pl.BlockSpec((sw, sw), lambda i: (0, 0)), pl.BlockSpec((1, sw), lambda i: (0, 0)),
                  pl.BlockSpec((hw, d), lambda i: (0, 0)), pl.BlockSpec((sw, d), lambda i: (hw // sw, 0))]
        + _row_specs(d, tm, rows_per_mod),
        out_specs=out_specs, out_shape=out_shape, compiler_params=_cp("parallel"), name="even_out",
    )(hy, y2, y2, proj, d_skip.reshape(1, sw), glu_w_bf16, glu_b.reshape(1, sw), w_bf16, w_bf16, x, gate,
      g2.reshape(1, d), sc2, sh2)


ML_HEAD_GROUP = 8


def _log_sigmoid(x):
    return jnp.minimum(x, 0.0) - jnp.log1p(jnp.exp(-jnp.abs(x)))


def _mlstm_kernel(q_ref, k_ref, v_ref, g_ref, gb_ref, c0_ref, n0_ref, m0_ref, h_ref, cf_ref, nf_ref, mf_ref,
                  c_s, m_s, *, nh, dh, tc, nc):
    d = pl.program_id(0)
    c = pl.program_id(2)

    @pl.when(c == 0)
    def _():
        for h in range(nh):
            c_s[h, :, :dh] = c0_ref[0, 0, h]
            c_s[h, :, dh:] = jnp.broadcast_to(n0_ref[0, 0, h:h + 1, :], (dh, dh)).T
        m_s[...] = m0_ref[0, 0]

    ones = jnp.ones((tc, dh), F32)
    gates = g_ref[...] + gb_ref[0]
    lane = lax.broadcasted_iota(jnp.int32, gates.shape, 1)
    logf = jnp.where(jnp.logical_and(lane >= nh, lane < 2 * nh), _log_sigmoid(gates), 0.0)
    r_i = lax.broadcasted_iota(jnp.int32, (tc, tc), 0)
    s_i = lax.broadcasted_iota(jnp.int32, (tc, tc), 1)
    causal = (r_i - s_i) * (1 - 2 * d) >= 0
    bcum = _dot(causal.astype(F32), logf, precision=HIGHEST)
    btot = jnp.sum(logf, axis=0, keepdims=True)
    gates_t = gates.T
    bcum_t = bcum.T
    for g0 in range(0, nh, ML_HEAD_GROUP):
        hds = list(range(g0, min(g0 + ML_HEAD_GROUP, nh)))
        hsl = {h: slice(h * dh, (h + 1) * dh) for h in hds}
        b_col = {h: bcum[:, nh + h:nh + h + 1] for h in hds}
        m_old = {h: m_s[h:h + 1, :] for h in hds}
        a = {h: b_col[h] + m_old[h] for h in hds}
        src = {h: jnp.where(causal, gates_t[h:h + 1, :] - bcum_t[nh + h:nh + h + 1, :], NEG_INF) for h in hds}
        mq = {h: jnp.maximum(a[h], b_col[h] + jnp.max(src[h], axis=-1, keepdims=True)) for h in hds}
        rel = {h: jnp.broadcast_to(b_col[h] - mq[h], (tc, tc)) for h in hds}
        s = {h: _dot_nt(q_ref[:, hsl[h]], k_ref[:, hsl[h]]) * jnp.exp(src[h] + rel[h]) for h in hds}
        v1 = {h: jnp.concatenate([v_ref[:, hsl[h]], ones], axis=1) for h in hds}
        cn = {h: c_s[h] for h in hds}
        qw = {h: jnp.exp(rel[h][:, :dh] + m_old[h]) * q_ref[:, hsl[h]] for h in hds}
        both = {h: _dot(s[h], v1[h]) + _dot(qw[h], cn[h]) for h in hds}
        for h in hds:
            h_ref[0, :, hsl[h]] = both[h][:, :dh] / jnp.maximum(jnp.abs(both[h][:, dh:]), jnp.exp(-mq[h]))
        b_last = {h: btot[:, nh + h:nh + h + 1] for h in hds}
        g = {h: b_last[h] - bcum_t[nh + h:nh + h + 1, :] + gates_t[h:h + 1, :] for h in hds}
        m_new = {h: jnp.maximum(b_last[h] + m_old[h], jnp.max(g[h], axis=1, keepdims=True)) for h in hds}
        kw_t = {h: k_ref[:, hsl[h]].astype(F32).T * jnp.exp(g[h] - m_new[h]) for h in hds}
        for h in hds:
            c_s[h] = jnp.exp(b_last[h] + m_old[h] - m_new[h]) * cn[h] + _dot(kw_t[h], v1[h])
            m_s[h:h + 1, :] = m_new[h]

    @pl.when(c == nc - 1)
    def _():
        for h in range(nh):
            cf_ref[0, 0, h] = c_s[h, :, :dh]
            nf_ref[0, 0, h:h + 1, :] = c_s[h, :, dh:].T[0:1, :]
        mf_ref[0, 0] = m_s[...]


def mlstm_scan(qk, proj, vcol, gate_bias, c0, n0, m0, nseq, seq_len, nh, dh):
    tc = ML_CHUNK
    nc = seq_len // tc
    w = nh * dh

    def chunk(d, c):
        return c + d * (nc - 1 - 2 * c)

    rowblk = lambda d, b, c: b * nc + chunk(d, c)
    st = lambda shp: pl.BlockSpec((1, 1) + shp, lambda d, b, c: (b, d) + (0,) * len(shp))
    return pl.pallas_call(
        functools.partial(_mlstm_kernel, nh=nh, dh=dh, tc=tc, nc=nc), grid=(2, nseq, nc),
        in_specs=[pl.BlockSpec((tc, w), lambda d, b, c: (rowblk(d, b, c), 0)),
                  pl.BlockSpec((tc, w), lambda d, b, c: (rowblk(d, b, c), 1)),
                  pl.BlockSpec((tc, w), lambda d, b, c: (rowblk(d, b, c), vcol)),
                  pl.BlockSpec((tc, LANES), lambda d, b, c: (rowblk(d, b, c), 4 * w // LANES + d)),
                  pl.BlockSpec((1, 1, LANES), lambda d, b, c: (d, 0, 0)),
                  st((nh, dh, dh)), st((nh, dh)), st((nh, 1))],
        out_specs=[pl.BlockSpec((1, tc, w), lambda d, b, c: (d, rowblk(d, b, c), 0)),
                   st((nh, dh, dh)), st((nh, dh)), st((nh, 1))],
        out_shape=[jax.ShapeDtypeStruct((2, nseq * seq_len, w), F32),
                   jax.ShapeDtypeStruct((nseq, 2, nh, dh, dh), F32), jax.ShapeDtypeStruct((nseq, 2, nh, dh), F32),
                   jax.ShapeDtypeStruct((nseq, 2, nh, 1), F32)],
        scratch_shapes=[pltpu.VMEM((nh, dh, 2 * dh), F32), pltpu.VMEM((nh, 1), F32)],
        compiler_params=_cp("parallel", "parallel", "arbitrary"), name="mlstm_scan",
    )(qk, qk, proj, proj, gate_bias, c0, n0, m0)


def _odd_out_kernel(hf_ref, hb_ref, og_ref, ng_ref, w_ref, x_ref, gate_ref, g2_ref, sc2_ref, sh2_ref, o_ref, hn_ref,
                    a_s, *, nh, dh):
    for h in range(nh):
        hs = slice(h * dh, (h + 1) * dh)
        blk = hf_ref[0, :, hs] + hb_ref[0, :, hs]
        blk = blk * lax.rsqrt(jnp.mean(blk * blk, axis=-1, keepdims=True) + EPS)
        a_s[:, hs] = ((blk * ng_ref[:, hs]) * _silu(og_ref[:, hs])).astype(BF16)
    _residual_and_next_norm(x_ref, gate_ref, _dot(a_s[...], w_ref[...]), g2_ref, sc2_ref, sh2_ref, o_ref, hn_ref)


def odd_out(h2, proj, ocol, norm_g, w_bf16, x, gate, g2, sc2, sh2, rows_per_mod, nh, dh, tm):
    n, d = x.shape
    w = nh * dh
    out_specs, out_shape = _row_outs(n, d, tm)
    return pl.pallas_call(
        functools.partial(_odd_out_kernel, nh=nh, dh=dh), grid=(n // tm,),
        in_specs=[pl.BlockSpec((1, tm, w), lambda i: (0, i, 0)), pl.BlockSpec((1, tm, w), lambda i: (1, i, 0)),
                  pl.BlockSpec((tm, w), lambda i: (i, ocol)), pl.BlockSpec((1, w), lambda i: (0, 0)),
                  pl.BlockSpec((w, d), lambda i: (0, 0))] + _row_specs(d, tm, rows_per_mod),
        out_specs=out_specs, out_shape=out_shape, scratch_shapes=[pltpu.VMEM((tm, w), BF16)],
        compiler_params=_cp("parallel"), name="odd_out",
    )(h2, h2, proj, norm_g.reshape(1, w), w_bf16, x, gate, g2.reshape(1, d), sc2, sh2)


def _sort_network(n):
    pairs, p = [], 1
    while p < n:
        k = p
        while k >= 1:
            for j in range(k % p, n - k, 2 * k):
                for i in range(min(k, n - j - k)):
                    if (i + j) // (2 * p) == (i + j + k) // (2 * p):
                        pairs.append((i + j, i + j + k))
            k //= 2
        p *= 2
    return pairs


def _top_values_tiled(arrays, k, outs):
    cols = []
    for arr in arrays:
        tiles = [arr[r:r + SUBLANES, :] for r in range(0, arr.shape[0], SUBLANES)]
        for lo, hi in _sort_network(len(tiles)):
            tiles[lo], tiles[hi] = jnp.maximum(tiles[lo], tiles[hi]), jnp.minimum(tiles[lo], tiles[hi])
        cols.append(tiles)
    for j in range(k):
        for tiles, out_s in zip(cols, outs):
            m = jnp.max(tiles[0], axis=0, keepdims=True)
            out_s[j:j + 1, :] = m
            hit = tiles[0] == m
            for i in range(min(len(tiles), k - 1 - j)):
                below = tiles[i + 1] if i + 1 < len(tiles) else NEG_INF
                tiles[i] = jnp.where(hit, below, tiles[i])


def _pair_candidates(k):
    return [(a, k // (a + 1)) for a in range(k)]


PEER_HEADS_PER_TRIP = 8


def _peer_score_kernel(h_ref, wq_ref, keys_ref, th_ref, s2_ref, w1_ref, w2_ref, q_s, v_s, cand_s, best_s, *,
                       nh, half, topk):
    q_s[...] = _dot_nt(wq_ref[...], h_ref[...])
    kk = topk + 1
    cand_s[...] = jnp.full(cand_s.shape, NEG_INF, F32)
    group = range(PEER_HEADS_PER_TRIP)

    def heads(trip, carry):
        hds = [trip * PEER_HEADS_PER_TRIP + u for u in group]
        scores = []
        for hd in hds:
            base = pl.multiple_of(hd * 2 * half, 2 * half)
            scores.append(_dot(keys_ref[hd, 0], q_s[pl.ds(base, half), :]))
            scores.append(_dot(keys_ref[hd, 1], q_s[pl.ds(base + half, half), :]))
        _top_values_tiled(scores, kk, [v_s.at[u, c] for u in group for c in range(2)])
        for u in group:
            off = 0
            for a, cnt in _pair_candidates(kk):
                cand_s[u, off:off + cnt, :] = v_s[u, 0, a:a + 1, :] + v_s[u, 1, 0:cnt, :]
                off += cnt
        _top_values_tiled([cand_s[u] for u in group], kk, [best_s.at[u] for u in group])
        for u, hd in enumerate(hds):
            s1, s2 = scores[2 * u], scores[2 * u + 1]
            best = best_s[u, 0:topk, :]
            z = jnp.sum(jnp.exp(best - best[0:1, :]), axis=0, keepdims=True)
            tmid = 0.5 * (best_s[u, topk - 1:topk, :] + best_s[u, topk:topk + 1, :])
            th = tmid - s1
            w1 = jnp.exp(s1 - v_s[u, 0, 0:1, :]) / z
            w2 = jnp.exp(s2 - v_s[u, 1, 0:1, :])
            for lt in range(s1.shape[1] // LANES):
                sl = slice(lt * LANES, (lt + 1) * LANES)
                th_ref[hd, lt] = th[:, sl]
                s2_ref[hd, lt] = s2[:, sl]
                w1_ref[hd, lt] = w1[:, sl]
                w2_ref[hd, lt] = w2[:, sl]
        return carry

    lax.fori_loop(0, nh // PEER_HEADS_PER_TRIP, heads, 0)


def peer_scores(h_bf16, wq_t_bf16, keys, tt):
    n, d = h_bf16.shape
    nh, _, nk, half = keys.shape
    kk = PK_TOPK + 1
    top_rows = -(-kk // SUBLANES) * SUBLANES
    ncand = SUBLANES
    while ncand < sum(c for _, c in _pair_candidates(kk)):
        ncand *= 2
    big = jax.ShapeDtypeStruct((nh, n // LANES, nk, LANES), F32)
    bspec = pl.BlockSpec((nh, tt // LANES, nk, LANES), lambda i: (0, i, 0, 0))
    return pl.pallas_call(
        functools.partial(_peer_score_kernel, nh=nh, half=half, topk=PK_TOPK), grid=(n // tt,),
        in_specs=[pl.BlockSpec((tt, d), lambda i: (i, 0)), pl.BlockSpec((nh * 2 * half, d), lambda i: (0, 0)),
                  pl.BlockSpec((nh, 2, nk, half), lambda i: (0, 0, 0, 0))],
        out_specs=[bspec, bspec, bspec, bspec],
        out_shape=[big, big, big, big],
        scratch_shapes=[pltpu.VMEM((nh * 2 * half, tt), F32), pltpu.VMEM((PEER_HEADS_PER_TRIP, 2, top_rows, tt), F32),
                        pltpu.VMEM((PEER_HEADS_PER_TRIP, ncand, tt), F32),
                        pltpu.VMEM((PEER_HEADS_PER_TRIP, top_rows, tt), F32)],
        compiler_params=_cp("parallel"), name="peer_scores",
    )(h_bf16, wq_t_bf16, keys)


PEER_KEY_ROWS = 16
GELU_C1 = math.sqrt(2.0 / math.pi)
GELU_C2 = 0.044715 * GELU_C1


def _gelu_tanh(x):
    half_x = 0.5 * x
    return half_x + half_x * jnp.tanh(x * (GELU_C1 + GELU_C2 * (x * x)))


def _peer_dense_kernel(h_ref, u_ref, vt_ref, th_ref, s2_ref, w1_ref, w2_ref, x_ref, gate_ref, fg_ref, o_ref,
                       acc_s, st_s, wt_s, *, nh, nk, ec, tt, final):
    e = pl.program_id(1)
    nlt = tt // LANES
    n_i1 = ec // nk
    nkt = nk // PEER_KEY_ROWS
    nsub = PEER_KEY_ROWS // SUBLANES

    @pl.when(e == 0)
    def _():
        acc_s[...] = jnp.zeros_like(acc_s)

    st = _gelu_tanh(_dot_nt(u_ref[...], h_ref[...]))
    for lt in range(nlt):
        st_s[lt] = st[:, lt * LANES:(lt + 1) * LANES]

    def tile(idx, carry):
        lt = idx // nkt
        k0 = (idx % nkt) * PEER_KEY_ROWS
        subs = [pl.ds(pl.multiple_of(k0 + j * SUBLANES, SUBLANES), SUBLANES) for j in range(nsub)]
        g = [[jnp.zeros((SUBLANES, LANES), F32) for _ in subs] for _ in range(n_i1)]
        for hd in range(nh):
            s2t = [s2_ref[hd, lt, sub, :] for sub in subs]
            w2t = [w2_ref[hd, lt, sub, :] for sub in subs]
            for li in range(n_i1):
                thb = jnp.broadcast_to(th_ref[hd, lt, li:li + 1, :], (SUBLANES, LANES))
                w1b = jnp.broadcast_to(w1_ref[hd, lt, li:li + 1, :], (SUBLANES, LANES))
                for j in range(nsub):
                    g[li][j] = g[li][j] + jnp.where(s2t[j] >= thb, w2t[j] * w1b, 0.0)
        for li in range(n_i1):
            rows = pl.ds(pl.multiple_of(li * nk + k0, PEER_KEY_ROWS), PEER_KEY_ROWS)
            wt_s[lt, rows, :] = (st_s[lt, rows, :] * jnp.concatenate(g[li], axis=0)).astype(BF16)
        return carry

    lax.fori_loop(0, nlt * nkt, tile, 0, unroll=2)
    wt = jnp.concatenate([wt_s[lt] for lt in range(nlt)], axis=1)
    acc_s[...] += _dot(vt_ref[0], wt)

    @pl.when(e == pl.num_programs(1) - 1)
    def _():
        xn = x_ref[...] + gate_ref[0] * acc_s[...].T
        if final:
            xn = (xn * lax.rsqrt(jnp.mean(xn * xn, axis=-1, keepdims=True) + EPS)) * fg_ref[...]
        o_ref[...] = xn


def peer_dense(h_bf16, u_bf16, vt_bf16, th, s2, w1, w2, x, gate, final_g, final, rows_per_mod, tt, ec):
    n, d = x.shape
    nh, _, nk, _ = s2.shape
    nchunk = u_bf16.shape[0] // ec
    bspec = pl.BlockSpec((nh, tt // LANES, nk, LANES), lambda i, e: (0, i, 0, 0))
    rspec = pl.BlockSpec((nh, tt // LANES, ec // nk, LANES), lambda i, e: (0, i, e, 0))
    tile_buf = (tt // LANES, ec, LANES)
    return pl.pallas_call(
        functools.partial(_peer_dense_kernel, nh=nh, nk=nk, ec=ec, tt=tt, final=final), grid=(n // tt, nchunk),
        in_specs=[pl.BlockSpec((tt, d), lambda i, e: (i, 0)), pl.BlockSpec((ec, d), lambda i, e: (e, 0)),
                  pl.BlockSpec((1, d, ec), lambda i, e: (e, 0, 0)), rspec, bspec, rspec, bspec,
                  pl.BlockSpec((tt, d), lambda i, e: (i, 0)),
                  pl.BlockSpec((1, 1, d), lambda i, e: ((i * tt) // rows_per_mod, 0, 0)),
                  pl.BlockSpec((1, d), lambda i, e: (0, 0))],
        out_specs=pl.BlockSpec((tt, d), lambda i, e: (i, 0)),
        out_shape=jax.ShapeDtypeStruct((n, d), F32),
        scratch_shapes=[pltpu.VMEM((d, tt), F32), pltpu.VMEM(tile_buf, F32), pltpu.VMEM(tile_buf, BF16)],
        compiler_params=_cp("parallel", "arbitrary"), name="peer_dense",
    )(h_bf16, u_bf16, vt_bf16, th, s2, w1, w2, x, gate, final_g.reshape(1, d))


def _s5_params(a_re, a_im, b_re, b_im, c_re, c_im, log_step):
    lam = lax.complex(a_re.astype(F32), a_im.astype(F32))
    lam_bar = jnp.exp(lam * jnp.exp(log_step.astype(F32))[..., None])
    b_bar = ((lam_bar - 1.0) / lam)[..., None] * lax.complex(b_re.astype(F32), b_im.astype(F32))
    ngrp, npst, nch = b_bar.shape[1:]
    eye = jnp.eye(ngrp, dtype=F32)

    def b_mat(part):
        return jnp.einsum("dgpj,gh->dgjhp", part, eye).reshape(2, ngrp * nch, ngrp * npst)

    def c_mat(part):
        return jnp.einsum("dgjp,gh->dgphj", part, eye).reshape(2, ngrp * npst, ngrp * nch)

    bre, bim = b_mat(b_bar.real).astype(BF16), b_mat(b_bar.imag).astype(BF16)
    cre, cim = c_mat(c_re.astype(F32)).astype(BF16), c_mat(-c_im.astype(F32)).astype(BF16)
    lam2 = jnp.stack([lam_bar.real.reshape(2, -1), lam_bar.imag.reshape(2, -1)], axis=1)
    return bre, bim, cre, cim, lam2


def _pos_embed(n_tok, d, grid_w):
    rows = n_tok // grid_w
    quarter = d // 4
    omega = 1.0 / (10000.0 ** (jnp.arange(quarter, dtype=F32) / quarter))

    def emb1d(pos):
        ang = pos.astype(F32)[:, None] * omega[None]
        return jnp.concatenate([jnp.sin(ang), jnp.cos(ang)], axis=-1)

    er = emb1d(jnp.arange(rows))
    ec = emb1d(jnp.arange(grid_w))
    half = d // 2
    pe = jnp.concatenate([jnp.broadcast_to(er[:, None], (rows, grid_w, half)),
                          jnp.broadcast_to(ec[None], (rows, grid_w, half))], axis=-1)
    return pe.reshape(rows * grid_w, d)


def _tile(n, pref):
    return pref if n % pref == 0 else n


def _trunk(x, mods, s5_h0, ml_c0, ml_n0, ml_m0, p, nseq, seq_len, rows_per_mod):
    n, d = x.shape
    tm = _tile(min(rows_per_mod, n), ROW_TILE)
    depth = p["norm_g"].shape[0]
    s5_fin, ml_fin = [], []
    for l in range(depth):
        sh1, sc1, g1, sh2, sc2, g2 = mods[l]
        i = l // 2
        if l % 2 == 0:
            hw = p["hy_bias"].shape[2]
            sw = p["s5_d"].shape[1]
            proj = normmod_matmul(x, p["norm_g"][l, 0], sc1, sh1, p["ev_w_in"][i].astype(BF16), rows_per_mod, tm,
                                  3 * hw + sw)
            hy_in = short_conv(proj, 3 * hw, p["hy_conv_w"][i], p["hy_conv_b"][i], jnp.ones((3 * hw,), F32),
                               seq_len, act=False, out_dtype=F32)
            cos_t, a_t, a_tt = dft_tables(seq_len)
            tf = _tile(seq_len, DFT_TILE)
            taps, sumsq = hyena_filter_taps(seq_len, p["hy_w1"][i], p["hy_b1"][i], p["hy_w2"][i], p["hy_b2"][i],
                                            p["hy_w3"][i], p["hy_freq"][i], p["hy_decay"][i], hw)
            kr, ki = hyena_filter_spectrum(cos_t, a_t, taps, sumsq, hw, tf)
            bias = p["hy_bias"][i].astype(F32)
            z, zcol = hy_in, 0
            for o in range(bias.shape[0]):
                yr, yi = hyena_fwd(cos_t, a_t, z, zcol, kr, ki, o, nseq, hw, tf)
                z = hyena_inv(cos_t, a_tt, yr, yi, z, zcol, hy_in, 1 + o, bias[o:o + 1], nseq, hw, tf)
                zcol = 0
            bre, bim, cre, cim, lam2 = _s5_params(p["s5_a_re"][i], p["s5_a_im"][i], p["s5_b_re"][i], p["s5_b_im"][i],
                                                  p["s5_c_re"][i], p["s5_c_im"][i], p["s5_log_step"][i])
            ucol = 3 * hw // sw
            y2, hfin = s5_scan(proj, ucol, bre, bim, cre, cim, lam2, s5_h0[i], nseq, seq_len, sw,
                               _tile(seq_len, S5_CHUNK))
            s5_fin.append(hfin)
            x, hn = even_out(z, y2, proj, ucol, p["s5_d"][i], p["s5_glu_w"][i].astype(BF16), p["s5_glu_b"][i],
                             p["ev_w_out"][i].astype(BF16), x, g1, p["norm_g"][l, 1], sc2, sh2, rows_per_mod, tm)
        else:
            nh = p["od_gate_b"].shape[2]
            w = p["ml_norm_g"].shape[1]
            dh = w // nh
            w_in = p["od_w_in"][i]
            wg = w_in[:, 4 * w:].reshape(d, 4, nh)
            gb = p["od_gate_b"][i].astype(F32)
            wg2 = jnp.zeros((d, 2, LANES), w_in.dtype)
            bg2 = jnp.zeros((2, 1, LANES), F32)
            for dr in range(2):
                wg2 = wg2.at[:, dr, :nh].set(wg[:, dr]).at[:, dr, nh:2 * nh].set(wg[:, 2 + dr])
                bg2 = bg2.at[dr, 0, :nh].set(gb[dr]).at[dr, 0, nh:2 * nh].set(gb[2 + dr])
            w_all = jnp.concatenate([w_in[:, :4 * w], wg2.reshape(d, 2 * LANES)], axis=1).astype(BF16)
            proj = normmod_matmul(x, p["norm_g"][l, 0], sc1, sh1, w_all, rows_per_mod, tm, w_all.shape[1] // 2)
            qscale = jnp.concatenate([jnp.full((w,), dh ** -0.5, F32), jnp.ones((w,), F32)])
            qk = short_conv(proj, 2 * w, p["ml_conv_w"][i], p["ml_conv_b"][i], qscale, seq_len, act=True, out_dtype=BF16)
            h2, cf, nf, mf = mlstm_scan(qk, proj, 2, bg2, ml_c0[i], ml_n0[i], ml_m0[i], nseq, seq_len, nh, dh)
            ml_fin.append((cf, nf, mf))
            x, hn = odd_out(h2, proj, 3, p["ml_norm_g"][i], p["od_w_out"][i].astype(BF16), x, g1, p["norm_g"][l, 1],
                            sc2, sh2, rows_per_mod, nh, dh, tm)
        tt = _tile(min(rows_per_mod, n), PEER_TOKENS)
        th, s2, w1, w2 = peer_scores(hn, p["pk_w_q"][l].T.astype(BF16), p["pk_keys"][l].astype(F32),
                                     _tile(tt, PEER_SCORE_TOKENS))
        ec = _tile(p["pk_u"].shape[1], PEER_EXPERTS)
        vt = p["pk_v"][l].astype(BF16).reshape(-1, ec, d).transpose(0, 2, 1)
        x = peer_dense(hn, p["pk_u"][l].astype(BF16), vt, th, s2, w1, w2, x, g2, p["final_g"], l == depth - 1,
                       rows_per_mod, tt, ec)
    return x, s5_fin, ml_fin


def kernel(x_prompt, x_sample, state_s5_re, state_s5_im, state_mlstm_C, state_mlstm_n, state_mlstm_m, c, c_ctx, norm_g, ada_w, ada_b, final_g, ev_w_in, hy_conv_w, hy_conv_b, hy_w1, hy_b1, hy_w2, hy_b2, hy_w3, hy_freq, hy_decay, hy_bias, s5_a_re, s5_a_im, s5_b_re, s5_b_im, s5_c_re, s5_c_im, s5_log_step, s5_d, s5_glu_w, s5_glu_b, ev_w_out, od_w_in, od_gate_b, ml_conv_w, ml_conv_b, ml_norm_g, od_w_out, pk_w_q, pk_keys, pk_u, pk_v):
    p = dict(norm_g=norm_g, ada_w=ada_w, ada_b=ada_b, final_g=final_g, ev_w_in=ev_w_in,
             hy_conv_w=hy_conv_w, hy_conv_b=hy_conv_b, hy_w1=hy_w1, hy_b1=hy_b1, hy_w2=hy_w2, hy_b2=hy_b2,
             hy_w3=hy_w3, hy_freq=hy_freq, hy_decay=hy_decay, hy_bias=hy_bias, s5_a_re=s5_a_re,
             s5_a_im=s5_a_im, s5_b_re=s5_b_re, s5_b_im=s5_b_im, s5_c_re=s5_c_re, s5_c_im=s5_c_im,
             s5_log_step=s5_log_step, s5_d=s5_d, s5_glu_w=s5_glu_w, s5_glu_b=s5_glu_b, ev_w_out=ev_w_out,
             od_w_in=od_w_in, od_gate_b=od_gate_b, ml_conv_w=ml_conv_w, ml_conv_b=ml_conv_b,
             ml_norm_g=ml_norm_g, od_w_out=od_w_out, pk_w_q=pk_w_q, pk_keys=pk_keys, pk_u=pk_u, pk_v=pk_v)
    nb, seq, d = x_prompt.shape
    db, dseq, _ = x_sample.shape
    depth = norm_g.shape[0]
    n_even, n_odd = (depth + 1) // 2, depth // 2
    assert db + 1 <= 8

    cond8 = jnp.zeros((8, d), F32).at[0].set(c_ctx.astype(F32)).at[1:1 + db].set(c.astype(F32))
    mods_ctx, mods_lat = [], []
    for l in range(depth):
        mod = ada_mod(cond8, ada_w[l].astype(F32), ada_b[l].astype(F32))
        chunks = [mod[:, j * d:(j + 1) * d] for j in range(6)]
        mods_ctx.append([ch[0:1].reshape(1, 1, d) for ch in chunks])
        mods_lat.append([ch[1:1 + db].reshape(db, 1, d) for ch in chunks])

    def s5_state(re, im, bsz):
        return [jnp.stack([re[:, i].reshape(bsz, 2, -1), im[:, i].reshape(bsz, 2, -1)], axis=2).astype(F32)
                for i in range(n_even)]

    ngrp, npst = s5_a_re.shape[2], s5_a_re.shape[3]
    nh, dh = state_mlstm_C.shape[3], state_mlstm_C.shape[4]
    zeros_s5 = jnp.zeros((nb, n_even, 2, ngrp, npst), F32)
    y_prompt, s5_fin, ml_fin = _trunk(
        x_prompt.reshape(nb * seq, d), mods_ctx, s5_state(zeros_s5, zeros_s5, nb),
        [jnp.zeros((nb, 2, nh, dh, dh), F32)] * n_odd, [jnp.zeros((nb, 2, nh, dh), F32)] * n_odd,
        [jnp.zeros((nb, 2, nh, 1), F32)] * n_odd, p, nb, seq, nb * seq)
    x_lat = add_pos(x_sample.reshape(db * dseq, d), _pos_embed(dseq, d, GRID_W), dseq, _tile(dseq, ROW_TILE))
    y_sample, _, _ = _trunk(
        x_lat, mods_lat, s5_state(state_s5_re, state_s5_im, db),
        [state_mlstm_C[:, i].astype(F32) for i in range(n_odd)], [state_mlstm_n[:, i].astype(F32) for i in range(n_odd)],
        [state_mlstm_m[:, i].astype(F32)[..., None] for i in range(n_odd)], p, db, dseq, dseq)

    new_s5_re = jnp.stack([h[:, :, 0].reshape(nb, 2, ngrp, npst) for h in s5_fin], axis=1)
    new_s5_im = jnp.stack([h[:, :, 1].reshape(nb, 2, ngrp, npst) for h in s5_fin], axis=1)
    new_c = jnp.stack([f[0] for f in ml_fin], axis=1)
    new_n = jnp.stack([f[1] for f in ml_fin], axis=1)
    new_m = jnp.stack([f[2][..., 0] for f in ml_fin], axis=1)
    return (y_prompt.reshape(nb, seq, d), y_sample.reshape(db, dseq, d), new_s5_re, new_s5_im, new_c, new_n, new_m)
```
